```python
import jax, jax.numpy as jnp
from jax import lax
import numpy as np

D_MODEL = 1024
BATCH = 8
SEQ = 2048
DEPTH = 2

CONV_A_CH = 512
CONV_A_WIDTH = 31
CONV_A_LN_EPS = 1e-5
RWKV_HEADS = 8
RWKV_HEAD = 64
RWKV_DIM = RWKV_HEADS * RWKV_HEAD
LORA_W = 64
LORA_A = 64
LORA_G = 128
RWKV_GN_EPS = 64e-5
LRU_DIM = 1024
LRU_HEADS = 8
LRU_BLOCK = LRU_DIM // LRU_HEADS
LRU_CONV = 4
LRU_C = 8.0
N_EXPERTS = 64
TOP_K = 8
N_GROUPS = 8
TOPK_GROUPS = 4
EXPERT_FF = 256
SHARED_FF = 256
ROUTED_SCALE = 2.5
MOE_BLOCK = 128
NORM_EPS = 1e-6
N_BRANCH = 3
N_MOD = 6

COLS_A = 2 * CONV_A_CH
COLS_B = 3 * RWKV_DIM + LORA_W + LORA_A + LORA_G
COLS_C = 2 * LRU_DIM
COLS_G = N_BRANCH * D_MODEL
IN_COLS = COLS_A + COLS_B + COLS_C + COLS_G
MIX_SPLITS = (COLS_A, COLS_A + COLS_B, COLS_A + COLS_B + COLS_C)
RWKV_SPLITS = (RWKV_DIM, 2 * RWKV_DIM, 3 * RWKV_DIM, 3 * RWKV_DIM + LORA_W,
               3 * RWKV_DIM + LORA_W + LORA_A)

kernel_name = 'adaln_hybrid_conv_rwkv7_rglru_moe'


def rmsnorm(x, g):
    xf = x.astype(jnp.float32)
    y = xf * lax.rsqrt(jnp.mean(xf * xf, axis=-1, keepdims=True) + NORM_EPS)
    return (y * g.astype(jnp.float32)).astype(x.dtype)


def layernorm(x, g, b, eps):
    xf = x.astype(jnp.float32)
    mu = jnp.mean(xf, axis=-1, keepdims=True)
    var = jnp.mean(jnp.square(xf - mu), axis=-1, keepdims=True)
    y = (xf - mu) * lax.rsqrt(var + eps)
    return (y * g.astype(jnp.float32) + b.astype(jnp.float32)).astype(x.dtype)


def causal_dwconv(x, w, b):
    width, ch = w.shape
    y = lax.conv_general_dilated(
        x, w[:, None, :].astype(x.dtype), window_strides=(1,), padding=[(width - 1, 0)],
        dimension_numbers=('NWC', 'WIO', 'NWC'), feature_group_count=ch)
    return y + b


def token_shift(t):
    return jnp.pad(t, ((0, 0), (1, 0), (0, 0)))[:, :-1]


def conformer_conv(p, conv_w, conv_b, ln_g, ln_b, proj):
    u = p[..., :CONV_A_CH] * jax.nn.sigmoid(p[..., CONV_A_CH:])
    u = causal_dwconv(u, conv_w, conv_b)
    u = layernorm(u, ln_g, ln_b, CONV_A_LN_EPS)
    return jax.nn.silu(u) @ proj


def rwkv7_time_mix(p, mu, w0, w_up, a0, a_up, g_up, k_k, k_a, r_k, gn_g, gn_b, proj):
    bsz, seq, _ = p.shape
    p = p + (token_shift(p) - p) * mu
    r, k, v, xw, xa, xg = jnp.split(p, RWKV_SPLITS, axis=-1)
    w = -jax.nn.softplus(-(w0 + jnp.tanh(xw) @ w_up)) - 0.5
    decay = jnp.exp(-jnp.exp(w.astype(jnp.float32)))
    a = jax.nn.sigmoid(a0 + xa @ a_up)
    g = jax.nn.sigmoid(xg) @ g_up
    heads = lambda t: t.reshape(bsz, seq, RWKV_HEADS, RWKV_HEAD).astype(jnp.float32)
    kk = heads(k * k_k)
    kk = kk / jnp.maximum(jnp.sqrt(jnp.sum(kk * kk, axis=-1, keepdims=True)), 1e-12)
    k = k * (1.0 + (a - 1.0) * k_a)
    rh, wh, kh, vh, ah = heads(r), heads(decay), heads(k), heads(v), heads(a)
    seq_major = lambda t: jnp.transpose(t, (1, 0, 2, 3))
    xs = (seq_major(rh), seq_major(wh), seq_major(kh), seq_major(vh),
          seq_major(kk), seq_major(kk * ah))

    def step(state, inp):
        r_t, w_t, k_t, v_t, kk_t, b_t = inp
        sa = jnp.einsum('bhvk,bhk->bhv', state, kk_t)
        state = (state * w_t[:, :, None, :] - sa[..., None] * b_t[:, :, None, :]
                 + v_t[..., None] * k_t[:, :, None, :])
        y_t = jnp.einsum('bhvk,bhk->bhv', state, r_t)
        return state, y_t

    s0 = jnp.zeros((bsz, RWKV_HEADS, RWKV_HEAD, RWKV_HEAD), jnp.float32)
    _, y = lax.scan(step, s0, xs)
    y = jnp.transpose(y, (1, 0, 2, 3))
    mu_y = jnp.mean(y, axis=-1, keepdims=True)
    var_y = jnp.mean(jnp.square(y - mu_y), axis=-1, keepdims=True)
    y = ((y - mu_y) * lax.rsqrt(var_y + RWKV_GN_EPS)).reshape(bsz, seq, RWKV_DIM)
    y = y * gn_g.astype(jnp.float32) + gn_b.astype(jnp.float32)
    bonus = jnp.sum(rh * kh * r_k.astype(jnp.float32), axis=-1, keepdims=True) * vh
    y = y + bonus.reshape(bsz, seq, RWKV_DIM)
    return (y.astype(p.dtype) * g) @ proj


def rglru_branch(p, conv_w, conv_b, wa, ba, wx, bx, lam, proj):
    bsz, seq, _ = p.shape
    y_gate = jax.nn.gelu(p[..., :LRU_DIM], approximate=True)
    xc = causal_dwconv(p[..., LRU_DIM:], conv_w, conv_b)
    xb = xc.reshape(bsz, seq, LRU_HEADS, LRU_BLOCK)
    gate_a = jax.nn.sigmoid(jnp.einsum('bshi,hij->bshj', xb, wa).reshape(bsz, seq, LRU_DIM) + ba)
    gate_x = jax.nn.sigmoid(jnp.einsum('bshi,hij->bshj', xb, wx).reshape(bsz, seq, LRU_DIM) + bx)
    log_a = -LRU_C * gate_a.astype(jnp.float32) * jax.nn.softplus(-lam.astype(jnp.float32))
    a = jnp.exp(log_a)
    mult = jnp.sqrt(-jnp.expm1(2.0 * log_a))
    b = xc.astype(jnp.float32) * gate_x.astype(jnp.float32) * mult

    def combine(left, right):
        a_l, b_l = left
        a_r, b_r = right
        return a_l * a_r, a_r * b_l + b_r

    _, h = lax.associative_scan(combine, (a, b), axis=1)
    return (h.astype(p.dtype) * y_gate) @ proj


def moe_ffn(h, router_w, router_bias, w1, w3, w2, sw1, sw3, sw2):
    bsz, seq, dm = h.shape
    n_tok = bsz * seq
    xt = h.reshape(n_tok, dm)
    scores = jax.nn.sigmoid(xt.astype(jnp.float32) @ router_w.astype(jnp.float32))
    biased = scores + router_bias.astype(jnp.float32)
    per_group = N_EXPERTS // N_GROUPS
    grp_score = jnp.sum(lax.top_k(biased.reshape(n_tok, N_GROUPS, per_group), 2)[0], axis=-1)
    _, grp_idx = lax.top_k(grp_score, TOPK_GROUPS)
    grp_mask = jnp.any(grp_idx[..., None] == jnp.arange(N_GROUPS), axis=1)
    masked = jnp.where(jnp.repeat(grp_mask, per_group, axis=1), biased, -jnp.inf)
    _, top_idx = lax.top_k(masked, TOP_K)
    top_w = jnp.take_along_axis(scores, top_idx, axis=1)
    top_w = top_w / jnp.sum(top_w, axis=-1, keepdims=True) * ROUTED_SCALE

    nk = n_tok * TOP_K
    e_flat = top_idx.reshape(nk).astype(jnp.int32)
    tok_flat = jnp.arange(nk, dtype=jnp.int32) // TOP_K
    w_flat = top_w.reshape(nk)
    order = jnp.argsort(e_flat)
    e_sorted = e_flat[order]
    counts = jnp.zeros((N_EXPERTS,), jnp.int32).at[e_flat].add(1)
    starts = jnp.cumsum(counts) - counts
    padded = (counts + MOE_BLOCK - 1) // MOE_BLOCK * MOE_BLOCK
    pad_end = jnp.cumsum(padded)
    pad_start = pad_end - padded
    dest = pad_start[e_sorted] + jnp.arange(nk, dtype=jnp.int32) - starts[e_sorted]
    n_blocks = -(-(nk + N_EXPERTS * (MOE_BLOCK - 1)) // MOE_BLOCK)
    n_rows = n_blocks * MOE_BLOCK
    row_tok = jnp.full((n_rows,), n_tok, jnp.int32).at[dest].set(tok_flat[order])
    row_w = jnp.zeros((n_rows,), jnp.float32).at[dest].set(w_flat[order])
    block_exp = jnp.minimum(
        jnp.searchsorted(pad_end, jnp.arange(n_blocks, dtype=jnp.int32) * MOE_BLOCK, side='right'),
        N_EXPERTS - 1)
    x_pad = jnp.concatenate([xt, jnp.zeros((1, dm), xt.dtype)], axis=0)

    def block_step(acc, inp):
        tok, wt, e = inp
        xb = x_pad[tok]
        hb = jax.nn.silu(xb @ w1[e]) * (xb @ w3[e])
        yb = (hb @ w2[e]).astype(jnp.float32) * wt[:, None]
        return acc.at[tok].add(yb), None

    acc0 = jnp.zeros((n_tok + 1, dm), jnp.float32)
    acc, _ = lax.scan(block_step, acc0, (row_tok.reshape(n_blocks, MOE_BLOCK),
                                         row_w.reshape(n_blocks, MOE_BLOCK), block_exp))
    routed = acc[:n_tok].astype(xt.dtype)
    shared = (jax.nn.silu(xt @ sw1) * (xt @ sw3)) @ sw2
    return (routed + shared).reshape(bsz, seq, dm)


def setup_inputs(seed: int = 0) -> dict:
    key = jax.random.key(seed)
    ks = iter(jax.random.split(key, 64))
    nrm = lambda shape, scale: jax.random.normal(next(ks), shape, jnp.float32) * scale
    L, D = DEPTH, D_MODEL
    u = jax.random.uniform(next(ks), (L, LRU_DIM), jnp.float32, 0.9, 0.999)
    s = u ** (1.0 / LRU_C)
    return {
        'x': nrm((BATCH, SEQ, D), 1.0),
        'c': nrm((BATCH, D), 1.0),
        'ada_w': nrm((L, D, N_MOD * D), 0.3 * D ** -0.5),
        'ada_b': nrm((L, N_MOD * D), 0.02),
        'norm1': 1.0 + nrm((L, D), 0.02),
        'norm2': 1.0 + nrm((L, D), 0.02),
        'w_in': nrm((L, D, IN_COLS), D ** -0.5),
        'conv_a_w': nrm((L, CONV_A_WIDTH, CONV_A_CH), CONV_A_WIDTH ** -0.5),
        'conv_a_b': nrm((L, CONV_A_CH), 0.02),
        'ln_a_g': 1.0 + nrm((L, CONV_A_CH), 0.02),
        'ln_a_b': nrm((L, CONV_A_CH), 0.02),
        'proj_a': nrm((L, CONV_A_CH, D), CONV_A_CH ** -0.5),
        'mu_b': jax.random.uniform(next(ks), (L, COLS_B), jnp.float32),
        'w0': jax.random.uniform(next(ks), (L, RWKV_DIM), jnp.float32, -6.0, -1.0),
        'w_up': nrm((L, LORA_W, RWKV_DIM), 0.5 * LORA_W ** -0.5),
        'a0': nrm((L, RWKV_DIM), 0.1),
        'a_up': nrm((L, LORA_A, RWKV_DIM), 0.5 * LORA_A ** -0.5),
        'g_up': nrm((L, LORA_G, RWKV_DIM), LORA_G ** -0.5),
        'k_k': 0.85 + nrm((L, RWKV_DIM), 0.05),
        'k_a': 1.0 + nrm((L, RWKV_DIM), 0.05),
        'r_k': nrm((L, RWKV_HEADS, RWKV_HEAD), 0.1),
        'gn_b_g': 1.0 + nrm((L, RWKV_DIM), 0.02),
        'gn_b_b': nrm((L, RWKV_DIM), 0.02),
        'proj_b': nrm((L, RWKV_DIM, D), RWKV_DIM ** -0.5),
        'conv_c_w': nrm((L, LRU_CONV, LRU_DIM), LRU_CONV ** -0.5),
        'conv_c_b': nrm((L, LRU_DIM), 0.02),
        'lru_wa': nrm((L, LRU_HEADS, LRU_BLOCK, LRU_BLOCK), LRU_BLOCK ** -0.5),
        'lru_ba': nrm((L, LRU_DIM), 0.02),
        'lru_wx': nrm((L, LRU_HEADS, LRU_BLOCK, LRU_BLOCK), LRU_BLOCK ** -0.5),
        'lru_bx': nrm((L, LRU_DIM), 0.02),
        'lru_lambda': jnp.log(s) - jnp.log1p(-s),
        'proj_c': nrm((L, LRU_DIM, D), LRU_DIM ** -0.5),
        'w_out': nrm((L, D, D), D ** -0.5),
        'router_w': nrm((L, D, N_EXPERTS), D ** -0.5),
        'router_bias': nrm((L, N_EXPERTS), 0.01),
        'exp_w1': nrm((L, N_EXPERTS, D, EXPERT_FF), D ** -0.5),
        'exp_w3': nrm((L, N_EXPERTS, D, EXPERT_FF), D ** -0.5),
        'exp_w2': nrm((L, N_EXPERTS, EXPERT_FF, D), EXPERT_FF ** -0.5),
        'sh_w1': nrm((L, D, SHARED_FF), D ** -0.5),
        'sh_w3': nrm((L, D, SHARED_FF), D ** -0.5),
        'sh_w2': nrm((L, SHARED_FF, D), SHARED_FF ** -0.5),
        'final_norm': 1.0 + nrm((D,), 0.02),
    }


def reference(x, c, ada_w, ada_b, norm1, norm2, w_in, conv_a_w, conv_a_b, ln_a_g, ln_a_b,
              proj_a, mu_b, w0, w_up, a0, a_up, g_up, k_k, k_a, r_k, gn_b_g, gn_b_b, proj_b,
              conv_c_w, conv_c_b, lru_wa, lru_ba, lru_wx, lru_bx, lru_lambda, proj_c, w_out,
              router_w, router_bias, exp_w1, exp_w3, exp_w2, sh_w1, sh_w3, sh_w2, final_norm):
    cond = jax.nn.silu(c)
    for l in range(DEPTH):
        mod = (cond @ ada_w[l] + ada_b[l])[:, None, :]
        sh1, sc1, g1, sh2, sc2, g2 = jnp.split(mod, N_MOD, axis=-1)
        h = rmsnorm(x, norm1[l]) * (1.0 + sc1) + sh1
        p = h @ w_in[l]
        pa, pb, pc, pg = jnp.split(p, MIX_SPLITS, axis=-1)
        oa = conformer_conv(pa, conv_a_w[l], conv_a_b[l], ln_a_g[l], ln_a_b[l], proj_a[l])
        ob = rwkv7_time_mix(pb, mu_b[l], w0[l], w_up[l], a0[l], a_up[l], g_up[l], k_k[l],
                            k_a[l], r_k[l], gn_b_g[l], gn_b_b[l], proj_b[l])
        oc = rglru_branch(pc, conv_c_w[l], conv_c_b[l], lru_wa[l], lru_ba[l], lru_wx[l],
                          lru_bx[l], lru_lambda[l], proj_c[l])
        gate = jax.nn.sigmoid(pg).reshape(pg.shape[:-1] + (N_BRANCH, D_MODEL))
        merged = gate[..., 0, :] * oa + gate[..., 1, :] * ob + gate[..., 2, :] * oc
        x = x + g1 * (merged @ w_out[l])
        h = rmsnorm(x, norm2[l]) * (1.0 + sc2) + sh2
        x = x + g2 * moe_ffn(h, router_w[l], router_bias[l], exp_w1[l], exp_w3[l], exp_w2[l],
                             sh_w1[l], sh_w3[l], sh_w2[l])
    return rmsnorm(x, final_norm)
```

```python
import functools

import jax
import jax.numpy as jnp
from jax import lax
from jax.experimental import pallas as pl
from jax.experimental.pallas import tpu as pltpu

F32 = jnp.float32
BF16 = jnp.bfloat16
I32 = jnp.int32
SDS = jax.ShapeDtypeStruct
HIGHEST = lax.Precision.HIGHEST

D_MODEL = 1024
N_MOD = 6
NORM_EPS = 1e-6
CONV_A_CH = 512
CONV_A_WIDTH = 31
CONV_A_LN_EPS = 1e-5
RWKV_HEADS = 8
RWKV_HEAD = 64
RWKV_DIM = RWKV_HEADS * RWKV_HEAD
LORA_W = 64
LORA_A = 64
LORA_G = 128
RWKV_GN_EPS = 64e-5
RWKV_CHUNK = 64
LRU_DIM = 1024
LRU_HEADS = 8
LRU_BLOCK = LRU_DIM // LRU_HEADS
LRU_CONV = 4
LRU_C = 8.0
N_EXPERTS = 64
TOP_K = 8
N_GROUPS = 8
GROUP_SIZE = N_EXPERTS // N_GROUPS
TOPK_GROUPS = 4
EXPERT_FF = 256
SHARED_FF = 256
ROUTED_SCALE = 2.5
SEG_C = 0
SEG_B = 2048
SEG_G = 4096
SEG_A = 7168
IN_COLS_PAD = 8192
RW_R, RW_K, RW_V, RW_XW, RW_XA, RW_XG = 0, 512, 1024, 1536, 1664, 1792
COLS_A = 2 * CONV_A_CH
COLS_B = 3 * RWKV_DIM + LORA_W + LORA_A + LORA_G
COLS_C = 2 * LRU_DIM
VMEM_LIMIT = 56 * 1024 * 1024
SUBLANES = 8
LANES = 128
MOE_ROWS = 256
ROUTE_TILE = 512
DISPATCH_TILE = 256
COMBINE_TILE = 128


def _cparams(sem):
    return pltpu.CompilerParams(dimension_semantics=sem, vmem_limit_bytes=VMEM_LIMIT)


def _bdot(a, b):
    return jnp.dot(a.astype(BF16), b.astype(BF16), preferred_element_type=F32)


def _hdot(a, b):
    return jnp.dot(a, b, preferred_element_type=F32, precision=HIGHEST)


def _hdot_nt(a, b):
    return lax.dot_general(a, b, (((1,), (1,)), ((), ())), preferred_element_type=F32,
                           precision=HIGHEST)


def _hdot_tn(a, b):
    return lax.dot_general(a, b, (((0,), (0,)), ((), ())), preferred_element_type=F32,
                           precision=HIGHEST)


def _ada_kernel(c_ref, w_ref, b_ref, o_ref):
    cond = jax.nn.silu(c_ref[...])
    o_ref[0] = _bdot(cond, w_ref[0]) + b_ref[0]


def _ada_mod(c, ada_w, ada_b):
    depth, _, n = ada_w.shape
    bsz = c.shape[0]
    tn = 1536
    return pl.pallas_call(
        _ada_kernel,
        out_shape=SDS((depth, bsz, n), F32),
        grid=(depth, n // tn),
        in_specs=[pl.BlockSpec((bsz, D_MODEL), lambda l, j: (0, 0)),
                  pl.BlockSpec((1, D_MODEL, tn), lambda l, j: (l, 0, j)),
                  pl.BlockSpec((1, 1, tn), lambda l, j: (l, 0, j))],
        out_specs=pl.BlockSpec((1, bsz, tn), lambda l, j: (l, 0, j)),
        compiler_params=_cparams(("arbitrary", "arbitrary")),
        name="ada_mod",
    )(c, ada_w, ada_b.reshape(depth, 1, n))


def _modulated_rmsnorm(x, g, sc, sh):
    y = x * lax.rsqrt(jnp.mean(x * x, axis=-1, keepdims=True) + NORM_EPS)
    return (y * g) * (1.0 + sc) + sh


def _in_kernel(x_ref, sc_ref, sh_ref, g_ref, w_ref, o_ref, h_ref):
    @pl.when(pl.program_id(2) == 0)
    def _():
        h_ref[...] = _modulated_rmsnorm(x_ref[0], g_ref[...], sc_ref[0], sh_ref[0]).astype(BF16)

    o_ref[0] = jnp.dot(h_ref[...], w_ref[...], preferred_element_type=F32).astype(o_ref.dtype)


def _in_proj(x, sc, sh, g, w_pad):
    bsz, seq, _ = x.shape
    ts = min(seq, 1024)
    tn = 1024
    return pl.pallas_call(
        _in_kernel,
        out_shape=SDS((bsz, seq, IN_COLS_PAD), BF16),
        grid=(bsz, seq // ts, IN_COLS_PAD // tn),
        in_specs=[pl.BlockSpec((1, ts, D_MODEL), lambda b, i, j: (b, i, 0)),
                  pl.BlockSpec((1, 1, D_MODEL), lambda b, i, j: (b, 0, 0)),
                  pl.BlockSpec((1, 1, D_MODEL), lambda b, i, j: (b, 0, 0)),
                  pl.BlockSpec((1, D_MODEL), lambda b, i, j: (0, 0)),
                  pl.BlockSpec((D_MODEL, tn), lambda b, i, j: (0, j))],
        out_specs=pl.BlockSpec((1, ts, tn), lambda b, i, j: (b, i, j)),
        scratch_shapes=[pltpu.VMEM((ts, D_MODEL), BF16)],
        compiler_params=_cparams(("arbitrary", "arbitrary", "arbitrary")),
        name="in_proj",
    )(x, sc, sh, g, w_pad)


CONV_A_HALO = 32
CONV_A_SUB = 64


def _conv_a_kernel(pa_ref, pg_ref, cw_ref, cb_ref, lg_ref, lb_ref, pj_ref, o_ref, ext_ref, y_ref):
    ts = pa_ref.shape[1]

    @pl.when(pl.program_id(1) == 0)
    def _():
        ext_ref[pl.ds(0, CONV_A_HALO), :] = jnp.zeros((CONV_A_HALO, CONV_A_CH), F32)

    pa = pa_ref[0].astype(F32)
    ext_ref[pl.ds(CONV_A_HALO, ts), :] = pa[:, :CONV_A_CH] * jax.nn.sigmoid(pa[:, CONV_A_CH:])
    first = CONV_A_HALO - (CONV_A_WIDTH - 1)
    for r0 in range(0, ts, CONV_A_SUB):
        acc = jnp.zeros((CONV_A_SUB, CONV_A_CH), F32) + cb_ref[...]
        for j in range(CONV_A_WIDTH):
            acc = acc + ext_ref[pl.ds(r0 + first + j, CONV_A_SUB), :] * cw_ref[pl.ds(j, 1), :]
        y_ref[pl.ds(r0, CONV_A_SUB), :] = acc
    ext_ref[pl.ds(0, CONV_A_HALO), :] = ext_ref[pl.ds(ts, CONV_A_HALO), :]
    y = y_ref[...]
    mu = jnp.mean(y, axis=-1, keepdims=True)
    d = y - mu
    var = jnp.mean(d * d, axis=-1, keepdims=True)
    yn = d * lax.rsqrt(var + CONV_A_LN_EPS) * lg_ref[...] + lb_ref[...]
    o = _bdot(jax.nn.silu(yn), pj_ref[...])
    o_ref[0] = jax.nn.sigmoid(pg_ref[0].astype(F32)) * o


def _conv_a(p, conv_w, conv_b, ln_g, ln_b, proj_bf):
    bsz, seq, _ = p.shape
    ts = min(seq, 512)
    row = lambda a: a.reshape(1, -1)
    full = lambda shape: pl.BlockSpec(shape, lambda b, i: (0,) * len(shape))
    return pl.pallas_call(
        _conv_a_kernel,
        out_shape=SDS((bsz, seq, D_MODEL), F32),
        grid=(bsz, seq // ts),
        in_specs=[pl.BlockSpec((1, ts, 2 * CONV_A_CH), lambda b, i: (b, i, SEG_A // (2 * CONV_A_CH))),
                  pl.BlockSpec((1, ts, D_MODEL), lambda b, i: (b, i, SEG_G // D_MODEL)),
                  full((CONV_A_WIDTH, CONV_A_CH)), full((1, CONV_A_CH)), full((1, CONV_A_CH)),
                  full((1, CONV_A_CH)), full((CONV_A_CH, D_MODEL))],
        out_specs=pl.BlockSpec((1, ts, D_MODEL), lambda b, i: (b, i, 0)),
        scratch_shapes=[pltpu.VMEM((ts + CONV_A_HALO, CONV_A_CH), F32),
                        pltpu.VMEM((ts, CONV_A_CH), F32)],
        compiler_params=_cparams(("arbitrary", "arbitrary")),
        name="conv_a",
    )(p, p, conv_w, row(conv_b), row(ln_g), row(ln_b), proj_bf)


def _lru_kernel(pc_ref, pg_ref, m_ref, cw_ref, cb_ref, wa_ref, ba_ref, wx_ref, bx_ref, lam_ref,
                pj_ref, o_ref, ext_ref, h_ref, a_ref, b_ref):
    ts = pc_ref.shape[1]
    groups = ts // SUBLANES

    @pl.when(pl.program_id(1) == 0)
    def _():
        ext_ref[pl.ds(0, SUBLANES), :] = jnp.zeros((SUBLANES, LRU_DIM), F32)
        h_ref[...] = jnp.zeros((SUBLANES, LRU_DIM), F32)

    pc = pc_ref[0].astype(F32)
    y_gate = jax.nn.gelu(pc[:, :LRU_DIM], approximate=True)
    ext_ref[pl.ds(SUBLANES, ts), :] = pc[:, LRU_DIM:]
    first = SUBLANES - (LRU_CONV - 1)
    xc = jnp.zeros((ts, LRU_DIM), F32) + cb_ref[...]
    for j in range(LRU_CONV):
        xc = xc + ext_ref[pl.ds(first + j, ts), :] * cw_ref[pl.ds(j, 1), :]
    ext_ref[pl.ds(0, SUBLANES), :] = ext_ref[pl.ds(ts, SUBLANES), :]

    def block_diag(w_ref):
        return jnp.concatenate(
            [_bdot(xc[:, h * LRU_BLOCK:(h + 1) * LRU_BLOCK], w_ref[h]) for h in range(LRU_HEADS)],
            axis=1)

    gate_a = jax.nn.sigmoid(block_diag(wa_ref) + ba_ref[...])
    gate_x = jax.nn.sigmoid(block_diag(wx_ref) + bx_ref[...])
    log_a = -LRU_C * gate_a * jax.nn.softplus(-lam_ref[...])
    a = jnp.exp(log_a)
    b = xc * gate_x * jnp.sqrt(1.0 - jnp.exp(2.0 * log_a))

    a3 = a.reshape(groups, SUBLANES, LRU_DIM)
    b3 = b.reshape(groups, SUBLANES, LRU_DIM)
    row = lax.broadcasted_iota(I32, (groups, SUBLANES, LRU_DIM), 1)
    for s in (1, 2, 4):
        keep = row >= s
        b3 = jnp.where(keep, a3 * pltpu.roll(b3, s, axis=1) + b3, b3)
        a3 = jnp.where(keep, a3 * pltpu.roll(a3, s, axis=1), a3)
    a_ref[...] = a3.reshape(ts, LRU_DIM)
    b_ref[...] = b3.reshape(ts, LRU_DIM)
    h = h_ref[...]
    for g in range(groups):
        rows = pl.ds(g * SUBLANES, SUBLANES)
        hg = a_ref[rows, :] * h + b_ref[rows, :]
        b_ref[rows, :] = hg
        h = jnp.broadcast_to(hg[SUBLANES - 1:SUBLANES, :], (SUBLANES, LRU_DIM))
    h_ref[...] = h
    o = _bdot(b_ref[...] * y_gate, pj_ref[...])
    o_ref[0] = m_ref[0] + jax.nn.sigmoid(pg_ref[0].astype(F32)) * o


def _lru(p, merged, conv_w, conv_b, wa_bf, ba, wx_bf, bx, lam, proj_bf):
    bsz, seq, _ = p.shape
    ts = min(seq, 256)
    row = lambda a: a.reshape(1, -1)
    full = lambda shape: pl.BlockSpec(shape, lambda b, i: (0,) * len(shape))
    return pl.pallas_call(
        _lru_kernel,
        out_shape=SDS((bsz, seq, D_MODEL), F32),
        grid=(bsz, seq // ts),
        in_specs=[pl.BlockSpec((1, ts, 2 * LRU_DIM), lambda b, i: (b, i, SEG_C // (2 * LRU_DIM))),
                  pl.BlockSpec((1, ts, D_MODEL), lambda b, i: (b, i, SEG_G // D_MODEL + 2)),
                  pl.BlockSpec((1, ts, D_MODEL), lambda b, i: (b, i, 0)),
                  full((LRU_CONV, LRU_DIM)), full((1, LRU_DIM)),
                  full((LRU_HEADS, LRU_BLOCK, LRU_BLOCK)), full((1, LRU_DIM)),
                  full((LRU_HEADS, LRU_BLOCK, LRU_BLOCK)), full((1, LRU_DIM)),
                  full((1, LRU_DIM)), full((LRU_DIM, D_MODEL))],
        out_specs=pl.BlockSpec((1, ts, D_MODEL), lambda b, i: (b, i, 0)),
        scratch_shapes=[pltpu.VMEM((ts + SUBLANES, LRU_DIM), F32),
                        pltpu.VMEM((SUBLANES, LRU_DIM), F32),
                        pltpu.VMEM((ts, LRU_DIM), F32),
                        pltpu.VMEM((ts, LRU_DIM), F32)],
        compiler_params=_cparams(("arbitrary", "arbitrary")),
        name="rg_lru",
    )(p, p, merged, conv_w, row(conv_b), wa_bf, row(ba), wx_bf, row(bx), row(lam), proj_bf)


def _rwkv_prep_kernel(pb_ref, mu_ref, w0_ref, wup_ref, a0_ref, aup_ref, gup_ref, kk_ref, ka_ref,
                      rk_ref, bd_ref, ltri_ref,
                      rt_ref, kkt_ref, kh_ref, bh_ref, v_ref, pinc_ref, bonus_ref, g_ref, ext_ref):
    ts = pb_ref.shape[1]

    @pl.when(pl.program_id(1) == 0)
    def _():
        ext_ref[pl.ds(0, SUBLANES), :] = jnp.zeros((SUBLANES, ext_ref.shape[1]), F32)

    p = pb_ref[0].astype(F32)
    ext_ref[pl.ds(SUBLANES, ts), :] = p
    prev = ext_ref[pl.ds(SUBLANES - 1, ts), :]
    ext_ref[pl.ds(0, SUBLANES), :] = ext_ref[pl.ds(ts, SUBLANES), :]
    pm = p + (prev - p) * mu_ref[...]
    r = pm[:, RW_R:RW_R + RWKV_DIM]
    k = pm[:, RW_K:RW_K + RWKV_DIM]
    v = pm[:, RW_V:RW_V + RWKV_DIM]
    xw = pm[:, RW_XW:RW_XW + LANES]
    xa = pm[:, RW_XA:RW_XA + LANES]
    xg = pm[:, RW_XG:RW_XG + LORA_G]
    w = -jax.nn.softplus(-(w0_ref[...] + _bdot(jnp.tanh(xw), wup_ref[...]))) - 0.5
    lw = -jnp.exp(w)
    a = jax.nn.sigmoid(a0_ref[...] + _bdot(xa, aup_ref[...]))
    g_ref[0] = _bdot(jax.nn.sigmoid(xg), gup_ref[...])
    kkr = k * kk_ref[...]
    ss = _hdot(kkr * kkr, bd_ref[...])
    kk = kkr / jnp.maximum(jnp.sqrt(ss), 1e-12)
    k2 = k * (1.0 + (a - 1.0) * ka_ref[...])
    lcum = _hdot(ltri_ref[...], lw)
    pinc = jnp.exp(lcum)
    pinv = jnp.exp(-lcum)
    rt_ref[0] = r * pinc
    kkt_ref[0] = kk * jnp.exp(lcum - lw)
    kh_ref[0] = k2 * pinv
    bh_ref[0] = kk * a * pinv
    v_ref[0] = v
    pinc_ref[0] = pinc
    bonus_ref[0] = _hdot(r * k2 * rk_ref[...], bd_ref[...]) * v


def _rwkv_scan_kernel(rt_ref, kkt_ref, kh_ref, bh_ref, v_ref, pinc_ref, y_ref, s_ref):
    c = RWKV_CHUNK

    @pl.when(pl.program_id(1) == 0)
    def _():
        s_ref[...] = jnp.zeros(s_ref.shape, F32)

    row = lax.broadcasted_iota(I32, (c, c), 0)
    col = lax.broadcasted_iota(I32, (c, c), 1)
    strict = row > col
    incl = row >= col
    eye = (row == col).astype(F32)
    same16 = (row // 16) == (col // 16)
    same32 = (row // 32) == (col // 32)
    for h in range(RWKV_HEADS):
        sl = pl.ds(h * RWKV_HEAD, RWKV_HEAD)
        rt, kkt, kh, bh, v = rt_ref[0, :, sl], kkt_ref[0, :, sl], kh_ref[0, :, sl], bh_ref[0, :, sl], v_ref[0, :, sl]
        pc = pinc_ref[0, pl.ds(c - 1, 1), sl]
        a_k = jnp.where(strict, _hdot_nt(kkt, kh), 0.0)
        a_b = jnp.where(strict, _hdot_nt(kkt, bh), 0.0)
        q_k = jnp.where(incl, _hdot_nt(rt, kh), 0.0)
        q_b = jnp.where(incl, _hdot_nt(rt, bh), 0.0)
        d16 = jnp.where(same16, a_b, 0.0)
        d2 = _hdot(d16, d16)
        d4 = _hdot(d2, d2)
        d8 = _hdot(d4, d4)
        t = eye - d16
        t = t + _hdot(t, d2)
        t = t + _hdot(t, d4)
        t = t + _hdot(t, d8)
        l32 = jnp.where(same32 & jnp.logical_not(same16), a_b, 0.0)
        t = t - _hdot(t, _hdot(l32, t))
        l64 = jnp.where(same32, 0.0, a_b)
        t = t - _hdot(t, _hdot(l64, t))
        w_mat = _hdot(t, kkt)
        u0 = _hdot(t, _hdot(a_k, v))
        s = s_ref[h]
        u = _hdot_nt(w_mat, s) + u0
        y_ref[0, :, sl] = _hdot_nt(rt, s) + _hdot(q_k, v) - _hdot(q_b, u)
        s_ref[h] = s * pc + _hdot_tn(v, kh * pc) - _hdot_tn(u, bh * pc)


def _rwkv_post_kernel(y_ref, bonus_ref, g_ref, pg_ref, m_ref, gg_ref, gb_ref, bdm_ref, pj_ref, o_ref):
    y = y_ref[0]
    mu = _hdot(y, bdm_ref[...])
    d = y - mu
    var = _hdot(d * d, bdm_ref[...])
    yn = d * lax.rsqrt(var + RWKV_GN_EPS) * gg_ref[...] + gb_ref[...] + bonus_ref[0]
    o = _bdot(yn * g_ref[0], pj_ref[...])
    o_ref[0] = m_ref[0] + jax.nn.sigmoid(pg_ref[0].astype(F32)) * o


def _rwkv(p, merged, mu_pad, w0, wup_pad, a0, aup_pad, g_up, k_k, k_a, r_k, gn_g, gn_b, proj_bf):
    bsz, seq, _ = p.shape
    row = lambda a: a.reshape(1, -1)
    full = lambda shape: pl.BlockSpec(shape, lambda b, i: (0,) * len(shape))
    head_id = jnp.arange(RWKV_DIM, dtype=I32) // RWKV_HEAD
    bd = (head_id[:, None] == head_id[None, :]).astype(F32)

    ts = min(seq, 256)
    t_id = jnp.arange(ts, dtype=I32)
    ltri = ((t_id[:, None] // RWKV_CHUNK == t_id[None, :] // RWKV_CHUNK)
            & (t_id[:, None] >= t_id[None, :])).astype(F32)
    seq_blk = lambda width: pl.BlockSpec((1, ts, width), lambda b, i: (b, i, 0))
    wide = SDS((bsz, seq, RWKV_DIM), F32)
    rt, kkt, kh, bh, v, pinc, bonus, g = pl.pallas_call(
        _rwkv_prep_kernel,
        out_shape=[wide] * 8,
        grid=(bsz, seq // ts),
        in_specs=[pl.BlockSpec((1, ts, 2048), lambda b, i: (b, i, SEG_B // 2048)),
                  full((1, 2048)), full((1, RWKV_DIM)), full((LANES, RWKV_DIM)),
                  full((1, RWKV_DIM)), full((LANES, RWKV_DIM)), full((LORA_G, RWKV_DIM)),
                  full((1, RWKV_DIM)), full((1, RWKV_DIM)), full((1, RWKV_DIM)),
                  full((RWKV_DIM, RWKV_DIM)), full((ts, ts))],
        out_specs=[seq_blk(RWKV_DIM)] * 8,
        scratch_shapes=[pltpu.VMEM((ts + SUBLANES, 2048), F32)],
        compiler_params=_cparams(("arbitrary", "arbitrary")),
        name="rwkv_prep",
    )(p, mu_pad, row(w0), wup_pad, row(a0), aup_pad, g_up, row(k_k), row(k_a), row(r_k), bd, ltri)

    c = RWKV_CHUNK
    chunk_blk = pl.BlockSpec((1, c, RWKV_DIM), lambda b, i: (b, i, 0))
    y = pl.pallas_call(
        _rwkv_scan_kernel,
        out_shape=wide,
        grid=(bsz, seq // c),
        in_specs=[chunk_blk] * 6,
        out_specs=chunk_blk,
        scratch_shapes=[pltpu.VMEM((RWKV_HEADS, RWKV_HEAD, RWKV_HEAD), F32)],
        compiler_params=_cparams(("arbitrary", "arbitrary")),
        name="rwkv_scan",
    )(rt, kkt, kh, bh, v, pinc)

    tp = min(seq, 512)
    blk = lambda width: pl.BlockSpec((1, tp, width), lambda b, i: (b, i, 0))
    return pl.pallas_call(
        _rwkv_post_kernel,
        out_shape=SDS((bsz, seq, D_MODEL), F32),
        grid=(bsz, seq // tp),
        in_specs=[blk(RWKV_DIM), blk(RWKV_DIM), blk(RWKV_DIM),
                  pl.BlockSpec((1, tp, D_MODEL), lambda b, i: (b, i, SEG_G // D_MODEL + 1)),
                  blk(D_MODEL), full((1, RWKV_DIM)), full((1, RWKV_DIM)),
                  full((RWKV_DIM, RWKV_DIM)), full((RWKV_DIM, D_MODEL))],
        out_specs=blk(D_MODEL),
        compiler_params=_cparams(("arbitrary", "arbitrary")),
        name="rwkv_post",
    )(y, bonus, g, p, merged, row(gn_g), row(gn_b), bd * (1.0 / RWKV_HEAD), proj_bf)


def _out_kernel(m_ref, x_ref, g1_ref, w_ref, n2_ref, sc_ref, sh_ref, xo_ref, h_ref):
    xn = x_ref[0] + g1_ref[0] * _bdot(m_ref[0], w_ref[...])
    xo_ref[0] = xn
    h_ref[0] = _modulated_rmsnorm(xn, n2_ref[...], sc_ref[0], sh_ref[0])


def _out_proj(merged, x, g1, w_bf, norm2, sc2, sh2):
    bsz, seq, _ = x.shape
    ts = min(seq, 512)
    blk = pl.BlockSpec((1, ts, D_MODEL), lambda b, i: (b, i, 0))
    per_b = pl.BlockSpec((1, 1, D_MODEL), lambda b, i: (b, 0, 0))
    return pl.pallas_call(
        _out_kernel,
        out_shape=[SDS((bsz, seq, D_MODEL), F32)] * 2,
        grid=(bsz, seq // ts),
        in_specs=[blk, blk, per_b, pl.BlockSpec((D_MODEL, D_MODEL), lambda b, i: (0, 0)),
                  pl.BlockSpec((1, D_MODEL), lambda b, i: (0, 0)), per_b, per_b],
        out_specs=[blk, blk],
        compiler_params=_cparams(("arbitrary", "arbitrary")),
        name="out_proj",
    )(merged, x, g1, w_bf, norm2, sc2, sh2)


def _route_kernel(h_ref, rw_ref, bias_ref, upper_ref, e_ref, w_ref, rank_ref, cnt_ref, carry_ref):
    tile = h_ref.shape[0]

    @pl.when(pl.program_id(0) == 0)
    def _():
        carry_ref[...] = jnp.zeros(carry_ref.shape, F32)

    neg = -jnp.inf
    scores = jax.nn.sigmoid(_hdot_nt(rw_ref[...], h_ref[...]))
    s3 = scores.reshape(GROUP_SIZE, N_GROUPS, tile)
    b3 = s3 + bias_ref[...].reshape(GROUP_SIZE, N_GROUPS, tile)
    slab = lax.broadcasted_iota(I32, b3.shape, 0).astype(F32)
    grp = lax.broadcasted_iota(I32, b3.shape, 1).astype(F32)
    eid = grp * GROUP_SIZE + slab
    m1 = jnp.max(b3, axis=0, keepdims=True)
    first = jnp.min(jnp.where(b3 == m1, slab, GROUP_SIZE), axis=0, keepdims=True)
    m2 = jnp.max(jnp.where(slab == first, neg, b3), axis=0, keepdims=True)
    gs = (m1 + m2)[0]
    gi = lax.broadcasted_iota(I32, gs.shape, 0).astype(F32)
    chosen = jnp.zeros(gs.shape, F32)
    for _ in range(TOPK_GROUPS):
        m = jnp.max(gs, axis=0, keepdims=True)
        hit = gi == jnp.min(jnp.where(gs == m, gi, N_GROUPS), axis=0, keepdims=True)
        chosen = jnp.where(hit, 1.0, chosen)
        gs = jnp.where(hit, neg, gs)
    cur = jnp.where((chosen > 0.0)[None], b3, neg)
    ones_cols = jnp.ones((tile, LANES), BF16)
    base = carry_ref[...]
    idx_rows, w_rows, rank_rows = [], [], []
    for _ in range(TOP_K):
        m = jnp.max(jnp.max(cur, axis=0), axis=0, keepdims=True)[None]
        pick = jnp.min(jnp.min(jnp.where(cur == m, eid, N_EXPERTS), axis=0), axis=0, keepdims=True)
        hit = eid == pick[None]
        w_rows.append(jnp.sum(jnp.sum(jnp.where(hit, s3, 0.0), axis=0), axis=0, keepdims=True))
        idx_rows.append(pick)
        cur = jnp.where(hit, neg, cur)
        onehot = hit.astype(BF16).reshape(N_EXPERTS, tile)
        before = jnp.dot(onehot, upper_ref[...], preferred_element_type=F32)
        offs = before + jnp.concatenate([base] * (tile // LANES), axis=1)
        rank = jnp.where(hit, offs.reshape(GROUP_SIZE, N_GROUPS, tile), 0.0)
        rank_rows.append(jnp.sum(jnp.sum(rank, axis=0), axis=0, keepdims=True))
        base = base + jnp.dot(onehot, ones_cols, preferred_element_type=F32)
    carry_ref[...] = base
    cnt_ref[...] = base
    w_all = jnp.concatenate(w_rows, axis=0)
    w_ref[...] = w_all / jnp.sum(w_all, axis=0, keepdims=True) * ROUTED_SCALE
    e_ref[...] = jnp.concatenate(idx_rows, axis=0).astype(I32)
    rank_ref[...] = jnp.concatenate(rank_rows, axis=0).astype(I32)


def _route(h2, router_w, router_bias):
    n_tok = h2.shape[0]
    tile = min(n_tok, ROUTE_TILE)
    regroup = lambda a: a.reshape(N_GROUPS, GROUP_SIZE, -1).transpose(1, 0, 2).reshape(N_EXPERTS, -1)
    rw = regroup(router_w.T)
    bias = jnp.broadcast_to(regroup(router_bias.reshape(N_EXPERTS, 1)), (N_EXPERTS, tile))
    t_id = jnp.arange(tile, dtype=I32)
    upper = (t_id[:, None] < t_id[None, :]).astype(BF16)
    tok_blk = lambda dt: pl.BlockSpec((TOP_K, tile), lambda i: (0, i))
    e_t, w_t, rank_t, cnt = pl.pallas_call(
        _route_kernel,
        out_shape=[SDS((TOP_K, n_tok), I32), SDS((TOP_K, n_tok), F32), SDS((TOP_K, n_tok), I32),
                   SDS((N_EXPERTS, LANES), F32)],
        grid=(n_tok // tile,),
        in_specs=[pl.BlockSpec((tile, D_MODEL), lambda i: (i, 0)),
                  pl.BlockSpec((N_EXPERTS, D_MODEL), lambda i: (0, 0)),
                  pl.BlockSpec((N_EXPERTS, tile), lambda i: (0, 0)),
                  pl.BlockSpec((tile, tile), lambda i: (0, 0))],
        out_specs=[tok_blk(I32), tok_blk(F32), tok_blk(I32),
                   pl.BlockSpec((N_EXPERTS, LANES), lambda i: (0, 0))],
        scratch_shapes=[pltpu.VMEM((N_EXPERTS, LANES), F32)],
        compiler_params=_cparams(("arbitrary",)),
        name="moe_route",
    )(h2, rw, bias, upper)
    counts = cnt[:, 0].astype(I32).reshape(GROUP_SIZE, N_GROUPS).T.reshape(N_EXPERTS)
    return e_t, w_t, rank_t, counts


def _dispatch_kernel(pad_end_ref, pos_ref, h_ref, xs_ref, zero_ref, sem):
    tile = h_ref.shape[0]

    @pl.when(pl.program_id(0) == 0)
    def _():
        zero_ref[...] = jnp.zeros(zero_ref.shape, F32)

        def last_block(e):
            start = pl.multiple_of(jnp.maximum(pad_end_ref[e] - MOE_ROWS, 0), MOE_ROWS)
            return pltpu.make_async_copy(zero_ref, xs_ref.at[pl.ds(start, MOE_ROWS), :], sem)

        def has_rows(e):
            prev = jnp.where(e > 0, pad_end_ref[jnp.maximum(e - 1, 0)], 0)
            return pad_end_ref[e] > prev

        def clear(e, carry):
            @pl.when(has_rows(e))
            def _():
                last_block(e).start()
            return carry
        lax.fori_loop(0, N_EXPERTS, clear, 0)

        def done(e, carry):
            @pl.when(has_rows(e))
            def _():
                last_block(e).wait()
            return carry
        lax.fori_loop(0, N_EXPERTS, done, 0)

        def tail_block(b):
            start = pl.multiple_of(b * MOE_ROWS, MOE_ROWS)
            return pltpu.make_async_copy(zero_ref, xs_ref.at[pl.ds(start, MOE_ROWS), :], sem)

        def clear_tail(b, carry):
            tail_block(b).start()
            return carry

        def done_tail(b, carry):
            tail_block(b).wait()
            return carry
        used = pad_end_ref[N_EXPERTS - 1] // MOE_ROWS
        lax.fori_loop(used, xs_ref.shape[0] // MOE_ROWS, clear_tail, 0)
        lax.fori_loop(used, xs_ref.shape[0] // MOE_ROWS, done_tail, 0)

    def row_copy(j, t):
        return pltpu.make_async_copy(h_ref.at[pl.ds(t, 1), :],
                                     xs_ref.at[pl.ds(pos_ref[j, t], 1), :], sem)

    for j in range(TOP_K):
        def issue(t, carry, j=j):
            row_copy(j, t).start()
            return carry
        lax.fori_loop(0, tile, issue, 0, unroll=8)
    for j in range(TOP_K):
        pltpu.make_async_copy(h_ref, xs_ref.at[pl.ds(0, tile), :], sem).wait()


def _dispatch(h2, pos, pad_end, n_rows):
    n_tok = h2.shape[0]
    tile = min(n_tok, DISPATCH_TILE)
    grid_spec = pltpu.PrefetchScalarGridSpec(
        num_scalar_prefetch=1,
        grid=(n_tok // tile,),
        in_specs=[pl.BlockSpec((TOP_K, tile), lambda i, pe: (0, i), memory_space=pltpu.SMEM),
                  pl.BlockSpec((tile, D_MODEL), lambda i, pe: (i, 0))],
        out_specs=pl.BlockSpec(memory_space=pl.ANY),
        scratch_shapes=[pltpu.VMEM((MOE_ROWS, D_MODEL), F32), pltpu.SemaphoreType.DMA],
    )
    return pl.pallas_call(
        _dispatch_kernel,
        out_shape=SDS((n_rows, D_MODEL), F32),
        grid_spec=grid_spec,
        compiler_params=_cparams(("arbitrary",)),
        name="moe_dispatch",
    )(pad_end, pos, h2)


def _expert_kernel(be_ref, used_ref, xs_ref, w1_ref, w3_ref, w2_ref, y_ref):
    @pl.when(pl.program_id(0) < used_ref[0])
    def _():
        x = xs_ref[...].astype(BF16)
        hid = jax.nn.silu(_bdot(x, w1_ref[0])) * _bdot(x, w3_ref[0])
        y_ref[...] = _bdot(hid, w2_ref[0])

    @pl.when(pl.program_id(0) >= used_ref[0])
    def _():
        y_ref[...] = jnp.zeros(y_ref.shape, F32)


def _experts(xs, block_exp, used, w1, w3, w2):
    n_rows = xs.shape[0]
    rows = MOE_ROWS
    grid_spec = pltpu.PrefetchScalarGridSpec(
        num_scalar_prefetch=2,
        grid=(n_rows // rows,),
        in_specs=[pl.BlockSpec((rows, D_MODEL), lambda b, be, used: (b, 0)),
                  pl.BlockSpec((1, D_MODEL, EXPERT_FF), lambda b, be, used: (be[b], 0, 0)),
                  pl.BlockSpec((1, D_MODEL, EXPERT_FF), lambda b, be, used: (be[b], 0, 0)),
                  pl.BlockSpec((1, EXPERT_FF, D_MODEL), lambda b, be, used: (be[b], 0, 0))],
        out_specs=pl.BlockSpec((rows, D_MODEL), lambda b, be, used: (b, 0)),
    )
    return pl.pallas_call(
        _expert_kernel,
        out_shape=SDS((n_rows, D_MODEL), F32),
        grid_spec=grid_spec,
        compiler_params=_cparams(("arbitrary",)),
        name="moe_experts",
    )(block_exp, used, xs, w1, w3, w2)


def _combine_kernel(pos_ref, y_ref, w_ref, h_ref, x_ref, g2_ref, s1_ref, s3_ref, s2_ref, fn_ref,
                    o_ref, buf_ref, sem, *, final):
    tile = h_ref.shape[1]

    def row_copy(j, t):
        return pltpu.make_async_copy(y_ref.at[pl.ds(pos_ref[j, t], 1), :],
                                     buf_ref.at[j, pl.ds(t, 1), :], sem)

    for j in range(TOP_K):
        def issue(t, carry, j=j):
            row_copy(j, t).start()
            return carry
        lax.fori_loop(0, tile, issue, 0, unroll=8)
    h = h_ref[0]
    shared = _bdot(jax.nn.silu(_bdot(h, s1_ref[...])) * _bdot(h, s3_ref[...]), s2_ref[...])
    for j in range(TOP_K):
        pltpu.make_async_copy(y_ref.at[pl.ds(0, tile), :], buf_ref.at[j], sem).wait()
    w = w_ref[0]
    routed = buf_ref[0] * w[:, 0:1]
    for j in range(1, TOP_K):
        routed = routed + buf_ref[j] * w[:, j:j + 1]
    xn = x_ref[0] + g2_ref[0] * (routed + shared)
    if final:
        xn = xn * lax.rsqrt(jnp.mean(xn * xn, axis=-1, keepdims=True) + NORM_EPS) * fn_ref[...]
    o_ref[0] = xn


def _combine(y, pos, w_nat, h2, x, g2, s1_bf, s3_bf, s2_bf, final_norm, final):
    bsz, seq, _ = x.shape
    tile = min(seq, COMBINE_TILE)
    per_seq = seq // tile
    blk = pl.BlockSpec((1, tile, D_MODEL), lambda b, i: (b, i, 0))
    full = lambda shape: pl.BlockSpec(shape, lambda b, i: (0,) * len(shape))
    return pl.pallas_call(
        functools.partial(_combine_kernel, final=final),
        out_shape=SDS((bsz, seq, D_MODEL), F32),
        grid=(bsz, per_seq),
        in_specs=[pl.BlockSpec((TOP_K, tile), lambda b, i: (0, b * per_seq + i), memory_space=pltpu.SMEM),
                  pl.BlockSpec(memory_space=pl.ANY),
                  pl.BlockSpec((1, tile, TOP_K), lambda b, i: (b, i, 0)),
                  blk, blk, pl.BlockSpec((1, 1, D_MODEL), lambda b, i: (b, 0, 0)),
                  full((D_MODEL, SHARED_FF)), full((D_MODEL, SHARED_FF)), full((SHARED_FF, D_MODEL)),
                  full((1, D_MODEL))],
        out_specs=blk,
        scratch_shapes=[pltpu.VMEM((TOP_K, tile, D_MODEL), F32), pltpu.SemaphoreType.DMA],
        compiler_params=_cparams(("arbitrary", "arbitrary")),
        name="moe_combine",
    )(pos, y, w_nat, h2, x, g2, s1_bf, s3_bf, s2_bf, final_norm)


def _moe(x, h2, g2, router_w, router_bias, w1, w3, w2, s1_bf, s3_bf, s2_bf, final_norm, final):
    bsz, seq, _ = x.shape
    n_tok = bsz * seq
    e_t, w_t, rank_t, counts = _route(h2.reshape(n_tok, D_MODEL), router_w, router_bias)
    padded = (counts + MOE_ROWS - 1) // MOE_ROWS * MOE_ROWS
    pad_end = jnp.cumsum(padded)
    pad_start = pad_end - padded
    n_blocks = (n_tok * TOP_K + N_EXPERTS * (MOE_ROWS - 1) + MOE_ROWS - 1) // MOE_ROWS
    block_first = jnp.arange(n_blocks, dtype=I32) * MOE_ROWS
    block_exp = jnp.minimum(jnp.sum((pad_end[None, :] <= block_first[:, None]).astype(I32), axis=1),
                            N_EXPERTS - 1).astype(I32)
    used = (pad_end[-1:] // MOE_ROWS).astype(I32)
    pos = (jnp.sum(jnp.where(e_t[..., None] == jnp.arange(N_EXPERTS, dtype=I32), pad_start, 0), axis=-1)
           + rank_t).astype(I32)
    xs = _dispatch(h2.reshape(n_tok, D_MODEL), pos, pad_end.astype(I32), n_blocks * MOE_ROWS)
    y = _experts(xs, block_exp, used, w1, w3, w2)
    w_nat = w_t.T.reshape(bsz, seq, TOP_K)
    return _combine(y, pos, w_nat, h2, x, g2, s1_bf, s3_bf, s2_bf, final_norm, final)


def _pad_cols(a, width):
    return jnp.pad(a, ((0, 0), (0, width - a.shape[1])))


def _layout_w_in(w_in):
    b0 = COLS_A
    c0 = COLS_A + COLS_B
    g0 = c0 + COLS_C
    rkv = 3 * RWKV_DIM
    seg_b = jnp.concatenate([
        w_in[:, b0:b0 + rkv],
        _pad_cols(w_in[:, b0 + rkv:b0 + rkv + LORA_W], LANES),
        _pad_cols(w_in[:, b0 + rkv + LORA_W:b0 + rkv + LORA_W + LORA_A], LANES),
        _pad_cols(w_in[:, b0 + rkv + LORA_W + LORA_A:c0], 2 * LANES)], axis=1)
    return jnp.concatenate([w_in[:, c0:g0], seg_b, w_in[:, g0:], w_in[:, :b0]], axis=1).astype(BF16)


def _layout_mu(mu):
    rkv = 3 * RWKV_DIM
    m = mu.reshape(1, -1)
    return jnp.concatenate([
        m[:, :rkv], _pad_cols(m[:, rkv:rkv + LORA_W], LANES),
        _pad_cols(m[:, rkv + LORA_W:rkv + LORA_W + LORA_A], LANES),
        _pad_cols(m[:, rkv + LORA_W + LORA_A:], 2 * LANES)], axis=1)


def _pad_rows(a, height):
    return jnp.pad(a, ((0, height - a.shape[0]), (0, 0)))


def kernel(x, c, ada_w, ada_b, norm1, norm2, w_in, conv_a_w, conv_a_b, ln_a_g, ln_a_b, proj_a, mu_b, w0, w_up, a0, a_up, g_up, k_k, k_a, r_k, gn_b_g, gn_b_b, proj_b, conv_c_w, conv_c_b, lru_wa, lru_ba, lru_wx, lru_bx, lru_lambda, proj_c, w_out, router_w, router_bias, exp_w1, exp_w3, exp_w2, sh_w1, sh_w3, sh_w2, final_norm):
    depth = ada_w.shape[0]
    bsz = x.shape[0]
    mod = _ada_mod(c, ada_w, ada_b)
    for l in range(depth):
        sh1, sc1, g1, sh2, sc2, g2 = [mod[l, :, i * D_MODEL:(i + 1) * D_MODEL].reshape(bsz, 1, D_MODEL)
                                      for i in range(N_MOD)]
        p = _in_proj(x, sc1, sh1, norm1[l].reshape(1, -1), _layout_w_in(w_in[l]))
        merged = _conv_a(p, conv_a_w[l], conv_a_b[l], ln_a_g[l], ln_a_b[l], proj_a[l].astype(BF16))
        merged = _rwkv(p, merged, _layout_mu(mu_b[l]), w0[l], _pad_rows(w_up[l], LANES).astype(BF16),
                       a0[l], _pad_rows(a_up[l], LANES).astype(BF16), g_up[l].astype(BF16),
                       k_k[l], k_a[l], r_k[l], gn_b_g[l], gn_b_b[l], proj_b[l].astype(BF16))
        merged = _lru(p, merged, conv_c_w[l], conv_c_b[l], lru_wa[l].astype(BF16), lru_ba[l],
                      lru_wx[l].astype(BF16), lru_bx[l], lru_lambda[l], proj_c[l].astype(BF16))
        x, h2 = _out_proj(merged, x, g1, w_out[l].astype(BF16), norm2[l].reshape(1, -1), sc2, sh2)
        x = _moe(x, h2, g2, router_w[l], router_bias[l], exp_w1[l], exp_w3[l], exp_w2[l],
                 sh_w1[l].astype(BF16), sh_w3[l].astype(BF16), sh_w2[l].astype(BF16),
                 final_norm.reshape(1, -1), final=(l == depth - 1))
    return x
```

```python
import functools

import jax
import jax.numpy as jnp
from jax import lax
from jax.experimental import pallas as pl
from jax.experimental.pallas import tpu as pltpu

F32 = jnp.float32
BF16 = jnp.bfloat16
I32 = jnp.int32
SDS = jax.ShapeDtypeStruct
HIGHEST = lax.Precision.HIGHEST

D_MODEL = 1024
N_MOD = 6
NORM_EPS = 1e-6
CONV_A_CH = 512
CONV_A_WIDTH = 31
CONV_A_LN_EPS = 1e-5
RWKV_HEADS = 8
RWKV_HEAD = 64
RWKV_DIM = RWKV_HEADS * RWKV_HEAD
LORA_W = 64
LORA_A = 64
LORA_G = 128
RWKV_GN_EPS = 64e-5
RWKV_CHUNK = 64
LRU_DIM = 1024
LRU_HEADS = 8
LRU_BLOCK = LRU_DIM // LRU_HEADS
LRU_CONV = 4
LRU_C = 8.0
N_EXPERTS = 64
TOP_K = 8
N_GROUPS = 8
GROUP_SIZE = N_EXPERTS // N_GROUPS
TOPK_GROUPS = 4
EXPERT_FF = 256
SHARED_FF = 256
ROUTED_SCALE = 2.5
SEG_C = 0
SEG_B = 2048
SEG_G = 4096
SEG_A = 7168
IN_COLS_PAD = 8192
RW_R, RW_K, RW_V, RW_XW, RW_XA, RW_XG = 0, 512, 1024, 1536, 1664, 1792
COLS_A = 2 * CONV_A_CH
COLS_B = 3 * RWKV_DIM + LORA_W + LORA_A + LORA_G
COLS_C = 2 * LRU_DIM
VMEM_LIMIT = 56 * 1024 * 1024
SUBLANES = 8
LANES = 128
MOE_ROWS = 256
ROUTE_TILE = 512
DISPATCH_TILE = 256
COMBINE_TILE = 128


def _cparams(sem):
    return pltpu.CompilerParams(dimension_semantics=sem, vmem_limit_bytes=VMEM_LIMIT)


def _bdot(a, b):
    return jnp.dot(a.astype(BF16), b.astype(BF16), preferred_element_type=F32)


def _hdot(a, b):
    return jnp.dot(a, b, preferred_element_type=F32, precision=HIGHEST)


def _hdot_nt(a, b):
    return lax.dot_general(a, b, (((1,), (1,)), ((), ())), preferred_element_type=F32,
                           precision=HIGHEST)


def _hdot_tn(a, b):
    return lax.dot_general(a, b, (((0,), (0,)), ((), ())), preferred_element_type=F32,
                           precision=HIGHEST)


def _ada_kernel(c_ref, w_ref, b_ref, o_ref):
    cond = jax.nn.silu(c_ref[...])
    o_ref[0] = _bdot(cond, w_ref[0]) + b_ref[0]


def _ada_mod(c, ada_w, ada_b):
    depth, _, n = ada_w.shape
    bsz = c.shape[0]
    tn = 1536
    return pl.pallas_call(
        _ada_kernel,
        out_shape=SDS((depth, bsz, n), F32),
        grid=(depth, n // tn),
        in_specs=[pl.BlockSpec((bsz, D_MODEL), lambda l, j: (0, 0)),
                  pl.BlockSpec((1, D_MODEL, tn), lambda l, j: (l, 0, j)),
                  pl.BlockSpec((1, 1, tn), lambda l, j: (l, 0, j))],
        out_specs=pl.BlockSpec((1, bsz, tn), lambda l, j: (l, 0, j)),
        compiler_params=_cparams(("arbitrary", "arbitrary")),
        name="ada_mod",
    )(c, ada_w, ada_b.reshape(depth, 1, n))


def _modulated_rmsnorm(x, g, sc, sh):
    y = x * lax.rsqrt(jnp.mean(x * x, axis=-1, keepdims=True) + NORM_EPS)
    return (y * g) * (1.0 + sc) + sh


def _in_kernel(x_ref, sc_ref, sh_ref, g_ref, w_ref, o_ref, h_ref):
    @pl.when(pl.program_id(2) == 0)
    def _():
        h_ref[...] = _modulated_rmsnorm(x_ref[0], g_ref[...], sc_ref[0], sh_ref[0]).astype(BF16)

    o_ref[0] = jnp.dot(h_ref[...], w_ref[...], preferred_element_type=F32).astype(o_ref.dtype)


def _in_proj(x, sc, sh, g, w_pad):
    bsz, seq, _ = x.shape
    ts = min(seq, 1024)
    tn = 1024
    return pl.pallas_call(
        _in_kernel,
        out_shape=SDS((bsz, seq, IN_COLS_PAD), BF16),
        grid=(bsz, seq // ts, IN_COLS_PAD // tn),
        in_specs=[pl.BlockSpec((1, ts, D_MODEL), lambda b, i, j: (b, i, 0)),
                  pl.BlockSpec((1, 1, D_MODEL), lambda b, i, j: (b, 0, 0)),
                  pl.BlockSpec((1, 1, D_MODEL), lambda b, i, j: (b, 0, 0)),
                  pl.BlockSpec((1, D_MODEL), lambda b, i, j: (0, 0)),
                  pl.BlockSpec((D_MODEL, tn), lambda b, i, j: (0, j))],
        out_specs=pl.BlockSpec((1, ts, tn), lambda b, i, j: (b, i, j)),
        scratch_shapes=[pltpu.VMEM((ts, D_MODEL), BF16)],
        compiler_params=_cparams(("arbitrary", "arbitrary", "arbitrary")),
        name="in_proj",
    )(x, sc, sh, g, w_pad)


CONV_A_HALO = 32
CONV_A_SUB = 64


def _conv_a_kernel(pa_ref, pg_ref, cw_ref, cb_ref, lg_ref, lb_ref, pj_ref, o_ref, ext_ref, y_ref):
    ts = pa_ref.shape[1]

    @pl.when(pl.program_id(1) == 0)
    def _():
        ext_ref[pl.ds(0, CONV_A_HALO), :] = jnp.zeros((CONV_A_HALO, CONV_A_CH), F32)

    pa = pa_ref[0].astype(F32)
    ext_ref[pl.ds(CONV_A_HALO, ts), :] = pa[:, :CONV_A_CH] * jax.nn.sigmoid(pa[:, CONV_A_CH:])
    first = CONV_A_HALO - (CONV_A_WIDTH - 1)
    for r0 in range(0, ts, CONV_A_SUB):
        acc = jnp.zeros((CONV_A_SUB, CONV_A_CH), F32) + cb_ref[...]
        for j in range(CONV_A_WIDTH):
            acc = acc + ext_ref[pl.ds(r0 + first + j, CONV_A_SUB), :] * cw_ref[pl.ds(j, 1), :]
        y_ref[pl.ds(r0, CONV_A_SUB), :] = acc
    ext_ref[pl.ds(0, CONV_A_HALO), :] = ext_ref[pl.ds(ts, CONV_A_HALO), :]
    y = y_ref[...]
    mu = jnp.mean(y, axis=-1, keepdims=True)
    d = y - mu
    var = jnp.mean(d * d, axis=-1, keepdims=True)
    yn = d * lax.rsqrt(var + CONV_A_LN_EPS) * lg_ref[...] + lb_ref[...]
    o = _bdot(jax.nn.silu(yn), pj_ref[...])
    o_ref[0] = jax.nn.sigmoid(pg_ref[0].astype(F32)) * o


def _conv_a(p, conv_w, conv_b, ln_g, ln_b, proj_bf):
    bsz, seq, _ = p.shape
    ts = min(seq, 512)
    row = lambda a: a.reshape(1, -1)
    full = lambda shape: pl.BlockSpec(shape, lambda b, i: (0,) * len(shape))
    return pl.pallas_call(
        _conv_a_kernel,
        out_shape=SDS((bsz, seq, D_MODEL), F32),
        grid=(bsz, seq // ts),
        in_specs=[pl.BlockSpec((1, ts, 2 * CONV_A_CH), lambda b, i: (b, i, SEG_A // (2 * CONV_A_CH))),
                  pl.BlockSpec((1, ts, D_MODEL), lambda b, i: (b, i, SEG_G // D_MODEL)),
                  full((CONV_A_WIDTH, CONV_A_CH)), full((1, CONV_A_CH)), full((1, CONV_A_CH)),
                  full((1, CONV_A_CH)), full((CONV_A_CH, D_MODEL))],
        out_specs=pl.BlockSpec((1, ts, D_MODEL), lambda b, i: (b, i, 0)),
        scratch_shapes=[pltpu.VMEM((ts + CONV_A_HALO, CONV_A_CH), F32),
                        pltpu.VMEM((ts, CONV_A_CH), F32)],
        compiler_params=_cparams(("arbitrary", "arbitrary")),
        name="conv_a",
    )(p, p, conv_w, row(conv_b), row(ln_g), row(ln_b), proj_bf)


def _lru_kernel(pc_ref, pg_ref, m_ref, cw_ref, cb_ref, wa_ref, ba_ref, wx_ref, bx_ref, lam_ref,
                pj_ref, o_ref, ext_ref, h_ref, a_ref, b_ref):
    ts = pc_ref.shape[1]
    groups = ts // SUBLANES

    @pl.when(pl.program_id(1) == 0)
    def _():
        ext_ref[pl.ds(0, SUBLANES), :] = jnp.zeros((SUBLANES, LRU_DIM), F32)
        h_ref[...] = jnp.zeros((SUBLANES, LRU_DIM), F32)

    pc = pc_ref[0].astype(F32)
    y_gate = jax.nn.gelu(pc[:, :LRU_DIM], approximate=True)
    ext_ref[pl.ds(SUBLANES, ts), :] = pc[:, LRU_DIM:]
    first = SUBLANES - (LRU_CONV - 1)
    xc = jnp.zeros((ts, LRU_DIM), F32) + cb_ref[...]
    for j in range(LRU_CONV):
        xc = xc + ext_ref[pl.ds(first + j, ts), :] * cw_ref[pl.ds(j, 1), :]
    ext_ref[pl.ds(0, SUBLANES), :] = ext_ref[pl.ds(ts, SUBLANES), :]

    def block_diag(w_ref):
        return jnp.concatenate(
            [_bdot(xc[:, h * LRU_BLOCK:(h + 1) * LRU_BLOCK], w_ref[h]) for h in range(LRU_HEADS)],
            axis=1)

    gate_a = jax.nn.sigmoid(block_diag(wa_ref) + ba_ref[...])
    gate_x = jax.nn.sigmoid(block_diag(wx_ref) + bx_ref[...])
    log_a = -LRU_C * gate_a * jax.nn.softplus(-lam_ref[...])
    a = jnp.exp(log_a)
    b = xc * gate_x * jnp.sqrt(1.0 - jnp.exp(2.0 * log_a))

    a3 = a.reshape(groups, SUBLANES, LRU_DIM)
    b3 = b.reshape(groups, SUBLANES, LRU_DIM)
    row = lax.broadcasted_iota(I32, (groups, SUBLANES, LRU_DIM), 1)
    for s in (1, 2, 4):
        keep = row >= s
        b3 = jnp.where(keep, a3 * pltpu.roll(b3, s, axis=1) + b3, b3)
        a3 = jnp.where(keep, a3 * pltpu.roll(a3, s, axis=1), a3)
    a_ref[...] = a3.reshape(ts, LRU_DIM)
    b_ref[...] = b3.reshape(ts, LRU_DIM)
    h = h_ref[...]
    for g in range(groups):
        rows = pl.ds(g * SUBLANES, SUBLANES)
        hg = a_ref[rows, :] * h + b_ref[rows, :]
        b_ref[rows, :] = hg
        h = jnp.broadcast_to(hg[SUBLANES - 1:SUBLANES, :], (SUBLANES, LRU_DIM))
    h_ref[...] = h
    o = _bdot(b_ref[...] * y_gate, pj_ref[...])
    o_ref[0] = m_ref[0] + jax.nn.sigmoid(pg_ref[0].astype(F32)) * o


def _lru(p, merged, conv_w, conv_b, wa_bf, ba, wx_bf, bx, lam, proj_bf):
    bsz, seq, _ = p.shape
    ts = min(seq, 256)
    row = lambda a: a.reshape(1, -1)
    full = lambda shape: pl.BlockSpec(shape, lambda b, i: (0,) * len(shape))
    return pl.pallas_call(
        _lru_kernel,
        out_shape=SDS((bsz, seq, D_MODEL), F32),
        grid=(bsz, seq // ts),
        in_specs=[pl.BlockSpec((1, ts, 2 * LRU_DIM), lambda b, i: (b, i, SEG_C // (2 * LRU_DIM))),
                  pl.BlockSpec((1, ts, D_MODEL), lambda b, i: (b, i, SEG_G // D_MODEL + 2)),
                  pl.BlockSpec((1, ts, D_MODEL), lambda b, i: (b, i, 0)),
                  full((LRU_CONV, LRU_DIM)), full((1, LRU_DIM)),
                  full((LRU_HEADS, LRU_BLOCK, LRU_BLOCK)), full((1, LRU_DIM)),
                  full((LRU_HEADS, LRU_BLOCK, LRU_BLOCK)), full((1, LRU_DIM)),
                  full((1, LRU_DIM)), full((LRU_DIM, D_MODEL))],
        out_specs=pl.BlockSpec((1, ts, D_MODEL), lambda b, i: (b, i, 0)),
        scratch_shapes=[pltpu.VMEM((ts + SUBLANES, LRU_DIM), F32),
                        pltpu.VMEM((SUBLANES, LRU_DIM), F32),
                        pltpu.VMEM((ts, LRU_DIM), F32),
                        pltpu.VMEM((ts, LRU_DIM), F32)],
        compiler_params=_cparams(("arbitrary", "arbitrary")),
        name="rg_lru",
    )(p, p, merged, conv_w, row(conv_b), wa_bf, row(ba), wx_bf, row(bx), row(lam), proj_bf)


def _rwkv_prep_kernel(pb_ref, mu_ref, w0_ref, wup_ref, a0_ref, aup_ref, gup_ref, kk_ref, ka_ref,
                      rk_ref, bd_ref, ltri_ref,
                      rt_ref, kkt_ref, kh_ref, bh_ref, v_ref, pinc_ref, bonus_ref, g_ref, ext_ref):
    ts = pb_ref.shape[1]

    @pl.when(pl.program_id(1) == 0)
    def _():
        ext_ref[pl.ds(0, SUBLANES), :] = jnp.zeros((SUBLANES, ext_ref.shape[1]), F32)

    p = pb_ref[0].astype(F32)
    ext_ref[pl.ds(SUBLANES, ts), :] = p
    prev = ext_ref[pl.ds(SUBLANES - 1, ts), :]
    ext_ref[pl.ds(0, SUBLANES), :] = ext_ref[pl.ds(ts, SUBLANES), :]
    pm = p + (prev - p) * mu_ref[...]
    r = pm[:, RW_R:RW_R + RWKV_DIM]
    k = pm[:, RW_K:RW_K + RWKV_DIM]
    v = pm[:, RW_V:RW_V + RWKV_DIM]
    xw = pm[:, RW_XW:RW_XW + LANES]
    xa = pm[:, RW_XA:RW_XA + LANES]
    xg = pm[:, RW_XG:RW_XG + LORA_G]
    w = -jax.nn.softplus(-(w0_ref[...] + _bdot(jnp.tanh(xw), wup_ref[...]))) - 0.5
    lw = -jnp.exp(w)
    a = jax.nn.sigmoid(a0_ref[...] + _bdot(xa, aup_ref[...]))
    g_ref[0] = _bdot(jax.nn.sigmoid(xg), gup_ref[...])
    kkr = k * kk_ref[...]
    ss = _hdot(kkr * kkr, bd_ref[...])
    kk = kkr / jnp.maximum(jnp.sqrt(ss), 1e-12)
    k2 = k * (1.0 + (a - 1.0) * ka_ref[...])
    lcum = _hdot(ltri_ref[...], lw)
    pinc = jnp.exp(lcum)
    pinv = jnp.exp(-lcum)
    rt_ref[0] = r * pinc
    kkt_ref[0] = kk * jnp.exp(lcum - lw)
    kh_ref[0] = k2 * pinv
    bh_ref[0] = kk * a * pinv
    v_ref[0] = v
    pinc_ref[0] = pinc
    bonus_ref[0] = _hdot(r * k2 * rk_ref[...], bd_ref[...]) * v


def _split(a):
    hi = a.astype(BF16)
    return hi, (a - hi.astype(F32)).astype(BF16)


def _dot3(a, b):
    d = lambda x, y: jnp.dot(x, y, preferred_element_type=F32)
    return d(a[0], b[0]) + d(a[0], b[1]) + d(a[1], b[0])


def _rwkv_scan_kernel(rt_ref, kkt_ref, kh_ref, bh_ref, v_ref, pinc_ref, y_ref, s_ref):
    c = RWKV_CHUNK
    n = RWKV_HEAD
    heads = range(RWKV_HEADS)

    @pl.when(pl.program_id(1) == 0)
    def _():
        s_ref[...] = jnp.zeros(s_ref.shape, F32)

    row = lax.broadcasted_iota(I32, (c, c), 0)
    col = lax.broadcasted_iota(I32, (c, c), 1)
    eye = (row == col).astype(F32)
    same16 = (row // 16) == (col // 16)
    same32 = (row // 32) == (col // 32)
    row2 = lax.broadcasted_iota(I32, (c, 2 * c), 0)
    col2 = lax.broadcasted_iota(I32, (c, 2 * c), 1) % c
    nt = lambda a, b: lax.dot_general(a, b, (((1,), (1,)), ((), ())), preferred_element_type=F32)
    tn = lambda a, b: lax.dot_general(a, b, (((0,), (0,)), ((), ())), preferred_element_type=F32)
    dot = lambda a, b: jnp.dot(a, b, preferred_element_type=F32)
    sl = [pl.ds(h * n, n) for h in heads]
    v = [v_ref[0, :, sl[h]] for h in heads]
    pc = [pinc_ref[0, pl.ds(c - 1, 1), sl[h]] for h in heads]
    s = [s_ref[:, sl[h]] for h in heads]
    lhs = [jnp.concatenate([kkt_ref[0, :, sl[h]], rt_ref[0, :, sl[h]]], axis=0) for h in heads]
    rhs = [jnp.concatenate([bh_ref[0, :, sl[h]], kh_ref[0, :, sl[h]]], axis=0) for h in heads]
    big = [nt(lhs[h], rhs[h]) for h in heads]
    from_state = [nt(lhs[h], s[h]) for h in heads]
    top = [jnp.where(row2 > col2, big[h][:c], 0.0) for h in heads]
    bot = [jnp.where(row2 >= col2, big[h][c:], 0.0) for h in heads]
    a_b = [top[h][:, :c] for h in heads]
    akv = [dot(top[h], jnp.concatenate([jnp.zeros((c, n), F32), v[h]], axis=0)) for h in heads]
    d16 = [jnp.where(same16, a_b[h], 0.0) for h in heads]
    sd = [_split(d16[h]) for h in heads]
    s2 = [_split(_dot3(sd[h], sd[h])) for h in heads]
    s4 = [_split(_dot3(s2[h], s2[h])) for h in heads]
    s8 = [_split(_dot3(s4[h], s4[h])) for h in heads]
    t = [eye - d16[h] for h in heads]
    for sp in (s2, s4, s8):
        t = [t[h] + _dot3(_split(t[h]), sp[h]) for h in heads]
    for off in ([jnp.where(same32 & jnp.logical_not(same16), a_b[h], 0.0) for h in heads],
                [jnp.where(same32, 0.0, a_b[h]) for h in heads]):
        st = [_split(t[h]) for h in heads]
        lt = [_split(_dot3(_split(off[h]), st[h])) for h in heads]
        t = [t[h] - _dot3(st[h], lt[h]) for h in heads]
    u = [dot(t[h], from_state[h][:c] + akv[h]) for h in heads]
    vu = [jnp.concatenate([-u[h], v[h]], axis=0) for h in heads]
    y = [from_state[h][c:] + dot(bot[h], vu[h]) for h in heads]
    s_new = [s[h] * pc[h] + tn(vu[h], rhs[h] * pc[h]) for h in heads]
    y_ref[0] = jnp.concatenate(y, axis=1)
    s_ref[...] = jnp.concatenate(s_new, axis=1)


def _rwkv_post_kernel(y_ref, bonus_ref, g_ref, pg_ref, m_ref, gg_ref, gb_ref, bdm_ref, pj_ref, o_ref):
    y = y_ref[0]
    mu = _hdot(y, bdm_ref[...])
    d = y - mu
    var = _hdot(d * d, bdm_ref[...])
    yn = d * lax.rsqrt(var + RWKV_GN_EPS) * gg_ref[...] + gb_ref[...] + bonus_ref[0]
    o = _bdot(yn * g_ref[0], pj_ref[...])
    o_ref[0] = m_ref[0] + jax.nn.sigmoid(pg_ref[0].astype(F32)) * o


def _rwkv(p, merged, mu_pad, w0, wup_pad, a0, aup_pad, g_up, k_k, k_a, r_k, gn_g, gn_b, proj_bf):
    bsz, seq, _ = p.shape
    row = lambda a: a.reshape(1, -1)
    full = lambda shape: pl.BlockSpec(shape, lambda b, i: (0,) * len(shape))
    head_id = jnp.arange(RWKV_DIM, dtype=I32) // RWKV_HEAD
    bd = (head_id[:, None] == head_id[None, :]).astype(F32)

    ts = min(seq, 256)
    t_id = jnp.arange(ts, dtype=I32)
    ltri = ((t_id[:, None] // RWKV_CHUNK == t_id[None, :] // RWKV_CHUNK)
            & (t_id[:, None] >= t_id[None, :])).astype(F32)
    seq_blk = lambda width: pl.BlockSpec((1, ts, width), lambda b, i: (b, i, 0))
    wide = SDS((bsz, seq, RWKV_DIM), F32)
    rt, kkt, kh, bh, v, pinc, bonus, g = pl.pallas_call(
        _rwkv_prep_kernel,
        out_shape=[wide] * 8,
        grid=(bsz, seq // ts),
        in_specs=[pl.BlockSpec((1, ts, 2048), lambda b, i: (b, i, SEG_B // 2048)),
                  full((1, 2048)), full((1, RWKV_DIM)), full((LANES, RWKV_DIM)),
                  full((1, RWKV_DIM)), full((LANES, RWKV_DIM)), full((LORA_G, RWKV_DIM)),
                  full((1, RWKV_DIM)), full((1, RWKV_DIM)), full((1, RWKV_DIM)),
                  full((RWKV_DIM, RWKV_DIM)), full((ts, ts))],
        out_specs=[seq_blk(RWKV_DIM)] * 8,
        scratch_shapes=[pltpu.VMEM((ts + SUBLANES, 2048), F32)],
        compiler_params=_cparams(("arbitrary", "arbitrary")),
        name="rwkv_prep",
    )(p, mu_pad, row(w0), wup_pad, row(a0), aup_pad, g_up, row(k_k), row(k_a), row(r_k), bd, ltri)

    c = RWKV_CHUNK
    chunk_blk = pl.BlockSpec((1, c, RWKV_DIM), lambda b, i: (b, i, 0))
    y = pl.pallas_call(
        _rwkv_scan_kernel,
        out_shape=wide,
        grid=(bsz, seq // c),
        in_specs=[chunk_blk] * 6,
        out_specs=chunk_blk,
        scratch_shapes=[pltpu.VMEM((RWKV_HEAD, RWKV_DIM), F32)],
        compiler_params=_cparams(("arbitrary", "arbitrary")),
        name="rwkv_scan",
    )(rt, kkt, kh, bh, v, pinc)

    tp = min(seq, 512)
    blk = lambda width: pl.BlockSpec((1, tp, width), lambda b, i: (b, i, 0))
    return pl.pallas_call(
        _rwkv_post_kernel,
        out_shape=SDS((bsz, seq, D_MODEL), F32),
        grid=(bsz, seq // tp),
        in_specs=[blk(RWKV_DIM), blk(RWKV_DIM), blk(RWKV_DIM),
                  pl.BlockSpec((1, tp, D_MODEL), lambda b, i: (b, i, SEG_G // D_MODEL + 1)),
                  blk(D_MODEL), full((1, RWKV_DIM)), full((1, RWKV_DIM)),
                  full((RWKV_DIM, RWKV_DIM)), full((RWKV_DIM, D_MODEL))],
        out_specs=blk(D_MODEL),
        compiler_params=_cparams(("arbitrary", "arbitrary")),
        name="rwkv_post",
    )(y, bonus, g, p, merged, row(gn_g), row(gn_b), bd * (1.0 / RWKV_HEAD), proj_bf)


def _out_kernel(m_ref, x_ref, g1_ref, w_ref, n2_ref, sc_ref, sh_ref, xo_ref, h_ref):
    xn = x_ref[0] + g1_ref[0] * _bdot(m_ref[0], w_ref[...])
    xo_ref[0] = xn
    h_ref[0] = _modulated_rmsnorm(xn, n2_ref[...], sc_ref[0], sh_ref[0])


def _out_proj(merged, x, g1, w_bf, norm2, sc2, sh2):
    bsz, seq, _ = x.shape
    ts = min(seq, 512)
    blk = pl.BlockSpec((1, ts, D_MODEL), lambda b, i: (b, i, 0))
    per_b = pl.BlockSpec((1, 1, D_MODEL), lambda b, i: (b, 0, 0))
    return pl.pallas_call(
        _out_kernel,
        out_shape=[SDS((bsz, seq, D_MODEL), F32)] * 2,
        grid=(bsz, seq // ts),
        in_specs=[blk, blk, per_b, pl.BlockSpec((D_MODEL, D_MODEL), lambda b, i: (0, 0)),
                  pl.BlockSpec((1, D_MODEL), lambda b, i: (0, 0)), per_b, per_b],
        out_specs=[blk, blk],
        compiler_params=_cparams(("arbitrary", "arbitrary")),
        name="out_proj",
    )(merged, x, g1, w_bf, norm2, sc2, sh2)


def _route_kernel(h_ref, rw_ref, bias_ref, upper_ref, e_ref, w_ref, rank_ref, cnt_ref, carry_ref):
    tile = h_ref.shape[0]

    @pl.when(pl.program_id(0) == 0)
    def _():
        carry_ref[...] = jnp.zeros(carry_ref.shape, F32)

    neg = -jnp.inf
    scores = jax.nn.sigmoid(_hdot_nt(rw_ref[...], h_ref[...]))
    s3 = scores.reshape(GROUP_SIZE, N_GROUPS, tile)
    b3 = s3 + bias_ref[...].reshape(GROUP_SIZE, N_GROUPS, tile)
    slab = lax.broadcasted_iota(I32, b3.shape, 0).astype(F32)
    grp = lax.broadcasted_iota(I32, b3.shape, 1).astype(F32)
    eid = grp * GROUP_SIZE + slab
    m1 = jnp.max(b3, axis=0, keepdims=True)
    first = jnp.min(jnp.where(b3 == m1, slab, GROUP_SIZE), axis=0, keepdims=True)
    m2 = jnp.max(jnp.where(slab == first, neg, b3), axis=0, keepdims=True)
    gs = (m1 + m2)[0]
    gi = lax.broadcasted_iota(I32, gs.shape, 0).astype(F32)
    chosen = jnp.zeros(gs.shape, F32)
    for _ in range(TOPK_GROUPS):
        m = jnp.max(gs, axis=0, keepdims=True)
        hit = gi == jnp.min(jnp.where(gs == m, gi, N_GROUPS), axis=0, keepdims=True)
        chosen = jnp.where(hit, 1.0, chosen)
        gs = jnp.where(hit, neg, gs)
    cur = jnp.where((chosen > 0.0)[None], b3, neg)
    ones_cols = jnp.ones((tile, LANES), BF16)
    base = carry_ref[...]
    idx_rows, w_rows, rank_rows = [], [], []
    for _ in range(TOP_K):
        m = jnp.max(jnp.max(cur, axis=0), axis=0, keepdims=True)[None]
        pick = jnp.min(jnp.min(jnp.where(cur == m, eid, N_EXPERTS), axis=0), axis=0, keepdims=True)
        hit = eid == pick[None]
        w_rows.append(jnp.sum(jnp.sum(jnp.where(hit, s3, 0.0), axis=0), axis=0, keepdims=True))
        idx_rows.append(pick)
        cur = jnp.where(hit, neg, cur)
        onehot = hit.astype(BF16).reshape(N_EXPERTS, tile)
        before = jnp.dot(onehot, upper_ref[...], preferred_element_type=F32)
        offs = before + jnp.concatenate([base] * (tile // LANES), axis=1)
        rank = jnp.where(hit, offs.reshape(GROUP_SIZE, N_GROUPS, tile), 0.0)
        rank_rows.append(jnp.sum(jnp.sum(rank, axis=0), axis=0, keepdims=True))
        base = base + jnp.dot(onehot, ones_cols, preferred_element_type=F32)
    carry_ref[...] = base
    cnt_ref[...] = base
    w_all = jnp.concatenate(w_rows, axis=0)
    w_ref[...] = w_all / jnp.sum(w_all, axis=0, keepdims=True) * ROUTED_SCALE
    e_ref[...] = jnp.concatenate(idx_rows, axis=0).astype(I32)
    rank_ref[...] = jnp.concatenate(rank_rows, axis=0).astype(I32)


def _route(h2, router_w, router_bias):
    n_tok = h2.shape[0]
    tile = min(n_tok, ROUTE_TILE)
    regroup = lambda a: a.reshape(N_GROUPS, GROUP_SIZE, -1).transpose(1, 0, 2).reshape(N_EXPERTS, -1)
    rw = regroup(router_w.T)
    bias = jnp.broadcast_to(regroup(router_bias.reshape(N_EXPERTS, 1)), (N_EXPERTS, tile))
    t_id = jnp.arange(tile, dtype=I32)
    upper = (t_id[:, None] < t_id[None, :]).astype(BF16)
    tok_blk = lambda dt: pl.BlockSpec((TOP_K, tile), lambda i: (0, i))
    e_t, w_t, rank_t, cnt = pl.pallas_call(
        _route_kernel,
        out_shape=[SDS((TOP_K, n_tok), I32), SDS((TOP_K, n_tok), F32), SDS((TOP_K, n_tok), I32),
                   SDS((N_EXPERTS, LANES), F32)],
        grid=(n_tok // tile,),
        in_specs=[pl.BlockSpec((tile, D_MODEL), lambda i: (i, 0)),
                  pl.BlockSpec((N_EXPERTS, D_MODEL), lambda i: (0, 0)),
                  pl.BlockSpec((N_EXPERTS, tile), lambda i: (0, 0)),
                  pl.BlockSpec((tile, tile), lambda i: (0, 0))],
        out_specs=[tok_blk(I32), tok_blk(F32), tok_blk(I32),
                   pl.BlockSpec((N_EXPERTS, LANES), lambda i: (0, 0))],
        scratch_shapes=[pltpu.VMEM((N_EXPERTS, LANES), F32)],
        compiler_params=_cparams(("arbitrary",)),
        name="moe_route",
    )(h2, rw, bias, upper)
    counts = cnt[:, 0].astype(I32).reshape(GROUP_SIZE, N_GROUPS).T.reshape(N_EXPERTS)
    return e_t, w_t, rank_t, counts


def _dispatch_kernel(pad_end_ref, pos_ref, h_ref, xs_ref, zero_ref, sem):
    tile = h_ref.shape[0]

    @pl.when(pl.program_id(0) == 0)
    def _():
        zero_ref[...] = jnp.zeros(zero_ref.shape, F32)

        def last_block(e):
            start = pl.multiple_of(jnp.maximum(pad_end_ref[e] - MOE_ROWS, 0), MOE_ROWS)
            return pltpu.make_async_copy(zero_ref, xs_ref.at[pl.ds(start, MOE_ROWS), :], sem)

        def has_rows(e):
            prev = jnp.where(e > 0, pad_end_ref[jnp.maximum(e - 1, 0)], 0)
            return pad_end_ref[e] > prev

        def clear(e, carry):
            @pl.when(has_rows(e))
            def _():
                last_block(e).start()
            return carry
        lax.fori_loop(0, N_EXPERTS, clear, 0)

        def done(e, carry):
            @pl.when(has_rows(e))
            def _():
                last_block(e).wait()
            return carry
        lax.fori_loop(0, N_EXPERTS, done, 0)

        def tail_block(b):
            start = pl.multiple_of(b * MOE_ROWS, MOE_ROWS)
            return pltpu.make_async_copy(zero_ref, xs_ref.at[pl.ds(start, MOE_ROWS), :], sem)

        def clear_tail(b, carry):
            tail_block(b).start()
            return carry

        def done_tail(b, carry):
            tail_block(b).wait()
            return carry
        used = pad_end_ref[N_EXPERTS - 1] // MOE_ROWS
        lax.fori_loop(used, xs_ref.shape[0] // MOE_ROWS, clear_tail, 0)
        lax.fori_loop(used, xs_ref.shape[0] // MOE_ROWS, done_tail, 0)

    def row_copy(j, t):
        return pltpu.make_async_copy(h_ref.at[pl.ds(t, 1), :],
                                     xs_ref.at[pl.ds(pos_ref[j, t], 1), :], sem)

    for j in range(TOP_K):
        def issue(t, carry, j=j):
            row_copy(j, t).start()
            return carry
        lax.fori_loop(0, tile, issue, 0, unroll=8)
    for j in range(TOP_K):
        pltpu.make_async_copy(h_ref, xs_ref.at[pl.ds(0, tile), :], sem).wait()


def _dispatch(h2, pos, pad_end, n_rows):
    n_tok = h2.shape[0]
    tile = min(n_tok, DISPATCH_TILE)
    grid_spec = pltpu.PrefetchScalarGridSpec(
        num_scalar_prefetch=1,
        grid=(n_tok // tile,),
        in_specs=[pl.BlockSpec((TOP_K, tile), lambda i, pe: (0, i), memory_space=pltpu.SMEM),
                  pl.BlockSpec((tile, D_MODEL), lambda i, pe: (i, 0))],
        out_specs=pl.BlockSpec(memory_space=pl.ANY),
        scratch_shapes=[pltpu.VMEM((MOE_ROWS, D_MODEL), F32), pltpu.SemaphoreType.DMA],
    )
    return pl.pallas_call(
        _dispatch_kernel,
        out_shape=SDS((n_rows, D_MODEL), F32),
        grid_spec=grid_spec,
        compiler_params=_cparams(("arbitrary",)),
        name="moe_dispatch",
    )(pad_end, pos, h2)


def _expert_kernel(be_ref, used_ref, xs_ref, w1_ref, w3_ref, w2_ref, y_ref):
    @pl.when(pl.program_id(0) < used_ref[0])
    def _():
        x = xs_ref[...].astype(BF16)
        hid = jax.nn.silu(_bdot(x, w1_ref[0])) * _bdot(x, w3_ref[0])
        y_ref[...] = _bdot(hid, w2_ref[0])

    @pl.when(pl.program_id(0) >= used_ref[0])
    def _():
        y_ref[...] = jnp.zeros(y_ref.shape, F32)


def _experts(xs, block_exp, used, w1, w3, w2):
    n_rows = xs.shape[0]
    rows = MOE_ROWS
    grid_spec = pltpu.PrefetchScalarGridSpec(
        num_scalar_prefetch=2,
        grid=(n_rows // rows,),
        in_specs=[pl.BlockSpec((rows, D_MODEL), lambda b, be, used: (b, 0)),
                  pl.BlockSpec((1, D_MODEL, EXPERT_FF), lambda b, be, used: (be[b], 0, 0)),
                  pl.BlockSpec((1, D_MODEL, EXPERT_FF), lambda b, be, used: (be[b], 0, 0)),
                  pl.BlockSpec((1, EXPERT_FF, D_MODEL), lambda b, be, used: (be[b], 0, 0))],
        out_specs=pl.BlockSpec((rows, D_MODEL), lambda b, be, used: (b, 0)),
    )
    return pl.pallas_call(
        _expert_kernel,
        out_shape=SDS((n_rows, D_MODEL), F32),
        grid_spec=grid_spec,
        compiler_params=_cparams(("arbitrary",)),
        name="moe_experts",
    )(block_exp, used, xs, w1, w3, w2)


def _combine_kernel(pos_ref, y_ref, w_ref, h_ref, x_ref, g2_ref, s1_ref, s3_ref, s2_ref, fn_ref,
                    o_ref, buf_ref, sem, *, final):
    tile = h_ref.shape[1]

    def row_copy(j, t):
        return pltpu.make_async_copy(y_ref.at[pl.ds(pos_ref[j, t], 1), :],
                                     buf_ref.at[j, pl.ds(t, 1), :], sem)

    for j in range(TOP_K):
        def issue(t, carry, j=j):
            row_copy(j, t).start()
            return carry
        lax.fori_loop(0, tile, issue, 0, unroll=8)
    h = h_ref[0]
    shared = _bdot(jax.nn.silu(_bdot(h, s1_ref[...])) * _bdot(h, s3_ref[...]), s2_ref[...])
    for j in range(TOP_K):
        pltpu.make_async_copy(y_ref.at[pl.ds(0, tile), :], buf_ref.at[j], sem).wait()
    w = w_ref[0]
    routed = buf_ref[0] * w[:, 0:1]
    for j in range(1, TOP_K):
        routed = routed + buf_ref[j] * w[:, j:j + 1]
    xn = x_ref[0] + g2_ref[0] * (routed + shared)
    if final:
        xn = xn * lax.rsqrt(jnp.mean(xn * xn, axis=-1, keepdims=True) + NORM_EPS) * fn_ref[...]
    o_ref[0] = xn


def _combine(y, pos, w_nat, h2, x, g2, s1_bf, s3_bf, s2_bf, final_norm, final):
    bsz, seq, _ = x.shape
    tile = min(seq, COMBINE_TILE)
    per_seq = seq // tile
    blk = pl.BlockSpec((1, tile, D_MODEL), lambda b, i: (b, i, 0))
    full = lambda shape: pl.BlockSpec(shape, lambda b, i: (0,) * len(shape))
    return pl.pallas_call(
        functools.partial(_combine_kernel, final=final),
        out_shape=SDS((bsz, seq, D_MODEL), F32),
        grid=(bsz, per_seq),
        in_specs=[pl.BlockSpec((TOP_K, tile), lambda b, i: (0, b * per_seq + i), memory_space=pltpu.SMEM),
                  pl.BlockSpec(memory_space=pl.ANY),
                  pl.BlockSpec((1, tile, TOP_K), lambda b, i: (b, i, 0)),
                  blk, blk, pl.BlockSpec((1, 1, D_MODEL), lambda b, i: (b, 0, 0)),
                  full((D_MODEL, SHARED_FF)), full((D_MODEL, SHARED_FF)), full((SHARED_FF, D_MODEL)),
                  full((1, D_MODEL))],
        out_specs=blk,
        scratch_shapes=[pltpu.VMEM((TOP_K, tile, D_MODEL), F32), pltpu.SemaphoreType.DMA],
        compiler_params=_cparams(("arbitrary", "arbitrary")),
        name="moe_combine",
    )(pos, y, w_nat, h2, x, g2, s1_bf, s3_bf, s2_bf, final_norm)


def _moe(x, h2, g2, router_w, router_bias, w1, w3, w2, s1_bf, s3_bf, s2_bf, final_norm, final):
    bsz, seq, _ = x.shape
    n_tok = bsz * seq
    e_t, w_t, rank_t, counts = _route(h2.reshape(n_tok, D_MODEL), router_w, router_bias)
    padded = (counts + MOE_ROWS - 1) // MOE_ROWS * MOE_ROWS
    pad_end = jnp.cumsum(padded)
    pad_start = pad_end - padded
    n_blocks = (n_tok * TOP_K + N_EXPERTS * (MOE_ROWS - 1) + MOE_ROWS - 1) // MOE_ROWS
    block_first = jnp.arange(n_blocks, dtype=I32) * MOE_ROWS
    block_exp = jnp.minimum(jnp.sum((pad_end[None, :] <= block_first[:, None]).astype(I32), axis=1),
                            N_EXPERTS - 1).astype(I32)
    used = (pad_end[-1:] // MOE_ROWS).astype(I32)
    pos = (jnp.sum(jnp.where(e_t[..., None] == jnp.arange(N_EXPERTS, dtype=I32), pad_start, 0), axis=-1)
           + rank_t).astype(I32)
    xs = _dispatch(h2.reshape(n_tok, D_MODEL), pos, pad_end.astype(I32), n_blocks * MOE_ROWS)
    y = _experts(xs, block_exp, used, w1, w3, w2)
    w_nat = w_t.T.reshape(bsz, seq, TOP_K)
    return _combine(y, pos, w_nat, h2, x, g2, s1_bf, s3_bf, s2_bf, final_norm, final)


def _pad_cols(a, width):
    return jnp.pad(a, ((0, 0), (0, width - a.shape[1])))


def _layout_w_in(w_in):
    b0 = COLS_A
    c0 = COLS_A + COLS_B
    g0 = c0 + COLS_C
    rkv = 3 * RWKV_DIM
    seg_b = jnp.concatenate([
        w_in[:, b0:b0 + rkv],
        _pad_cols(w_in[:, b0 + rkv:b0 + rkv + LORA_W], LANES),
        _pad_cols(w_in[:, b0 + rkv + LORA_W:b0 + rkv + LORA_W + LORA_A], LANES),
        _pad_cols(w_in[:, b0 + rkv + LORA_W + LORA_A:c0], 2 * LANES)], axis=1)
    return jnp.concatenate([w_in[:, c0:g0], seg_b, w_in[:, g0:], w_in[:, :b0]], axis=1).astype(BF16)


def _layout_mu(mu):
    rkv = 3 * RWKV_DIM
    m = mu.reshape(1, -1)
    return jnp.concatenate([
        m[:, :rkv], _pad_cols(m[:, rkv:rkv + LORA_W], LANES),
        _pad_cols(m[:, rkv + LORA_W:rkv + LORA_W + LORA_A], LANES),
        _pad_cols(m[:, rkv + LORA_W + LORA_A:], 2 * LANES)], axis=1)


def _pad_rows(a, height):
    return jnp.pad(a, ((0, height - a.shape[0]), (0, 0)))


def kernel(x, c, ada_w, ada_b, norm1, norm2, w_in, conv_a_w, conv_a_b, ln_a_g, ln_a_b, proj_a, mu_b, w0, w_up, a0, a_up, g_up, k_k, k_a, r_k, gn_b_g, gn_b_b, proj_b, conv_c_w, conv_c_b, lru_wa, lru_ba, lru_wx, lru_bx, lru_lambda, proj_c, w_out, router_w, router_bias, exp_w1, exp_w3, exp_w2, sh_w1, sh_w3, sh_w2, final_norm):
    depth = ada_w.shape[0]
    bsz = x.shape[0]
    mod = _ada_mod(c, ada_w, ada_b)
    for l in range(depth):
        sh1, sc1, g1, sh2, sc2, g2 = [mod[l, :, i * D_MODEL:(i + 1) * D_MODEL].reshape(bsz, 1, D_MODEL)
                                      for i in range(N_MOD)]
        p = _in_proj(x, sc1, sh1, norm1[l].reshape(1, -1), _layout_w_in(w_in[l]))
        merged = _conv_a(p, conv_a_w[l], conv_a_b[l], ln_a_g[l], ln_a_b[l], proj_a[l].astype(BF16))
        merged = _rwkv(p, merged, _layout_mu(mu_b[l]), w0[l], _pad_rows(w_up[l], LANES).astype(BF16),
                       a0[l], _pad_rows(a_up[l], LANES).astype(BF16), g_up[l].astype(BF16),
                       k_k[l], k_a[l], r_k[l], gn_b_g[l], gn_b_b[l], proj_b[l].astype(BF16))
        merged = _lru(p, merged, conv_c_w[l], conv_c_b[l], lru_wa[l].astype(BF16), lru_ba[l],
                      lru_wx[l].astype(BF16), lru_bx[l], lru_lambda[l], proj_c[l].astype(BF16))
        x, h2 = _out_proj(merged, x, g1, w_out[l].astype(BF16), norm2[l].reshape(1, -1), sc2, sh2)
        x = _moe(x, h2, g2, router_w[l], router_bias[l], exp_w1[l], exp_w3[l], exp_w2[l],
                 sh_w1[l].astype(BF16), sh_w3[l].astype(BF16), sh_w2[l].astype(BF16),
                 final_norm.reshape(1, -1), final=(l == depth - 1))
    return x
```

```python
import functools

import jax
import jax.numpy as jnp
from jax import lax
from jax.experimental import pallas as pl
from jax.experimental.pallas import tpu as pltpu

F32 = jnp.float32
BF16 = jnp.bfloat16
I32 = jnp.int32
SDS = jax.ShapeDtypeStruct
HIGHEST = lax.Precision.HIGHEST

D_MODEL = 1024
N_MOD = 6
NORM_EPS = 1e-6
CONV_A_CH = 512
CONV_A_WIDTH = 31
CONV_A_LN_EPS = 1e-5
RWKV_HEADS = 8
RWKV_HEAD = 64
RWKV_DIM = RWKV_HEADS * RWKV_HEAD
LORA_W = 64
LORA_A = 64
LORA_G = 128
RWKV_GN_EPS = 64e-5
RWKV_CHUNK = 64
LRU_DIM = 1024
LRU_HEADS = 8
LRU_BLOCK = LRU_DIM // LRU_HEADS
LRU_CONV = 4
LRU_C = 8.0
N_EXPERTS = 64
TOP_K = 8
N_GROUPS = 8
GROUP_SIZE = N_EXPERTS // N_GROUPS
TOPK_GROUPS = 4
EXPERT_FF = 256
SHARED_FF = 256
ROUTED_SCALE = 2.5
SEG_C = 0
SEG_B = 2048
SEG_G = 4096
SEG_A = 7168
IN_COLS_PAD = 8192
RW_R, RW_K, RW_V, RW_XW, RW_XA, RW_XG = 0, 512, 1024, 1536, 1664, 1792
COLS_A = 2 * CONV_A_CH
COLS_B = 3 * RWKV_DIM + LORA_W + LORA_A + LORA_G
COLS_C = 2 * LRU_DIM
VMEM_LIMIT = 56 * 1024 * 1024
SUBLANES = 8
LANES = 128
MOE_ROWS = 256


def _cparams(sem):
    return pltpu.CompilerParams(dimension_semantics=sem, vmem_limit_bytes=VMEM_LIMIT)


def _bdot(a, b):
    return jnp.dot(a.astype(BF16), b.astype(BF16), preferred_element_type=F32)


def _hdot(a, b):
    return jnp.dot(a, b, preferred_element_type=F32, precision=HIGHEST)


def _split(a):
    hi = a.astype(BF16)
    return hi, (a - hi.astype(F32)).astype(BF16)


def _head_sums(a, ones_bf):
    hi, lo = _split(a)
    return (jnp.dot(hi, ones_bf, preferred_element_type=F32)
            + jnp.dot(lo, ones_bf, preferred_element_type=F32))


def _hdot_nt(a, b):
    return lax.dot_general(a, b, (((1,), (1,)), ((), ())), preferred_element_type=F32,
                           precision=HIGHEST)


def _hdot_tn(a, b):
    return lax.dot_general(a, b, (((0,), (0,)), ((), ())), preferred_element_type=F32,
                           precision=HIGHEST)


def _ada_kernel(c_ref, w_ref, b_ref, o_ref):
    cond = jax.nn.silu(c_ref[...])
    o_ref[0] = _bdot(cond, w_ref[0]) + b_ref[0]


def _ada_mod(c, ada_w, ada_b):
    depth, _, n = ada_w.shape
    bsz = c.shape[0]
    tn = 1536
    return pl.pallas_call(
        _ada_kernel,
        out_shape=SDS((depth, bsz, n), F32),
        grid=(depth, n // tn),
        in_specs=[pl.BlockSpec((bsz, D_MODEL), lambda l, j: (0, 0)),
                  pl.BlockSpec((1, D_MODEL, tn), lambda l, j: (l, 0, j)),
                  pl.BlockSpec((1, 1, tn), lambda l, j: (l, 0, j))],
        out_specs=pl.BlockSpec((1, bsz, tn), lambda l, j: (l, 0, j)),
        compiler_params=_cparams(("arbitrary", "arbitrary")),
        name="ada_mod",
    )(c, ada_w, ada_b.reshape(depth, 1, n))


def _modulated_rmsnorm(x, g, sc, sh):
    y = x * lax.rsqrt(jnp.mean(x * x, axis=-1, keepdims=True) + NORM_EPS)
    return (y * g) * (1.0 + sc) + sh


def _in_kernel(x_ref, sc_ref, sh_ref, g_ref, w_ref, o_ref, h_ref):
    @pl.when(pl.program_id(2) == 0)
    def _():
        h_ref[...] = _modulated_rmsnorm(x_ref[0], g_ref[...], sc_ref[0], sh_ref[0]).astype(BF16)

    o_ref[0] = jnp.dot(h_ref[...], w_ref[...], preferred_element_type=F32).astype(o_ref.dtype)


def _in_proj(x, sc, sh, g, w_pad):
    bsz, seq, _ = x.shape
    ts = min(seq, 1024)
    tn = 1024
    return pl.pallas_call(
        _in_kernel,
        out_shape=SDS((bsz, seq, IN_COLS_PAD), BF16),
        grid=(bsz, seq // ts, IN_COLS_PAD // tn),
        in_specs=[pl.BlockSpec((1, ts, D_MODEL), lambda b, i, j: (b, i, 0)),
                  pl.BlockSpec((1, 1, D_MODEL), lambda b, i, j: (b, 0, 0)),
                  pl.BlockSpec((1, 1, D_MODEL), lambda b, i, j: (b, 0, 0)),
                  pl.BlockSpec((1, D_MODEL), lambda b, i, j: (0, 0)),
                  pl.BlockSpec((D_MODEL, tn), lambda b, i, j: (0, j))],
        out_specs=pl.BlockSpec((1, ts, tn), lambda b, i, j: (b, i, j)),
        scratch_shapes=[pltpu.VMEM((ts, D_MODEL), BF16)],
        compiler_params=_cparams(("arbitrary", "arbitrary", "arbitrary")),
        name="in_proj",
    )(x, sc, sh, g, w_pad)


CONV_A_HALO = 32
CONV_A_SUB = 64


def _conv_a_kernel(pa_ref, pg_ref, cw_ref, cb_ref, lg_ref, lb_ref, pj_ref, o_ref, ext_ref, y_ref):
    ts = pa_ref.shape[1]

    @pl.when(pl.program_id(1) == 0)
    def _():
        ext_ref[pl.ds(0, CONV_A_HALO), :] = jnp.zeros((CONV_A_HALO, CONV_A_CH), F32)

    pa = pa_ref[0].astype(F32)
    ext_ref[pl.ds(CONV_A_HALO, ts), :] = pa[:, :CONV_A_CH] * jax.nn.sigmoid(pa[:, CONV_A_CH:])
    first = CONV_A_HALO - (CONV_A_WIDTH - 1)
    for r0 in range(0, ts, CONV_A_SUB):
        acc = jnp.zeros((CONV_A_SUB, CONV_A_CH), F32) + cb_ref[...]
        for j in range(CONV_A_WIDTH):
            acc = acc + ext_ref[pl.ds(r0 + first + j, CONV_A_SUB), :] * cw_ref[pl.ds(j, 1), :]
        y_ref[pl.ds(r0, CONV_A_SUB), :] = acc
    ext_ref[pl.ds(0, CONV_A_HALO), :] = ext_ref[pl.ds(ts, CONV_A_HALO), :]
    y = y_ref[...]
    mu = jnp.mean(y, axis=-1, keepdims=True)
    d = y - mu
    var = jnp.mean(d * d, axis=-1, keepdims=True)
    yn = d * lax.rsqrt(var + CONV_A_LN_EPS) * lg_ref[...] + lb_ref[...]
    o = _bdot(jax.nn.silu(yn), pj_ref[...])
    o_ref[0] = jax.nn.sigmoid(pg_ref[0].astype(F32)) * o


def _conv_a(p, conv_w, conv_b, ln_g, ln_b, proj_bf):
    bsz, seq, _ = p.shape
    ts = min(seq, 512)
    row = lambda a: a.reshape(1, -1)
    full = lambda shape: pl.BlockSpec(shape, lambda b, i: (0,) * len(shape))
    return pl.pallas_call(
        _conv_a_kernel,
        out_shape=SDS((bsz, seq, D_MODEL), F32),
        grid=(bsz, seq // ts),
        in_specs=[pl.BlockSpec((1, ts, 2 * CONV_A_CH), lambda b, i: (b, i, SEG_A // (2 * CONV_A_CH))),
                  pl.BlockSpec((1, ts, D_MODEL), lambda b, i: (b, i, SEG_G // D_MODEL)),
                  full((CONV_A_WIDTH, CONV_A_CH)), full((1, CONV_A_CH)), full((1, CONV_A_CH)),
                  full((1, CONV_A_CH)), full((CONV_A_CH, D_MODEL))],
        out_specs=pl.BlockSpec((1, ts, D_MODEL), lambda b, i: (b, i, 0)),
        scratch_shapes=[pltpu.VMEM((ts + CONV_A_HALO, CONV_A_CH), F32),
                        pltpu.VMEM((ts, CONV_A_CH), F32)],
        compiler_params=_cparams(("arbitrary", "arbitrary")),
        name="conv_a",
    )(p, p, conv_w, row(conv_b), row(ln_g), row(ln_b), proj_bf)


def _lru_kernel(pc_ref, pg_ref, m_ref, cw_ref, cb_ref, wa_ref, ba_ref, wx_ref, bx_ref, lam_ref,
                pj_ref, o_ref, ext_ref, h_ref, a_ref, b_ref):
    ts = pc_ref.shape[1]
    groups = ts // SUBLANES

    @pl.when(pl.program_id(1) == 0)
    def _():
        ext_ref[pl.ds(0, SUBLANES), :] = jnp.zeros((SUBLANES, LRU_DIM), F32)
        h_ref[...] = jnp.zeros((SUBLANES, LRU_DIM), F32)

    pc = pc_ref[0].astype(F32)
    y_gate = jax.nn.gelu(pc[:, :LRU_DIM], approximate=True)
    ext_ref[pl.ds(SUBLANES, ts), :] = pc[:, LRU_DIM:]
    first = SUBLANES - (LRU_CONV - 1)
    xc = jnp.zeros((ts, LRU_DIM), F32) + cb_ref[...]
    for j in range(LRU_CONV):
        xc = xc + ext_ref[pl.ds(first + j, ts), :] * cw_ref[pl.ds(j, 1), :]
    ext_ref[pl.ds(0, SUBLANES), :] = ext_ref[pl.ds(ts, SUBLANES), :]

    def block_diag(w_ref):
        return jnp.concatenate(
            [_bdot(xc[:, h * LRU_BLOCK:(h + 1) * LRU_BLOCK], w_ref[h]) for h in range(LRU_HEADS)],
            axis=1)

    gate_a = jax.nn.sigmoid(block_diag(wa_ref) + ba_ref[...])
    gate_x = jax.nn.sigmoid(block_diag(wx_ref) + bx_ref[...])
    log_a = -LRU_C * gate_a * jax.nn.softplus(-lam_ref[...])
    a = jnp.exp(log_a)
    b = xc * gate_x * jnp.sqrt(1.0 - jnp.exp(2.0 * log_a))

    a3 = a.reshape(groups, SUBLANES, LRU_DIM)
    b3 = b.reshape(groups, SUBLANES, LRU_DIM)
    row = lax.broadcasted_iota(I32, (groups, SUBLANES, LRU_DIM), 1)
    for s in (1, 2, 4):
        keep = row >= s
        b3 = jnp.where(keep, a3 * pltpu.roll(b3, s, axis=1) + b3, b3)
        a3 = jnp.where(keep, a3 * pltpu.roll(a3, s, axis=1), a3)
    a_ref[...] = a3.reshape(ts, LRU_DIM)
    b_ref[...] = b3.reshape(ts, LRU_DIM)
    h = h_ref[...]
    for g in range(groups):
        rows = pl.ds(g * SUBLANES, SUBLANES)
        hg = a_ref[rows, :] * h + b_ref[rows, :]
        b_ref[rows, :] = hg
        h = jnp.broadcast_to(hg[SUBLANES - 1:SUBLANES, :], (SUBLANES, LRU_DIM))
    h_ref[...] = h
    o = _bdot(b_ref[...] * y_gate, pj_ref[...])
    o_ref[0] = m_ref[0] + jax.nn.sigmoid(pg_ref[0].astype(F32)) * o


def _lru(p, merged, conv_w, conv_b, wa_bf, ba, wx_bf, bx, lam, proj_bf):
    bsz, seq, _ = p.shape
    ts = min(seq, 256)
    row = lambda a: a.reshape(1, -1)
    full = lambda shape: pl.BlockSpec(shape, lambda b, i: (0,) * len(shape))
    return pl.pallas_call(
        _lru_kernel,
        out_shape=SDS((bsz, seq, D_MODEL), F32),
        grid=(bsz, seq // ts),
        in_specs=[pl.BlockSpec((1, ts, 2 * LRU_DIM), lambda b, i: (b, i, SEG_C // (2 * LRU_DIM))),
                  pl.BlockSpec((1, ts, D_MODEL), lambda b, i: (b, i, SEG_G // D_MODEL + 2)),
                  pl.BlockSpec((1, ts, D_MODEL), lambda b, i: (b, i, 0)),
                  full((LRU_CONV, LRU_DIM)), full((1, LRU_DIM)),
                  full((LRU_HEADS, LRU_BLOCK, LRU_BLOCK)), full((1, LRU_DIM)),
                  full((LRU_HEADS, LRU_BLOCK, LRU_BLOCK)), full((1, LRU_DIM)),
                  full((1, LRU_DIM)), full((LRU_DIM, D_MODEL))],
        out_specs=pl.BlockSpec((1, ts, D_MODEL), lambda b, i: (b, i, 0)),
        scratch_shapes=[pltpu.VMEM((ts + SUBLANES, LRU_DIM), F32),
                        pltpu.VMEM((SUBLANES, LRU_DIM), F32),
                        pltpu.VMEM((ts, LRU_DIM), F32),
                        pltpu.VMEM((ts, LRU_DIM), F32)],
        compiler_params=_cparams(("arbitrary", "arbitrary")),
        name="rg_lru",
    )(p, p, merged, conv_w, row(conv_b), wa_bf, row(ba), wx_bf, row(bx), row(lam), proj_bf)


def _rwkv_prep_kernel(pb_ref, mu_ref, w0_ref, wup_ref, a0_ref, aup_ref, gup_ref, kk_ref, ka_ref,
                      rk_ref, bd_ref, ltri_ref,
                      rt_ref, kkt_ref, kh_ref, bh_ref, v_ref, pinc_ref, bonus_ref, g_ref, ext_ref):
    ts = pb_ref.shape[1]

    @pl.when(pl.program_id(1) == 0)
    def _():
        ext_ref[pl.ds(0, SUBLANES), :] = jnp.zeros((SUBLANES, ext_ref.shape[1]), F32)

    p = pb_ref[0].astype(F32)
    ext_ref[pl.ds(SUBLANES, ts), :] = p
    prev = ext_ref[pl.ds(SUBLANES - 1, ts), :]
    ext_ref[pl.ds(0, SUBLANES), :] = ext_ref[pl.ds(ts, SUBLANES), :]
    pm = p + (prev - p) * mu_ref[...]
    r = pm[:, RW_R:RW_R + RWKV_DIM]
    k = pm[:, RW_K:RW_K + RWKV_DIM]
    v = pm[:, RW_V:RW_V + RWKV_DIM]
    xw = pm[:, RW_XW:RW_XW + LANES]
    xa = pm[:, RW_XA:RW_XA + LANES]
    xg = pm[:, RW_XG:RW_XG + LORA_G]
    w = -jax.nn.softplus(-(w0_ref[...] + _bdot(jnp.tanh(xw), wup_ref[...]))) - 0.5
    lw = -jnp.exp(w)
    a = jax.nn.sigmoid(a0_ref[...] + _bdot(xa, aup_ref[...]))
    g_ref[0] = _bdot(jax.nn.sigmoid(xg), gup_ref[...])
    kkr = k * kk_ref[...]
    ss = _head_sums(kkr * kkr, bd_ref[...])
    kk = kkr / jnp.maximum(jnp.sqrt(ss), 1e-12)
    k2 = k * (1.0 + (a - 1.0) * ka_ref[...])
    lw_hi = lw.astype(BF16)
    lw_mid, lw_lo = _split(lw - lw_hi.astype(F32))
    tri = ltri_ref[...]
    lcum = (jnp.dot(tri, lw_hi, preferred_element_type=F32) + jnp.dot(tri, lw_mid, preferred_element_type=F32)
            + jnp.dot(tri, lw_lo, preferred_element_type=F32))
    pinc = jnp.exp(lcum)
    pinv = jnp.exp(-lcum)
    rt_ref[0] = r * pinc
    kkt_ref[0] = kk * jnp.exp(lcum - lw)
    kh_ref[0] = k2 * pinv
    bh_ref[0] = kk * a * pinv
    v_ref[0] = v
    pinc_ref[0] = pinc
    bonus_ref[0] = _head_sums(r * k2 * rk_ref[...], bd_ref[...]) * v


def _dot3(a, b):
    d = lambda x, y: jnp.dot(x, y, preferred_element_type=F32)
    return d(a[0], b[0]) + d(a[0], b[1]) + d(a[1], b[0])


def _rwkv_scan_kernel(rt_ref, kkt_ref, kh_ref, bh_ref, v_ref, pinc_ref, y_ref, s_ref):
    c = RWKV_CHUNK
    n = RWKV_HEAD
    heads = range(RWKV_HEADS)

    @pl.when(pl.program_id(1) == 0)
    def _():
        s_ref[...] = jnp.zeros(s_ref.shape, F32)

    row = lax.broadcasted_iota(I32, (c, c), 0)
    col = lax.broadcasted_iota(I32, (c, c), 1)
    eye = (row == col).astype(F32)
    same16 = (row // 16) == (col // 16)
    same32 = (row // 32) == (col // 32)
    row2 = lax.broadcasted_iota(I32, (c, 2 * c), 0)
    col2 = lax.broadcasted_iota(I32, (c, 2 * c), 1) % c
    nt = lambda a, b: lax.dot_general(a, b, (((1,), (1,)), ((), ())), preferred_element_type=F32)
    tn = lambda a, b: lax.dot_general(a, b, (((0,), (0,)), ((), ())), preferred_element_type=F32)
    dot = lambda a, b: jnp.dot(a, b, preferred_element_type=F32)
    sl = [pl.ds(h * n, n) for h in heads]
    v = [v_ref[0, :, sl[h]] for h in heads]
    pc = [pinc_ref[0, pl.ds(c - 1, 1), sl[h]] for h in heads]
    s = [s_ref[:, sl[h]] for h in heads]
    lhs = [jnp.concatenate([kkt_ref[0, :, sl[h]], rt_ref[0, :, sl[h]]], axis=0) for h in heads]
    rhs = [jnp.concatenate([bh_ref[0, :, sl[h]], kh_ref[0, :, sl[h]]], axis=0) for h in heads]
    big = [nt(lhs[h], rhs[h]) for h in heads]
    from_state = [nt(lhs[h], s[h]) for h in heads]
    top = [jnp.where(row2 > col2, big[h][:c], 0.0) for h in heads]
    bot = [jnp.where(row2 >= col2, big[h][c:], 0.0) for h in heads]
    a_b = [top[h][:, :c] for h in heads]
    akv = [dot(top[h], jnp.concatenate([jnp.zeros((c, n), F32), v[h]], axis=0)) for h in heads]
    d16 = [jnp.where(same16, a_b[h], 0.0) for h in heads]
    sd = [_split(d16[h]) for h in heads]
    s2 = [_split(_dot3(sd[h], sd[h])) for h in heads]
    s4 = [_split(_dot3(s2[h], s2[h])) for h in heads]
    s8 = [_split(_dot3(s4[h], s4[h])) for h in heads]
    t = [eye - d16[h] for h in heads]
    for sp in (s2, s4, s8):
        t = [t[h] + _dot3(_split(t[h]), sp[h]) for h in heads]
    for off in ([jnp.where(same32 & jnp.logical_not(same16), a_b[h], 0.0) for h in heads],
                [jnp.where(same32, 0.0, a_b[h]) for h in heads]):
        st = [_split(t[h]) for h in heads]
        lt = [_split(_dot3(_split(off[h]), st[h])) for h in heads]
        t = [t[h] - _dot3(st[h], lt[h]) for h in heads]
    u = [dot(t[h], from_state[h][:c] + akv[h]) for h in heads]
    vu = [jnp.concatenate([-u[h], v[h]], axis=0) for h in heads]
    y = [from_state[h][c:] + dot(bot[h], vu[h]) for h in heads]
    s_new = [s[h] * pc[h] + tn(vu[h], rhs[h] * pc[h]) for h in heads]
    y_ref[0] = jnp.concatenate(y, axis=1)
    s_ref[...] = jnp.concatenate(s_new, axis=1)


def _rwkv_post_kernel(y_ref, bonus_ref, g_ref, pg_ref, m_ref, gg_ref, gb_ref, bdm_ref, pj_ref, o_ref):
    y = y_ref[0]
    mu = _head_sums(y, bdm_ref[...])
    d = y - mu
    var = _head_sums(d * d, bdm_ref[...])
    yn = d * lax.rsqrt(var + RWKV_GN_EPS) * gg_ref[...] + gb_ref[...] + bonus_ref[0]
    o = _bdot(yn * g_ref[0], pj_ref[...])
    o_ref[0] = m_ref[0] + jax.nn.sigmoid(pg_ref[0].astype(F32)) * o


def _rwkv(p, merged, mu_pad, w0, wup_pad, a0, aup_pad, g_up, k_k, k_a, r_k, gn_g, gn_b, proj_bf):
    bsz, seq, _ = p.shape
    row = lambda a: a.reshape(1, -1)
    full = lambda shape: pl.BlockSpec(shape, lambda b, i: (0,) * len(shape))
    head_id = jnp.arange(RWKV_DIM, dtype=I32) // RWKV_HEAD
    bd = (head_id[:, None] == head_id[None, :]).astype(BF16)

    ts = min(seq, 256)
    t_id = jnp.arange(ts, dtype=I32)
    ltri = ((t_id[:, None] // RWKV_CHUNK == t_id[None, :] // RWKV_CHUNK)
            & (t_id[:, None] >= t_id[None, :])).astype(BF16)
    seq_blk = lambda width: pl.BlockSpec((1, ts, width), lambda b, i: (b, i, 0))
    wide = SDS((bsz, seq, RWKV_DIM), F32)
    rt, kkt, kh, bh, v, pinc, bonus, g = pl.pallas_call(
        _rwkv_prep_kernel,
        out_shape=[wide] * 8,
        grid=(bsz, seq // ts),
        in_specs=[pl.BlockSpec((1, ts, 2048), lambda b, i: (b, i, SEG_B // 2048)),
                  full((1, 2048)), full((1, RWKV_DIM)), full((LANES, RWKV_DIM)),
                  full((1, RWKV_DIM)), full((LANES, RWKV_DIM)), full((LORA_G, RWKV_DIM)),
                  full((1, RWKV_DIM)), full((1, RWKV_DIM)), full((1, RWKV_DIM)),
                  full((RWKV_DIM, RWKV_DIM)), full((ts, ts))],
        out_specs=[seq_blk(RWKV_DIM)] * 8,
        scratch_shapes=[pltpu.VMEM((ts + SUBLANES, 2048), F32)],
        compiler_params=_cparams(("arbitrary", "arbitrary")),
        name="rwkv_prep",
    )(p, mu_pad, row(w0), wup_pad, row(a0), aup_pad, g_up, row(k_k), row(k_a), row(r_k), bd, ltri)

    c = RWKV_CHUNK
    chunk_blk = pl.BlockSpec((1, c, RWKV_DIM), lambda b, i: (b, i, 0))
    y = pl.pallas_call(
        _rwkv_scan_kernel,
        out_shape=wide,
        grid=(bsz, seq // c),
        in_specs=[chunk_blk] * 6,
        out_specs=chunk_blk,
        scratch_shapes=[pltpu.VMEM((RWKV_HEAD, RWKV_DIM), F32)],
        compiler_params=_cparams(("arbitrary", "arbitrary")),
        name="rwkv_scan",
    )(rt, kkt, kh, bh, v, pinc)

    tp = min(seq, 512)
    blk = lambda width: pl.BlockSpec((1, tp, width), lambda b, i: (b, i, 0))
    return pl.pallas_call(
        _rwkv_post_kernel,
        out_shape=SDS((bsz, seq, D_MODEL), F32),
        grid=(bsz, seq // tp),
        in_specs=[blk(RWKV_DIM), blk(RWKV_DIM), blk(RWKV_DIM),
                  pl.BlockSpec((1, tp, D_MODEL), lambda b, i: (b, i, SEG_G // D_MODEL + 1)),
                  blk(D_MODEL), full((1, RWKV_DIM)), full((1, RWKV_DIM)),
                  full((RWKV_DIM, RWKV_DIM)), full((RWKV_DIM, D_MODEL))],
        out_specs=blk(D_MODEL),
        compiler_params=_cparams(("arbitrary", "arbitrary")),
        name="rwkv_post",
    )(y, bonus, g, p, merged, row(gn_g), row(gn_b), bd * (1.0 / RWKV_HEAD), proj_bf)


def _out_kernel(m_ref, x_ref, g1_ref, w_ref, n2_ref, sc_ref, sh_ref, xo_ref, h_ref):
    xn = x_ref[0] + g1_ref[0] * _bdot(m_ref[0], w_ref[...])
    xo_ref[0] = xn
    h_ref[0] = _modulated_rmsnorm(xn, n2_ref[...], sc_ref[0], sh_ref[0])


def _out_proj(merged, x, g1, w_bf, norm2, sc2, sh2):
    bsz, seq, _ = x.shape
    ts = min(seq, 512)
    blk = pl.BlockSpec((1, ts, D_MODEL), lambda b, i: (b, i, 0))
    per_b = pl.BlockSpec((1, 1, D_MODEL), lambda b, i: (b, 0, 0))
    return pl.pallas_call(
        _out_kernel,
        out_shape=[SDS((bsz, seq, D_MODEL), F32)] * 2,
        grid=(bsz, seq // ts),
        in_specs=[blk, blk, per_b, pl.BlockSpec((D_MODEL, D_MODEL), lambda b, i: (0, 0)),
                  pl.BlockSpec((1, D_MODEL), lambda b, i: (0, 0)), per_b, per_b],
        out_specs=[blk, blk],
        compiler_params=_cparams(("arbitrary", "arbitrary")),
        name="out_proj",
    )(merged, x, g1, w_bf, norm2, sc2, sh2)


MOE_TILE = 256
MOE_SLOTS = 2560
MOE_CHUNK = 32
U32 = jnp.uint32


def _route_kernel(h_ref, rw_ref, bias_ref, upper_ref, ltri_ref, slot_ref, w_ref, n8_ref):
    tile = h_ref.shape[0]
    neg = -jnp.inf
    scores = jax.nn.sigmoid(_hdot_nt(rw_ref[...], h_ref[...]))
    s3 = scores.reshape(GROUP_SIZE, N_GROUPS, tile)
    b3 = s3 + bias_ref[...].reshape(GROUP_SIZE, N_GROUPS, tile)
    slab = lax.broadcasted_iota(I32, b3.shape, 0).astype(F32)
    grp = lax.broadcasted_iota(I32, b3.shape, 1).astype(F32)
    eid = grp * GROUP_SIZE + slab
    m1 = jnp.max(b3, axis=0, keepdims=True)
    first = jnp.min(jnp.where(b3 == m1, slab, GROUP_SIZE), axis=0, keepdims=True)
    m2 = jnp.max(jnp.where(slab == first, neg, b3), axis=0, keepdims=True)
    gs = (m1 + m2)[0]
    gi = lax.broadcasted_iota(I32, gs.shape, 0).astype(F32)
    chosen = jnp.zeros(gs.shape, F32)
    for _ in range(TOPK_GROUPS):
        m = jnp.max(gs, axis=0, keepdims=True)
        hit = gi == jnp.min(jnp.where(gs == m, gi, N_GROUPS), axis=0, keepdims=True)
        chosen = jnp.where(hit, 1.0, chosen)
        gs = jnp.where(hit, neg, gs)
    cur = jnp.where((chosen > 0.0)[None], b3, neg)
    ones_cols = jnp.ones((tile, LANES), BF16)
    lanes_of = lambda a: jnp.concatenate([a] * (tile // LANES), axis=1)
    to3 = lambda a: a.reshape(GROUP_SIZE, N_GROUPS, tile)
    fold = lambda a: jnp.sum(jnp.sum(a, axis=0), axis=0, keepdims=True)
    base = jnp.zeros((N_EXPERTS, LANES), F32)
    picks, w_rows, rank_rows = [], [], []
    for _ in range(TOP_K):
        m = jnp.max(jnp.max(cur, axis=0), axis=0, keepdims=True)[None]
        pick = jnp.min(jnp.min(jnp.where(cur == m, eid, N_EXPERTS), axis=0), axis=0, keepdims=True)
        hit = eid == pick[None]
        w_rows.append(fold(jnp.where(hit, s3, 0.0)))
        picks.append(pick)
        cur = jnp.where(hit, neg, cur)
        onehot = hit.astype(BF16).reshape(N_EXPERTS, tile)
        before = jnp.dot(onehot, upper_ref[...], preferred_element_type=F32)
        rank_rows.append(fold(jnp.where(hit, to3(before + lanes_of(base)), 0.0)))
        base = base + jnp.dot(onehot, ones_cols, preferred_element_type=F32)
    n8 = jnp.floor((base + (SUBLANES - 1.0)) * (1.0 / SUBLANES)) * SUBLANES
    run_start = to3(lanes_of(_hdot(ltri_ref[...], n8)))
    slots = [rank_rows[j] + fold(jnp.where(eid == picks[j][None], run_start, 0.0)) for j in range(TOP_K)]
    n8_ref[0] = n8
    w_all = jnp.concatenate(w_rows, axis=0)
    w_ref[...] = w_all / jnp.sum(w_all, axis=0, keepdims=True) * ROUTED_SCALE
    slot_ref[...] = jnp.concatenate(slots, axis=0).astype(I32)


def _route(h2, router_w, router_bias):
    n_tok = h2.shape[0]
    tile = MOE_TILE
    n_tiles = n_tok // tile
    regroup = lambda a: a.reshape(N_GROUPS, GROUP_SIZE, -1).transpose(1, 0, 2).reshape(N_EXPERTS, -1)
    rw = regroup(router_w.T)
    bias = jnp.broadcast_to(regroup(router_bias.reshape(N_EXPERTS, 1)), (N_EXPERTS, tile))
    t_id = jnp.arange(tile, dtype=I32)
    upper = (t_id[:, None] < t_id[None, :]).astype(BF16)
    e_id = jnp.arange(N_EXPERTS, dtype=I32)
    ltri = (e_id[:, None] > e_id[None, :]).astype(F32)
    tok_blk = pl.BlockSpec((TOP_K, tile), lambda i: (0, i))
    full = lambda shape: pl.BlockSpec(shape, lambda i: (0,) * len(shape))
    slot_t, w_t, n8 = pl.pallas_call(
        _route_kernel,
        out_shape=[SDS((TOP_K, n_tok), I32), SDS((TOP_K, n_tok), F32),
                   SDS((n_tiles, N_EXPERTS, LANES), F32)],
        grid=(n_tiles,),
        in_specs=[pl.BlockSpec((tile, D_MODEL), lambda i: (i, 0)),
                  full((N_EXPERTS, D_MODEL)), full((N_EXPERTS, tile)), full((tile, tile)),
                  full((N_EXPERTS, N_EXPERTS))],
        out_specs=[tok_blk, tok_blk, pl.BlockSpec((1, N_EXPERTS, LANES), lambda i: (i, 0, 0))],
        compiler_params=_cparams(("arbitrary",)),
        name="moe_route",
    )(h2, rw, bias, upper, ltri)
    return slot_t, w_t, n8[:, :, 0].astype(I32)


def _for_each_run_piece(tile_idx, n8_ref, fn):
    def per_expert(r, carry):
        n = n8_ref[tile_idx * N_EXPERTS + r]
        whole = n // MOE_CHUNK

        def chunk(k, c):
            fn(r, k * MOE_CHUNK, MOE_CHUNK)
            return c
        lax.fori_loop(0, whole, chunk, 0)
        rest = whole * MOE_CHUNK
        for size in (16, 8):
            @pl.when((n & size) != 0)
            def _(size=size):
                fn(r, rest + (n & (MOE_CHUNK - 1) & ~(2 * size - 1)), size)
        return carry
    lax.fori_loop(0, N_EXPERTS, per_expert, 0)


def _wait_rows(total, make_wait):
    size = SUBLANES
    while size <= MOE_SLOTS:
        @pl.when((total & size) != 0)
        def _(size=size):
            make_wait(size).wait()
        size *= 2


def _pack_pairs(hi_bits, lo_bits):
    return (hi_bits & jnp.uint32(0xFFFF0000)) | (lo_bits >> 16)


def _unpack_pairs(u):
    hi = lax.bitcast_convert_type(u & jnp.uint32(0xFFFF0000), F32)
    lo = lax.bitcast_convert_type(u << 16, F32)
    return jnp.concatenate([hi, lo], axis=1).astype(BF16)


def _dispatch_kernel(n8_ref, off_ref, dst_ref, tot_ref, pad_end_ref, slot_ref, h_ref, xs_ref,
                     g_ref, zero_ref, sem):
    i = pl.program_id(0)
    tile = h_ref.shape[0]
    half = D_MODEL // 2

    @pl.when(i == 0)
    def _():
        zero_ref[...] = jnp.zeros(zero_ref.shape, U32)

        def last_block(e):
            start = pl.multiple_of(jnp.maximum(pad_end_ref[e] - MOE_ROWS, 0), MOE_ROWS)
            return pltpu.make_async_copy(zero_ref, xs_ref.at[pl.ds(start, MOE_ROWS), :], sem)

        def has_rows(e):
            prev = jnp.where(e > 0, pad_end_ref[jnp.maximum(e - 1, 0)], 0)
            return pad_end_ref[e] > prev

        def clear(e, carry):
            @pl.when(has_rows(e))
            def _():
                last_block(e).start()
            return carry
        lax.fori_loop(0, N_EXPERTS, clear, 0)

        def done(e, carry):
            @pl.when(has_rows(e))
            def _():
                last_block(e).wait()
            return carry
        lax.fori_loop(0, N_EXPERTS, done, 0)

        def tail_block(b):
            start = pl.multiple_of(b * MOE_ROWS, MOE_ROWS)
            return pltpu.make_async_copy(zero_ref, xs_ref.at[pl.ds(start, MOE_ROWS), :], sem)

        def clear_tail(b, carry):
            tail_block(b).start()
            return carry

        def done_tail(b, carry):
            tail_block(b).wait()
            return carry
        used = pad_end_ref[N_EXPERTS - 1] // MOE_ROWS
        lax.fori_loop(used, xs_ref.shape[0] // MOE_ROWS, clear_tail, 0)
        lax.fori_loop(used, xs_ref.shape[0] // MOE_ROWS, done_tail, 0)

    slot_id = lax.broadcasted_iota(I32, (MOE_SLOTS, tile), 0)
    sel = jnp.zeros((MOE_SLOTS, tile), F32)
    for j in range(TOP_K):
        sel = jnp.where(slot_id == slot_ref[pl.ds(j, 1), :], 1.0, sel)
    g = jnp.dot(sel.astype(BF16), h_ref[...].astype(BF16), preferred_element_type=F32)
    bits = lax.bitcast_convert_type(g, U32)
    g_ref[...] = _pack_pairs(bits[:, :half], bits[:, half:])

    def piece(r, offset, rows):
        src = pl.multiple_of(off_ref[i * N_EXPERTS + r] + offset, SUBLANES)
        dst = pl.multiple_of(dst_ref[i * N_EXPERTS + r] + offset, SUBLANES)
        pltpu.make_async_copy(g_ref.at[pl.ds(src, rows), :], xs_ref.at[pl.ds(dst, rows), :], sem).start()

    _for_each_run_piece(i, n8_ref, piece)
    _wait_rows(tot_ref[i], lambda rows: pltpu.make_async_copy(
        g_ref.at[pl.ds(0, rows), :], xs_ref.at[pl.ds(0, rows), :], sem))


def _dispatch(h2, slot_t, n8_flat, off_flat, dst_flat, tot, pad_end, n_rows):
    n_tok = h2.shape[0]
    tile = MOE_TILE
    grid_spec = pltpu.PrefetchScalarGridSpec(
        num_scalar_prefetch=5,
        grid=(n_tok // tile,),
        in_specs=[pl.BlockSpec((TOP_K, tile), lambda i, *_: (0, i)),
                  pl.BlockSpec((tile, D_MODEL), lambda i, *_: (i, 0))],
        out_specs=pl.BlockSpec(memory_space=pl.ANY),
        scratch_shapes=[pltpu.VMEM((MOE_SLOTS, D_MODEL // 2), U32),
                        pltpu.VMEM((MOE_ROWS, D_MODEL // 2), U32), pltpu.SemaphoreType.DMA],
    )
    return pl.pallas_call(
        _dispatch_kernel,
        out_shape=SDS((n_rows, D_MODEL // 2), U32),
        grid_spec=grid_spec,
        compiler_params=_cparams(("arbitrary",)),
        name="moe_dispatch",
    )(n8_flat, off_flat, dst_flat, tot, pad_end, slot_t, h2)


def _expert_kernel(be_ref, used_ref, xs_ref, w1_ref, w3_ref, w2_ref, y_ref):
    half = D_MODEL // 2

    @pl.when(pl.program_id(0) < used_ref[0])
    def _():
        x = _unpack_pairs(xs_ref[...])
        hid = jax.nn.silu(_bdot(x, w1_ref[0])) * _bdot(x, w3_ref[0])
        y = _bdot(hid, w2_ref[0]).astype(BF16).astype(F32)
        bits = lax.bitcast_convert_type(y, U32)
        y_ref[...] = _pack_pairs(bits[:, :half], bits[:, half:])

    @pl.when(pl.program_id(0) >= used_ref[0])
    def _():
        y_ref[...] = jnp.zeros(y_ref.shape, U32)


def _experts(xs, block_exp, used, w1, w3, w2):
    n_rows = xs.shape[0]
    rows = MOE_ROWS
    grid_spec = pltpu.PrefetchScalarGridSpec(
        num_scalar_prefetch=2,
        grid=(n_rows // rows,),
        in_specs=[pl.BlockSpec((rows, D_MODEL // 2), lambda b, be, used: (b, 0)),
                  pl.BlockSpec((1, D_MODEL, EXPERT_FF), lambda b, be, used: (be[b], 0, 0)),
                  pl.BlockSpec((1, D_MODEL, EXPERT_FF), lambda b, be, used: (be[b], 0, 0)),
                  pl.BlockSpec((1, EXPERT_FF, D_MODEL), lambda b, be, used: (be[b], 0, 0))],
        out_specs=pl.BlockSpec((rows, D_MODEL // 2), lambda b, be, used: (b, 0)),
    )
    return pl.pallas_call(
        _expert_kernel,
        out_shape=SDS((n_rows, D_MODEL // 2), U32),
        grid_spec=grid_spec,
        compiler_params=_cparams(("arbitrary",)),
        name="moe_experts",
    )(block_exp, used, xs, w1, w3, w2)


def _combine_kernel(n8_ref, off_ref, src_ref, tot_ref, y_ref, slot_ref, w_ref, h_ref, x_ref, g2_ref,
                    s1_ref, s3_ref, s2_ref, fn_ref, o_ref, yt_ref, sem, *, final):
    tile = h_ref.shape[1]
    i = pl.program_id(0) * pl.num_programs(1) + pl.program_id(1)

    @pl.when(i == 0)
    def _():
        yt_ref[...] = jnp.zeros(yt_ref.shape, U32)

    def piece(r, offset, rows):
        src = pl.multiple_of(src_ref[i * N_EXPERTS + r] + offset, SUBLANES)
        dst = pl.multiple_of(off_ref[i * N_EXPERTS + r] + offset, SUBLANES)
        pltpu.make_async_copy(y_ref.at[pl.ds(src, rows), :], yt_ref.at[pl.ds(dst, rows), :], sem).start()

    _for_each_run_piece(i, n8_ref, piece)
    h = h_ref[0]
    shared = _bdot(jax.nn.silu(_bdot(h, s1_ref[...])) * _bdot(h, s3_ref[...]), s2_ref[...])
    slot_id = lax.broadcasted_iota(I32, (tile, MOE_SLOTS), 1)
    slots = slot_ref[0]
    w = w_ref[0]
    pw = jnp.zeros((tile, MOE_SLOTS), F32)
    for j in range(TOP_K):
        pw = jnp.where(slot_id == slots[:, j:j + 1], w[:, j:j + 1], pw)
    _wait_rows(tot_ref[i], lambda rows: pltpu.make_async_copy(
        y_ref.at[pl.ds(0, rows), :], yt_ref.at[pl.ds(0, rows), :], sem))
    routed = jnp.dot(pw.astype(BF16), _unpack_pairs(yt_ref[...]), preferred_element_type=F32)
    xn = x_ref[0] + g2_ref[0] * (routed + shared)
    if final:
        xn = xn * lax.rsqrt(jnp.mean(xn * xn, axis=-1, keepdims=True) + NORM_EPS) * fn_ref[...]
    o_ref[0] = xn


def _combine(y, slot_nat, w_nat, n8_flat, off_flat, dst_flat, tot, h2, x, g2, s1_bf, s3_bf, s2_bf,
             final_norm, final):
    bsz, seq, _ = x.shape
    tile = MOE_TILE
    per_seq = seq // tile
    blk = pl.BlockSpec((1, tile, D_MODEL), lambda b, i, *_: (b, i, 0))
    tok = pl.BlockSpec((1, tile, TOP_K), lambda b, i, *_: (b, i, 0))
    full = lambda shape: pl.BlockSpec(shape, lambda b, i, *_: (0,) * len(shape))
    grid_spec = pltpu.PrefetchScalarGridSpec(
        num_scalar_prefetch=4,
        grid=(bsz, per_seq),
        in_specs=[pl.BlockSpec(memory_space=pl.ANY), tok, tok, blk, blk,
                  pl.BlockSpec((1, 1, D_MODEL), lambda b, i, *_: (b, 0, 0)),
                  full((D_MODEL, SHARED_FF)), full((D_MODEL, SHARED_FF)), full((SHARED_FF, D_MODEL)),
                  full((1, D_MODEL))],
        out_specs=blk,
        scratch_shapes=[pltpu.VMEM((MOE_SLOTS, D_MODEL // 2), U32), pltpu.SemaphoreType.DMA],
    )
    return pl.pallas_call(
        functools.partial(_combine_kernel, final=final),
        out_shape=SDS((bsz, seq, D_MODEL), F32),
        grid_spec=grid_spec,
        compiler_params=_cparams(("arbitrary", "arbitrary")),
        name="moe_combine",
    )(n8_flat, off_flat, dst_flat, tot, y, slot_nat, w_nat, h2, x, g2, s1_bf, s3_bf, s2_bf, final_norm)


def _moe(x, h2, g2, router_w, router_bias, w1, w3, w2, s1_bf, s3_bf, s2_bf, final_norm, final):
    bsz, seq, _ = x.shape
    assert seq % MOE_TILE == 0, "token tiles must not straddle sequences"
    n_tok = bsz * seq
    slot_t, w_t, n8 = _route(h2.reshape(n_tok, D_MODEL), router_w, router_bias)
    n_tiles = n8.shape[0]
    counts = jnp.sum(n8, axis=0)
    padded = (counts + MOE_ROWS - 1) // MOE_ROWS * MOE_ROWS
    pad_end = jnp.cumsum(padded).astype(I32)
    pad_start = pad_end - padded
    run_row = (pad_start[None, :] + jnp.cumsum(n8, axis=0) - n8).astype(I32)
    run_slot = (jnp.cumsum(n8, axis=1) - n8).astype(I32)
    tot = jnp.sum(n8, axis=1).astype(I32)
    max_rows = n_tok * TOP_K + n_tiles * N_EXPERTS * (SUBLANES - 1) + N_EXPERTS * (MOE_ROWS - 1)
    n_blocks = (max_rows + MOE_ROWS - 1) // MOE_ROWS
    block_first = jnp.arange(n_blocks, dtype=I32) * MOE_ROWS
    block_row = jnp.minimum(jnp.sum((pad_end[None, :] <= block_first[:, None]).astype(I32), axis=1),
                            N_EXPERTS - 1)
    block_exp = ((block_row % N_GROUPS) * GROUP_SIZE + block_row // N_GROUPS).astype(I32)
    used = (pad_end[-1:] // MOE_ROWS).astype(I32)
    flat = lambda a: a.reshape(-1).astype(I32)
    xs = _dispatch(h2.reshape(n_tok, D_MODEL), slot_t, flat(n8), flat(run_slot), flat(run_row), tot,
                   pad_end, n_blocks * MOE_ROWS)
    y = _experts(xs, block_exp, used, w1, w3, w2)
    nat = lambda a: a.T.reshape(bsz, seq, TOP_K)
    return _combine(y, nat(slot_t), nat(w_t), flat(n8), flat(run_slot), flat(run_row), tot, h2, x, g2,
                    s1_bf, s3_bf, s2_bf, final_norm, final)


def _pad_cols(a, width):
    return jnp.pad(a, ((0, 0), (0, width - a.shape[1])))


def _layout_w_in(w_in):
    b0 = COLS_A
    c0 = COLS_A + COLS_B
    g0 = c0 + COLS_C
    rkv = 3 * RWKV_DIM
    seg_b = jnp.concatenate([
        w_in[:, b0:b0 + rkv],
        _pad_cols(w_in[:, b0 + rkv:b0 + rkv + LORA_W], LANES),
        _pad_cols(w_in[:, b0 + rkv + LORA_W:b0 + rkv + LORA_W + LORA_A], LANES),
        _pad_cols(w_in[:, b0 + rkv + LORA_W + LORA_A:c0], 2 * LANES)], axis=1)
    return jnp.concatenate([w_in[:, c0:g0], seg_b, w_in[:, g0:], w_in[:, :b0]], axis=1).astype(BF16)


def _layout_mu(mu):
    rkv = 3 * RWKV_DIM
    m = mu.reshape(1, -1)
    return jnp.concatenate([
        m[:, :rkv], _pad_cols(m[:, rkv:rkv + LORA_W], LANES),
        _pad_cols(m[:, rkv + LORA_W:rkv + LORA_W + LORA_A], LANES),
        _pad_cols(m[:, rkv + LORA_W + LORA_A:], 2 * LANES)], axis=1)


def _pad_rows(a, height):
    return jnp.pad(a, ((0, height - a.shape[0]), (0, 0)))


def kernel(x, c, ada_w, ada_b, norm1, norm2, w_in, conv_a_w, conv_a_b, ln_a_g, ln_a_b, proj_a, mu_b, w0, w_up, a0, a_up, g_up, k_k, k_a, r_k, gn_b_g, gn_b_b, proj_b, conv_c_w, conv_c_b, lru_wa, lru_ba, lru_wx, lru_bx, lru_lambda, proj_c, w_out, router_w, router_bias, exp_w1, exp_w3, exp_w2, sh_w1, sh_w3, sh_w2, final_norm):
    depth = ada_w.shape[0]
    bsz = x.shape[0]
    mod = _ada_mod(c, ada_w, ada_b)
    for l in range(depth):
        sh1, sc1, g1, sh2, sc2, g2 = [mod[l, :, i * D_MODEL:(i + 1) * D_MODEL].reshape(bsz, 1, D_MODEL)
                                      for i in range(N_MOD)]
        p = _in_proj(x, sc1, sh1, norm1[l].reshape(1, -1), _layout_w_in(w_in[l]))
        merged = _conv_a(p, conv_a_w[l], conv_a_b[l], ln_a_g[l], ln_a_b[l], proj_a[l].astype(BF16))
        merged = _rwkv(p, merged, _layout_mu(mu_b[l]), w0[l], _pad_rows(w_up[l], LANES).astype(BF16),
                       a0[l], _pad_rows(a_up[l], LANES).astype(BF16), g_up[l].astype(BF16),
                       k_k[l], k_a[l], r_k[l], gn_b_g[l], gn_b_b[l], proj_b[l].astype(BF16))
        merged = _lru(p, merged, conv_c_w[l], conv_c_b[l], lru_wa[l].astype(BF16), lru_ba[l],
                      lru_wx[l].astype(BF16), lru_bx[l], lru_lambda[l], proj_c[l].astype(BF16))
        x, h2 = _out_proj(merged, x, g1, w_out[l].astype(BF16), norm2[l].reshape(1, -1), sc2, sh2)
        x = _moe(x, h2, g2, router_w[l], router_bias[l], exp_w1[l], exp_w3[l], exp_w2[l],
                 sh_w1[l].astype(BF16), sh_w3[l].astype(BF16), sh_w2[l].astype(BF16),
                 final_norm.reshape(1, -1), final=(l == depth - 1))
    return x
```

```python
import functools

import jax
import jax.numpy as jnp
from jax import lax
from jax.experimental import pallas as pl
from jax.experimental.pallas import tpu as pltpu

F32 = jnp.float32
BF16 = jnp.bfloat16
I32 = jnp.int32
SDS = jax.ShapeDtypeStruct
HIGHEST = lax.Precision.HIGHEST

D_MODEL = 1024
N_MOD = 6
NORM_EPS = 1e-6
CONV_A_CH = 512
CONV_A_WIDTH = 31
CONV_A_LN_EPS = 1e-5
RWKV_HEADS = 8
RWKV_HEAD = 64
RWKV_DIM = RWKV_HEADS * RWKV_HEAD
LORA_W = 64
LORA_A = 64
LORA_G = 128
RWKV_GN_EPS = 64e-5
RWKV_CHUNK = 64
LRU_DIM = 1024
LRU_HEADS = 8
LRU_BLOCK = LRU_DIM // LRU_HEADS
LRU_CONV = 4
LRU_C = 8.0
N_EXPERTS = 64
TOP_K = 8
N_GROUPS = 8
GROUP_SIZE = N_EXPERTS // N_GROUPS
TOPK_GROUPS = 4
EXPERT_FF = 256
SHARED_FF = 256
ROUTED_SCALE = 2.5
SEG_C = 0
SEG_B = 2048
SEG_G = 4096
SEG_A = 7168
IN_COLS_PAD = 8192
RW_R, RW_K, RW_V, RW_XWA, RW_XG = 0, 512, 1024, 1536, 1664
COLS_A = 2 * CONV_A_CH
COLS_B = 3 * RWKV_DIM + LORA_W + LORA_A + LORA_G
COLS_C = 2 * LRU_DIM
VMEM_LIMIT = 56 * 1024 * 1024
SUBLANES = 8
LANES = 128
MOE_ROWS = 512


def _cparams(sem):
    return pltpu.CompilerParams(dimension_semantics=sem, vmem_limit_bytes=VMEM_LIMIT)


def _bdot(a, b):
    return jnp.dot(a.astype(BF16), b.astype(BF16), preferred_element_type=F32)


def _hdot(a, b):
    return jnp.dot(a, b, preferred_element_type=F32, precision=HIGHEST)


def _split(a):
    hi = a.astype(BF16)
    return hi, (a - hi.astype(F32)).astype(BF16)


def _head_sums(a, ones_bf):
    hi, lo = _split(a)
    return (jnp.dot(hi, ones_bf, preferred_element_type=F32)
            + jnp.dot(lo, ones_bf, preferred_element_type=F32))


def _hdot_nt(a, b):
    return lax.dot_general(a, b, (((1,), (1,)), ((), ())), preferred_element_type=F32,
                           precision=HIGHEST)


def _hdot_tn(a, b):
    return lax.dot_general(a, b, (((0,), (0,)), ((), ())), preferred_element_type=F32,
                           precision=HIGHEST)


def _ada_kernel(c_ref, w_ref, b_ref, o_ref):
    cond = jax.nn.silu(c_ref[...])
    o_ref[0] = _bdot(cond, w_ref[0]) + b_ref[0]


def _ada_mod(c, ada_w, ada_b):
    depth, _, n = ada_w.shape
    bsz = c.shape[0]
    tn = 1536
    return pl.pallas_call(
        _ada_kernel,
        out_shape=SDS((depth, bsz, n), F32),
        grid=(depth, n // tn),
        in_specs=[pl.BlockSpec((bsz, D_MODEL), lambda l, j: (0, 0)),
                  pl.BlockSpec((1, D_MODEL, tn), lambda l, j: (l, 0, j)),
                  pl.BlockSpec((1, 1, tn), lambda l, j: (l, 0, j))],
        out_specs=pl.BlockSpec((1, bsz, tn), lambda l, j: (l, 0, j)),
        compiler_params=_cparams(("arbitrary", "arbitrary")),
        name="ada_mod",
    )(c, ada_w, ada_b.reshape(depth, 1, n))


def _modulated_rmsnorm(x, g, sc, sh):
    y = x * lax.rsqrt(jnp.mean(x * x, axis=-1, keepdims=True) + NORM_EPS)
    return (y * g) * (1.0 + sc) + sh


def _in_kernel(x_ref, sc_ref, sh_ref, g_ref, w_ref, o_ref, h_ref):
    @pl.when(pl.program_id(2) == 0)
    def _():
        h_ref[...] = _modulated_rmsnorm(x_ref[0], g_ref[...], sc_ref[0], sh_ref[0]).astype(BF16)

    o_ref[0] = jnp.dot(h_ref[...], w_ref[...], preferred_element_type=F32).astype(o_ref.dtype)


def _in_proj(x, sc, sh, g, w_pad):
    bsz, seq, _ = x.shape
    ts = min(seq, 1024)
    tn = 1024
    return pl.pallas_call(
        _in_kernel,
        out_shape=SDS((bsz, seq, IN_COLS_PAD), BF16),
        grid=(bsz, seq // ts, IN_COLS_PAD // tn),
        in_specs=[pl.BlockSpec((1, ts, D_MODEL), lambda b, i, j: (b, i, 0)),
                  pl.BlockSpec((1, 1, D_MODEL), lambda b, i, j: (b, 0, 0)),
                  pl.BlockSpec((1, 1, D_MODEL), lambda b, i, j: (b, 0, 0)),
                  pl.BlockSpec((1, D_MODEL), lambda b, i, j: (0, 0)),
                  pl.BlockSpec((D_MODEL, tn), lambda b, i, j: (0, j))],
        out_specs=pl.BlockSpec((1, ts, tn), lambda b, i, j: (b, i, j)),
        scratch_shapes=[pltpu.VMEM((ts, D_MODEL), BF16)],
        compiler_params=_cparams(("arbitrary", "arbitrary", "arbitrary")),
        name="in_proj",
    )(x, sc, sh, g, w_pad)


CONV_A_HALO = 32
CONV_A_SUB = 64


def _conv_a_kernel(pa_ref, pg_ref, cw_ref, cb_ref, lg_ref, lb_ref, pj_ref, o_ref, ext_ref, sh_ref, y_ref):
    ts = pa_ref.shape[1]

    @pl.when(pl.program_id(1) == 0)
    def _():
        ext_ref[pl.ds(0, CONV_A_HALO), :] = jnp.zeros((CONV_A_HALO, CONV_A_CH), F32)

    pa = pa_ref[0].astype(F32)
    ext_ref[pl.ds(CONV_A_HALO, ts), :] = pa[:, :CONV_A_CH] * jax.nn.sigmoid(pa[:, CONV_A_CH:])
    for p in range(1, SUBLANES):
        sh_ref[p - 1] = ext_ref[pl.ds(p, sh_ref.shape[1]), :]
    first = CONV_A_HALO - (CONV_A_WIDTH - 1)
    for r0 in range(0, ts, CONV_A_SUB):
        acc = jnp.zeros((CONV_A_SUB, CONV_A_CH), F32) + cb_ref[...]
        for j in range(CONV_A_WIDTH):
            phase = (first + j) % SUBLANES
            rows = pl.ds(r0 + first + j - phase, CONV_A_SUB)
            tap = ext_ref[rows, :] if phase == 0 else sh_ref[phase - 1, rows, :]
            acc = acc + tap * cw_ref[pl.ds(j, 1), :]
        y_ref[pl.ds(r0, CONV_A_SUB), :] = acc
    ext_ref[pl.ds(0, CONV_A_HALO), :] = ext_ref[pl.ds(ts, CONV_A_HALO), :]
    y = y_ref[...]
    mu = jnp.mean(y, axis=-1, keepdims=True)
    d = y - mu
    var = jnp.mean(d * d, axis=-1, keepdims=True)
    yn = d * lax.rsqrt(var + CONV_A_LN_EPS) * lg_ref[...] + lb_ref[...]
    o = _bdot(jax.nn.silu(yn), pj_ref[...])
    o_ref[0] = jax.nn.sigmoid(pg_ref[0].astype(F32)) * o


def _conv_a(p, conv_w, conv_b, ln_g, ln_b, proj_bf):
    bsz, seq, _ = p.shape
    ts = min(seq, 512)
    row = lambda a: a.reshape(1, -1)
    full = lambda shape: pl.BlockSpec(shape, lambda b, i: (0,) * len(shape))
    return pl.pallas_call(
        _conv_a_kernel,
        out_shape=SDS((bsz, seq, D_MODEL), F32),
        grid=(bsz, seq // ts),
        in_specs=[pl.BlockSpec((1, ts, 2 * CONV_A_CH), lambda b, i: (b, i, SEG_A // (2 * CONV_A_CH))),
                  pl.BlockSpec((1, ts, D_MODEL), lambda b, i: (b, i, SEG_G // D_MODEL)),
                  full((CONV_A_WIDTH, CONV_A_CH)), full((1, CONV_A_CH)), full((1, CONV_A_CH)),
                  full((1, CONV_A_CH)), full((CONV_A_CH, D_MODEL))],
        out_specs=pl.BlockSpec((1, ts, D_MODEL), lambda b, i: (b, i, 0)),
        scratch_shapes=[pltpu.VMEM((ts + CONV_A_HALO, CONV_A_CH), F32),
                        pltpu.VMEM((SUBLANES - 1, ts + CONV_A_HALO - SUBLANES, CONV_A_CH), F32),
                        pltpu.VMEM((ts, CONV_A_CH), F32)],
        compiler_params=_cparams(("arbitrary", "arbitrary")),
        name="conv_a",
    )(p, p, conv_w, row(conv_b), row(ln_g), row(ln_b), proj_bf)


def _lru_kernel(pc_ref, pg_ref, m_ref, cw_ref, cb_ref, wa_ref, ba_ref, wx_ref, bx_ref, lam_ref,
                pj_ref, o_ref, ext_ref, h_ref, a_ref, b_ref):
    ts = pc_ref.shape[1]
    groups = ts // SUBLANES

    @pl.when(pl.program_id(1) == 0)
    def _():
        ext_ref[pl.ds(0, SUBLANES), :] = jnp.zeros((SUBLANES, LRU_DIM), F32)
        h_ref[...] = jnp.zeros((SUBLANES, LRU_DIM), F32)

    pc = pc_ref[0].astype(F32)
    y_gate = jax.nn.gelu(pc[:, :LRU_DIM], approximate=True)
    ext_ref[pl.ds(SUBLANES, ts), :] = pc[:, LRU_DIM:]
    first = SUBLANES - (LRU_CONV - 1)
    xc = jnp.zeros((ts, LRU_DIM), F32) + cb_ref[...]
    for j in range(LRU_CONV):
        xc = xc + ext_ref[pl.ds(first + j, ts), :] * cw_ref[pl.ds(j, 1), :]
    ext_ref[pl.ds(0, SUBLANES), :] = ext_ref[pl.ds(ts, SUBLANES), :]

    def block_diag(w_ref):
        return jnp.concatenate(
            [_bdot(xc[:, h * LRU_BLOCK:(h + 1) * LRU_BLOCK], w_ref[h]) for h in range(LRU_HEADS)],
            axis=1)

    gate_a = jax.nn.sigmoid(block_diag(wa_ref) + ba_ref[...])
    gate_x = jax.nn.sigmoid(block_diag(wx_ref) + bx_ref[...])
    log_a = -LRU_C * gate_a * jax.nn.softplus(-lam_ref[...])
    a = jnp.exp(log_a)
    b = xc * gate_x * jnp.sqrt(1.0 - jnp.exp(2.0 * log_a))

    a3 = a.reshape(groups, SUBLANES, LRU_DIM)
    b3 = b.reshape(groups, SUBLANES, LRU_DIM)
    row = lax.broadcasted_iota(I32, (groups, SUBLANES, LRU_DIM), 1)
    for s in (1, 2, 4):
        keep = row >= s
        b3 = jnp.where(keep, a3 * pltpu.roll(b3, s, axis=1) + b3, b3)
        a3 = jnp.where(keep, a3 * pltpu.roll(a3, s, axis=1), a3)
    a_ref[...] = a3.reshape(ts, LRU_DIM)
    b_ref[...] = b3.reshape(ts, LRU_DIM)
    h = h_ref[...]
    for g in range(groups):
        rows = pl.ds(g * SUBLANES, SUBLANES)
        hg = a_ref[rows, :] * h + b_ref[rows, :]
        b_ref[rows, :] = hg
        h = jnp.broadcast_to(hg[SUBLANES - 1:SUBLANES, :], (SUBLANES, LRU_DIM))
    h_ref[...] = h
    o = _bdot(b_ref[...] * y_gate, pj_ref[...])
    o_ref[0] = m_ref[0] + jax.nn.sigmoid(pg_ref[0].astype(F32)) * o


def _lru(p, merged, conv_w, conv_b, wa_bf, ba, wx_bf, bx, lam, proj_bf):
    bsz, seq, _ = p.shape
    ts = min(seq, 256)
    row = lambda a: a.reshape(1, -1)
    full = lambda shape: pl.BlockSpec(shape, lambda b, i: (0,) * len(shape))
    return pl.pallas_call(
        _lru_kernel,
        out_shape=SDS((bsz, seq, D_MODEL), F32),
        grid=(bsz, seq // ts),
        in_specs=[pl.BlockSpec((1, ts, 2 * LRU_DIM), lambda b, i: (b, i, SEG_C // (2 * LRU_DIM))),
                  pl.BlockSpec((1, ts, D_MODEL), lambda b, i: (b, i, SEG_G // D_MODEL + 2)),
                  pl.BlockSpec((1, ts, D_MODEL), lambda b, i: (b, i, 0)),
                  full((LRU_CONV, LRU_DIM)), full((1, LRU_DIM)),
                  full((LRU_HEADS, LRU_BLOCK, LRU_BLOCK)), full((1, LRU_DIM)),
                  full((LRU_HEADS, LRU_BLOCK, LRU_BLOCK)), full((1, LRU_DIM)),
                  full((1, LRU_DIM)), full((LRU_DIM, D_MODEL))],
        out_specs=pl.BlockSpec((1, ts, D_MODEL), lambda b, i: (b, i, 0)),
        scratch_shapes=[pltpu.VMEM((ts + SUBLANES, LRU_DIM), F32),
                        pltpu.VMEM((SUBLANES, LRU_DIM), F32),
                        pltpu.VMEM((ts, LRU_DIM), F32),
                        pltpu.VMEM((ts, LRU_DIM), F32)],
        compiler_params=_cparams(("arbitrary", "arbitrary")),
        name="rg_lru",
    )(p, p, merged, conv_w, row(conv_b), wa_bf, row(ba), wx_bf, row(bx), row(lam), proj_bf)


def _rwkv_prep_kernel(pb_ref, mu_ref, w0_ref, wup_ref, a0_ref, aup_ref, gup_ref, kk_ref, ka_ref,
                      rk_ref, bd_ref, ltri_ref,
                      rt_ref, kkt_ref, kh_ref, bh_ref, v_ref, pinc_ref, bonus_ref, g_ref, ext_ref):
    ts = pb_ref.shape[1]

    @pl.when(pl.program_id(1) == 0)
    def _():
        ext_ref[pl.ds(0, SUBLANES), :] = jnp.zeros((SUBLANES, ext_ref.shape[1]), F32)

    p = pb_ref[0].astype(F32)
    ext_ref[pl.ds(SUBLANES, ts), :] = p
    prev = ext_ref[pl.ds(SUBLANES - 1, ts), :]
    ext_ref[pl.ds(0, SUBLANES), :] = ext_ref[pl.ds(ts, SUBLANES), :]
    pm = p + (prev - p) * mu_ref[...]
    r = pm[:, RW_R:RW_R + RWKV_DIM]
    k = pm[:, RW_K:RW_K + RWKV_DIM]
    v = pm[:, RW_V:RW_V + RWKV_DIM]
    xwa = pm[:, RW_XWA:RW_XWA + LANES]
    xg = pm[:, RW_XG:RW_XG + LORA_G]
    w = -jax.nn.softplus(-(w0_ref[...] + _bdot(jnp.tanh(xwa), wup_ref[...]))) - 0.5
    lw = -jnp.exp(w)
    a = jax.nn.sigmoid(a0_ref[...] + _bdot(xwa, aup_ref[...]))
    g_ref[0] = _bdot(jax.nn.sigmoid(xg), gup_ref[...])
    kkr = k * kk_ref[...]
    ss = _head_sums(kkr * kkr, bd_ref[...])
    kk = kkr / jnp.maximum(jnp.sqrt(ss), 1e-12)
    k2 = k * (1.0 + (a - 1.0) * ka_ref[...])
    lw_hi = lw.astype(BF16)
    lw_mid, lw_lo = _split(lw - lw_hi.astype(F32))
    tri = ltri_ref[...]
    lcum = (jnp.dot(tri, lw_hi, preferred_element_type=F32) + jnp.dot(tri, lw_mid, preferred_element_type=F32)
            + jnp.dot(tri, lw_lo, preferred_element_type=F32))
    pinc = jnp.exp(lcum)
    pinv = jnp.exp(-lcum)
    rt_ref[0] = r * pinc
    kkt_ref[0] = kk * jnp.exp(lcum - lw)
    kh_ref[0] = k2 * pinv
    bh_ref[0] = kk * a * pinv
    v_ref[0] = v
    pinc_ref[0] = pinc
    bonus_ref[0] = _head_sums(r * k2 * rk_ref[...], bd_ref[...]) * v


def _dot3(a, b):
    d = lambda x, y: jnp.dot(x, y, preferred_element_type=F32)
    return d(a[0], b[0]) + d(a[0], b[1]) + d(a[1], b[0])


def _rwkv_scan_kernel(rt_ref, kkt_ref, kh_ref, bh_ref, v_ref, pinc_ref, y_ref, s_ref):
    c = RWKV_CHUNK
    n = RWKV_HEAD
    nb = rt_ref.shape[0]
    heads = range(nb * RWKV_HEADS)

    @pl.when(pl.program_id(1) == 0)
    def _():
        s_ref[...] = jnp.zeros(s_ref.shape, F32)

    row = lax.broadcasted_iota(I32, (c, c), 0)
    col = lax.broadcasted_iota(I32, (c, c), 1)
    eye = (row == col).astype(F32)
    same16 = (row // 16) == (col // 16)
    same32 = (row // 32) == (col // 32)
    row2 = lax.broadcasted_iota(I32, (c, 2 * c), 0)
    col2 = lax.broadcasted_iota(I32, (c, 2 * c), 1) % c
    nt = lambda a, b: lax.dot_general(a, b, (((1,), (1,)), ((), ())), preferred_element_type=F32)
    tn = lambda a, b: lax.dot_general(a, b, (((0,), (0,)), ((), ())), preferred_element_type=F32)
    dot = lambda a, b: jnp.dot(a, b, preferred_element_type=F32)
    sl = [pl.ds((h % RWKV_HEADS) * n, n) for h in heads]
    sq = [h // RWKV_HEADS for h in heads]
    v = [v_ref[sq[h], :, sl[h]] for h in heads]
    pc = [pinc_ref[sq[h], pl.ds(c - 1, 1), sl[h]] for h in heads]
    s = [s_ref[sq[h], :, sl[h]] for h in heads]
    lhs = [jnp.concatenate([kkt_ref[sq[h], :, sl[h]], rt_ref[sq[h], :, sl[h]]], axis=0) for h in heads]
    rhs = [jnp.concatenate([bh_ref[sq[h], :, sl[h]], kh_ref[sq[h], :, sl[h]]], axis=0) for h in heads]
    big = [nt(lhs[h], rhs[h]) for h in heads]
    from_state = [nt(lhs[h], s[h]) for h in heads]
    top = [jnp.where(row2 > col2, big[h][:c], 0.0) for h in heads]
    bot = [jnp.where(row2 >= col2, big[h][c:], 0.0) for h in heads]
    a_b = [top[h][:, :c] for h in heads]
    akv = [dot(top[h], jnp.concatenate([jnp.zeros((c, n), F32), v[h]], axis=0)) for h in heads]
    d16 = [jnp.where(same16, a_b[h], 0.0) for h in heads]
    sd = [_split(d16[h]) for h in heads]
    s2 = [_split(_dot3(sd[h], sd[h])) for h in heads]
    s4 = [_split(_dot3(s2[h], s2[h])) for h in heads]
    s8 = [_split(_dot3(s4[h], s4[h])) for h in heads]
    t = [eye - d16[h] for h in heads]
    for sp in (s2, s4, s8):
        t = [t[h] + _dot3(_split(t[h]), sp[h]) for h in heads]
    for off in ([jnp.where(same32 & jnp.logical_not(same16), a_b[h], 0.0) for h in heads],
                [jnp.where(same32, 0.0, a_b[h]) for h in heads]):
        st = [_split(t[h]) for h in heads]
        lt = [_split(_dot3(_split(off[h]), st[h])) for h in heads]
        t = [t[h] - _dot3(st[h], lt[h]) for h in heads]
    u = [dot(t[h], from_state[h][:c] + akv[h]) for h in heads]
    vu = [jnp.concatenate([-u[h], v[h]], axis=0) for h in heads]
    y = [from_state[h][c:] + dot(bot[h], vu[h]) for h in heads]
    s_new = [s[h] * pc[h] + tn(vu[h], rhs[h] * pc[h]) for h in heads]
    for q in range(nb):
        mine = slice(q * RWKV_HEADS, (q + 1) * RWKV_HEADS)
        y_ref[q] = jnp.concatenate(y[mine], axis=1)
        s_ref[q] = jnp.concatenate(s_new[mine], axis=1)


def _rwkv_post_kernel(y_ref, bonus_ref, g_ref, pg_ref, m_ref, gg_ref, gb_ref, bdm_ref, pj_ref, o_ref):
    y = y_ref[0]
    mu = _head_sums(y, bdm_ref[...])
    d = y - mu
    var = _head_sums(d * d, bdm_ref[...])
    yn = d * lax.rsqrt(var + RWKV_GN_EPS) * gg_ref[...] + gb_ref[...] + bonus_ref[0]
    o = _bdot(yn * g_ref[0], pj_ref[...])
    o_ref[0] = m_ref[0] + jax.nn.sigmoid(pg_ref[0].astype(F32)) * o


def _rwkv(p, merged, mu_pad, w0, wup_pad, a0, aup_pad, g_up, k_k, k_a, r_k, gn_g, gn_b, proj_bf):
    bsz, seq, _ = p.shape
    row = lambda a: a.reshape(1, -1)
    full = lambda shape: pl.BlockSpec(shape, lambda b, i: (0,) * len(shape))
    head_id = jnp.arange(RWKV_DIM, dtype=I32) // RWKV_HEAD
    bd = (head_id[:, None] == head_id[None, :]).astype(BF16)

    ts = min(seq, 256)
    t_id = jnp.arange(ts, dtype=I32)
    ltri = ((t_id[:, None] // RWKV_CHUNK == t_id[None, :] // RWKV_CHUNK)
            & (t_id[:, None] >= t_id[None, :])).astype(BF16)
    seq_blk = lambda width: pl.BlockSpec((1, ts, width), lambda b, i: (b, i, 0))
    wide = SDS((bsz, seq, RWKV_DIM), F32)
    rt, kkt, kh, bh, v, pinc, bonus, g = pl.pallas_call(
        _rwkv_prep_kernel,
        out_shape=[wide] * 8,
        grid=(bsz, seq // ts),
        in_specs=[pl.BlockSpec((1, ts, 2048), lambda b, i: (b, i, SEG_B // 2048)),
                  full((1, 2048)), full((1, RWKV_DIM)), full((LANES, RWKV_DIM)),
                  full((1, RWKV_DIM)), full((LANES, RWKV_DIM)), full((LORA_G, RWKV_DIM)),
                  full((1, RWKV_DIM)), full((1, RWKV_DIM)), full((1, RWKV_DIM)),
                  full((RWKV_DIM, RWKV_DIM)), full((ts, ts))],
        out_specs=[seq_blk(RWKV_DIM)] * 8,
        scratch_shapes=[pltpu.VMEM((ts + SUBLANES, 2048), F32)],
        compiler_params=_cparams(("arbitrary", "arbitrary")),
        name="rwkv_prep",
    )(p, mu_pad, row(w0), wup_pad, row(a0), aup_pad, g_up, row(k_k), row(k_a), row(r_k), bd, ltri)

    c = RWKV_CHUNK
    nb = 2 if bsz % 2 == 0 else 1
    chunk_blk = pl.BlockSpec((nb, c, RWKV_DIM), lambda b, i: (b, i, 0))
    y = pl.pallas_call(
        _rwkv_scan_kernel,
        out_shape=wide,
        grid=(bsz // nb, seq // c),
        in_specs=[chunk_blk] * 6,
        out_specs=chunk_blk,
        scratch_shapes=[pltpu.VMEM((nb, RWKV_HEAD, RWKV_DIM), F32)],
        compiler_params=_cparams(("arbitrary", "arbitrary")),
        name="rwkv_scan",
    )(rt, kkt, kh, bh, v, pinc)

    tp = min(seq, 512)
    blk = lambda width: pl.BlockSpec((1, tp, width), lambda b, i: (b, i, 0))
    return pl.pallas_call(
        _rwkv_post_kernel,
        out_shape=SDS((bsz, seq, D_MODEL), F32),
        grid=(bsz, seq // tp),
        in_specs=[blk(RWKV_DIM), blk(RWKV_DIM), blk(RWKV_DIM),
                  pl.BlockSpec((1, tp, D_MODEL), lambda b, i: (b, i, SEG_G // D_MODEL + 1)),
                  blk(D_MODEL), full((1, RWKV_DIM)), full((1, RWKV_DIM)),
                  full((RWKV_DIM, RWKV_DIM)), full((RWKV_DIM, D_MODEL))],
        out_specs=blk(D_MODEL),
        compiler_params=_cparams(("arbitrary", "arbitrary")),
        name="rwkv_post",
    )(y, bonus, g, p, merged, row(gn_g), row(gn_b), bd * (1.0 / RWKV_HEAD), proj_bf)


def _out_kernel(m_ref, x_ref, g1_ref, w_ref, n2_ref, sc_ref, sh_ref, xo_ref, h_ref):
    xn = x_ref[0] + g1_ref[0] * _bdot(m_ref[0], w_ref[...])
    xo_ref[0] = xn
    h_ref[0] = _modulated_rmsnorm(xn, n2_ref[...], sc_ref[0], sh_ref[0])


def _out_proj(merged, x, g1, w_bf, norm2, sc2, sh2):
    bsz, seq, _ = x.shape
    ts = min(seq, 512)
    blk = pl.BlockSpec((1, ts, D_MODEL), lambda b, i: (b, i, 0))
    per_b = pl.BlockSpec((1, 1, D_MODEL), lambda b, i: (b, 0, 0))
    return pl.pallas_call(
        _out_kernel,
        out_shape=[SDS((bsz, seq, D_MODEL), F32)] * 2,
        grid=(bsz, seq // ts),
        in_specs=[blk, blk, per_b, pl.BlockSpec((D_MODEL, D_MODEL), lambda b, i: (0, 0)),
                  pl.BlockSpec((1, D_MODEL), lambda b, i: (0, 0)), per_b, per_b],
        out_specs=[blk, blk],
        compiler_params=_cparams(("arbitrary", "arbitrary")),
        name="out_proj",
    )(merged, x, g1, w_bf, norm2, sc2, sh2)


MOE_TILE = 256
MOE_SLOTS = 2560
MOE_CHUNK = 32
U32 = jnp.uint32


def _route_kernel(h_ref, rw_ref, bias_ref, upper_ref, ltri_ref, slot_ref, w_ref, n8_ref):
    tile = h_ref.shape[0]
    neg = -jnp.inf
    scores = jax.nn.sigmoid(_hdot_nt(rw_ref[...], h_ref[...]))
    s3 = scores.reshape(GROUP_SIZE, N_GROUPS, tile)
    b3 = s3 + bias_ref[...].reshape(GROUP_SIZE, N_GROUPS, tile)
    slab = lax.broadcasted_iota(I32, b3.shape, 0).astype(F32)
    grp = lax.broadcasted_iota(I32, b3.shape, 1).astype(F32)
    eid = grp * GROUP_SIZE + slab
    m1 = jnp.max(b3, axis=0, keepdims=True)
    first = jnp.min(jnp.where(b3 == m1, slab, GROUP_SIZE), axis=0, keepdims=True)
    m2 = jnp.max(jnp.where(slab == first, neg, b3), axis=0, keepdims=True)
    gs = (m1 + m2)[0]
    gi = lax.broadcasted_iota(I32, gs.shape, 0).astype(F32)
    chosen = jnp.zeros(gs.shape, F32)
    for _ in range(TOPK_GROUPS):
        m = jnp.max(gs, axis=0, keepdims=True)
        hit = gi == jnp.min(jnp.where(gs == m, gi, N_GROUPS), axis=0, keepdims=True)
        chosen = jnp.where(hit, 1.0, chosen)
        gs = jnp.where(hit, neg, gs)
    cur = jnp.where((chosen > 0.0)[None], b3, neg)
    ones_cols = jnp.ones((tile, LANES), BF16)
    lanes_of = lambda a: jnp.concatenate([a] * (tile // LANES), axis=1)
    to3 = lambda a: a.reshape(GROUP_SIZE, N_GROUPS, tile)
    fold = lambda a: jnp.sum(jnp.sum(a, axis=0), axis=0, keepdims=True)
    base = jnp.zeros((N_EXPERTS, LANES), F32)
    picks, w_rows, rank_rows = [], [], []
    for _ in range(TOP_K):
        m = jnp.max(jnp.max(cur, axis=0), axis=0, keepdims=True)[None]
        pick = jnp.min(jnp.min(jnp.where(cur == m, eid, N_EXPERTS), axis=0), axis=0, keepdims=True)
        hit = eid == pick[None]
        w_rows.append(fold(jnp.where(hit, s3, 0.0)))
        picks.append(pick)
        cur = jnp.where(hit, neg, cur)
        onehot = hit.astype(BF16).reshape(N_EXPERTS, tile)
        before = jnp.dot(onehot, upper_ref[...], preferred_element_type=F32)
        rank_rows.append(fold(jnp.where(hit, to3(before + lanes_of(base)), 0.0)))
        base = base + jnp.dot(onehot, ones_cols, preferred_element_type=F32)
    n8 = jnp.floor((base + (SUBLANES - 1.0)) * (1.0 / SUBLANES)) * SUBLANES
    run_start = to3(lanes_of(_hdot(ltri_ref[...], n8)))
    slots = [rank_rows[j] + fold(jnp.where(eid == picks[j][None], run_start, 0.0)) for j in range(TOP_K)]
    n8_ref[0] = n8
    w_all = jnp.concatenate(w_rows, axis=0)
    w_ref[...] = w_all / jnp.sum(w_all, axis=0, keepdims=True) * ROUTED_SCALE
    slot_ref[...] = jnp.concatenate(slots, axis=0).astype(I32)


def _route(h2, router_w, router_bias):
    n_tok = h2.shape[0]
    tile = MOE_TILE
    n_tiles = n_tok // tile
    regroup = lambda a: a.reshape(N_GROUPS, GROUP_SIZE, -1).transpose(1, 0, 2).reshape(N_EXPERTS, -1)
    rw = regroup(router_w.T)
    bias = jnp.broadcast_to(regroup(router_bias.reshape(N_EXPERTS, 1)), (N_EXPERTS, tile))
    t_id = jnp.arange(tile, dtype=I32)
    upper = (t_id[:, None] < t_id[None, :]).astype(BF16)
    e_id = jnp.arange(N_EXPERTS, dtype=I32)
    ltri = (e_id[:, None] > e_id[None, :]).astype(F32)
    tok_blk = pl.BlockSpec((TOP_K, tile), lambda i: (0, i))
    full = lambda shape: pl.BlockSpec(shape, lambda i: (0,) * len(shape))
    slot_t, w_t, n8 = pl.pallas_call(
        _route_kernel,
        out_shape=[SDS((TOP_K, n_tok), I32), SDS((TOP_K, n_tok), F32),
                   SDS((n_tiles, N_EXPERTS, LANES), F32)],
        grid=(n_tiles,),
        in_specs=[pl.BlockSpec((tile, D_MODEL), lambda i: (i, 0)),
                  full((N_EXPERTS, D_MODEL)), full((N_EXPERTS, tile)), full((tile, tile)),
                  full((N_EXPERTS, N_EXPERTS))],
        out_specs=[tok_blk, tok_blk, pl.BlockSpec((1, N_EXPERTS, LANES), lambda i: (i, 0, 0))],
        compiler_params=_cparams(("arbitrary",)),
        name="moe_route",
    )(h2, rw, bias, upper, ltri)
    return slot_t, w_t, n8[:, :, 0].astype(I32)


def _for_each_run_piece(tile_idx, n8_ref, fn):
    def per_expert(r, carry):
        n = n8_ref[tile_idx * N_EXPERTS + r]
        whole = n // MOE_CHUNK

        def chunk(k, c):
            fn(r, k * MOE_CHUNK, MOE_CHUNK)
            return c
        lax.fori_loop(0, whole, chunk, 0)
        rest = whole * MOE_CHUNK
        for size in (16, 8):
            @pl.when((n & size) != 0)
            def _(size=size):
                fn(r, rest + (n & (MOE_CHUNK - 1) & ~(2 * size - 1)), size)
        return carry
    lax.fori_loop(0, N_EXPERTS, per_expert, 0)


def _wait_rows(total, make_wait):
    size = SUBLANES
    while size <= MOE_SLOTS:
        @pl.when((total & size) != 0)
        def _(size=size):
            make_wait(size).wait()
        size *= 2


def _pack_pairs(hi_bits, lo_bits):
    return (hi_bits & jnp.uint32(0xFFFF0000)) | (lo_bits >> 16)


def _unpack_pairs(u):
    hi = lax.bitcast_convert_type(u & jnp.uint32(0xFFFF0000), F32)
    lo = lax.bitcast_convert_type(u << 16, F32)
    return jnp.concatenate([hi, lo], axis=1).astype(BF16)


def _dispatch_kernel(n8_ref, off_ref, dst_ref, tot_ref, pad_end_ref, slot_ref, h_ref, xs_ref,
                     g_ref, zero_ref, sem):
    i = pl.program_id(0)
    tile = h_ref.shape[0]
    half = D_MODEL // 2

    @pl.when(i == 0)
    def _():
        zero_ref[...] = jnp.zeros(zero_ref.shape, U32)

        def last_block(e):
            start = pl.multiple_of(jnp.maximum(pad_end_ref[e] - MOE_ROWS, 0), MOE_ROWS)
            return pltpu.make_async_copy(zero_ref, xs_ref.at[pl.ds(start, MOE_ROWS), :], sem)

        def has_rows(e):
            prev = jnp.where(e > 0, pad_end_ref[jnp.maximum(e - 1, 0)], 0)
            return pad_end_ref[e] > prev

        def clear(e, carry):
            @pl.when(has_rows(e))
            def _():
                last_block(e).start()
            return carry
        lax.fori_loop(0, N_EXPERTS, clear, 0)

        def done(e, carry):
            @pl.when(has_rows(e))
            def _():
                last_block(e).wait()
            return carry
        lax.fori_loop(0, N_EXPERTS, done, 0)

        def tail_block(b):
            start = pl.multiple_of(b * MOE_ROWS, MOE_ROWS)
            return pltpu.make_async_copy(zero_ref, xs_ref.at[pl.ds(start, MOE_ROWS), :], sem)

        def clear_tail(b, carry):
            tail_block(b).start()
            return carry

        def done_tail(b, carry):
            tail_block(b).wait()
            return carry
        used = pad_end_ref[N_EXPERTS - 1] // MOE_ROWS
        lax.fori_loop(used, xs_ref.shape[0] // MOE_ROWS, clear_tail, 0)
        lax.fori_loop(used, xs_ref.shape[0] // MOE_ROWS, done_tail, 0)

    slot_id = lax.broadcasted_iota(I32, (MOE_SLOTS, tile), 0)
    sel = jnp.zeros((MOE_SLOTS, tile), F32)
    for j in range(TOP_K):
        sel = jnp.where(slot_id == slot_ref[pl.ds(j, 1), :], 1.0, sel)
    g = jnp.dot(sel.astype(BF16), h_ref[...].astype(BF16), preferred_element_type=F32)
    bits = lax.bitcast_convert_type(g, U32)
    g_ref[...] = _pack_pairs(bits[:, :half], bits[:, half:])

    def piece(r, offset, rows):
        src = pl.multiple_of(off_ref[i * N_EXPERTS + r] + offset, SUBLANES)
        dst = pl.multiple_of(dst_ref[i * N_EXPERTS + r] + offset, SUBLANES)
        pltpu.make_async_copy(g_ref.at[pl.ds(src, rows), :], xs_ref.at[pl.ds(dst, rows), :], sem).start()

    _for_each_run_piece(i, n8_ref, piece)
    _wait_rows(tot_ref[i], lambda rows: pltpu.make_async_copy(
        g_ref.at[pl.ds(0, rows), :], xs_ref.at[pl.ds(0, rows), :], sem))


def _dispatch(h2, slot_t, n8_flat, off_flat, dst_flat, tot, pad_end, n_rows):
    n_tok = h2.shape[0]
    tile = MOE_TILE
    grid_spec = pltpu.PrefetchScalarGridSpec(
        num_scalar_prefetch=5,
        grid=(n_tok // tile,),
        in_specs=[pl.BlockSpec((TOP_K, tile), lambda i, *_: (0, i)),
                  pl.BlockSpec((tile, D_MODEL), lambda i, *_: (i, 0))],
        out_specs=pl.BlockSpec(memory_space=pl.ANY),
        scratch_shapes=[pltpu.VMEM((MOE_SLOTS, D_MODEL // 2), U32),
                        pltpu.VMEM((MOE_ROWS, D_MODEL // 2), U32), pltpu.SemaphoreType.DMA],
    )
    return pl.pallas_call(
        _dispatch_kernel,
        out_shape=SDS((n_rows, D_MODEL // 2), U32),
        grid_spec=grid_spec,
        compiler_params=_cparams(("arbitrary",)),
        name="moe_dispatch",
    )(n8_flat, off_flat, dst_flat, tot, pad_end, slot_t, h2)


def _expert_kernel(be_ref, used_ref, xs_ref, w1_ref, w3_ref, w2_ref, y_ref):
    half = D_MODEL // 2

    @pl.when(pl.program_id(0) < used_ref[0])
    def _():
        x = _unpack_pairs(xs_ref[...])
        hid = jax.nn.silu(_bdot(x, w1_ref[0, 0])) * _bdot(x, w3_ref[0, 0])
        y = _bdot(hid, w2_ref[0, 0]).astype(BF16).astype(F32)
        bits = lax.bitcast_convert_type(y, U32)
        y_ref[...] = _pack_pairs(bits[:, :half], bits[:, half:])

    @pl.when(pl.program_id(0) >= used_ref[0])
    def _():
        y_ref[...] = jnp.zeros(y_ref.shape, U32)


def _experts(xs, block_exp, used, w1, w3, w2, layer):
    n_rows = xs.shape[0]
    rows = MOE_ROWS
    grid_spec = pltpu.PrefetchScalarGridSpec(
        num_scalar_prefetch=2,
        grid=(n_rows // rows,),
        in_specs=[pl.BlockSpec((rows, D_MODEL // 2), lambda b, be, used: (b, 0)),
                  pl.BlockSpec((1, 1, D_MODEL, EXPERT_FF), lambda b, be, used: (layer, be[b], 0, 0)),
                  pl.BlockSpec((1, 1, D_MODEL, EXPERT_FF), lambda b, be, used: (layer, be[b], 0, 0)),
                  pl.BlockSpec((1, 1, EXPERT_FF, D_MODEL), lambda b, be, used: (layer, be[b], 0, 0))],
        out_specs=pl.BlockSpec((rows, D_MODEL // 2), lambda b, be, used: (b, 0)),
    )
    return pl.pallas_call(
        _expert_kernel,
        out_shape=SDS((n_rows, D_MODEL // 2), U32),
        grid_spec=grid_spec,
        compiler_params=_cparams(("arbitrary",)),
        name="moe_experts",
    )(block_exp, used, xs, w1, w3, w2)


def _combine_kernel(n8_ref, off_ref, src_ref, tot_ref, y_ref, slot_ref, w_ref, h_ref, x_ref, g2_ref,
                    s1_ref, s3_ref, s2_ref, fn_ref, o_ref, yt_ref, sem, *, final):
    tile = h_ref.shape[1]
    i = pl.program_id(0) * pl.num_programs(1) + pl.program_id(1)

    @pl.when(i == 0)
    def _():
        yt_ref[...] = jnp.zeros(yt_ref.shape, U32)

    def piece(r, offset, rows):
        src = pl.multiple_of(src_ref[i * N_EXPERTS + r] + offset, SUBLANES)
        dst = pl.multiple_of(off_ref[i * N_EXPERTS + r] + offset, SUBLANES)
        pltpu.make_async_copy(y_ref.at[pl.ds(src, rows), :], yt_ref.at[pl.ds(dst, rows), :], sem).start()

    _for_each_run_piece(i, n8_ref, piece)
    h = h_ref[0]
    shared = _bdot(jax.nn.silu(_bdot(h, s1_ref[...])) * _bdot(h, s3_ref[...]), s2_ref[...])
    slot_id = lax.broadcasted_iota(I32, (tile, MOE_SLOTS), 1)
    slots = slot_ref[0]
    w = w_ref[0]
    pw = jnp.zeros((tile, MOE_SLOTS), F32)
    for j in range(TOP_K):
        pw = jnp.where(slot_id == slots[:, j:j + 1], w[:, j:j + 1], pw)
    _wait_rows(tot_ref[i], lambda rows: pltpu.make_async_copy(
        y_ref.at[pl.ds(0, rows), :], yt_ref.at[pl.ds(0, rows), :], sem))
    routed = jnp.dot(pw.astype(BF16), _unpack_pairs(yt_ref[...]), preferred_element_type=F32)
    xn = x_ref[0] + g2_ref[0] * (routed + shared)
    if final:
        xn = xn * lax.rsqrt(jnp.mean(xn * xn, axis=-1, keepdims=True) + NORM_EPS) * fn_ref[...]
    o_ref[0] = xn


def _combine(y, slot_nat, w_nat, n8_flat, off_flat, dst_flat, tot, h2, x, g2, s1_bf, s3_bf, s2_bf,
             final_norm, final):
    bsz, seq, _ = x.shape
    tile = MOE_TILE
    per_seq = seq // tile
    blk = pl.BlockSpec((1, tile, D_MODEL), lambda b, i, *_: (b, i, 0))
    tok = pl.BlockSpec((1, tile, TOP_K), lambda b, i, *_: (b, i, 0))
    full = lambda shape: pl.BlockSpec(shape, lambda b, i, *_: (0,) * len(shape))
    grid_spec = pltpu.PrefetchScalarGridSpec(
        num_scalar_prefetch=4,
        grid=(bsz, per_seq),
        in_specs=[pl.BlockSpec(memory_space=pl.ANY), tok, tok, blk, blk,
                  pl.BlockSpec((1, 1, D_MODEL), lambda b, i, *_: (b, 0, 0)),
                  full((D_MODEL, SHARED_FF)), full((D_MODEL, SHARED_FF)), full((SHARED_FF, D_MODEL)),
                  full((1, D_MODEL))],
        out_specs=blk,
        scratch_shapes=[pltpu.VMEM((MOE_SLOTS, D_MODEL // 2), U32), pltpu.SemaphoreType.DMA],
    )
    return pl.pallas_call(
        functools.partial(_combine_kernel, final=final),
        out_shape=SDS((bsz, seq, D_MODEL), F32),
        grid_spec=grid_spec,
        compiler_params=_cparams(("arbitrary", "arbitrary")),
        name="moe_combine",
    )(n8_flat, off_flat, dst_flat, tot, y, slot_nat, w_nat, h2, x, g2, s1_bf, s3_bf, s2_bf, final_norm)


def _moe(x, h2, g2, router_w, router_bias, w1, w3, w2, layer, s1_bf, s3_bf, s2_bf, final_norm, final):
    bsz, seq, _ = x.shape
    assert seq % MOE_TILE == 0, "token tiles must not straddle sequences"
    n_tok = bsz * seq
    slot_t, w_t, n8 = _route(h2.reshape(n_tok, D_MODEL), router_w, router_bias)
    n_tiles = n8.shape[0]
    counts = jnp.sum(n8, axis=0)
    padded = (counts + MOE_ROWS - 1) // MOE_ROWS * MOE_ROWS
    pad_end = jnp.cumsum(padded).astype(I32)
    pad_start = pad_end - padded
    run_row = (pad_start[None, :] + jnp.cumsum(n8, axis=0) - n8).astype(I32)
    run_slot = (jnp.cumsum(n8, axis=1) - n8).astype(I32)
    tot = jnp.sum(n8, axis=1).astype(I32)
    max_rows = n_tok * TOP_K + n_tiles * N_EXPERTS * (SUBLANES - 1) + N_EXPERTS * (MOE_ROWS - 1)
    n_blocks = (max_rows + MOE_ROWS - 1) // MOE_ROWS
    block_first = jnp.arange(n_blocks, dtype=I32) * MOE_ROWS
    block_row = jnp.minimum(jnp.sum((pad_end[None, :] <= block_first[:, None]).astype(I32), axis=1),
                            N_EXPERTS - 1)
    block_exp = ((block_row % N_GROUPS) * GROUP_SIZE + block_row // N_GROUPS).astype(I32)
    used = (pad_end[-1:] // MOE_ROWS).astype(I32)
    flat = lambda a: a.reshape(-1).astype(I32)
    xs = _dispatch(h2.reshape(n_tok, D_MODEL), slot_t, flat(n8), flat(run_slot), flat(run_row), tot,
                   pad_end, n_blocks * MOE_ROWS)
    y = _experts(xs, block_exp, used, w1, w3, w2, layer)
    nat = lambda a: a.T.reshape(bsz, seq, TOP_K)
    return _combine(y, nat(slot_t), nat(w_t), flat(n8), flat(run_slot), flat(run_row), tot, h2, x, g2,
                    s1_bf, s3_bf, s2_bf, final_norm, final)


def _pad_cols(a, width):
    return jnp.pad(a, ((0, 0), (0, width - a.shape[1])))


def _layout_w_in(w_in):
    b0 = COLS_A
    c0 = COLS_A + COLS_B
    g0 = c0 + COLS_C
    seg_b = _pad_cols(w_in[:, b0:c0], SEG_G - SEG_B)
    return jnp.concatenate([w_in[:, c0:g0], seg_b, w_in[:, g0:], w_in[:, :b0]], axis=1).astype(BF16)


def _layout_mu(mu):
    return _pad_cols(mu.reshape(1, -1), SEG_G - SEG_B)


def _pad_rows(a, height):
    return jnp.pad(a, ((0, height - a.shape[0]), (0, 0)))


def kernel(x, c, ada_w, ada_b, norm1, norm2, w_in, conv_a_w, conv_a_b, ln_a_g, ln_a_b, proj_a, mu_b, w0, w_up, a0, a_up, g_up, k_k, k_a, r_k, gn_b_g, gn_b_b, proj_b, conv_c_w, conv_c_b, lru_wa, lru_ba, lru_wx, lru_bx, lru_lambda, proj_c, w_out, router_w, router_bias, exp_w1, exp_w3, exp_w2, sh_w1, sh_w3, sh_w2, final_norm):
    depth = ada_w.shape[0]
    bsz = x.shape[0]
    mod = _ada_mod(c, ada_w, ada_b)
    for l in range(depth):
        sh1, sc1, g1, sh2, sc2, g2 = [mod[l, :, i * D_MODEL:(i + 1) * D_MODEL].reshape(bsz, 1, D_MODEL)
                                      for i in range(N_MOD)]
        p = _in_proj(x, sc1, sh1, norm1[l].reshape(1, -1), _layout_w_in(w_in[l]))
        merged = _conv_a(p, conv_a_w[l], conv_a_b[l], ln_a_g[l], ln_a_b[l], proj_a[l].astype(BF16))
        merged = _rwkv(p, merged, _layout_mu(mu_b[l]), w0[l], _pad_rows(w_up[l], LANES).astype(BF16),
                       a0[l], jnp.pad(a_up[l], ((LORA_W, 0), (0, 0))).astype(BF16), g_up[l].astype(BF16),
                       k_k[l], k_a[l], r_k[l], gn_b_g[l], gn_b_b[l], proj_b[l].astype(BF16))
        merged = _lru(p, merged, conv_c_w[l], conv_c_b[l], lru_wa[l].astype(BF16), lru_ba[l],
                      lru_wx[l].astype(BF16), lru_bx[l], lru_lambda[l], proj_c[l].astype(BF16))
        x, h2 = _out_proj(merged, x, g1, w_out[l].astype(BF16), norm2[l].reshape(1, -1), sc2, sh2)
        x = _moe(x, h2, g2, router_w[l], router_bias[l], exp_w1, exp_w3, exp_w2, l,
                 sh_w1[l].astype(BF16), sh_w3[l].astype(BF16), sh_w2[l].astype(BF16),
                 final_norm.reshape(1, -1), final=(l == depth - 1))
    return x
```

```python
import functools

import jax
import jax.numpy as jnp
from jax import lax
from jax.experimental import pallas as pl
from jax.experimental.pallas import tpu as pltpu

F32 = jnp.float32
BF16 = jnp.bfloat16
I32 = jnp.int32
SDS = jax.ShapeDtypeStruct
HIGHEST = lax.Precision.HIGHEST

D_MODEL = 1024
N_MOD = 6
NORM_EPS = 1e-6
CONV_A_CH = 512
CONV_A_WIDTH = 31
CONV_A_LN_EPS = 1e-5
RWKV_HEADS = 8
RWKV_HEAD = 64
RWKV_DIM = RWKV_HEADS * RWKV_HEAD
LORA_W = 64
LORA_A = 64
LORA_G = 128
RWKV_GN_EPS = 64e-5
RWKV_CHUNK = 64
LRU_DIM = 1024
LRU_HEADS = 8
LRU_BLOCK = LRU_DIM // LRU_HEADS
LRU_CONV = 4
LRU_C = 8.0
N_EXPERTS = 64
TOP_K = 8
N_GROUPS = 8
GROUP_SIZE = N_EXPERTS // N_GROUPS
TOPK_GROUPS = 4
EXPERT_FF = 256
SHARED_FF = 256
ROUTED_SCALE = 2.5
SEG_C = 0
SEG_B = 2048
SEG_G = 4096
SEG_A = 7168
IN_COLS_PAD = 8192
RW_R, RW_K, RW_V, RW_XWA, RW_XG = 0, 512, 1024, 1536, 1664
COLS_A = 2 * CONV_A_CH
COLS_B = 3 * RWKV_DIM + LORA_W + LORA_A + LORA_G
COLS_C = 2 * LRU_DIM
VMEM_LIMIT = 56 * 1024 * 1024
SUBLANES = 8
LANES = 128
MOE_ROWS = 512


def _cparams(sem):
    return pltpu.CompilerParams(dimension_semantics=sem, vmem_limit_bytes=VMEM_LIMIT)


def _bdot(a, b):
    return jnp.dot(a.astype(BF16), b.astype(BF16), preferred_element_type=F32)


def _hdot(a, b):
    return jnp.dot(a, b, preferred_element_type=F32, precision=HIGHEST)


def _split(a):
    hi = a.astype(BF16)
    return hi, (a - hi.astype(F32)).astype(BF16)


def _head_sums(a, ones_bf):
    hi, lo = _split(a)
    return (jnp.dot(hi, ones_bf, preferred_element_type=F32)
            + jnp.dot(lo, ones_bf, preferred_element_type=F32))


def _hdot_nt(a, b):
    return lax.dot_general(a, b, (((1,), (1,)), ((), ())), preferred_element_type=F32,
                           precision=HIGHEST)


def _hdot_tn(a, b):
    return lax.dot_general(a, b, (((0,), (0,)), ((), ())), preferred_element_type=F32,
                           precision=HIGHEST)


def _ada_kernel(c_ref, w_ref, b_ref, o_ref):
    cond = jax.nn.silu(c_ref[...])
    o_ref[0] = _bdot(cond, w_ref[0]) + b_ref[0]


def _ada_mod(c, ada_w, ada_b):
    depth, _, n = ada_w.shape
    bsz = c.shape[0]
    tn = 1536
    return pl.pallas_call(
        _ada_kernel,
        out_shape=SDS((depth, bsz, n), F32),
        grid=(depth, n // tn),
        in_specs=[pl.BlockSpec((bsz, D_MODEL), lambda l, j: (0, 0)),
                  pl.BlockSpec((1, D_MODEL, tn), lambda l, j: (l, 0, j)),
                  pl.BlockSpec((1, 1, tn), lambda l, j: (l, 0, j))],
        out_specs=pl.BlockSpec((1, bsz, tn), lambda l, j: (l, 0, j)),
        compiler_params=_cparams(("arbitrary", "arbitrary")),
        name="ada_mod",
    )(c, ada_w, ada_b.reshape(depth, 1, n))


def _modulated_rmsnorm(x, g, sc, sh):
    y = x * lax.rsqrt(jnp.mean(x * x, axis=-1, keepdims=True) + NORM_EPS)
    return (y * g) * (1.0 + sc) + sh


def _in_kernel(x_ref, sc_ref, sh_ref, g_ref, w_ref, o_ref, h_ref):
    @pl.when(pl.program_id(2) == 0)
    def _():
        h_ref[...] = _modulated_rmsnorm(x_ref[0], g_ref[...], sc_ref[0], sh_ref[0]).astype(BF16)

    acc = jnp.dot(h_ref[...], w_ref[...], preferred_element_type=F32)
    j = pl.program_id(2)
    tn = o_ref.shape[2]
    is_gelu = j == SEG_C // tn
    is_gate = (j >= SEG_G // tn) & (j < SEG_A // tn)
    is_glu = j == SEG_A // tn

    @pl.when(is_gelu)
    def _():
        o_ref[0] = jax.nn.gelu(acc, approximate=True).astype(o_ref.dtype)

    @pl.when(is_gate)
    def _():
        o_ref[0] = jax.nn.sigmoid(acc).astype(o_ref.dtype)

    @pl.when(is_glu)
    def _():
        half = tn // 2
        u = acc[:, :half] * jax.nn.sigmoid(acc[:, half:])
        o_ref[0] = jnp.concatenate([u, jnp.zeros_like(u)], axis=1).astype(o_ref.dtype)

    @pl.when(jnp.logical_not(is_gelu | is_gate | is_glu))
    def _():
        o_ref[0] = acc.astype(o_ref.dtype)


def _in_proj(x, sc, sh, g, w_pad):
    bsz, seq, _ = x.shape
    ts = min(seq, 1024)
    tn = 1024
    return pl.pallas_call(
        _in_kernel,
        out_shape=SDS((bsz, seq, IN_COLS_PAD), BF16),
        grid=(bsz, seq // ts, IN_COLS_PAD // tn),
        in_specs=[pl.BlockSpec((1, ts, D_MODEL), lambda b, i, j: (b, i, 0)),
                  pl.BlockSpec((1, 1, D_MODEL), lambda b, i, j: (b, 0, 0)),
                  pl.BlockSpec((1, 1, D_MODEL), lambda b, i, j: (b, 0, 0)),
                  pl.BlockSpec((1, D_MODEL), lambda b, i, j: (0, 0)),
                  pl.BlockSpec((D_MODEL, tn), lambda b, i, j: (0, j))],
        out_specs=pl.BlockSpec((1, ts, tn), lambda b, i, j: (b, i, j)),
        scratch_shapes=[pltpu.VMEM((ts, D_MODEL), BF16)],
        compiler_params=_cparams(("arbitrary", "arbitrary", "arbitrary")),
        name="in_proj",
    )(x, sc, sh, g, w_pad)


CONV_A_HALO = 32
CONV_A_SUB = 64


def _conv_a_kernel(pa_ref, pg_ref, cw_ref, cb_ref, lg_ref, lb_ref, pj_ref, o_ref, ext_ref, sh_ref, y_ref):
    ts = pa_ref.shape[1]

    @pl.when(pl.program_id(1) == 0)
    def _():
        ext_ref[pl.ds(0, CONV_A_HALO), :] = jnp.zeros((CONV_A_HALO, CONV_A_CH), F32)

    ext_ref[pl.ds(CONV_A_HALO, ts), :] = pa_ref[0].astype(F32)
    for p in range(1, SUBLANES):
        sh_ref[p - 1] = ext_ref[pl.ds(p, sh_ref.shape[1]), :]
    first = CONV_A_HALO - (CONV_A_WIDTH - 1)
    for r0 in range(0, ts, CONV_A_SUB):
        acc = jnp.zeros((CONV_A_SUB, CONV_A_CH), F32) + cb_ref[...]
        for j in range(CONV_A_WIDTH):
            phase = (first + j) % SUBLANES
            rows = pl.ds(r0 + first + j - phase, CONV_A_SUB)
            tap = ext_ref[rows, :] if phase == 0 else sh_ref[phase - 1, rows, :]
            acc = acc + tap * cw_ref[pl.ds(j, 1), :]
        y_ref[pl.ds(r0, CONV_A_SUB), :] = acc
    ext_ref[pl.ds(0, CONV_A_HALO), :] = ext_ref[pl.ds(ts, CONV_A_HALO), :]
    y = y_ref[...]
    mu = jnp.mean(y, axis=-1, keepdims=True)
    d = y - mu
    var = jnp.mean(d * d, axis=-1, keepdims=True)
    yn = d * lax.rsqrt(var + CONV_A_LN_EPS) * lg_ref[...] + lb_ref[...]
    o = _bdot(jax.nn.silu(yn), pj_ref[...])
    o_ref[0] = pg_ref[0].astype(F32) * o


def _conv_a(p, conv_w, conv_b, ln_g, ln_b, proj_bf):
    bsz, seq, _ = p.shape
    ts = min(seq, 512)
    row = lambda a: a.reshape(1, -1)
    full = lambda shape: pl.BlockSpec(shape, lambda b, i: (0,) * len(shape))
    return pl.pallas_call(
        _conv_a_kernel,
        out_shape=SDS((bsz, seq, D_MODEL), F32),
        grid=(bsz, seq // ts),
        in_specs=[pl.BlockSpec((1, ts, CONV_A_CH), lambda b, i: (b, i, SEG_A // CONV_A_CH)),
                  pl.BlockSpec((1, ts, D_MODEL), lambda b, i: (b, i, SEG_G // D_MODEL)),
                  full((CONV_A_WIDTH, CONV_A_CH)), full((1, CONV_A_CH)), full((1, CONV_A_CH)),
                  full((1, CONV_A_CH)), full((CONV_A_CH, D_MODEL))],
        out_specs=pl.BlockSpec((1, ts, D_MODEL), lambda b, i: (b, i, 0)),
        scratch_shapes=[pltpu.VMEM((ts + CONV_A_HALO, CONV_A_CH), F32),
                        pltpu.VMEM((SUBLANES - 1, ts + CONV_A_HALO - SUBLANES, CONV_A_CH), F32),
                        pltpu.VMEM((ts, CONV_A_CH), F32)],
        compiler_params=_cparams(("arbitrary", "arbitrary")),
        name="conv_a",
    )(p, p, conv_w, row(conv_b), row(ln_g), row(ln_b), proj_bf)


def _lru_kernel(pc_ref, pg_ref, m_ref, cw_ref, cb_ref, wa_ref, ba_ref, wx_ref, bx_ref, lam_ref,
                pj_ref, o_ref, ext_ref, h_ref, a_ref, b_ref):
    ts = pc_ref.shape[1]
    groups = ts // SUBLANES

    @pl.when(pl.program_id(1) == 0)
    def _():
        ext_ref[pl.ds(0, SUBLANES), :] = jnp.zeros((SUBLANES, LRU_DIM), F32)
        h_ref[...] = jnp.zeros((SUBLANES, LRU_DIM), F32)

    pc = pc_ref[0].astype(F32)
    y_gate = pc[:, :LRU_DIM]
    ext_ref[pl.ds(SUBLANES, ts), :] = pc[:, LRU_DIM:]
    first = SUBLANES - (LRU_CONV - 1)
    xc = jnp.zeros((ts, LRU_DIM), F32) + cb_ref[...]
    for j in range(LRU_CONV):
        xc = xc + ext_ref[pl.ds(first + j, ts), :] * cw_ref[pl.ds(j, 1), :]
    ext_ref[pl.ds(0, SUBLANES), :] = ext_ref[pl.ds(ts, SUBLANES), :]

    def block_diag(w_ref):
        return jnp.concatenate(
            [_bdot(xc[:, h * LRU_BLOCK:(h + 1) * LRU_BLOCK], w_ref[h]) for h in range(LRU_HEADS)],
            axis=1)

    gate_a = jax.nn.sigmoid(block_diag(wa_ref) + ba_ref[...])
    gate_x = jax.nn.sigmoid(block_diag(wx_ref) + bx_ref[...])
    log_a = -LRU_C * gate_a * jax.nn.softplus(-lam_ref[...])
    a = jnp.exp(log_a)
    b = xc * gate_x * jnp.sqrt(1.0 - jnp.exp(2.0 * log_a))

    a3 = a.reshape(groups, SUBLANES, LRU_DIM)
    b3 = b.reshape(groups, SUBLANES, LRU_DIM)
    row = lax.broadcasted_iota(I32, (groups, SUBLANES, LRU_DIM), 1)
    for s in (1, 2, 4):
        keep = row >= s
        b3 = jnp.where(keep, a3 * pltpu.roll(b3, s, axis=1) + b3, b3)
        a3 = jnp.where(keep, a3 * pltpu.roll(a3, s, axis=1), a3)
    a_ref[...] = a3.reshape(ts, LRU_DIM)
    b_ref[...] = b3.reshape(ts, LRU_DIM)
    h = h_ref[...]
    for g in range(groups):
        rows = pl.ds(g * SUBLANES, SUBLANES)
        hg = a_ref[rows, :] * h + b_ref[rows, :]
        b_ref[rows, :] = hg
        h = jnp.broadcast_to(hg[SUBLANES - 1:SUBLANES, :], (SUBLANES, LRU_DIM))
    h_ref[...] = h
    o = _bdot(b_ref[...] * y_gate, pj_ref[...])
    o_ref[0] = m_ref[0] + pg_ref[0].astype(F32) * o


def _lru(p, merged, conv_w, conv_b, wa_bf, ba, wx_bf, bx, lam, proj_bf):
    bsz, seq, _ = p.shape
    ts = min(seq, 256)
    row = lambda a: a.reshape(1, -1)
    full = lambda shape: pl.BlockSpec(shape, lambda b, i: (0,) * len(shape))
    return pl.pallas_call(
        _lru_kernel,
        out_shape=SDS((bsz, seq, D_MODEL), F32),
        grid=(bsz, seq // ts),
        in_specs=[pl.BlockSpec((1, ts, 2 * LRU_DIM), lambda b, i: (b, i, SEG_C // (2 * LRU_DIM))),
                  pl.BlockSpec((1, ts, D_MODEL), lambda b, i: (b, i, SEG_G // D_MODEL + 2)),
                  pl.BlockSpec((1, ts, D_MODEL), lambda b, i: (b, i, 0)),
                  full((LRU_CONV, LRU_DIM)), full((1, LRU_DIM)),
                  full((LRU_HEADS, LRU_BLOCK, LRU_BLOCK)), full((1, LRU_DIM)),
                  full((LRU_HEADS, LRU_BLOCK, LRU_BLOCK)), full((1, LRU_DIM)),
                  full((1, LRU_DIM)), full((LRU_DIM, D_MODEL))],
        out_specs=pl.BlockSpec((1, ts, D_MODEL), lambda b, i: (b, i, 0)),
        scratch_shapes=[pltpu.VMEM((ts + SUBLANES, LRU_DIM), F32),
                        pltpu.VMEM((SUBLANES, LRU_DIM), F32),
                        pltpu.VMEM((ts, LRU_DIM), F32),
                        pltpu.VMEM((ts, LRU_DIM), F32)],
        compiler_params=_cparams(("arbitrary", "arbitrary")),
        name="rg_lru",
    )(p, p, merged, conv_w, row(conv_b), wa_bf, row(ba), wx_bf, row(bx), row(lam), proj_bf)


def _rwkv_prep_kernel(pb_ref, mu_ref, w0_ref, wup_ref, a0_ref, aup_ref, gup_ref, kk_ref, ka_ref,
                      rk_ref, bd_ref, ltri_ref,
                      rt_ref, kkt_ref, kh_ref, bh_ref, v_ref, pinc_ref, bonus_ref, g_ref, ext_ref):
    ts = pb_ref.shape[1]

    @pl.when(pl.program_id(1) == 0)
    def _():
        ext_ref[pl.ds(0, SUBLANES), :] = jnp.zeros((SUBLANES, ext_ref.shape[1]), F32)

    p = pb_ref[0].astype(F32)
    ext_ref[pl.ds(SUBLANES, ts), :] = p
    prev = ext_ref[pl.ds(SUBLANES - 1, ts), :]
    ext_ref[pl.ds(0, SUBLANES), :] = ext_ref[pl.ds(ts, SUBLANES), :]
    pm = p + (prev - p) * mu_ref[...]
    r = pm[:, RW_R:RW_R + RWKV_DIM]
    k = pm[:, RW_K:RW_K + RWKV_DIM]
    v = pm[:, RW_V:RW_V + RWKV_DIM]
    xwa = pm[:, RW_XWA:RW_XWA + LANES]
    xg = pm[:, RW_XG:RW_XG + LORA_G]
    w = -jax.nn.softplus(-(w0_ref[...] + _bdot(jnp.tanh(xwa), wup_ref[...]))) - 0.5
    lw = -jnp.exp(w)
    a = jax.nn.sigmoid(a0_ref[...] + _bdot(xwa, aup_ref[...]))
    g_ref[0] = _bdot(jax.nn.sigmoid(xg), gup_ref[...])
    kkr = k * kk_ref[...]
    ss = _head_sums(kkr * kkr, bd_ref[...])
    kk = kkr / jnp.maximum(jnp.sqrt(ss), 1e-12)
    k2 = k * (1.0 + (a - 1.0) * ka_ref[...])
    lw_hi = lw.astype(BF16)
    lw_mid, lw_lo = _split(lw - lw_hi.astype(F32))
    tri = ltri_ref[...]
    lcum = (jnp.dot(tri, lw_hi, preferred_element_type=F32) + jnp.dot(tri, lw_mid, preferred_element_type=F32)
            + jnp.dot(tri, lw_lo, preferred_element_type=F32))
    pinc = jnp.exp(lcum)
    pinv = jnp.exp(-lcum)
    rt_ref[0] = r * pinc
    kkt_ref[0] = kk * jnp.exp(lcum - lw)
    kh_ref[0] = k2 * pinv
    bh_ref[0] = kk * a * pinv
    v_ref[0] = v
    pinc_ref[0] = pinc
    bonus_ref[0] = _head_sums(r * k2 * rk_ref[...], bd_ref[...]) * v


def _dot3(a, b):
    d = lambda x, y: jnp.dot(x, y, preferred_element_type=F32)
    m = a[0].shape[0]
    both = d(jnp.concatenate([a[0], a[1]], axis=0), b[0])
    return both[:m] + both[m:] + d(a[0], b[1])


def _rwkv_scan_kernel(rt_ref, kkt_ref, kh_ref, bh_ref, v_ref, pinc_ref, y_ref, s_ref):
    c = RWKV_CHUNK
    n = RWKV_HEAD
    nb = rt_ref.shape[0]
    heads = range(nb * RWKV_HEADS)

    @pl.when(pl.program_id(1) == 0)
    def _():
        s_ref[...] = jnp.zeros(s_ref.shape, F32)

    row = lax.broadcasted_iota(I32, (c, c), 0)
    col = lax.broadcasted_iota(I32, (c, c), 1)
    eye = (row == col).astype(F32)
    same16 = (row // 16) == (col // 16)
    same32 = (row // 32) == (col // 32)
    row2 = lax.broadcasted_iota(I32, (c, 2 * c), 0)
    col2 = lax.broadcasted_iota(I32, (c, 2 * c), 1) % c
    nt = lambda a, b: lax.dot_general(a, b, (((1,), (1,)), ((), ())), preferred_element_type=F32)
    tn = lambda a, b: lax.dot_general(a, b, (((0,), (0,)), ((), ())), preferred_element_type=F32)
    dot = lambda a, b: jnp.dot(a, b, preferred_element_type=F32)
    sl = [pl.ds((h % RWKV_HEADS) * n, n) for h in heads]
    sq = [h // RWKV_HEADS for h in heads]
    v = [v_ref[sq[h], :, sl[h]] for h in heads]
    pc = [pinc_ref[sq[h], pl.ds(c - 1, 1), sl[h]] for h in heads]
    s = [s_ref[sq[h], :, sl[h]] for h in heads]
    lhs = [jnp.concatenate([kkt_ref[sq[h], :, sl[h]], rt_ref[sq[h], :, sl[h]]], axis=0) for h in heads]
    rhs = [jnp.concatenate([bh_ref[sq[h], :, sl[h]], kh_ref[sq[h], :, sl[h]]], axis=0) for h in heads]
    big = [nt(lhs[h], rhs[h]) for h in heads]
    from_state = [nt(lhs[h], s[h]) for h in heads]
    top = [jnp.where(row2 > col2, big[h][:c], 0.0) for h in heads]
    bot = [jnp.where(row2 >= col2, big[h][c:], 0.0) for h in heads]
    a_b = [top[h][:, :c] for h in heads]
    akv = [dot(top[h], jnp.concatenate([jnp.zeros((c, n), F32), v[h]], axis=0)) for h in heads]
    d16 = [jnp.where(same16, a_b[h], 0.0) for h in heads]
    sd = [_split(d16[h]) for h in heads]
    s2 = [_split(_dot3(sd[h], sd[h])) for h in heads]
    s4 = [_split(_dot3(s2[h], s2[h])) for h in heads]
    s8 = [_split(_dot3(s4[h], s4[h])) for h in heads]
    t = [eye - d16[h] for h in heads]
    for sp in (s2, s4, s8):
        t = [t[h] + _dot3(_split(t[h]), sp[h]) for h in heads]
    for off in ([jnp.where(same32 & jnp.logical_not(same16), a_b[h], 0.0) for h in heads],
                [jnp.where(same32, 0.0, a_b[h]) for h in heads]):
        st = [_split(t[h]) for h in heads]
        lt = [_split(_dot3(_split(off[h]), st[h])) for h in heads]
        t = [t[h] - _dot3(st[h], lt[h]) for h in heads]
    u = [dot(t[h], from_state[h][:c] + akv[h]) for h in heads]
    vu = [jnp.concatenate([-u[h], v[h]], axis=0) for h in heads]
    y = [from_state[h][c:] + dot(bot[h], vu[h]) for h in heads]
    s_new = [s[h] * pc[h] + tn(vu[h], rhs[h] * pc[h]) for h in heads]
    for q in range(nb):
        mine = slice(q * RWKV_HEADS, (q + 1) * RWKV_HEADS)
        y_ref[q] = jnp.concatenate(y[mine], axis=1)
        s_ref[q] = jnp.concatenate(s_new[mine], axis=1)


def _rwkv_post_kernel(y_ref, bonus_ref, g_ref, pg_ref, m_ref, gg_ref, gb_ref, bdm_ref, pj_ref, o_ref):
    y = y_ref[0]
    mu = _head_sums(y, bdm_ref[...])
    d = y - mu
    var = _head_sums(d * d, bdm_ref[...])
    yn = d * lax.rsqrt(var + RWKV_GN_EPS) * gg_ref[...] + gb_ref[...] + bonus_ref[0]
    o = _bdot(yn * g_ref[0], pj_ref[...])
    o_ref[0] = m_ref[0] + pg_ref[0].astype(F32) * o


def _rwkv(p, merged, mu_pad, w0, wup_pad, a0, aup_pad, g_up, k_k, k_a, r_k, gn_g, gn_b, proj_bf):
    bsz, seq, _ = p.shape
    row = lambda a: a.reshape(1, -1)
    full = lambda shape: pl.BlockSpec(shape, lambda b, i: (0,) * len(shape))
    head_id = jnp.arange(RWKV_DIM, dtype=I32) // RWKV_HEAD
    bd = (head_id[:, None] == head_id[None, :]).astype(BF16)

    ts = min(seq, 256)
    t_id = jnp.arange(ts, dtype=I32)
    ltri = ((t_id[:, None] // RWKV_CHUNK == t_id[None, :] // RWKV_CHUNK)
            & (t_id[:, None] >= t_id[None, :])).astype(BF16)
    seq_blk = lambda width: pl.BlockSpec((1, ts, width), lambda b, i: (b, i, 0))
    wide = SDS((bsz, seq, RWKV_DIM), F32)
    rt, kkt, kh, bh, v, pinc, bonus, g = pl.pallas_call(
        _rwkv_prep_kernel,
        out_shape=[wide] * 8,
        grid=(bsz, seq // ts),
        in_specs=[pl.BlockSpec((1, ts, 2048), lambda b, i: (b, i, SEG_B // 2048)),
                  full((1, 2048)), full((1, RWKV_DIM)), full((LANES, RWKV_DIM)),
                  full((1, RWKV_DIM)), full((LANES, RWKV_DIM)), full((LORA_G, RWKV_DIM)),
                  full((1, RWKV_DIM)), full((1, RWKV_DIM)), full((1, RWKV_DIM)),
                  full((RWKV_DIM, RWKV_DIM)), full((ts, ts))],
        out_specs=[seq_blk(RWKV_DIM)] * 8,
        scratch_shapes=[pltpu.VMEM((ts + SUBLANES, 2048), F32)],
        compiler_params=_cparams(("arbitrary", "arbitrary")),
        name="rwkv_prep",
    )(p, mu_pad, row(w0), wup_pad, row(a0), aup_pad, g_up, row(k_k), row(k_a), row(r_k), bd, ltri)

    c = RWKV_CHUNK
    nb = 2 if bsz % 2 == 0 else 1
    chunk_blk = pl.BlockSpec((nb, c, RWKV_DIM), lambda b, i: (b, i, 0))
    y = pl.pallas_call(
        _rwkv_scan_kernel,
        out_shape=wide,
        grid=(bsz // nb, seq // c),
        in_specs=[chunk_blk] * 6,
        out_specs=chunk_blk,
        scratch_shapes=[pltpu.VMEM((nb, RWKV_HEAD, RWKV_DIM), F32)],
        compiler_params=_cparams(("arbitrary", "arbitrary")),
        name="rwkv_scan",
    )(rt, kkt, kh, bh, v, pinc)

    tp = min(seq, 512)
    blk = lambda width: pl.BlockSpec((1, tp, width), lambda b, i: (b, i, 0))
    return pl.pallas_call(
        _rwkv_post_kernel,
        out_shape=SDS((bsz, seq, D_MODEL), F32),
        grid=(bsz, seq // tp),
        in_specs=[blk(RWKV_DIM), blk(RWKV_DIM), blk(RWKV_DIM),
                  pl.BlockSpec((1, tp, D_MODEL), lambda b, i: (b, i, SEG_G // D_MODEL + 1)),
                  blk(D_MODEL), full((1, RWKV_DIM)), full((1, RWKV_DIM)),
                  full((RWKV_DIM, RWKV_DIM)), full((RWKV_DIM, D_MODEL))],
        out_specs=blk(D_MODEL),
        compiler_params=_cparams(("arbitrary", "arbitrary")),
        name="rwkv_post",
    )(y, bonus, g, p, merged, row(gn_g), row(gn_b), bd * (1.0 / RWKV_HEAD), proj_bf)


def _out_kernel(m_ref, x_ref, g1_ref, w_ref, n2_ref, sc_ref, sh_ref, xo_ref, h_ref):
    xn = x_ref[0] + g1_ref[0] * _bdot(m_ref[0], w_ref[...])
    xo_ref[0] = xn
    h_ref[0] = _modulated_rmsnorm(xn, n2_ref[...], sc_ref[0], sh_ref[0])


def _out_proj(merged, x, g1, w_bf, norm2, sc2, sh2):
    bsz, seq, _ = x.shape
    ts = min(seq, 512)
    blk = pl.BlockSpec((1, ts, D_MODEL), lambda b, i: (b, i, 0))
    per_b = pl.BlockSpec((1, 1, D_MODEL), lambda b, i: (b, 0, 0))
    return pl.pallas_call(
        _out_kernel,
        out_shape=[SDS((bsz, seq, D_MODEL), F32)] * 2,
        grid=(bsz, seq // ts),
        in_specs=[blk, blk, per_b, pl.BlockSpec((D_MODEL, D_MODEL), lambda b, i: (0, 0)),
                  pl.BlockSpec((1, D_MODEL), lambda b, i: (0, 0)), per_b, per_b],
        out_specs=[blk, blk],
        compiler_params=_cparams(("arbitrary", "arbitrary")),
        name="out_proj",
    )(merged, x, g1, w_bf, norm2, sc2, sh2)


MOE_TILE = 256
MOE_SLOTS = 2560
MOE_CHUNK = 32
U32 = jnp.uint32


def _route_kernel(h_ref, rw_ref, bias_ref, upper_ref, ltri_ref, slot_ref, w_ref, n8_ref):
    tile = h_ref.shape[0]
    neg = -jnp.inf
    scores = jax.nn.sigmoid(_hdot_nt(rw_ref[...], h_ref[...]))
    s3 = scores.reshape(GROUP_SIZE, N_GROUPS, tile)
    b3 = s3 + bias_ref[...].reshape(GROUP_SIZE, N_GROUPS, tile)
    slab = lax.broadcasted_iota(I32, b3.shape, 0).astype(F32)
    grp = lax.broadcasted_iota(I32, b3.shape, 1).astype(F32)
    eid = grp * GROUP_SIZE + slab
    m1 = jnp.max(b3, axis=0, keepdims=True)
    first = jnp.min(jnp.where(b3 == m1, slab, GROUP_SIZE), axis=0, keepdims=True)
    m2 = jnp.max(jnp.where(slab == first, neg, b3), axis=0, keepdims=True)
    gs = (m1 + m2)[0]
    gi = lax.broadcasted_iota(I32, gs.shape, 0).astype(F32)
    chosen = jnp.zeros(gs.shape, F32)
    for _ in range(TOPK_GROUPS):
        m = jnp.max(gs, axis=0, keepdims=True)
        hit = gi == jnp.min(jnp.where(gs == m, gi, N_GROUPS), axis=0, keepdims=True)
        chosen = jnp.where(hit, 1.0, chosen)
        gs = jnp.where(hit, neg, gs)
    cur = jnp.where((chosen > 0.0)[None], b3, neg)
    ones_cols = jnp.ones((tile, LANES), BF16)
    lanes_of = lambda a: jnp.concatenate([a] * (tile // LANES), axis=1)
    to3 = lambda a: a.reshape(GROUP_SIZE, N_GROUPS, tile)
    fold = lambda a: jnp.sum(jnp.sum(a, axis=0), axis=0, keepdims=True)
    base = jnp.zeros((N_EXPERTS, LANES), F32)
    picks, w_rows, rank_rows = [], [], []
    for _ in range(TOP_K):
        m = jnp.max(jnp.max(cur, axis=0), axis=0, keepdims=True)[None]
        pick = jnp.min(jnp.min(jnp.where(cur == m, eid, N_EXPERTS), axis=0), axis=0, keepdims=True)
        hit = eid == pick[None]
        w_rows.append(fold(jnp.where(hit, s3, 0.0)))
        picks.append(pick)
        cur = jnp.where(hit, neg, cur)
        onehot = hit.astype(BF16).reshape(N_EXPERTS, tile)
        before = jnp.dot(onehot, upper_ref[...], preferred_element_type=F32)
        rank_rows.append(fold(jnp.where(hit, to3(before + lanes_of(base)), 0.0)))
        base = base + jnp.dot(onehot, ones_cols, preferred_element_type=F32)
    n8 = jnp.floor((base + (SUBLANES - 1.0)) * (1.0 / SUBLANES)) * SUBLANES
    run_start = to3(lanes_of(_hdot(ltri_ref[...], n8)))
    slots = [rank_rows[j] + fold(jnp.where(eid == picks[j][None], run_start, 0.0)) for j in range(TOP_K)]
    n8_ref[0] = n8
    w_all = jnp.concatenate(w_rows, axis=0)
    w_ref[...] = w_all / jnp.sum(w_all, axis=0, keepdims=True) * ROUTED_SCALE
    slot_ref[...] = jnp.concatenate(slots, axis=0).astype(I32)


def _route(h2, router_w, router_bias):
    n_tok = h2.shape[0]
    tile = MOE_TILE
    n_tiles = n_tok // tile
    regroup = lambda a: a.reshape(N_GROUPS, GROUP_SIZE, -1).transpose(1, 0, 2).reshape(N_EXPERTS, -1)
    rw = regroup(router_w.T)
    bias = jnp.broadcast_to(regroup(router_bias.reshape(N_EXPERTS, 1)), (N_EXPERTS, tile))
    t_id = jnp.arange(tile, dtype=I32)
    upper = (t_id[:, None] < t_id[None, :]).astype(BF16)
    e_id = jnp.arange(N_EXPERTS, dtype=I32)
    ltri = (e_id[:, None] > e_id[None, :]).astype(F32)
    tok_blk = pl.BlockSpec((TOP_K, tile), lambda i: (0, i))
    full = lambda shape: pl.BlockSpec(shape, lambda i: (0,) * len(shape))
    slot_t, w_t, n8 = pl.pallas_call(
        _route_kernel,
        out_shape=[SDS((TOP_K, n_tok), I32), SDS((TOP_K, n_tok), F32),
                   SDS((n_tiles, N_EXPERTS, LANES), F32)],
        grid=(n_tiles,),
        in_specs=[pl.BlockSpec((tile, D_MODEL), lambda i: (i, 0)),
                  full((N_EXPERTS, D_MODEL)), full((N_EXPERTS, tile)), full((tile, tile)),
                  full((N_EXPERTS, N_EXPERTS))],
        out_specs=[tok_blk, tok_blk, pl.BlockSpec((1, N_EXPERTS, LANES), lambda i: (i, 0, 0))],
        compiler_params=_cparams(("arbitrary",)),
        name="moe_route",
    )(h2, rw, bias, upper, ltri)
    return slot_t, w_t, n8[:, :, 0].astype(I32)


def _for_each_run_piece(tile_idx, n8_ref, fn):
    def per_expert(r, carry):
        n = n8_ref[tile_idx * N_EXPERTS + r]
        whole = n // MOE_CHUNK

        def chunk(k, c):
            fn(r, k * MOE_CHUNK, MOE_CHUNK)
            return c
        lax.fori_loop(0, whole, chunk, 0)
        rest = whole * MOE_CHUNK
        for size in (16, 8):
            @pl.when((n & size) != 0)
            def _(size=size):
                fn(r, rest + (n & (MOE_CHUNK - 1) & ~(2 * size - 1)), size)
        return carry
    lax.fori_loop(0, N_EXPERTS, per_expert, 0)


def _wait_rows(total, make_wait):
    size = SUBLANES
    while size <= MOE_SLOTS:
        @pl.when((total & size) != 0)
        def _(size=size):
            make_wait(size).wait()
        size *= 2


def _pack_pairs(hi_bits, lo_bits):
    return (hi_bits & jnp.uint32(0xFFFF0000)) | (lo_bits >> 16)


def _unpack_pairs(u):
    hi = lax.bitcast_convert_type(u & jnp.uint32(0xFFFF0000), F32)
    lo = lax.bitcast_convert_type(u << 16, F32)
    return jnp.concatenate([hi, lo], axis=1).astype(BF16)


def _dispatch_kernel(n8_ref, off_ref, dst_ref, tot_ref, pad_end_ref, slot_ref, h_ref, xs_ref,
                     g_ref, zero_ref, sem):
    i = pl.program_id(0)
    tile = h_ref.shape[0]
    half = D_MODEL // 2

    @pl.when(i == 0)
    def _():
        zero_ref[...] = jnp.zeros(zero_ref.shape, U32)

        def last_block(e):
            start = pl.multiple_of(jnp.maximum(pad_end_ref[e] - MOE_ROWS, 0), MOE_ROWS)
            return pltpu.make_async_copy(zero_ref, xs_ref.at[pl.ds(start, MOE_ROWS), :], sem)

        def has_rows(e):
            prev = jnp.where(e > 0, pad_end_ref[jnp.maximum(e - 1, 0)], 0)
            return pad_end_ref[e] > prev

        def clear(e, carry):
            @pl.when(has_rows(e))
            def _():
                last_block(e).start()
            return carry
        lax.fori_loop(0, N_EXPERTS, clear, 0)

        def done(e, carry):
            @pl.when(has_rows(e))
            def _():
                last_block(e).wait()
            return carry
        lax.fori_loop(0, N_EXPERTS, done, 0)

        def tail_block(b):
            start = pl.multiple_of(b * MOE_ROWS, MOE_ROWS)
            return pltpu.make_async_copy(zero_ref, xs_ref.at[pl.ds(start, MOE_ROWS), :], sem)

        def clear_tail(b, carry):
            tail_block(b).start()
            return carry

        def done_tail(b, carry):
            tail_block(b).wait()
            return carry
        used = pad_end_ref[N_EXPERTS - 1] // MOE_ROWS
        lax.fori_loop(used, xs_ref.shape[0] // MOE_ROWS, clear_tail, 0)
        lax.fori_loop(used, xs_ref.shape[0] // MOE_ROWS, done_tail, 0)

    slot_id = lax.broadcasted_iota(I32, (MOE_SLOTS, tile), 0)
    sel = jnp.zeros((MOE_SLOTS, tile), F32)
    for j in range(TOP_K):
        sel = jnp.where(slot_id == slot_ref[pl.ds(j, 1), :], 1.0, sel)
    g = jnp.dot(sel.astype(BF16), h_ref[...].astype(BF16), preferred_element_type=F32)
    bits = lax.bitcast_convert_type(g, U32)
    g_ref[...] = _pack_pairs(bits[:, :half], bits[:, half:])

    def piece(r, offset, rows):
        src = pl.multiple_of(off_ref[i * N_EXPERTS + r] + offset, SUBLANES)
        dst = pl.multiple_of(dst_ref[i * N_EXPERTS + r] + offset, SUBLANES)
        pltpu.make_async_copy(g_ref.at[pl.ds(src, rows), :], xs_ref.at[pl.ds(dst, rows), :], sem).start()

    _for_each_run_piece(i, n8_ref, piece)
    _wait_rows(tot_ref[i], lambda rows: pltpu.make_async_copy(
        g_ref.at[pl.ds(0, rows), :], xs_ref.at[pl.ds(0, rows), :], sem))


def _dispatch(h2, slot_t, n8_flat, off_flat, dst_flat, tot, pad_end, n_rows):
    n_tok = h2.shape[0]
    tile = MOE_TILE
    grid_spec = pltpu.PrefetchScalarGridSpec(
        num_scalar_prefetch=5,
        grid=(n_tok // tile,),
        in_specs=[pl.BlockSpec((TOP_K, tile), lambda i, *_: (0, i)),
                  pl.BlockSpec((tile, D_MODEL), lambda i, *_: (i, 0))],
        out_specs=pl.BlockSpec(memory_space=pl.ANY),
        scratch_shapes=[pltpu.VMEM((MOE_SLOTS, D_MODEL // 2), U32),
                        pltpu.VMEM((MOE_ROWS, D_MODEL // 2), U32), pltpu.SemaphoreType.DMA],
    )
    return pl.pallas_call(
        _dispatch_kernel,
        out_shape=SDS((n_rows, D_MODEL // 2), U32),
        grid_spec=grid_spec,
        compiler_params=_cparams(("arbitrary",)),
        name="moe_dispatch",
    )(n8_flat, off_flat, dst_flat, tot, pad_end, slot_t, h2)


def _expert_kernel(first_ref, nblk_ref, exp_of_ref, xs_ref, w1_ref, w3_ref, w2_ref, y_ref,
                   xbuf, ybuf, wb1, wb3, wb2, in_sem, out_sem):
    del exp_of_ref
    r = pl.program_id(0)
    half = D_MODEL // 2
    used = first_ref[N_EXPERTS - 1] + nblk_ref[N_EXPERTS - 1]
    rows_of = lambda g: pl.ds(pl.multiple_of(g * MOE_ROWS, MOE_ROWS), MOE_ROWS)
    in_copy = lambda g, slot: pltpu.make_async_copy(xs_ref.at[rows_of(g), :], xbuf.at[slot], in_sem.at[slot])
    out_copy = lambda g, slot: pltpu.make_async_copy(ybuf.at[slot], y_ref.at[rows_of(g), :], out_sem.at[slot])

    @pl.when((r == 0) & (used > 0))
    def _():
        in_copy(0, 0).start()

    wb1[...] = w1_ref[0, 0].astype(BF16)
    wb3[...] = w3_ref[0, 0].astype(BF16)
    wb2[...] = w2_ref[0, 0].astype(BF16)

    def block(k, carry):
        g = first_ref[r] + k
        slot = g % 2
        in_copy(g, slot).wait()

        @pl.when(g + 1 < used)
        def _():
            in_copy(g + 1, 1 - slot).start()

        @pl.when(g >= 2)
        def _():
            out_copy(g - 2, slot).wait()

        x = _unpack_pairs(xbuf[slot])
        hid = jax.nn.silu(jnp.dot(x, wb1[...], preferred_element_type=F32)) * jnp.dot(
            x, wb3[...], preferred_element_type=F32)
        y = jnp.dot(hid.astype(BF16), wb2[...], preferred_element_type=F32).astype(BF16).astype(F32)
        bits = lax.bitcast_convert_type(y, U32)
        ybuf[slot] = _pack_pairs(bits[:, :half], bits[:, half:])
        out_copy(g, slot).start()
        return carry
    lax.fori_loop(0, nblk_ref[r], block, 0)

    @pl.when(r == N_EXPERTS - 1)
    def _():
        @pl.when(used >= 2)
        def _():
            out_copy(used - 2, used % 2).wait()

        @pl.when(used >= 1)
        def _():
            out_copy(used - 1, (used - 1) % 2).wait()

        ybuf[0] = jnp.zeros(ybuf.shape[1:], U32)
        n_blocks = y_ref.shape[0] // MOE_ROWS

        def clear(g, carry):
            out_copy(g, 0).start()
            return carry

        def done(g, carry):
            out_copy(g, 0).wait()
            return carry
        lax.fori_loop(used, n_blocks, clear, 0)
        lax.fori_loop(used, n_blocks, done, 0)


def _experts(xs, first_block, n_block, exp_of_row, w1, w3, w2, layer):
    n_rows = xs.shape[0]
    half = D_MODEL // 2
    w_in_blk = pl.BlockSpec((1, 1, D_MODEL, EXPERT_FF), lambda r, first, nblk, eo: (layer, eo[r], 0, 0))
    grid_spec = pltpu.PrefetchScalarGridSpec(
        num_scalar_prefetch=3,
        grid=(N_EXPERTS,),
        in_specs=[pl.BlockSpec(memory_space=pl.ANY), w_in_blk, w_in_blk,
                  pl.BlockSpec((1, 1, EXPERT_FF, D_MODEL), lambda r, first, nblk, eo: (layer, eo[r], 0, 0))],
        out_specs=pl.BlockSpec(memory_space=pl.ANY),
        scratch_shapes=[pltpu.VMEM((2, MOE_ROWS, half), U32), pltpu.VMEM((2, MOE_ROWS, half), U32),
                        pltpu.VMEM((D_MODEL, EXPERT_FF), BF16), pltpu.VMEM((D_MODEL, EXPERT_FF), BF16),
                        pltpu.VMEM((EXPERT_FF, D_MODEL), BF16),
                        pltpu.SemaphoreType.DMA((2,)), pltpu.SemaphoreType.DMA((2,))],
    )
    return pl.pallas_call(
        _expert_kernel,
        out_shape=SDS((n_rows, half), U32),
        grid_spec=grid_spec,
        compiler_params=_cparams(("arbitrary",)),
        name="moe_experts",
    )(first_block, n_block, exp_of_row, xs, w1, w3, w2)


def _combine_kernel(n8_ref, off_ref, src_ref, tot_ref, y_ref, slot_ref, w_ref, h_ref, x_ref, g2_ref,
                    s1_ref, s3_ref, s2_ref, fn_ref, o_ref, yt_ref, sem, *, final):
    tile = h_ref.shape[1]
    i = pl.program_id(0) * pl.num_programs(1) + pl.program_id(1)

    @pl.when(i == 0)
    def _():
        yt_ref[...] = jnp.zeros(yt_ref.shape, U32)

    def piece(r, offset, rows):
        src = pl.multiple_of(src_ref[i * N_EXPERTS + r] + offset, SUBLANES)
        dst = pl.multiple_of(off_ref[i * N_EXPERTS + r] + offset, SUBLANES)
        pltpu.make_async_copy(y_ref.at[pl.ds(src, rows), :], yt_ref.at[pl.ds(dst, rows), :], sem).start()

    _for_each_run_piece(i, n8_ref, piece)
    h = h_ref[0]
    shared = _bdot(jax.nn.silu(_bdot(h, s1_ref[...])) * _bdot(h, s3_ref[...]), s2_ref[...])
    slot_id = lax.broadcasted_iota(I32, (tile, MOE_SLOTS), 1)
    slots = slot_ref[0]
    w = w_ref[0]
    pw = jnp.zeros((tile, MOE_SLOTS), F32)
    for j in range(TOP_K):
        pw = jnp.where(slot_id == slots[:, j:j + 1], w[:, j:j + 1], pw)
    _wait_rows(tot_ref[i], lambda rows: pltpu.make_async_copy(
        y_ref.at[pl.ds(0, rows), :], yt_ref.at[pl.ds(0, rows), :], sem))
    routed = jnp.dot(pw.astype(BF16), _unpack_pairs(yt_ref[...]), preferred_element_type=F32)
    xn = x_ref[0] + g2_ref[0] * (routed + shared)
    if final:
        xn = xn * lax.rsqrt(jnp.mean(xn * xn, axis=-1, keepdims=True) + NORM_EPS) * fn_ref[...]
    o_ref[0] = xn


def _combine(y, slot_nat, w_nat, n8_flat, off_flat, dst_flat, tot, h2, x, g2, s1_bf, s3_bf, s2_bf,
             final_norm, final):
    bsz, seq, _ = x.shape
    tile = MOE_TILE
    per_seq = seq // tile
    blk = pl.BlockSpec((1, tile, D_MODEL), lambda b, i, *_: (b, i, 0))
    tok = pl.BlockSpec((1, tile, TOP_K), lambda b, i, *_: (b, i, 0))
    full = lambda shape: pl.BlockSpec(shape, lambda b, i, *_: (0,) * len(shape))
    grid_spec = pltpu.PrefetchScalarGridSpec(
        num_scalar_prefetch=4,
        grid=(bsz, per_seq),
        in_specs=[pl.BlockSpec(memory_space=pl.ANY), tok, tok, blk, blk,
                  pl.BlockSpec((1, 1, D_MODEL), lambda b, i, *_: (b, 0, 0)),
                  full((D_MODEL, SHARED_FF)), full((D_MODEL, SHARED_FF)), full((SHARED_FF, D_MODEL)),
                  full((1, D_MODEL))],
        out_specs=blk,
        scratch_shapes=[pltpu.VMEM((MOE_SLOTS, D_MODEL // 2), U32), pltpu.SemaphoreType.DMA],
    )
    return pl.pallas_call(
        functools.partial(_combine_kernel, final=final),
        out_shape=SDS((bsz, seq, D_MODEL), F32),
        grid_spec=grid_spec,
        compiler_params=_cparams(("arbitrary", "arbitrary")),
        name="moe_combine",
    )(n8_flat, off_flat, dst_flat, tot, y, slot_nat, w_nat, h2, x, g2, s1_bf, s3_bf, s2_bf, final_norm)


def _moe(x, h2, g2, router_w, router_bias, w1, w3, w2, layer, s1_bf, s3_bf, s2_bf, final_norm, final):
    bsz, seq, _ = x.shape
    assert seq % MOE_TILE == 0, "token tiles must not straddle sequences"
    n_tok = bsz * seq
    slot_t, w_t, n8 = _route(h2.reshape(n_tok, D_MODEL), router_w, router_bias)
    n_tiles = n8.shape[0]
    counts = jnp.sum(n8, axis=0)
    padded = (counts + MOE_ROWS - 1) // MOE_ROWS * MOE_ROWS
    pad_end = jnp.cumsum(padded).astype(I32)
    pad_start = pad_end - padded
    run_row = (pad_start[None, :] + jnp.cumsum(n8, axis=0) - n8).astype(I32)
    run_slot = (jnp.cumsum(n8, axis=1) - n8).astype(I32)
    tot = jnp.sum(n8, axis=1).astype(I32)
    max_rows = n_tok * TOP_K + n_tiles * N_EXPERTS * (SUBLANES - 1) + N_EXPERTS * (MOE_ROWS - 1)
    n_blocks = (max_rows + MOE_ROWS - 1) // MOE_ROWS
    row_id = jnp.arange(N_EXPERTS, dtype=I32)
    exp_of_row = (row_id % N_GROUPS) * GROUP_SIZE + row_id // N_GROUPS
    flat = lambda a: a.reshape(-1).astype(I32)
    xs = _dispatch(h2.reshape(n_tok, D_MODEL), slot_t, flat(n8), flat(run_slot), flat(run_row), tot,
                   pad_end, n_blocks * MOE_ROWS)
    y = _experts(xs, (pad_start // MOE_ROWS).astype(I32), (padded // MOE_ROWS).astype(I32), exp_of_row,
                 w1, w3, w2, layer)
    nat = lambda a: a.T.reshape(bsz, seq, TOP_K)
    return _combine(y, nat(slot_t), nat(w_t), flat(n8), flat(run_slot), flat(run_row), tot, h2, x, g2,
                    s1_bf, s3_bf, s2_bf, final_norm, final)


def _pad_cols(a, width):
    return jnp.pad(a, ((0, 0), (0, width - a.shape[1])))


def _layout_w_in(w_in):
    b0 = COLS_A
    c0 = COLS_A + COLS_B
    g0 = c0 + COLS_C
    seg_b = _pad_cols(w_in[:, b0:c0], SEG_G - SEG_B)
    return jnp.concatenate([w_in[:, c0:g0], seg_b, w_in[:, g0:], w_in[:, :b0]], axis=1).astype(BF16)


def _layout_mu(mu):
    return _pad_cols(mu.reshape(1, -1), SEG_G - SEG_B)


def _pad_rows(a, height):
    return jnp.pad(a, ((0, height - a.shape[0]), (0, 0)))


def kernel(x, c, ada_w, ada_b, norm1, norm2, w_in, conv_a_w, conv_a_b, ln_a_g, ln_a_b, proj_a, mu_b, w0, w_up, a0, a_up, g_up, k_k, k_a, r_k, gn_b_g, gn_b_b, proj_b, conv_c_w, conv_c_b, lru_wa, lru_ba, lru_wx, lru_bx, lru_lambda, proj_c, w_out, router_w, router_bias, exp_w1, exp_w3, exp_w2, sh_w1, sh_w3, sh_w2, final_norm):
    depth = ada_w.shape[0]
    bsz = x.shape[0]
    mod = _ada_mod(c, ada_w, ada_b)
    for l in range(depth):
        sh1, sc1, g1, sh2, sc2, g2 = [mod[l, :, i * D_MODEL:(i + 1) * D_MODEL].reshape(bsz, 1, D_MODEL)
                                      for i in range(N_MOD)]
        p = _in_proj(x, sc1, sh1, norm1[l].reshape(1, -1), _layout_w_in(w_in[l]))
        merged = _conv_a(p, conv_a_w[l], conv_a_b[l], ln_a_g[l], ln_a_b[l], proj_a[l].astype(BF16))
        merged = _rwkv(p, merged, _layout_mu(mu_b[l]), w0[l], _pad_rows(w_up[l], LANES).astype(BF16),
                       a0[l], jnp.pad(a_up[l], ((LORA_W, 0), (0, 0))).astype(BF16), g_up[l].astype(BF16),
                       k_k[l], k_a[l], r_k[l], gn_b_g[l], gn_b_b[l], proj_b[l].astype(BF16))
        merged = _lru(p, merged, conv_c_w[l], conv_c_b[l], lru_wa[l].astype(BF16), lru_ba[l],
                      lru_wx[l].astype(BF16), lru_bx[l], lru_lambda[l], proj_c[l].astype(BF16))
        x, h2 = _out_proj(merged, x, g1, w_out[l].astype(BF16), norm2[l].reshape(1, -1), sc2, sh2)
        x = _moe(x, h2, g2, router_w[l], router_bias[l], exp_w1, exp_w3, exp_w2, l,
                 sh_w1[l].astype(BF16), sh_w3[l].astype(BF16), sh_w2[l].astype(BF16),
                 final_norm.reshape(1, -1), final=(l == depth - 1))
    return x
```

```python
import functools

import jax
import jax.numpy as jnp
from jax import lax
from jax.experimental import pallas as pl
from jax.experimental.pallas import tpu as pltpu

F32 = jnp.float32
BF16 = jnp.bfloat16
I32 = jnp.int32
SDS = jax.ShapeDtypeStruct
HIGHEST = lax.Precision.HIGHEST

D_MODEL = 1024
N_MOD = 6
NORM_EPS = 1e-6
CONV_A_CH = 512
CONV_A_WIDTH = 31
CONV_A_LN_EPS = 1e-5
RWKV_HEADS = 8
RWKV_HEAD = 64
RWKV_DIM = RWKV_HEADS * RWKV_HEAD
LORA_W = 64
LORA_A = 64
LORA_G = 128
RWKV_GN_EPS = 64e-5
RWKV_CHUNK = 64
LRU_DIM = 1024
LRU_HEADS = 8
LRU_BLOCK = LRU_DIM // LRU_HEADS
LRU_CONV = 4
LRU_C = 8.0
N_EXPERTS = 64
TOP_K = 8
N_GROUPS = 8
GROUP_SIZE = N_EXPERTS // N_GROUPS
TOPK_GROUPS = 4
EXPERT_FF = 256
SHARED_FF = 256
ROUTED_SCALE = 2.5
SEG_C = 0
SEG_B = 2048
SEG_G = 4096
SEG_A = 7168
IN_COLS_PAD = 8192
RW_R, RW_K, RW_V, RW_XWA, RW_XG = 0, 512, 1024, 1536, 1664
COLS_A = 2 * CONV_A_CH
COLS_B = 3 * RWKV_DIM + LORA_W + LORA_A + LORA_G
COLS_C = 2 * LRU_DIM
VMEM_LIMIT = 56 * 1024 * 1024
SUBLANES = 8
LANES = 128
MOE_ROWS = 512
EXPERT_DMA_PARTS = 4


def _cparams(sem):
    return pltpu.CompilerParams(dimension_semantics=sem, vmem_limit_bytes=VMEM_LIMIT)


def _bdot(a, b):
    return jnp.dot(a.astype(BF16), b.astype(BF16), preferred_element_type=F32)


def _hdot(a, b):
    return jnp.dot(a, b, preferred_element_type=F32, precision=HIGHEST)


def _split(a):
    hi = a.astype(BF16)
    return hi, (a - hi.astype(F32)).astype(BF16)


def _head_sums(a, ones_bf):
    hi, lo = _split(a)
    return (jnp.dot(hi, ones_bf, preferred_element_type=F32)
            + jnp.dot(lo, ones_bf, preferred_element_type=F32))


def _hdot_nt(a, b):
    return lax.dot_general(a, b, (((1,), (1,)), ((), ())), preferred_element_type=F32,
                           precision=HIGHEST)


def _hdot_tn(a, b):
    return lax.dot_general(a, b, (((0,), (0,)), ((), ())), preferred_element_type=F32,
                           precision=HIGHEST)


def _ada_kernel(c_ref, w_ref, b_ref, o_ref):
    cond = jax.nn.silu(c_ref[...])
    o_ref[0] = _bdot(cond, w_ref[0]) + b_ref[0]


def _ada_mod(c, ada_w, ada_b):
    depth, _, n = ada_w.shape
    bsz = c.shape[0]
    tn = 1536
    return pl.pallas_call(
        _ada_kernel,
        out_shape=SDS((depth, bsz, n), F32),
        grid=(depth, n // tn),
        in_specs=[pl.BlockSpec((bsz, D_MODEL), lambda l, j: (0, 0)),
                  pl.BlockSpec((1, D_MODEL, tn), lambda l, j: (l, 0, j)),
                  pl.BlockSpec((1, 1, tn), lambda l, j: (l, 0, j))],
        out_specs=pl.BlockSpec((1, bsz, tn), lambda l, j: (l, 0, j)),
        compiler_params=_cparams(("arbitrary", "arbitrary")),
        name="ada_mod",
    )(c, ada_w, ada_b.reshape(depth, 1, n))


def _modulated_rmsnorm(x, g, sc, sh):
    y = x * lax.rsqrt(jnp.mean(x * x, axis=-1, keepdims=True) + NORM_EPS)
    return (y * g) * (1.0 + sc) + sh


IN_PROJ_SUB = 512


def _in_kernel(x_ref, sc_ref, sh_ref, g_ref, w_ref, o_ref, h_ref):
    @pl.when(pl.program_id(2) == 0)
    def _():
        h_ref[...] = _modulated_rmsnorm(x_ref[0], g_ref[...], sc_ref[0], sh_ref[0]).astype(BF16)

    j = pl.program_id(2)
    ts, tn = o_ref.shape[1], o_ref.shape[2]
    is_gelu = j == SEG_C // tn
    is_gate = (j >= SEG_G // tn) & (j < SEG_A // tn)
    is_glu = j == SEG_A // tn

    def glu(acc):
        u = acc[:, :tn // 2] * jax.nn.sigmoid(acc[:, tn // 2:])
        return jnp.concatenate([u, jnp.zeros_like(u)], axis=1)

    def emit(act, sub):
        for r0 in range(0, ts, sub):
            rows = pl.ds(r0, sub)
            acc = jnp.dot(h_ref[rows, :], w_ref[...], preferred_element_type=F32)
            o_ref[0, rows, :] = act(acc).astype(o_ref.dtype)

    sub = min(ts, IN_PROJ_SUB)
    pl.when(is_gelu)(lambda: emit(functools.partial(jax.nn.gelu, approximate=True), sub))
    pl.when(is_gate)(lambda: emit(jax.nn.sigmoid, sub))
    pl.when(is_glu)(lambda: emit(glu, sub))
    pl.when(jnp.logical_not(is_gelu | is_gate | is_glu))(lambda: emit(lambda acc: acc, ts))


def _in_proj(x, sc, sh, g, w_pad):
    bsz, seq, _ = x.shape
    ts = min(seq, 1024)
    tn = 1024
    return pl.pallas_call(
        _in_kernel,
        out_shape=SDS((bsz, seq, IN_COLS_PAD), BF16),
        grid=(bsz, seq // ts, IN_COLS_PAD // tn),
        in_specs=[pl.BlockSpec((1, ts, D_MODEL), lambda b, i, j: (b, i, 0)),
                  pl.BlockSpec((1, 1, D_MODEL), lambda b, i, j: (b, 0, 0)),
                  pl.BlockSpec((1, 1, D_MODEL), lambda b, i, j: (b, 0, 0)),
                  pl.BlockSpec((1, D_MODEL), lambda b, i, j: (0, 0)),
                  pl.BlockSpec((D_MODEL, tn), lambda b, i, j: (0, j))],
        out_specs=pl.BlockSpec((1, ts, tn), lambda b, i, j: (b, i, j)),
        scratch_shapes=[pltpu.VMEM((ts, D_MODEL), BF16)],
        compiler_params=_cparams(("arbitrary", "arbitrary", "arbitrary")),
        name="in_proj",
    )(x, sc, sh, g, w_pad)


CONV_A_HALO = 32
CONV_A_SUB = 64


def _conv_a_kernel(pa_ref, pg_ref, cw_ref, cb_ref, lg_ref, lb_ref, pj_ref, o_ref, ext_ref, sh_ref, y_ref):
    ts = pa_ref.shape[1]

    @pl.when(pl.program_id(1) == 0)
    def _():
        ext_ref[pl.ds(0, CONV_A_HALO), :] = jnp.zeros((CONV_A_HALO, CONV_A_CH), F32)

    ext_ref[pl.ds(CONV_A_HALO, ts), :] = pa_ref[0].astype(F32)
    for p in range(1, SUBLANES):
        sh_ref[p - 1] = ext_ref[pl.ds(p, sh_ref.shape[1]), :]
    first = CONV_A_HALO - (CONV_A_WIDTH - 1)
    for r0 in range(0, ts, CONV_A_SUB):
        acc = jnp.zeros((CONV_A_SUB, CONV_A_CH), F32) + cb_ref[...]
        for j in range(CONV_A_WIDTH):
            phase = (first + j) % SUBLANES
            rows = pl.ds(r0 + first + j - phase, CONV_A_SUB)
            tap = ext_ref[rows, :] if phase == 0 else sh_ref[phase - 1, rows, :]
            acc = acc + tap * cw_ref[pl.ds(j, 1), :]
        y_ref[pl.ds(r0, CONV_A_SUB), :] = acc
    ext_ref[pl.ds(0, CONV_A_HALO), :] = ext_ref[pl.ds(ts, CONV_A_HALO), :]
    y = y_ref[...]
    mu = jnp.mean(y, axis=-1, keepdims=True)
    d = y - mu
    var = jnp.mean(d * d, axis=-1, keepdims=True)
    yn = d * lax.rsqrt(var + CONV_A_LN_EPS) * lg_ref[...] + lb_ref[...]
    o = _bdot(jax.nn.silu(yn), pj_ref[...])
    o_ref[0] = pg_ref[0].astype(F32) * o


def _conv_a(p, conv_w, conv_b, ln_g, ln_b, proj_bf):
    bsz, seq, _ = p.shape
    ts = min(seq, 512)
    row = lambda a: a.reshape(1, -1)
    full = lambda shape: pl.BlockSpec(shape, lambda b, i: (0,) * len(shape))
    return pl.pallas_call(
        _conv_a_kernel,
        out_shape=SDS((bsz, seq, D_MODEL), F32),
        grid=(bsz, seq // ts),
        in_specs=[pl.BlockSpec((1, ts, CONV_A_CH), lambda b, i: (b, i, SEG_A // CONV_A_CH)),
                  pl.BlockSpec((1, ts, D_MODEL), lambda b, i: (b, i, SEG_G // D_MODEL)),
                  full((CONV_A_WIDTH, CONV_A_CH)), full((1, CONV_A_CH)), full((1, CONV_A_CH)),
                  full((1, CONV_A_CH)), full((CONV_A_CH, D_MODEL))],
        out_specs=pl.BlockSpec((1, ts, D_MODEL), lambda b, i: (b, i, 0)),
        scratch_shapes=[pltpu.VMEM((ts + CONV_A_HALO, CONV_A_CH), F32),
                        pltpu.VMEM((SUBLANES - 1, ts + CONV_A_HALO - SUBLANES, CONV_A_CH), F32),
                        pltpu.VMEM((ts, CONV_A_CH), F32)],
        compiler_params=_cparams(("arbitrary", "arbitrary")),
        name="conv_a",
    )(p, p, conv_w, row(conv_b), row(ln_g), row(ln_b), proj_bf)


def _lru_kernel(pc_ref, pg_ref, m_ref, cw_ref, cb_ref, wa_ref, ba_ref, wx_ref, bx_ref, lam_ref,
                pj_ref, o_ref, ext_ref, h_ref, a_ref, b_ref):
    ts = pc_ref.shape[1]
    groups = ts // SUBLANES

    @pl.when(pl.program_id(1) == 0)
    def _():
        ext_ref[pl.ds(0, SUBLANES), :] = jnp.zeros((SUBLANES, LRU_DIM), F32)
        h_ref[...] = jnp.zeros((SUBLANES, LRU_DIM), F32)

    pc = pc_ref[0].astype(F32)
    y_gate = pc[:, :LRU_DIM]
    ext_ref[pl.ds(SUBLANES, ts), :] = pc[:, LRU_DIM:]
    first = SUBLANES - (LRU_CONV - 1)
    xc = jnp.zeros((ts, LRU_DIM), F32) + cb_ref[...]
    for j in range(LRU_CONV):
        xc = xc + ext_ref[pl.ds(first + j, ts), :] * cw_ref[pl.ds(j, 1), :]
    ext_ref[pl.ds(0, SUBLANES), :] = ext_ref[pl.ds(ts, SUBLANES), :]

    def block_diag(w_ref):
        return jnp.concatenate(
            [_bdot(xc[:, h * LRU_BLOCK:(h + 1) * LRU_BLOCK], w_ref[h]) for h in range(LRU_HEADS)],
            axis=1)

    gate_a = jax.nn.sigmoid(block_diag(wa_ref) + ba_ref[...])
    gate_x = jax.nn.sigmoid(block_diag(wx_ref) + bx_ref[...])
    log_a = -LRU_C * gate_a * jax.nn.softplus(-lam_ref[...])
    a = jnp.exp(log_a)
    b = xc * gate_x * jnp.sqrt(1.0 - jnp.exp(2.0 * log_a))

    a3 = a.reshape(groups, SUBLANES, LRU_DIM)
    b3 = b.reshape(groups, SUBLANES, LRU_DIM)
    row = lax.broadcasted_iota(I32, (groups, SUBLANES, LRU_DIM), 1)
    for s in (1, 2, 4):
        keep = row >= s
        b3 = jnp.where(keep, a3 * pltpu.roll(b3, s, axis=1) + b3, b3)
        a3 = jnp.where(keep, a3 * pltpu.roll(a3, s, axis=1), a3)
    a_ref[...] = a3.reshape(ts, LRU_DIM)
    b_ref[...] = b3.reshape(ts, LRU_DIM)
    h = h_ref[...]
    for g in range(groups):
        rows = pl.ds(g * SUBLANES, SUBLANES)
        hg = a_ref[rows, :] * h + b_ref[rows, :]
        b_ref[rows, :] = hg
        h = jnp.broadcast_to(hg[SUBLANES - 1:SUBLANES, :], (SUBLANES, LRU_DIM))
    h_ref[...] = h
    o = _bdot(b_ref[...] * y_gate, pj_ref[...])
    o_ref[0] = m_ref[0] + pg_ref[0].astype(F32) * o


def _lru(p, merged, conv_w, conv_b, wa_bf, ba, wx_bf, bx, lam, proj_bf):
    bsz, seq, _ = p.shape
    ts = min(seq, 256)
    row = lambda a: a.reshape(1, -1)
    full = lambda shape: pl.BlockSpec(shape, lambda b, i: (0,) * len(shape))
    return pl.pallas_call(
        _lru_kernel,
        out_shape=SDS((bsz, seq, D_MODEL), F32),
        grid=(bsz, seq // ts),
        in_specs=[pl.BlockSpec((1, ts, 2 * LRU_DIM), lambda b, i: (b, i, SEG_C // (2 * LRU_DIM))),
                  pl.BlockSpec((1, ts, D_MODEL), lambda b, i: (b, i, SEG_G // D_MODEL + 2)),
                  pl.BlockSpec((1, ts, D_MODEL), lambda b, i: (b, i, 0)),
                  full((LRU_CONV, LRU_DIM)), full((1, LRU_DIM)),
                  full((LRU_HEADS, LRU_BLOCK, LRU_BLOCK)), full((1, LRU_DIM)),
                  full((LRU_HEADS, LRU_BLOCK, LRU_BLOCK)), full((1, LRU_DIM)),
                  full((1, LRU_DIM)), full((LRU_DIM, D_MODEL))],
        out_specs=pl.BlockSpec((1, ts, D_MODEL), lambda b, i: (b, i, 0)),
        scratch_shapes=[pltpu.VMEM((ts + SUBLANES, LRU_DIM), F32),
                        pltpu.VMEM((SUBLANES, LRU_DIM), F32),
                        pltpu.VMEM((ts, LRU_DIM), F32),
                        pltpu.VMEM((ts, LRU_DIM), F32)],
        compiler_params=_cparams(("arbitrary", "arbitrary")),
        name="rg_lru",
    )(p, p, merged, conv_w, row(conv_b), wa_bf, row(ba), wx_bf, row(bx), row(lam), proj_bf)


def _rwkv_prep_kernel(pb_ref, mu_ref, w0_ref, wup_ref, a0_ref, aup_ref, gup_ref, kk_ref, ka_ref,
                      rk_ref, bd_ref, ltri_ref,
                      rt_ref, kkt_ref, kh_ref, bh_ref, v_ref, pinc_ref, bonus_ref, g_ref, ext_ref):
    ts = pb_ref.shape[1]

    @pl.when(pl.program_id(1) == 0)
    def _():
        ext_ref[pl.ds(0, SUBLANES), :] = jnp.zeros((SUBLANES, ext_ref.shape[1]), F32)

    p = pb_ref[0].astype(F32)
    ext_ref[pl.ds(SUBLANES, ts), :] = p
    prev = ext_ref[pl.ds(SUBLANES - 1, ts), :]
    ext_ref[pl.ds(0, SUBLANES), :] = ext_ref[pl.ds(ts, SUBLANES), :]
    pm = p + (prev - p) * mu_ref[...]
    r = pm[:, RW_R:RW_R + RWKV_DIM]
    k = pm[:, RW_K:RW_K + RWKV_DIM]
    v = pm[:, RW_V:RW_V + RWKV_DIM]
    xwa = pm[:, RW_XWA:RW_XWA + LANES]
    xg = pm[:, RW_XG:RW_XG + LORA_G]
    w = -jax.nn.softplus(-(w0_ref[...] + _bdot(jnp.tanh(xwa), wup_ref[...]))) - 0.5
    lw = -jnp.exp(w)
    a = jax.nn.sigmoid(a0_ref[...] + _bdot(xwa, aup_ref[...]))
    g_ref[0] = _bdot(jax.nn.sigmoid(xg), gup_ref[...])
    kkr = k * kk_ref[...]
    ss = _head_sums(kkr * kkr, bd_ref[...])
    kk = kkr / jnp.maximum(jnp.sqrt(ss), 1e-12)
    k2 = k * (1.0 + (a - 1.0) * ka_ref[...])
    lw_hi = lw.astype(BF16)
    lw_mid, lw_lo = _split(lw - lw_hi.astype(F32))
    tri = ltri_ref[...]
    lcum = (jnp.dot(tri, lw_hi, preferred_element_type=F32) + jnp.dot(tri, lw_mid, preferred_element_type=F32)
            + jnp.dot(tri, lw_lo, preferred_element_type=F32))
    pinc = jnp.exp(lcum)
    pinv = jnp.exp(-lcum)
    rt_ref[0] = r * pinc
    kkt_ref[0] = kk * jnp.exp(lcum - lw)
    kh_ref[0] = k2 * pinv
    bh_ref[0] = kk * a * pinv
    v_ref[0] = v
    pinc_ref[0] = pinc
    bonus_ref[0] = _head_sums(r * k2 * rk_ref[...], bd_ref[...]) * v


def _dot3(a, b):
    d = lambda x, y: jnp.dot(x, y, preferred_element_type=F32)
    m = a[0].shape[0]
    both = d(jnp.concatenate([a[0], a[1]], axis=0), b[0])
    return both[:m] + both[m:] + d(a[0], b[1])


def _rwkv_scan_kernel(rt_ref, kkt_ref, kh_ref, bh_ref, v_ref, pinc_ref, y_ref, s_ref):
    c = RWKV_CHUNK
    n = RWKV_HEAD
    nb = rt_ref.shape[0]
    heads = range(nb * RWKV_HEADS)

    @pl.when(pl.program_id(1) == 0)
    def _():
        s_ref[...] = jnp.zeros(s_ref.shape, F32)

    row = lax.broadcasted_iota(I32, (c, c), 0)
    col = lax.broadcasted_iota(I32, (c, c), 1)
    eye = (row == col).astype(F32)
    same16 = (row // 16) == (col // 16)
    same32 = (row // 32) == (col // 32)
    row2 = lax.broadcasted_iota(I32, (c, 2 * c), 0)
    col2 = lax.broadcasted_iota(I32, (c, 2 * c), 1) % c
    nt = lambda a, b: lax.dot_general(a, b, (((1,), (1,)), ((), ())), preferred_element_type=F32)
    tn = lambda a, b: lax.dot_general(a, b, (((0,), (0,)), ((), ())), preferred_element_type=F32)
    dot = lambda a, b: jnp.dot(a, b, preferred_element_type=F32)
    sl = [pl.ds((h % RWKV_HEADS) * n, n) for h in heads]
    sq = [h // RWKV_HEADS for h in heads]
    v = [v_ref[sq[h], :, sl[h]] for h in heads]
    pc = [pinc_ref[sq[h], pl.ds(c - 1, 1), sl[h]] for h in heads]
    s = [s_ref[sq[h], :, sl[h]] for h in heads]
    lhs = [jnp.concatenate([kkt_ref[sq[h], :, sl[h]], rt_ref[sq[h], :, sl[h]]], axis=0) for h in heads]
    rhs = [jnp.concatenate([bh_ref[sq[h], :, sl[h]], kh_ref[sq[h], :, sl[h]]], axis=0) for h in heads]
    big = [nt(lhs[h], rhs[h]) for h in heads]
    from_state = [nt(lhs[h], s[h]) for h in heads]
    top = [jnp.where(row2 > col2, big[h][:c], 0.0) for h in heads]
    bot = [jnp.where(row2 >= col2, big[h][c:], 0.0) for h in heads]
    a_b = [top[h][:, :c] for h in heads]
    akv = [dot(top[h], jnp.concatenate([jnp.zeros((c, n), F32), v[h]], axis=0)) for h in heads]
    d16 = [jnp.where(same16, a_b[h], 0.0) for h in heads]
    sd = [_split(d16[h]) for h in heads]
    s2 = [_split(_dot3(sd[h], sd[h])) for h in heads]
    s4 = [_split(_dot3(s2[h], s2[h])) for h in heads]
    s8 = [_split(_dot3(s4[h], s4[h])) for h in heads]
    t = [eye - d16[h] for h in heads]
    for sp in (s2, s4, s8):
        t = [t[h] + _dot3(_split(t[h]), sp[h]) for h in heads]
    for off in ([jnp.where(same32 & jnp.logical_not(same16), a_b[h], 0.0) for h in heads],
                [jnp.where(same32, 0.0, a_b[h]) for h in heads]):
        st = [_split(t[h]) for h in heads]
        lt = [_split(_dot3(_split(off[h]), st[h])) for h in heads]
        t = [t[h] - _dot3(st[h], lt[h]) for h in heads]
    u = [dot(t[h], from_state[h][:c] + akv[h]) for h in heads]
    vu = [jnp.concatenate([-u[h], v[h]], axis=0) for h in heads]
    y = [from_state[h][c:] + dot(bot[h], vu[h]) for h in heads]
    s_new = [s[h] * pc[h] + tn(vu[h], rhs[h] * pc[h]) for h in heads]
    for q in range(nb):
        mine = slice(q * RWKV_HEADS, (q + 1) * RWKV_HEADS)
        y_ref[q] = jnp.concatenate(y[mine], axis=1)
        s_ref[q] = jnp.concatenate(s_new[mine], axis=1)


def _rwkv_post_kernel(y_ref, bonus_ref, g_ref, pg_ref, m_ref, gg_ref, gb_ref, bdm_ref, pj_ref, o_ref):
    y = y_ref[0]
    mu = _head_sums(y, bdm_ref[...])
    d = y - mu
    var = _head_sums(d * d, bdm_ref[...])
    yn = d * lax.rsqrt(var + RWKV_GN_EPS) * gg_ref[...] + gb_ref[...] + bonus_ref[0]
    o = _bdot(yn * g_ref[0], pj_ref[...])
    o_ref[0] = m_ref[0] + pg_ref[0].astype(F32) * o


def _rwkv(p, merged, mu_pad, w0, wup_pad, a0, aup_pad, g_up, k_k, k_a, r_k, gn_g, gn_b, proj_bf):
    bsz, seq, _ = p.shape
    row = lambda a: a.reshape(1, -1)
    full = lambda shape: pl.BlockSpec(shape, lambda b, i: (0,) * len(shape))
    head_id = jnp.arange(RWKV_DIM, dtype=I32) // RWKV_HEAD
    bd = (head_id[:, None] == head_id[None, :]).astype(BF16)

    ts = min(seq, 256)
    t_id = jnp.arange(ts, dtype=I32)
    ltri = ((t_id[:, None] // RWKV_CHUNK == t_id[None, :] // RWKV_CHUNK)
            & (t_id[:, None] >= t_id[None, :])).astype(BF16)
    seq_blk = lambda width: pl.BlockSpec((1, ts, width), lambda b, i: (b, i, 0))
    wide = SDS((bsz, seq, RWKV_DIM), F32)
    rt, kkt, kh, bh, v, pinc, bonus, g = pl.pallas_call(
        _rwkv_prep_kernel,
        out_shape=[wide] * 8,
        grid=(bsz, seq // ts),
        in_specs=[pl.BlockSpec((1, ts, 2048), lambda b, i: (b, i, SEG_B // 2048)),
                  full((1, 2048)), full((1, RWKV_DIM)), full((LANES, RWKV_DIM)),
                  full((1, RWKV_DIM)), full((LANES, RWKV_DIM)), full((LORA_G, RWKV_DIM)),
                  full((1, RWKV_DIM)), full((1, RWKV_DIM)), full((1, RWKV_DIM)),
                  full((RWKV_DIM, RWKV_DIM)), full((ts, ts))],
        out_specs=[seq_blk(RWKV_DIM)] * 8,
        scratch_shapes=[pltpu.VMEM((ts + SUBLANES, 2048), F32)],
        compiler_params=_cparams(("arbitrary", "arbitrary")),
        name="rwkv_prep",
    )(p, mu_pad, row(w0), wup_pad, row(a0), aup_pad, g_up, row(k_k), row(k_a), row(r_k), bd, ltri)

    c = RWKV_CHUNK
    nb = 2 if bsz % 2 == 0 else 1
    chunk_blk = pl.BlockSpec((nb, c, RWKV_DIM), lambda b, i: (b, i, 0))
    y = pl.pallas_call(
        _rwkv_scan_kernel,
        out_shape=wide,
        grid=(bsz // nb, seq // c),
        in_specs=[chunk_blk] * 6,
        out_specs=chunk_blk,
        scratch_shapes=[pltpu.VMEM((nb, RWKV_HEAD, RWKV_DIM), F32)],
        compiler_params=_cparams(("arbitrary", "arbitrary")),
        name="rwkv_scan",
    )(rt, kkt, kh, bh, v, pinc)

    tp = min(seq, 512)
    blk = lambda width: pl.BlockSpec((1, tp, width), lambda b, i: (b, i, 0))
    return pl.pallas_call(
        _rwkv_post_kernel,
        out_shape=SDS((bsz, seq, D_MODEL), F32),
        grid=(bsz, seq // tp),
        in_specs=[blk(RWKV_DIM), blk(RWKV_DIM), blk(RWKV_DIM),
                  pl.BlockSpec((1, tp, D_MODEL), lambda b, i: (b, i, SEG_G // D_MODEL + 1)),
                  blk(D_MODEL), full((1, RWKV_DIM)), full((1, RWKV_DIM)),
                  full((RWKV_DIM, RWKV_DIM)), full((RWKV_DIM, D_MODEL))],
        out_specs=blk(D_MODEL),
        compiler_params=_cparams(("arbitrary", "arbitrary")),
        name="rwkv_post",
    )(y, bonus, g, p, merged, row(gn_g), row(gn_b), bd * (1.0 / RWKV_HEAD), proj_bf)


def _out_kernel(m_ref, x_ref, g1_ref, w_ref, n2_ref, sc_ref, sh_ref, xo_ref, h_ref):
    xn = x_ref[0] + g1_ref[0] * _bdot(m_ref[0], w_ref[...])
    xo_ref[0] = xn
    h_ref[0] = _modulated_rmsnorm(xn, n2_ref[...], sc_ref[0], sh_ref[0])


def _out_proj(merged, x, g1, w_bf, norm2, sc2, sh2):
    bsz, seq, _ = x.shape
    ts = min(seq, 512)
    blk = pl.BlockSpec((1, ts, D_MODEL), lambda b, i: (b, i, 0))
    per_b = pl.BlockSpec((1, 1, D_MODEL), lambda b, i: (b, 0, 0))
    return pl.pallas_call(
        _out_kernel,
        out_shape=[SDS((bsz, seq, D_MODEL), F32)] * 2,
        grid=(bsz, seq // ts),
        in_specs=[blk, blk, per_b, pl.BlockSpec((D_MODEL, D_MODEL), lambda b, i: (0, 0)),
                  pl.BlockSpec((1, D_MODEL), lambda b, i: (0, 0)), per_b, per_b],
        out_specs=[blk, blk],
        compiler_params=_cparams(("arbitrary", "arbitrary")),
        name="out_proj",
    )(merged, x, g1, w_bf, norm2, sc2, sh2)


MOE_TILE = 256
MOE_SLOTS = 2560
MOE_CHUNK = 32
U32 = jnp.uint32


def _route_kernel(h_ref, rw_ref, bias_ref, upper_ref, ltri_ref, slot_ref, w_ref, n8_ref):
    tile = h_ref.shape[0]
    neg = -jnp.inf
    scores = jax.nn.sigmoid(_hdot_nt(rw_ref[...], h_ref[...]))
    s3 = scores.reshape(GROUP_SIZE, N_GROUPS, tile)
    b3 = s3 + bias_ref[...].reshape(GROUP_SIZE, N_GROUPS, tile)
    slab = lax.broadcasted_iota(I32, b3.shape, 0).astype(F32)
    grp = lax.broadcasted_iota(I32, b3.shape, 1).astype(F32)
    eid = grp * GROUP_SIZE + slab
    m1 = jnp.max(b3, axis=0, keepdims=True)
    first = jnp.min(jnp.where(b3 == m1, slab, GROUP_SIZE), axis=0, keepdims=True)
    m2 = jnp.max(jnp.where(slab == first, neg, b3), axis=0, keepdims=True)
    gs = (m1 + m2)[0]
    gi = lax.broadcasted_iota(I32, gs.shape, 0).astype(F32)
    chosen = jnp.zeros(gs.shape, F32)
    for _ in range(TOPK_GROUPS):
        m = jnp.max(gs, axis=0, keepdims=True)
        hit = gi == jnp.min(jnp.where(gs == m, gi, N_GROUPS), axis=0, keepdims=True)
        chosen = jnp.where(hit, 1.0, chosen)
        gs = jnp.where(hit, neg, gs)
    cur = jnp.where((chosen > 0.0)[None], b3, neg)
    ones_cols = jnp.ones((tile, LANES), BF16)
    lanes_of = lambda a: jnp.concatenate([a] * (tile // LANES), axis=1)
    to3 = lambda a: a.reshape(GROUP_SIZE, N_GROUPS, tile)
    fold = lambda a: jnp.sum(jnp.sum(a, axis=0), axis=0, keepdims=True)
    base = jnp.zeros((N_EXPERTS, LANES), F32)
    picks, w_rows, rank_rows = [], [], []
    for _ in range(TOP_K):
        m = jnp.max(jnp.max(cur, axis=0), axis=0, keepdims=True)[None]
        pick = jnp.min(jnp.min(jnp.where(cur == m, eid, N_EXPERTS), axis=0), axis=0, keepdims=True)
        hit = eid == pick[None]
        w_rows.append(fold(jnp.where(hit, s3, 0.0)))
        picks.append(pick)
        cur = jnp.where(hit, neg, cur)
        onehot = hit.astype(BF16).reshape(N_EXPERTS, tile)
        before = jnp.dot(onehot, upper_ref[...], preferred_element_type=F32)
        rank_rows.append(fold(jnp.where(hit, to3(before + lanes_of(base)), 0.0)))
        base = base + jnp.dot(onehot, ones_cols, preferred_element_type=F32)
    n8 = jnp.floor((base + (SUBLANES - 1.0)) * (1.0 / SUBLANES)) * SUBLANES
    run_start = to3(lanes_of(_hdot(ltri_ref[...], n8)))
    slots = [rank_rows[j] + fold(jnp.where(eid == picks[j][None], run_start, 0.0)) for j in range(TOP_K)]
    n8_ref[0] = n8
    w_all = jnp.concatenate(w_rows, axis=0)
    w_ref[...] = w_all / jnp.sum(w_all, axis=0, keepdims=True) * ROUTED_SCALE
    slot_ref[...] = jnp.concatenate(slots, axis=0).astype(I32)


def _route(h2, router_w, router_bias):
    n_tok = h2.shape[0]
    tile = MOE_TILE
    n_tiles = n_tok // tile
    regroup = lambda a: a.reshape(N_GROUPS, GROUP_SIZE, -1).transpose(1, 0, 2).reshape(N_EXPERTS, -1)
    rw = regroup(router_w.T)
    bias = jnp.broadcast_to(regroup(router_bias.reshape(N_EXPERTS, 1)), (N_EXPERTS, tile))
    t_id = jnp.arange(tile, dtype=I32)
    upper = (t_id[:, None] < t_id[None, :]).astype(BF16)
    e_id = jnp.arange(N_EXPERTS, dtype=I32)
    ltri = (e_id[:, None] > e_id[None, :]).astype(F32)
    tok_blk = pl.BlockSpec((TOP_K, tile), lambda i: (0, i))
    full = lambda shape: pl.BlockSpec(shape, lambda i: (0,) * len(shape))
    slot_t, w_t, n8 = pl.pallas_call(
        _route_kernel,
        out_shape=[SDS((TOP_K, n_tok), I32), SDS((TOP_K, n_tok), F32),
                   SDS((n_tiles, N_EXPERTS, LANES), F32)],
        grid=(n_tiles,),
        in_specs=[pl.BlockSpec((tile, D_MODEL), lambda i: (i, 0)),
                  full((N_EXPERTS, D_MODEL)), full((N_EXPERTS, tile)), full((tile, tile)),
                  full((N_EXPERTS, N_EXPERTS))],
        out_specs=[tok_blk, tok_blk, pl.BlockSpec((1, N_EXPERTS, LANES), lambda i: (i, 0, 0))],
        compiler_params=_cparams(("arbitrary",)),
        name="moe_route",
    )(h2, rw, bias, upper, ltri)
    return slot_t, w_t, n8[:, :, 0].astype(I32)


def _for_each_run_piece(tile_idx, n8_ref, fn):
    def per_expert(r, carry):
        n = n8_ref[tile_idx * N_EXPERTS + r]
        whole = n // MOE_CHUNK

        def chunk(k, c):
            fn(r, k * MOE_CHUNK, MOE_CHUNK)
            return c
        lax.fori_loop(0, whole, chunk, 0)
        rest = whole * MOE_CHUNK
        for size in (16, 8):
            @pl.when((n & size) != 0)
            def _(size=size):
                fn(r, rest + (n & (MOE_CHUNK - 1) & ~(2 * size - 1)), size)
        return carry
    lax.fori_loop(0, N_EXPERTS, per_expert, 0)


def _wait_rows(total, make_wait):
    size = SUBLANES
    while size <= MOE_SLOTS:
        @pl.when((total & size) != 0)
        def _(size=size):
            make_wait(size).wait()
        size *= 2


def _pack_pairs(hi_bits, lo_bits):
    return (hi_bits & jnp.uint32(0xFFFF0000)) | (lo_bits >> 16)


def _unpack_pairs(u):
    hi = lax.bitcast_convert_type(u & jnp.uint32(0xFFFF0000), F32)
    lo = lax.bitcast_convert_type(u << 16, F32)
    return jnp.concatenate([hi, lo], axis=1).astype(BF16)


def _dispatch_kernel(n8_ref, off_ref, dst_ref, tot_ref, pad_end_ref, slot_ref, h_ref, xs_ref,
                     g_ref, zero_ref, sem):
    i = pl.program_id(0)
    tile = h_ref.shape[0]
    half = D_MODEL // 2

    @pl.when(i == 0)
    def _():
        zero_ref[...] = jnp.zeros(zero_ref.shape, U32)

        def last_block(e):
            start = pl.multiple_of(jnp.maximum(pad_end_ref[e] - MOE_ROWS, 0), MOE_ROWS)
            return pltpu.make_async_copy(zero_ref, xs_ref.at[pl.ds(start, MOE_ROWS), :], sem)

        def has_rows(e):
            prev = jnp.where(e > 0, pad_end_ref[jnp.maximum(e - 1, 0)], 0)
            return pad_end_ref[e] > prev

        def clear(e, carry):
            @pl.when(has_rows(e))
            def _():
                last_block(e).start()
            return carry
        lax.fori_loop(0, N_EXPERTS, clear, 0)

        def done(e, carry):
            @pl.when(has_rows(e))
            def _():
                last_block(e).wait()
            return carry
        lax.fori_loop(0, N_EXPERTS, done, 0)

        def tail_block(b):
            start = pl.multiple_of(b * MOE_ROWS, MOE_ROWS)
            return pltpu.make_async_copy(zero_ref, xs_ref.at[pl.ds(start, MOE_ROWS), :], sem)

        def clear_tail(b, carry):
            tail_block(b).start()
            return carry

        def done_tail(b, carry):
            tail_block(b).wait()
            return carry
        used = pad_end_ref[N_EXPERTS - 1] // MOE_ROWS
        lax.fori_loop(used, xs_ref.shape[0] // MOE_ROWS, clear_tail, 0)
        lax.fori_loop(used, xs_ref.shape[0] // MOE_ROWS, done_tail, 0)

    slot_id = lax.broadcasted_iota(I32, (MOE_SLOTS, tile), 0)
    sel = jnp.zeros((MOE_SLOTS, tile), F32)
    for j in range(TOP_K):
        sel = jnp.where(slot_id == slot_ref[pl.ds(j, 1), :], 1.0, sel)
    g = jnp.dot(sel.astype(BF16), h_ref[...].astype(BF16), preferred_element_type=F32)
    bits = lax.bitcast_convert_type(g, U32)
    g_ref[...] = _pack_pairs(bits[:, :half], bits[:, half:])

    def piece(r, offset, rows):
        src = pl.multiple_of(off_ref[i * N_EXPERTS + r] + offset, SUBLANES)
        dst = pl.multiple_of(dst_ref[i * N_EXPERTS + r] + offset, SUBLANES)
        pltpu.make_async_copy(g_ref.at[pl.ds(src, rows), :], xs_ref.at[pl.ds(dst, rows), :], sem).start()

    _for_each_run_piece(i, n8_ref, piece)
    _wait_rows(tot_ref[i], lambda rows: pltpu.make_async_copy(
        g_ref.at[pl.ds(0, rows), :], xs_ref.at[pl.ds(0, rows), :], sem))


def _dispatch(h2, slot_t, n8_flat, off_flat, dst_flat, tot, pad_end, n_rows):
    n_tok = h2.shape[0]
    tile = MOE_TILE
    grid_spec = pltpu.PrefetchScalarGridSpec(
        num_scalar_prefetch=5,
        grid=(n_tok // tile,),
        in_specs=[pl.BlockSpec((TOP_K, tile), lambda i, *_: (0, i)),
                  pl.BlockSpec((tile, D_MODEL), lambda i, *_: (i, 0))],
        out_specs=pl.BlockSpec(memory_space=pl.ANY),
        scratch_shapes=[pltpu.VMEM((MOE_SLOTS, D_MODEL // 2), U32),
                        pltpu.VMEM((MOE_ROWS, D_MODEL // 2), U32), pltpu.SemaphoreType.DMA],
    )
    return pl.pallas_call(
        _dispatch_kernel,
        out_shape=SDS((n_rows, D_MODEL // 2), U32),
        grid_spec=grid_spec,
        compiler_params=_cparams(("arbitrary",)),
        name="moe_dispatch",
    )(n8_flat, off_flat, dst_flat, tot, pad_end, slot_t, h2)


def _expert_kernel(first_ref, nblk_ref, exp_of_ref, xs_ref, w1_ref, w3_ref, w2_ref, y_ref,
                   xbuf, ybuf, wb1, wb3, wb2, in_sem, out_sem):
    del exp_of_ref
    r = pl.program_id(0)
    half = D_MODEL // 2
    used = first_ref[N_EXPERTS - 1] + nblk_ref[N_EXPERTS - 1]
    part = MOE_ROWS // EXPERT_DMA_PARTS

    class _Copies:
        def __init__(self, make):
            self.parts = [make(p) for p in range(EXPERT_DMA_PARTS)]

        def start(self):
            for c in self.parts:
                c.start()

        def wait(self):
            for c in self.parts:
                c.wait()

    def hbm_rows(g, p):
        return pl.ds(pl.multiple_of(g * MOE_ROWS + p * part, part), part)

    in_copy = lambda g, slot: _Copies(lambda p: pltpu.make_async_copy(
        xs_ref.at[hbm_rows(g, p), :], xbuf.at[slot, pl.ds(p * part, part), :], in_sem.at[slot]))
    out_copy = lambda g, slot: _Copies(lambda p: pltpu.make_async_copy(
        ybuf.at[slot, pl.ds(p * part, part), :], y_ref.at[hbm_rows(g, p), :], out_sem.at[slot]))

    @pl.when((r == 0) & (used > 0))
    def _():
        in_copy(0, 0).start()

    wb1[...] = w1_ref[0, 0].astype(BF16)
    wb3[...] = w3_ref[0, 0].astype(BF16)
    wb2[...] = w2_ref[0, 0].astype(BF16)

    def block(k, carry):
        g = first_ref[r] + k
        slot = g % 2
        in_copy(g, slot).wait()

        @pl.when(g + 1 < used)
        def _():
            in_copy(g + 1, 1 - slot).start()

        @pl.when(g >= 2)
        def _():
            out_copy(g - 2, slot).wait()

        halves = (pl.ds(0, MOE_ROWS // 2), pl.ds(MOE_ROWS // 2, MOE_ROWS // 2))
        dot = lambda a, b: jnp.dot(a, b, preferred_element_type=F32)
        xs_half = [_unpack_pairs(xbuf[slot, rows, :]) for rows in halves]
        gate = [dot(x, wb1[...]) for x in xs_half]
        up = [dot(x, wb3[...]) for x in xs_half]
        hid = [(jax.nn.silu(a) * b).astype(BF16) for a, b in zip(gate, up)]
        for rows, h in zip(halves, hid):
            y = dot(h, wb2[...]).astype(BF16).astype(F32)
            bits = lax.bitcast_convert_type(y, U32)
            ybuf[slot, rows, :] = _pack_pairs(bits[:, :half], bits[:, half:])
        out_copy(g, slot).start()
        return carry
    lax.fori_loop(0, nblk_ref[r], block, 0)

    @pl.when(r == N_EXPERTS - 1)
    def _():
        @pl.when(used >= 2)
        def _():
            out_copy(used - 2, used % 2).wait()

        @pl.when(used >= 1)
        def _():
            out_copy(used - 1, (used - 1) % 2).wait()

        ybuf[0] = jnp.zeros(ybuf.shape[1:], U32)
        n_blocks = y_ref.shape[0] // MOE_ROWS

        def clear(g, carry):
            out_copy(g, 0).start()
            return carry

        def done(g, carry):
            out_copy(g, 0).wait()
            return carry
        lax.fori_loop(used, n_blocks, clear, 0)
        lax.fori_loop(used, n_blocks, done, 0)


def _experts(xs, first_block, n_block, exp_of_row, w1, w3, w2, layer):
    n_rows = xs.shape[0]
    half = D_MODEL // 2
    w_in_blk = pl.BlockSpec((1, 1, D_MODEL, EXPERT_FF), lambda r, first, nblk, eo: (layer, eo[r], 0, 0))
    grid_spec = pltpu.PrefetchScalarGridSpec(
        num_scalar_prefetch=3,
        grid=(N_EXPERTS,),
        in_specs=[pl.BlockSpec(memory_space=pl.ANY), w_in_blk, w_in_blk,
                  pl.BlockSpec((1, 1, EXPERT_FF, D_MODEL), lambda r, first, nblk, eo: (layer, eo[r], 0, 0))],
        out_specs=pl.BlockSpec(memory_space=pl.ANY),
        scratch_shapes=[pltpu.VMEM((2, MOE_ROWS, half), U32), pltpu.VMEM((2, MOE_ROWS, half), U32),
                        pltpu.VMEM((D_MODEL, EXPERT_FF), BF16), pltpu.VMEM((D_MODEL, EXPERT_FF), BF16),
                        pltpu.VMEM((EXPERT_FF, D_MODEL), BF16),
                        pltpu.SemaphoreType.DMA((2,)), pltpu.SemaphoreType.DMA((2,))],
    )
    return pl.pallas_call(
        _expert_kernel,
        out_shape=SDS((n_rows, half), U32),
        grid_spec=grid_spec,
        compiler_params=_cparams(("arbitrary",)),
        name="moe_experts",
    )(first_block, n_block, exp_of_row, xs, w1, w3, w2)


def _combine_kernel(n8_ref, off_ref, src_ref, tot_ref, y_ref, slot_ref, w_ref, h_ref, x_ref, g2_ref,
                    s1_ref, s3_ref, s2_ref, fn_ref, o_ref, yt_ref, sem, *, final):
    tile = h_ref.shape[1]
    i = pl.program_id(0) * pl.num_programs(1) + pl.program_id(1)

    @pl.when(i == 0)
    def _():
        yt_ref[...] = jnp.zeros(yt_ref.shape, U32)

    def piece(r, offset, rows):
        src = pl.multiple_of(src_ref[i * N_EXPERTS + r] + offset, SUBLANES)
        dst = pl.multiple_of(off_ref[i * N_EXPERTS + r] + offset, SUBLANES)
        pltpu.make_async_copy(y_ref.at[pl.ds(src, rows), :], yt_ref.at[pl.ds(dst, rows), :], sem).start()

    _for_each_run_piece(i, n8_ref, piece)
    h = h_ref[0]
    shared = _bdot(jax.nn.silu(_bdot(h, s1_ref[...])) * _bdot(h, s3_ref[...]), s2_ref[...])
    slot_id = lax.broadcasted_iota(I32, (tile, MOE_SLOTS), 1)
    slots = slot_ref[0]
    w = w_ref[0]
    pw = jnp.zeros((tile, MOE_SLOTS), F32)
    for j in range(TOP_K):
        pw = jnp.where(slot_id == slots[:, j:j + 1], w[:, j:j + 1], pw)
    _wait_rows(tot_ref[i], lambda rows: pltpu.make_async_copy(
        y_ref.at[pl.ds(0, rows), :], yt_ref.at[pl.ds(0, rows), :], sem))
    routed = jnp.dot(pw.astype(BF16), _unpack_pairs(yt_ref[...]), preferred_element_type=F32)
    xn = x_ref[0] + g2_ref[0] * (routed + shared)
    if final:
        xn = xn * lax.rsqrt(jnp.mean(xn * xn, axis=-1, keepdims=True) + NORM_EPS) * fn_ref[...]
    o_ref[0] = xn


def _combine(y, slot_nat, w_nat, n8_flat, off_flat, dst_flat, tot, h2, x, g2, s1_bf, s3_bf, s2_bf,
             final_norm, final):
    bsz, seq, _ = x.shape
    tile = MOE_TILE
    per_seq = seq // tile
    blk = pl.BlockSpec((1, tile, D_MODEL), lambda b, i, *_: (b, i, 0))
    tok = pl.BlockSpec((1, tile, TOP_K), lambda b, i, *_: (b, i, 0))
    full = lambda shape: pl.BlockSpec(shape, lambda b, i, *_: (0,) * len(shape))
    grid_spec = pltpu.PrefetchScalarGridSpec(
        num_scalar_prefetch=4,
        grid=(bsz, per_seq),
        in_specs=[pl.BlockSpec(memory_space=pl.ANY), tok, tok, blk, blk,
                  pl.BlockSpec((1, 1, D_MODEL), lambda b, i, *_: (b, 0, 0)),
                  full((D_MODEL, SHARED_FF)), full((D_MODEL, SHARED_FF)), full((SHARED_FF, D_MODEL)),
                  full((1, D_MODEL))],
        out_specs=blk,
        scratch_shapes=[pltpu.VMEM((MOE_SLOTS, D_MODEL // 2), U32), pltpu.SemaphoreType.DMA],
    )
    return pl.pallas_call(
        functools.partial(_combine_kernel, final=final),
        out_shape=SDS((bsz, seq, D_MODEL), F32),
        grid_spec=grid_spec,
        compiler_params=_cparams(("arbitrary", "arbitrary")),
        name="moe_combine",
    )(n8_flat, off_flat, dst_flat, tot, y, slot_nat, w_nat, h2, x, g2, s1_bf, s3_bf, s2_bf, final_norm)


def _moe(x, h2, g2, router_w, router_bias, w1, w3, w2, layer, s1_bf, s3_bf, s2_bf, final_norm, final):
    bsz, seq, _ = x.shape
    assert seq % MOE_TILE == 0, "token tiles must not straddle sequences"
    n_tok = bsz * seq
    slot_t, w_t, n8 = _route(h2.reshape(n_tok, D_MODEL), router_w, router_bias)
    n_tiles = n8.shape[0]
    counts = jnp.sum(n8, axis=0)
    padded = (counts + MOE_ROWS - 1) // MOE_ROWS * MOE_ROWS
    pad_end = jnp.cumsum(padded).astype(I32)
    pad_start = pad_end - padded
    run_row = (pad_start[None, :] + jnp.cumsum(n8, axis=0) - n8).astype(I32)
    run_slot = (jnp.cumsum(n8, axis=1) - n8).astype(I32)
    tot = jnp.sum(n8, axis=1).astype(I32)
    max_rows = n_tok * TOP_K + n_tiles * N_EXPERTS * (SUBLANES - 1) + N_EXPERTS * (MOE_ROWS - 1)
    n_blocks = (max_rows + MOE_ROWS - 1) // MOE_ROWS
    row_id = jnp.arange(N_EXPERTS, dtype=I32)
    exp_of_row = (row_id % N_GROUPS) * GROUP_SIZE + row_id // N_GROUPS
    flat = lambda a: a.reshape(-1).astype(I32)
    xs = _dispatch(h2.reshape(n_tok, D_MODEL), slot_t, flat(n8), flat(run_slot), flat(run_row), tot,
                   pad_end, n_blocks * MOE_ROWS)
    y = _experts(xs, (pad_start // MOE_ROWS).astype(I32), (padded // MOE_ROWS).astype(I32), exp_of_row,
                 w1, w3, w2, layer)
    nat = lambda a: a.T.reshape(bsz, seq, TOP_K)
    return _combine(y, nat(slot_t), nat(w_t), flat(n8), flat(run_slot), flat(run_row), tot, h2, x, g2,
                    s1_bf, s3_bf, s2_bf, final_norm, final)


def _pad_cols(a, width):
    return jnp.pad(a, ((0, 0), (0, width - a.shape[1])))


def _layout_w_in(w_in):
    b0 = COLS_A
    c0 = COLS_A + COLS_B
    g0 = c0 + COLS_C
    seg_b = _pad_cols(w_in[:, b0:c0], SEG_G - SEG_B)
    return jnp.concatenate([w_in[:, c0:g0], seg_b, w_in[:, g0:], w_in[:, :b0]], axis=1).astype(BF16)


def _layout_mu(mu):
    return _pad_cols(mu.reshape(1, -1), SEG_G - SEG_B)


def _pad_rows(a, height):
    return jnp.pad(a, ((0, height - a.shape[0]), (0, 0)))


def kernel(x, c, ada_w, ada_b, norm1, norm2, w_in, conv_a_w, conv_a_b, ln_a_g, ln_a_b, proj_a, mu_b, w0, w_up, a0, a_up, g_up, k_k, k_a, r_k, gn_b_g, gn_b_b, proj_b, conv_c_w, conv_c_b, lru_wa, lru_ba, lru_wx, lru_bx, lru_lambda, proj_c, w_out, router_w, router_bias, exp_w1, exp_w3, exp_w2, sh_w1, sh_w3, sh_w2, final_norm):
    depth = ada_w.shape[0]
    bsz = x.shape[0]
    mod = _ada_mod(c, ada_w, ada_b)
    for l in range(depth):
        sh1, sc1, g1, sh2, sc2, g2 = [mod[l, :, i * D_MODEL:(i + 1) * D_MODEL].reshape(bsz, 1, D_MODEL)
                                      for i in range(N_MOD)]
        p = _in_proj(x, sc1, sh1, norm1[l].reshape(1, -1), _layout_w_in(w_in[l]))
        merged = _conv_a(p, conv_a_w[l], conv_a_b[l], ln_a_g[l], ln_a_b[l], proj_a[l].astype(BF16))
        merged = _rwkv(p, merged, _layout_mu(mu_b[l]), w0[l], _pad_rows(w_up[l], LANES).astype(BF16),
                       a0[l], jnp.pad(a_up[l], ((LORA_W, 0), (0, 0))).astype(BF16), g_up[l].astype(BF16),
                       k_k[l], k_a[l], r_k[l], gn_b_g[l], gn_b_b[l], proj_b[l].astype(BF16))
        merged = _lru(p, merged, conv_c_w[l], conv_c_b[l], lru_wa[l].astype(BF16), lru_ba[l],
                      lru_wx[l].astype(BF16), lru_bx[l], lru_lambda[l], proj_c[l].astype(BF16))
        x, h2 = _out_proj(merged, x, g1, w_out[l].astype(BF16), norm2[l].reshape(1, -1), sc2, sh2)
        x = _moe(x, h2, g2, router_w[l], router_bias[l], exp_w1, exp_w3, exp_w2, l,
                 sh_w1[l].astype(BF16), sh_w3[l].astype(BF16), sh_w2[l].astype(BF16),
                 final_norm.reshape(1, -1), final=(l == depth - 1))
    return x
```

```python
import functools

import jax
import jax.numpy as jnp
from jax import lax
from jax.experimental import pallas as pl
from jax.experimental.pallas import tpu as pltpu

F32 = jnp.float32
BF16 = jnp.bfloat16
I32 = jnp.int32
SDS = jax.ShapeDtypeStruct
HIGHEST = lax.Precision.HIGHEST

D_MODEL = 1024
N_MOD = 6
NORM_EPS = 1e-6
CONV_A_CH = 512
CONV_A_WIDTH = 31
CONV_A_LN_EPS = 1e-5
RWKV_HEADS = 8
RWKV_HEAD = 64
RWKV_DIM = RWKV_HEADS * RWKV_HEAD
LORA_W = 64
LORA_A = 64
LORA_G = 128
RWKV_GN_EPS = 64e-5
RWKV_CHUNK = 64
LRU_DIM = 1024
LRU_HEADS = 8
LRU_BLOCK = LRU_DIM // LRU_HEADS
LRU_CONV = 4
LRU_C = 8.0
N_EXPERTS = 64
TOP_K = 8
N_GROUPS = 8
GROUP_SIZE = N_EXPERTS // N_GROUPS
TOPK_GROUPS = 4
EXPERT_FF = 256
SHARED_FF = 256
ROUTED_SCALE = 2.5
SEG_C = 0
SEG_B = 2048
SEG_G = 4096
SEG_A = 7168
IN_COLS_PAD = 8192
RW_R, RW_K, RW_V, RW_XWA, RW_XG = 0, 512, 1024, 1536, 1664
COLS_A = 2 * CONV_A_CH
COLS_B = 3 * RWKV_DIM + LORA_W + LORA_A + LORA_G
COLS_C = 2 * LRU_DIM
VMEM_LIMIT = 56 * 1024 * 1024
SUBLANES = 8
LANES = 128
MOE_ROWS = 512
EXPERT_DMA_PARTS = 4


def _cparams(sem):
    return pltpu.CompilerParams(dimension_semantics=sem, vmem_limit_bytes=VMEM_LIMIT)


def _bdot(a, b):
    return jnp.dot(a.astype(BF16), b.astype(BF16), preferred_element_type=F32)


def _hdot(a, b):
    return jnp.dot(a, b, preferred_element_type=F32, precision=HIGHEST)


def _split(a):
    hi = a.astype(BF16)
    return hi, (a - hi.astype(F32)).astype(BF16)


def _head_sums(a, ones_bf):
    hi, lo = _split(a)
    return (jnp.dot(hi, ones_bf, preferred_element_type=F32)
            + jnp.dot(lo, ones_bf, preferred_element_type=F32))


def _hdot_nt(a, b):
    return lax.dot_general(a, b, (((1,), (1,)), ((), ())), preferred_element_type=F32,
                           precision=HIGHEST)


def _hdot_tn(a, b):
    return lax.dot_general(a, b, (((0,), (0,)), ((), ())), preferred_element_type=F32,
                           precision=HIGHEST)


def _ada_kernel(c_ref, w_ref, b_ref, o_ref):
    cond = jax.nn.silu(c_ref[...])
    o_ref[0] = _bdot(cond, w_ref[0]) + b_ref[0]


def _ada_mod(c, ada_w, ada_b):
    depth, _, n = ada_w.shape
    bsz = c.shape[0]
    tn = 1536
    return pl.pallas_call(
        _ada_kernel,
        out_shape=SDS((depth, bsz, n), F32),
        grid=(depth, n // tn),
        in_specs=[pl.BlockSpec((bsz, D_MODEL), lambda l, j: (0, 0)),
                  pl.BlockSpec((1, D_MODEL, tn), lambda l, j: (l, 0, j)),
                  pl.BlockSpec((1, 1, tn), lambda l, j: (l, 0, j))],
        out_specs=pl.BlockSpec((1, bsz, tn), lambda l, j: (l, 0, j)),
        compiler_params=_cparams(("arbitrary", "arbitrary")),
        name="ada_mod",
    )(c, ada_w, ada_b.reshape(depth, 1, n))


def _modulated_rmsnorm(x, g, sc, sh):
    y = x * lax.rsqrt(jnp.mean(x * x, axis=-1, keepdims=True) + NORM_EPS)
    return (y * g) * (1.0 + sc) + sh


IN_PROJ_SUB = 512


def _in_kernel(x_ref, sc_ref, sh_ref, g_ref, w_ref, o_ref, h_ref):
    @pl.when(pl.program_id(2) == 0)
    def _():
        h_ref[...] = _modulated_rmsnorm(x_ref[0], g_ref[...], sc_ref[0], sh_ref[0]).astype(BF16)

    j = pl.program_id(2)
    ts, tn = o_ref.shape[1], o_ref.shape[2]
    is_gelu = j == SEG_C // tn
    is_gate = (j >= SEG_G // tn) & (j < SEG_A // tn)
    is_glu = j == SEG_A // tn

    def glu(acc):
        u = acc[:, :tn // 2] * jax.nn.sigmoid(acc[:, tn // 2:])
        return jnp.concatenate([u, jnp.zeros_like(u)], axis=1)

    def emit(act, sub):
        for r0 in range(0, ts, sub):
            rows = pl.ds(r0, sub)
            acc = jnp.dot(h_ref[rows, :], w_ref[...], preferred_element_type=F32)
            o_ref[0, rows, :] = act(acc).astype(o_ref.dtype)

    sub = min(ts, IN_PROJ_SUB)
    pl.when(is_gelu)(lambda: emit(functools.partial(jax.nn.gelu, approximate=True), sub))
    pl.when(is_gate)(lambda: emit(jax.nn.sigmoid, sub))
    pl.when(is_glu)(lambda: emit(glu, sub))
    pl.when(jnp.logical_not(is_gelu | is_gate | is_glu))(lambda: emit(lambda acc: acc, ts))


def _in_proj(x, sc, sh, g, w_pad):
    bsz, seq, _ = x.shape
    ts = min(seq, 1024)
    tn = 1024
    return pl.pallas_call(
        _in_kernel,
        out_shape=SDS((bsz, seq, IN_COLS_PAD), BF16),
        grid=(bsz, seq // ts, IN_COLS_PAD // tn),
        in_specs=[pl.BlockSpec((1, ts, D_MODEL), lambda b, i, j: (b, i, 0)),
                  pl.BlockSpec((1, 1, D_MODEL), lambda b, i, j: (b, 0, 0)),
                  pl.BlockSpec((1, 1, D_MODEL), lambda b, i, j: (b, 0, 0)),
                  pl.BlockSpec((1, D_MODEL), lambda b, i, j: (0, 0)),
                  pl.BlockSpec((D_MODEL, tn), lambda b, i, j: (0, j))],
        out_specs=pl.BlockSpec((1, ts, tn), lambda b, i, j: (b, i, j)),
        scratch_shapes=[pltpu.VMEM((ts, D_MODEL), BF16)],
        compiler_params=_cparams(("arbitrary", "arbitrary", "arbitrary")),
        name="in_proj",
    )(x, sc, sh, g, w_pad)


CONV_A_HALO = 32
CONV_A_SUB = 64


def _conv_a_kernel(pa_ref, pg_ref, cw_ref, cb_ref, lg_ref, lb_ref, pj_ref, o_ref, ext_ref, sh_ref, y_ref):
    ts = pa_ref.shape[1]

    @pl.when(pl.program_id(1) == 0)
    def _():
        ext_ref[pl.ds(0, CONV_A_HALO), :] = jnp.zeros((CONV_A_HALO, CONV_A_CH), F32)

    ext_ref[pl.ds(CONV_A_HALO, ts), :] = pa_ref[0].astype(F32)
    for p in range(1, SUBLANES):
        sh_ref[p - 1] = ext_ref[pl.ds(p, sh_ref.shape[1]), :]
    first = CONV_A_HALO - (CONV_A_WIDTH - 1)
    for r0 in range(0, ts, CONV_A_SUB):
        acc = jnp.zeros((CONV_A_SUB, CONV_A_CH), F32) + cb_ref[...]
        for j in range(CONV_A_WIDTH):
            phase = (first + j) % SUBLANES
            rows = pl.ds(r0 + first + j - phase, CONV_A_SUB)
            tap = ext_ref[rows, :] if phase == 0 else sh_ref[phase - 1, rows, :]
            acc = acc + tap * cw_ref[pl.ds(j, 1), :]
        y_ref[pl.ds(r0, CONV_A_SUB), :] = acc
    ext_ref[pl.ds(0, CONV_A_HALO), :] = ext_ref[pl.ds(ts, CONV_A_HALO), :]
    y = y_ref[...]
    mu = jnp.mean(y, axis=-1, keepdims=True)
    d = y - mu
    var = jnp.mean(d * d, axis=-1, keepdims=True)
    yn = d * lax.rsqrt(var + CONV_A_LN_EPS) * lg_ref[...] + lb_ref[...]
    o = _bdot(jax.nn.silu(yn), pj_ref[...])
    o_ref[0] = pg_ref[0].astype(F32) * o


def _conv_a(p, conv_w, conv_b, ln_g, ln_b, proj_bf):
    bsz, seq, _ = p.shape
    ts = min(seq, 512)
    row = lambda a: a.reshape(1, -1)
    full = lambda shape: pl.BlockSpec(shape, lambda b, i: (0,) * len(shape))
    return pl.pallas_call(
        _conv_a_kernel,
        out_shape=SDS((bsz, seq, D_MODEL), F32),
        grid=(bsz, seq // ts),
        in_specs=[pl.BlockSpec((1, ts, CONV_A_CH), lambda b, i: (b, i, SEG_A // CONV_A_CH)),
                  pl.BlockSpec((1, ts, D_MODEL), lambda b, i: (b, i, SEG_G // D_MODEL)),
                  full((CONV_A_WIDTH, CONV_A_CH)), full((1, CONV_A_CH)), full((1, CONV_A_CH)),
                  full((1, CONV_A_CH)), full((CONV_A_CH, D_MODEL))],
        out_specs=pl.BlockSpec((1, ts, D_MODEL), lambda b, i: (b, i, 0)),
        scratch_shapes=[pltpu.VMEM((ts + CONV_A_HALO, CONV_A_CH), F32),
                        pltpu.VMEM((SUBLANES - 1, ts + CONV_A_HALO - SUBLANES, CONV_A_CH), F32),
                        pltpu.VMEM((ts, CONV_A_CH), F32)],
        compiler_params=_cparams(("arbitrary", "arbitrary")),
        name="conv_a",
    )(p, p, conv_w, row(conv_b), row(ln_g), row(ln_b), proj_bf)


def _lru_kernel(pc_ref, pg_ref, m_ref, cw_ref, cb_ref, wa_ref, ba_ref, wx_ref, bx_ref, lam_ref,
                pj_ref, o_ref, ext_ref, h_ref, a_ref, b_ref):
    ts = pc_ref.shape[1]
    groups = ts // SUBLANES

    @pl.when(pl.program_id(1) == 0)
    def _():
        ext_ref[pl.ds(0, SUBLANES), :] = jnp.zeros((SUBLANES, LRU_DIM), F32)
        h_ref[...] = jnp.zeros((SUBLANES, LRU_DIM), F32)

    pc = pc_ref[0].astype(F32)
    y_gate = pc[:, :LRU_DIM]
    ext_ref[pl.ds(SUBLANES, ts), :] = pc[:, LRU_DIM:]
    first = SUBLANES - (LRU_CONV - 1)
    xc = jnp.zeros((ts, LRU_DIM), F32) + cb_ref[...]
    for j in range(LRU_CONV):
        xc = xc + ext_ref[pl.ds(first + j, ts), :] * cw_ref[pl.ds(j, 1), :]
    ext_ref[pl.ds(0, SUBLANES), :] = ext_ref[pl.ds(ts, SUBLANES), :]

    def block_diag(w_ref):
        return jnp.concatenate(
            [_bdot(xc[:, h * LRU_BLOCK:(h + 1) * LRU_BLOCK], w_ref[h]) for h in range(LRU_HEADS)],
            axis=1)

    gate_a = jax.nn.sigmoid(block_diag(wa_ref) + ba_ref[...])
    gate_x = jax.nn.sigmoid(block_diag(wx_ref) + bx_ref[...])
    log_a = -LRU_C * gate_a * jax.nn.softplus(-lam_ref[...])
    a = jnp.exp(log_a)
    b = xc * gate_x * jnp.sqrt(1.0 - jnp.exp(2.0 * log_a))

    a3 = a.reshape(groups, SUBLANES, LRU_DIM)
    b3 = b.reshape(groups, SUBLANES, LRU_DIM)
    row = lax.broadcasted_iota(I32, (groups, SUBLANES, LRU_DIM), 1)
    for s in (1, 2, 4):
        keep = row >= s
        b3 = jnp.where(keep, a3 * pltpu.roll(b3, s, axis=1) + b3, b3)
        a3 = jnp.where(keep, a3 * pltpu.roll(a3, s, axis=1), a3)
    a_ref[...] = a3.reshape(ts, LRU_DIM)
    b_ref[...] = b3.reshape(ts, LRU_DIM)
    h = h_ref[...]
    for g in range(groups):
        rows = pl.ds(g * SUBLANES, SUBLANES)
        hg = a_ref[rows, :] * h + b_ref[rows, :]
        b_ref[rows, :] = hg
        h = jnp.broadcast_to(hg[SUBLANES - 1:SUBLANES, :], (SUBLANES, LRU_DIM))
    h_ref[...] = h
    o = _bdot(b_ref[...] * y_gate, pj_ref[...])
    o_ref[0] = m_ref[0] + pg_ref[0].astype(F32) * o


def _lru(p, merged, conv_w, conv_b, wa_bf, ba, wx_bf, bx, lam, proj_bf):
    bsz, seq, _ = p.shape
    ts = min(seq, 256)
    row = lambda a: a.reshape(1, -1)
    full = lambda shape: pl.BlockSpec(shape, lambda b, i: (0,) * len(shape))
    return pl.pallas_call(
        _lru_kernel,
        out_shape=SDS((bsz, seq, D_MODEL), F32),
        grid=(bsz, seq // ts),
        in_specs=[pl.BlockSpec((1, ts, 2 * LRU_DIM), lambda b, i: (b, i, SEG_C // (2 * LRU_DIM))),
                  pl.BlockSpec((1, ts, D_MODEL), lambda b, i: (b, i, SEG_G // D_MODEL + 2)),
                  pl.BlockSpec((1, ts, D_MODEL), lambda b, i: (b, i, 0)),
                  full((LRU_CONV, LRU_DIM)), full((1, LRU_DIM)),
                  full((LRU_HEADS, LRU_BLOCK, LRU_BLOCK)), full((1, LRU_DIM)),
                  full((LRU_HEADS, LRU_BLOCK, LRU_BLOCK)), full((1, LRU_DIM)),
                  full((1, LRU_DIM)), full((LRU_DIM, D_MODEL))],
        out_specs=pl.BlockSpec((1, ts, D_MODEL), lambda b, i: (b, i, 0)),
        scratch_shapes=[pltpu.VMEM((ts + SUBLANES, LRU_DIM), F32),
                        pltpu.VMEM((SUBLANES, LRU_DIM), F32),
                        pltpu.VMEM((ts, LRU_DIM), F32),
                        pltpu.VMEM((ts, LRU_DIM), F32)],
        compiler_params=_cparams(("arbitrary", "arbitrary")),
        name="rg_lru",
    )(p, p, merged, conv_w, row(conv_b), wa_bf, row(ba), wx_bf, row(bx), row(lam), proj_bf)


def _rwkv_prep_kernel(pb_ref, mu_ref, w0_ref, wup_ref, a0_ref, aup_ref, gup_ref, kk_ref, ka_ref,
                      rk_ref, bd_ref, ltri_ref,
                      rt_ref, kkt_ref, kh_ref, bh_ref, v_ref, pinc_ref, bonus_ref, g_ref, ext_ref):
    ts = pb_ref.shape[1]

    @pl.when(pl.program_id(1) == 0)
    def _():
        ext_ref[pl.ds(0, SUBLANES), :] = jnp.zeros((SUBLANES, ext_ref.shape[1]), F32)

    p = pb_ref[0].astype(F32)
    ext_ref[pl.ds(SUBLANES, ts), :] = p
    prev = ext_ref[pl.ds(SUBLANES - 1, ts), :]
    ext_ref[pl.ds(0, SUBLANES), :] = ext_ref[pl.ds(ts, SUBLANES), :]
    pm = p + (prev - p) * mu_ref[...]
    r = pm[:, RW_R:RW_R + RWKV_DIM]
    k = pm[:, RW_K:RW_K + RWKV_DIM]
    v = pm[:, RW_V:RW_V + RWKV_DIM]
    xwa = pm[:, RW_XWA:RW_XWA + LANES]
    xg = pm[:, RW_XG:RW_XG + LORA_G]
    w = -jax.nn.softplus(-(w0_ref[...] + _bdot(jnp.tanh(xwa), wup_ref[...]))) - 0.5
    lw = -jnp.exp(w)
    a = jax.nn.sigmoid(a0_ref[...] + _bdot(xwa, aup_ref[...]))
    g_ref[0] = _bdot(jax.nn.sigmoid(xg), gup_ref[...])
    kkr = k * kk_ref[...]
    ss = _head_sums(kkr * kkr, bd_ref[...])
    kk = kkr / jnp.maximum(jnp.sqrt(ss), 1e-12)
    k2 = k * (1.0 + (a - 1.0) * ka_ref[...])
    lw_hi = lw.astype(BF16)
    lw_mid, lw_lo = _split(lw - lw_hi.astype(F32))
    tri = ltri_ref[...]
    lcum = (jnp.dot(tri, lw_hi, preferred_element_type=F32) + jnp.dot(tri, lw_mid, preferred_element_type=F32)
            + jnp.dot(tri, lw_lo, preferred_element_type=F32))
    pinc = jnp.exp(lcum)
    pinv = jnp.exp(-lcum)
    rt_ref[0] = r * pinc
    kkt_ref[0] = kk * jnp.exp(lcum - lw)
    kh_ref[0] = k2 * pinv
    bh_ref[0] = kk * a * pinv
    v_ref[0] = v
    pinc_ref[0] = pinc
    bonus_ref[0] = _head_sums(r * k2 * rk_ref[...], bd_ref[...]) * v


def _dot3(a, b):
    d = lambda x, y: jnp.dot(x, y, preferred_element_type=F32)
    m = a[0].shape[0]
    both = d(jnp.concatenate([a[0], a[1]], axis=0), b[0])
    return both[:m] + both[m:] + d(a[0], b[1])


def _rwkv_scan_kernel(rt_ref, kkt_ref, kh_ref, bh_ref, v_ref, pinc_ref, y_ref, s_ref):
    c = RWKV_CHUNK
    n = RWKV_HEAD
    nb = rt_ref.shape[0]
    heads = range(nb * RWKV_HEADS)

    @pl.when(pl.program_id(1) == 0)
    def _():
        s_ref[...] = jnp.zeros(s_ref.shape, F32)

    row = lax.broadcasted_iota(I32, (c, c), 0)
    col = lax.broadcasted_iota(I32, (c, c), 1)
    eye = (row == col).astype(F32)
    same16 = (row // 16) == (col // 16)
    same32 = (row // 32) == (col // 32)
    row2 = lax.broadcasted_iota(I32, (c, 2 * c), 0)
    col2 = lax.broadcasted_iota(I32, (c, 2 * c), 1) % c
    nt = lambda a, b: lax.dot_general(a, b, (((1,), (1,)), ((), ())), preferred_element_type=F32)
    tn = lambda a, b: lax.dot_general(a, b, (((0,), (0,)), ((), ())), preferred_element_type=F32)
    dot = lambda a, b: jnp.dot(a, b, preferred_element_type=F32)
    sl = [pl.ds((h % RWKV_HEADS) * n, n) for h in heads]
    sq = [h // RWKV_HEADS for h in heads]
    v = [v_ref[sq[h], :, sl[h]] for h in heads]
    pc = [pinc_ref[sq[h], pl.ds(c - 1, 1), sl[h]] for h in heads]
    s = [s_ref[sq[h], :, sl[h]] for h in heads]
    lhs = [jnp.concatenate([kkt_ref[sq[h], :, sl[h]], rt_ref[sq[h], :, sl[h]]], axis=0) for h in heads]
    rhs = [jnp.concatenate([bh_ref[sq[h], :, sl[h]], kh_ref[sq[h], :, sl[h]]], axis=0) for h in heads]
    big = [nt(lhs[h], rhs[h]) for h in heads]
    from_state = [nt(lhs[h], s[h]) for h in heads]
    top = [jnp.where(row2 > col2, big[h][:c], 0.0) for h in heads]
    bot = [jnp.where(row2 >= col2, big[h][c:], 0.0) for h in heads]
    a_b = [top[h][:, :c] for h in heads]
    akv = [dot(top[h], jnp.concatenate([jnp.zeros((c, n), F32), v[h]], axis=0)) for h in heads]
    d16 = [jnp.where(same16, a_b[h], 0.0) for h in heads]
    sd = [_split(d16[h]) for h in heads]
    s2 = [_split(_dot3(sd[h], sd[h])) for h in heads]
    s4 = [_split(_dot3(s2[h], s2[h])) for h in heads]
    s8 = [_split(_dot3(s4[h], s4[h])) for h in heads]
    t = [eye - d16[h] for h in heads]
    for sp in (s2, s4, s8):
        t = [t[h] + _dot3(_split(t[h]), sp[h]) for h in heads]
    for off in ([jnp.where(same32 & jnp.logical_not(same16), a_b[h], 0.0) for h in heads],
                [jnp.where(same32, 0.0, a_b[h]) for h in heads]):
        tb = [t[h].astype(BF16) for h in heads]
        lt = [dot(off[h].astype(BF16), tb[h]).astype(BF16) for h in heads]
        t = [t[h] - dot(tb[h], lt[h]) for h in heads]
    u = [dot(t[h], from_state[h][:c] + akv[h]) for h in heads]
    vu = [jnp.concatenate([-u[h], v[h]], axis=0) for h in heads]
    y = [from_state[h][c:] + dot(bot[h], vu[h]) for h in heads]
    s_new = [s[h] * pc[h] + tn(vu[h], rhs[h] * pc[h]) for h in heads]
    for q in range(nb):
        mine = slice(q * RWKV_HEADS, (q + 1) * RWKV_HEADS)
        y_ref[q] = jnp.concatenate(y[mine], axis=1)
        s_ref[q] = jnp.concatenate(s_new[mine], axis=1)


def _rwkv_post_kernel(y_ref, bonus_ref, g_ref, pg_ref, m_ref, gg_ref, gb_ref, bdm_ref, pj_ref, o_ref):
    y = y_ref[0]
    mu = _head_sums(y, bdm_ref[...])
    d = y - mu
    var = _head_sums(d * d, bdm_ref[...])
    yn = d * lax.rsqrt(var + RWKV_GN_EPS) * gg_ref[...] + gb_ref[...] + bonus_ref[0]
    o = _bdot(yn * g_ref[0], pj_ref[...])
    o_ref[0] = m_ref[0] + pg_ref[0].astype(F32) * o


def _rwkv(p, merged, mu_pad, w0, wup_pad, a0, aup_pad, g_up, k_k, k_a, r_k, gn_g, gn_b, proj_bf):
    bsz, seq, _ = p.shape
    row = lambda a: a.reshape(1, -1)
    full = lambda shape: pl.BlockSpec(shape, lambda b, i: (0,) * len(shape))
    head_id = jnp.arange(RWKV_DIM, dtype=I32) // RWKV_HEAD
    bd = (head_id[:, None] == head_id[None, :]).astype(BF16)

    ts = min(seq, 256)
    t_id = jnp.arange(ts, dtype=I32)
    ltri = ((t_id[:, None] // RWKV_CHUNK == t_id[None, :] // RWKV_CHUNK)
            & (t_id[:, None] >= t_id[None, :])).astype(BF16)
    seq_blk = lambda width: pl.BlockSpec((1, ts, width), lambda b, i: (b, i, 0))
    wide = SDS((bsz, seq, RWKV_DIM), F32)
    rt, kkt, kh, bh, v, pinc, bonus, g = pl.pallas_call(
        _rwkv_prep_kernel,
        out_shape=[wide] * 8,
        grid=(bsz, seq // ts),
        in_specs=[pl.BlockSpec((1, ts, 2048), lambda b, i: (b, i, SEG_B // 2048)),
                  full((1, 2048)), full((1, RWKV_DIM)), full((LANES, RWKV_DIM)),
                  full((1, RWKV_DIM)), full((LANES, RWKV_DIM)), full((LORA_G, RWKV_DIM)),
                  full((1, RWKV_DIM)), full((1, RWKV_DIM)), full((1, RWKV_DIM)),
                  full((RWKV_DIM, RWKV_DIM)), full((ts, ts))],
        out_specs=[seq_blk(RWKV_DIM)] * 8,
        scratch_shapes=[pltpu.VMEM((ts + SUBLANES, 2048), F32)],
        compiler_params=_cparams(("arbitrary", "arbitrary")),
        name="rwkv_prep",
    )(p, mu_pad, row(w0), wup_pad, row(a0), aup_pad, g_up, row(k_k), row(k_a), row(r_k), bd, ltri)

    c = RWKV_CHUNK
    nb = 2 if bsz % 2 == 0 else 1
    chunk_blk = pl.BlockSpec((nb, c, RWKV_DIM), lambda b, i: (b, i, 0))
    y = pl.pallas_call(
        _rwkv_scan_kernel,
        out_shape=wide,
        grid=(bsz // nb, seq // c),
        in_specs=[chunk_blk] * 6,
        out_specs=chunk_blk,
        scratch_shapes=[pltpu.VMEM((nb, RWKV_HEAD, RWKV_DIM), F32)],
        compiler_params=_cparams(("arbitrary", "arbitrary")),
        name="rwkv_scan",
    )(rt, kkt, kh, bh, v, pinc)

    tp = min(seq, 512)
    blk = lambda width: pl.BlockSpec((1, tp, width), lambda b, i: (b, i, 0))
    return pl.pallas_call(
        _rwkv_post_kernel,
        out_shape=SDS((bsz, seq, D_MODEL), F32),
        grid=(bsz, seq // tp),
        in_specs=[blk(RWKV_DIM), blk(RWKV_DIM), blk(RWKV_DIM),
                  pl.BlockSpec((1, tp, D_MODEL), lambda b, i: (b, i, SEG_G // D_MODEL + 1)),
                  blk(D_MODEL), full((1, RWKV_DIM)), full((1, RWKV_DIM)),
                  full((RWKV_DIM, RWKV_DIM)), full((RWKV_DIM, D_MODEL))],
        out_specs=blk(D_MODEL),
        compiler_params=_cparams(("arbitrary", "arbitrary")),
        name="rwkv_post",
    )(y, bonus, g, p, merged, row(gn_g), row(gn_b), bd * (1.0 / RWKV_HEAD), proj_bf)


def _out_kernel(m_ref, x_ref, g1_ref, w_ref, n2_ref, sc_ref, sh_ref, xo_ref, h_ref):
    xn = x_ref[0] + g1_ref[0] * _bdot(m_ref[0], w_ref[...])
    xo_ref[0] = xn
    h_ref[0] = _modulated_rmsnorm(xn, n2_ref[...], sc_ref[0], sh_ref[0])


def _out_proj(merged, x, g1, w_bf, norm2, sc2, sh2):
    bsz, seq, _ = x.shape
    ts = min(seq, 512)
    blk = pl.BlockSpec((1, ts, D_MODEL), lambda b, i: (b, i, 0))
    per_b = pl.BlockSpec((1, 1, D_MODEL), lambda b, i: (b, 0, 0))
    return pl.pallas_call(
        _out_kernel,
        out_shape=[SDS((bsz, seq, D_MODEL), F32)] * 2,
        grid=(bsz, seq // ts),
        in_specs=[blk, blk, per_b, pl.BlockSpec((D_MODEL, D_MODEL), lambda b, i: (0, 0)),
                  pl.BlockSpec((1, D_MODEL), lambda b, i: (0, 0)), per_b, per_b],
        out_specs=[blk, blk],
        compiler_params=_cparams(("arbitrary", "arbitrary")),
        name="out_proj",
    )(merged, x, g1, w_bf, norm2, sc2, sh2)


MOE_TILE = 256
MOE_SLOTS = 2560
MOE_CHUNK = 32
U32 = jnp.uint32


def _route_kernel(h_ref, rw_ref, bias_ref, upper_ref, ltri_ref, slot_ref, w_ref, n8_ref):
    tile = h_ref.shape[0]
    neg = -jnp.inf
    scores = jax.nn.sigmoid(_hdot_nt(rw_ref[...], h_ref[...]))
    s3 = scores.reshape(GROUP_SIZE, N_GROUPS, tile)
    b3 = s3 + bias_ref[...].reshape(GROUP_SIZE, N_GROUPS, tile)
    slab = lax.broadcasted_iota(I32, b3.shape, 0).astype(F32)
    grp = lax.broadcasted_iota(I32, b3.shape, 1).astype(F32)
    eid = grp * GROUP_SIZE + slab
    m1 = jnp.max(b3, axis=0, keepdims=True)
    first = jnp.min(jnp.where(b3 == m1, slab, GROUP_SIZE), axis=0, keepdims=True)
    m2 = jnp.max(jnp.where(slab == first, neg, b3), axis=0, keepdims=True)
    gs = (m1 + m2)[0]
    gi = lax.broadcasted_iota(I32, gs.shape, 0).astype(F32)
    chosen = jnp.zeros(gs.shape, F32)
    for _ in range(TOPK_GROUPS):
        m = jnp.max(gs, axis=0, keepdims=True)
        hit = gi == jnp.min(jnp.where(gs == m, gi, N_GROUPS), axis=0, keepdims=True)
        chosen = jnp.where(hit, 1.0, chosen)
        gs = jnp.where(hit, neg, gs)
    cur = jnp.where((chosen > 0.0)[None], b3, neg)
    ones_cols = jnp.ones((tile, LANES), BF16)
    lanes_of = lambda a: jnp.concatenate([a] * (tile // LANES), axis=1)
    to3 = lambda a: a.reshape(GROUP_SIZE, N_GROUPS, tile)
    fold = lambda a: jnp.sum(jnp.sum(a, axis=0), axis=0, keepdims=True)
    base = jnp.zeros((N_EXPERTS, LANES), F32)
    picks, w_rows, rank_rows = [], [], []
    for _ in range(TOP_K):
        m = jnp.max(jnp.max(cur, axis=0), axis=0, keepdims=True)[None]
        pick = jnp.min(jnp.min(jnp.where(cur == m, eid, N_EXPERTS), axis=0), axis=0, keepdims=True)
        hit = eid == pick[None]
        w_rows.append(fold(jnp.where(hit, s3, 0.0)))
        picks.append(pick)
        cur = jnp.where(hit, neg, cur)
        onehot = hit.astype(BF16).reshape(N_EXPERTS, tile)
        before = jnp.dot(onehot, upper_ref[...], preferred_element_type=F32)
        rank_rows.append(fold(jnp.where(hit, to3(before + lanes_of(base)), 0.0)))
        base = base + jnp.dot(onehot, ones_cols, preferred_element_type=F32)
    n8 = jnp.floor((base + (SUBLANES - 1.0)) * (1.0 / SUBLANES)) * SUBLANES
    run_start = to3(lanes_of(_hdot(ltri_ref[...], n8)))
    slots = [rank_rows[j] + fold(jnp.where(eid == picks[j][None], run_start, 0.0)) for j in range(TOP_K)]
    n8_ref[0] = n8
    w_all = jnp.concatenate(w_rows, axis=0)
    w_ref[...] = w_all / jnp.sum(w_all, axis=0, keepdims=True) * ROUTED_SCALE
    slot_ref[...] = jnp.concatenate(slots, axis=0).astype(I32)


def _route(h2, router_w, router_bias):
    n_tok = h2.shape[0]
    tile = MOE_TILE
    n_tiles = n_tok // tile
    regroup = lambda a: a.reshape(N_GROUPS, GROUP_SIZE, -1).transpose(1, 0, 2).reshape(N_EXPERTS, -1)
    rw = regroup(router_w.T)
    bias = jnp.broadcast_to(regroup(router_bias.reshape(N_EXPERTS, 1)), (N_EXPERTS, tile))
    t_id = jnp.arange(tile, dtype=I32)
    upper = (t_id[:, None] < t_id[None, :]).astype(BF16)
    e_id = jnp.arange(N_EXPERTS, dtype=I32)
    ltri = (e_id[:, None] > e_id[None, :]).astype(F32)
    tok_blk = pl.BlockSpec((TOP_K, tile), lambda i: (0, i))
    full = lambda shape: pl.BlockSpec(shape, lambda i: (0,) * len(shape))
    slot_t, w_t, n8 = pl.pallas_call(
        _route_kernel,
        out_shape=[SDS((TOP_K, n_tok), I32), SDS((TOP_K, n_tok), F32),
                   SDS((n_tiles, N_EXPERTS, LANES), F32)],
        grid=(n_tiles,),
        in_specs=[pl.BlockSpec((tile, D_MODEL), lambda i: (i, 0)),
                  full((N_EXPERTS, D_MODEL)), full((N_EXPERTS, tile)), full((tile, tile)),
                  full((N_EXPERTS, N_EXPERTS))],
        out_specs=[tok_blk, tok_blk, pl.BlockSpec((1, N_EXPERTS, LANES), lambda i: (i, 0, 0))],
        compiler_params=_cparams(("arbitrary",)),
        name="moe_route",
    )(h2, rw, bias, upper, ltri)
    return slot_t, w_t, n8[:, :, 0].astype(I32)


def _for_each_run_piece(tile_idx, n8_ref, fn):
    def per_expert(r, carry):
        n = n8_ref[tile_idx * N_EXPERTS + r]
        whole = n // MOE_CHUNK

        def chunk(k, c):
            fn(r, k * MOE_CHUNK, MOE_CHUNK)
            return c
        lax.fori_loop(0, whole, chunk, 0)
        rest = whole * MOE_CHUNK
        for size in (16, 8):
            @pl.when((n & size) != 0)
            def _(size=size):
                fn(r, rest + (n & (MOE_CHUNK - 1) & ~(2 * size - 1)), size)
        return carry
    lax.fori_loop(0, N_EXPERTS, per_expert, 0)


def _wait_rows(total, make_wait):
    size = SUBLANES
    while size <= MOE_SLOTS:
        @pl.when((total & size) != 0)
        def _(size=size):
            make_wait(size).wait()
        size *= 2


def _pack_pairs(hi_bits, lo_bits):
    return (hi_bits & jnp.uint32(0xFFFF0000)) | (lo_bits >> 16)


def _unpack_pairs(u):
    hi = lax.bitcast_convert_type(u & jnp.uint32(0xFFFF0000), F32)
    lo = lax.bitcast_convert_type(u << 16, F32)
    return jnp.concatenate([hi, lo], axis=1).astype(BF16)


def _dispatch_kernel(n8_ref, off_ref, dst_ref, tot_ref, pad_end_ref, slot_ref, h_ref, xs_ref,
                     g_ref, zero_ref, sem):
    i = pl.program_id(0)
    tile = h_ref.shape[0]
    half = D_MODEL // 2

    @pl.when(i == 0)
    def _():
        zero_ref[...] = jnp.zeros(zero_ref.shape, U32)

        def last_block(e):
            start = pl.multiple_of(jnp.maximum(pad_end_ref[e] - MOE_ROWS, 0), MOE_ROWS)
            return pltpu.make_async_copy(zero_ref, xs_ref.at[pl.ds(start, MOE_ROWS), :], sem.at[0])

        def has_rows(e):
            prev = jnp.where(e > 0, pad_end_ref[jnp.maximum(e - 1, 0)], 0)
            return pad_end_ref[e] > prev

        def clear(e, carry):
            @pl.when(has_rows(e))
            def _():
                last_block(e).start()
            return carry
        lax.fori_loop(0, N_EXPERTS, clear, 0)

        def done(e, carry):
            @pl.when(has_rows(e))
            def _():
                last_block(e).wait()
            return carry
        lax.fori_loop(0, N_EXPERTS, done, 0)

        def tail_block(b):
            start = pl.multiple_of(b * MOE_ROWS, MOE_ROWS)
            return pltpu.make_async_copy(zero_ref, xs_ref.at[pl.ds(start, MOE_ROWS), :], sem.at[0])

        def clear_tail(b, carry):
            tail_block(b).start()
            return carry

        def done_tail(b, carry):
            tail_block(b).wait()
            return carry
        used = pad_end_ref[N_EXPERTS - 1] // MOE_ROWS
        lax.fori_loop(used, xs_ref.shape[0] // MOE_ROWS, clear_tail, 0)
        lax.fori_loop(used, xs_ref.shape[0] // MOE_ROWS, done_tail, 0)

    slot_id = lax.broadcasted_iota(I32, (MOE_SLOTS, tile), 0)
    sel = jnp.zeros((MOE_SLOTS, tile), F32)
    for j in range(TOP_K):
        sel = jnp.where(slot_id == slot_ref[pl.ds(j, 1), :], 1.0, sel)
    g = jnp.dot(sel.astype(BF16), h_ref[...].astype(BF16), preferred_element_type=F32)
    bits = lax.bitcast_convert_type(g, U32)
    buf = i % 2
    g_ref[buf] = _pack_pairs(bits[:, :half], bits[:, half:])

    def piece(r, offset, rows):
        src = pl.multiple_of(off_ref[i * N_EXPERTS + r] + offset, SUBLANES)
        dst = pl.multiple_of(dst_ref[i * N_EXPERTS + r] + offset, SUBLANES)
        pltpu.make_async_copy(g_ref.at[buf, pl.ds(src, rows), :], xs_ref.at[pl.ds(dst, rows), :],
                              sem.at[buf]).start()

    _for_each_run_piece(i, n8_ref, piece)

    def drain(step):
        _wait_rows(tot_ref[step], lambda rows: pltpu.make_async_copy(
            g_ref.at[step % 2, pl.ds(0, rows), :], xs_ref.at[pl.ds(0, rows), :], sem.at[step % 2]))

    pl.when(i > 0)(lambda: drain(i - 1))
    pl.when(i == pl.num_programs(0) - 1)(lambda: drain(i))


def _dispatch(h2, slot_t, n8_flat, off_flat, dst_flat, tot, pad_end, n_rows):
    n_tok = h2.shape[0]
    tile = MOE_TILE
    grid_spec = pltpu.PrefetchScalarGridSpec(
        num_scalar_prefetch=5,
        grid=(n_tok // tile,),
        in_specs=[pl.BlockSpec((TOP_K, tile), lambda i, *_: (0, i)),
                  pl.BlockSpec((tile, D_MODEL), lambda i, *_: (i, 0))],
        out_specs=pl.BlockSpec(memory_space=pl.ANY),
        scratch_shapes=[pltpu.VMEM((2, MOE_SLOTS, D_MODEL // 2), U32),
                        pltpu.VMEM((MOE_ROWS, D_MODEL // 2), U32), pltpu.SemaphoreType.DMA((2,))],
    )
    return pl.pallas_call(
        _dispatch_kernel,
        out_shape=SDS((n_rows, D_MODEL // 2), U32),
        grid_spec=grid_spec,
        compiler_params=_cparams(("arbitrary",)),
        name="moe_dispatch",
    )(n8_flat, off_flat, dst_flat, tot, pad_end, slot_t, h2)


def _expert_kernel(first_ref, nblk_ref, exp_of_ref, xs_ref, w1_ref, w3_ref, w2_ref, y_ref,
                   xbuf, ybuf, wb1, wb3, wb2, in_sem, out_sem):
    del exp_of_ref
    r = pl.program_id(0)
    half = D_MODEL // 2
    used = first_ref[N_EXPERTS - 1] + nblk_ref[N_EXPERTS - 1]
    part = MOE_ROWS // EXPERT_DMA_PARTS

    class _Copies:
        def __init__(self, make):
            self.parts = [make(p) for p in range(EXPERT_DMA_PARTS)]

        def start(self):
            for c in self.parts:
                c.start()

        def wait(self):
            for c in self.parts:
                c.wait()

    def hbm_rows(g, p):
        return pl.ds(pl.multiple_of(g * MOE_ROWS + p * part, part), part)

    in_copy = lambda g, slot: _Copies(lambda p: pltpu.make_async_copy(
        xs_ref.at[hbm_rows(g, p), :], xbuf.at[slot, pl.ds(p * part, part), :], in_sem.at[slot]))
    out_copy = lambda g, slot: _Copies(lambda p: pltpu.make_async_copy(
        ybuf.at[slot, pl.ds(p * part, part), :], y_ref.at[hbm_rows(g, p), :], out_sem.at[slot]))

    @pl.when((r == 0) & (used > 0))
    def _():
        in_copy(0, 0).start()

    wb1[...] = w1_ref[0, 0].astype(BF16)
    wb3[...] = w3_ref[0, 0].astype(BF16)
    wb2[...] = w2_ref[0, 0].astype(BF16)

    def block(k, carry):
        g = first_ref[r] + k
        slot = g % 2
        in_copy(g, slot).wait()

        @pl.when(g + 1 < used)
        def _():
            in_copy(g + 1, 1 - slot).start()

        @pl.when(g >= 2)
        def _():
            out_copy(g - 2, slot).wait()

        halves = (pl.ds(0, MOE_ROWS // 2), pl.ds(MOE_ROWS // 2, MOE_ROWS // 2))
        dot = lambda a, b: jnp.dot(a, b, preferred_element_type=F32)
        xs_half = [_unpack_pairs(xbuf[slot, rows, :]) for rows in halves]
        gate = [dot(x, wb1[...]) for x in xs_half]
        up = [dot(x, wb3[...]) for x in xs_half]
        hid = [(jax.nn.silu(a) * b).astype(BF16) for a, b in zip(gate, up)]
        for rows, h in zip(halves, hid):
            y = dot(h, wb2[...]).astype(BF16).astype(F32)
            bits = lax.bitcast_convert_type(y, U32)
            ybuf[slot, rows, :] = _pack_pairs(bits[:, :half], bits[:, half:])
        out_copy(g, slot).start()
        return carry
    lax.fori_loop(0, nblk_ref[r], block, 0)

    @pl.when(r == N_EXPERTS - 1)
    def _():
        @pl.when(used >= 2)
        def _():
            out_copy(used - 2, used % 2).wait()

        @pl.when(used >= 1)
        def _():
            out_copy(used - 1, (used - 1) % 2).wait()

        ybuf[0] = jnp.zeros(ybuf.shape[1:], U32)
        n_blocks = y_ref.shape[0] // MOE_ROWS

        def clear(g, carry):
            out_copy(g, 0).start()
            return carry

        def done(g, carry):
            out_copy(g, 0).wait()
            return carry
        lax.fori_loop(used, n_blocks, clear, 0)
        lax.fori_loop(used, n_blocks, done, 0)


def _experts(xs, first_block, n_block, exp_of_row, w1, w3, w2, layer):
    n_rows = xs.shape[0]
    half = D_MODEL // 2
    w_in_blk = pl.BlockSpec((1, 1, D_MODEL, EXPERT_FF), lambda r, first, nblk, eo: (layer, eo[r], 0, 0))
    grid_spec = pltpu.PrefetchScalarGridSpec(
        num_scalar_prefetch=3,
        grid=(N_EXPERTS,),
        in_specs=[pl.BlockSpec(memory_space=pl.ANY), w_in_blk, w_in_blk,
                  pl.BlockSpec((1, 1, EXPERT_FF, D_MODEL), lambda r, first, nblk, eo: (layer, eo[r], 0, 0))],
        out_specs=pl.BlockSpec(memory_space=pl.ANY),
        scratch_shapes=[pltpu.VMEM((2, MOE_ROWS, half), U32), pltpu.VMEM((2, MOE_ROWS, half), U32),
                        pltpu.VMEM((D_MODEL, EXPERT_FF), BF16), pltpu.VMEM((D_MODEL, EXPERT_FF), BF16),
                        pltpu.VMEM((EXPERT_FF, D_MODEL), BF16),
                        pltpu.SemaphoreType.DMA((2,)), pltpu.SemaphoreType.DMA((2,))],
    )
    return pl.pallas_call(
        _expert_kernel,
        out_shape=SDS((n_rows, half), U32),
        grid_spec=grid_spec,
        compiler_params=_cparams(("arbitrary",)),
        name="moe_experts",
    )(first_block, n_block, exp_of_row, xs, w1, w3, w2)


def _combine_kernel(n8_ref, off_ref, src_ref, tot_ref, y_ref, slot_ref, w_ref, h_ref, x_ref, g2_ref,
                    s1_ref, s3_ref, s2_ref, fn_ref, o_ref, yt_ref, sem, *, final):
    tile = h_ref.shape[1]
    i = pl.program_id(0) * pl.num_programs(1) + pl.program_id(1)

    def fetch(step):
        def piece(r, offset, rows):
            src = pl.multiple_of(src_ref[step * N_EXPERTS + r] + offset, SUBLANES)
            dst = pl.multiple_of(off_ref[step * N_EXPERTS + r] + offset, SUBLANES)
            pltpu.make_async_copy(y_ref.at[pl.ds(src, rows), :], yt_ref.at[step % 2, pl.ds(dst, rows), :],
                                  sem.at[step % 2]).start()
        _for_each_run_piece(step, n8_ref, piece)

    @pl.when(i == 0)
    def _():
        yt_ref[...] = jnp.zeros(yt_ref.shape, U32)
        fetch(i)

    pl.when(i + 1 < pl.num_programs(0) * pl.num_programs(1))(lambda: fetch(i + 1))
    h = h_ref[0]
    shared = _bdot(jax.nn.silu(_bdot(h, s1_ref[...])) * _bdot(h, s3_ref[...]), s2_ref[...])
    slot_id = lax.broadcasted_iota(I32, (tile, MOE_SLOTS), 1)
    slots = slot_ref[0]
    w = w_ref[0]
    pw = jnp.zeros((tile, MOE_SLOTS), F32)
    for j in range(TOP_K):
        pw = jnp.where(slot_id == slots[:, j:j + 1], w[:, j:j + 1], pw)
    _wait_rows(tot_ref[i], lambda rows: pltpu.make_async_copy(
        y_ref.at[pl.ds(0, rows), :], yt_ref.at[i % 2, pl.ds(0, rows), :], sem.at[i % 2]))
    routed = jnp.dot(pw.astype(BF16), _unpack_pairs(yt_ref[i % 2]), preferred_element_type=F32)
    xn = x_ref[0] + g2_ref[0] * (routed + shared)
    if final:
        xn = xn * lax.rsqrt(jnp.mean(xn * xn, axis=-1, keepdims=True) + NORM_EPS) * fn_ref[...]
    o_ref[0] = xn


def _combine(y, slot_nat, w_nat, n8_flat, off_flat, dst_flat, tot, h2, x, g2, s1_bf, s3_bf, s2_bf,
             final_norm, final):
    bsz, seq, _ = x.shape
    tile = MOE_TILE
    per_seq = seq // tile
    blk = pl.BlockSpec((1, tile, D_MODEL), lambda b, i, *_: (b, i, 0))
    tok = pl.BlockSpec((1, tile, TOP_K), lambda b, i, *_: (b, i, 0))
    full = lambda shape: pl.BlockSpec(shape, lambda b, i, *_: (0,) * len(shape))
    grid_spec = pltpu.PrefetchScalarGridSpec(
        num_scalar_prefetch=4,
        grid=(bsz, per_seq),
        in_specs=[pl.BlockSpec(memory_space=pl.ANY), tok, tok, blk, blk,
                  pl.BlockSpec((1, 1, D_MODEL), lambda b, i, *_: (b, 0, 0)),
                  full((D_MODEL, SHARED_FF)), full((D_MODEL, SHARED_FF)), full((SHARED_FF, D_MODEL)),
                  full((1, D_MODEL))],
        out_specs=blk,
        scratch_shapes=[pltpu.VMEM((2, MOE_SLOTS, D_MODEL // 2), U32), pltpu.SemaphoreType.DMA((2,))],
    )
    return pl.pallas_call(
        functools.partial(_combine_kernel, final=final),
        out_shape=SDS((bsz, seq, D_MODEL), F32),
        grid_spec=grid_spec,
        compiler_params=_cparams(("arbitrary", "arbitrary")),
        name="moe_combine",
    )(n8_flat, off_flat, dst_flat, tot, y, slot_nat, w_nat, h2, x, g2, s1_bf, s3_bf, s2_bf, final_norm)


def _moe(x, h2, g2, router_w, router_bias, w1, w3, w2, layer, s1_bf, s3_bf, s2_bf, final_norm, final):
    bsz, seq, _ = x.shape
    assert seq % MOE_TILE == 0, "token tiles must not straddle sequences"
    n_tok = bsz * seq
    slot_t, w_t, n8 = _route(h2.reshape(n_tok, D_MODEL), router_w, router_bias)
    n_tiles = n8.shape[0]
    counts = jnp.sum(n8, axis=0)
    padded = (counts + MOE_ROWS - 1) // MOE_ROWS * MOE_ROWS
    pad_end = jnp.cumsum(padded).astype(I32)
    pad_start = pad_end - padded
    run_row = (pad_start[None, :] + jnp.cumsum(n8, axis=0) - n8).astype(I32)
    run_slot = (jnp.cumsum(n8, axis=1) - n8).astype(I32)
    tot = jnp.sum(n8, axis=1).astype(I32)
    max_rows = n_tok * TOP_K + n_tiles * N_EXPERTS * (SUBLANES - 1) + N_EXPERTS * (MOE_ROWS - 1)
    n_blocks = (max_rows + MOE_ROWS - 1) // MOE_ROWS
    row_id = jnp.arange(N_EXPERTS, dtype=I32)
    exp_of_row = (row_id % N_GROUPS) * GROUP_SIZE + row_id // N_GROUPS
    flat = lambda a: a.reshape(-1).astype(I32)
    xs = _dispatch(h2.reshape(n_tok, D_MODEL), slot_t, flat(n8), flat(run_slot), flat(run_row), tot,
                   pad_end, n_blocks * MOE_ROWS)
    y = _experts(xs, (pad_start // MOE_ROWS).astype(I32), (padded // MOE_ROWS).astype(I32), exp_of_row,
                 w1, w3, w2, layer)
    nat = lambda a: a.T.reshape(bsz, seq, TOP_K)
    return _combine(y, nat(slot_t), nat(w_t), flat(n8), flat(run_slot), flat(run_row), tot, h2, x, g2,
                    s1_bf, s3_bf, s2_bf, final_norm, final)


def _pad_cols(a, width):
    return jnp.pad(a, ((0, 0), (0, width - a.shape[1])))


def _layout_w_in(w_in):
    b0 = COLS_A
    c0 = COLS_A + COLS_B
    g0 = c0 + COLS_C
    seg_b = _pad_cols(w_in[:, b0:c0], SEG_G - SEG_B)
    return jnp.concatenate([w_in[:, c0:g0], seg_b, w_in[:, g0:], w_in[:, :b0]], axis=1).astype(BF16)


def _layout_mu(mu):
    return _pad_cols(mu.reshape(1, -1), SEG_G - SEG_B)


def _pad_rows(a, height):
    return jnp.pad(a, ((0, height - a.shape[0]), (0, 0)))


def kernel(x, c, ada_w, ada_b, norm1, norm2, w_in, conv_a_w, conv_a_b, ln_a_g, ln_a_b, proj_a, mu_b, w0, w_up, a0, a_up, g_up, k_k, k_a, r_k, gn_b_g, gn_b_b, proj_b, conv_c_w, conv_c_b, lru_wa, lru_ba, lru_wx, lru_bx, lru_lambda, proj_c, w_out, router_w, router_bias, exp_w1, exp_w3, exp_w2, sh_w1, sh_w3, sh_w2, final_norm):
    depth = ada_w.shape[0]
    bsz = x.shape[0]
    mod = _ada_mod(c, ada_w, ada_b)
    for l in range(depth):
        sh1, sc1, g1, sh2, sc2, g2 = [mod[l, :, i * D_MODEL:(i + 1) * D_MODEL].reshape(bsz, 1, D_MODEL)
                                      for i in range(N_MOD)]
        p = _in_proj(x, sc1, sh1, norm1[l].reshape(1, -1), _layout_w_in(w_in[l]))
        merged = _conv_a(p, conv_a_w[l], conv_a_b[l], ln_a_g[l], ln_a_b[l], proj_a[l].astype(BF16))
        merged = _rwkv(p, merged, _layout_mu(mu_b[l]), w0[l], _pad_rows(w_up[l], LANES).astype(BF16),
                       a0[l], jnp.pad(a_up[l], ((LORA_W, 0), (0, 0))).astype(BF16), g_up[l].astype(BF16),
                       k_k[l], k_a[l], r_k[l], gn_b_g[l], gn_b_b[l], proj_b[l].astype(BF16))
        merged = _lru(p, merged, conv_c_w[l], conv_c_b[l], lru_wa[l].astype(BF16), lru_ba[l],
                      lru_wx[l].astype(BF16), lru_bx[l], lru_lambda[l], proj_c[l].astype(BF16))
        x, h2 = _out_proj(merged, x, g1, w_out[l].astype(BF16), norm2[l].reshape(1, -1), sc2, sh2)
        x = _moe(x, h2, g2, router_w[l], router_bias[l], exp_w1, exp_w3, exp_w2, l,
                 sh_w1[l].astype(BF16), sh_w3[l].astype(BF16), sh_w2[l].astype(BF16),
                 final_norm.reshape(1, -1), final=(l == depth - 1))
    return x
```

```python
import functools

import jax
import jax.numpy as jnp
from jax import lax
from jax.experimental import pallas as pl
from jax.experimental.pallas import tpu as pltpu

F32 = jnp.float32
BF16 = jnp.bfloat16
I32 = jnp.int32
SDS = jax.ShapeDtypeStruct
HIGHEST = lax.Precision.HIGHEST

D_MODEL = 1024
N_MOD = 6
NORM_EPS = 1e-6
CONV_A_CH = 512
CONV_A_WIDTH = 31
CONV_A_LN_EPS = 1e-5
RWKV_HEADS = 8
RWKV_HEAD = 64
RWKV_DIM = RWKV_HEADS * RWKV_HEAD
LORA_W = 64
LORA_A = 64
LORA_G = 128
RWKV_GN_EPS = 64e-5
RWKV_CHUNK = 64
LRU_DIM = 1024
LRU_HEADS = 8
LRU_BLOCK = LRU_DIM // LRU_HEADS
LRU_CONV = 4
LRU_C = 8.0
N_EXPERTS = 64
TOP_K = 8
N_GROUPS = 8
GROUP_SIZE = N_EXPERTS // N_GROUPS
TOPK_GROUPS = 4
EXPERT_FF = 256
SHARED_FF = 256
ROUTED_SCALE = 2.5
SEG_C = 0
SEG_B = 2048
SEG_G = 4096
SEG_A = 7168
IN_COLS_PAD = 8192
RW_R, RW_K, RW_V, RW_XWA, RW_XG = 0, 512, 1024, 1536, 1664
COLS_A = 2 * CONV_A_CH
COLS_B = 3 * RWKV_DIM + LORA_W + LORA_A + LORA_G
COLS_C = 2 * LRU_DIM
VMEM_LIMIT = 56 * 1024 * 1024
SUBLANES = 8
LANES = 128
MOE_ROWS = 512
EXPERT_DMA_PARTS = 4


def _cparams(sem):
    return pltpu.CompilerParams(dimension_semantics=sem, vmem_limit_bytes=VMEM_LIMIT)


def _bdot(a, b):
    return jnp.dot(a.astype(BF16), b.astype(BF16), preferred_element_type=F32)


def _hdot(a, b):
    return jnp.dot(a, b, preferred_element_type=F32, precision=HIGHEST)


def _split(a):
    hi = a.astype(BF16)
    return hi, (a - hi.astype(F32)).astype(BF16)


def _head_sums(a, ones_bf):
    hi, lo = _split(a)
    return (jnp.dot(hi, ones_bf, preferred_element_type=F32)
            + jnp.dot(lo, ones_bf, preferred_element_type=F32))


def _hdot_nt(a, b):
    return lax.dot_general(a, b, (((1,), (1,)), ((), ())), preferred_element_type=F32,
                           precision=HIGHEST)


def _hdot_tn(a, b):
    return lax.dot_general(a, b, (((0,), (0,)), ((), ())), preferred_element_type=F32,
                           precision=HIGHEST)


def _ada_kernel(c_ref, w_ref, b_ref, o_ref):
    cond = jax.nn.silu(c_ref[...])
    o_ref[0] = _bdot(cond, w_ref[0]) + b_ref[0]


def _ada_mod(c, ada_w, ada_b):
    depth, _, n = ada_w.shape
    bsz = c.shape[0]
    tn = 1536
    return pl.pallas_call(
        _ada_kernel,
        out_shape=SDS((depth, bsz, n), F32),
        grid=(depth, n // tn),
        in_specs=[pl.BlockSpec((bsz, D_MODEL), lambda l, j: (0, 0)),
                  pl.BlockSpec((1, D_MODEL, tn), lambda l, j: (l, 0, j)),
                  pl.BlockSpec((1, 1, tn), lambda l, j: (l, 0, j))],
        out_specs=pl.BlockSpec((1, bsz, tn), lambda l, j: (l, 0, j)),
        compiler_params=_cparams(("arbitrary", "arbitrary")),
        name="ada_mod",
    )(c, ada_w, ada_b.reshape(depth, 1, n))


def _modulated_rmsnorm(x, g, sc, sh):
    y = x * lax.rsqrt(jnp.mean(x * x, axis=-1, keepdims=True) + NORM_EPS)
    return (y * g) * (1.0 + sc) + sh


IN_PROJ_SUB = 512


def _in_kernel(x_ref, sc_ref, sh_ref, g_ref, w_ref, o_ref, h_ref):
    @pl.when(pl.program_id(2) == 0)
    def _():
        h_ref[...] = _modulated_rmsnorm(x_ref[0], g_ref[...], sc_ref[0], sh_ref[0]).astype(BF16)

    j = pl.program_id(2)
    ts, tn = o_ref.shape[1], o_ref.shape[2]
    is_gelu = j == SEG_C // tn
    is_gate = (j >= SEG_G // tn) & (j < SEG_A // tn)
    is_glu = j == SEG_A // tn

    def glu(acc):
        u = acc[:, :tn // 2] * jax.nn.sigmoid(acc[:, tn // 2:])
        return jnp.concatenate([u, jnp.zeros_like(u)], axis=1)

    def emit(act, sub):
        for r0 in range(0, ts, sub):
            rows = pl.ds(r0, sub)
            acc = jnp.dot(h_ref[rows, :], w_ref[...], preferred_element_type=F32)
            o_ref[0, rows, :] = act(acc).astype(o_ref.dtype)

    sub = min(ts, IN_PROJ_SUB)
    pl.when(is_gelu)(lambda: emit(functools.partial(jax.nn.gelu, approximate=True), sub))
    pl.when(is_gate)(lambda: emit(jax.nn.sigmoid, sub))
    pl.when(is_glu)(lambda: emit(glu, sub))
    pl.when(jnp.logical_not(is_gelu | is_gate | is_glu))(lambda: emit(lambda acc: acc, ts))


def _in_proj(x, sc, sh, g, w_pad):
    bsz, seq, _ = x.shape
    ts = min(seq, 1024)
    tn = 1024
    return pl.pallas_call(
        _in_kernel,
        out_shape=SDS((bsz, seq, IN_COLS_PAD), BF16),
        grid=(bsz, seq // ts, IN_COLS_PAD // tn),
        in_specs=[pl.BlockSpec((1, ts, D_MODEL), lambda b, i, j: (b, i, 0)),
                  pl.BlockSpec((1, 1, D_MODEL), lambda b, i, j: (b, 0, 0)),
                  pl.BlockSpec((1, 1, D_MODEL), lambda b, i, j: (b, 0, 0)),
                  pl.BlockSpec((1, D_MODEL), lambda b, i, j: (0, 0)),
                  pl.BlockSpec((D_MODEL, tn), lambda b, i, j: (0, j))],
        out_specs=pl.BlockSpec((1, ts, tn), lambda b, i, j: (b, i, j)),
        scratch_shapes=[pltpu.VMEM((ts, D_MODEL), BF16)],
        compiler_params=_cparams(("arbitrary", "arbitrary", "arbitrary")),
        name="in_proj",
    )(x, sc, sh, g, w_pad)


CONV_A_HALO = 32
CONV_A_SUB = 64


def _conv_a_kernel(pa_ref, pg_ref, cw_ref, cb_ref, lg_ref, lb_ref, pj_ref, o_ref, ext_ref, sh_ref, y_ref):
    ts = pa_ref.shape[1]

    @pl.when(pl.program_id(1) == 0)
    def _():
        ext_ref[pl.ds(0, CONV_A_HALO), :] = jnp.zeros((CONV_A_HALO, CONV_A_CH), F32)

    ext_ref[pl.ds(CONV_A_HALO, ts), :] = pa_ref[0].astype(F32)
    for p in range(1, SUBLANES):
        sh_ref[p - 1] = ext_ref[pl.ds(p, sh_ref.shape[1]), :]
    first = CONV_A_HALO - (CONV_A_WIDTH - 1)
    for r0 in range(0, ts, CONV_A_SUB):
        acc = jnp.zeros((CONV_A_SUB, CONV_A_CH), F32) + cb_ref[...]
        for j in range(CONV_A_WIDTH):
            phase = (first + j) % SUBLANES
            rows = pl.ds(r0 + first + j - phase, CONV_A_SUB)
            tap = ext_ref[rows, :] if phase == 0 else sh_ref[phase - 1, rows, :]
            acc = acc + tap * cw_ref[pl.ds(j, 1), :]
        y_ref[pl.ds(r0, CONV_A_SUB), :] = acc
    ext_ref[pl.ds(0, CONV_A_HALO), :] = ext_ref[pl.ds(ts, CONV_A_HALO), :]
    y = y_ref[...]
    mu = jnp.mean(y, axis=-1, keepdims=True)
    d = y - mu
    var = jnp.mean(d * d, axis=-1, keepdims=True)
    yn = d * lax.rsqrt(var + CONV_A_LN_EPS) * lg_ref[...] + lb_ref[...]
    o = _bdot(jax.nn.silu(yn), pj_ref[...])
    o_ref[0] = pg_ref[0].astype(F32) * o


def _conv_a(p, conv_w, conv_b, ln_g, ln_b, proj_bf):
    bsz, seq, _ = p.shape
    ts = min(seq, 512)
    row = lambda a: a.reshape(1, -1)
    full = lambda shape: pl.BlockSpec(shape, lambda b, i: (0,) * len(shape))
    return pl.pallas_call(
        _conv_a_kernel,
        out_shape=SDS((bsz, seq, D_MODEL), F32),
        grid=(bsz, seq // ts),
        in_specs=[pl.BlockSpec((1, ts, CONV_A_CH), lambda b, i: (b, i, SEG_A // CONV_A_CH)),
                  pl.BlockSpec((1, ts, D_MODEL), lambda b, i: (b, i, SEG_G // D_MODEL)),
                  full((CONV_A_WIDTH, CONV_A_CH)), full((1, CONV_A_CH)), full((1, CONV_A_CH)),
                  full((1, CONV_A_CH)), full((CONV_A_CH, D_MODEL))],
        out_specs=pl.BlockSpec((1, ts, D_MODEL), lambda b, i: (b, i, 0)),
        scratch_shapes=[pltpu.VMEM((ts + CONV_A_HALO, CONV_A_CH), F32),
                        pltpu.VMEM((SUBLANES - 1, ts + CONV_A_HALO - SUBLANES, CONV_A_CH), F32),
                        pltpu.VMEM((ts, CONV_A_CH), F32)],
        compiler_params=_cparams(("arbitrary", "arbitrary")),
        name="conv_a",
    )(p, p, conv_w, row(conv_b), row(ln_g), row(ln_b), proj_bf)


def _lru_kernel(pc_ref, pg_ref, m_ref, cw_ref, cb_ref, wa_ref, ba_ref, wx_ref, bx_ref, lam_ref,
                pj_ref, o_ref, ext_ref, h_ref, a_ref, b_ref):
    ts = pc_ref.shape[1]
    groups = ts // SUBLANES

    @pl.when(pl.program_id(1) == 0)
    def _():
        ext_ref[pl.ds(0, SUBLANES), :] = jnp.zeros((SUBLANES, LRU_DIM), F32)
        h_ref[...] = jnp.zeros((SUBLANES, LRU_DIM), F32)

    pc = pc_ref[0].astype(F32)
    y_gate = pc[:, :LRU_DIM]
    ext_ref[pl.ds(SUBLANES, ts), :] = pc[:, LRU_DIM:]
    first = SUBLANES - (LRU_CONV - 1)
    xc = jnp.zeros((ts, LRU_DIM), F32) + cb_ref[...]
    for j in range(LRU_CONV):
        xc = xc + ext_ref[pl.ds(first + j, ts), :] * cw_ref[pl.ds(j, 1), :]
    ext_ref[pl.ds(0, SUBLANES), :] = ext_ref[pl.ds(ts, SUBLANES), :]

    def block_diag(w_ref):
        return jnp.concatenate(
            [_bdot(xc[:, h * LRU_BLOCK:(h + 1) * LRU_BLOCK], w_ref[h]) for h in range(LRU_HEADS)],
            axis=1)

    gate_a = jax.nn.sigmoid(block_diag(wa_ref) + ba_ref[...])
    gate_x = jax.nn.sigmoid(block_diag(wx_ref) + bx_ref[...])
    log_a = -LRU_C * gate_a * jax.nn.softplus(-lam_ref[...])
    a = jnp.exp(log_a)
    b = xc * gate_x * jnp.sqrt(1.0 - jnp.exp(2.0 * log_a))

    a3 = a.reshape(groups, SUBLANES, LRU_DIM)
    b3 = b.reshape(groups, SUBLANES, LRU_DIM)
    row = lax.broadcasted_iota(I32, (groups, SUBLANES, LRU_DIM), 1)
    for s in (1, 2, 4):
        keep = row >= s
        b3 = jnp.where(keep, a3 * pltpu.roll(b3, s, axis=1) + b3, b3)
        a3 = jnp.where(keep, a3 * pltpu.roll(a3, s, axis=1), a3)
    a_ref[...] = a3.reshape(ts, LRU_DIM)
    b_ref[...] = b3.reshape(ts, LRU_DIM)
    h = h_ref[...]
    for g in range(groups):
        rows = pl.ds(g * SUBLANES, SUBLANES)
        hg = a_ref[rows, :] * h + b_ref[rows, :]
        b_ref[rows, :] = hg
        h = jnp.broadcast_to(hg[SUBLANES - 1:SUBLANES, :], (SUBLANES, LRU_DIM))
    h_ref[...] = h
    o = _bdot(b_ref[...] * y_gate, pj_ref[...])
    o_ref[0] = m_ref[0] + pg_ref[0].astype(F32) * o


def _lru(p, merged, conv_w, conv_b, wa_bf, ba, wx_bf, bx, lam, proj_bf):
    bsz, seq, _ = p.shape
    ts = min(seq, 256)
    row = lambda a: a.reshape(1, -1)
    full = lambda shape: pl.BlockSpec(shape, lambda b, i: (0,) * len(shape))
    return pl.pallas_call(
        _lru_kernel,
        out_shape=SDS((bsz, seq, D_MODEL), F32),
        grid=(bsz, seq // ts),
        in_specs=[pl.BlockSpec((1, ts, 2 * LRU_DIM), lambda b, i: (b, i, SEG_C // (2 * LRU_DIM))),
                  pl.BlockSpec((1, ts, D_MODEL), lambda b, i: (b, i, SEG_G // D_MODEL + 2)),
                  pl.BlockSpec((1, ts, D_MODEL), lambda b, i: (b, i, 0)),
                  full((LRU_CONV, LRU_DIM)), full((1, LRU_DIM)),
                  full((LRU_HEADS, LRU_BLOCK, LRU_BLOCK)), full((1, LRU_DIM)),
                  full((LRU_HEADS, LRU_BLOCK, LRU_BLOCK)), full((1, LRU_DIM)),
                  full((1, LRU_DIM)), full((LRU_DIM, D_MODEL))],
        out_specs=pl.BlockSpec((1, ts, D_MODEL), lambda b, i: (b, i, 0)),
        scratch_shapes=[pltpu.VMEM((ts + SUBLANES, LRU_DIM), F32),
                        pltpu.VMEM((SUBLANES, LRU_DIM), F32),
                        pltpu.VMEM((ts, LRU_DIM), F32),
                        pltpu.VMEM((ts, LRU_DIM), F32)],
        compiler_params=_cparams(("arbitrary", "arbitrary")),
        name="rg_lru",
    )(p, p, merged, conv_w, row(conv_b), wa_bf, row(ba), wx_bf, row(bx), row(lam), proj_bf)


def _rwkv_prep_kernel(pb_ref, mu_ref, w0_ref, wup_ref, a0_ref, aup_ref, gup_ref, kk_ref, ka_ref,
                      rk_ref, bd_ref, ltri_ref,
                      rt_ref, kkt_ref, kh_ref, bh_ref, v_ref, pinc_ref, bonus_ref, g_ref, ext_ref):
    ts = pb_ref.shape[1]

    @pl.when(pl.program_id(1) == 0)
    def _():
        ext_ref[pl.ds(0, SUBLANES), :] = jnp.zeros((SUBLANES, ext_ref.shape[1]), F32)

    p = pb_ref[0].astype(F32)
    ext_ref[pl.ds(SUBLANES, ts), :] = p
    prev = ext_ref[pl.ds(SUBLANES - 1, ts), :]
    ext_ref[pl.ds(0, SUBLANES), :] = ext_ref[pl.ds(ts, SUBLANES), :]
    pm = p + (prev - p) * mu_ref[...]
    r = pm[:, RW_R:RW_R + RWKV_DIM]
    k = pm[:, RW_K:RW_K + RWKV_DIM]
    v = pm[:, RW_V:RW_V + RWKV_DIM]
    xwa = pm[:, RW_XWA:RW_XWA + LANES]
    xg = pm[:, RW_XG:RW_XG + LORA_G]
    w = -jax.nn.softplus(-(w0_ref[...] + _bdot(jnp.tanh(xwa), wup_ref[...]))) - 0.5
    lw = -jnp.exp(w)
    a = jax.nn.sigmoid(a0_ref[...] + _bdot(xwa, aup_ref[...]))
    g_ref[0] = _bdot(jax.nn.sigmoid(xg), gup_ref[...])
    kkr = k * kk_ref[...]
    ss = _head_sums(kkr * kkr, bd_ref[...])
    kk = kkr / jnp.maximum(jnp.sqrt(ss), 1e-12)
    k2 = k * (1.0 + (a - 1.0) * ka_ref[...])
    lw_hi = lw.astype(BF16)
    lw_mid, lw_lo = _split(lw - lw_hi.astype(F32))
    tri = ltri_ref[...]
    lcum = (jnp.dot(tri, lw_hi, preferred_element_type=F32) + jnp.dot(tri, lw_mid, preferred_element_type=F32)
            + jnp.dot(tri, lw_lo, preferred_element_type=F32))
    pinc = jnp.exp(lcum)
    pinv = jnp.exp(-lcum)
    rt_ref[0] = r * pinc
    kkt_ref[0] = kk * jnp.exp(lcum - lw)
    kh_ref[0] = k2 * pinv
    bh_ref[0] = kk * a * pinv
    v_ref[0] = v
    pinc_ref[0] = pinc
    bonus_ref[0] = _head_sums(r * k2 * rk_ref[...], bd_ref[...]) * v


def _dot3(a, b):
    d = lambda x, y: jnp.dot(x, y, preferred_element_type=F32)
    m = a[0].shape[0]
    both = d(jnp.concatenate([a[0], a[1]], axis=0), b[0])
    return both[:m] + both[m:] + d(a[0], b[1])


def _rwkv_scan_kernel(rt_ref, kkt_ref, kh_ref, bh_ref, v_ref, pinc_ref, y_ref, s_ref):
    c = RWKV_CHUNK
    n = RWKV_HEAD
    nb = rt_ref.shape[0]
    heads = range(nb * RWKV_HEADS)

    @pl.when(pl.program_id(1) == 0)
    def _():
        s_ref[...] = jnp.zeros(s_ref.shape, F32)

    row = lax.broadcasted_iota(I32, (c, c), 0)
    col = lax.broadcasted_iota(I32, (c, c), 1)
    eye = (row == col).astype(F32)
    same16 = (row // 16) == (col // 16)
    same32 = (row // 32) == (col // 32)
    row2 = lax.broadcasted_iota(I32, (c, 2 * c), 0)
    col2 = lax.broadcasted_iota(I32, (c, 2 * c), 1) % c
    nt = lambda a, b: lax.dot_general(a, b, (((1,), (1,)), ((), ())), preferred_element_type=F32)
    tn = lambda a, b: lax.dot_general(a, b, (((0,), (0,)), ((), ())), preferred_element_type=F32)
    dot = lambda a, b: jnp.dot(a, b, preferred_element_type=F32)
    sl = [pl.ds((h % RWKV_HEADS) * n, n) for h in heads]
    sq = [h // RWKV_HEADS for h in heads]
    v = [v_ref[sq[h], :, sl[h]] for h in heads]
    pc = [pinc_ref[sq[h], pl.ds(c - 1, 1), sl[h]] for h in heads]
    s = [s_ref[sq[h], :, sl[h]] for h in heads]
    lhs = [jnp.concatenate([kkt_ref[sq[h], :, sl[h]], rt_ref[sq[h], :, sl[h]]], axis=0) for h in heads]
    rhs = [jnp.concatenate([bh_ref[sq[h], :, sl[h]], kh_ref[sq[h], :, sl[h]]], axis=0) for h in heads]
    big = [nt(lhs[h], rhs[h]) for h in heads]
    from_state = [nt(lhs[h], s[h]) for h in heads]
    top = [jnp.where(row2 > col2, big[h][:c], 0.0) for h in heads]
    bot = [jnp.where(row2 >= col2, big[h][c:], 0.0) for h in heads]
    a_b = [top[h][:, :c] for h in heads]
    akv = [dot(top[h], jnp.concatenate([jnp.zeros((c, n), F32), v[h]], axis=0)) for h in heads]
    d16 = [jnp.where(same16, a_b[h], 0.0) for h in heads]
    sd = [_split(d16[h]) for h in heads]
    s2 = [_split(_dot3(sd[h], sd[h])) for h in heads]
    s4 = [_split(_dot3(s2[h], s2[h])) for h in heads]
    s8 = [_split(_dot3(s4[h], s4[h])) for h in heads]
    t = [eye - d16[h] for h in heads]
    for sp in (s2, s4, s8):
        t = [t[h] + _dot3(_split(t[h]), sp[h]) for h in heads]
    for off in ([jnp.where(same32 & jnp.logical_not(same16), a_b[h], 0.0) for h in heads],
                [jnp.where(same32, 0.0, a_b[h]) for h in heads]):
        tb = [t[h].astype(BF16) for h in heads]
        lt = [dot(off[h].astype(BF16), tb[h]).astype(BF16) for h in heads]
        t = [t[h] - dot(tb[h], lt[h]) for h in heads]
    u = [dot(t[h], from_state[h][:c] + akv[h]) for h in heads]
    vu = [jnp.concatenate([-u[h], v[h]], axis=0) for h in heads]
    y = [from_state[h][c:] + dot(bot[h], vu[h]) for h in heads]
    s_new = [s[h] * pc[h] + tn(vu[h], rhs[h] * pc[h]) for h in heads]
    for q in range(nb):
        mine = slice(q * RWKV_HEADS, (q + 1) * RWKV_HEADS)
        y_ref[q] = jnp.concatenate(y[mine], axis=1)
        s_ref[q] = jnp.concatenate(s_new[mine], axis=1)


def _rwkv_post_kernel(y_ref, bonus_ref, g_ref, pg_ref, m_ref, gg_ref, gb_ref, bdm_ref, pj_ref, o_ref):
    y = y_ref[0]
    mu = _head_sums(y, bdm_ref[...])
    d = y - mu
    var = _head_sums(d * d, bdm_ref[...])
    yn = d * lax.rsqrt(var + RWKV_GN_EPS) * gg_ref[...] + gb_ref[...] + bonus_ref[0]
    o = _bdot(yn * g_ref[0], pj_ref[...])
    o_ref[0] = m_ref[0] + pg_ref[0].astype(F32) * o


def _rwkv(p, merged, mu_pad, w0, wup_pad, a0, aup_pad, g_up, k_k, k_a, r_k, gn_g, gn_b, proj_bf):
    bsz, seq, _ = p.shape
    row = lambda a: a.reshape(1, -1)
    full = lambda shape: pl.BlockSpec(shape, lambda b, i: (0,) * len(shape))
    head_id = jnp.arange(RWKV_DIM, dtype=I32) // RWKV_HEAD
    bd = (head_id[:, None] == head_id[None, :]).astype(BF16)

    ts = min(seq, 256)
    t_id = jnp.arange(ts, dtype=I32)
    ltri = ((t_id[:, None] // RWKV_CHUNK == t_id[None, :] // RWKV_CHUNK)
            & (t_id[:, None] >= t_id[None, :])).astype(BF16)
    seq_blk = lambda width: pl.BlockSpec((1, ts, width), lambda b, i: (b, i, 0))
    wide = SDS((bsz, seq, RWKV_DIM), F32)
    rt, kkt, kh, bh, v, pinc, bonus, g = pl.pallas_call(
        _rwkv_prep_kernel,
        out_shape=[wide] * 8,
        grid=(bsz, seq // ts),
        in_specs=[pl.BlockSpec((1, ts, 2048), lambda b, i: (b, i, SEG_B // 2048)),
                  full((1, 2048)), full((1, RWKV_DIM)), full((LANES, RWKV_DIM)),
                  full((1, RWKV_DIM)), full((LANES, RWKV_DIM)), full((LORA_G, RWKV_DIM)),
                  full((1, RWKV_DIM)), full((1, RWKV_DIM)), full((1, RWKV_DIM)),
                  full((RWKV_DIM, RWKV_DIM)), full((ts, ts))],
        out_specs=[seq_blk(RWKV_DIM)] * 8,
        scratch_shapes=[pltpu.VMEM((ts + SUBLANES, 2048), F32)],
        compiler_params=_cparams(("arbitrary", "arbitrary")),
        name="rwkv_prep",
    )(p, mu_pad, row(w0), wup_pad, row(a0), aup_pad, g_up, row(k_k), row(k_a), row(r_k), bd, ltri)

    c = RWKV_CHUNK
    nb = 2 if bsz % 2 == 0 else 1
    chunk_blk = pl.BlockSpec((nb, c, RWKV_DIM), lambda b, i: (b, i, 0))
    y = pl.pallas_call(
        _rwkv_scan_kernel,
        out_shape=wide,
        grid=(bsz // nb, seq // c),
        in_specs=[chunk_blk] * 6,
        out_specs=chunk_blk,
        scratch_shapes=[pltpu.VMEM((nb, RWKV_HEAD, RWKV_DIM), F32)],
        compiler_params=_cparams(("arbitrary", "arbitrary")),
        name="rwkv_scan",
    )(rt, kkt, kh, bh, v, pinc)

    tp = min(seq, 512)
    blk = lambda width: pl.BlockSpec((1, tp, width), lambda b, i: (b, i, 0))
    return pl.pallas_call(
        _rwkv_post_kernel,
        out_shape=SDS((bsz, seq, D_MODEL), F32),
        grid=(bsz, seq // tp),
        in_specs=[blk(RWKV_DIM), blk(RWKV_DIM), blk(RWKV_DIM),
                  pl.BlockSpec((1, tp, D_MODEL), lambda b, i: (b, i, SEG_G // D_MODEL + 1)),
                  blk(D_MODEL), full((1, RWKV_DIM)), full((1, RWKV_DIM)),
                  full((RWKV_DIM, RWKV_DIM)), full((RWKV_DIM, D_MODEL))],
        out_specs=blk(D_MODEL),
        compiler_params=_cparams(("arbitrary", "arbitrary")),
        name="rwkv_post",
    )(y, bonus, g, p, merged, row(gn_g), row(gn_b), bd * (1.0 / RWKV_HEAD), proj_bf)


def _out_kernel(m_ref, x_ref, g1_ref, w_ref, n2_ref, sc_ref, sh_ref, xo_ref, h_ref):
    xn = x_ref[0] + g1_ref[0] * _bdot(m_ref[0], w_ref[...])
    xo_ref[0] = xn
    h_ref[0] = _modulated_rmsnorm(xn, n2_ref[...], sc_ref[0], sh_ref[0])


def _out_proj(merged, x, g1, w_bf, norm2, sc2, sh2):
    bsz, seq, _ = x.shape
    ts = min(seq, 512)
    blk = pl.BlockSpec((1, ts, D_MODEL), lambda b, i: (b, i, 0))
    per_b = pl.BlockSpec((1, 1, D_MODEL), lambda b, i: (b, 0, 0))
    return pl.pallas_call(
        _out_kernel,
        out_shape=[SDS((bsz, seq, D_MODEL), F32)] * 2,
        grid=(bsz, seq // ts),
        in_specs=[blk, blk, per_b, pl.BlockSpec((D_MODEL, D_MODEL), lambda b, i: (0, 0)),
                  pl.BlockSpec((1, D_MODEL), lambda b, i: (0, 0)), per_b, per_b],
        out_specs=[blk, blk],
        compiler_params=_cparams(("arbitrary", "arbitrary")),
        name="out_proj",
    )(merged, x, g1, w_bf, norm2, sc2, sh2)


MOE_TILE = 256
MOE_SLOTS = 2560
RUN_LOOP_UNROLL = 4
DISPATCH_PIECE = 64
DISPATCH_SPARE = DISPATCH_PIECE - SUBLANES
DISPATCH_WAIT_ROWS = 2048
assert MOE_SLOTS >= TOP_K * MOE_TILE + N_EXPERTS * (SUBLANES - 1) + DISPATCH_SPARE
U32 = jnp.uint32


def _route_kernel(h_ref, rw_ref, bias_ref, upper_ref, ltri_ref, slot_ref, w_ref, n8_ref):
    tile = h_ref.shape[0]
    neg = -jnp.inf
    scores = jax.nn.sigmoid(_hdot_nt(rw_ref[...], h_ref[...]))
    s3 = scores.reshape(GROUP_SIZE, N_GROUPS, tile)
    b3 = s3 + bias_ref[...].reshape(GROUP_SIZE, N_GROUPS, tile)
    slab = lax.broadcasted_iota(I32, b3.shape, 0).astype(F32)
    grp = lax.broadcasted_iota(I32, b3.shape, 1).astype(F32)
    eid = grp * GROUP_SIZE + slab
    m1 = jnp.max(b3, axis=0, keepdims=True)
    first = jnp.min(jnp.where(b3 == m1, slab, GROUP_SIZE), axis=0, keepdims=True)
    m2 = jnp.max(jnp.where(slab == first, neg, b3), axis=0, keepdims=True)
    gs = (m1 + m2)[0]
    gi = lax.broadcasted_iota(I32, gs.shape, 0).astype(F32)
    chosen = jnp.zeros(gs.shape, F32)
    for _ in range(TOPK_GROUPS):
        m = jnp.max(gs, axis=0, keepdims=True)
        hit = gi == jnp.min(jnp.where(gs == m, gi, N_GROUPS), axis=0, keepdims=True)
        chosen = jnp.where(hit, 1.0, chosen)
        gs = jnp.where(hit, neg, gs)
    cur = jnp.where((chosen > 0.0)[None], b3, neg)
    ones_cols = jnp.ones((tile, LANES), BF16)
    lanes_of = lambda a: jnp.concatenate([a] * (tile // LANES), axis=1)
    to3 = lambda a: a.reshape(GROUP_SIZE, N_GROUPS, tile)
    fold = lambda a: jnp.sum(jnp.sum(a, axis=0), axis=0, keepdims=True)
    base = jnp.zeros((N_EXPERTS, LANES), F32)
    picks, w_rows, rank_rows = [], [], []
    for _ in range(TOP_K):
        m = jnp.max(jnp.max(cur, axis=0), axis=0, keepdims=True)[None]
        pick = jnp.min(jnp.min(jnp.where(cur == m, eid, N_EXPERTS), axis=0), axis=0, keepdims=True)
        hit = eid == pick[None]
        w_rows.append(fold(jnp.where(hit, s3, 0.0)))
        picks.append(pick)
        cur = jnp.where(hit, neg, cur)
        onehot = hit.astype(BF16).reshape(N_EXPERTS, tile)
        before = jnp.dot(onehot, upper_ref[...], preferred_element_type=F32)
        rank_rows.append(fold(jnp.where(hit, to3(before + lanes_of(base)), 0.0)))
        base = base + jnp.dot(onehot, ones_cols, preferred_element_type=F32)
    n8 = jnp.floor((base + (SUBLANES - 1.0)) * (1.0 / SUBLANES)) * SUBLANES
    run_start = to3(lanes_of(_hdot(ltri_ref[...], n8)))
    slots = [rank_rows[j] + fold(jnp.where(eid == picks[j][None], run_start, 0.0)) for j in range(TOP_K)]
    n8_ref[0] = n8
    w_all = jnp.concatenate(w_rows, axis=0)
    w_ref[...] = w_all / jnp.sum(w_all, axis=0, keepdims=True) * ROUTED_SCALE
    slot_ref[...] = jnp.concatenate(slots, axis=0).astype(I32)


def _route(h2, router_w, router_bias):
    n_tok = h2.shape[0]
    tile = MOE_TILE
    n_tiles = n_tok // tile
    regroup = lambda a: a.reshape(N_GROUPS, GROUP_SIZE, -1).transpose(1, 0, 2).reshape(N_EXPERTS, -1)
    rw = regroup(router_w.T)
    bias = jnp.broadcast_to(regroup(router_bias.reshape(N_EXPERTS, 1)), (N_EXPERTS, tile))
    t_id = jnp.arange(tile, dtype=I32)
    upper = (t_id[:, None] < t_id[None, :]).astype(BF16)
    e_id = jnp.arange(N_EXPERTS, dtype=I32)
    ltri = (e_id[:, None] > e_id[None, :]).astype(F32)
    tok_blk = pl.BlockSpec((TOP_K, tile), lambda i: (0, i))
    full = lambda shape: pl.BlockSpec(shape, lambda i: (0,) * len(shape))
    slot_t, w_t, n8 = pl.pallas_call(
        _route_kernel,
        out_shape=[SDS((TOP_K, n_tok), I32), SDS((TOP_K, n_tok), F32),
                   SDS((n_tiles, N_EXPERTS, LANES), F32)],
        grid=(n_tiles,),
        in_specs=[pl.BlockSpec((tile, D_MODEL), lambda i: (i, 0)),
                  full((N_EXPERTS, D_MODEL)), full((N_EXPERTS, tile)), full((tile, tile)),
                  full((N_EXPERTS, N_EXPERTS))],
        out_specs=[tok_blk, tok_blk, pl.BlockSpec((1, N_EXPERTS, LANES), lambda i: (i, 0, 0))],
        compiler_params=_cparams(("arbitrary",)),
        name="moe_route",
    )(h2, rw, bias, upper, ltri)
    return slot_t, w_t, n8[:, :, 0].astype(I32)


def _for_each_run_piece(tile_idx, n8_ref, fn):
    for r in range(N_EXPERTS):
        n = n8_ref[tile_idx * N_EXPERTS + r]
        size = MOE_TILE
        while size >= SUBLANES:
            @pl.when((n & size) != 0)
            def _(size=size):
                fn(r, n & ~(2 * size - 1), size)
            size //= 2


def _wait_rows(total, make_wait):
    size = SUBLANES
    while size <= MOE_SLOTS:
        @pl.when((total & size) != 0)
        def _(size=size):
            make_wait(size).wait()
        size *= 2


def _pack_pairs(hi_bits, lo_bits):
    return (hi_bits & jnp.uint32(0xFFFF0000)) | (lo_bits >> 16)


def _unpack_pairs(u):
    hi = lax.bitcast_convert_type(u & jnp.uint32(0xFFFF0000), F32)
    lo = lax.bitcast_convert_type(u << 16, F32)
    return jnp.concatenate([hi, lo], axis=1).astype(BF16)


def _dispatch_kernel(n8_ref, off_ref, dst_ref, tot_ref, pad_end_ref, slot_ref, h_ref, xs_ref,
                     g_ref, zero_ref, sem):
    i = pl.program_id(0)
    tile = h_ref.shape[0]
    half = D_MODEL // 2

    @pl.when(i == 0)
    def _():
        zero_ref[...] = jnp.zeros(zero_ref.shape, U32)

        def last_block(e, back):
            start = pl.multiple_of(jnp.maximum(pad_end_ref[e] - back * MOE_ROWS, 0), MOE_ROWS)
            return pltpu.make_async_copy(zero_ref, xs_ref.at[pl.ds(start, MOE_ROWS), :], sem.at[0])

        def has_blocks(e, back):
            prev = jnp.where(e > 0, pad_end_ref[jnp.maximum(e - 1, 0)], 0)
            return pad_end_ref[e] - prev >= back * MOE_ROWS

        def clear(e, carry):
            for back in (1, 2):
                pl.when(has_blocks(e, back))(lambda back=back: last_block(e, back).start())
            return carry
        lax.fori_loop(0, N_EXPERTS, clear, 0)

        def done(e, carry):
            for back in (1, 2):
                pl.when(has_blocks(e, back))(lambda back=back: last_block(e, back).wait())
            return carry
        lax.fori_loop(0, N_EXPERTS, done, 0)

        def tail_block(b):
            start = pl.multiple_of(b * MOE_ROWS, MOE_ROWS)
            return pltpu.make_async_copy(zero_ref, xs_ref.at[pl.ds(start, MOE_ROWS), :], sem.at[0])

        def clear_tail(b, carry):
            tail_block(b).start()
            return carry

        def done_tail(b, carry):
            tail_block(b).wait()
            return carry
        used = pad_end_ref[N_EXPERTS - 1] // MOE_ROWS
        lax.fori_loop(used, xs_ref.shape[0] // MOE_ROWS, clear_tail, 0)
        lax.fori_loop(used, xs_ref.shape[0] // MOE_ROWS, done_tail, 0)

    slot_id = lax.broadcasted_iota(I32, (MOE_SLOTS, tile), 0)
    sel = jnp.zeros((MOE_SLOTS, tile), F32)
    for j in range(TOP_K):
        sel = jnp.where(slot_id == slot_ref[pl.ds(j, 1), :], 1.0, sel)
    g = jnp.dot(sel.astype(BF16), h_ref[...].astype(BF16), preferred_element_type=F32)
    bits = lax.bitcast_convert_type(g, U32)
    buf = i % 2
    g_ref[buf] = _pack_pairs(bits[:, :half], bits[:, half:])

    def drain(step):
        total = tot_ref[step]
        whole = pltpu.make_async_copy(g_ref.at[step % 2, pl.ds(0, DISPATCH_WAIT_ROWS), :],
                                      xs_ref.at[pl.ds(0, DISPATCH_WAIT_ROWS), :], sem.at[step % 2])

        def wait_whole(k, carry):
            whole.wait()
            return carry
        lax.fori_loop(0, lax.shift_right_logical(total, DISPATCH_WAIT_ROWS.bit_length() - 1), wait_whole, 0)
        _wait_rows(total & (DISPATCH_WAIT_ROWS - 1), lambda rows: pltpu.make_async_copy(
            g_ref.at[step % 2, pl.ds(0, rows), :], xs_ref.at[pl.ds(0, rows), :], sem.at[step % 2]))

    pl.when(i > 0)(lambda: drain(i - 1))

    def copy(r, k):
        src = pl.multiple_of(off_ref[i * N_EXPERTS + r] + k * DISPATCH_PIECE, SUBLANES)
        dst = pl.multiple_of(dst_ref[i * N_EXPERTS + r] + k * DISPATCH_PIECE, SUBLANES)
        pltpu.make_async_copy(g_ref.at[buf, pl.ds(src, DISPATCH_PIECE), :],
                              xs_ref.at[pl.ds(dst, DISPATCH_PIECE), :], sem.at[buf]).start()

    def per_expert(r, carry):
        n = n8_ref[i * N_EXPERTS + r]
        pl.when(n > 0)(lambda: copy(r, 0))

        @pl.when(n > DISPATCH_PIECE)
        def _():
            def more(k, c):
                copy(r, k)
                return c
            lax.fori_loop(1, lax.shift_right_logical(n + (DISPATCH_PIECE - 1), DISPATCH_PIECE.bit_length() - 1),
                          more, 0)
        return carry
    lax.fori_loop(0, N_EXPERTS, per_expert, 0, unroll=RUN_LOOP_UNROLL)
    pl.when(i == pl.num_programs(0) - 1)(lambda: drain(i))


def _dispatch(h2, slot_t, n8_flat, off_flat, dst_flat, tot, pad_end, n_rows):
    n_tok = h2.shape[0]
    tile = MOE_TILE
    grid_spec = pltpu.PrefetchScalarGridSpec(
        num_scalar_prefetch=5,
        grid=(n_tok // tile,),
        in_specs=[pl.BlockSpec((TOP_K, tile), lambda i, *_: (0, i)),
                  pl.BlockSpec((tile, D_MODEL), lambda i, *_: (i, 0))],
        out_specs=pl.BlockSpec(memory_space=pl.ANY),
        scratch_shapes=[pltpu.VMEM((2, MOE_SLOTS, D_MODEL // 2), U32),
                        pltpu.VMEM((MOE_ROWS, D_MODEL // 2), U32), pltpu.SemaphoreType.DMA((2,))],
    )
    return pl.pallas_call(
        _dispatch_kernel,
        out_shape=SDS((n_rows, D_MODEL // 2), U32),
        grid_spec=grid_spec,
        compiler_params=_cparams(("arbitrary",)),
        name="moe_dispatch",
    )(n8_flat, off_flat, dst_flat, tot, pad_end, slot_t, h2)


def _expert_kernel(first_ref, nblk_ref, exp_of_ref, xs_ref, w1_ref, w3_ref, w2_ref, y_ref,
                   xbuf, ybuf, wb1, wb3, wb2, in_sem, out_sem):
    del exp_of_ref
    r = pl.program_id(0)
    half = D_MODEL // 2
    used = first_ref[N_EXPERTS - 1] + nblk_ref[N_EXPERTS - 1]
    part = MOE_ROWS // EXPERT_DMA_PARTS

    class _Copies:
        def __init__(self, make):
            self.parts = [make(p) for p in range(EXPERT_DMA_PARTS)]

        def start(self):
            for c in self.parts:
                c.start()

        def wait(self):
            for c in self.parts:
                c.wait()

    def hbm_rows(g, p):
        return pl.ds(pl.multiple_of(g * MOE_ROWS + p * part, part), part)

    in_copy = lambda g, slot: _Copies(lambda p: pltpu.make_async_copy(
        xs_ref.at[hbm_rows(g, p), :], xbuf.at[slot, pl.ds(p * part, part), :], in_sem.at[slot]))
    out_copy = lambda g, slot: _Copies(lambda p: pltpu.make_async_copy(
        ybuf.at[slot, pl.ds(p * part, part), :], y_ref.at[hbm_rows(g, p), :], out_sem.at[slot]))

    @pl.when((r == 0) & (used > 0))
    def _():
        in_copy(0, 0).start()

    wb1[...] = w1_ref[0, 0].astype(BF16)
    wb3[...] = w3_ref[0, 0].astype(BF16)
    wb2[...] = w2_ref[0, 0].astype(BF16)

    def block(k, carry):
        g = first_ref[r] + k
        slot = g % 2
        in_copy(g, slot).wait()

        @pl.when(g + 1 < used)
        def _():
            in_copy(g + 1, 1 - slot).start()

        @pl.when(g >= 2)
        def _():
            out_copy(g - 2, slot).wait()

        halves = (pl.ds(0, MOE_ROWS // 2), pl.ds(MOE_ROWS // 2, MOE_ROWS // 2))
        dot = lambda a, b: jnp.dot(a, b, preferred_element_type=F32)
        xs_half = [_unpack_pairs(xbuf[slot, rows, :]) for rows in halves]
        gate = [dot(x, wb1[...]) for x in xs_half]
        up = [dot(x, wb3[...]) for x in xs_half]
        hid = [(jax.nn.silu(a) * b).astype(BF16) for a, b in zip(gate, up)]
        for rows, h in zip(halves, hid):
            y = dot(h, wb2[...]).astype(BF16).astype(F32)
            bits = lax.bitcast_convert_type(y, U32)
            ybuf[slot, rows, :] = _pack_pairs(bits[:, :half], bits[:, half:])
        out_copy(g, slot).start()
        return carry
    lax.fori_loop(0, nblk_ref[r], block, 0)

    @pl.when(r == N_EXPERTS - 1)
    def _():
        @pl.when(used >= 2)
        def _():
            out_copy(used - 2, used % 2).wait()

        @pl.when(used >= 1)
        def _():
            out_copy(used - 1, (used - 1) % 2).wait()

        ybuf[0] = jnp.zeros(ybuf.shape[1:], U32)
        n_blocks = y_ref.shape[0] // MOE_ROWS

        def clear(g, carry):
            out_copy(g, 0).start()
            return carry

        def done(g, carry):
            out_copy(g, 0).wait()
            return carry
        lax.fori_loop(used, n_blocks, clear, 0)
        lax.fori_loop(used, n_blocks, done, 0)


def _experts(xs, first_block, n_block, exp_of_row, w1, w3, w2, layer):
    n_rows = xs.shape[0]
    half = D_MODEL // 2
    w_in_blk = pl.BlockSpec((1, 1, D_MODEL, EXPERT_FF), lambda r, first, nblk, eo: (layer, eo[r], 0, 0))
    grid_spec = pltpu.PrefetchScalarGridSpec(
        num_scalar_prefetch=3,
        grid=(N_EXPERTS,),
        in_specs=[pl.BlockSpec(memory_space=pl.ANY), w_in_blk, w_in_blk,
                  pl.BlockSpec((1, 1, EXPERT_FF, D_MODEL), lambda r, first, nblk, eo: (layer, eo[r], 0, 0))],
        out_specs=pl.BlockSpec(memory_space=pl.ANY),
        scratch_shapes=[pltpu.VMEM((2, MOE_ROWS, half), U32), pltpu.VMEM((2, MOE_ROWS, half), U32),
                        pltpu.VMEM((D_MODEL, EXPERT_FF), BF16), pltpu.VMEM((D_MODEL, EXPERT_FF), BF16),
                        pltpu.VMEM((EXPERT_FF, D_MODEL), BF16),
                        pltpu.SemaphoreType.DMA((2,)), pltpu.SemaphoreType.DMA((2,))],
    )
    return pl.pallas_call(
        _expert_kernel,
        out_shape=SDS((n_rows, half), U32),
        grid_spec=grid_spec,
        compiler_params=_cparams(("arbitrary",)),
        name="moe_experts",
    )(first_block, n_block, exp_of_row, xs, w1, w3, w2)


def _combine_kernel(n8_ref, off_ref, src_ref, tot_ref, y_ref, slot_ref, w_ref, h_ref, x_ref, g2_ref,
                    s1_ref, s3_ref, s2_ref, fn_ref, o_ref, yt_ref, sem, *, final):
    tile = h_ref.shape[1]
    i = pl.program_id(0) * pl.num_programs(1) + pl.program_id(1)

    @pl.when(i == 0)
    def _():
        yt_ref[...] = jnp.zeros(yt_ref.shape, U32)

    def piece(r, offset, rows):
        src = pl.multiple_of(src_ref[i * N_EXPERTS + r] + offset, SUBLANES)
        dst = pl.multiple_of(off_ref[i * N_EXPERTS + r] + offset, SUBLANES)
        pltpu.make_async_copy(y_ref.at[pl.ds(src, rows), :], yt_ref.at[pl.ds(dst, rows), :], sem).start()

    _for_each_run_piece(i, n8_ref, piece)
    h = h_ref[0]
    shared = _bdot(jax.nn.silu(_bdot(h, s1_ref[...])) * _bdot(h, s3_ref[...]), s2_ref[...])
    slot_id = lax.broadcasted_iota(I32, (tile, MOE_SLOTS), 1)
    slots = slot_ref[0]
    w = w_ref[0]
    pw = jnp.zeros((tile, MOE_SLOTS), F32)
    for j in range(TOP_K):
        pw = jnp.where(slot_id == slots[:, j:j + 1], w[:, j:j + 1], pw)
    _wait_rows(tot_ref[i], lambda rows: pltpu.make_async_copy(
        y_ref.at[pl.ds(0, rows), :], yt_ref.at[pl.ds(0, rows), :], sem))
    routed = jnp.dot(pw.astype(BF16), _unpack_pairs(yt_ref[...]), preferred_element_type=F32)
    xn = x_ref[0] + g2_ref[0] * (routed + shared)
    if final:
        xn = xn * lax.rsqrt(jnp.mean(xn * xn, axis=-1, keepdims=True) + NORM_EPS) * fn_ref[...]
    o_ref[0] = xn


def _combine(y, slot_nat, w_nat, n8_flat, off_flat, dst_flat, tot, h2, x, g2, s1_bf, s3_bf, s2_bf,
             final_norm, final):
    bsz, seq, _ = x.shape
    tile = MOE_TILE
    per_seq = seq // tile
    blk = pl.BlockSpec((1, tile, D_MODEL), lambda b, i, *_: (b, i, 0))
    tok = pl.BlockSpec((1, tile, TOP_K), lambda b, i, *_: (b, i, 0))
    full = lambda shape: pl.BlockSpec(shape, lambda b, i, *_: (0,) * len(shape))
    grid_spec = pltpu.PrefetchScalarGridSpec(
        num_scalar_prefetch=4,
        grid=(bsz, per_seq),
        in_specs=[pl.BlockSpec(memory_space=pl.ANY), tok, tok, blk, blk,
                  pl.BlockSpec((1, 1, D_MODEL), lambda b, i, *_: (b, 0, 0)),
                  full((D_MODEL, SHARED_FF)), full((D_MODEL, SHARED_FF)), full((SHARED_FF, D_MODEL)),
                  full((1, D_MODEL))],
        out_specs=blk,
        scratch_shapes=[pltpu.VMEM((MOE_SLOTS, D_MODEL // 2), U32), pltpu.SemaphoreType.DMA],
    )
    return pl.pallas_call(
        functools.partial(_combine_kernel, final=final),
        out_shape=SDS((bsz, seq, D_MODEL), F32),
        grid_spec=grid_spec,
        compiler_params=_cparams(("arbitrary", "arbitrary")),
        name="moe_combine",
    )(n8_flat, off_flat, dst_flat, tot, y, slot_nat, w_nat, h2, x, g2, s1_bf, s3_bf, s2_bf, final_norm)


def _moe(x, h2, g2, router_w, router_bias, w1, w3, w2, layer, s1_bf, s3_bf, s2_bf, final_norm, final):
    bsz, seq, _ = x.shape
    assert seq % MOE_TILE == 0, "token tiles must not straddle sequences"
    n_tok = bsz * seq
    slot_t, w_t, n8 = _route(h2.reshape(n_tok, D_MODEL), router_w, router_bias)
    n_tiles = n8.shape[0]
    counts = jnp.sum(n8, axis=0)
    padded = jnp.where(counts > 0, (counts + DISPATCH_SPARE + MOE_ROWS - 1) // MOE_ROWS * MOE_ROWS, 0)
    pad_end = jnp.cumsum(padded).astype(I32)
    pad_start = pad_end - padded
    run_row = (pad_start[None, :] + jnp.cumsum(n8, axis=0) - n8).astype(I32)
    run_slot = (jnp.cumsum(n8, axis=1) - n8).astype(I32)
    tot = jnp.sum(n8, axis=1).astype(I32)
    copied = jnp.sum((n8 + DISPATCH_PIECE - 1) // DISPATCH_PIECE * DISPATCH_PIECE, axis=1).astype(I32)
    max_rows = (n_tok * TOP_K + n_tiles * N_EXPERTS * (SUBLANES - 1)
                + N_EXPERTS * (DISPATCH_SPARE + MOE_ROWS - 1))
    n_blocks = (max_rows + MOE_ROWS - 1) // MOE_ROWS
    row_id = jnp.arange(N_EXPERTS, dtype=I32)
    exp_of_row = (row_id % N_GROUPS) * GROUP_SIZE + row_id // N_GROUPS
    flat = lambda a: a.reshape(-1).astype(I32)
    xs = _dispatch(h2.reshape(n_tok, D_MODEL), slot_t, flat(n8), flat(run_slot), flat(run_row), copied,
                   pad_end, n_blocks * MOE_ROWS)
    y = _experts(xs, (pad_start // MOE_ROWS).astype(I32), (padded // MOE_ROWS).astype(I32), exp_of_row,
                 w1, w3, w2, layer)
    nat = lambda a: a.T.reshape(bsz, seq, TOP_K)
    return _combine(y, nat(slot_t), nat(w_t), flat(n8), flat(run_slot), flat(run_row), tot, h2, x, g2,
                    s1_bf, s3_bf, s2_bf, final_norm, final)


def _pad_cols(a, width):
    return jnp.pad(a, ((0, 0), (0, width - a.shape[1])))


def _layout_w_in(w_in):
    b0 = COLS_A
    c0 = COLS_A + COLS_B
    g0 = c0 + COLS_C
    seg_b = _pad_cols(w_in[:, b0:c0], SEG_G - SEG_B)
    return jnp.concatenate([w_in[:, c0:g0], seg_b, w_in[:, g0:], w_in[:, :b0]], axis=1).astype(BF16)


def _layout_mu(mu):
    return _pad_cols(mu.reshape(1, -1), SEG_G - SEG_B)


def _pad_rows(a, height):
    return jnp.pad(a, ((0, height - a.shape[0]), (0, 0)))


def kernel(x, c, ada_w, ada_b, norm1, norm2, w_in, conv_a_w, conv_a_b, ln_a_g, ln_a_b, proj_a, mu_b, w0, w_up, a0, a_up, g_up, k_k, k_a, r_k, gn_b_g, gn_b_b, proj_b, conv_c_w, conv_c_b, lru_wa, lru_ba, lru_wx, lru_bx, lru_lambda, proj_c, w_out, router_w, router_bias, exp_w1, exp_w3, exp_w2, sh_w1, sh_w3, sh_w2, final_norm):
    depth = ada_w.shape[0]
    bsz = x.shape[0]
    mod = _ada_mod(c, ada_w, ada_b)
    for l in range(depth):
        sh1, sc1, g1, sh2, sc2, g2 = [mod[l, :, i * D_MODEL:(i + 1) * D_MODEL].reshape(bsz, 1, D_MODEL)
                                      for i in range(N_MOD)]
        p = _in_proj(x, sc1, sh1, norm1[l].reshape(1, -1), _layout_w_in(w_in[l]))
        merged = _conv_a(p, conv_a_w[l], conv_a_b[l], ln_a_g[l], ln_a_b[l], proj_a[l].astype(BF16))
        merged = _rwkv(p, merged, _layout_mu(mu_b[l]), w0[l], _pad_rows(w_up[l], LANES).astype(BF16),
                       a0[l], jnp.pad(a_up[l], ((LORA_W, 0), (0, 0))).astype(BF16), g_up[l].astype(BF16),
                       k_k[l], k_a[l], r_k[l], gn_b_g[l], gn_b_b[l], proj_b[l].astype(BF16))
        merged = _lru(p, merged, conv_c_w[l], conv_c_b[l], lru_wa[l].astype(BF16), lru_ba[l],
                      lru_wx[l].astype(BF16), lru_bx[l], lru_lambda[l], proj_c[l].astype(BF16))
        x, h2 = _out_proj(merged, x, g1, w_out[l].astype(BF16), norm2[l].reshape(1, -1), sc2, sh2)
        x = _moe(x, h2, g2, router_w[l], router_bias[l], exp_w1, exp_w3, exp_w2, l,
                 sh_w1[l].astype(BF16), sh_w3[l].astype(BF16), sh_w2[l].astype(BF16),
                 final_norm.reshape(1, -1), final=(l == depth - 1))
    return x
```

```python
import functools

import jax
import jax.numpy as jnp
from jax import lax
from jax.experimental import pallas as pl
from jax.experimental.pallas import tpu as pltpu

F32 = jnp.float32
BF16 = jnp.bfloat16
I32 = jnp.int32
SDS = jax.ShapeDtypeStruct
HIGHEST = lax.Precision.HIGHEST

D_MODEL = 1024
N_MOD = 6
NORM_EPS = 1e-6
CONV_A_CH = 512
CONV_A_WIDTH = 31
CONV_A_LN_EPS = 1e-5
RWKV_HEADS = 8
RWKV_HEAD = 64
RWKV_DIM = RWKV_HEADS * RWKV_HEAD
LORA_W = 64
LORA_A = 64
LORA_G = 128
RWKV_GN_EPS = 64e-5
RWKV_CHUNK = 64
LRU_DIM = 1024
LRU_HEADS = 8
LRU_BLOCK = LRU_DIM // LRU_HEADS
LRU_CONV = 4
LRU_C = 8.0
N_EXPERTS = 64
TOP_K = 8
N_GROUPS = 8
GROUP_SIZE = N_EXPERTS // N_GROUPS
TOPK_GROUPS = 4
EXPERT_FF = 256
SHARED_FF = 256
ROUTED_SCALE = 2.5
SEG_C = 0
SEG_B = 2048
SEG_G = 4096
SEG_A = 7168
IN_COLS_PAD = 8192
RW_R, RW_K, RW_V, RW_XWA, RW_XG = 0, 512, 1024, 1536, 1664
COLS_A = 2 * CONV_A_CH
COLS_B = 3 * RWKV_DIM + LORA_W + LORA_A + LORA_G
COLS_C = 2 * LRU_DIM
VMEM_LIMIT = 56 * 1024 * 1024
SUBLANES = 8
LANES = 128
MOE_ROWS = 512
EXPERT_DMA_PARTS = 4
EXPERT_BUFS = 3


def _cparams(sem):
    return pltpu.CompilerParams(dimension_semantics=sem, vmem_limit_bytes=VMEM_LIMIT)


def _bdot(a, b):
    return jnp.dot(a.astype(BF16), b.astype(BF16), preferred_element_type=F32)


def _hdot(a, b):
    return jnp.dot(a, b, preferred_element_type=F32, precision=HIGHEST)


def _split(a):
    hi = a.astype(BF16)
    return hi, (a - hi.astype(F32)).astype(BF16)


def _head_sums(a, ones_bf):
    hi, lo = _split(a)
    return (jnp.dot(hi, ones_bf, preferred_element_type=F32)
            + jnp.dot(lo, ones_bf, preferred_element_type=F32))


def _hdot_nt(a, b):
    return lax.dot_general(a, b, (((1,), (1,)), ((), ())), preferred_element_type=F32,
                           precision=HIGHEST)


def _hdot_tn(a, b):
    return lax.dot_general(a, b, (((0,), (0,)), ((), ())), preferred_element_type=F32,
                           precision=HIGHEST)


def _ada_kernel(c_ref, w_ref, b_ref, o_ref):
    cond = jax.nn.silu(c_ref[...])
    o_ref[0] = _bdot(cond, w_ref[0]) + b_ref[0]


def _ada_mod(c, ada_w, ada_b):
    depth, _, n = ada_w.shape
    bsz = c.shape[0]
    tn = 1536
    return pl.pallas_call(
        _ada_kernel,
        out_shape=SDS((depth, bsz, n), F32),
        grid=(depth, n // tn),
        in_specs=[pl.BlockSpec((bsz, D_MODEL), lambda l, j: (0, 0)),
                  pl.BlockSpec((1, D_MODEL, tn), lambda l, j: (l, 0, j)),
                  pl.BlockSpec((1, 1, tn), lambda l, j: (l, 0, j))],
        out_specs=pl.BlockSpec((1, bsz, tn), lambda l, j: (l, 0, j)),
        compiler_params=_cparams(("arbitrary", "arbitrary")),
        name="ada_mod",
    )(c, ada_w, ada_b.reshape(depth, 1, n))


def _modulated_rmsnorm(x, g, sc, sh):
    y = x * lax.rsqrt(jnp.mean(x * x, axis=-1, keepdims=True) + NORM_EPS)
    return (y * g) * (1.0 + sc) + sh


IN_PROJ_SUB = 512


def _in_kernel(x_ref, sc_ref, sh_ref, g_ref, w_ref, o_ref, h_ref):
    @pl.when(pl.program_id(2) == 0)
    def _():
        h_ref[...] = _modulated_rmsnorm(x_ref[0], g_ref[...], sc_ref[0], sh_ref[0]).astype(BF16)

    j = pl.program_id(2)
    ts, tn = o_ref.shape[1], o_ref.shape[2]
    is_gelu = j == SEG_C // tn
    is_gate = (j >= SEG_G // tn) & (j < SEG_A // tn)
    is_glu = j == SEG_A // tn

    def glu(acc):
        u = acc[:, :tn // 2] * jax.nn.sigmoid(acc[:, tn // 2:])
        return jnp.concatenate([u, jnp.zeros_like(u)], axis=1)

    def emit(act, sub):
        for r0 in range(0, ts, sub):
            rows = pl.ds(r0, sub)
            acc = jnp.dot(h_ref[rows, :], w_ref[...], preferred_element_type=F32)
            o_ref[0, rows, :] = act(acc).astype(o_ref.dtype)

    sub = min(ts, IN_PROJ_SUB)
    pl.when(is_gelu)(lambda: emit(functools.partial(jax.nn.gelu, approximate=True), sub))
    pl.when(is_gate)(lambda: emit(jax.nn.sigmoid, sub))
    pl.when(is_glu)(lambda: emit(glu, sub))
    pl.when(jnp.logical_not(is_gelu | is_gate | is_glu))(lambda: emit(lambda acc: acc, ts))


def _in_proj(x, sc, sh, g, w_pad):
    bsz, seq, _ = x.shape
    ts = min(seq, 1024)
    tn = 1024
    return pl.pallas_call(
        _in_kernel,
        out_shape=SDS((bsz, seq, IN_COLS_PAD), BF16),
        grid=(bsz, seq // ts, IN_COLS_PAD // tn),
        in_specs=[pl.BlockSpec((1, ts, D_MODEL), lambda b, i, j: (b, i, 0)),
                  pl.BlockSpec((1, 1, D_MODEL), lambda b, i, j: (b, 0, 0)),
                  pl.BlockSpec((1, 1, D_MODEL), lambda b, i, j: (b, 0, 0)),
                  pl.BlockSpec((1, D_MODEL), lambda b, i, j: (0, 0)),
                  pl.BlockSpec((D_MODEL, tn), lambda b, i, j: (0, j))],
        out_specs=pl.BlockSpec((1, ts, tn), lambda b, i, j: (b, i, j)),
        scratch_shapes=[pltpu.VMEM((ts, D_MODEL), BF16)],
        compiler_params=_cparams(("arbitrary", "arbitrary", "arbitrary")),
        name="in_proj",
    )(x, sc, sh, g, w_pad)


CONV_A_HALO = 32
CONV_A_SUB = 64


def _conv_a_kernel(pa_ref, pg_ref, cw_ref, cb_ref, lg_ref, lb_ref, pj_ref, o_ref, ext_ref, sh_ref, y_ref):
    ts = pa_ref.shape[1]

    @pl.when(pl.program_id(1) == 0)
    def _():
        ext_ref[pl.ds(0, CONV_A_HALO), :] = jnp.zeros((CONV_A_HALO, CONV_A_CH), F32)

    ext_ref[pl.ds(CONV_A_HALO, ts), :] = pa_ref[0].astype(F32)
    for p in range(1, SUBLANES):
        sh_ref[p - 1] = ext_ref[pl.ds(p, sh_ref.shape[1]), :]
    first = CONV_A_HALO - (CONV_A_WIDTH - 1)
    for r0 in range(0, ts, CONV_A_SUB):
        acc = jnp.zeros((CONV_A_SUB, CONV_A_CH), F32) + cb_ref[...]
        for j in range(CONV_A_WIDTH):
            phase = (first + j) % SUBLANES
            rows = pl.ds(r0 + first + j - phase, CONV_A_SUB)
            tap = ext_ref[rows, :] if phase == 0 else sh_ref[phase - 1, rows, :]
            acc = acc + tap * cw_ref[pl.ds(j, 1), :]
        y_ref[pl.ds(r0, CONV_A_SUB), :] = acc
    ext_ref[pl.ds(0, CONV_A_HALO), :] = ext_ref[pl.ds(ts, CONV_A_HALO), :]
    y = y_ref[...]
    mu = jnp.mean(y, axis=-1, keepdims=True)
    d = y - mu
    var = jnp.mean(d * d, axis=-1, keepdims=True)
    yn = d * lax.rsqrt(var + CONV_A_LN_EPS) * lg_ref[...] + lb_ref[...]
    o = _bdot(jax.nn.silu(yn), pj_ref[...])
    o_ref[0] = pg_ref[0].astype(F32) * o


def _conv_a(p, conv_w, conv_b, ln_g, ln_b, proj_bf):
    bsz, seq, _ = p.shape
    ts = min(seq, 512)
    row = lambda a: a.reshape(1, -1)
    full = lambda shape: pl.BlockSpec(shape, lambda b, i: (0,) * len(shape))
    return pl.pallas_call(
        _conv_a_kernel,
        out_shape=SDS((bsz, seq, D_MODEL), F32),
        grid=(bsz, seq // ts),
        in_specs=[pl.BlockSpec((1, ts, CONV_A_CH), lambda b, i: (b, i, SEG_A // CONV_A_CH)),
                  pl.BlockSpec((1, ts, D_MODEL), lambda b, i: (b, i, SEG_G // D_MODEL)),
                  full((CONV_A_WIDTH, CONV_A_CH)), full((1, CONV_A_CH)), full((1, CONV_A_CH)),
                  full((1, CONV_A_CH)), full((CONV_A_CH, D_MODEL))],
        out_specs=pl.BlockSpec((1, ts, D_MODEL), lambda b, i: (b, i, 0)),
        scratch_shapes=[pltpu.VMEM((ts + CONV_A_HALO, CONV_A_CH), F32),
                        pltpu.VMEM((SUBLANES - 1, ts + CONV_A_HALO - SUBLANES, CONV_A_CH), F32),
                        pltpu.VMEM((ts, CONV_A_CH), F32)],
        compiler_params=_cparams(("arbitrary", "arbitrary")),
        name="conv_a",
    )(p, p, conv_w, row(conv_b), row(ln_g), row(ln_b), proj_bf)


def _lru_kernel(pc_ref, pg_ref, m_ref, cw_ref, cb_ref, wa_ref, ba_ref, wx_ref, bx_ref, lam_ref,
                pj_ref, o_ref, ext_ref, h_ref, a_ref, b_ref):
    ts = pc_ref.shape[1]
    groups = ts // SUBLANES

    @pl.when(pl.program_id(1) == 0)
    def _():
        ext_ref[pl.ds(0, SUBLANES), :] = jnp.zeros((SUBLANES, LRU_DIM), F32)
        h_ref[...] = jnp.zeros((SUBLANES, LRU_DIM), F32)

    pc = pc_ref[0].astype(F32)
    y_gate = pc[:, :LRU_DIM]
    ext_ref[pl.ds(SUBLANES, ts), :] = pc[:, LRU_DIM:]
    first = SUBLANES - (LRU_CONV - 1)
    xc = jnp.zeros((ts, LRU_DIM), F32) + cb_ref[...]
    for j in range(LRU_CONV):
        xc = xc + ext_ref[pl.ds(first + j, ts), :] * cw_ref[pl.ds(j, 1), :]
    ext_ref[pl.ds(0, SUBLANES), :] = ext_ref[pl.ds(ts, SUBLANES), :]

    def block_diag(w_ref):
        return jnp.concatenate(
            [_bdot(xc[:, h * LRU_BLOCK:(h + 1) * LRU_BLOCK], w_ref[h]) for h in range(LRU_HEADS)],
            axis=1)

    gate_a = jax.nn.sigmoid(block_diag(wa_ref) + ba_ref[...])
    gate_x = jax.nn.sigmoid(block_diag(wx_ref) + bx_ref[...])
    log_a = -LRU_C * gate_a * jax.nn.softplus(-lam_ref[...])
    a = jnp.exp(log_a)
    b = xc * gate_x * jnp.sqrt(1.0 - jnp.exp(2.0 * log_a))

    a3 = a.reshape(groups, SUBLANES, LRU_DIM)
    b3 = b.reshape(groups, SUBLANES, LRU_DIM)
    row = lax.broadcasted_iota(I32, (groups, SUBLANES, LRU_DIM), 1)
    for s in (1, 2, 4):
        keep = row >= s
        b3 = jnp.where(keep, a3 * pltpu.roll(b3, s, axis=1) + b3, b3)
        a3 = jnp.where(keep, a3 * pltpu.roll(a3, s, axis=1), a3)
    a_ref[...] = a3.reshape(ts, LRU_DIM)
    b_ref[...] = b3.reshape(ts, LRU_DIM)
    h = h_ref[...]
    for g in range(groups):
        rows = pl.ds(g * SUBLANES, SUBLANES)
        hg = a_ref[rows, :] * h + b_ref[rows, :]
        b_ref[rows, :] = hg
        h = jnp.broadcast_to(hg[SUBLANES - 1:SUBLANES, :], (SUBLANES, LRU_DIM))
    h_ref[...] = h
    o = _bdot(b_ref[...] * y_gate, pj_ref[...])
    o_ref[0] = m_ref[0] + pg_ref[0].astype(F32) * o


def _lru(p, merged, conv_w, conv_b, wa_bf, ba, wx_bf, bx, lam, proj_bf):
    bsz, seq, _ = p.shape
    ts = min(seq, 256)
    row = lambda a: a.reshape(1, -1)
    full = lambda shape: pl.BlockSpec(shape, lambda b, i: (0,) * len(shape))
    return pl.pallas_call(
        _lru_kernel,
        out_shape=SDS((bsz, seq, D_MODEL), F32),
        grid=(bsz, seq // ts),
        in_specs=[pl.BlockSpec((1, ts, 2 * LRU_DIM), lambda b, i: (b, i, SEG_C // (2 * LRU_DIM))),
                  pl.BlockSpec((1, ts, D_MODEL), lambda b, i: (b, i, SEG_G // D_MODEL + 2)),
                  pl.BlockSpec((1, ts, D_MODEL), lambda b, i: (b, i, 0)),
                  full((LRU_CONV, LRU_DIM)), full((1, LRU_DIM)),
                  full((LRU_HEADS, LRU_BLOCK, LRU_BLOCK)), full((1, LRU_DIM)),
                  full((LRU_HEADS, LRU_BLOCK, LRU_BLOCK)), full((1, LRU_DIM)),
                  full((1, LRU_DIM)), full((LRU_DIM, D_MODEL))],
        out_specs=pl.BlockSpec((1, ts, D_MODEL), lambda b, i: (b, i, 0)),
        scratch_shapes=[pltpu.VMEM((ts + SUBLANES, LRU_DIM), F32),
                        pltpu.VMEM((SUBLANES, LRU_DIM), F32),
                        pltpu.VMEM((ts, LRU_DIM), F32),
                        pltpu.VMEM((ts, LRU_DIM), F32)],
        compiler_params=_cparams(("arbitrary", "arbitrary")),
        name="rg_lru",
    )(p, p, merged, conv_w, row(conv_b), wa_bf, row(ba), wx_bf, row(bx), row(lam), proj_bf)


def _rwkv_prep_kernel(pb_ref, mu_ref, w0_ref, wup_ref, a0_ref, aup_ref, gup_ref, kk_ref, ka_ref,
                      rk_ref, bd_ref, ltri_ref,
                      rt_ref, kkt_ref, kh_ref, bh_ref, v_ref, pinc_ref, bonus_ref, g_ref, ext_ref):
    ts = pb_ref.shape[1]

    @pl.when(pl.program_id(1) == 0)
    def _():
        ext_ref[pl.ds(0, SUBLANES), :] = jnp.zeros((SUBLANES, ext_ref.shape[1]), F32)

    p = pb_ref[0].astype(F32)
    ext_ref[pl.ds(SUBLANES, ts), :] = p
    prev = ext_ref[pl.ds(SUBLANES - 1, ts), :]
    ext_ref[pl.ds(0, SUBLANES), :] = ext_ref[pl.ds(ts, SUBLANES), :]
    pm = p + (prev - p) * mu_ref[...]
    r = pm[:, RW_R:RW_R + RWKV_DIM]
    k = pm[:, RW_K:RW_K + RWKV_DIM]
    v = pm[:, RW_V:RW_V + RWKV_DIM]
    xwa = pm[:, RW_XWA:RW_XWA + LANES]
    xg = pm[:, RW_XG:RW_XG + LORA_G]
    w = -jax.nn.softplus(-(w0_ref[...] + _bdot(jnp.tanh(xwa), wup_ref[...]))) - 0.5
    lw = -jnp.exp(w)
    a = jax.nn.sigmoid(a0_ref[...] + _bdot(xwa, aup_ref[...]))
    g_ref[0] = _bdot(jax.nn.sigmoid(xg), gup_ref[...])
    kkr = k * kk_ref[...]
    ss = _head_sums(kkr * kkr, bd_ref[...])
    kk = kkr / jnp.maximum(jnp.sqrt(ss), 1e-12)
    k2 = k * (1.0 + (a - 1.0) * ka_ref[...])
    lw_hi = lw.astype(BF16)
    lw_mid, lw_lo = _split(lw - lw_hi.astype(F32))
    tri = ltri_ref[...]
    lcum = (jnp.dot(tri, lw_hi, preferred_element_type=F32) + jnp.dot(tri, lw_mid, preferred_element_type=F32)
            + jnp.dot(tri, lw_lo, preferred_element_type=F32))
    pinc = jnp.exp(lcum)
    pinv = jnp.exp(-lcum)
    rt_ref[0] = r * pinc
    kkt_ref[0] = kk * jnp.exp(lcum - lw)
    kh_ref[0] = k2 * pinv
    bh_ref[0] = kk * a * pinv
    v_ref[0] = v
    pinc_ref[0] = pinc
    bonus_ref[0] = _head_sums(r * k2 * rk_ref[...], bd_ref[...]) * v


def _dot3(a, b):
    d = lambda x, y: jnp.dot(x, y, preferred_element_type=F32)
    m = a[0].shape[0]
    both = d(jnp.concatenate([a[0], a[1]], axis=0), b[0])
    return both[:m] + both[m:] + d(a[0], b[1])


def _rwkv_scan_kernel(rt_ref, kkt_ref, kh_ref, bh_ref, v_ref, pinc_ref, y_ref, s_ref):
    c = RWKV_CHUNK
    n = RWKV_HEAD
    nb = rt_ref.shape[0]
    heads = range(nb * RWKV_HEADS)

    @pl.when(pl.program_id(1) == 0)
    def _():
        s_ref[...] = jnp.zeros(s_ref.shape, F32)

    row = lax.broadcasted_iota(I32, (c, c), 0)
    col = lax.broadcasted_iota(I32, (c, c), 1)
    eye = (row == col).astype(F32)
    same16 = (row // 16) == (col // 16)
    same32 = (row // 32) == (col // 32)
    row2 = lax.broadcasted_iota(I32, (c, 2 * c), 0)
    col2 = lax.broadcasted_iota(I32, (c, 2 * c), 1) % c
    nt = lambda a, b: lax.dot_general(a, b, (((1,), (1,)), ((), ())), preferred_element_type=F32)
    tn = lambda a, b: lax.dot_general(a, b, (((0,), (0,)), ((), ())), preferred_element_type=F32)
    dot = lambda a, b: jnp.dot(a, b, preferred_element_type=F32)
    sl = [pl.ds((h % RWKV_HEADS) * n, n) for h in heads]
    sq = [h // RWKV_HEADS for h in heads]
    v = [v_ref[sq[h], :, sl[h]] for h in heads]
    pc = [pinc_ref[sq[h], pl.ds(c - 1, 1), sl[h]] for h in heads]
    s = [s_ref[sq[h], :, sl[h]] for h in heads]
    lhs = [jnp.concatenate([kkt_ref[sq[h], :, sl[h]], rt_ref[sq[h], :, sl[h]]], axis=0) for h in heads]
    rhs = [jnp.concatenate([bh_ref[sq[h], :, sl[h]], kh_ref[sq[h], :, sl[h]]], axis=0) for h in heads]
    big = [nt(lhs[h], rhs[h]) for h in heads]
    from_state = [nt(lhs[h], s[h]) for h in heads]
    top = [jnp.where(row2 > col2, big[h][:c], 0.0) for h in heads]
    bot = [jnp.where(row2 >= col2, big[h][c:], 0.0) for h in heads]
    a_b = [top[h][:, :c] for h in heads]
    akv = [dot(top[h], jnp.concatenate([jnp.zeros((c, n), F32), v[h]], axis=0)) for h in heads]
    d16 = [jnp.where(same16, a_b[h], 0.0) for h in heads]
    sd = [_split(d16[h]) for h in heads]
    s2 = [_split(_dot3(sd[h], sd[h])) for h in heads]
    s4 = [_split(_dot3(s2[h], s2[h])) for h in heads]
    s8 = [_split(_dot3(s4[h], s4[h])) for h in heads]
    t = [eye - d16[h] for h in heads]
    for sp in (s2, s4, s8):
        t = [t[h] + _dot3(_split(t[h]), sp[h]) for h in heads]
    for off in ([jnp.where(same32 & jnp.logical_not(same16), a_b[h], 0.0) for h in heads],
                [jnp.where(same32, 0.0, a_b[h]) for h in heads]):
        tb = [t[h].astype(BF16) for h in heads]
        lt = [dot(off[h].astype(BF16), tb[h]).astype(BF16) for h in heads]
        t = [t[h] - dot(tb[h], lt[h]) for h in heads]
    u = [dot(t[h], from_state[h][:c] + akv[h]) for h in heads]
    vu = [jnp.concatenate([-u[h], v[h]], axis=0) for h in heads]
    y = [from_state[h][c:] + dot(bot[h], vu[h]) for h in heads]
    s_new = [s[h] * pc[h] + tn(vu[h], rhs[h] * pc[h]) for h in heads]
    for q in range(nb):
        mine = slice(q * RWKV_HEADS, (q + 1) * RWKV_HEADS)
        y_ref[q] = jnp.concatenate(y[mine], axis=1)
        s_ref[q] = jnp.concatenate(s_new[mine], axis=1)


def _rwkv_post_kernel(y_ref, bonus_ref, g_ref, pg_ref, m_ref, gg_ref, gb_ref, bdm_ref, pj_ref, o_ref):
    y = y_ref[0]
    mu = _head_sums(y, bdm_ref[...])
    d = y - mu
    var = _head_sums(d * d, bdm_ref[...])
    yn = d * lax.rsqrt(var + RWKV_GN_EPS) * gg_ref[...] + gb_ref[...] + bonus_ref[0]
    o = _bdot(yn * g_ref[0], pj_ref[...])
    o_ref[0] = m_ref[0] + pg_ref[0].astype(F32) * o


def _rwkv(p, merged, mu_pad, w0, wup_pad, a0, aup_pad, g_up, k_k, k_a, r_k, gn_g, gn_b, proj_bf):
    bsz, seq, _ = p.shape
    row = lambda a: a.reshape(1, -1)
    full = lambda shape: pl.BlockSpec(shape, lambda b, i: (0,) * len(shape))
    head_id = jnp.arange(RWKV_DIM, dtype=I32) // RWKV_HEAD
    bd = (head_id[:, None] == head_id[None, :]).astype(BF16)

    ts = min(seq, 256)
    t_id = jnp.arange(ts, dtype=I32)
    ltri = ((t_id[:, None] // RWKV_CHUNK == t_id[None, :] // RWKV_CHUNK)
            & (t_id[:, None] >= t_id[None, :])).astype(BF16)
    seq_blk = lambda width: pl.BlockSpec((1, ts, width), lambda b, i: (b, i, 0))
    wide = SDS((bsz, seq, RWKV_DIM), F32)
    rt, kkt, kh, bh, v, pinc, bonus, g = pl.pallas_call(
        _rwkv_prep_kernel,
        out_shape=[wide] * 8,
        grid=(bsz, seq // ts),
        in_specs=[pl.BlockSpec((1, ts, 2048), lambda b, i: (b, i, SEG_B // 2048)),
                  full((1, 2048)), full((1, RWKV_DIM)), full((LANES, RWKV_DIM)),
                  full((1, RWKV_DIM)), full((LANES, RWKV_DIM)), full((LORA_G, RWKV_DIM)),
                  full((1, RWKV_DIM)), full((1, RWKV_DIM)), full((1, RWKV_DIM)),
                  full((RWKV_DIM, RWKV_DIM)), full((ts, ts))],
        out_specs=[seq_blk(RWKV_DIM)] * 8,
        scratch_shapes=[pltpu.VMEM((ts + SUBLANES, 2048), F32)],
        compiler_params=_cparams(("arbitrary", "arbitrary")),
        name="rwkv_prep",
    )(p, mu_pad, row(w0), wup_pad, row(a0), aup_pad, g_up, row(k_k), row(k_a), row(r_k), bd, ltri)

    c = RWKV_CHUNK
    nb = 2 if bsz % 2 == 0 else 1
    chunk_blk = pl.BlockSpec((nb, c, RWKV_DIM), lambda b, i: (b, i, 0))
    y = pl.pallas_call(
        _rwkv_scan_kernel,
        out_shape=wide,
        grid=(bsz // nb, seq // c),
        in_specs=[chunk_blk] * 6,
        out_specs=chunk_blk,
        scratch_shapes=[pltpu.VMEM((nb, RWKV_HEAD, RWKV_DIM), F32)],
        compiler_params=_cparams(("arbitrary", "arbitrary")),
        name="rwkv_scan",
    )(rt, kkt, kh, bh, v, pinc)

    tp = min(seq, 512)
    blk = lambda width: pl.BlockSpec((1, tp, width), lambda b, i: (b, i, 0))
    return pl.pallas_call(
        _rwkv_post_kernel,
        out_shape=SDS((bsz, seq, D_MODEL), F32),
        grid=(bsz, seq // tp),
        in_specs=[blk(RWKV_DIM), blk(RWKV_DIM), blk(RWKV_DIM),
                  pl.BlockSpec((1, tp, D_MODEL), lambda b, i: (b, i, SEG_G // D_MODEL + 1)),
                  blk(D_MODEL), full((1, RWKV_DIM)), full((1, RWKV_DIM)),
                  full((RWKV_DIM, RWKV_DIM)), full((RWKV_DIM, D_MODEL))],
        out_specs=blk(D_MODEL),
        compiler_params=_cparams(("arbitrary", "arbitrary")),
        name="rwkv_post",
    )(y, bonus, g, p, merged, row(gn_g), row(gn_b), bd * (1.0 / RWKV_HEAD), proj_bf)


def _out_kernel(m_ref, x_ref, g1_ref, w_ref, n2_ref, sc_ref, sh_ref, xo_ref, h_ref):
    xn = x_ref[0] + g1_ref[0] * _bdot(m_ref[0], w_ref[...])
    xo_ref[0] = xn
    h_ref[0] = _modulated_rmsnorm(xn, n2_ref[...], sc_ref[0], sh_ref[0])


def _out_proj(merged, x, g1, w_bf, norm2, sc2, sh2):
    bsz, seq, _ = x.shape
    ts = min(seq, 512)
    blk = pl.BlockSpec((1, ts, D_MODEL), lambda b, i: (b, i, 0))
    per_b = pl.BlockSpec((1, 1, D_MODEL), lambda b, i: (b, 0, 0))
    return pl.pallas_call(
        _out_kernel,
        out_shape=[SDS((bsz, seq, D_MODEL), F32)] * 2,
        grid=(bsz, seq // ts),
        in_specs=[blk, blk, per_b, pl.BlockSpec((D_MODEL, D_MODEL), lambda b, i: (0, 0)),
                  pl.BlockSpec((1, D_MODEL), lambda b, i: (0, 0)), per_b, per_b],
        out_specs=[blk, blk],
        compiler_params=_cparams(("arbitrary", "arbitrary")),
        name="out_proj",
    )(merged, x, g1, w_bf, norm2, sc2, sh2)


MOE_TILE = 256
MOE_SLOTS = 2560
RUN_LOOP_UNROLL = 4
DISPATCH_PIECE = 64
DISPATCH_SPARE = DISPATCH_PIECE - SUBLANES
DISPATCH_WAIT_ROWS = 2048
assert MOE_SLOTS >= TOP_K * MOE_TILE + N_EXPERTS * (SUBLANES - 1) + DISPATCH_SPARE
U32 = jnp.uint32


def _route_kernel(h_ref, rw_ref, bias_ref, upper_ref, ltri_ref, slot_ref, w_ref, n8_ref):
    tile = h_ref.shape[0]
    neg = -jnp.inf
    scores = jax.nn.sigmoid(_hdot_nt(rw_ref[...], h_ref[...]))
    s3 = scores.reshape(GROUP_SIZE, N_GROUPS, tile)
    b3 = s3 + bias_ref[...].reshape(GROUP_SIZE, N_GROUPS, tile)
    slab = lax.broadcasted_iota(I32, b3.shape, 0).astype(F32)
    grp = lax.broadcasted_iota(I32, b3.shape, 1).astype(F32)
    eid = grp * GROUP_SIZE + slab
    m1 = jnp.max(b3, axis=0, keepdims=True)
    first = jnp.min(jnp.where(b3 == m1, slab, GROUP_SIZE), axis=0, keepdims=True)
    m2 = jnp.max(jnp.where(slab == first, neg, b3), axis=0, keepdims=True)
    gs = (m1 + m2)[0]
    gi = lax.broadcasted_iota(I32, gs.shape, 0).astype(F32)
    chosen = jnp.zeros(gs.shape, F32)
    for _ in range(TOPK_GROUPS):
        m = jnp.max(gs, axis=0, keepdims=True)
        hit = gi == jnp.min(jnp.where(gs == m, gi, N_GROUPS), axis=0, keepdims=True)
        chosen = jnp.where(hit, 1.0, chosen)
        gs = jnp.where(hit, neg, gs)
    cur = jnp.where((chosen > 0.0)[None], b3, neg)
    ones_cols = jnp.ones((tile, LANES), BF16)
    lanes_of = lambda a: jnp.concatenate([a] * (tile // LANES), axis=1)
    to3 = lambda a: a.reshape(GROUP_SIZE, N_GROUPS, tile)
    fold = lambda a: jnp.sum(jnp.sum(a, axis=0), axis=0, keepdims=True)
    base = jnp.zeros((N_EXPERTS, LANES), F32)
    picks, w_rows, rank_rows = [], [], []
    for _ in range(TOP_K):
        m = jnp.max(jnp.max(cur, axis=0), axis=0, keepdims=True)[None]
        pick = jnp.min(jnp.min(jnp.where(cur == m, eid, N_EXPERTS), axis=0), axis=0, keepdims=True)
        hit = eid == pick[None]
        w_rows.append(fold(jnp.where(hit, s3, 0.0)))
        picks.append(pick)
        cur = jnp.where(hit, neg, cur)
        onehot = hit.astype(BF16).reshape(N_EXPERTS, tile)
        before = jnp.dot(onehot, upper_ref[...], preferred_element_type=F32)
        rank_rows.append(fold(jnp.where(hit, to3(before + lanes_of(base)), 0.0)))
        base = base + jnp.dot(onehot, ones_cols, preferred_element_type=F32)
    n8 = jnp.floor((base + (SUBLANES - 1.0)) * (1.0 / SUBLANES)) * SUBLANES
    run_start = to3(lanes_of(_hdot(ltri_ref[...], n8)))
    slots = [rank_rows[j] + fold(jnp.where(eid == picks[j][None], run_start, 0.0)) for j in range(TOP_K)]
    n8_ref[0] = n8
    w_all = jnp.concatenate(w_rows, axis=0)
    w_ref[...] = w_all / jnp.sum(w_all, axis=0, keepdims=True) * ROUTED_SCALE
    slot_ref[...] = jnp.concatenate(slots, axis=0).astype(I32)


def _route(h2, router_w, router_bias):
    n_tok = h2.shape[0]
    tile = MOE_TILE
    n_tiles = n_tok // tile
    regroup = lambda a: a.reshape(N_GROUPS, GROUP_SIZE, -1).transpose(1, 0, 2).reshape(N_EXPERTS, -1)
    rw = regroup(router_w.T)
    bias = jnp.broadcast_to(regroup(router_bias.reshape(N_EXPERTS, 1)), (N_EXPERTS, tile))
    t_id = jnp.arange(tile, dtype=I32)
    upper = (t_id[:, None] < t_id[None, :]).astype(BF16)
    e_id = jnp.arange(N_EXPERTS, dtype=I32)
    ltri = (e_id[:, None] > e_id[None, :]).astype(F32)
    tok_blk = pl.BlockSpec((TOP_K, tile), lambda i: (0, i))
    full = lambda shape: pl.BlockSpec(shape, lambda i: (0,) * len(shape))
    slot_t, w_t, n8 = pl.pallas_call(
        _route_kernel,
        out_shape=[SDS((TOP_K, n_tok), I32), SDS((TOP_K, n_tok), F32),
                   SDS((n_tiles, N_EXPERTS, LANES), F32)],
        grid=(n_tiles,),
        in_specs=[pl.BlockSpec((tile, D_MODEL), lambda i: (i, 0)),
                  full((N_EXPERTS, D_MODEL)), full((N_EXPERTS, tile)), full((tile, tile)),
                  full((N_EXPERTS, N_EXPERTS))],
        out_specs=[tok_blk, tok_blk, pl.BlockSpec((1, N_EXPERTS, LANES), lambda i: (i, 0, 0))],
        compiler_params=_cparams(("arbitrary",)),
        name="moe_route",
    )(h2, rw, bias, upper, ltri)
    return slot_t, w_t, n8[:, :, 0].astype(I32)


def _for_each_run_piece(tile_idx, n8_ref, fn):
    for r in range(N_EXPERTS):
        n = n8_ref[tile_idx * N_EXPERTS + r]
        size = MOE_TILE
        while size >= SUBLANES:
            @pl.when((n & size) != 0)
            def _(size=size):
                fn(r, n & ~(2 * size - 1), size)
            size //= 2


def _wait_rows(total, make_wait):
    size = SUBLANES
    while size <= MOE_SLOTS:
        @pl.when((total & size) != 0)
        def _(size=size):
            make_wait(size).wait()
        size *= 2


def _pack_pairs(hi_bits, lo_bits):
    return (hi_bits & jnp.uint32(0xFFFF0000)) | (lo_bits >> 16)


def _unpack_pairs(u):
    hi = lax.bitcast_convert_type(u & jnp.uint32(0xFFFF0000), F32)
    lo = lax.bitcast_convert_type(u << 16, F32)
    return jnp.concatenate([hi, lo], axis=1).astype(BF16)


def _dispatch_kernel(n8_ref, off_ref, dst_ref, tot_ref, pad_end_ref, slot_ref, h_ref, xs_ref,
                     g_ref, zero_ref, sem):
    i = pl.program_id(0)
    tile = h_ref.shape[0]
    half = D_MODEL // 2

    @pl.when(i == 0)
    def _():
        zero_ref[...] = jnp.zeros(zero_ref.shape, U32)

        def last_block(e, back):
            start = pl.multiple_of(jnp.maximum(pad_end_ref[e] - back * MOE_ROWS, 0), MOE_ROWS)
            return pltpu.make_async_copy(zero_ref, xs_ref.at[pl.ds(start, MOE_ROWS), :], sem.at[0])

        def has_blocks(e, back):
            prev = jnp.where(e > 0, pad_end_ref[jnp.maximum(e - 1, 0)], 0)
            return pad_end_ref[e] - prev >= back * MOE_ROWS

        def clear(e, carry):
            for back in (1, 2):
                pl.when(has_blocks(e, back))(lambda back=back: last_block(e, back).start())
            return carry
        lax.fori_loop(0, N_EXPERTS, clear, 0)

        def done(e, carry):
            for back in (1, 2):
                pl.when(has_blocks(e, back))(lambda back=back: last_block(e, back).wait())
            return carry
        lax.fori_loop(0, N_EXPERTS, done, 0)

        def tail_block(b):
            start = pl.multiple_of(b * MOE_ROWS, MOE_ROWS)
            return pltpu.make_async_copy(zero_ref, xs_ref.at[pl.ds(start, MOE_ROWS), :], sem.at[0])

        def clear_tail(b, carry):
            tail_block(b).start()
            return carry

        def done_tail(b, carry):
            tail_block(b).wait()
            return carry
        used = pad_end_ref[N_EXPERTS - 1] // MOE_ROWS
        lax.fori_loop(used, xs_ref.shape[0] // MOE_ROWS, clear_tail, 0)
        lax.fori_loop(used, xs_ref.shape[0] // MOE_ROWS, done_tail, 0)

    slot_id = lax.broadcasted_iota(I32, (MOE_SLOTS, tile), 0)
    sel = jnp.zeros((MOE_SLOTS, tile), F32)
    for j in range(TOP_K):
        sel = jnp.where(slot_id == slot_ref[pl.ds(j, 1), :], 1.0, sel)
    g = jnp.dot(sel.astype(BF16), h_ref[...].astype(BF16), preferred_element_type=F32)
    bits = lax.bitcast_convert_type(g, U32)
    buf = i % 2
    g_ref[buf] = _pack_pairs(bits[:, :half], bits[:, half:])

    def drain(step):
        total = tot_ref[step]
        whole = pltpu.make_async_copy(g_ref.at[step % 2, pl.ds(0, DISPATCH_WAIT_ROWS), :],
                                      xs_ref.at[pl.ds(0, DISPATCH_WAIT_ROWS), :], sem.at[step % 2])

        def wait_whole(k, carry):
            whole.wait()
            return carry
        lax.fori_loop(0, lax.shift_right_logical(total, DISPATCH_WAIT_ROWS.bit_length() - 1), wait_whole, 0)
        _wait_rows(total & (DISPATCH_WAIT_ROWS - 1), lambda rows: pltpu.make_async_copy(
            g_ref.at[step % 2, pl.ds(0, rows), :], xs_ref.at[pl.ds(0, rows), :], sem.at[step % 2]))

    pl.when(i > 0)(lambda: drain(i - 1))

    def copy(r, k):
        src = pl.multiple_of(off_ref[i * N_EXPERTS + r] + k * DISPATCH_PIECE, SUBLANES)
        dst = pl.multiple_of(dst_ref[i * N_EXPERTS + r] + k * DISPATCH_PIECE, SUBLANES)
        pltpu.make_async_copy(g_ref.at[buf, pl.ds(src, DISPATCH_PIECE), :],
                              xs_ref.at[pl.ds(dst, DISPATCH_PIECE), :], sem.at[buf]).start()

    def per_expert(r, carry):
        n = n8_ref[i * N_EXPERTS + r]
        pl.when(n > 0)(lambda: copy(r, 0))

        @pl.when(n > DISPATCH_PIECE)
        def _():
            def more(k, c):
                copy(r, k)
                return c
            lax.fori_loop(1, lax.shift_right_logical(n + (DISPATCH_PIECE - 1), DISPATCH_PIECE.bit_length() - 1),
                          more, 0)
        return carry
    lax.fori_loop(0, N_EXPERTS, per_expert, 0, unroll=RUN_LOOP_UNROLL)
    pl.when(i == pl.num_programs(0) - 1)(lambda: drain(i))


def _dispatch(h2, slot_t, n8_flat, off_flat, dst_flat, tot, pad_end, n_rows):
    n_tok = h2.shape[0]
    tile = MOE_TILE
    grid_spec = pltpu.PrefetchScalarGridSpec(
        num_scalar_prefetch=5,
        grid=(n_tok // tile,),
        in_specs=[pl.BlockSpec((TOP_K, tile), lambda i, *_: (0, i)),
                  pl.BlockSpec((tile, D_MODEL), lambda i, *_: (i, 0))],
        out_specs=pl.BlockSpec(memory_space=pl.ANY),
        scratch_shapes=[pltpu.VMEM((2, MOE_SLOTS, D_MODEL // 2), U32),
                        pltpu.VMEM((MOE_ROWS, D_MODEL // 2), U32), pltpu.SemaphoreType.DMA((2,))],
    )
    return pl.pallas_call(
        _dispatch_kernel,
        out_shape=SDS((n_rows, D_MODEL // 2), U32),
        grid_spec=grid_spec,
        compiler_params=_cparams(("arbitrary",)),
        name="moe_dispatch",
    )(n8_flat, off_flat, dst_flat, tot, pad_end, slot_t, h2)


def _expert_kernel(first_ref, nblk_ref, exp_of_ref, xs_ref, w1_ref, w3_ref, w2_ref, y_ref,
                   xbuf, ybuf, wb1, wb3, wb2, in_sem, out_sem):
    del exp_of_ref
    r = pl.program_id(0)
    half = D_MODEL // 2
    used = first_ref[N_EXPERTS - 1] + nblk_ref[N_EXPERTS - 1]
    part = MOE_ROWS // EXPERT_DMA_PARTS

    class _Copies:
        def __init__(self, make):
            self.parts = [make(p) for p in range(EXPERT_DMA_PARTS)]

        def start(self):
            for c in self.parts:
                c.start()

        def wait(self):
            for c in self.parts:
                c.wait()

    def hbm_rows(g, p):
        return pl.ds(pl.multiple_of(g * MOE_ROWS + p * part, part), part)

    in_copy = lambda g, slot: _Copies(lambda p: pltpu.make_async_copy(
        xs_ref.at[hbm_rows(g, p), :], xbuf.at[slot, pl.ds(p * part, part), :], in_sem.at[slot]))
    out_copy = lambda g, slot: _Copies(lambda p: pltpu.make_async_copy(
        ybuf.at[slot, pl.ds(p * part, part), :], y_ref.at[hbm_rows(g, p), :], out_sem.at[slot]))

    @pl.when(r == 0)
    def _():
        for g in range(EXPERT_BUFS - 1):
            pl.when(g < used)(lambda g=g: in_copy(g, g).start())

    wb1[...] = w1_ref[0, 0].astype(BF16)
    wb3[...] = w3_ref[0, 0].astype(BF16)
    wb2[...] = w2_ref[0, 0].astype(BF16)

    def block(k, carry):
        g = first_ref[r] + k
        slot = g % EXPERT_BUFS
        in_copy(g, slot).wait()
        ahead = g + (EXPERT_BUFS - 1)
        pl.when(ahead < used)(lambda: in_copy(ahead, ahead % EXPERT_BUFS).start())
        pl.when(g >= EXPERT_BUFS)(lambda: out_copy(g - EXPERT_BUFS, slot).wait())
        dot = lambda a, b: jnp.dot(a, b, preferred_element_type=F32)
        x = _unpack_pairs(xbuf[slot])
        hid = (jax.nn.silu(dot(x, wb1[...])) * dot(x, wb3[...])).astype(BF16)
        y = dot(hid, wb2[...]).astype(BF16).astype(F32)
        bits = lax.bitcast_convert_type(y, U32)
        ybuf[slot] = _pack_pairs(bits[:, :half], bits[:, half:])
        out_copy(g, slot).start()
        return carry
    lax.fori_loop(0, nblk_ref[r], block, 0)

    @pl.when(r == N_EXPERTS - 1)
    def _():
        for back in range(EXPERT_BUFS, 0, -1):
            pl.when(used >= back)(
                lambda back=back: out_copy(used - back, (used - back) % EXPERT_BUFS).wait())

        ybuf[0] = jnp.zeros(ybuf.shape[1:], U32)
        n_blocks = y_ref.shape[0] // MOE_ROWS

        def clear(g, carry):
            out_copy(g, 0).start()
            return carry

        def done(g, carry):
            out_copy(g, 0).wait()
            return carry
        lax.fori_loop(used, n_blocks, clear, 0)
        lax.fori_loop(used, n_blocks, done, 0)


def _experts(xs, first_block, n_block, exp_of_row, w1, w3, w2, layer):
    n_rows = xs.shape[0]
    half = D_MODEL // 2
    w_in_blk = pl.BlockSpec((1, 1, D_MODEL, EXPERT_FF), lambda r, first, nblk, eo: (layer, eo[r], 0, 0))
    grid_spec = pltpu.PrefetchScalarGridSpec(
        num_scalar_prefetch=3,
        grid=(N_EXPERTS,),
        in_specs=[pl.BlockSpec(memory_space=pl.ANY), w_in_blk, w_in_blk,
                  pl.BlockSpec((1, 1, EXPERT_FF, D_MODEL), lambda r, first, nblk, eo: (layer, eo[r], 0, 0))],
        out_specs=pl.BlockSpec(memory_space=pl.ANY),
        scratch_shapes=[pltpu.VMEM((EXPERT_BUFS, MOE_ROWS, half), U32),
                        pltpu.VMEM((EXPERT_BUFS, MOE_ROWS, half), U32),
                        pltpu.VMEM((D_MODEL, EXPERT_FF), BF16), pltpu.VMEM((D_MODEL, EXPERT_FF), BF16),
                        pltpu.VMEM((EXPERT_FF, D_MODEL), BF16),
                        pltpu.SemaphoreType.DMA((EXPERT_BUFS,)), pltpu.SemaphoreType.DMA((EXPERT_BUFS,))],
    )
    return pl.pallas_call(
        _expert_kernel,
        out_shape=SDS((n_rows, half), U32),
        grid_spec=grid_spec,
        compiler_params=_cparams(("arbitrary",)),
        name="moe_experts",
    )(first_block, n_block, exp_of_row, xs, w1, w3, w2)


def _combine_kernel(n8_ref, off_ref, src_ref, tot_ref, y_ref, slot_ref, w_ref, h_ref, x_ref, g2_ref,
                    s1_ref, s3_ref, s2_ref, fn_ref, o_ref, yt_ref, sem, *, final):
    tile = h_ref.shape[1]
    i = pl.program_id(0) * pl.num_programs(1) + pl.program_id(1)

    last = pl.num_programs(0) * pl.num_programs(1) - 1

    def fetch(step, buf):
        def piece(r, offset, rows):
            src = pl.multiple_of(src_ref[step * N_EXPERTS + r] + offset, SUBLANES)
            dst = pl.multiple_of(off_ref[step * N_EXPERTS + r] + offset, SUBLANES)
            pltpu.make_async_copy(y_ref.at[pl.ds(src, rows), :], yt_ref.at[buf, pl.ds(dst, rows), :],
                                  sem.at[buf]).start()
        _for_each_run_piece(step, n8_ref, piece)

    def arrived(step, buf):
        _wait_rows(tot_ref[step], lambda rows: pltpu.make_async_copy(
            y_ref.at[pl.ds(0, rows), :], yt_ref.at[buf, pl.ds(0, rows), :], sem.at[buf]))

    @pl.when(i == 0)
    def _():
        yt_ref[...] = jnp.zeros(yt_ref.shape, U32)
        fetch(0, 0)

    fetch(jnp.minimum(i + 1, last), (i + 1) % 2)
    h = h_ref[0]
    shared = _bdot(jax.nn.silu(_bdot(h, s1_ref[...])) * _bdot(h, s3_ref[...]), s2_ref[...])
    slot_id = lax.broadcasted_iota(I32, (tile, MOE_SLOTS), 1)
    slots = slot_ref[0]
    w = w_ref[0]
    pw = jnp.zeros((tile, MOE_SLOTS), F32)
    for j in range(TOP_K):
        pw = jnp.where(slot_id == slots[:, j:j + 1], w[:, j:j + 1], pw)
    arrived(i, i % 2)
    routed = jnp.dot(pw.astype(BF16), _unpack_pairs(yt_ref[i % 2]), preferred_element_type=F32)
    pl.when(i == last)(lambda: arrived(i, (i + 1) % 2))
    xn = x_ref[0] + g2_ref[0] * (routed + shared)
    if final:
        xn = xn * lax.rsqrt(jnp.mean(xn * xn, axis=-1, keepdims=True) + NORM_EPS) * fn_ref[...]
    o_ref[0] = xn


def _combine(y, slot_nat, w_nat, n8_flat, off_flat, dst_flat, tot, h2, x, g2, s1_bf, s3_bf, s2_bf,
             final_norm, final):
    bsz, seq, _ = x.shape
    tile = MOE_TILE
    per_seq = seq // tile
    blk = pl.BlockSpec((1, tile, D_MODEL), lambda b, i, *_: (b, i, 0))
    tok = pl.BlockSpec((1, tile, TOP_K), lambda b, i, *_: (b, i, 0))
    full = lambda shape: pl.BlockSpec(shape, lambda b, i, *_: (0,) * len(shape))
    grid_spec = pltpu.PrefetchScalarGridSpec(
        num_scalar_prefetch=4,
        grid=(bsz, per_seq),
        in_specs=[pl.BlockSpec(memory_space=pl.ANY), tok, tok, blk, blk,
                  pl.BlockSpec((1, 1, D_MODEL), lambda b, i, *_: (b, 0, 0)),
                  full((D_MODEL, SHARED_FF)), full((D_MODEL, SHARED_FF)), full((SHARED_FF, D_MODEL)),
                  full((1, D_MODEL))],
        out_specs=blk,
        scratch_shapes=[pltpu.VMEM((2, MOE_SLOTS, D_MODEL // 2), U32), pltpu.SemaphoreType.DMA((2,))],
    )
    return pl.pallas_call(
        functools.partial(_combine_kernel, final=final),
        out_shape=SDS((bsz, seq, D_MODEL), F32),
        grid_spec=grid_spec,
        compiler_params=_cparams(("arbitrary", "arbitrary")),
        name="moe_combine",
    )(n8_flat, off_flat, dst_flat, tot, y, slot_nat, w_nat, h2, x, g2, s1_bf, s3_bf, s2_bf, final_norm)


def _moe(x, h2, g2, router_w, router_bias, w1, w3, w2, layer, s1_bf, s3_bf, s2_bf, final_norm, final):
    bsz, seq, _ = x.shape
    assert seq % MOE_TILE == 0, "token tiles must not straddle sequences"
    n_tok = bsz * seq
    slot_t, w_t, n8 = _route(h2.reshape(n_tok, D_MODEL), router_w, router_bias)
    n_tiles = n8.shape[0]
    counts = jnp.sum(n8, axis=0)
    padded = jnp.where(counts > 0, (counts + DISPATCH_SPARE + MOE_ROWS - 1) // MOE_ROWS * MOE_ROWS, 0)
    pad_end = jnp.cumsum(padded).astype(I32)
    pad_start = pad_end - padded
    run_row = (pad_start[None, :] + jnp.cumsum(n8, axis=0) - n8).astype(I32)
    run_slot = (jnp.cumsum(n8, axis=1) - n8).astype(I32)
    tot = jnp.sum(n8, axis=1).astype(I32)
    copied = jnp.sum((n8 + DISPATCH_PIECE - 1) // DISPATCH_PIECE * DISPATCH_PIECE, axis=1).astype(I32)
    max_rows = (n_tok * TOP_K + n_tiles * N_EXPERTS * (SUBLANES - 1)
                + N_EXPERTS * (DISPATCH_SPARE + MOE_ROWS - 1))
    n_blocks = (max_rows + MOE_ROWS - 1) // MOE_ROWS
    row_id = jnp.arange(N_EXPERTS, dtype=I32)
    exp_of_row = (row_id % N_GROUPS) * GROUP_SIZE + row_id // N_GROUPS
    flat = lambda a: a.reshape(-1).astype(I32)
    xs = _dispatch(h2.reshape(n_tok, D_MODEL), slot_t, flat(n8), flat(run_slot), flat(run_row), copied,
                   pad_end, n_blocks * MOE_ROWS)
    y = _experts(xs, (pad_start // MOE_ROWS).astype(I32), (padded // MOE_ROWS).astype(I32), exp_of_row,
                 w1, w3, w2, layer)
    nat = lambda a: a.T.reshape(bsz, seq, TOP_K)
    return _combine(y, nat(slot_t), nat(w_t), flat(n8), flat(run_slot), flat(run_row), tot, h2, x, g2,
                    s1_bf, s3_bf, s2_bf, final_norm, final)


def _pad_cols(a, width):
    return jnp.pad(a, ((0, 0), (0, width - a.shape[1])))


def _layout_w_in(w_in):
    b0 = COLS_A
    c0 = COLS_A + COLS_B
    g0 = c0 + COLS_C
    seg_b = _pad_cols(w_in[:, b0:c0], SEG_G - SEG_B)
    return jnp.concatenate([w_in[:, c0:g0], seg_b, w_in[:, g0:], w_in[:, :b0]], axis=1).astype(BF16)


def _layout_mu(mu):
    return _pad_cols(mu.reshape(1, -1), SEG_G - SEG_B)


def _pad_rows(a, height):
    return jnp.pad(a, ((0, height - a.shape[0]), (0, 0)))


def kernel(x, c, ada_w, ada_b, norm1, norm2, w_in, conv_a_w, conv_a_b, ln_a_g, ln_a_b, proj_a, mu_b, w0, w_up, a0, a_up, g_up, k_k, k_a, r_k, gn_b_g, gn_b_b, proj_b, conv_c_w, conv_c_b, lru_wa, lru_ba, lru_wx, lru_bx, lru_lambda, proj_c, w_out, router_w, router_bias, exp_w1, exp_w3, exp_w2, sh_w1, sh_w3, sh_w2, final_norm):
    depth = ada_w.shape[0]
    bsz = x.shape[0]
    mod = _ada_mod(c, ada_w, ada_b)
    for l in range(depth):
        sh1, sc1, g1, sh2, sc2, g2 = [mod[l, :, i * D_MODEL:(i + 1) * D_MODEL].reshape(bsz, 1, D_MODEL)
                                      for i in range(N_MOD)]
        p = _in_proj(x, sc1, sh1, norm1[l].reshape(1, -1), _layout_w_in(w_in[l]))
        merged = _conv_a(p, conv_a_w[l], conv_a_b[l], ln_a_g[l], ln_a_b[l], proj_a[l].astype(BF16))
        merged = _rwkv(p, merged, _layout_mu(mu_b[l]), w0[l], _pad_rows(w_up[l], LANES).astype(BF16),
                       a0[l], jnp.pad(a_up[l], ((LORA_W, 0), (0, 0))).astype(BF16), g_up[l].astype(BF16),
                       k_k[l], k_a[l], r_k[l], gn_b_g[l], gn_b_b[l], proj_b[l].astype(BF16))
        merged = _lru(p, merged, conv_c_w[l], conv_c_b[l], lru_wa[l].astype(BF16), lru_ba[l],
                      lru_wx[l].astype(BF16), lru_bx[l], lru_lambda[l], proj_c[l].astype(BF16))
        x, h2 = _out_proj(merged, x, g1, w_out[l].astype(BF16), norm2[l].reshape(1, -1), sc2, sh2)
        x = _moe(x, h2, g2, router_w[l], router_bias[l], exp_w1, exp_w3, exp_w2, l,
                 sh_w1[l].astype(BF16), sh_w3[l].astype(BF16), sh_w2[l].astype(BF16),
                 final_norm.reshape(1, -1), final=(l == depth - 1))
    return x
```

```python
import functools

import jax
import jax.numpy as jnp
from jax import lax
from jax.experimental import pallas as pl
from jax.experimental.pallas import tpu as pltpu

F32 = jnp.float32
BF16 = jnp.bfloat16
I32 = jnp.int32
SDS = jax.ShapeDtypeStruct
HIGHEST = lax.Precision.HIGHEST

D_MODEL = 1024
N_MOD = 6
NORM_EPS = 1e-6
CONV_A_CH = 512
CONV_A_WIDTH = 31
CONV_A_LN_EPS = 1e-5
RWKV_HEADS = 8
RWKV_HEAD = 64
RWKV_DIM = RWKV_HEADS * RWKV_HEAD
LORA_W = 64
LORA_A = 64
LORA_G = 128
RWKV_GN_EPS = 64e-5
RWKV_CHUNK = 64
LRU_DIM = 1024
LRU_HEADS = 8
LRU_BLOCK = LRU_DIM // LRU_HEADS
LRU_CONV = 4
LRU_C = 8.0
N_EXPERTS = 64
TOP_K = 8
N_GROUPS = 8
GROUP_SIZE = N_EXPERTS // N_GROUPS
TOPK_GROUPS = 4
EXPERT_FF = 256
SHARED_FF = 256
ROUTED_SCALE = 2.5
SEG_C = 0
SEG_B = 2048
SEG_G = 4096
SEG_A = 7168
IN_COLS_PAD = 8192
RW_R, RW_K, RW_V, RW_XWA, RW_XG = 0, 512, 1024, 1536, 1664
COLS_A = 2 * CONV_A_CH
COLS_B = 3 * RWKV_DIM + LORA_W + LORA_A + LORA_G
COLS_C = 2 * LRU_DIM
VMEM_LIMIT = 56 * 1024 * 1024
MERGED_DTYPE = BF16
SUBLANES = 8
LANES = 128
MOE_ROWS = 512
EXPERT_DMA_PARTS = 4
EXPERT_BUFS = 4


def _cparams(sem):
    return pltpu.CompilerParams(dimension_semantics=sem, vmem_limit_bytes=VMEM_LIMIT)


def _bdot(a, b):
    return jnp.dot(a.astype(BF16), b.astype(BF16), preferred_element_type=F32)


def _hdot(a, b):
    return jnp.dot(a, b, preferred_element_type=F32, precision=HIGHEST)


def _split(a):
    hi = a.astype(BF16)
    return hi, (a - hi.astype(F32)).astype(BF16)


def _head_sums(a, ones_bf):
    hi, lo = _split(a)
    return (jnp.dot(hi, ones_bf, preferred_element_type=F32)
            + jnp.dot(lo, ones_bf, preferred_element_type=F32))


def _hdot_nt(a, b):
    return lax.dot_general(a, b, (((1,), (1,)), ((), ())), preferred_element_type=F32,
                           precision=HIGHEST)


def _hdot_tn(a, b):
    return lax.dot_general(a, b, (((0,), (0,)), ((), ())), preferred_element_type=F32,
                           precision=HIGHEST)


def _ada_kernel(c_ref, w_ref, b_ref, o_ref):
    cond = jax.nn.silu(c_ref[...])
    o_ref[0] = _bdot(cond, w_ref[0]) + b_ref[0]


def _ada_mod(c, ada_w, ada_b):
    depth, _, n = ada_w.shape
    bsz = c.shape[0]
    tn = 1536
    return pl.pallas_call(
        _ada_kernel,
        out_shape=SDS((depth, bsz, n), F32),
        grid=(depth, n // tn),
        in_specs=[pl.BlockSpec((bsz, D_MODEL), lambda l, j: (0, 0)),
                  pl.BlockSpec((1, D_MODEL, tn), lambda l, j: (l, 0, j)),
                  pl.BlockSpec((1, 1, tn), lambda l, j: (l, 0, j))],
        out_specs=pl.BlockSpec((1, bsz, tn), lambda l, j: (l, 0, j)),
        compiler_params=_cparams(("arbitrary", "arbitrary")),
        name="ada_mod",
    )(c, ada_w, ada_b.reshape(depth, 1, n))


def _modulated_rmsnorm(x, g, sc, sh):
    y = x * lax.rsqrt(jnp.mean(x * x, axis=-1, keepdims=True) + NORM_EPS)
    return (y * g) * (1.0 + sc) + sh


IN_PROJ_SUB = 512


def _in_kernel(x_ref, sc_ref, sh_ref, g_ref, w_ref, o_ref, h_ref):
    @pl.when(pl.program_id(2) == 0)
    def _():
        h_ref[...] = _modulated_rmsnorm(x_ref[0], g_ref[...], sc_ref[0], sh_ref[0]).astype(BF16)

    j = pl.program_id(2)
    ts, tn = o_ref.shape[1], o_ref.shape[2]
    is_gelu = j == SEG_C // tn
    is_gate = (j >= SEG_G // tn) & (j < SEG_A // tn)
    is_glu = j == SEG_A // tn

    def glu(acc):
        u = acc[:, :tn // 2] * jax.nn.sigmoid(acc[:, tn // 2:])
        return jnp.concatenate([u, jnp.zeros_like(u)], axis=1)

    def emit(act, sub):
        for r0 in range(0, ts, sub):
            rows = pl.ds(r0, sub)
            acc = jnp.dot(h_ref[rows, :], w_ref[...], preferred_element_type=F32)
            o_ref[0, rows, :] = act(acc).astype(o_ref.dtype)

    sub = min(ts, IN_PROJ_SUB)
    pl.when(is_gelu)(lambda: emit(functools.partial(jax.nn.gelu, approximate=True), sub))
    pl.when(is_gate)(lambda: emit(jax.nn.sigmoid, sub))
    pl.when(is_glu)(lambda: emit(glu, sub))
    pl.when(jnp.logical_not(is_gelu | is_gate | is_glu))(lambda: emit(lambda acc: acc, ts))


def _in_proj(x, sc, sh, g, w_pad):
    bsz, seq, _ = x.shape
    ts = min(seq, 1024)
    tn = 1024
    return pl.pallas_call(
        _in_kernel,
        out_shape=SDS((bsz, seq, IN_COLS_PAD), BF16),
        grid=(bsz, seq // ts, IN_COLS_PAD // tn),
        in_specs=[pl.BlockSpec((1, ts, D_MODEL), lambda b, i, j: (b, i, 0)),
                  pl.BlockSpec((1, 1, D_MODEL), lambda b, i, j: (b, 0, 0)),
                  pl.BlockSpec((1, 1, D_MODEL), lambda b, i, j: (b, 0, 0)),
                  pl.BlockSpec((1, D_MODEL), lambda b, i, j: (0, 0)),
                  pl.BlockSpec((D_MODEL, tn), lambda b, i, j: (0, j))],
        out_specs=pl.BlockSpec((1, ts, tn), lambda b, i, j: (b, i, j)),
        scratch_shapes=[pltpu.VMEM((ts, D_MODEL), BF16)],
        compiler_params=_cparams(("arbitrary", "arbitrary", "arbitrary")),
        name="in_proj",
    )(x, sc, sh, g, w_pad)


CONV_A_HALO = 32
CONV_A_SUB = 64


def _conv_a_kernel(pa_ref, pg_ref, cw_ref, cb_ref, lg_ref, lb_ref, pj_ref, o_ref, ext_ref, sh_ref, y_ref):
    ts = pa_ref.shape[1]

    @pl.when(pl.program_id(1) == 0)
    def _():
        ext_ref[pl.ds(0, CONV_A_HALO), :] = jnp.zeros((CONV_A_HALO, CONV_A_CH), F32)

    ext_ref[pl.ds(CONV_A_HALO, ts), :] = pa_ref[0].astype(F32)
    for p in range(1, SUBLANES):
        sh_ref[p - 1] = ext_ref[pl.ds(p, sh_ref.shape[1]), :]
    first = CONV_A_HALO - (CONV_A_WIDTH - 1)
    for r0 in range(0, ts, CONV_A_SUB):
        acc = jnp.zeros((CONV_A_SUB, CONV_A_CH), F32) + cb_ref[...]
        for j in range(CONV_A_WIDTH):
            phase = (first + j) % SUBLANES
            rows = pl.ds(r0 + first + j - phase, CONV_A_SUB)
            tap = ext_ref[rows, :] if phase == 0 else sh_ref[phase - 1, rows, :]
            acc = acc + tap * cw_ref[pl.ds(j, 1), :]
        y_ref[pl.ds(r0, CONV_A_SUB), :] = acc
    ext_ref[pl.ds(0, CONV_A_HALO), :] = ext_ref[pl.ds(ts, CONV_A_HALO), :]
    y = y_ref[...]
    mu = jnp.mean(y, axis=-1, keepdims=True)
    d = y - mu
    var = jnp.mean(d * d, axis=-1, keepdims=True)
    yn = d * lax.rsqrt(var + CONV_A_LN_EPS) * lg_ref[...] + lb_ref[...]
    o = _bdot(jax.nn.silu(yn), pj_ref[...])
    o_ref[0] = (pg_ref[0].astype(F32) * o).astype(o_ref.dtype)


def _conv_a(p, conv_w, conv_b, ln_g, ln_b, proj_bf):
    bsz, seq, _ = p.shape
    ts = min(seq, 512)
    row = lambda a: a.reshape(1, -1)
    full = lambda shape: pl.BlockSpec(shape, lambda b, i: (0,) * len(shape))
    return pl.pallas_call(
        _conv_a_kernel,
        out_shape=SDS((bsz, seq, D_MODEL), MERGED_DTYPE),
        grid=(bsz, seq // ts),
        in_specs=[pl.BlockSpec((1, ts, CONV_A_CH), lambda b, i: (b, i, SEG_A // CONV_A_CH)),
                  pl.BlockSpec((1, ts, D_MODEL), lambda b, i: (b, i, SEG_G // D_MODEL)),
                  full((CONV_A_WIDTH, CONV_A_CH)), full((1, CONV_A_CH)), full((1, CONV_A_CH)),
                  full((1, CONV_A_CH)), full((CONV_A_CH, D_MODEL))],
        out_specs=pl.BlockSpec((1, ts, D_MODEL), lambda b, i: (b, i, 0)),
        scratch_shapes=[pltpu.VMEM((ts + CONV_A_HALO, CONV_A_CH), F32),
                        pltpu.VMEM((SUBLANES - 1, ts + CONV_A_HALO - SUBLANES, CONV_A_CH), F32),
                        pltpu.VMEM((ts, CONV_A_CH), F32)],
        compiler_params=_cparams(("arbitrary", "arbitrary")),
        name="conv_a",
    )(p, p, conv_w, row(conv_b), row(ln_g), row(ln_b), proj_bf)


def _lru_kernel(pc_ref, pg_ref, m_ref, cw_ref, cb_ref, wa_ref, ba_ref, wx_ref, bx_ref, lam_ref,
                pj_ref, o_ref, ext_ref, h_ref, a_ref, b_ref):
    ts = pc_ref.shape[1]
    groups = ts // SUBLANES

    @pl.when(pl.program_id(1) == 0)
    def _():
        ext_ref[pl.ds(0, SUBLANES), :] = jnp.zeros((SUBLANES, LRU_DIM), F32)
        h_ref[...] = jnp.zeros((SUBLANES, LRU_DIM), F32)

    pc = pc_ref[0].astype(F32)
    y_gate = pc[:, :LRU_DIM]
    ext_ref[pl.ds(SUBLANES, ts), :] = pc[:, LRU_DIM:]
    first = SUBLANES - (LRU_CONV - 1)
    xc = jnp.zeros((ts, LRU_DIM), F32) + cb_ref[...]
    for j in range(LRU_CONV):
        xc = xc + ext_ref[pl.ds(first + j, ts), :] * cw_ref[pl.ds(j, 1), :]
    ext_ref[pl.ds(0, SUBLANES), :] = ext_ref[pl.ds(ts, SUBLANES), :]

    def block_diag(w_ref):
        return jnp.concatenate(
            [_bdot(xc[:, h * LRU_BLOCK:(h + 1) * LRU_BLOCK], w_ref[h]) for h in range(LRU_HEADS)],
            axis=1)

    gate_a = jax.nn.sigmoid(block_diag(wa_ref) + ba_ref[...])
    gate_x = jax.nn.sigmoid(block_diag(wx_ref) + bx_ref[...])
    log_a = -LRU_C * gate_a * jax.nn.softplus(-lam_ref[...])
    a = jnp.exp(log_a)
    b = xc * gate_x * jnp.sqrt(1.0 - jnp.exp(2.0 * log_a))

    a3 = a.reshape(groups, SUBLANES, LRU_DIM)
    b3 = b.reshape(groups, SUBLANES, LRU_DIM)
    row = lax.broadcasted_iota(I32, (groups, SUBLANES, LRU_DIM), 1)
    for s in (1, 2, 4):
        keep = row >= s
        b3 = jnp.where(keep, a3 * pltpu.roll(b3, s, axis=1) + b3, b3)
        a3 = jnp.where(keep, a3 * pltpu.roll(a3, s, axis=1), a3)
    a_ref[...] = a3.reshape(ts, LRU_DIM)
    b_ref[...] = b3.reshape(ts, LRU_DIM)
    h = h_ref[...]
    for g in range(groups):
        rows = pl.ds(g * SUBLANES, SUBLANES)
        hg = a_ref[rows, :] * h + b_ref[rows, :]
        b_ref[rows, :] = hg
        h = jnp.broadcast_to(hg[SUBLANES - 1:SUBLANES, :], (SUBLANES, LRU_DIM))
    h_ref[...] = h
    o = _bdot(b_ref[...] * y_gate, pj_ref[...])
    o_ref[0] = (m_ref[0].astype(F32) + pg_ref[0].astype(F32) * o).astype(o_ref.dtype)


def _lru(p, merged, conv_w, conv_b, wa_bf, ba, wx_bf, bx, lam, proj_bf):
    bsz, seq, _ = p.shape
    ts = min(seq, 256)
    row = lambda a: a.reshape(1, -1)
    full = lambda shape: pl.BlockSpec(shape, lambda b, i: (0,) * len(shape))
    return pl.pallas_call(
        _lru_kernel,
        out_shape=SDS((bsz, seq, D_MODEL), MERGED_DTYPE),
        grid=(bsz, seq // ts),
        in_specs=[pl.BlockSpec((1, ts, 2 * LRU_DIM), lambda b, i: (b, i, SEG_C // (2 * LRU_DIM))),
                  pl.BlockSpec((1, ts, D_MODEL), lambda b, i: (b, i, SEG_G // D_MODEL + 2)),
                  pl.BlockSpec((1, ts, D_MODEL), lambda b, i: (b, i, 0)),
                  full((LRU_CONV, LRU_DIM)), full((1, LRU_DIM)),
                  full((LRU_HEADS, LRU_BLOCK, LRU_BLOCK)), full((1, LRU_DIM)),
                  full((LRU_HEADS, LRU_BLOCK, LRU_BLOCK)), full((1, LRU_DIM)),
                  full((1, LRU_DIM)), full((LRU_DIM, D_MODEL))],
        out_specs=pl.BlockSpec((1, ts, D_MODEL), lambda b, i: (b, i, 0)),
        scratch_shapes=[pltpu.VMEM((ts + SUBLANES, LRU_DIM), F32),
                        pltpu.VMEM((SUBLANES, LRU_DIM), F32),
                        pltpu.VMEM((ts, LRU_DIM), F32),
                        pltpu.VMEM((ts, LRU_DIM), F32)],
        compiler_params=_cparams(("arbitrary", "arbitrary")),
        name="rg_lru",
    )(p, p, merged, conv_w, row(conv_b), wa_bf, row(ba), wx_bf, row(bx), row(lam), proj_bf)


def _rwkv_prep_kernel(pb_ref, mu_ref, w0_ref, wup_ref, a0_ref, aup_ref, gup_ref, kk_ref, ka_ref,
                      rk_ref, bd_ref, ltri_ref,
                      rt_ref, kkt_ref, kh_ref, bh_ref, v_ref, pinc_ref, bonus_ref, g_ref, ext_ref):
    ts = pb_ref.shape[1]

    @pl.when(pl.program_id(1) == 0)
    def _():
        ext_ref[pl.ds(0, SUBLANES), :] = jnp.zeros((SUBLANES, ext_ref.shape[1]), F32)

    p = pb_ref[0].astype(F32)
    ext_ref[pl.ds(SUBLANES, ts), :] = p
    prev = ext_ref[pl.ds(SUBLANES - 1, ts), :]
    ext_ref[pl.ds(0, SUBLANES), :] = ext_ref[pl.ds(ts, SUBLANES), :]
    pm = p + (prev - p) * mu_ref[...]
    r = pm[:, RW_R:RW_R + RWKV_DIM]
    k = pm[:, RW_K:RW_K + RWKV_DIM]
    v = pm[:, RW_V:RW_V + RWKV_DIM]
    xwa = pm[:, RW_XWA:RW_XWA + LANES]
    xg = pm[:, RW_XG:RW_XG + LORA_G]
    w = -jax.nn.softplus(-(w0_ref[...] + _bdot(jnp.tanh(xwa), wup_ref[...]))) - 0.5
    lw = -jnp.exp(w)
    a = jax.nn.sigmoid(a0_ref[...] + _bdot(xwa, aup_ref[...]))
    g_ref[0] = _bdot(jax.nn.sigmoid(xg), gup_ref[...])
    kkr = k * kk_ref[...]
    ss = _head_sums(kkr * kkr, bd_ref[...])
    kk = kkr / jnp.maximum(jnp.sqrt(ss), 1e-12)
    k2 = k * (1.0 + (a - 1.0) * ka_ref[...])
    lw_hi = lw.astype(BF16)
    lw_mid, lw_lo = _split(lw - lw_hi.astype(F32))
    tri = ltri_ref[...]
    lcum = (jnp.dot(tri, lw_hi, preferred_element_type=F32) + jnp.dot(tri, lw_mid, preferred_element_type=F32)
            + jnp.dot(tri, lw_lo, preferred_element_type=F32))
    pinc = jnp.exp(lcum)
    pinv = jnp.exp(-lcum)
    rt_ref[0] = r * pinc
    kkt_ref[0] = kk * jnp.exp(lcum - lw)
    kh_ref[0] = k2 * pinv
    bh_ref[0] = kk * a * pinv
    v_ref[0] = v
    pinc_ref[0] = pinc
    bonus_ref[0] = _head_sums(r * k2 * rk_ref[...], bd_ref[...]) * v


def _dot3(a, b):
    d = lambda x, y: jnp.dot(x, y, preferred_element_type=F32)
    m = a[0].shape[0]
    both = d(jnp.concatenate([a[0], a[1]], axis=0), b[0])
    return both[:m] + both[m:] + d(a[0], b[1])


def _rwkv_scan_kernel(rt_ref, kkt_ref, kh_ref, bh_ref, v_ref, pinc_ref, y_ref, s_ref):
    c = RWKV_CHUNK
    n = RWKV_HEAD
    nb = rt_ref.shape[0]
    heads = range(nb * RWKV_HEADS)

    @pl.when(pl.program_id(1) == 0)
    def _():
        s_ref[...] = jnp.zeros(s_ref.shape, F32)

    row = lax.broadcasted_iota(I32, (c, c), 0)
    col = lax.broadcasted_iota(I32, (c, c), 1)
    eye = (row == col).astype(F32)
    same16 = (row // 16) == (col // 16)
    same32 = (row // 32) == (col // 32)
    row2 = lax.broadcasted_iota(I32, (c, 2 * c), 0)
    col2 = lax.broadcasted_iota(I32, (c, 2 * c), 1) % c
    nt = lambda a, b: lax.dot_general(a, b, (((1,), (1,)), ((), ())), preferred_element_type=F32)
    tn = lambda a, b: lax.dot_general(a, b, (((0,), (0,)), ((), ())), preferred_element_type=F32)
    dot = lambda a, b: jnp.dot(a, b, preferred_element_type=F32)
    sl = [pl.ds((h % RWKV_HEADS) * n, n) for h in heads]
    sq = [h // RWKV_HEADS for h in heads]
    v = [v_ref[sq[h], :, sl[h]] for h in heads]
    pc = [pinc_ref[sq[h], pl.ds(c - 1, 1), sl[h]] for h in heads]
    s = [s_ref[sq[h], :, sl[h]] for h in heads]
    lhs = [jnp.concatenate([kkt_ref[sq[h], :, sl[h]], rt_ref[sq[h], :, sl[h]]], axis=0) for h in heads]
    rhs = [jnp.concatenate([bh_ref[sq[h], :, sl[h]], kh_ref[sq[h], :, sl[h]]], axis=0) for h in heads]
    big = [nt(lhs[h], rhs[h]) for h in heads]
    from_state = [nt(lhs[h], s[h]) for h in heads]
    top = [jnp.where(row2 > col2, big[h][:c], 0.0) for h in heads]
    bot = [jnp.where(row2 >= col2, big[h][c:], 0.0) for h in heads]
    a_b = [top[h][:, :c] for h in heads]
    akv = [dot(top[h], jnp.concatenate([jnp.zeros((c, n), F32), v[h]], axis=0)) for h in heads]
    d16 = [jnp.where(same16, a_b[h], 0.0) for h in heads]
    sd = [_split(d16[h]) for h in heads]
    s2 = [_split(_dot3(sd[h], sd[h])) for h in heads]
    s4 = [_split(_dot3(s2[h], s2[h])) for h in heads]
    s8 = [_split(_dot3(s4[h], s4[h])) for h in heads]
    t = [eye - d16[h] for h in heads]
    for sp in (s2, s4, s8):
        t = [t[h] + _dot3(_split(t[h]), sp[h]) for h in heads]
    for off in ([jnp.where(same32 & jnp.logical_not(same16), a_b[h], 0.0) for h in heads],
                [jnp.where(same32, 0.0, a_b[h]) for h in heads]):
        tb = [t[h].astype(BF16) for h in heads]
        lt = [dot(off[h].astype(BF16), tb[h]).astype(BF16) for h in heads]
        t = [t[h] - dot(tb[h], lt[h]) for h in heads]
    u = [dot(t[h], from_state[h][:c] + akv[h]) for h in heads]
    vu = [jnp.concatenate([-u[h], v[h]], axis=0) for h in heads]
    y = [from_state[h][c:] + dot(bot[h], vu[h]) for h in heads]
    s_new = [s[h] * pc[h] + tn(vu[h], rhs[h] * pc[h]) for h in heads]
    for q in range(nb):
        mine = slice(q * RWKV_HEADS, (q + 1) * RWKV_HEADS)
        y_ref[q] = jnp.concatenate(y[mine], axis=1)
        s_ref[q] = jnp.concatenate(s_new[mine], axis=1)


def _rwkv_post_kernel(y_ref, bonus_ref, g_ref, pg_ref, m_ref, gg_ref, gb_ref, bdm_ref, pj_ref, o_ref):
    y = y_ref[0]
    mu = _head_sums(y, bdm_ref[...])
    d = y - mu
    var = _head_sums(d * d, bdm_ref[...])
    yn = d * lax.rsqrt(var + RWKV_GN_EPS) * gg_ref[...] + gb_ref[...] + bonus_ref[0]
    o = _bdot(yn * g_ref[0], pj_ref[...])
    o_ref[0] = (m_ref[0].astype(F32) + pg_ref[0].astype(F32) * o).astype(o_ref.dtype)


def _rwkv(p, merged, mu_pad, w0, wup_pad, a0, aup_pad, g_up, k_k, k_a, r_k, gn_g, gn_b, proj_bf):
    bsz, seq, _ = p.shape
    row = lambda a: a.reshape(1, -1)
    full = lambda shape: pl.BlockSpec(shape, lambda b, i: (0,) * len(shape))
    head_id = jnp.arange(RWKV_DIM, dtype=I32) // RWKV_HEAD
    bd = (head_id[:, None] == head_id[None, :]).astype(BF16)

    ts = min(seq, 256)
    t_id = jnp.arange(ts, dtype=I32)
    ltri = ((t_id[:, None] // RWKV_CHUNK == t_id[None, :] // RWKV_CHUNK)
            & (t_id[:, None] >= t_id[None, :])).astype(BF16)
    seq_blk = lambda width: pl.BlockSpec((1, ts, width), lambda b, i: (b, i, 0))
    wide = SDS((bsz, seq, RWKV_DIM), F32)
    rt, kkt, kh, bh, v, pinc, bonus, g = pl.pallas_call(
        _rwkv_prep_kernel,
        out_shape=[wide] * 8,
        grid=(bsz, seq // ts),
        in_specs=[pl.BlockSpec((1, ts, 2048), lambda b, i: (b, i, SEG_B // 2048)),
                  full((1, 2048)), full((1, RWKV_DIM)), full((LANES, RWKV_DIM)),
                  full((1, RWKV_DIM)), full((LANES, RWKV_DIM)), full((LORA_G, RWKV_DIM)),
                  full((1, RWKV_DIM)), full((1, RWKV_DIM)), full((1, RWKV_DIM)),
                  full((RWKV_DIM, RWKV_DIM)), full((ts, ts))],
        out_specs=[seq_blk(RWKV_DIM)] * 8,
        scratch_shapes=[pltpu.VMEM((ts + SUBLANES, 2048), F32)],
        compiler_params=_cparams(("arbitrary", "arbitrary")),
        name="rwkv_prep",
    )(p, mu_pad, row(w0), wup_pad, row(a0), aup_pad, g_up, row(k_k), row(k_a), row(r_k), bd, ltri)

    c = RWKV_CHUNK
    nb = 2 if bsz % 2 == 0 else 1
    chunk_blk = pl.BlockSpec((nb, c, RWKV_DIM), lambda b, i: (b, i, 0))
    y = pl.pallas_call(
        _rwkv_scan_kernel,
        out_shape=wide,
        grid=(bsz // nb, seq // c),
        in_specs=[chunk_blk] * 6,
        out_specs=chunk_blk,
        scratch_shapes=[pltpu.VMEM((nb, RWKV_HEAD, RWKV_DIM), F32)],
        compiler_params=_cparams(("arbitrary", "arbitrary")),
        name="rwkv_scan",
    )(rt, kkt, kh, bh, v, pinc)

    tp = min(seq, 512)
    blk = lambda width: pl.BlockSpec((1, tp, width), lambda b, i: (b, i, 0))
    return pl.pallas_call(
        _rwkv_post_kernel,
        out_shape=SDS((bsz, seq, D_MODEL), MERGED_DTYPE),
        grid=(bsz, seq // tp),
        in_specs=[blk(RWKV_DIM), blk(RWKV_DIM), blk(RWKV_DIM),
                  pl.BlockSpec((1, tp, D_MODEL), lambda b, i: (b, i, SEG_G // D_MODEL + 1)),
                  blk(D_MODEL), full((1, RWKV_DIM)), full((1, RWKV_DIM)),
                  full((RWKV_DIM, RWKV_DIM)), full((RWKV_DIM, D_MODEL))],
        out_specs=blk(D_MODEL),
        compiler_params=_cparams(("arbitrary", "arbitrary")),
        name="rwkv_post",
    )(y, bonus, g, p, merged, row(gn_g), row(gn_b), bd * (1.0 / RWKV_HEAD), proj_bf)


def _out_kernel(m_ref, x_ref, g1_ref, w_ref, n2_ref, sc_ref, sh_ref, xo_ref, h_ref):
    xn = x_ref[0] + g1_ref[0] * _bdot(m_ref[0], w_ref[...])
    xo_ref[0] = xn
    h_ref[0] = _modulated_rmsnorm(xn, n2_ref[...], sc_ref[0], sh_ref[0])


def _out_proj(merged, x, g1, w_bf, norm2, sc2, sh2):
    bsz, seq, _ = x.shape
    ts = min(seq, 512)
    blk = pl.BlockSpec((1, ts, D_MODEL), lambda b, i: (b, i, 0))
    per_b = pl.BlockSpec((1, 1, D_MODEL), lambda b, i: (b, 0, 0))
    return pl.pallas_call(
        _out_kernel,
        out_shape=[SDS((bsz, seq, D_MODEL), F32)] * 2,
        grid=(bsz, seq // ts),
        in_specs=[blk, blk, per_b, pl.BlockSpec((D_MODEL, D_MODEL), lambda b, i: (0, 0)),
                  pl.BlockSpec((1, D_MODEL), lambda b, i: (0, 0)), per_b, per_b],
        out_specs=[blk, blk],
        compiler_params=_cparams(("arbitrary", "arbitrary")),
        name="out_proj",
    )(merged, x, g1, w_bf, norm2, sc2, sh2)


MOE_TILE = 256
MOE_SLOTS = 2560
RUN_LOOP_UNROLL = 4
DISPATCH_PIECE = 48
DISPATCH_SPARE = DISPATCH_PIECE - SUBLANES
DISPATCH_WAIT_ROWS = 2048
assert MOE_SLOTS >= TOP_K * MOE_TILE + N_EXPERTS * (SUBLANES - 1) + DISPATCH_SPARE
U32 = jnp.uint32


def _route_kernel(h_ref, rw_ref, bias_ref, upper_ref, ltri_ref, slot_ref, w_ref, n8_ref):
    tile = h_ref.shape[0]
    neg = -jnp.inf
    scores = jax.nn.sigmoid(_hdot_nt(rw_ref[...], h_ref[...]))
    s3 = scores.reshape(GROUP_SIZE, N_GROUPS, tile)
    b3 = s3 + bias_ref[...].reshape(GROUP_SIZE, N_GROUPS, tile)
    slab = lax.broadcasted_iota(I32, b3.shape, 0).astype(F32)
    grp = lax.broadcasted_iota(I32, b3.shape, 1).astype(F32)
    eid = grp * GROUP_SIZE + slab
    m1 = jnp.max(b3, axis=0, keepdims=True)
    first = jnp.min(jnp.where(b3 == m1, slab, GROUP_SIZE), axis=0, keepdims=True)
    m2 = jnp.max(jnp.where(slab == first, neg, b3), axis=0, keepdims=True)
    gs = (m1 + m2)[0]
    gi = lax.broadcasted_iota(I32, gs.shape, 0).astype(F32)
    chosen = jnp.zeros(gs.shape, F32)
    for _ in range(TOPK_GROUPS):
        m = jnp.max(gs, axis=0, keepdims=True)
        hit = gi == jnp.min(jnp.where(gs == m, gi, N_GROUPS), axis=0, keepdims=True)
        chosen = jnp.where(hit, 1.0, chosen)
        gs = jnp.where(hit, neg, gs)
    cur = jnp.where((chosen > 0.0)[None], b3, neg)
    ones_cols = jnp.ones((tile, LANES), BF16)
    lanes_of = lambda a: jnp.concatenate([a] * (tile // LANES), axis=1)
    to3 = lambda a: a.reshape(GROUP_SIZE, N_GROUPS, tile)
    fold = lambda a: jnp.sum(jnp.sum(a, axis=0), axis=0, keepdims=True)
    base = jnp.zeros((N_EXPERTS, LANES), F32)
    picks, w_rows, rank_rows = [], [], []
    for _ in range(TOP_K):
        m = jnp.max(jnp.max(cur, axis=0), axis=0, keepdims=True)[None]
        pick = jnp.min(jnp.min(jnp.where(cur == m, eid, N_EXPERTS), axis=0), axis=0, keepdims=True)
        hit = eid == pick[None]
        w_rows.append(fold(jnp.where(hit, s3, 0.0)))
        picks.append(pick)
        cur = jnp.where(hit, neg, cur)
        onehot = hit.astype(BF16).reshape(N_EXPERTS, tile)
        before = jnp.dot(onehot, upper_ref[...], preferred_element_type=F32)
        rank_rows.append(fold(jnp.where(hit, to3(before + lanes_of(base)), 0.0)))
        base = base + jnp.dot(onehot, ones_cols, preferred_element_type=F32)
    n8 = jnp.floor((base + (SUBLANES - 1.0)) * (1.0 / SUBLANES)) * SUBLANES
    run_start = to3(lanes_of(_hdot(ltri_ref[...], n8)))
    slots = [rank_rows[j] + fold(jnp.where(eid == picks[j][None], run_start, 0.0)) for j in range(TOP_K)]
    n8_ref[0] = n8
    w_all = jnp.concatenate(w_rows, axis=0)
    w_ref[...] = w_all / jnp.sum(w_all, axis=0, keepdims=True) * ROUTED_SCALE
    slot_ref[...] = jnp.concatenate(slots, axis=0).astype(I32)


def _route(h2, router_w, router_bias):
    n_tok = h2.shape[0]
    tile = MOE_TILE
    n_tiles = n_tok // tile
    regroup = lambda a: a.reshape(N_GROUPS, GROUP_SIZE, -1).transpose(1, 0, 2).reshape(N_EXPERTS, -1)
    rw = regroup(router_w.T)
    bias = jnp.broadcast_to(regroup(router_bias.reshape(N_EXPERTS, 1)), (N_EXPERTS, tile))
    t_id = jnp.arange(tile, dtype=I32)
    upper = (t_id[:, None] < t_id[None, :]).astype(BF16)
    e_id = jnp.arange(N_EXPERTS, dtype=I32)
    ltri = (e_id[:, None] > e_id[None, :]).astype(F32)
    tok_blk = pl.BlockSpec((TOP_K, tile), lambda i: (0, i))
    full = lambda shape: pl.BlockSpec(shape, lambda i: (0,) * len(shape))
    slot_t, w_t, n8 = pl.pallas_call(
        _route_kernel,
        out_shape=[SDS((TOP_K, n_tok), I32), SDS((TOP_K, n_tok), F32),
                   SDS((n_tiles, N_EXPERTS, LANES), F32)],
        grid=(n_tiles,),
        in_specs=[pl.BlockSpec((tile, D_MODEL), lambda i: (i, 0)),
                  full((N_EXPERTS, D_MODEL)), full((N_EXPERTS, tile)), full((tile, tile)),
                  full((N_EXPERTS, N_EXPERTS))],
        out_specs=[tok_blk, tok_blk, pl.BlockSpec((1, N_EXPERTS, LANES), lambda i: (i, 0, 0))],
        compiler_params=_cparams(("arbitrary",)),
        name="moe_route",
    )(h2, rw, bias, upper, ltri)
    return slot_t, w_t, n8[:, :, 0].astype(I32)


def _for_each_run_piece(tile_idx, n8_ref, fn):
    for r in range(N_EXPERTS):
        n = n8_ref[tile_idx * N_EXPERTS + r]
        size = MOE_TILE
        while size >= SUBLANES:
            @pl.when((n & size) != 0)
            def _(size=size):
                fn(r, n & ~(2 * size - 1), size)
            size //= 2


def _wait_rows(total, make_wait):
    size = SUBLANES
    while size <= MOE_SLOTS:
        @pl.when((total & size) != 0)
        def _(size=size):
            make_wait(size).wait()
        size *= 2


def _pack_pairs(hi_bits, lo_bits):
    return (hi_bits & jnp.uint32(0xFFFF0000)) | (lo_bits >> 16)


def _unpack_pairs(u):
    hi = lax.bitcast_convert_type(u & jnp.uint32(0xFFFF0000), F32)
    lo = lax.bitcast_convert_type(u << 16, F32)
    return jnp.concatenate([hi, lo], axis=1).astype(BF16)


def _dispatch_kernel(n8_ref, off_ref, dst_ref, tot_ref, pad_end_ref, slot_ref, h_ref, xs_ref,
                     g_ref, zero_ref, sem):
    i = pl.program_id(0)
    tile = h_ref.shape[0]
    half = D_MODEL // 2

    @pl.when(i == 0)
    def _():
        zero_ref[...] = jnp.zeros(zero_ref.shape, U32)

        def last_block(e, back):
            start = pl.multiple_of(jnp.maximum(pad_end_ref[e] - back * MOE_ROWS, 0), MOE_ROWS)
            return pltpu.make_async_copy(zero_ref, xs_ref.at[pl.ds(start, MOE_ROWS), :], sem.at[0])

        def has_blocks(e, back):
            prev = jnp.where(e > 0, pad_end_ref[jnp.maximum(e - 1, 0)], 0)
            return pad_end_ref[e] - prev >= back * MOE_ROWS

        def clear(e, carry):
            for back in (1, 2):
                pl.when(has_blocks(e, back))(lambda back=back: last_block(e, back).start())
            return carry
        lax.fori_loop(0, N_EXPERTS, clear, 0)

        def done(e, carry):
            for back in (1, 2):
                pl.when(has_blocks(e, back))(lambda back=back: last_block(e, back).wait())
            return carry
        lax.fori_loop(0, N_EXPERTS, done, 0)

        def tail_block(b):
            start = pl.multiple_of(b * MOE_ROWS, MOE_ROWS)
            return pltpu.make_async_copy(zero_ref, xs_ref.at[pl.ds(start, MOE_ROWS), :], sem.at[0])

        def clear_tail(b, carry):
            tail_block(b).start()
            return carry

        def done_tail(b, carry):
            tail_block(b).wait()
            return carry
        used = pad_end_ref[N_EXPERTS - 1] // MOE_ROWS
        lax.fori_loop(used, xs_ref.shape[0] // MOE_ROWS, clear_tail, 0)
        lax.fori_loop(used, xs_ref.shape[0] // MOE_ROWS, done_tail, 0)

    slot_id = lax.broadcasted_iota(I32, (MOE_SLOTS, tile), 0)
    sel = jnp.zeros((MOE_SLOTS, tile), F32)
    for j in range(TOP_K):
        sel = jnp.where(slot_id == slot_ref[pl.ds(j, 1), :], 1.0, sel)
    g = jnp.dot(sel.astype(BF16), h_ref[...].astype(BF16), preferred_element_type=F32)
    bits = lax.bitcast_convert_type(g, U32)
    buf = i % 2
    g_ref[buf] = _pack_pairs(bits[:, :half], bits[:, half:])

    def drain(step):
        total = tot_ref[step]
        whole = pltpu.make_async_copy(g_ref.at[step % 2, pl.ds(0, DISPATCH_WAIT_ROWS), :],
                                      xs_ref.at[pl.ds(0, DISPATCH_WAIT_ROWS), :], sem.at[step % 2])

        def wait_whole(k, carry):
            whole.wait()
            return carry
        lax.fori_loop(0, lax.shift_right_logical(total, DISPATCH_WAIT_ROWS.bit_length() - 1), wait_whole, 0)
        _wait_rows(total & (DISPATCH_WAIT_ROWS - 1), lambda rows: pltpu.make_async_copy(
            g_ref.at[step % 2, pl.ds(0, rows), :], xs_ref.at[pl.ds(0, rows), :], sem.at[step % 2]))

    pl.when(i > 0)(lambda: drain(i - 1))

    def copy(r, k):
        src = pl.multiple_of(off_ref[i * N_EXPERTS + r] + k * DISPATCH_PIECE, SUBLANES)
        dst = pl.multiple_of(dst_ref[i * N_EXPERTS + r] + k * DISPATCH_PIECE, SUBLANES)
        pltpu.make_async_copy(g_ref.at[buf, pl.ds(src, DISPATCH_PIECE), :],
                              xs_ref.at[pl.ds(dst, DISPATCH_PIECE), :], sem.at[buf]).start()

    def per_expert(r, carry):
        n = n8_ref[i * N_EXPERTS + r]
        pl.when(n > 0)(lambda: copy(r, 0))

        @pl.when(n > DISPATCH_PIECE)
        def _():
            def more(k, c):
                copy(r, k)
                return c
            lax.fori_loop(1, (n + (DISPATCH_PIECE - 1)) // DISPATCH_PIECE, more, 0)
        return carry
    lax.fori_loop(0, N_EXPERTS, per_expert, 0, unroll=RUN_LOOP_UNROLL)
    pl.when(i == pl.num_programs(0) - 1)(lambda: drain(i))


def _dispatch(h2, slot_t, n8_flat, off_flat, dst_flat, tot, pad_end, n_rows):
    n_tok = h2.shape[0]
    tile = MOE_TILE
    grid_spec = pltpu.PrefetchScalarGridSpec(
        num_scalar_prefetch=5,
        grid=(n_tok // tile,),
        in_specs=[pl.BlockSpec((TOP_K, tile), lambda i, *_: (0, i)),
                  pl.BlockSpec((tile, D_MODEL), lambda i, *_: (i, 0))],
        out_specs=pl.BlockSpec(memory_space=pl.ANY),
        scratch_shapes=[pltpu.VMEM((2, MOE_SLOTS, D_MODEL // 2), U32),
                        pltpu.VMEM((MOE_ROWS, D_MODEL // 2), U32), pltpu.SemaphoreType.DMA((2,))],
    )
    return pl.pallas_call(
        _dispatch_kernel,
        out_shape=SDS((n_rows, D_MODEL // 2), U32),
        grid_spec=grid_spec,
        compiler_params=_cparams(("arbitrary",)),
        name="moe_dispatch",
    )(n8_flat, off_flat, dst_flat, tot, pad_end, slot_t, h2)


def _expert_kernel(first_ref, nblk_ref, exp_of_ref, xs_ref, w1_ref, w3_ref, w2_ref, y_ref,
                   xbuf, ybuf, wb1, wb3, wb2, in_sem, out_sem):
    del exp_of_ref
    r = pl.program_id(0)
    half = D_MODEL // 2
    used = first_ref[N_EXPERTS - 1] + nblk_ref[N_EXPERTS - 1]
    part = MOE_ROWS // EXPERT_DMA_PARTS

    class _Copies:
        def __init__(self, make):
            self.parts = [make(p) for p in range(EXPERT_DMA_PARTS)]

        def start(self):
            for c in self.parts:
                c.start()

        def wait(self):
            for c in self.parts:
                c.wait()

    def hbm_rows(g, p):
        return pl.ds(pl.multiple_of(g * MOE_ROWS + p * part, part), part)

    in_copy = lambda g, slot: _Copies(lambda p: pltpu.make_async_copy(
        xs_ref.at[hbm_rows(g, p), :], xbuf.at[slot, pl.ds(p * part, part), :], in_sem.at[slot]))
    out_copy = lambda g, slot: _Copies(lambda p: pltpu.make_async_copy(
        ybuf.at[slot, pl.ds(p * part, part), :], y_ref.at[hbm_rows(g, p), :], out_sem.at[slot]))

    @pl.when(r == 0)
    def _():
        for g in range(EXPERT_BUFS - 1):
            pl.when(g < used)(lambda g=g: in_copy(g, g).start())

    wb1[...] = w1_ref[0, 0].astype(BF16)
    wb3[...] = w3_ref[0, 0].astype(BF16)
    wb2[...] = w2_ref[0, 0].astype(BF16)

    def block(k, carry):
        g = first_ref[r] + k
        slot = g % EXPERT_BUFS
        in_copy(g, slot).wait()
        ahead = g + (EXPERT_BUFS - 1)
        pl.when(ahead < used)(lambda: in_copy(ahead, ahead % EXPERT_BUFS).start())
        pl.when(g >= EXPERT_BUFS)(lambda: out_copy(g - EXPERT_BUFS, slot).wait())
        dot = lambda a, b: jnp.dot(a, b, preferred_element_type=F32)
        x = _unpack_pairs(xbuf[slot])
        hid = (jax.nn.silu(dot(x, wb1[...])) * dot(x, wb3[...])).astype(BF16)
        y = dot(hid, wb2[...]).astype(BF16).astype(F32)
        bits = lax.bitcast_convert_type(y, U32)
        ybuf[slot] = _pack_pairs(bits[:, :half], bits[:, half:])
        out_copy(g, slot).start()
        return carry
    lax.fori_loop(0, nblk_ref[r], block, 0)

    @pl.when(r == N_EXPERTS - 1)
    def _():
        for back in range(EXPERT_BUFS, 0, -1):
            pl.when(used >= back)(
                lambda back=back: out_copy(used - back, (used - back) % EXPERT_BUFS).wait())

        ybuf[0] = jnp.zeros(ybuf.shape[1:], U32)
        n_blocks = y_ref.shape[0] // MOE_ROWS

        def clear(g, carry):
            out_copy(g, 0).start()
            return carry

        def done(g, carry):
            out_copy(g, 0).wait()
            return carry
        lax.fori_loop(used, n_blocks, clear, 0)
        lax.fori_loop(used, n_blocks, done, 0)


def _experts(xs, first_block, n_block, exp_of_row, w1, w3, w2, layer):
    n_rows = xs.shape[0]
    half = D_MODEL // 2
    w_in_blk = pl.BlockSpec((1, 1, D_MODEL, EXPERT_FF), lambda r, first, nblk, eo: (layer, eo[r], 0, 0))
    grid_spec = pltpu.PrefetchScalarGridSpec(
        num_scalar_prefetch=3,
        grid=(N_EXPERTS,),
        in_specs=[pl.BlockSpec(memory_space=pl.ANY), w_in_blk, w_in_blk,
                  pl.BlockSpec((1, 1, EXPERT_FF, D_MODEL), lambda r, first, nblk, eo: (layer, eo[r], 0, 0))],
        out_specs=pl.BlockSpec(memory_space=pl.ANY),
        scratch_shapes=[pltpu.VMEM((EXPERT_BUFS, MOE_ROWS, half), U32),
                        pltpu.VMEM((EXPERT_BUFS, MOE_ROWS, half), U32),
                        pltpu.VMEM((D_MODEL, EXPERT_FF), BF16), pltpu.VMEM((D_MODEL, EXPERT_FF), BF16),
                        pltpu.VMEM((EXPERT_FF, D_MODEL), BF16),
                        pltpu.SemaphoreType.DMA((EXPERT_BUFS,)), pltpu.SemaphoreType.DMA((EXPERT_BUFS,))],
    )
    return pl.pallas_call(
        _expert_kernel,
        out_shape=SDS((n_rows, half), U32),
        grid_spec=grid_spec,
        compiler_params=_cparams(("arbitrary",)),
        name="moe_experts",
    )(first_block, n_block, exp_of_row, xs, w1, w3, w2)


def _combine_kernel(n8_ref, off_ref, src_ref, tot_ref, y_ref, slot_ref, w_ref, h_ref, x_ref, g2_ref,
                    s1_ref, s3_ref, s2_ref, fn_ref, o_ref, yt_ref, sem, *, final):
    tile = h_ref.shape[1]
    i = pl.program_id(0) * pl.num_programs(1) + pl.program_id(1)

    @pl.when(i == 0)
    def _():
        yt_ref[...] = jnp.zeros(yt_ref.shape, U32)

    def piece(r, offset, rows):
        src = pl.multiple_of(src_ref[i * N_EXPERTS + r] + offset, SUBLANES)
        dst = pl.multiple_of(off_ref[i * N_EXPERTS + r] + offset, SUBLANES)
        pltpu.make_async_copy(y_ref.at[pl.ds(src, rows), :], yt_ref.at[pl.ds(dst, rows), :], sem).start()

    _for_each_run_piece(i, n8_ref, piece)
    h = h_ref[0]
    shared = _bdot(jax.nn.silu(_bdot(h, s1_ref[...])) * _bdot(h, s3_ref[...]), s2_ref[...])
    slot_id = lax.broadcasted_iota(I32, (tile, MOE_SLOTS), 1)
    slots = slot_ref[0]
    w = w_ref[0]
    pw = jnp.zeros((tile, MOE_SLOTS), F32)
    for j in range(TOP_K):
        pw = jnp.where(slot_id == slots[:, j:j + 1], w[:, j:j + 1], pw)
    _wait_rows(tot_ref[i], lambda rows: pltpu.make_async_copy(
        y_ref.at[pl.ds(0, rows), :], yt_ref.at[pl.ds(0, rows), :], sem))
    routed = jnp.dot(pw.astype(BF16), _unpack_pairs(yt_ref[...]), preferred_element_type=F32)
    xn = x_ref[0] + g2_ref[0] * (routed + shared)
    if final:
        xn = xn * lax.rsqrt(jnp.mean(xn * xn, axis=-1, keepdims=True) + NORM_EPS) * fn_ref[...]
    o_ref[0] = xn


def _combine(y, slot_nat, w_nat, n8_flat, off_flat, dst_flat, tot, h2, x, g2, s1_bf, s3_bf, s2_bf,
             final_norm, final):
    bsz, seq, _ = x.shape
    tile = MOE_TILE
    per_seq = seq // tile
    blk = pl.BlockSpec((1, tile, D_MODEL), lambda b, i, *_: (b, i, 0))
    tok = pl.BlockSpec((1, tile, TOP_K), lambda b, i, *_: (b, i, 0))
    full = lambda shape: pl.BlockSpec(shape, lambda b, i, *_: (0,) * len(shape))
    grid_spec = pltpu.PrefetchScalarGridSpec(
        num_scalar_prefetch=4,
        grid=(bsz, per_seq),
        in_specs=[pl.BlockSpec(memory_space=pl.ANY), tok, tok, blk, blk,
                  pl.BlockSpec((1, 1, D_MODEL), lambda b, i, *_: (b, 0, 0)),
                  full((D_MODEL, SHARED_FF)), full((D_MODEL, SHARED_FF)), full((SHARED_FF, D_MODEL)),
                  full((1, D_MODEL))],
        out_specs=blk,
        scratch_shapes=[pltpu.VMEM((MOE_SLOTS, D_MODEL // 2), U32), pltpu.SemaphoreType.DMA],
    )
    return pl.pallas_call(
        functools.partial(_combine_kernel, final=final),
        out_shape=SDS((bsz, seq, D_MODEL), F32),
        grid_spec=grid_spec,
        compiler_params=_cparams(("arbitrary", "arbitrary")),
        name="moe_combine",
    )(n8_flat, off_flat, dst_flat, tot, y, slot_nat, w_nat, h2, x, g2, s1_bf, s3_bf, s2_bf, final_norm)


def _moe(x, h2, g2, router_w, router_bias, w1, w3, w2, layer, s1_bf, s3_bf, s2_bf, final_norm, final):
    bsz, seq, _ = x.shape
    assert seq % MOE_TILE == 0, "token tiles must not straddle sequences"
    n_tok = bsz * seq
    slot_t, w_t, n8 = _route(h2.reshape(n_tok, D_MODEL), router_w, router_bias)
    n_tiles = n8.shape[0]
    counts = jnp.sum(n8, axis=0)
    padded = jnp.where(counts > 0, (counts + DISPATCH_SPARE + MOE_ROWS - 1) // MOE_ROWS * MOE_ROWS, 0)
    pad_end = jnp.cumsum(padded).astype(I32)
    pad_start = pad_end - padded
    run_row = (pad_start[None, :] + jnp.cumsum(n8, axis=0) - n8).astype(I32)
    run_slot = (jnp.cumsum(n8, axis=1) - n8).astype(I32)
    tot = jnp.sum(n8, axis=1).astype(I32)
    copied = jnp.sum((n8 + DISPATCH_PIECE - 1) // DISPATCH_PIECE * DISPATCH_PIECE, axis=1).astype(I32)
    max_rows = (n_tok * TOP_K + n_tiles * N_EXPERTS * (SUBLANES - 1)
                + N_EXPERTS * (DISPATCH_SPARE + MOE_ROWS - 1))
    n_blocks = (max_rows + MOE_ROWS - 1) // MOE_ROWS
    row_id = jnp.arange(N_EXPERTS, dtype=I32)
    exp_of_row = (row_id % N_GROUPS) * GROUP_SIZE + row_id // N_GROUPS
    flat = lambda a: a.reshape(-1).astype(I32)
    xs = _dispatch(h2.reshape(n_tok, D_MODEL), slot_t, flat(n8), flat(run_slot), flat(run_row), copied,
                   pad_end, n_blocks * MOE_ROWS)
    y = _experts(xs, (pad_start // MOE_ROWS).astype(I32), (padded // MOE_ROWS).astype(I32), exp_of_row,
                 w1, w3, w2, layer)
    nat = lambda a: a.T.reshape(bsz, seq, TOP_K)
    return _combine(y, nat(slot_t), nat(w_t), flat(n8), flat(run_slot), flat(run_row), tot, h2, x, g2,
                    s1_bf, s3_bf, s2_bf, final_norm, final)


def _pad_cols(a, width):
    return jnp.pad(a, ((0, 0), (0, width - a.shape[1])))


def _layout_w_in(w_in):
    b0 = COLS_A
    c0 = COLS_A + COLS_B
    g0 = c0 + COLS_C
    seg_b = _pad_cols(w_in[:, b0:c0], SEG_G - SEG_B)
    return jnp.concatenate([w_in[:, c0:g0], seg_b, w_in[:, g0:], w_in[:, :b0]], axis=1).astype(BF16)


def _layout_mu(mu):
    return _pad_cols(mu.reshape(1, -1), SEG_G - SEG_B)


def _pad_rows(a, height):
    return jnp.pad(a, ((0, height - a.shape[0]), (0, 0)))


def kernel(x, c, ada_w, ada_b, norm1, norm2, w_in, conv_a_w, conv_a_b, ln_a_g, ln_a_b, proj_a, mu_b, w0, w_up, a0, a_up, g_up, k_k, k_a, r_k, gn_b_g, gn_b_b, proj_b, conv_c_w, conv_c_b, lru_wa, lru_ba, lru_wx, lru_bx, lru_lambda, proj_c, w_out, router_w, router_bias, exp_w1, exp_w3, exp_w2, sh_w1, sh_w3, sh_w2, final_norm):
    depth = ada_w.shape[0]
    bsz = x.shape[0]
    mod = _ada_mod(c, ada_w, ada_b)
    for l in range(depth):
        sh1, sc1, g1, sh2, sc2, g2 = [mod[l, :, i * D_MODEL:(i + 1) * D_MODEL].reshape(bsz, 1, D_MODEL)
                                      for i in range(N_MOD)]
        p = _in_proj(x, sc1, sh1, norm1[l].reshape(1, -1), _layout_w_in(w_in[l]))
        merged = _conv_a(p, conv_a_w[l], conv_a_b[l], ln_a_g[l], ln_a_b[l], proj_a[l].astype(BF16))
        merged = _rwkv(p, merged, _layout_mu(mu_b[l]), w0[l], _pad_rows(w_up[l], LANES).astype(BF16),
                       a0[l], jnp.pad(a_up[l], ((LORA_W, 0), (0, 0))).astype(BF16), g_up[l].astype(BF16),
                       k_k[l], k_a[l], r_k[l], gn_b_g[l], gn_b_b[l], proj_b[l].astype(BF16))
        merged = _lru(p, merged, conv_c_w[l], conv_c_b[l], lru_wa[l].astype(BF16), lru_ba[l],
                      lru_wx[l].astype(BF16), lru_bx[l], lru_lambda[l], proj_c[l].astype(BF16))
        x, h2 = _out_proj(merged, x, g1, w_out[l].astype(BF16), norm2[l].reshape(1, -1), sc2, sh2)
        x = _moe(x, h2, g2, router_w[l], router_bias[l], exp_w1, exp_w3, exp_w2, l,
                 sh_w1[l].astype(BF16), sh_w3[l].astype(BF16), sh_w2[l].astype(BF16),
                 final_norm.reshape(1, -1), final=(l == depth - 1))
    return x
```

```python
import functools

import jax
import jax.numpy as jnp
from jax import lax
from jax.experimental import pallas as pl
from jax.experimental.pallas import tpu as pltpu

F32 = jnp.float32
BF16 = jnp.bfloat16
I32 = jnp.int32
SDS = jax.ShapeDtypeStruct
HIGHEST = lax.Precision.HIGHEST

D_MODEL = 1024
N_MOD = 6
NORM_EPS = 1e-6
CONV_A_CH = 512
CONV_A_WIDTH = 31
CONV_A_LN_EPS = 1e-5
RWKV_HEADS = 8
RWKV_HEAD = 64
RWKV_DIM = RWKV_HEADS * RWKV_HEAD
LORA_W = 64
LORA_A = 64
LORA_G = 128
RWKV_GN_EPS = 64e-5
RWKV_CHUNK = 64
LRU_DIM = 1024
LRU_HEADS = 8
LRU_BLOCK = LRU_DIM // LRU_HEADS
LRU_CONV = 4
LRU_C = 8.0
N_EXPERTS = 64
TOP_K = 8
N_GROUPS = 8
GROUP_SIZE = N_EXPERTS // N_GROUPS
TOPK_GROUPS = 4
EXPERT_FF = 256
SHARED_FF = 256
ROUTED_SCALE = 2.5
SEG_C = 0
SEG_B = 2048
SEG_G = 4096
SEG_A = 7168
IN_COLS_PAD = 8192
RW_R, RW_K, RW_V, RW_XWA, RW_XG = 0, 512, 1024, 1536, 1664
COLS_A = 2 * CONV_A_CH
COLS_B = 3 * RWKV_DIM + LORA_W + LORA_A + LORA_G
COLS_C = 2 * LRU_DIM
VMEM_LIMIT = 56 * 1024 * 1024
MERGED_DTYPE = BF16
SUBLANES = 8
LANES = 128
MOE_ROWS = 512
EXPERT_DMA_PARTS = 4
EXPERT_BUFS = 4


def _cparams(sem):
    return pltpu.CompilerParams(dimension_semantics=sem, vmem_limit_bytes=VMEM_LIMIT)


def _bdot(a, b):
    return jnp.dot(a.astype(BF16), b.astype(BF16), preferred_element_type=F32)


def _hdot(a, b):
    return jnp.dot(a, b, preferred_element_type=F32, precision=HIGHEST)


def _split(a):
    hi = a.astype(BF16)
    return hi, (a - hi.astype(F32)).astype(BF16)


def _head_sums(a, ones_bf):
    hi, lo = _split(a)
    return (jnp.dot(hi, ones_bf, preferred_element_type=F32)
            + jnp.dot(lo, ones_bf, preferred_element_type=F32))


def _hdot_nt(a, b):
    return lax.dot_general(a, b, (((1,), (1,)), ((), ())), preferred_element_type=F32,
                           precision=HIGHEST)


def _ada_kernel(c_ref, w_ref, b_ref, o_ref):
    cond = jax.nn.silu(c_ref[...])
    o_ref[0] = _bdot(cond, w_ref[0]) + b_ref[0]


def _ada_mod(c, ada_w, ada_b):
    depth, _, n = ada_w.shape
    bsz = c.shape[0]
    tn = 1536
    return pl.pallas_call(
        _ada_kernel,
        out_shape=SDS((depth, bsz, n), F32),
        grid=(depth, n // tn),
        in_specs=[pl.BlockSpec((bsz, D_MODEL), lambda l, j: (0, 0)),
                  pl.BlockSpec((1, D_MODEL, tn), lambda l, j: (l, 0, j)),
                  pl.BlockSpec((1, 1, tn), lambda l, j: (l, 0, j))],
        out_specs=pl.BlockSpec((1, bsz, tn), lambda l, j: (l, 0, j)),
        compiler_params=_cparams(("arbitrary", "arbitrary")),
        name="ada_mod",
    )(c, ada_w, ada_b.reshape(depth, 1, n))


def _modulated_rmsnorm(x, g, sc, sh):
    y = x * lax.rsqrt(jnp.mean(x * x, axis=-1, keepdims=True) + NORM_EPS)
    return (y * g) * (1.0 + sc) + sh


IN_PROJ_SUB = 512


def _in_kernel(x_ref, sc_ref, sh_ref, g_ref, w_ref, o_ref, h_ref):
    @pl.when(pl.program_id(2) == 0)
    def _():
        h_ref[...] = _modulated_rmsnorm(x_ref[0], g_ref[...], sc_ref[0], sh_ref[0]).astype(BF16)

    j = pl.program_id(2)
    ts, tn = o_ref.shape[1], o_ref.shape[2]
    is_gelu = j == SEG_C // tn
    is_gate = (j >= SEG_G // tn) & (j < SEG_A // tn)
    is_glu = j == SEG_A // tn

    def glu(acc):
        u = acc[:, :tn // 2] * jax.nn.sigmoid(acc[:, tn // 2:])
        return jnp.concatenate([u, jnp.zeros_like(u)], axis=1)

    def emit(act, sub):
        for r0 in range(0, ts, sub):
            rows = pl.ds(r0, sub)
            acc = jnp.dot(h_ref[rows, :], w_ref[...], preferred_element_type=F32)
            o_ref[0, rows, :] = act(acc).astype(o_ref.dtype)

    sub = min(ts, IN_PROJ_SUB)
    pl.when(is_gelu)(lambda: emit(functools.partial(jax.nn.gelu, approximate=True), sub))
    pl.when(is_gate)(lambda: emit(jax.nn.sigmoid, sub))
    pl.when(is_glu)(lambda: emit(glu, sub))
    pl.when(jnp.logical_not(is_gelu | is_gate | is_glu))(lambda: emit(lambda acc: acc, ts))


def _in_proj(x, sc, sh, g, w_pad):
    bsz, seq, _ = x.shape
    ts = min(seq, 1024)
    tn = 1024
    return pl.pallas_call(
        _in_kernel,
        out_shape=SDS((bsz, seq, IN_COLS_PAD), BF16),
        grid=(bsz, seq // ts, IN_COLS_PAD // tn),
        in_specs=[pl.BlockSpec((1, ts, D_MODEL), lambda b, i, j: (b, i, 0)),
                  pl.BlockSpec((1, 1, D_MODEL), lambda b, i, j: (b, 0, 0)),
                  pl.BlockSpec((1, 1, D_MODEL), lambda b, i, j: (b, 0, 0)),
                  pl.BlockSpec((1, D_MODEL), lambda b, i, j: (0, 0)),
                  pl.BlockSpec((D_MODEL, tn), lambda b, i, j: (0, j))],
        out_specs=pl.BlockSpec((1, ts, tn), lambda b, i, j: (b, i, j)),
        scratch_shapes=[pltpu.VMEM((ts, D_MODEL), BF16)],
        compiler_params=_cparams(("arbitrary", "arbitrary", "arbitrary")),
        name="in_proj",
    )(x, sc, sh, g, w_pad)


CONV_A_HALO = 32
CONV_A_SUB = 64


def _conv_a_kernel(pa_ref, pg_ref, cw_ref, cb_ref, lg_ref, lb_ref, pj_ref, o_ref, ext_ref, sh_ref, y_ref):
    ts = pa_ref.shape[1]

    @pl.when(pl.program_id(1) == 0)
    def _():
        ext_ref[pl.ds(0, CONV_A_HALO), :] = jnp.zeros((CONV_A_HALO, CONV_A_CH), F32)

    ext_ref[pl.ds(CONV_A_HALO, ts), :] = pa_ref[0].astype(F32)
    for p in range(1, SUBLANES):
        sh_ref[p - 1] = ext_ref[pl.ds(p, sh_ref.shape[1]), :]
    first = CONV_A_HALO - (CONV_A_WIDTH - 1)
    for r0 in range(0, ts, CONV_A_SUB):
        acc = jnp.zeros((CONV_A_SUB, CONV_A_CH), F32) + cb_ref[...]
        for j in range(CONV_A_WIDTH):
            phase = (first + j) % SUBLANES
            rows = pl.ds(r0 + first + j - phase, CONV_A_SUB)
            tap = ext_ref[rows, :] if phase == 0 else sh_ref[phase - 1, rows, :]
            acc = acc + tap * cw_ref[pl.ds(j, 1), :]
        y_ref[pl.ds(r0, CONV_A_SUB), :] = acc
    ext_ref[pl.ds(0, CONV_A_HALO), :] = ext_ref[pl.ds(ts, CONV_A_HALO), :]
    y = y_ref[...]
    mu = jnp.mean(y, axis=-1, keepdims=True)
    d = y - mu
    var = jnp.mean(d * d, axis=-1, keepdims=True)
    yn = d * lax.rsqrt(var + CONV_A_LN_EPS) * lg_ref[...] + lb_ref[...]
    o = _bdot(jax.nn.silu(yn), pj_ref[...])
    o_ref[0] = (pg_ref[0].astype(F32) * o).astype(o_ref.dtype)


def _conv_a(p, conv_w, conv_b, ln_g, ln_b, proj_bf):
    bsz, seq, _ = p.shape
    ts = min(seq, 512)
    row = lambda a: a.reshape(1, -1)
    full = lambda shape: pl.BlockSpec(shape, lambda b, i: (0,) * len(shape))
    return pl.pallas_call(
        _conv_a_kernel,
        out_shape=SDS((bsz, seq, D_MODEL), MERGED_DTYPE),
        grid=(bsz, seq // ts),
        in_specs=[pl.BlockSpec((1, ts, CONV_A_CH), lambda b, i: (b, i, SEG_A // CONV_A_CH)),
                  pl.BlockSpec((1, ts, D_MODEL), lambda b, i: (b, i, SEG_G // D_MODEL)),
                  full((CONV_A_WIDTH, CONV_A_CH)), full((1, CONV_A_CH)), full((1, CONV_A_CH)),
                  full((1, CONV_A_CH)), full((CONV_A_CH, D_MODEL))],
        out_specs=pl.BlockSpec((1, ts, D_MODEL), lambda b, i: (b, i, 0)),
        scratch_shapes=[pltpu.VMEM((ts + CONV_A_HALO, CONV_A_CH), F32),
                        pltpu.VMEM((SUBLANES - 1, ts + CONV_A_HALO - SUBLANES, CONV_A_CH), F32),
                        pltpu.VMEM((ts, CONV_A_CH), F32)],
        compiler_params=_cparams(("arbitrary", "arbitrary")),
        name="conv_a",
    )(p, p, conv_w, row(conv_b), row(ln_g), row(ln_b), proj_bf)


def _lru_kernel(pc_ref, pg_ref, m_ref, cw_ref, cb_ref, wa_ref, ba_ref, wx_ref, bx_ref, lam_ref,
                pj_ref, o_ref, ext_ref, h_ref, a_ref, b_ref):
    ts = pc_ref.shape[1]
    groups = ts // SUBLANES

    @pl.when(pl.program_id(1) == 0)
    def _():
        ext_ref[pl.ds(0, SUBLANES), :] = jnp.zeros((SUBLANES, LRU_DIM), F32)
        h_ref[...] = jnp.zeros((SUBLANES, LRU_DIM), F32)

    pc = pc_ref[0].astype(F32)
    y_gate = pc[:, :LRU_DIM]
    ext_ref[pl.ds(SUBLANES, ts), :] = pc[:, LRU_DIM:]
    first = SUBLANES - (LRU_CONV - 1)
    xc = jnp.zeros((ts, LRU_DIM), F32) + cb_ref[...]
    for j in range(LRU_CONV):
        xc = xc + ext_ref[pl.ds(first + j, ts), :] * cw_ref[pl.ds(j, 1), :]
    ext_ref[pl.ds(0, SUBLANES), :] = ext_ref[pl.ds(ts, SUBLANES), :]

    def block_diag(w_ref):
        return jnp.concatenate(
            [_bdot(xc[:, h * LRU_BLOCK:(h + 1) * LRU_BLOCK], w_ref[h]) for h in range(LRU_HEADS)],
            axis=1)

    gate_a = jax.nn.sigmoid(block_diag(wa_ref) + ba_ref[...])
    gate_x = jax.nn.sigmoid(block_diag(wx_ref) + bx_ref[...])
    log_a = -LRU_C * gate_a * jax.nn.softplus(-lam_ref[...])
    a = jnp.exp(log_a)
    b = xc * gate_x * jnp.sqrt(1.0 - jnp.exp(2.0 * log_a))

    a3 = a.reshape(groups, SUBLANES, LRU_DIM)
    b3 = b.reshape(groups, SUBLANES, LRU_DIM)
    row = lax.broadcasted_iota(I32, (groups, SUBLANES, LRU_DIM), 1)
    for s in (1, 2, 4):
        keep = row >= s
        b3 = jnp.where(keep, a3 * pltpu.roll(b3, s, axis=1) + b3, b3)
        a3 = jnp.where(keep, a3 * pltpu.roll(a3, s, axis=1), a3)
    a_ref[...] = a3.reshape(ts, LRU_DIM)
    b_ref[...] = b3.reshape(ts, LRU_DIM)
    h = h_ref[...]
    for g in range(groups):
        rows = pl.ds(g * SUBLANES, SUBLANES)
        hg = a_ref[rows, :] * h + b_ref[rows, :]
        b_ref[rows, :] = hg
        h = jnp.broadcast_to(hg[SUBLANES - 1:SUBLANES, :], (SUBLANES, LRU_DIM))
    h_ref[...] = h
    o = _bdot(b_ref[...] * y_gate, pj_ref[...])
    o_ref[0] = (m_ref[0].astype(F32) + pg_ref[0].astype(F32) * o).astype(o_ref.dtype)


def _lru(p, merged, conv_w, conv_b, wa_bf, ba, wx_bf, bx, lam, proj_bf):
    bsz, seq, _ = p.shape
    ts = min(seq, 256)
    row = lambda a: a.reshape(1, -1)
    full = lambda shape: pl.BlockSpec(shape, lambda b, i: (0,) * len(shape))
    return pl.pallas_call(
        _lru_kernel,
        out_shape=SDS((bsz, seq, D_MODEL), MERGED_DTYPE),
        grid=(bsz, seq // ts),
        in_specs=[pl.BlockSpec((1, ts, 2 * LRU_DIM), lambda b, i: (b, i, SEG_C // (2 * LRU_DIM))),
                  pl.BlockSpec((1, ts, D_MODEL), lambda b, i: (b, i, SEG_G // D_MODEL + 2)),
                  pl.BlockSpec((1, ts, D_MODEL), lambda b, i: (b, i, 0)),
                  full((LRU_CONV, LRU_DIM)), full((1, LRU_DIM)),
                  full((LRU_HEADS, LRU_BLOCK, LRU_BLOCK)), full((1, LRU_DIM)),
                  full((LRU_HEADS, LRU_BLOCK, LRU_BLOCK)), full((1, LRU_DIM)),
                  full((1, LRU_DIM)), full((LRU_DIM, D_MODEL))],
        out_specs=pl.BlockSpec((1, ts, D_MODEL), lambda b, i: (b, i, 0)),
        scratch_shapes=[pltpu.VMEM((ts + SUBLANES, LRU_DIM), F32),
                        pltpu.VMEM((SUBLANES, LRU_DIM), F32),
                        pltpu.VMEM((ts, LRU_DIM), F32),
                        pltpu.VMEM((ts, LRU_DIM), F32)],
        compiler_params=_cparams(("arbitrary", "arbitrary")),
        name="rg_lru",
    )(p, p, merged, conv_w, row(conv_b), wa_bf, row(ba), wx_bf, row(bx), row(lam), proj_bf)


def _rwkv_prep_kernel(pb_ref, mu_ref, w0_ref, wup_ref, a0_ref, aup_ref, gup_ref, kk_ref, ka_ref,
                      rk_ref, bd_ref, ltri_ref,
                      rt_ref, kkt_ref, kh_ref, bh_ref, v_ref, pinc_ref, bonus_ref, g_ref, ext_ref):
    ts = pb_ref.shape[1]

    @pl.when(pl.program_id(1) == 0)
    def _():
        ext_ref[pl.ds(0, SUBLANES), :] = jnp.zeros((SUBLANES, ext_ref.shape[1]), F32)

    p = pb_ref[0].astype(F32)
    ext_ref[pl.ds(SUBLANES, ts), :] = p
    prev = ext_ref[pl.ds(SUBLANES - 1, ts), :]
    ext_ref[pl.ds(0, SUBLANES), :] = ext_ref[pl.ds(ts, SUBLANES), :]
    pm = p + (prev - p) * mu_ref[...]
    r = pm[:, RW_R:RW_R + RWKV_DIM]
    k = pm[:, RW_K:RW_K + RWKV_DIM]
    v = pm[:, RW_V:RW_V + RWKV_DIM]
    xwa = pm[:, RW_XWA:RW_XWA + LANES]
    xg = pm[:, RW_XG:RW_XG + LORA_G]
    w = -jax.nn.softplus(-(w0_ref[...] + _bdot(jnp.tanh(xwa), wup_ref[...]))) - 0.5
    lw = -jnp.exp(w)
    a = jax.nn.sigmoid(a0_ref[...] + _bdot(xwa, aup_ref[...]))
    g_ref[0] = _bdot(jax.nn.sigmoid(xg), gup_ref[...])
    kkr = k * kk_ref[...]
    ss = _head_sums(kkr * kkr, bd_ref[...])
    kk = kkr / jnp.maximum(jnp.sqrt(ss), 1e-12)
    k2 = k * (1.0 + (a - 1.0) * ka_ref[...])
    lw_hi = lw.astype(BF16)
    lw_mid, lw_lo = _split(lw - lw_hi.astype(F32))
    tri = ltri_ref[...]
    lcum = (jnp.dot(tri, lw_hi, preferred_element_type=F32) + jnp.dot(tri, lw_mid, preferred_element_type=F32)
            + jnp.dot(tri, lw_lo, preferred_element_type=F32))
    pinc = jnp.exp(lcum)
    pinv = jnp.exp(-lcum)
    rt_ref[0] = r * pinc
    kkt_ref[0] = kk * jnp.exp(lcum - lw)
    kh_ref[0] = k2 * pinv
    bh_ref[0] = kk * a * pinv
    v_ref[0] = v
    pinc_ref[0] = pinc
    bonus_ref[0] = _head_sums(r * k2 * rk_ref[...], bd_ref[...]) * v


def _dot3(a, b):
    d = lambda x, y: jnp.dot(x, y, preferred_element_type=F32)
    m = a[0].shape[0]
    both = d(jnp.concatenate([a[0], a[1]], axis=0), b[0])
    return both[:m] + both[m:] + d(a[0], b[1])


def _rwkv_scan_kernel(rt_ref, kkt_ref, kh_ref, bh_ref, v_ref, pinc_ref, y_ref, s_ref):
    c = RWKV_CHUNK
    n = RWKV_HEAD
    nb = rt_ref.shape[0]
    heads = range(nb * RWKV_HEADS)

    @pl.when(pl.program_id(1) == 0)
    def _():
        s_ref[...] = jnp.zeros(s_ref.shape, F32)

    row = lax.broadcasted_iota(I32, (c, c), 0)
    col = lax.broadcasted_iota(I32, (c, c), 1)
    eye = (row == col).astype(F32)
    same16 = (row // 16) == (col // 16)
    same32 = (row // 32) == (col // 32)
    row2 = lax.broadcasted_iota(I32, (c, 2 * c), 0)
    col2 = lax.broadcasted_iota(I32, (c, 2 * c), 1) % c
    nt = lambda a, b: lax.dot_general(a, b, (((1,), (1,)), ((), ())), preferred_element_type=F32)
    tn = lambda a, b: lax.dot_general(a, b, (((0,), (0,)), ((), ())), preferred_element_type=F32)
    dot = lambda a, b: jnp.dot(a, b, preferred_element_type=F32)
    sl = [pl.ds((h % RWKV_HEADS) * n, n) for h in heads]
    sq = [h // RWKV_HEADS for h in heads]
    v = [v_ref[sq[h], :, sl[h]] for h in heads]
    pc = [pinc_ref[sq[h], pl.ds(c - 1, 1), sl[h]] for h in heads]
    s = [s_ref[sq[h], :, sl[h]] for h in heads]
    lhs = [jnp.concatenate([kkt_ref[sq[h], :, sl[h]], rt_ref[sq[h], :, sl[h]]], axis=0) for h in heads]
    rhs = [jnp.concatenate([bh_ref[sq[h], :, sl[h]], kh_ref[sq[h], :, sl[h]]], axis=0) for h in heads]
    big = [nt(lhs[h], rhs[h]) for h in heads]
    from_state = [nt(lhs[h], s[h]) for h in heads]
    top = [jnp.where(row2 > col2, big[h][:c], 0.0) for h in heads]
    bot = [jnp.where(row2 >= col2, big[h][c:], 0.0) for h in heads]
    a_b = [top[h][:, :c] for h in heads]
    akv = [dot(top[h], jnp.concatenate([jnp.zeros((c, n), F32), v[h]], axis=0)) for h in heads]
    d16 = [jnp.where(same16, a_b[h], 0.0) for h in heads]
    sd = [_split(d16[h]) for h in heads]
    s2 = [_split(_dot3(sd[h], sd[h])) for h in heads]
    s4 = [_split(_dot3(s2[h], s2[h])) for h in heads]
    s8 = [_split(_dot3(s4[h], s4[h])) for h in heads]
    t = [eye - d16[h] for h in heads]
    for sp in (s2, s4, s8):
        t = [t[h] + _dot3(_split(t[h]), sp[h]) for h in heads]
    for off in ([jnp.where(same32 & jnp.logical_not(same16), a_b[h], 0.0) for h in heads],
                [jnp.where(same32, 0.0, a_b[h]) for h in heads]):
        tb = [t[h].astype(BF16) for h in heads]
        lt = [dot(off[h].astype(BF16), tb[h]).astype(BF16) for h in heads]
        t = [t[h] - dot(tb[h], lt[h]) for h in heads]
    u = [dot(t[h], from_state[h][:c] + akv[h]) for h in heads]
    vu = [jnp.concatenate([-u[h], v[h]], axis=0) for h in heads]
    y = [from_state[h][c:] + dot(bot[h], vu[h]) for h in heads]
    s_new = [s[h] * pc[h] + tn(vu[h], rhs[h] * pc[h]) for h in heads]
    for q in range(nb):
        mine = slice(q * RWKV_HEADS, (q + 1) * RWKV_HEADS)
        y_ref[q] = jnp.concatenate(y[mine], axis=1)
        s_ref[q] = jnp.concatenate(s_new[mine], axis=1)


def _rwkv_post_kernel(y_ref, bonus_ref, g_ref, pg_ref, m_ref, gg_ref, gb_ref, bdm_ref, pj_ref, o_ref):
    y = y_ref[0]
    mu = _head_sums(y, bdm_ref[...])
    d = y - mu
    var = _head_sums(d * d, bdm_ref[...])
    yn = d * lax.rsqrt(var + RWKV_GN_EPS) * gg_ref[...] + gb_ref[...] + bonus_ref[0]
    o = _bdot(yn * g_ref[0], pj_ref[...])
    o_ref[0] = (m_ref[0].astype(F32) + pg_ref[0].astype(F32) * o).astype(o_ref.dtype)


def _rwkv(p, merged, mu_pad, w0, wup_pad, a0, aup_pad, g_up, k_k, k_a, r_k, gn_g, gn_b, proj_bf):
    bsz, seq, _ = p.shape
    row = lambda a: a.reshape(1, -1)
    full = lambda shape: pl.BlockSpec(shape, lambda b, i: (0,) * len(shape))
    head_id = jnp.arange(RWKV_DIM, dtype=I32) // RWKV_HEAD
    bd = (head_id[:, None] == head_id[None, :]).astype(BF16)

    ts = min(seq, 256)
    t_id = jnp.arange(ts, dtype=I32)
    ltri = ((t_id[:, None] // RWKV_CHUNK == t_id[None, :] // RWKV_CHUNK)
            & (t_id[:, None] >= t_id[None, :])).astype(BF16)
    seq_blk = lambda width: pl.BlockSpec((1, ts, width), lambda b, i: (b, i, 0))
    wide = SDS((bsz, seq, RWKV_DIM), F32)
    rt, kkt, kh, bh, v, pinc, bonus, g = pl.pallas_call(
        _rwkv_prep_kernel,
        out_shape=[wide] * 8,
        grid=(bsz, seq // ts),
        in_specs=[pl.BlockSpec((1, ts, 2048), lambda b, i: (b, i, SEG_B // 2048)),
                  full((1, 2048)), full((1, RWKV_DIM)), full((LANES, RWKV_DIM)),
                  full((1, RWKV_DIM)), full((LANES, RWKV_DIM)), full((LORA_G, RWKV_DIM)),
                  full((1, RWKV_DIM)), full((1, RWKV_DIM)), full((1, RWKV_DIM)),
                  full((RWKV_DIM, RWKV_DIM)), full((ts, ts))],
        out_specs=[seq_blk(RWKV_DIM)] * 8,
        scratch_shapes=[pltpu.VMEM((ts + SUBLANES, 2048), F32)],
        compiler_params=_cparams(("arbitrary", "arbitrary")),
        name="rwkv_prep",
    )(p, mu_pad, row(w0), wup_pad, row(a0), aup_pad, g_up, row(k_k), row(k_a), row(r_k), bd, ltri)

    c = RWKV_CHUNK
    nb = 2 if bsz % 2 == 0 else 1
    chunk_blk = pl.BlockSpec((nb, c, RWKV_DIM), lambda b, i: (b, i, 0))
    y = pl.pallas_call(
        _rwkv_scan_kernel,
        out_shape=wide,
        grid=(bsz // nb, seq // c),
        in_specs=[chunk_blk] * 6,
        out_specs=chunk_blk,
        scratch_shapes=[pltpu.VMEM((nb, RWKV_HEAD, RWKV_DIM), F32)],
        compiler_params=_cparams(("arbitrary", "arbitrary")),
        name="rwkv_scan",
    )(rt, kkt, kh, bh, v, pinc)

    tp = min(seq, 512)
    blk = lambda width: pl.BlockSpec((1, tp, width), lambda b, i: (b, i, 0))
    return pl.pallas_call(
        _rwkv_post_kernel,
        out_shape=SDS((bsz, seq, D_MODEL), MERGED_DTYPE),
        grid=(bsz, seq // tp),
        in_specs=[blk(RWKV_DIM), blk(RWKV_DIM), blk(RWKV_DIM),
                  pl.BlockSpec((1, tp, D_MODEL), lambda b, i: (b, i, SEG_G // D_MODEL + 1)),
                  blk(D_MODEL), full((1, RWKV_DIM)), full((1, RWKV_DIM)),
                  full((RWKV_DIM, RWKV_DIM)), full((RWKV_DIM, D_MODEL))],
        out_specs=blk(D_MODEL),
        compiler_params=_cparams(("arbitrary", "arbitrary")),
        name="rwkv_post",
    )(y, bonus, g, p, merged, row(gn_g), row(gn_b), bd * (1.0 / RWKV_HEAD), proj_bf)


def _out_kernel(m_ref, x_ref, g1_ref, w_ref, n2_ref, sc_ref, sh_ref, xo_ref, h_ref):
    xn = x_ref[0] + g1_ref[0] * _bdot(m_ref[0], w_ref[...])
    xo_ref[0] = xn
    h_ref[0] = _modulated_rmsnorm(xn, n2_ref[...], sc_ref[0], sh_ref[0])


def _out_proj(merged, x, g1, w_bf, norm2, sc2, sh2):
    bsz, seq, _ = x.shape
    ts = min(seq, 512)
    blk = pl.BlockSpec((1, ts, D_MODEL), lambda b, i: (b, i, 0))
    per_b = pl.BlockSpec((1, 1, D_MODEL), lambda b, i: (b, 0, 0))
    return pl.pallas_call(
        _out_kernel,
        out_shape=[SDS((bsz, seq, D_MODEL), F32)] * 2,
        grid=(bsz, seq // ts),
        in_specs=[blk, blk, per_b, pl.BlockSpec((D_MODEL, D_MODEL), lambda b, i: (0, 0)),
                  pl.BlockSpec((1, D_MODEL), lambda b, i: (0, 0)), per_b, per_b],
        out_specs=[blk, blk],
        compiler_params=_cparams(("arbitrary", "arbitrary")),
        name="out_proj",
    )(merged, x, g1, w_bf, norm2, sc2, sh2)


MOE_TILE = 256
MOE_SLOTS = 2560
RUN_LOOP_UNROLL = 4
DISPATCH_PIECE = 48
DISPATCH_SPARE = DISPATCH_PIECE - SUBLANES
DISPATCH_WAIT_ROWS = 2048
assert MOE_SLOTS >= TOP_K * MOE_TILE + N_EXPERTS * (SUBLANES - 1) + DISPATCH_SPARE
U32 = jnp.uint32


ROUTE_TILES = 2


def _route_kernel(h_ref, rw_ref, bias_ref, upper_ref, ltri_ref, slot_ref, w_ref, n8_ref):
    tile = MOE_TILE
    subs = range(h_ref.shape[0] // tile)
    neg = -jnp.inf
    shape3 = (GROUP_SIZE, N_GROUPS, tile)
    to3 = lambda a: a.reshape(shape3)
    lanes_of = lambda a: jnp.concatenate([a] * (tile // LANES), axis=1)
    fold = lambda a: jnp.sum(jnp.sum(a, axis=0), axis=0, keepdims=True)
    slab = lax.broadcasted_iota(I32, shape3, 0).astype(F32)
    grp = lax.broadcasted_iota(I32, shape3, 1).astype(F32)
    eid = grp * GROUP_SIZE + slab
    gi = lax.broadcasted_iota(I32, (N_GROUPS, tile), 0).astype(F32)
    ones_cols = jnp.ones((tile, LANES), BF16)
    s3 = [to3(jax.nn.sigmoid(_hdot_nt(rw_ref[...], h_ref[pl.ds(q * tile, tile), :]))) for q in subs]
    b3 = [s3[q] + to3(bias_ref[...]) for q in subs]
    m1 = [jnp.max(b3[q], axis=0, keepdims=True) for q in subs]
    first = [jnp.min(jnp.where(b3[q] == m1[q], slab, GROUP_SIZE), axis=0, keepdims=True) for q in subs]
    m2 = [jnp.max(jnp.where(slab == first[q], neg, b3[q]), axis=0, keepdims=True) for q in subs]
    gs = [(m1[q] + m2[q])[0] for q in subs]
    chosen = [jnp.zeros((N_GROUPS, tile), F32) for q in subs]
    for _ in range(TOPK_GROUPS):
        m = [jnp.max(gs[q], axis=0, keepdims=True) for q in subs]
        hit = [gi == jnp.min(jnp.where(gs[q] == m[q], gi, N_GROUPS), axis=0, keepdims=True) for q in subs]
        chosen = [jnp.where(hit[q], 1.0, chosen[q]) for q in subs]
        gs = [jnp.where(hit[q], neg, gs[q]) for q in subs]
    cur = [jnp.where((chosen[q] > 0.0)[None], b3[q], neg) for q in subs]
    base = [jnp.zeros((N_EXPERTS, LANES), F32) for q in subs]
    picks, w_rows, rank_rows = [[] for q in subs], [[] for q in subs], [[] for q in subs]
    for _ in range(TOP_K):
        m = [jnp.max(jnp.max(cur[q], axis=0), axis=0, keepdims=True)[None] for q in subs]
        pick = [jnp.min(jnp.min(jnp.where(cur[q] == m[q], eid, N_EXPERTS), axis=0), axis=0, keepdims=True)
                for q in subs]
        hit = [eid == pick[q][None] for q in subs]
        onehot = [hit[q].astype(BF16).reshape(N_EXPERTS, tile) for q in subs]
        before = [jnp.dot(onehot[q], upper_ref[...], preferred_element_type=F32) for q in subs]
        count = [jnp.dot(onehot[q], ones_cols, preferred_element_type=F32) for q in subs]
        for q in subs:
            w_rows[q].append(fold(jnp.where(hit[q], s3[q], 0.0)))
            picks[q].append(pick[q])
            rank_rows[q].append(fold(jnp.where(hit[q], to3(before[q] + lanes_of(base[q])), 0.0)))
        cur = [jnp.where(hit[q], neg, cur[q]) for q in subs]
        base = [base[q] + count[q] for q in subs]
    n8 = [jnp.floor((base[q] + (SUBLANES - 1.0)) * (1.0 / SUBLANES)) * SUBLANES for q in subs]
    run_start = [to3(lanes_of(_hdot(ltri_ref[...], n8[q]))) for q in subs]
    slots, weights = [], []
    for q in subs:
        slots.append(jnp.concatenate(
            [rank_rows[q][j] + fold(jnp.where(eid == picks[q][j][None], run_start[q], 0.0)) for j in range(TOP_K)],
            axis=0))
        w_all = jnp.concatenate(w_rows[q], axis=0)
        weights.append(w_all / jnp.sum(w_all, axis=0, keepdims=True) * ROUTED_SCALE)
    for q in subs:
        n8_ref[q] = n8[q]
    w_ref[...] = jnp.concatenate(weights, axis=1)
    slot_ref[...] = jnp.concatenate(slots, axis=1).astype(I32)


def _route(h2, router_w, router_bias):
    n_tok = h2.shape[0]
    tile = MOE_TILE
    n_tiles = n_tok // tile
    per_step = ROUTE_TILES if n_tiles % ROUTE_TILES == 0 else 1
    regroup = lambda a: a.reshape(N_GROUPS, GROUP_SIZE, -1).transpose(1, 0, 2).reshape(N_EXPERTS, -1)
    rw = regroup(router_w.T)
    bias = jnp.broadcast_to(regroup(router_bias.reshape(N_EXPERTS, 1)), (N_EXPERTS, tile))
    t_id = jnp.arange(tile, dtype=I32)
    upper = (t_id[:, None] < t_id[None, :]).astype(BF16)
    e_id = jnp.arange(N_EXPERTS, dtype=I32)
    ltri = (e_id[:, None] > e_id[None, :]).astype(F32)
    tok_blk = pl.BlockSpec((TOP_K, per_step * tile), lambda i: (0, i))
    full = lambda shape: pl.BlockSpec(shape, lambda i: (0,) * len(shape))
    slot_t, w_t, n8 = pl.pallas_call(
        _route_kernel,
        out_shape=[SDS((TOP_K, n_tok), I32), SDS((TOP_K, n_tok), F32),
                   SDS((n_tiles, N_EXPERTS, LANES), F32)],
        grid=(n_tiles // per_step,),
        in_specs=[pl.BlockSpec((per_step * tile, D_MODEL), lambda i: (i, 0)),
                  full((N_EXPERTS, D_MODEL)), full((N_EXPERTS, tile)), full((tile, tile)),
                  full((N_EXPERTS, N_EXPERTS))],
        out_specs=[tok_blk, tok_blk, pl.BlockSpec((per_step, N_EXPERTS, LANES), lambda i: (i, 0, 0))],
        compiler_params=_cparams(("arbitrary",)),
        name="moe_route",
    )(h2, rw, bias, upper, ltri)
    return slot_t, w_t, n8[:, :, 0].astype(I32)


def _for_each_run_piece(tile_idx, n8_ref, fn):
    for r in range(N_EXPERTS):
        n = n8_ref[tile_idx * N_EXPERTS + r]
        size = MOE_TILE
        while size >= SUBLANES:
            @pl.when((n & size) != 0)
            def _(size=size):
                fn(r, n & ~(2 * size - 1), size)
            size //= 2


def _wait_rows(total, make_wait):
    size = SUBLANES
    while size <= MOE_SLOTS:
        @pl.when((total & size) != 0)
        def _(size=size):
            make_wait(size).wait()
        size *= 2


def _pack_pairs(hi_bits, lo_bits):
    return (hi_bits & jnp.uint32(0xFFFF0000)) | (lo_bits >> 16)


def _unpack_pairs(u):
    hi = lax.bitcast_convert_type(u & jnp.uint32(0xFFFF0000), F32)
    lo = lax.bitcast_convert_type(u << 16, F32)
    return jnp.concatenate([hi, lo], axis=1).astype(BF16)


def _dispatch_kernel(n8_ref, off_ref, dst_ref, tot_ref, pad_end_ref, slot_ref, h_ref, xs_ref,
                     g_ref, zero_ref, sem):
    i = pl.program_id(0)
    tile = h_ref.shape[0]
    half = D_MODEL // 2

    @pl.when(i == 0)
    def _():
        zero_ref[...] = jnp.zeros(zero_ref.shape, U32)

        def last_block(e, back):
            start = pl.multiple_of(jnp.maximum(pad_end_ref[e] - back * MOE_ROWS, 0), MOE_ROWS)
            return pltpu.make_async_copy(zero_ref, xs_ref.at[pl.ds(start, MOE_ROWS), :], sem.at[0])

        def has_blocks(e, back):
            prev = jnp.where(e > 0, pad_end_ref[jnp.maximum(e - 1, 0)], 0)
            return pad_end_ref[e] - prev >= back * MOE_ROWS

        def clear(e, carry):
            for back in (1, 2):
                pl.when(has_blocks(e, back))(lambda back=back: last_block(e, back).start())
            return carry
        lax.fori_loop(0, N_EXPERTS, clear, 0)

        def done(e, carry):
            for back in (1, 2):
                pl.when(has_blocks(e, back))(lambda back=back: last_block(e, back).wait())
            return carry
        lax.fori_loop(0, N_EXPERTS, done, 0)

        def tail_block(b):
            start = pl.multiple_of(b * MOE_ROWS, MOE_ROWS)
            return pltpu.make_async_copy(zero_ref, xs_ref.at[pl.ds(start, MOE_ROWS), :], sem.at[0])

        def clear_tail(b, carry):
            tail_block(b).start()
            return carry

        def done_tail(b, carry):
            tail_block(b).wait()
            return carry
        used = pad_end_ref[N_EXPERTS - 1] // MOE_ROWS
        lax.fori_loop(used, xs_ref.shape[0] // MOE_ROWS, clear_tail, 0)
        lax.fori_loop(used, xs_ref.shape[0] // MOE_ROWS, done_tail, 0)

    slot_id = lax.broadcasted_iota(I32, (MOE_SLOTS, tile), 0)
    sel = jnp.zeros((MOE_SLOTS, tile), F32)
    for j in range(TOP_K):
        sel = jnp.where(slot_id == slot_ref[pl.ds(j, 1), :], 1.0, sel)
    g = jnp.dot(sel.astype(BF16), h_ref[...].astype(BF16), preferred_element_type=F32)
    bits = lax.bitcast_convert_type(g, U32)
    buf = i % 2
    g_ref[buf] = _pack_pairs(bits[:, :half], bits[:, half:])

    def drain(step):
        total = tot_ref[step]
        whole = pltpu.make_async_copy(g_ref.at[step % 2, pl.ds(0, DISPATCH_WAIT_ROWS), :],
                                      xs_ref.at[pl.ds(0, DISPATCH_WAIT_ROWS), :], sem.at[step % 2])

        def wait_whole(k, carry):
            whole.wait()
            return carry
        lax.fori_loop(0, lax.shift_right_logical(total, DISPATCH_WAIT_ROWS.bit_length() - 1), wait_whole, 0)
        _wait_rows(total & (DISPATCH_WAIT_ROWS - 1), lambda rows: pltpu.make_async_copy(
            g_ref.at[step % 2, pl.ds(0, rows), :], xs_ref.at[pl.ds(0, rows), :], sem.at[step % 2]))

    pl.when(i > 0)(lambda: drain(i - 1))

    def copy(r, k):
        src = pl.multiple_of(off_ref[i * N_EXPERTS + r] + k * DISPATCH_PIECE, SUBLANES)
        dst = pl.multiple_of(dst_ref[i * N_EXPERTS + r] + k * DISPATCH_PIECE, SUBLANES)
        pltpu.make_async_copy(g_ref.at[buf, pl.ds(src, DISPATCH_PIECE), :],
                              xs_ref.at[pl.ds(dst, DISPATCH_PIECE), :], sem.at[buf]).start()

    def per_expert(r, carry):
        n = n8_ref[i * N_EXPERTS + r]
        pl.when(n > 0)(lambda: copy(r, 0))

        @pl.when(n > DISPATCH_PIECE)
        def _():
            def more(k, c):
                copy(r, k)
                return c
            lax.fori_loop(1, (n + (DISPATCH_PIECE - 1)) // DISPATCH_PIECE, more, 0)
        return carry
    lax.fori_loop(0, N_EXPERTS, per_expert, 0, unroll=RUN_LOOP_UNROLL)
    pl.when(i == pl.num_programs(0) - 1)(lambda: drain(i))


def _dispatch(h2, slot_t, n8_flat, off_flat, dst_flat, tot, pad_end, n_rows):
    n_tok = h2.shape[0]
    tile = MOE_TILE
    grid_spec = pltpu.PrefetchScalarGridSpec(
        num_scalar_prefetch=5,
        grid=(n_tok // tile,),
        in_specs=[pl.BlockSpec((TOP_K, tile), lambda i, *_: (0, i)),
                  pl.BlockSpec((tile, D_MODEL), lambda i, *_: (i, 0))],
        out_specs=pl.BlockSpec(memory_space=pl.ANY),
        scratch_shapes=[pltpu.VMEM((2, MOE_SLOTS, D_MODEL // 2), U32),
                        pltpu.VMEM((MOE_ROWS, D_MODEL // 2), U32), pltpu.SemaphoreType.DMA((2,))],
    )
    return pl.pallas_call(
        _dispatch_kernel,
        out_shape=SDS((n_rows, D_MODEL // 2), U32),
        grid_spec=grid_spec,
        compiler_params=_cparams(("arbitrary",)),
        name="moe_dispatch",
    )(n8_flat, off_flat, dst_flat, tot, pad_end, slot_t, h2)


def _expert_kernel(first_ref, nblk_ref, exp_of_ref, xs_ref, w1_ref, w3_ref, w2_ref, y_ref,
                   xbuf, ybuf, wb1, wb3, wb2, in_sem, out_sem):
    del exp_of_ref
    r = pl.program_id(0)
    half = D_MODEL // 2
    used = first_ref[N_EXPERTS - 1] + nblk_ref[N_EXPERTS - 1]
    part = MOE_ROWS // EXPERT_DMA_PARTS

    class _Copies:
        def __init__(self, make):
            self.parts = [make(p) for p in range(EXPERT_DMA_PARTS)]

        def start(self):
            for c in self.parts:
                c.start()

        def wait(self):
            for c in self.parts:
                c.wait()

    def hbm_rows(g, p):
        return pl.ds(pl.multiple_of(g * MOE_ROWS + p * part, part), part)

    in_copy = lambda g, slot: _Copies(lambda p: pltpu.make_async_copy(
        xs_ref.at[hbm_rows(g, p), :], xbuf.at[slot, pl.ds(p * part, part), :], in_sem.at[slot]))
    out_copy = lambda g, slot: _Copies(lambda p: pltpu.make_async_copy(
        ybuf.at[slot, pl.ds(p * part, part), :], y_ref.at[hbm_rows(g, p), :], out_sem.at[slot]))

    @pl.when(r == 0)
    def _():
        for g in range(EXPERT_BUFS - 1):
            pl.when(g < used)(lambda g=g: in_copy(g, g).start())

    wb1[...] = w1_ref[0, 0].astype(BF16)
    wb3[...] = w3_ref[0, 0].astype(BF16)
    wb2[...] = w2_ref[0, 0].astype(BF16)

    def block(k, carry):
        g = first_ref[r] + k
        slot = g % EXPERT_BUFS
        in_copy(g, slot).wait()
        ahead = g + (EXPERT_BUFS - 1)
        pl.when(ahead < used)(lambda: in_copy(ahead, ahead % EXPERT_BUFS).start())
        pl.when(g >= EXPERT_BUFS)(lambda: out_copy(g - EXPERT_BUFS, slot).wait())
        dot = lambda a, b: jnp.dot(a, b, preferred_element_type=F32)
        x = _unpack_pairs(xbuf[slot])
        hid = (jax.nn.silu(dot(x, wb1[...])) * dot(x, wb3[...])).astype(BF16)
        y = dot(hid, wb2[...]).astype(BF16).astype(F32)
        bits = lax.bitcast_convert_type(y, U32)
        ybuf[slot] = _pack_pairs(bits[:, :half], bits[:, half:])
        out_copy(g, slot).start()
        return carry
    lax.fori_loop(0, nblk_ref[r], block, 0)

    @pl.when(r == N_EXPERTS - 1)
    def _():
        for back in range(EXPERT_BUFS, 0, -1):
            pl.when(used >= back)(
                lambda back=back: out_copy(used - back, (used - back) % EXPERT_BUFS).wait())

        ybuf[0] = jnp.zeros(ybuf.shape[1:], U32)
        n_blocks = y_ref.shape[0] // MOE_ROWS

        def clear(g, carry):
            out_copy(g, 0).start()
            return carry

        def done(g, carry):
            out_copy(g, 0).wait()
            return carry
        lax.fori_loop(used, n_blocks, clear, 0)
        lax.fori_loop(used, n_blocks, done, 0)


def _experts(xs, first_block, n_block, exp_of_row, w1, w3, w2, layer):
    n_rows = xs.shape[0]
    half = D_MODEL // 2
    w_in_blk = pl.BlockSpec((1, 1, D_MODEL, EXPERT_FF), lambda r, first, nblk, eo: (layer, eo[r], 0, 0))
    grid_spec = pltpu.PrefetchScalarGridSpec(
        num_scalar_prefetch=3,
        grid=(N_EXPERTS,),
        in_specs=[pl.BlockSpec(memory_space=pl.ANY), w_in_blk, w_in_blk,
                  pl.BlockSpec((1, 1, EXPERT_FF, D_MODEL), lambda r, first, nblk, eo: (layer, eo[r], 0, 0))],
        out_specs=pl.BlockSpec(memory_space=pl.ANY),
        scratch_shapes=[pltpu.VMEM((EXPERT_BUFS, MOE_ROWS, half), U32),
                        pltpu.VMEM((EXPERT_BUFS, MOE_ROWS, half), U32),
                        pltpu.VMEM((D_MODEL, EXPERT_FF), BF16), pltpu.VMEM((D_MODEL, EXPERT_FF), BF16),
                        pltpu.VMEM((EXPERT_FF, D_MODEL), BF16),
                        pltpu.SemaphoreType.DMA((EXPERT_BUFS,)), pltpu.SemaphoreType.DMA((EXPERT_BUFS,))],
    )
    return pl.pallas_call(
        _expert_kernel,
        out_shape=SDS((n_rows, half), U32),
        grid_spec=grid_spec,
        compiler_params=_cparams(("arbitrary",)),
        name="moe_experts",
    )(first_block, n_block, exp_of_row, xs, w1, w3, w2)


def _combine_kernel(n8_ref, off_ref, src_ref, tot_ref, y_ref, slot_ref, w_ref, h_ref, x_ref, g2_ref,
                    s1_ref, s3_ref, s2_ref, fn_ref, o_ref, yt_ref, sem, *, final):
    tile = h_ref.shape[1]
    i = pl.program_id(0) * pl.num_programs(1) + pl.program_id(1)

    @pl.when(i == 0)
    def _():
        yt_ref[...] = jnp.zeros(yt_ref.shape, U32)

    def piece(r, offset, rows):
        src = pl.multiple_of(src_ref[i * N_EXPERTS + r] + offset, SUBLANES)
        dst = pl.multiple_of(off_ref[i * N_EXPERTS + r] + offset, SUBLANES)
        pltpu.make_async_copy(y_ref.at[pl.ds(src, rows), :], yt_ref.at[pl.ds(dst, rows), :], sem).start()

    _for_each_run_piece(i, n8_ref, piece)
    h = h_ref[0]
    shared = _bdot(jax.nn.silu(_bdot(h, s1_ref[...])) * _bdot(h, s3_ref[...]), s2_ref[...])
    slot_id = lax.broadcasted_iota(I32, (tile, MOE_SLOTS), 1)
    slots = slot_ref[0]
    w = w_ref[0]
    pw = jnp.zeros((tile, MOE_SLOTS), F32)
    for j in range(TOP_K):
        pw = jnp.where(slot_id == slots[:, j:j + 1], w[:, j:j + 1], pw)
    _wait_rows(tot_ref[i], lambda rows: pltpu.make_async_copy(
        y_ref.at[pl.ds(0, rows), :], yt_ref.at[pl.ds(0, rows), :], sem))
    routed = jnp.dot(pw.astype(BF16), _unpack_pairs(yt_ref[...]), preferred_element_type=F32)
    xn = x_ref[0] + g2_ref[0] * (routed + shared)
    if final:
        xn = xn * lax.rsqrt(jnp.mean(xn * xn, axis=-1, keepdims=True) + NORM_EPS) * fn_ref[...]
    o_ref[0] = xn


def _combine(y, slot_nat, w_nat, n8_flat, off_flat, dst_flat, tot, h2, x, g2, s1_bf, s3_bf, s2_bf,
             final_norm, final):
    bsz, seq, _ = x.shape
    tile = MOE_TILE
    per_seq = seq // tile
    blk = pl.BlockSpec((1, tile, D_MODEL), lambda b, i, *_: (b, i, 0))
    tok = pl.BlockSpec((1, tile, TOP_K), lambda b, i, *_: (b, i, 0))
    full = lambda shape: pl.BlockSpec(shape, lambda b, i, *_: (0,) * len(shape))
    grid_spec = pltpu.PrefetchScalarGridSpec(
        num_scalar_prefetch=4,
        grid=(bsz, per_seq),
        in_specs=[pl.BlockSpec(memory_space=pl.ANY), tok, tok, blk, blk,
                  pl.BlockSpec((1, 1, D_MODEL), lambda b, i, *_: (b, 0, 0)),
                  full((D_MODEL, SHARED_FF)), full((D_MODEL, SHARED_FF)), full((SHARED_FF, D_MODEL)),
                  full((1, D_MODEL))],
        out_specs=blk,
        scratch_shapes=[pltpu.VMEM((MOE_SLOTS, D_MODEL // 2), U32), pltpu.SemaphoreType.DMA],
    )
    return pl.pallas_call(
        functools.partial(_combine_kernel, final=final),
        out_shape=SDS((bsz, seq, D_MODEL), F32),
        grid_spec=grid_spec,
        compiler_params=_cparams(("arbitrary", "arbitrary")),
        name="moe_combine",
    )(n8_flat, off_flat, dst_flat, tot, y, slot_nat, w_nat, h2, x, g2, s1_bf, s3_bf, s2_bf, final_norm)


def _moe(x, h2, g2, router_w, router_bias, w1, w3, w2, layer, s1_bf, s3_bf, s2_bf, final_norm, final):
    bsz, seq, _ = x.shape
    assert seq % MOE_TILE == 0, "token tiles must not straddle sequences"
    n_tok = bsz * seq
    slot_t, w_t, n8 = _route(h2.reshape(n_tok, D_MODEL), router_w, router_bias)
    n_tiles = n8.shape[0]
    counts = jnp.sum(n8, axis=0)
    padded = jnp.where(counts > 0, (counts + DISPATCH_SPARE + MOE_ROWS - 1) // MOE_ROWS * MOE_ROWS, 0)
    pad_end = jnp.cumsum(padded).astype(I32)
    pad_start = pad_end - padded
    run_row = (pad_start[None, :] + jnp.cumsum(n8, axis=0) - n8).astype(I32)
    run_slot = (jnp.cumsum(n8, axis=1) - n8).astype(I32)
    tot = jnp.sum(n8, axis=1).astype(I32)
    copied = jnp.sum((n8 + DISPATCH_PIECE - 1) // DISPATCH_PIECE * DISPATCH_PIECE, axis=1).astype(I32)
    max_rows = (n_tok * TOP_K + n_tiles * N_EXPERTS * (SUBLANES - 1)
                + N_EXPERTS * (DISPATCH_SPARE + MOE_ROWS - 1))
    n_blocks = (max_rows + MOE_ROWS - 1) // MOE_ROWS
    row_id = jnp.arange(N_EXPERTS, dtype=I32)
    exp_of_row = (row_id % N_GROUPS) * GROUP_SIZE + row_id // N_GROUPS
    flat = lambda a: a.reshape(-1).astype(I32)
    xs = _dispatch(h2.reshape(n_tok, D_MODEL), slot_t, flat(n8), flat(run_slot), flat(run_row), copied,
                   pad_end, n_blocks * MOE_ROWS)
    y = _experts(xs, (pad_start // MOE_ROWS).astype(I32), (padded // MOE_ROWS).astype(I32), exp_of_row,
                 w1, w3, w2, layer)
    nat = lambda a: a.T.reshape(bsz, seq, TOP_K)
    return _combine(y, nat(slot_t), nat(w_t), flat(n8), flat(run_slot), flat(run_row), tot, h2, x, g2,
                    s1_bf, s3_bf, s2_bf, final_norm, final)


def _pad_cols(a, width):
    return jnp.pad(a, ((0, 0), (0, width - a.shape[1])))


def _layout_w_in(w_in):
    b0 = COLS_A
    c0 = COLS_A + COLS_B
    g0 = c0 + COLS_C
    seg_b = _pad_cols(w_in[:, b0:c0], SEG_G - SEG_B)
    return jnp.concatenate([w_in[:, c0:g0], seg_b, w_in[:, g0:], w_in[:, :b0]], axis=1).astype(BF16)


def _layout_mu(mu):
    return _pad_cols(mu.reshape(1, -1), SEG_G - SEG_B)


def _pad_rows(a, height):
    return jnp.pad(a, ((0, height - a.shape[0]), (0, 0)))


def kernel(x, c, ada_w, ada_b, norm1, norm2, w_in, conv_a_w, conv_a_b, ln_a_g, ln_a_b, proj_a, mu_b, w0, w_up, a0, a_up, g_up, k_k, k_a, r_k, gn_b_g, gn_b_b, proj_b, conv_c_w, conv_c_b, lru_wa, lru_ba, lru_wx, lru_bx, lru_lambda, proj_c, w_out, router_w, router_bias, exp_w1, exp_w3, exp_w2, sh_w1, sh_w3, sh_w2, final_norm):
    depth = ada_w.shape[0]
    bsz = x.shape[0]
    mod = _ada_mod(c, ada_w, ada_b)
    for l in range(depth):
        sh1, sc1, g1, sh2, sc2, g2 = [mod[l, :, i * D_MODEL:(i + 1) * D_MODEL].reshape(bsz, 1, D_MODEL)
                                      for i in range(N_MOD)]
        p = _in_proj(x, sc1, sh1, norm1[l].reshape(1, -1), _layout_w_in(w_in[l]))
        merged = _conv_a(p, conv_a_w[l], conv_a_b[l], ln_a_g[l], ln_a_b[l], proj_a[l].astype(BF16))
        merged = _rwkv(p, merged, _layout_mu(mu_b[l]), w0[l], _pad_rows(w_up[l], LANES).astype(BF16),
                       a0[l], jnp.pad(a_up[l], ((LORA_W, 0), (0, 0))).astype(BF16), g_up[l].astype(BF16),
                       k_k[l], k_a[l], r_k[l], gn_b_g[l], gn_b_b[l], proj_b[l].astype(BF16))
        merged = _lru(p, merged, conv_c_w[l], conv_c_b[l], lru_wa[l].astype(BF16), lru_ba[l],
                      lru_wx[l].astype(BF16), lru_bx[l], lru_lambda[l], proj_c[l].astype(BF16))
        x, h2 = _out_proj(merged, x, g1, w_out[l].astype(BF16), norm2[l].reshape(1, -1), sc2, sh2)
        x = _moe(x, h2, g2, router_w[l], router_bias[l], exp_w1, exp_w3, exp_w2, l,
                 sh_w1[l].astype(BF16), sh_w3[l].astype(BF16), sh_w2[l].astype(BF16),
                 final_norm.reshape(1, -1), final=(l == depth - 1))
    return x
```

```python
import functools

import jax
import jax.numpy as jnp
from jax import lax
from jax.experimental import pallas as pl
from jax.experimental.pallas import tpu as pltpu

F32 = jnp.float32
BF16 = jnp.bfloat16
I32 = jnp.int32
I16 = jnp.int16
SDS = jax.ShapeDtypeStruct
HIGHEST = lax.Precision.HIGHEST

D_MODEL = 1024
N_MOD = 6
NORM_EPS = 1e-6
CONV_A_CH = 512
CONV_A_WIDTH = 31
CONV_A_LN_EPS = 1e-5
RWKV_HEADS = 8
RWKV_HEAD = 64
RWKV_DIM = RWKV_HEADS * RWKV_HEAD
LORA_W = 64
LORA_A = 64
LORA_G = 128
RWKV_GN_EPS = 64e-5
RWKV_CHUNK = 64
LRU_DIM = 1024
LRU_HEADS = 8
LRU_BLOCK = LRU_DIM // LRU_HEADS
LRU_CONV = 4
LRU_C = 8.0
N_EXPERTS = 64
TOP_K = 8
N_GROUPS = 8
GROUP_SIZE = N_EXPERTS // N_GROUPS
TOPK_GROUPS = 4
EXPERT_FF = 256
SHARED_FF = 256
ROUTED_SCALE = 2.5
SEG_C = 0
SEG_B = 2048
SEG_G = 4096
SEG_A = 7168
IN_COLS_PAD = 8192
RW_R, RW_K, RW_V, RW_XWA, RW_XG = 0, 512, 1024, 1536, 1664
COLS_A = 2 * CONV_A_CH
COLS_B = 3 * RWKV_DIM + LORA_W + LORA_A + LORA_G
COLS_C = 2 * LRU_DIM
VMEM_LIMIT = 56 * 1024 * 1024
MERGED_DTYPE = BF16
SUBLANES = 8
LANES = 128
MOE_ROWS = 512
EXPERT_DMA_PARTS = 4
EXPERT_BUFS = 4


def _cparams(sem):
    return pltpu.CompilerParams(dimension_semantics=sem, vmem_limit_bytes=VMEM_LIMIT)


def _bdot(a, b):
    return jnp.dot(a.astype(BF16), b.astype(BF16), preferred_element_type=F32)


def _hdot(a, b):
    return jnp.dot(a, b, preferred_element_type=F32, precision=HIGHEST)


def _split(a):
    hi = a.astype(BF16)
    return hi, (a - hi.astype(F32)).astype(BF16)


def _head_sums(a, ones_bf):
    hi, lo = _split(a)
    return (jnp.dot(hi, ones_bf, preferred_element_type=F32)
            + jnp.dot(lo, ones_bf, preferred_element_type=F32))


def _hdot_nt(a, b):
    return lax.dot_general(a, b, (((1,), (1,)), ((), ())), preferred_element_type=F32,
                           precision=HIGHEST)


def _ada_kernel(c_ref, w_ref, b_ref, o_ref):
    cond = jax.nn.silu(c_ref[...])
    o_ref[0] = _bdot(cond, w_ref[0]) + b_ref[0]


def _ada_mod(c, ada_w, ada_b):
    depth, _, n = ada_w.shape
    bsz = c.shape[0]
    tn = 1536
    return pl.pallas_call(
        _ada_kernel,
        out_shape=SDS((depth, bsz, n), F32),
        grid=(depth, n // tn),
        in_specs=[pl.BlockSpec((bsz, D_MODEL), lambda l, j: (0, 0)),
                  pl.BlockSpec((1, D_MODEL, tn), lambda l, j: (l, 0, j)),
                  pl.BlockSpec((1, 1, tn), lambda l, j: (l, 0, j))],
        out_specs=pl.BlockSpec((1, bsz, tn), lambda l, j: (l, 0, j)),
        compiler_params=_cparams(("arbitrary", "arbitrary")),
        name="ada_mod",
    )(c, ada_w, ada_b.reshape(depth, 1, n))


def _modulated_rmsnorm(x, g, sc, sh):
    y = x * lax.rsqrt(jnp.mean(x * x, axis=-1, keepdims=True) + NORM_EPS)
    return (y * g) * (1.0 + sc) + sh


IN_PROJ_SUB = 512


def _in_kernel(x_ref, sc_ref, sh_ref, g_ref, w_ref, o_ref, h_ref):
    @pl.when(pl.program_id(2) == 0)
    def _():
        h_ref[...] = _modulated_rmsnorm(x_ref[0], g_ref[...], sc_ref[0], sh_ref[0]).astype(BF16)

    j = pl.program_id(2)
    ts, tn = o_ref.shape[1], o_ref.shape[2]
    is_gelu = j == SEG_C // tn
    is_gate = (j >= SEG_G // tn) & (j < SEG_A // tn)
    is_glu = j == SEG_A // tn

    def glu(acc):
        u = acc[:, :tn // 2] * jax.nn.sigmoid(acc[:, tn // 2:])
        return jnp.concatenate([u, jnp.zeros_like(u)], axis=1)

    def emit(act, sub):
        for r0 in range(0, ts, sub):
            rows = pl.ds(r0, sub)
            acc = jnp.dot(h_ref[rows, :], w_ref[...], preferred_element_type=F32)
            o_ref[0, rows, :] = act(acc).astype(o_ref.dtype)

    sub = min(ts, IN_PROJ_SUB)
    pl.when(is_gelu)(lambda: emit(functools.partial(jax.nn.gelu, approximate=True), sub))
    pl.when(is_gate)(lambda: emit(jax.nn.sigmoid, sub))
    pl.when(is_glu)(lambda: emit(glu, sub))
    pl.when(jnp.logical_not(is_gelu | is_gate | is_glu))(lambda: emit(lambda acc: acc, ts))


def _in_proj(x, sc, sh, g, w_pad):
    bsz, seq, _ = x.shape
    ts = min(seq, 1024)
    tn = 1024
    return pl.pallas_call(
        _in_kernel,
        out_shape=SDS((bsz, seq, IN_COLS_PAD), BF16),
        grid=(bsz, seq // ts, IN_COLS_PAD // tn),
        in_specs=[pl.BlockSpec((1, ts, D_MODEL), lambda b, i, j: (b, i, 0)),
                  pl.BlockSpec((1, 1, D_MODEL), lambda b, i, j: (b, 0, 0)),
                  pl.BlockSpec((1, 1, D_MODEL), lambda b, i, j: (b, 0, 0)),
                  pl.BlockSpec((1, D_MODEL), lambda b, i, j: (0, 0)),
                  pl.BlockSpec((D_MODEL, tn), lambda b, i, j: (0, j))],
        out_specs=pl.BlockSpec((1, ts, tn), lambda b, i, j: (b, i, j)),
        scratch_shapes=[pltpu.VMEM((ts, D_MODEL), BF16)],
        compiler_params=_cparams(("arbitrary", "arbitrary", "arbitrary")),
        name="in_proj",
    )(x, sc, sh, g, w_pad)


CONV_A_HALO = 32
CONV_A_SUB = 64


def _conv_a_kernel(pa_ref, pg_ref, cw_ref, cb_ref, lg_ref, lb_ref, pj_ref, o_ref, ext_ref, sh_ref, y_ref):
    ts = pa_ref.shape[1]

    @pl.when(pl.program_id(1) == 0)
    def _():
        ext_ref[pl.ds(0, CONV_A_HALO), :] = jnp.zeros((CONV_A_HALO, CONV_A_CH), F32)

    ext_ref[pl.ds(CONV_A_HALO, ts), :] = pa_ref[0].astype(F32)
    for p in range(1, SUBLANES):
        sh_ref[p - 1] = ext_ref[pl.ds(p, sh_ref.shape[1]), :]
    first = CONV_A_HALO - (CONV_A_WIDTH - 1)
    for r0 in range(0, ts, CONV_A_SUB):
        acc = jnp.zeros((CONV_A_SUB, CONV_A_CH), F32) + cb_ref[...]
        for j in range(CONV_A_WIDTH):
            phase = (first + j) % SUBLANES
            rows = pl.ds(r0 + first + j - phase, CONV_A_SUB)
            tap = ext_ref[rows, :] if phase == 0 else sh_ref[phase - 1, rows, :]
            acc = acc + tap * cw_ref[pl.ds(j, 1), :]
        y_ref[pl.ds(r0, CONV_A_SUB), :] = acc
    ext_ref[pl.ds(0, CONV_A_HALO), :] = ext_ref[pl.ds(ts, CONV_A_HALO), :]
    y = y_ref[...]
    mu = jnp.mean(y, axis=-1, keepdims=True)
    d = y - mu
    var = jnp.mean(d * d, axis=-1, keepdims=True)
    yn = d * lax.rsqrt(var + CONV_A_LN_EPS) * lg_ref[...] + lb_ref[...]
    o = _bdot(jax.nn.silu(yn), pj_ref[...])
    o_ref[0] = (pg_ref[0].astype(F32) * o).astype(o_ref.dtype)


def _conv_a(p, conv_w, conv_b, ln_g, ln_b, proj_bf):
    bsz, seq, _ = p.shape
    ts = min(seq, 512)
    row = lambda a: a.reshape(1, -1)
    full = lambda shape: pl.BlockSpec(shape, lambda b, i: (0,) * len(shape))
    return pl.pallas_call(
        _conv_a_kernel,
        out_shape=SDS((bsz, seq, D_MODEL), MERGED_DTYPE),
        grid=(bsz, seq // ts),
        in_specs=[pl.BlockSpec((1, ts, CONV_A_CH), lambda b, i: (b, i, SEG_A // CONV_A_CH)),
                  pl.BlockSpec((1, ts, D_MODEL), lambda b, i: (b, i, SEG_G // D_MODEL)),
                  full((CONV_A_WIDTH, CONV_A_CH)), full((1, CONV_A_CH)), full((1, CONV_A_CH)),
                  full((1, CONV_A_CH)), full((CONV_A_CH, D_MODEL))],
        out_specs=pl.BlockSpec((1, ts, D_MODEL), lambda b, i: (b, i, 0)),
        scratch_shapes=[pltpu.VMEM((ts + CONV_A_HALO, CONV_A_CH), F32),
                        pltpu.VMEM((SUBLANES - 1, ts + CONV_A_HALO - SUBLANES, CONV_A_CH), F32),
                        pltpu.VMEM((ts, CONV_A_CH), F32)],
        compiler_params=_cparams(("arbitrary", "arbitrary")),
        name="conv_a",
    )(p, p, conv_w, row(conv_b), row(ln_g), row(ln_b), proj_bf)


def _lru_kernel(pc_ref, pg_ref, m_ref, cw_ref, cb_ref, wa_ref, ba_ref, wx_ref, bx_ref, lam_ref,
                pj_ref, o_ref, ext_ref, h_ref, a_ref, b_ref):
    ts = pc_ref.shape[1]
    groups = ts // SUBLANES

    @pl.when(pl.program_id(1) == 0)
    def _():
        ext_ref[pl.ds(0, SUBLANES), :] = jnp.zeros((SUBLANES, LRU_DIM), F32)
        h_ref[...] = jnp.zeros((SUBLANES, LRU_DIM), F32)

    pc = pc_ref[0].astype(F32)
    y_gate = pc[:, :LRU_DIM]
    ext_ref[pl.ds(SUBLANES, ts), :] = pc[:, LRU_DIM:]
    first = SUBLANES - (LRU_CONV - 1)
    xc = jnp.zeros((ts, LRU_DIM), F32) + cb_ref[...]
    for j in range(LRU_CONV):
        xc = xc + ext_ref[pl.ds(first + j, ts), :] * cw_ref[pl.ds(j, 1), :]
    ext_ref[pl.ds(0, SUBLANES), :] = ext_ref[pl.ds(ts, SUBLANES), :]

    def block_diag(w_ref):
        return jnp.concatenate(
            [_bdot(xc[:, h * LRU_BLOCK:(h + 1) * LRU_BLOCK], w_ref[h]) for h in range(LRU_HEADS)],
            axis=1)

    gate_a = jax.nn.sigmoid(block_diag(wa_ref) + ba_ref[...])
    gate_x = jax.nn.sigmoid(block_diag(wx_ref) + bx_ref[...])
    log_a = -LRU_C * gate_a * jax.nn.softplus(-lam_ref[...])
    a = jnp.exp(log_a)
    b = xc * gate_x * jnp.sqrt(1.0 - jnp.exp(2.0 * log_a))

    a3 = a.reshape(groups, SUBLANES, LRU_DIM)
    b3 = b.reshape(groups, SUBLANES, LRU_DIM)
    row = lax.broadcasted_iota(I32, (groups, SUBLANES, LRU_DIM), 1)
    for s in (1, 2, 4):
        keep = row >= s
        b3 = jnp.where(keep, a3 * pltpu.roll(b3, s, axis=1) + b3, b3)
        a3 = jnp.where(keep, a3 * pltpu.roll(a3, s, axis=1), a3)
    a_ref[...] = a3.reshape(ts, LRU_DIM)
    b_ref[...] = b3.reshape(ts, LRU_DIM)
    h = h_ref[...]
    for g in range(groups):
        rows = pl.ds(g * SUBLANES, SUBLANES)
        hg = a_ref[rows, :] * h + b_ref[rows, :]
        b_ref[rows, :] = hg
        h = jnp.broadcast_to(hg[SUBLANES - 1:SUBLANES, :], (SUBLANES, LRU_DIM))
    h_ref[...] = h
    o = _bdot(b_ref[...] * y_gate, pj_ref[...])
    o_ref[0] = (m_ref[0].astype(F32) + pg_ref[0].astype(F32) * o).astype(o_ref.dtype)


def _lru(p, merged, conv_w, conv_b, wa_bf, ba, wx_bf, bx, lam, proj_bf):
    bsz, seq, _ = p.shape
    ts = min(seq, 256)
    row = lambda a: a.reshape(1, -1)
    full = lambda shape: pl.BlockSpec(shape, lambda b, i: (0,) * len(shape))
    return pl.pallas_call(
        _lru_kernel,
        out_shape=SDS((bsz, seq, D_MODEL), MERGED_DTYPE),
        grid=(bsz, seq // ts),
        in_specs=[pl.BlockSpec((1, ts, 2 * LRU_DIM), lambda b, i: (b, i, SEG_C // (2 * LRU_DIM))),
                  pl.BlockSpec((1, ts, D_MODEL), lambda b, i: (b, i, SEG_G // D_MODEL + 2)),
                  pl.BlockSpec((1, ts, D_MODEL), lambda b, i: (b, i, 0)),
                  full((LRU_CONV, LRU_DIM)), full((1, LRU_DIM)),
                  full((LRU_HEADS, LRU_BLOCK, LRU_BLOCK)), full((1, LRU_DIM)),
                  full((LRU_HEADS, LRU_BLOCK, LRU_BLOCK)), full((1, LRU_DIM)),
                  full((1, LRU_DIM)), full((LRU_DIM, D_MODEL))],
        out_specs=pl.BlockSpec((1, ts, D_MODEL), lambda b, i: (b, i, 0)),
        scratch_shapes=[pltpu.VMEM((ts + SUBLANES, LRU_DIM), F32),
                        pltpu.VMEM((SUBLANES, LRU_DIM), F32),
                        pltpu.VMEM((ts, LRU_DIM), F32),
                        pltpu.VMEM((ts, LRU_DIM), F32)],
        compiler_params=_cparams(("arbitrary", "arbitrary")),
        name="rg_lru",
    )(p, p, merged, conv_w, row(conv_b), wa_bf, row(ba), wx_bf, row(bx), row(lam), proj_bf)


def _rwkv_prep_kernel(pb_ref, mu_ref, w0_ref, wup_ref, a0_ref, aup_ref, gup_ref, kk_ref, ka_ref,
                      rk_ref, bd_ref, ltri_ref,
                      rt_ref, kkt_ref, kh_ref, bh_ref, v_ref, pinc_ref, bonus_ref, g_ref, ext_ref):
    ts = pb_ref.shape[1]

    @pl.when(pl.program_id(1) == 0)
    def _():
        ext_ref[pl.ds(0, SUBLANES), :] = jnp.zeros((SUBLANES, ext_ref.shape[1]), F32)

    p = pb_ref[0].astype(F32)
    ext_ref[pl.ds(SUBLANES, ts), :] = p
    prev = ext_ref[pl.ds(SUBLANES - 1, ts), :]
    ext_ref[pl.ds(0, SUBLANES), :] = ext_ref[pl.ds(ts, SUBLANES), :]
    pm = p + (prev - p) * mu_ref[...]
    r = pm[:, RW_R:RW_R + RWKV_DIM]
    k = pm[:, RW_K:RW_K + RWKV_DIM]
    v = pm[:, RW_V:RW_V + RWKV_DIM]
    xwa = pm[:, RW_XWA:RW_XWA + LANES]
    xg = pm[:, RW_XG:RW_XG + LORA_G]
    w = -jax.nn.softplus(-(w0_ref[...] + _bdot(jnp.tanh(xwa), wup_ref[...]))) - 0.5
    lw = -jnp.exp(w)
    a = jax.nn.sigmoid(a0_ref[...] + _bdot(xwa, aup_ref[...]))
    g_ref[0] = _bdot(jax.nn.sigmoid(xg), gup_ref[...])
    kkr = k * kk_ref[...]
    ss = _head_sums(kkr * kkr, bd_ref[...])
    kk = kkr / jnp.maximum(jnp.sqrt(ss), 1e-12)
    k2 = k * (1.0 + (a - 1.0) * ka_ref[...])
    lw_hi = lw.astype(BF16)
    lw_mid, lw_lo = _split(lw - lw_hi.astype(F32))
    tri = ltri_ref[...]
    lcum = (jnp.dot(tri, lw_hi, preferred_element_type=F32) + jnp.dot(tri, lw_mid, preferred_element_type=F32)
            + jnp.dot(tri, lw_lo, preferred_element_type=F32))
    pinc = jnp.exp(lcum)
    pinv = jnp.exp(-lcum)
    rt_ref[0] = r * pinc
    kkt_ref[0] = kk * jnp.exp(lcum - lw)
    kh_ref[0] = k2 * pinv
    bh_ref[0] = kk * a * pinv
    v_ref[0] = v
    pinc_ref[0] = pinc
    bonus_ref[0] = _head_sums(r * k2 * rk_ref[...], bd_ref[...]) * v


def _dot3(a, b):
    d = lambda x, y: jnp.dot(x, y, preferred_element_type=F32)
    m = a[0].shape[0]
    both = d(jnp.concatenate([a[0], a[1]], axis=0), b[0])
    return both[:m] + both[m:] + d(a[0], b[1])


def _rwkv_scan_kernel(rt_ref, kkt_ref, kh_ref, bh_ref, v_ref, pinc_ref, y_ref, s_ref):
    c = RWKV_CHUNK
    n = RWKV_HEAD
    nb = rt_ref.shape[0]
    heads = range(nb * RWKV_HEADS)

    @pl.when(pl.program_id(1) == 0)
    def _():
        s_ref[...] = jnp.zeros(s_ref.shape, F32)

    row = lax.broadcasted_iota(I32, (c, c), 0)
    col = lax.broadcasted_iota(I32, (c, c), 1)
    eye = (row == col).astype(F32)
    same16 = (row // 16) == (col // 16)
    same32 = (row // 32) == (col // 32)
    row2 = lax.broadcasted_iota(I32, (c, 2 * c), 0)
    col2 = lax.broadcasted_iota(I32, (c, 2 * c), 1) % c
    nt = lambda a, b: lax.dot_general(a, b, (((1,), (1,)), ((), ())), preferred_element_type=F32)
    tn = lambda a, b: lax.dot_general(a, b, (((0,), (0,)), ((), ())), preferred_element_type=F32)
    dot = lambda a, b: jnp.dot(a, b, preferred_element_type=F32)
    sl = [pl.ds((h % RWKV_HEADS) * n, n) for h in heads]
    sq = [h // RWKV_HEADS for h in heads]
    v = [v_ref[sq[h], :, sl[h]] for h in heads]
    pc = [pinc_ref[sq[h], pl.ds(c - 1, 1), sl[h]] for h in heads]
    s = [s_ref[sq[h], :, sl[h]] for h in heads]
    lhs = [jnp.concatenate([kkt_ref[sq[h], :, sl[h]], rt_ref[sq[h], :, sl[h]]], axis=0) for h in heads]
    rhs = [jnp.concatenate([bh_ref[sq[h], :, sl[h]], kh_ref[sq[h], :, sl[h]]], axis=0) for h in heads]
    big = [nt(lhs[h], rhs[h]) for h in heads]
    from_state = [nt(lhs[h], s[h]) for h in heads]
    top = [jnp.where(row2 > col2, big[h][:c], 0.0) for h in heads]
    bot = [jnp.where(row2 >= col2, big[h][c:], 0.0) for h in heads]
    a_b = [top[h][:, :c] for h in heads]
    akv = [dot(top[h], jnp.concatenate([jnp.zeros((c, n), F32), v[h]], axis=0)) for h in heads]
    d16 = [jnp.where(same16, a_b[h], 0.0) for h in heads]
    sd = [_split(d16[h]) for h in heads]
    s2 = [_split(_dot3(sd[h], sd[h])) for h in heads]
    s4 = [_split(_dot3(s2[h], s2[h])) for h in heads]
    s8 = [_split(_dot3(s4[h], s4[h])) for h in heads]
    t = [eye - d16[h] for h in heads]
    for sp in (s2, s4, s8):
        t = [t[h] + _dot3(_split(t[h]), sp[h]) for h in heads]
    for off in ([jnp.where(same32 & jnp.logical_not(same16), a_b[h], 0.0) for h in heads],
                [jnp.where(same32, 0.0, a_b[h]) for h in heads]):
        tb = [t[h].astype(BF16) for h in heads]
        lt = [dot(off[h].astype(BF16), tb[h]).astype(BF16) for h in heads]
        t = [t[h] - dot(tb[h], lt[h]) for h in heads]
    u = [dot(t[h], from_state[h][:c] + akv[h]) for h in heads]
    vu = [jnp.concatenate([-u[h], v[h]], axis=0) for h in heads]
    y = [from_state[h][c:] + dot(bot[h], vu[h]) for h in heads]
    s_new = [s[h] * pc[h] + tn(vu[h], rhs[h] * pc[h]) for h in heads]
    for q in range(nb):
        mine = slice(q * RWKV_HEADS, (q + 1) * RWKV_HEADS)
        y_ref[q] = jnp.concatenate(y[mine], axis=1)
        s_ref[q] = jnp.concatenate(s_new[mine], axis=1)


def _rwkv_post_kernel(y_ref, bonus_ref, g_ref, pg_ref, m_ref, gg_ref, gb_ref, bdm_ref, pj_ref, o_ref):
    y = y_ref[0]
    mu = _head_sums(y, bdm_ref[...])
    d = y - mu
    var = _head_sums(d * d, bdm_ref[...])
    yn = d * lax.rsqrt(var + RWKV_GN_EPS) * gg_ref[...] + gb_ref[...] + bonus_ref[0]
    o = _bdot(yn * g_ref[0], pj_ref[...])
    o_ref[0] = (m_ref[0].astype(F32) + pg_ref[0].astype(F32) * o).astype(o_ref.dtype)


def _rwkv(p, merged, mu_pad, w0, wup_pad, a0, aup_pad, g_up, k_k, k_a, r_k, gn_g, gn_b, proj_bf):
    bsz, seq, _ = p.shape
    row = lambda a: a.reshape(1, -1)
    full = lambda shape: pl.BlockSpec(shape, lambda b, i: (0,) * len(shape))
    head_id = jnp.arange(RWKV_DIM, dtype=I32) // RWKV_HEAD
    bd = (head_id[:, None] == head_id[None, :]).astype(BF16)

    ts = min(seq, 256)
    t_id = jnp.arange(ts, dtype=I32)
    ltri = ((t_id[:, None] // RWKV_CHUNK == t_id[None, :] // RWKV_CHUNK)
            & (t_id[:, None] >= t_id[None, :])).astype(BF16)
    seq_blk = lambda width: pl.BlockSpec((1, ts, width), lambda b, i: (b, i, 0))
    wide = SDS((bsz, seq, RWKV_DIM), F32)
    rt, kkt, kh, bh, v, pinc, bonus, g = pl.pallas_call(
        _rwkv_prep_kernel,
        out_shape=[wide] * 8,
        grid=(bsz, seq // ts),
        in_specs=[pl.BlockSpec((1, ts, 2048), lambda b, i: (b, i, SEG_B // 2048)),
                  full((1, 2048)), full((1, RWKV_DIM)), full((LANES, RWKV_DIM)),
                  full((1, RWKV_DIM)), full((LANES, RWKV_DIM)), full((LORA_G, RWKV_DIM)),
                  full((1, RWKV_DIM)), full((1, RWKV_DIM)), full((1, RWKV_DIM)),
                  full((RWKV_DIM, RWKV_DIM)), full((ts, ts))],
        out_specs=[seq_blk(RWKV_DIM)] * 8,
        scratch_shapes=[pltpu.VMEM((ts + SUBLANES, 2048), F32)],
        compiler_params=_cparams(("arbitrary", "arbitrary")),
        name="rwkv_prep",
    )(p, mu_pad, row(w0), wup_pad, row(a0), aup_pad, g_up, row(k_k), row(k_a), row(r_k), bd, ltri)

    c = RWKV_CHUNK
    nb = 2 if bsz % 2 == 0 else 1
    chunk_blk = pl.BlockSpec((nb, c, RWKV_DIM), lambda b, i: (b, i, 0))
    y = pl.pallas_call(
        _rwkv_scan_kernel,
        out_shape=wide,
        grid=(bsz // nb, seq // c),
        in_specs=[chunk_blk] * 6,
        out_specs=chunk_blk,
        scratch_shapes=[pltpu.VMEM((nb, RWKV_HEAD, RWKV_DIM), F32)],
        compiler_params=_cparams(("arbitrary", "arbitrary")),
        name="rwkv_scan",
    )(rt, kkt, kh, bh, v, pinc)

    tp = min(seq, 512)
    blk = lambda width: pl.BlockSpec((1, tp, width), lambda b, i: (b, i, 0))
    return pl.pallas_call(
        _rwkv_post_kernel,
        out_shape=SDS((bsz, seq, D_MODEL), MERGED_DTYPE),
        grid=(bsz, seq // tp),
        in_specs=[blk(RWKV_DIM), blk(RWKV_DIM), blk(RWKV_DIM),
                  pl.BlockSpec((1, tp, D_MODEL), lambda b, i: (b, i, SEG_G // D_MODEL + 1)),
                  blk(D_MODEL), full((1, RWKV_DIM)), full((1, RWKV_DIM)),
                  full((RWKV_DIM, RWKV_DIM)), full((RWKV_DIM, D_MODEL))],
        out_specs=blk(D_MODEL),
        compiler_params=_cparams(("arbitrary", "arbitrary")),
        name="rwkv_post",
    )(y, bonus, g, p, merged, row(gn_g), row(gn_b), bd * (1.0 / RWKV_HEAD), proj_bf)


def _out_kernel(m_ref, x_ref, g1_ref, w_ref, n2_ref, sc_ref, sh_ref, xo_ref, h_ref):
    xn = x_ref[0] + g1_ref[0] * _bdot(m_ref[0], w_ref[...])
    xo_ref[0] = xn
    h_ref[0] = _modulated_rmsnorm(xn, n2_ref[...], sc_ref[0], sh_ref[0])


def _out_proj(merged, x, g1, w_bf, norm2, sc2, sh2):
    bsz, seq, _ = x.shape
    ts = min(seq, 512)
    blk = pl.BlockSpec((1, ts, D_MODEL), lambda b, i: (b, i, 0))
    per_b = pl.BlockSpec((1, 1, D_MODEL), lambda b, i: (b, 0, 0))
    return pl.pallas_call(
        _out_kernel,
        out_shape=[SDS((bsz, seq, D_MODEL), F32)] * 2,
        grid=(bsz, seq // ts),
        in_specs=[blk, blk, per_b, pl.BlockSpec((D_MODEL, D_MODEL), lambda b, i: (0, 0)),
                  pl.BlockSpec((1, D_MODEL), lambda b, i: (0, 0)), per_b, per_b],
        out_specs=[blk, blk],
        compiler_params=_cparams(("arbitrary", "arbitrary")),
        name="out_proj",
    )(merged, x, g1, w_bf, norm2, sc2, sh2)


MOE_TILE = 256
MOE_SLOTS = 2560
RUN_LOOP_UNROLL = 4
DISPATCH_PIECE = 48
DISPATCH_SPARE = DISPATCH_PIECE - SUBLANES
DISPATCH_WAIT_ROWS = 2048
assert MOE_SLOTS >= TOP_K * MOE_TILE + N_EXPERTS * (SUBLANES - 1) + DISPATCH_SPARE
U32 = jnp.uint32


ROUTE_TILES = 2


def _route_kernel(h_ref, rw_ref, bias_ref, upper_ref, ltri_ref, slot_ref, w_ref, n8_ref):
    tile = MOE_TILE
    subs = range(h_ref.shape[0] // tile)
    neg = -jnp.inf
    shape3 = (GROUP_SIZE, N_GROUPS, tile)
    to3 = lambda a: a.reshape(shape3)
    lanes_of = lambda a: jnp.concatenate([a] * (tile // LANES), axis=1)
    fold = lambda a: jnp.sum(jnp.sum(a, axis=0), axis=0, keepdims=True)
    slab = lax.broadcasted_iota(I32, shape3, 0).astype(F32)
    grp = lax.broadcasted_iota(I32, shape3, 1).astype(F32)
    eid = grp * GROUP_SIZE + slab
    gi = lax.broadcasted_iota(I32, (N_GROUPS, tile), 0).astype(F32)
    ones_cols = jnp.ones((tile, LANES), BF16)
    s3 = [to3(jax.nn.sigmoid(_hdot_nt(rw_ref[...], h_ref[pl.ds(q * tile, tile), :]))) for q in subs]
    b3 = [s3[q] + to3(bias_ref[...]) for q in subs]
    m1 = [jnp.max(b3[q], axis=0, keepdims=True) for q in subs]
    first = [jnp.min(jnp.where(b3[q] == m1[q], slab, GROUP_SIZE), axis=0, keepdims=True) for q in subs]
    m2 = [jnp.max(jnp.where(slab == first[q], neg, b3[q]), axis=0, keepdims=True) for q in subs]
    gs = [(m1[q] + m2[q])[0] for q in subs]
    chosen = [jnp.zeros((N_GROUPS, tile), F32) for q in subs]
    for _ in range(TOPK_GROUPS):
        m = [jnp.max(gs[q], axis=0, keepdims=True) for q in subs]
        hit = [gi == jnp.min(jnp.where(gs[q] == m[q], gi, N_GROUPS), axis=0, keepdims=True) for q in subs]
        chosen = [jnp.where(hit[q], 1.0, chosen[q]) for q in subs]
        gs = [jnp.where(hit[q], neg, gs[q]) for q in subs]
    cur = [jnp.where((chosen[q] > 0.0)[None], b3[q], neg) for q in subs]
    base = [jnp.zeros((N_EXPERTS, LANES), F32) for q in subs]
    picks, w_rows, rank_rows = [[] for q in subs], [[] for q in subs], [[] for q in subs]
    for _ in range(TOP_K):
        m = [jnp.max(jnp.max(cur[q], axis=0), axis=0, keepdims=True)[None] for q in subs]
        pick = [jnp.min(jnp.min(jnp.where(cur[q] == m[q], eid, N_EXPERTS), axis=0), axis=0, keepdims=True)
                for q in subs]
        hit = [eid == pick[q][None] for q in subs]
        onehot = [hit[q].astype(BF16).reshape(N_EXPERTS, tile) for q in subs]
        before = [jnp.dot(onehot[q], upper_ref[...], preferred_element_type=F32) for q in subs]
        count = [jnp.dot(onehot[q], ones_cols, preferred_element_type=F32) for q in subs]
        for q in subs:
            w_rows[q].append(fold(jnp.where(hit[q], s3[q], 0.0)))
            picks[q].append(pick[q])
            rank_rows[q].append(fold(jnp.where(hit[q], to3(before[q] + lanes_of(base[q])), 0.0)))
        cur = [jnp.where(hit[q], neg, cur[q]) for q in subs]
        base = [base[q] + count[q] for q in subs]
    n8 = [jnp.floor((base[q] + (SUBLANES - 1.0)) * (1.0 / SUBLANES)) * SUBLANES for q in subs]
    run_start = [to3(lanes_of(_hdot(ltri_ref[...], n8[q]))) for q in subs]
    slots, weights = [], []
    for q in subs:
        slots.append(jnp.concatenate(
            [rank_rows[q][j] + fold(jnp.where(eid == picks[q][j][None], run_start[q], 0.0)) for j in range(TOP_K)],
            axis=0))
        w_all = jnp.concatenate(w_rows[q], axis=0)
        weights.append(w_all / jnp.sum(w_all, axis=0, keepdims=True) * ROUTED_SCALE)
    for q in subs:
        n8_ref[q] = n8[q]
    w_ref[...] = jnp.concatenate(weights, axis=1)
    slot_ref[...] = jnp.concatenate(slots, axis=1).astype(I32)


def _route(h2, router_w, router_bias):
    n_tok = h2.shape[0]
    tile = MOE_TILE
    n_tiles = n_tok // tile
    per_step = ROUTE_TILES if n_tiles % ROUTE_TILES == 0 else 1
    regroup = lambda a: a.reshape(N_GROUPS, GROUP_SIZE, -1).transpose(1, 0, 2).reshape(N_EXPERTS, -1)
    rw = regroup(router_w.T)
    bias = jnp.broadcast_to(regroup(router_bias.reshape(N_EXPERTS, 1)), (N_EXPERTS, tile))
    t_id = jnp.arange(tile, dtype=I32)
    upper = (t_id[:, None] < t_id[None, :]).astype(BF16)
    e_id = jnp.arange(N_EXPERTS, dtype=I32)
    ltri = (e_id[:, None] > e_id[None, :]).astype(F32)
    tok_blk = pl.BlockSpec((TOP_K, per_step * tile), lambda i: (0, i))
    full = lambda shape: pl.BlockSpec(shape, lambda i: (0,) * len(shape))
    slot_t, w_t, n8 = pl.pallas_call(
        _route_kernel,
        out_shape=[SDS((TOP_K, n_tok), I32), SDS((TOP_K, n_tok), F32),
                   SDS((n_tiles, N_EXPERTS, LANES), F32)],
        grid=(n_tiles // per_step,),
        in_specs=[pl.BlockSpec((per_step * tile, D_MODEL), lambda i: (i, 0)),
                  full((N_EXPERTS, D_MODEL)), full((N_EXPERTS, tile)), full((tile, tile)),
                  full((N_EXPERTS, N_EXPERTS))],
        out_specs=[tok_blk, tok_blk, pl.BlockSpec((per_step, N_EXPERTS, LANES), lambda i: (i, 0, 0))],
        compiler_params=_cparams(("arbitrary",)),
        name="moe_route",
    )(h2, rw, bias, upper, ltri)
    return slot_t, w_t, n8[:, :, 0].astype(I32)


def _for_each_run_piece(tile_idx, n8_ref, fn):
    for r in range(N_EXPERTS):
        n = n8_ref[tile_idx * N_EXPERTS + r]
        size = MOE_TILE
        while size >= SUBLANES:
            @pl.when((n & size) != 0)
            def _(size=size):
                fn(r, n & ~(2 * size - 1), size)
            size //= 2


def _wait_rows(total, make_wait):
    size = SUBLANES
    while size <= MOE_SLOTS:
        @pl.when((total & size) != 0)
        def _(size=size):
            make_wait(size).wait()
        size *= 2


def _pack_pairs(hi_bits, lo_bits):
    return (hi_bits & jnp.uint32(0xFFFF0000)) | (lo_bits >> 16)


def _unpack_pairs(u):
    hi = lax.bitcast_convert_type(u & jnp.uint32(0xFFFF0000), F32)
    lo = lax.bitcast_convert_type(u << 16, F32)
    return jnp.concatenate([hi, lo], axis=1).astype(BF16)


def _dispatch_kernel(n8_ref, off_ref, dst_ref, tot_ref, pad_end_ref, slot_ref, h_ref, xs_ref,
                     g_ref, zero_ref, sem):
    i = pl.program_id(0)
    tile = h_ref.shape[0]
    half = D_MODEL // 2

    @pl.when(i == 0)
    def _():
        zero_ref[...] = jnp.zeros(zero_ref.shape, U32)

        def last_block(e, back):
            start = pl.multiple_of(jnp.maximum(pad_end_ref[e] - back * MOE_ROWS, 0), MOE_ROWS)
            return pltpu.make_async_copy(zero_ref, xs_ref.at[pl.ds(start, MOE_ROWS), :], sem.at[0])

        def has_blocks(e, back):
            prev = jnp.where(e > 0, pad_end_ref[jnp.maximum(e - 1, 0)], 0)
            return pad_end_ref[e] - prev >= back * MOE_ROWS

        def clear(e, carry):
            for back in (1, 2):
                pl.when(has_blocks(e, back))(lambda back=back: last_block(e, back).start())
            return carry
        lax.fori_loop(0, N_EXPERTS, clear, 0)

        def done(e, carry):
            for back in (1, 2):
                pl.when(has_blocks(e, back))(lambda back=back: last_block(e, back).wait())
            return carry
        lax.fori_loop(0, N_EXPERTS, done, 0)

        def tail_block(b):
            start = pl.multiple_of(b * MOE_ROWS, MOE_ROWS)
            return pltpu.make_async_copy(zero_ref, xs_ref.at[pl.ds(start, MOE_ROWS), :], sem.at[0])

        def clear_tail(b, carry):
            tail_block(b).start()
            return carry

        def done_tail(b, carry):
            tail_block(b).wait()
            return carry
        used = pad_end_ref[N_EXPERTS - 1] // MOE_ROWS
        lax.fori_loop(used, xs_ref.shape[0] // MOE_ROWS, clear_tail, 0)
        lax.fori_loop(used, xs_ref.shape[0] // MOE_ROWS, done_tail, 0)

    slot_id = lax.broadcasted_iota(I16, (MOE_SLOTS, tile), 0)
    slots = slot_ref[...].astype(I16)
    sel = jnp.zeros((MOE_SLOTS, tile), BF16)
    for j in range(TOP_K):
        sel = jnp.where(slot_id == slots[j:j + 1, :], jnp.ones((), BF16), sel)
    g = jnp.dot(sel, h_ref[...].astype(BF16), preferred_element_type=F32)
    bits = lax.bitcast_convert_type(g, U32)
    buf = i % 2
    g_ref[buf] = _pack_pairs(bits[:, :half], bits[:, half:])

    def drain(step):
        total = tot_ref[step]
        whole = pltpu.make_async_copy(g_ref.at[step % 2, pl.ds(0, DISPATCH_WAIT_ROWS), :],
                                      xs_ref.at[pl.ds(0, DISPATCH_WAIT_ROWS), :], sem.at[step % 2])

        def wait_whole(k, carry):
            whole.wait()
            return carry
        lax.fori_loop(0, lax.shift_right_logical(total, DISPATCH_WAIT_ROWS.bit_length() - 1), wait_whole, 0)
        _wait_rows(total & (DISPATCH_WAIT_ROWS - 1), lambda rows: pltpu.make_async_copy(
            g_ref.at[step % 2, pl.ds(0, rows), :], xs_ref.at[pl.ds(0, rows), :], sem.at[step % 2]))

    pl.when(i > 0)(lambda: drain(i - 1))

    def copy(r, k):
        src = pl.multiple_of(off_ref[i * N_EXPERTS + r] + k * DISPATCH_PIECE, SUBLANES)
        dst = pl.multiple_of(dst_ref[i * N_EXPERTS + r] + k * DISPATCH_PIECE, SUBLANES)
        pltpu.make_async_copy(g_ref.at[buf, pl.ds(src, DISPATCH_PIECE), :],
                              xs_ref.at[pl.ds(dst, DISPATCH_PIECE), :], sem.at[buf]).start()

    def per_expert(r, carry):
        n = n8_ref[i * N_EXPERTS + r]
        pl.when(n > 0)(lambda: copy(r, 0))

        @pl.when(n > DISPATCH_PIECE)
        def _():
            def more(k, c):
                copy(r, k)
                return c
            lax.fori_loop(1, (n + (DISPATCH_PIECE - 1)) // DISPATCH_PIECE, more, 0)
        return carry
    lax.fori_loop(0, N_EXPERTS, per_expert, 0, unroll=RUN_LOOP_UNROLL)
    pl.when(i == pl.num_programs(0) - 1)(lambda: drain(i))


def _dispatch(h2, slot_t, n8_flat, off_flat, dst_flat, tot, pad_end, n_rows):
    n_tok = h2.shape[0]
    tile = MOE_TILE
    grid_spec = pltpu.PrefetchScalarGridSpec(
        num_scalar_prefetch=5,
        grid=(n_tok // tile,),
        in_specs=[pl.BlockSpec((TOP_K, tile), lambda i, *_: (0, i)),
                  pl.BlockSpec((tile, D_MODEL), lambda i, *_: (i, 0))],
        out_specs=pl.BlockSpec(memory_space=pl.ANY),
        scratch_shapes=[pltpu.VMEM((2, MOE_SLOTS, D_MODEL // 2), U32),
                        pltpu.VMEM((MOE_ROWS, D_MODEL // 2), U32), pltpu.SemaphoreType.DMA((2,))],
    )
    return pl.pallas_call(
        _dispatch_kernel,
        out_shape=SDS((n_rows, D_MODEL // 2), U32),
        grid_spec=grid_spec,
        compiler_params=_cparams(("arbitrary",)),
        name="moe_dispatch",
    )(n8_flat, off_flat, dst_flat, tot, pad_end, slot_t, h2)


def _expert_kernel(first_ref, nblk_ref, exp_of_ref, xs_ref, w1_ref, w3_ref, w2_ref, y_ref,
                   xbuf, ybuf, wb1, wb3, wb2, in_sem, out_sem):
    del exp_of_ref
    r = pl.program_id(0)
    half = D_MODEL // 2
    used = first_ref[N_EXPERTS - 1] + nblk_ref[N_EXPERTS - 1]
    part = MOE_ROWS // EXPERT_DMA_PARTS

    class _Copies:
        def __init__(self, make):
            self.parts = [make(p) for p in range(EXPERT_DMA_PARTS)]

        def start(self):
            for c in self.parts:
                c.start()

        def wait(self):
            for c in self.parts:
                c.wait()

    def hbm_rows(g, p):
        return pl.ds(pl.multiple_of(g * MOE_ROWS + p * part, part), part)

    in_copy = lambda g, slot: _Copies(lambda p: pltpu.make_async_copy(
        xs_ref.at[hbm_rows(g, p), :], xbuf.at[slot, pl.ds(p * part, part), :], in_sem.at[slot]))
    out_copy = lambda g, slot: _Copies(lambda p: pltpu.make_async_copy(
        ybuf.at[slot, pl.ds(p * part, part), :], y_ref.at[hbm_rows(g, p), :], out_sem.at[slot]))

    @pl.when(r == 0)
    def _():
        for g in range(EXPERT_BUFS - 1):
            pl.when(g < used)(lambda g=g: in_copy(g, g).start())

    wb1[...] = w1_ref[0, 0].astype(BF16)
    wb3[...] = w3_ref[0, 0].astype(BF16)
    wb2[...] = w2_ref[0, 0].astype(BF16)

    def block(k, carry):
        g = first_ref[r] + k
        slot = g % EXPERT_BUFS
        in_copy(g, slot).wait()
        ahead = g + (EXPERT_BUFS - 1)
        pl.when(ahead < used)(lambda: in_copy(ahead, ahead % EXPERT_BUFS).start())
        pl.when(g >= EXPERT_BUFS)(lambda: out_copy(g - EXPERT_BUFS, slot).wait())
        dot = lambda a, b: jnp.dot(a, b, preferred_element_type=F32)
        x = _unpack_pairs(xbuf[slot])
        hid = (jax.nn.silu(dot(x, wb1[...])) * dot(x, wb3[...])).astype(BF16)
        y = dot(hid, wb2[...]).astype(BF16).astype(F32)
        bits = lax.bitcast_convert_type(y, U32)
        ybuf[slot] = _pack_pairs(bits[:, :half], bits[:, half:])
        out_copy(g, slot).start()
        return carry
    lax.fori_loop(0, nblk_ref[r], block, 0)

    @pl.when(r == N_EXPERTS - 1)
    def _():
        for back in range(EXPERT_BUFS, 0, -1):
            pl.when(used >= back)(
                lambda back=back: out_copy(used - back, (used - back) % EXPERT_BUFS).wait())

        ybuf[0] = jnp.zeros(ybuf.shape[1:], U32)
        n_blocks = y_ref.shape[0] // MOE_ROWS

        def clear(g, carry):
            out_copy(g, 0).start()
            return carry

        def done(g, carry):
            out_copy(g, 0).wait()
            return carry
        lax.fori_loop(used, n_blocks, clear, 0)
        lax.fori_loop(used, n_blocks, done, 0)


def _experts(xs, first_block, n_block, exp_of_row, w1, w3, w2, layer):
    n_rows = xs.shape[0]
    half = D_MODEL // 2
    w_in_blk = pl.BlockSpec((1, 1, D_MODEL, EXPERT_FF), lambda r, first, nblk, eo: (layer, eo[r], 0, 0))
    grid_spec = pltpu.PrefetchScalarGridSpec(
        num_scalar_prefetch=3,
        grid=(N_EXPERTS,),
        in_specs=[pl.BlockSpec(memory_space=pl.ANY), w_in_blk, w_in_blk,
                  pl.BlockSpec((1, 1, EXPERT_FF, D_MODEL), lambda r, first, nblk, eo: (layer, eo[r], 0, 0))],
        out_specs=pl.BlockSpec(memory_space=pl.ANY),
        scratch_shapes=[pltpu.VMEM((EXPERT_BUFS, MOE_ROWS, half), U32),
                        pltpu.VMEM((EXPERT_BUFS, MOE_ROWS, half), U32),
                        pltpu.VMEM((D_MODEL, EXPERT_FF), BF16), pltpu.VMEM((D_MODEL, EXPERT_FF), BF16),
                        pltpu.VMEM((EXPERT_FF, D_MODEL), BF16),
                        pltpu.SemaphoreType.DMA((EXPERT_BUFS,)), pltpu.SemaphoreType.DMA((EXPERT_BUFS,))],
    )
    return pl.pallas_call(
        _expert_kernel,
        out_shape=SDS((n_rows, half), U32),
        grid_spec=grid_spec,
        compiler_params=_cparams(("arbitrary",)),
        name="moe_experts",
    )(first_block, n_block, exp_of_row, xs, w1, w3, w2)


def _combine_kernel(n8_ref, off_ref, src_ref, tot_ref, y_ref, slot_ref, w_ref, h_ref, x_ref, g2_ref,
                    s1_ref, s3_ref, s2_ref, fn_ref, o_ref, yt_ref, sem, *, final):
    tile = h_ref.shape[1]
    i = pl.program_id(0) * pl.num_programs(1) + pl.program_id(1)

    @pl.when(i == 0)
    def _():
        yt_ref[...] = jnp.zeros(yt_ref.shape, U32)

    def piece(r, offset, rows):
        src = pl.multiple_of(src_ref[i * N_EXPERTS + r] + offset, SUBLANES)
        dst = pl.multiple_of(off_ref[i * N_EXPERTS + r] + offset, SUBLANES)
        pltpu.make_async_copy(y_ref.at[pl.ds(src, rows), :], yt_ref.at[pl.ds(dst, rows), :], sem).start()

    _for_each_run_piece(i, n8_ref, piece)
    h = h_ref[0]
    shared = _bdot(jax.nn.silu(_bdot(h, s1_ref[...])) * _bdot(h, s3_ref[...]), s2_ref[...])
    slot_id = lax.broadcasted_iota(I16, (tile, MOE_SLOTS), 1)
    slots = slot_ref[0].astype(I16)
    w = w_ref[0].astype(BF16)
    pw = jnp.zeros((tile, MOE_SLOTS), BF16)
    for j in range(TOP_K):
        pw = jnp.where(slot_id == slots[:, j:j + 1], w[:, j:j + 1], pw)
    _wait_rows(tot_ref[i], lambda rows: pltpu.make_async_copy(
        y_ref.at[pl.ds(0, rows), :], yt_ref.at[pl.ds(0, rows), :], sem))
    routed = jnp.dot(pw, _unpack_pairs(yt_ref[...]), preferred_element_type=F32)
    xn = x_ref[0] + g2_ref[0] * (routed + shared)
    if final:
        xn = xn * lax.rsqrt(jnp.mean(xn * xn, axis=-1, keepdims=True) + NORM_EPS) * fn_ref[...]
    o_ref[0] = xn


def _combine(y, slot_nat, w_nat, n8_flat, off_flat, dst_flat, tot, h2, x, g2, s1_bf, s3_bf, s2_bf,
             final_norm, final):
    bsz, seq, _ = x.shape
    tile = MOE_TILE
    per_seq = seq // tile
    blk = pl.BlockSpec((1, tile, D_MODEL), lambda b, i, *_: (b, i, 0))
    tok = pl.BlockSpec((1, tile, TOP_K), lambda b, i, *_: (b, i, 0))
    full = lambda shape: pl.BlockSpec(shape, lambda b, i, *_: (0,) * len(shape))
    grid_spec = pltpu.PrefetchScalarGridSpec(
        num_scalar_prefetch=4,
        grid=(bsz, per_seq),
        in_specs=[pl.BlockSpec(memory_space=pl.ANY), tok, tok, blk, blk,
                  pl.BlockSpec((1, 1, D_MODEL), lambda b, i, *_: (b, 0, 0)),
                  full((D_MODEL, SHARED_FF)), full((D_MODEL, SHARED_FF)), full((SHARED_FF, D_MODEL)),
                  full((1, D_MODEL))],
        out_specs=blk,
        scratch_shapes=[pltpu.VMEM((MOE_SLOTS, D_MODEL // 2), U32), pltpu.SemaphoreType.DMA],
    )
    return pl.pallas_call(
        functools.partial(_combine_kernel, final=final),
        out_shape=SDS((bsz, seq, D_MODEL), F32),
        grid_spec=grid_spec,
        compiler_params=_cparams(("arbitrary", "arbitrary")),
        name="moe_combine",
    )(n8_flat, off_flat, dst_flat, tot, y, slot_nat, w_nat, h2, x, g2, s1_bf, s3_bf, s2_bf, final_norm)


def _moe(x, h2, g2, router_w, router_bias, w1, w3, w2, layer, s1_bf, s3_bf, s2_bf, final_norm, final):
    bsz, seq, _ = x.shape
    assert seq % MOE_TILE == 0, "token tiles must not straddle sequences"
    n_tok = bsz * seq
    slot_t, w_t, n8 = _route(h2.reshape(n_tok, D_MODEL), router_w, router_bias)
    n_tiles = n8.shape[0]
    counts = jnp.sum(n8, axis=0)
    padded = jnp.where(counts > 0, (counts + DISPATCH_SPARE + MOE_ROWS - 1) // MOE_ROWS * MOE_ROWS, 0)
    pad_end = jnp.cumsum(padded).astype(I32)
    pad_start = pad_end - padded
    run_row = (pad_start[None, :] + jnp.cumsum(n8, axis=0) - n8).astype(I32)
    run_slot = (jnp.cumsum(n8, axis=1) - n8).astype(I32)
    tot = jnp.sum(n8, axis=1).astype(I32)
    copied = jnp.sum((n8 + DISPATCH_PIECE - 1) // DISPATCH_PIECE * DISPATCH_PIECE, axis=1).astype(I32)
    max_rows = (n_tok * TOP_K + n_tiles * N_EXPERTS * (SUBLANES - 1)
                + N_EXPERTS * (DISPATCH_SPARE + MOE_ROWS - 1))
    n_blocks = (max_rows + MOE_ROWS - 1) // MOE_ROWS
    row_id = jnp.arange(N_EXPERTS, dtype=I32)
    exp_of_row = (row_id % N_GROUPS) * GROUP_SIZE + row_id // N_GROUPS
    flat = lambda a: a.reshape(-1).astype(I32)
    xs = _dispatch(h2.reshape(n_tok, D_MODEL), slot_t, flat(n8), flat(run_slot), flat(run_row), copied,
                   pad_end, n_blocks * MOE_ROWS)
    y = _experts(xs, (pad_start // MOE_ROWS).astype(I32), (padded // MOE_ROWS).astype(I32), exp_of_row,
                 w1, w3, w2, layer)
    nat = lambda a: a.T.reshape(bsz, seq, TOP_K)
    return _combine(y, nat(slot_t), nat(w_t), flat(n8), flat(run_slot), flat(run_row), tot, h2, x, g2,
                    s1_bf, s3_bf, s2_bf, final_norm, final)


def _pad_cols(a, width):
    return jnp.pad(a, ((0, 0), (0, width - a.shape[1])))


def _layout_w_in(w_in):
    b0 = COLS_A
    c0 = COLS_A + COLS_B
    g0 = c0 + COLS_C
    seg_b = _pad_cols(w_in[:, b0:c0], SEG_G - SEG_B)
    return jnp.concatenate([w_in[:, c0:g0], seg_b, w_in[:, g0:], w_in[:, :b0]], axis=1).astype(BF16)


def _layout_mu(mu):
    return _pad_cols(mu.reshape(1, -1), SEG_G - SEG_B)


def _pad_rows(a, height):
    return jnp.pad(a, ((0, height - a.shape[0]), (0, 0)))


def kernel(x, c, ada_w, ada_b, norm1, norm2, w_in, conv_a_w, conv_a_b, ln_a_g, ln_a_b, proj_a, mu_b, w0, w_up, a0, a_up, g_up, k_k, k_a, r_k, gn_b_g, gn_b_b, proj_b, conv_c_w, conv_c_b, lru_wa, lru_ba, lru_wx, lru_bx, lru_lambda, proj_c, w_out, router_w, router_bias, exp_w1, exp_w3, exp_w2, sh_w1, sh_w3, sh_w2, final_norm):
    depth = ada_w.shape[0]
    bsz = x.shape[0]
    mod = _ada_mod(c, ada_w, ada_b)
    for l in range(depth):
        sh1, sc1, g1, sh2, sc2, g2 = [mod[l, :, i * D_MODEL:(i + 1) * D_MODEL].reshape(bsz, 1, D_MODEL)
                                      for i in range(N_MOD)]
        p = _in_proj(x, sc1, sh1, norm1[l].reshape(1, -1), _layout_w_in(w_in[l]))
        merged = _conv_a(p, conv_a_w[l], conv_a_b[l], ln_a_g[l], ln_a_b[l], proj_a[l].astype(BF16))
        merged = _rwkv(p, merged, _layout_mu(mu_b[l]), w0[l], _pad_rows(w_up[l], LANES).astype(BF16),
                       a0[l], jnp.pad(a_up[l], ((LORA_W, 0), (0, 0))).astype(BF16), g_up[l].astype(BF16),
                       k_k[l], k_a[l], r_k[l], gn_b_g[l], gn_b_b[l], proj_b[l].astype(BF16))
        merged = _lru(p, merged, conv_c_w[l], conv_c_b[l], lru_wa[l].astype(BF16), lru_ba[l],
                      lru_wx[l].astype(BF16), lru_bx[l], lru_lambda[l], proj_c[l].astype(BF16))
        x, h2 = _out_proj(merged, x, g1, w_out[l].astype(BF16), norm2[l].reshape(1, -1), sc2, sh2)
        x = _moe(x, h2, g2, router_w[l], router_bias[l], exp_w1, exp_w3, exp_w2, l,
                 sh_w1[l].astype(BF16), sh_w3[l].astype(BF16), sh_w2[l].astype(BF16),
                 final_norm.reshape(1, -1), final=(l == depth - 1))
    return x
```

```python
import functools

import jax
import jax.numpy as jnp
from jax import lax
from jax.experimental import pallas as pl
from jax.experimental.pallas import tpu as pltpu

F32 = jnp.float32
BF16 = jnp.bfloat16
I32 = jnp.int32
I16 = jnp.int16
SDS = jax.ShapeDtypeStruct
HIGHEST = lax.Precision.HIGHEST

D_MODEL = 1024
N_MOD = 6
NORM_EPS = 1e-6
CONV_A_CH = 512
CONV_A_WIDTH = 31
CONV_A_LN_EPS = 1e-5
RWKV_HEADS = 8
RWKV_HEAD = 64
RWKV_DIM = RWKV_HEADS * RWKV_HEAD
LORA_W = 64
LORA_A = 64
LORA_G = 128
RWKV_GN_EPS = 64e-5
RWKV_CHUNK = 64
LRU_DIM = 1024
LRU_HEADS = 8
LRU_BLOCK = LRU_DIM // LRU_HEADS
LRU_CONV = 4
LRU_C = 8.0
N_EXPERTS = 64
TOP_K = 8
N_GROUPS = 8
GROUP_SIZE = N_EXPERTS // N_GROUPS
TOPK_GROUPS = 4
EXPERT_FF = 256
SHARED_FF = 256
ROUTED_SCALE = 2.5
SEG_C = 0
SEG_B = 2048
SEG_G = 4096
SEG_A = 7168
IN_COLS_PAD = 8192
RW_R, RW_K, RW_V, RW_XWA, RW_XG = 0, 512, 1024, 1536, 1664
COLS_A = 2 * CONV_A_CH
COLS_B = 3 * RWKV_DIM + LORA_W + LORA_A + LORA_G
COLS_C = 2 * LRU_DIM
VMEM_LIMIT = 56 * 1024 * 1024
MERGED_DTYPE = BF16
SUBLANES = 8
LANES = 128
MOE_ROWS = 512
EXPERT_DMA_PARTS = 4
EXPERT_AHEAD = 4
EXPERT_BUFS = EXPERT_AHEAD + 2


def _cparams(sem):
    return pltpu.CompilerParams(dimension_semantics=sem, vmem_limit_bytes=VMEM_LIMIT)


def _bdot(a, b):
    return jnp.dot(a.astype(BF16), b.astype(BF16), preferred_element_type=F32)


def _hdot(a, b):
    return jnp.dot(a, b, preferred_element_type=F32, precision=HIGHEST)


def _split(a):
    hi = a.astype(BF16)
    return hi, (a - hi.astype(F32)).astype(BF16)


def _head_sums(a, ones_bf):
    hi, lo = _split(a)
    return (jnp.dot(hi, ones_bf, preferred_element_type=F32)
            + jnp.dot(lo, ones_bf, preferred_element_type=F32))


def _hdot_nt(a, b):
    return lax.dot_general(a, b, (((1,), (1,)), ((), ())), preferred_element_type=F32,
                           precision=HIGHEST)


def _ada_kernel(c_ref, w_ref, b_ref, o_ref):
    cond = jax.nn.silu(c_ref[...])
    o_ref[0] = _bdot(cond, w_ref[0]) + b_ref[0]


def _ada_mod(c, ada_w, ada_b):
    depth, _, n = ada_w.shape
    bsz = c.shape[0]
    tn = 1536
    return pl.pallas_call(
        _ada_kernel,
        out_shape=SDS((depth, bsz, n), F32),
        grid=(depth, n // tn),
        in_specs=[pl.BlockSpec((bsz, D_MODEL), lambda l, j: (0, 0)),
                  pl.BlockSpec((1, D_MODEL, tn), lambda l, j: (l, 0, j)),
                  pl.BlockSpec((1, 1, tn), lambda l, j: (l, 0, j))],
        out_specs=pl.BlockSpec((1, bsz, tn), lambda l, j: (l, 0, j)),
        compiler_params=_cparams(("arbitrary", "arbitrary")),
        name="ada_mod",
    )(c, ada_w, ada_b.reshape(depth, 1, n))


def _modulated_rmsnorm(x, g, sc, sh):
    y = x * lax.rsqrt(jnp.mean(x * x, axis=-1, keepdims=True) + NORM_EPS)
    return (y * g) * (1.0 + sc) + sh


IN_PROJ_SUB = 512


def _in_kernel(x_ref, sc_ref, sh_ref, g_ref, w_ref, o_ref, h_ref):
    @pl.when(pl.program_id(2) == 0)
    def _():
        h_ref[...] = _modulated_rmsnorm(x_ref[0], g_ref[...], sc_ref[0], sh_ref[0]).astype(BF16)

    j = pl.program_id(2)
    ts, tn = o_ref.shape[1], o_ref.shape[2]
    is_gelu = j == SEG_C // tn
    is_gate = (j >= SEG_G // tn) & (j < SEG_A // tn)
    is_glu = j == SEG_A // tn

    def glu(acc):
        u = acc[:, :tn // 2] * jax.nn.sigmoid(acc[:, tn // 2:])
        return jnp.concatenate([u, jnp.zeros_like(u)], axis=1)

    def emit(act, sub):
        for r0 in range(0, ts, sub):
            rows = pl.ds(r0, sub)
            acc = jnp.dot(h_ref[rows, :], w_ref[...], preferred_element_type=F32)
            o_ref[0, rows, :] = act(acc).astype(o_ref.dtype)

    sub = min(ts, IN_PROJ_SUB)
    pl.when(is_gelu)(lambda: emit(functools.partial(jax.nn.gelu, approximate=True), sub))
    pl.when(is_gate)(lambda: emit(jax.nn.sigmoid, sub))
    pl.when(is_glu)(lambda: emit(glu, sub))
    pl.when(jnp.logical_not(is_gelu | is_gate | is_glu))(lambda: emit(lambda acc: acc, ts))


def _in_proj(x, sc, sh, g, w_pad):
    bsz, seq, _ = x.shape
    ts = min(seq, 1024)
    tn = 1024
    return pl.pallas_call(
        _in_kernel,
        out_shape=SDS((bsz, seq, IN_COLS_PAD), BF16),
        grid=(bsz, seq // ts, IN_COLS_PAD // tn),
        in_specs=[pl.BlockSpec((1, ts, D_MODEL), lambda b, i, j: (b, i, 0)),
                  pl.BlockSpec((1, 1, D_MODEL), lambda b, i, j: (b, 0, 0)),
                  pl.BlockSpec((1, 1, D_MODEL), lambda b, i, j: (b, 0, 0)),
                  pl.BlockSpec((1, D_MODEL), lambda b, i, j: (0, 0)),
                  pl.BlockSpec((D_MODEL, tn), lambda b, i, j: (0, j))],
        out_specs=pl.BlockSpec((1, ts, tn), lambda b, i, j: (b, i, j)),
        scratch_shapes=[pltpu.VMEM((ts, D_MODEL), BF16)],
        compiler_params=_cparams(("arbitrary", "arbitrary", "arbitrary")),
        name="in_proj",
    )(x, sc, sh, g, w_pad)


CONV_A_HALO = 32
CONV_A_SUB = 64


def _conv_a_kernel(pa_ref, pg_ref, cw_ref, cb_ref, lg_ref, lb_ref, pj_ref, o_ref, ext_ref, sh_ref, y_ref):
    ts = pa_ref.shape[1]

    @pl.when(pl.program_id(1) == 0)
    def _():
        ext_ref[pl.ds(0, CONV_A_HALO), :] = jnp.zeros((CONV_A_HALO, CONV_A_CH), F32)

    ext_ref[pl.ds(CONV_A_HALO, ts), :] = pa_ref[0].astype(F32)
    for p in range(1, SUBLANES):
        sh_ref[p - 1] = ext_ref[pl.ds(p, sh_ref.shape[1]), :]
    first = CONV_A_HALO - (CONV_A_WIDTH - 1)
    for r0 in range(0, ts, CONV_A_SUB):
        acc = jnp.zeros((CONV_A_SUB, CONV_A_CH), F32) + cb_ref[...]
        for j in range(CONV_A_WIDTH):
            phase = (first + j) % SUBLANES
            rows = pl.ds(r0 + first + j - phase, CONV_A_SUB)
            tap = ext_ref[rows, :] if phase == 0 else sh_ref[phase - 1, rows, :]
            acc = acc + tap * cw_ref[pl.ds(j, 1), :]
        y_ref[pl.ds(r0, CONV_A_SUB), :] = acc
    ext_ref[pl.ds(0, CONV_A_HALO), :] = ext_ref[pl.ds(ts, CONV_A_HALO), :]
    y = y_ref[...]
    mu = jnp.mean(y, axis=-1, keepdims=True)
    d = y - mu
    var = jnp.mean(d * d, axis=-1, keepdims=True)
    yn = d * lax.rsqrt(var + CONV_A_LN_EPS) * lg_ref[...] + lb_ref[...]
    o = _bdot(jax.nn.silu(yn), pj_ref[...])
    o_ref[0] = (pg_ref[0].astype(F32) * o).astype(o_ref.dtype)


def _conv_a(p, conv_w, conv_b, ln_g, ln_b, proj_bf):
    bsz, seq, _ = p.shape
    ts = min(seq, 512)
    row = lambda a: a.reshape(1, -1)
    full = lambda shape: pl.BlockSpec(shape, lambda b, i: (0,) * len(shape))
    return pl.pallas_call(
        _conv_a_kernel,
        out_shape=SDS((bsz, seq, D_MODEL), MERGED_DTYPE),
        grid=(bsz, seq // ts),
        in_specs=[pl.BlockSpec((1, ts, CONV_A_CH), lambda b, i: (b, i, SEG_A // CONV_A_CH)),
                  pl.BlockSpec((1, ts, D_MODEL), lambda b, i: (b, i, SEG_G // D_MODEL)),
                  full((CONV_A_WIDTH, CONV_A_CH)), full((1, CONV_A_CH)), full((1, CONV_A_CH)),
                  full((1, CONV_A_CH)), full((CONV_A_CH, D_MODEL))],
        out_specs=pl.BlockSpec((1, ts, D_MODEL), lambda b, i: (b, i, 0)),
        scratch_shapes=[pltpu.VMEM((ts + CONV_A_HALO, CONV_A_CH), F32),
                        pltpu.VMEM((SUBLANES - 1, ts + CONV_A_HALO - SUBLANES, CONV_A_CH), F32),
                        pltpu.VMEM((ts, CONV_A_CH), F32)],
        compiler_params=_cparams(("arbitrary", "arbitrary")),
        name="conv_a",
    )(p, p, conv_w, row(conv_b), row(ln_g), row(ln_b), proj_bf)


def _lru_kernel(pc_ref, pg_ref, m_ref, cw_ref, cb_ref, wa_ref, ba_ref, wx_ref, bx_ref, lam_ref,
                pj_ref, o_ref, ext_ref, h_ref, a_ref, b_ref):
    ts = pc_ref.shape[1]
    groups = ts // SUBLANES

    @pl.when(pl.program_id(1) == 0)
    def _():
        ext_ref[pl.ds(0, SUBLANES), :] = jnp.zeros((SUBLANES, LRU_DIM), F32)
        h_ref[...] = jnp.zeros((SUBLANES, LRU_DIM), F32)

    pc = pc_ref[0].astype(F32)
    y_gate = pc[:, :LRU_DIM]
    ext_ref[pl.ds(SUBLANES, ts), :] = pc[:, LRU_DIM:]
    first = SUBLANES - (LRU_CONV - 1)
    xc = jnp.zeros((ts, LRU_DIM), F32) + cb_ref[...]
    for j in range(LRU_CONV):
        xc = xc + ext_ref[pl.ds(first + j, ts), :] * cw_ref[pl.ds(j, 1), :]
    ext_ref[pl.ds(0, SUBLANES), :] = ext_ref[pl.ds(ts, SUBLANES), :]

    def block_diag(w_ref):
        return jnp.concatenate(
            [_bdot(xc[:, h * LRU_BLOCK:(h + 1) * LRU_BLOCK], w_ref[h]) for h in range(LRU_HEADS)],
            axis=1)

    gate_a = jax.nn.sigmoid(block_diag(wa_ref) + ba_ref[...])
    gate_x = jax.nn.sigmoid(block_diag(wx_ref) + bx_ref[...])
    log_a = -LRU_C * gate_a * jax.nn.softplus(-lam_ref[...])
    a = jnp.exp(log_a)
    b = xc * gate_x * jnp.sqrt(1.0 - jnp.exp(2.0 * log_a))

    a3 = a.reshape(groups, SUBLANES, LRU_DIM)
    b3 = b.reshape(groups, SUBLANES, LRU_DIM)
    row = lax.broadcasted_iota(I32, (groups, SUBLANES, LRU_DIM), 1)
    for s in (1, 2, 4):
        keep = row >= s
        b3 = jnp.where(keep, a3 * pltpu.roll(b3, s, axis=1) + b3, b3)
        a3 = jnp.where(keep, a3 * pltpu.roll(a3, s, axis=1), a3)
    a_ref[...] = a3.reshape(ts, LRU_DIM)
    b_ref[...] = b3.reshape(ts, LRU_DIM)
    h = h_ref[...]
    for g in range(groups):
        rows = pl.ds(g * SUBLANES, SUBLANES)
        hg = a_ref[rows, :] * h + b_ref[rows, :]
        b_ref[rows, :] = hg
        h = jnp.broadcast_to(hg[SUBLANES - 1:SUBLANES, :], (SUBLANES, LRU_DIM))
    h_ref[...] = h
    o = _bdot(b_ref[...] * y_gate, pj_ref[...])
    o_ref[0] = (m_ref[0].astype(F32) + pg_ref[0].astype(F32) * o).astype(o_ref.dtype)


def _lru(p, merged, conv_w, conv_b, wa_bf, ba, wx_bf, bx, lam, proj_bf):
    bsz, seq, _ = p.shape
    ts = min(seq, 256)
    row = lambda a: a.reshape(1, -1)
    full = lambda shape: pl.BlockSpec(shape, lambda b, i: (0,) * len(shape))
    return pl.pallas_call(
        _lru_kernel,
        out_shape=SDS((bsz, seq, D_MODEL), MERGED_DTYPE),
        grid=(bsz, seq // ts),
        in_specs=[pl.BlockSpec((1, ts, 2 * LRU_DIM), lambda b, i: (b, i, SEG_C // (2 * LRU_DIM))),
                  pl.BlockSpec((1, ts, D_MODEL), lambda b, i: (b, i, SEG_G // D_MODEL + 2)),
                  pl.BlockSpec((1, ts, D_MODEL), lambda b, i: (b, i, 0)),
                  full((LRU_CONV, LRU_DIM)), full((1, LRU_DIM)),
                  full((LRU_HEADS, LRU_BLOCK, LRU_BLOCK)), full((1, LRU_DIM)),
                  full((LRU_HEADS, LRU_BLOCK, LRU_BLOCK)), full((1, LRU_DIM)),
                  full((1, LRU_DIM)), full((LRU_DIM, D_MODEL))],
        out_specs=pl.BlockSpec((1, ts, D_MODEL), lambda b, i: (b, i, 0)),
        scratch_shapes=[pltpu.VMEM((ts + SUBLANES, LRU_DIM), F32),
                        pltpu.VMEM((SUBLANES, LRU_DIM), F32),
                        pltpu.VMEM((ts, LRU_DIM), F32),
                        pltpu.VMEM((ts, LRU_DIM), F32)],
        compiler_params=_cparams(("arbitrary", "arbitrary")),
        name="rg_lru",
    )(p, p, merged, conv_w, row(conv_b), wa_bf, row(ba), wx_bf, row(bx), row(lam), proj_bf)


def _rwkv_prep_kernel(pb_ref, mu_ref, w0_ref, wup_ref, a0_ref, aup_ref, gup_ref, kk_ref, ka_ref,
                      rk_ref, bd_ref, ltri_ref,
                      rt_ref, kkt_ref, kh_ref, bh_ref, v_ref, pinc_ref, bonus_ref, g_ref, ext_ref):
    ts = pb_ref.shape[1]

    @pl.when(pl.program_id(1) == 0)
    def _():
        ext_ref[pl.ds(0, SUBLANES), :] = jnp.zeros((SUBLANES, ext_ref.shape[1]), F32)

    p = pb_ref[0].astype(F32)
    ext_ref[pl.ds(SUBLANES, ts), :] = p
    prev = ext_ref[pl.ds(SUBLANES - 1, ts), :]
    ext_ref[pl.ds(0, SUBLANES), :] = ext_ref[pl.ds(ts, SUBLANES), :]
    pm = p + (prev - p) * mu_ref[...]
    r = pm[:, RW_R:RW_R + RWKV_DIM]
    k = pm[:, RW_K:RW_K + RWKV_DIM]
    v = pm[:, RW_V:RW_V + RWKV_DIM]
    xwa = pm[:, RW_XWA:RW_XWA + LANES]
    xg = pm[:, RW_XG:RW_XG + LORA_G]
    w = -jax.nn.softplus(-(w0_ref[...] + _bdot(jnp.tanh(xwa), wup_ref[...]))) - 0.5
    lw = -jnp.exp(w)
    a = jax.nn.sigmoid(a0_ref[...] + _bdot(xwa, aup_ref[...]))
    g_ref[0] = _bdot(jax.nn.sigmoid(xg), gup_ref[...])
    kkr = k * kk_ref[...]
    ss = _head_sums(kkr * kkr, bd_ref[...])
    kk = kkr / jnp.maximum(jnp.sqrt(ss), 1e-12)
    k2 = k * (1.0 + (a - 1.0) * ka_ref[...])
    lw_hi = lw.astype(BF16)
    lw_mid, lw_lo = _split(lw - lw_hi.astype(F32))
    tri = ltri_ref[...]
    lcum = (jnp.dot(tri, lw_hi, preferred_element_type=F32) + jnp.dot(tri, lw_mid, preferred_element_type=F32)
            + jnp.dot(tri, lw_lo, preferred_element_type=F32))
    pinc = jnp.exp(lcum)
    pinv = jnp.exp(-lcum)
    rt_ref[0] = r * pinc
    kkt_ref[0] = kk * jnp.exp(lcum - lw)
    kh_ref[0] = k2 * pinv
    bh_ref[0] = kk * a * pinv
    v_ref[0] = v
    pinc_ref[0] = pinc
    bonus_ref[0] = _head_sums(r * k2 * rk_ref[...], bd_ref[...]) * v


def _dot3(a, b):
    d = lambda x, y: jnp.dot(x, y, preferred_element_type=F32)
    m = a[0].shape[0]
    both = d(jnp.concatenate([a[0], a[1]], axis=0), b[0])
    return both[:m] + both[m:] + d(a[0], b[1])


def _rwkv_scan_kernel(rt_ref, kkt_ref, kh_ref, bh_ref, v_ref, pinc_ref, y_ref, s_ref):
    c = RWKV_CHUNK
    n = RWKV_HEAD
    nb = rt_ref.shape[0]
    heads = range(nb * RWKV_HEADS)

    @pl.when(pl.program_id(1) == 0)
    def _():
        s_ref[...] = jnp.zeros(s_ref.shape, F32)

    row = lax.broadcasted_iota(I32, (c, c), 0)
    col = lax.broadcasted_iota(I32, (c, c), 1)
    eye = (row == col).astype(F32)
    same16 = (row // 16) == (col // 16)
    same32 = (row // 32) == (col // 32)
    row2 = lax.broadcasted_iota(I32, (c, 2 * c), 0)
    col2 = lax.broadcasted_iota(I32, (c, 2 * c), 1) % c
    nt = lambda a, b: lax.dot_general(a, b, (((1,), (1,)), ((), ())), preferred_element_type=F32)
    tn = lambda a, b: lax.dot_general(a, b, (((0,), (0,)), ((), ())), preferred_element_type=F32)
    dot = lambda a, b: jnp.dot(a, b, preferred_element_type=F32)
    sl = [pl.ds((h % RWKV_HEADS) * n, n) for h in heads]
    sq = [h // RWKV_HEADS for h in heads]
    v = [v_ref[sq[h], :, sl[h]] for h in heads]
    pc = [pinc_ref[sq[h], pl.ds(c - 1, 1), sl[h]] for h in heads]
    s = [s_ref[sq[h], :, sl[h]] for h in heads]
    lhs = [jnp.concatenate([kkt_ref[sq[h], :, sl[h]], rt_ref[sq[h], :, sl[h]]], axis=0) for h in heads]
    rhs = [jnp.concatenate([bh_ref[sq[h], :, sl[h]], kh_ref[sq[h], :, sl[h]]], axis=0) for h in heads]
    big = [nt(lhs[h], rhs[h]) for h in heads]
    from_state = [nt(lhs[h], s[h]) for h in heads]
    top = [jnp.where(row2 > col2, big[h][:c], 0.0) for h in heads]
    bot = [jnp.where(row2 >= col2, big[h][c:], 0.0) for h in heads]
    a_b = [top[h][:, :c] for h in heads]
    akv = [dot(top[h], jnp.concatenate([jnp.zeros((c, n), F32), v[h]], axis=0)) for h in heads]
    d16 = [jnp.where(same16, a_b[h], 0.0) for h in heads]
    sd = [_split(d16[h]) for h in heads]
    s2 = [_split(_dot3(sd[h], sd[h])) for h in heads]
    s4 = [_split(_dot3(s2[h], s2[h])) for h in heads]
    s8 = [_split(_dot3(s4[h], s4[h])) for h in heads]
    t = [eye - d16[h] for h in heads]
    for sp in (s2, s4, s8):
        t = [t[h] + _dot3(_split(t[h]), sp[h]) for h in heads]
    for off in ([jnp.where(same32 & jnp.logical_not(same16), a_b[h], 0.0) for h in heads],
                [jnp.where(same32, 0.0, a_b[h]) for h in heads]):
        tb = [t[h].astype(BF16) for h in heads]
        lt = [dot(off[h].astype(BF16), tb[h]).astype(BF16) for h in heads]
        t = [t[h] - dot(tb[h], lt[h]) for h in heads]
    u = [dot(t[h], from_state[h][:c] + akv[h]) for h in heads]
    vu = [jnp.concatenate([-u[h], v[h]], axis=0) for h in heads]
    y = [from_state[h][c:] + dot(bot[h], vu[h]) for h in heads]
    s_new = [s[h] * pc[h] + tn(vu[h], rhs[h] * pc[h]) for h in heads]
    for q in range(nb):
        mine = slice(q * RWKV_HEADS, (q + 1) * RWKV_HEADS)
        y_ref[q] = jnp.concatenate(y[mine], axis=1)
        s_ref[q] = jnp.concatenate(s_new[mine], axis=1)


def _rwkv_post_kernel(y_ref, bonus_ref, g_ref, pg_ref, m_ref, gg_ref, gb_ref, bdm_ref, pj_ref, o_ref):
    y = y_ref[0]
    mu = _head_sums(y, bdm_ref[...])
    d = y - mu
    var = _head_sums(d * d, bdm_ref[...])
    yn = d * lax.rsqrt(var + RWKV_GN_EPS) * gg_ref[...] + gb_ref[...] + bonus_ref[0]
    o = _bdot(yn * g_ref[0], pj_ref[...])
    o_ref[0] = (m_ref[0].astype(F32) + pg_ref[0].astype(F32) * o).astype(o_ref.dtype)


def _rwkv(p, merged, mu_pad, w0, wup_pad, a0, aup_pad, g_up, k_k, k_a, r_k, gn_g, gn_b, proj_bf):
    bsz, seq, _ = p.shape
    row = lambda a: a.reshape(1, -1)
    full = lambda shape: pl.BlockSpec(shape, lambda b, i: (0,) * len(shape))
    head_id = jnp.arange(RWKV_DIM, dtype=I32) // RWKV_HEAD
    bd = (head_id[:, None] == head_id[None, :]).astype(BF16)

    ts = min(seq, 256)
    t_id = jnp.arange(ts, dtype=I32)
    ltri = ((t_id[:, None] // RWKV_CHUNK == t_id[None, :] // RWKV_CHUNK)
            & (t_id[:, None] >= t_id[None, :])).astype(BF16)
    seq_blk = lambda width: pl.BlockSpec((1, ts, width), lambda b, i: (b, i, 0))
    wide = SDS((bsz, seq, RWKV_DIM), F32)
    rt, kkt, kh, bh, v, pinc, bonus, g = pl.pallas_call(
        _rwkv_prep_kernel,
        out_shape=[wide] * 8,
        grid=(bsz, seq // ts),
        in_specs=[pl.BlockSpec((1, ts, 2048), lambda b, i: (b, i, SEG_B // 2048)),
                  full((1, 2048)), full((1, RWKV_DIM)), full((LANES, RWKV_DIM)),
                  full((1, RWKV_DIM)), full((LANES, RWKV_DIM)), full((LORA_G, RWKV_DIM)),
                  full((1, RWKV_DIM)), full((1, RWKV_DIM)), full((1, RWKV_DIM)),
                  full((RWKV_DIM, RWKV_DIM)), full((ts, ts))],
        out_specs=[seq_blk(RWKV_DIM)] * 8,
        scratch_shapes=[pltpu.VMEM((ts + SUBLANES, 2048), F32)],
        compiler_params=_cparams(("arbitrary", "arbitrary")),
        name="rwkv_prep",
    )(p, mu_pad, row(w0), wup_pad, row(a0), aup_pad, g_up, row(k_k), row(k_a), row(r_k), bd, ltri)

    c = RWKV_CHUNK
    nb = 2 if bsz % 2 == 0 else 1
    chunk_blk = pl.BlockSpec((nb, c, RWKV_DIM), lambda b, i: (b, i, 0))
    y = pl.pallas_call(
        _rwkv_scan_kernel,
        out_shape=wide,
        grid=(bsz // nb, seq // c),
        in_specs=[chunk_blk] * 6,
        out_specs=chunk_blk,
        scratch_shapes=[pltpu.VMEM((nb, RWKV_HEAD, RWKV_DIM), F32)],
        compiler_params=_cparams(("arbitrary", "arbitrary")),
        name="rwkv_scan",
    )(rt, kkt, kh, bh, v, pinc)

    tp = min(seq, 512)
    blk = lambda width: pl.BlockSpec((1, tp, width), lambda b, i: (b, i, 0))
    return pl.pallas_call(
        _rwkv_post_kernel,
        out_shape=SDS((bsz, seq, D_MODEL), MERGED_DTYPE),
        grid=(bsz, seq // tp),
        in_specs=[blk(RWKV_DIM), blk(RWKV_DIM), blk(RWKV_DIM),
                  pl.BlockSpec((1, tp, D_MODEL), lambda b, i: (b, i, SEG_G // D_MODEL + 1)),
                  blk(D_MODEL), full((1, RWKV_DIM)), full((1, RWKV_DIM)),
                  full((RWKV_DIM, RWKV_DIM)), full((RWKV_DIM, D_MODEL))],
        out_specs=blk(D_MODEL),
        compiler_params=_cparams(("arbitrary", "arbitrary")),
        name="rwkv_post",
    )(y, bonus, g, p, merged, row(gn_g), row(gn_b), bd * (1.0 / RWKV_HEAD), proj_bf)


def _out_kernel(m_ref, x_ref, g1_ref, w_ref, n2_ref, sc_ref, sh_ref, xo_ref, h_ref):
    xn = x_ref[0] + g1_ref[0] * _bdot(m_ref[0], w_ref[...])
    xo_ref[0] = xn
    h_ref[0] = _modulated_rmsnorm(xn, n2_ref[...], sc_ref[0], sh_ref[0])


def _out_proj(merged, x, g1, w_bf, norm2, sc2, sh2):
    bsz, seq, _ = x.shape
    ts = min(seq, 512)
    blk = pl.BlockSpec((1, ts, D_MODEL), lambda b, i: (b, i, 0))
    per_b = pl.BlockSpec((1, 1, D_MODEL), lambda b, i: (b, 0, 0))
    return pl.pallas_call(
        _out_kernel,
        out_shape=[SDS((bsz, seq, D_MODEL), F32)] * 2,
        grid=(bsz, seq // ts),
        in_specs=[blk, blk, per_b, pl.BlockSpec((D_MODEL, D_MODEL), lambda b, i: (0, 0)),
                  pl.BlockSpec((1, D_MODEL), lambda b, i: (0, 0)), per_b, per_b],
        out_specs=[blk, blk],
        compiler_params=_cparams(("arbitrary", "arbitrary")),
        name="out_proj",
    )(merged, x, g1, w_bf, norm2, sc2, sh2)


MOE_TILE = 256
MOE_SLOTS = 2560
RUN_LOOP_UNROLL = 4
DISPATCH_PIECE = 48
DISPATCH_SPARE = DISPATCH_PIECE - SUBLANES
DISPATCH_WAIT_ROWS = 2048
assert MOE_SLOTS >= TOP_K * MOE_TILE + N_EXPERTS * (SUBLANES - 1) + DISPATCH_SPARE
U32 = jnp.uint32


ROUTE_TILES = 2


def _route_kernel(h_ref, rw_ref, bias_ref, upper_ref, ltri_ref, slot_ref, w_ref, n8_ref):
    tile = MOE_TILE
    subs = range(h_ref.shape[0] // tile)
    neg = -jnp.inf
    shape3 = (GROUP_SIZE, N_GROUPS, tile)
    to3 = lambda a: a.reshape(shape3)
    lanes_of = lambda a: jnp.concatenate([a] * (tile // LANES), axis=1)
    fold = lambda a: jnp.sum(jnp.sum(a, axis=0), axis=0, keepdims=True)
    slab = lax.broadcasted_iota(I32, shape3, 0).astype(F32)
    grp = lax.broadcasted_iota(I32, shape3, 1).astype(F32)
    eid = grp * GROUP_SIZE + slab
    gi = lax.broadcasted_iota(I32, (N_GROUPS, tile), 0).astype(F32)
    ones_cols = jnp.ones((tile, LANES), BF16)
    s3 = [to3(jax.nn.sigmoid(_hdot_nt(rw_ref[...], h_ref[pl.ds(q * tile, tile), :]))) for q in subs]
    b3 = [s3[q] + to3(bias_ref[...]) for q in subs]
    m1 = [jnp.max(b3[q], axis=0, keepdims=True) for q in subs]
    first = [jnp.min(jnp.where(b3[q] == m1[q], slab, GROUP_SIZE), axis=0, keepdims=True) for q in subs]
    m2 = [jnp.max(jnp.where(slab == first[q], neg, b3[q]), axis=0, keepdims=True) for q in subs]
    gs = [(m1[q] + m2[q])[0] for q in subs]
    chosen = [jnp.zeros((N_GROUPS, tile), F32) for q in subs]
    for _ in range(TOPK_GROUPS):
        m = [jnp.max(gs[q], axis=0, keepdims=True) for q in subs]
        hit = [gi == jnp.min(jnp.where(gs[q] == m[q], gi, N_GROUPS), axis=0, keepdims=True) for q in subs]
        chosen = [jnp.where(hit[q], 1.0, chosen[q]) for q in subs]
        gs = [jnp.where(hit[q], neg, gs[q]) for q in subs]
    cur = [jnp.where((chosen[q] > 0.0)[None], b3[q], neg) for q in subs]
    base = [jnp.zeros((N_EXPERTS, LANES), F32) for q in subs]
    picks, w_rows, rank_rows = [[] for q in subs], [[] for q in subs], [[] for q in subs]
    for _ in range(TOP_K):
        m = [jnp.max(jnp.max(cur[q], axis=0), axis=0, keepdims=True)[None] for q in subs]
        pick = [jnp.min(jnp.min(jnp.where(cur[q] == m[q], eid, N_EXPERTS), axis=0), axis=0, keepdims=True)
                for q in subs]
        hit = [eid == pick[q][None] for q in subs]
        onehot = [hit[q].astype(BF16).reshape(N_EXPERTS, tile) for q in subs]
        before = [jnp.dot(onehot[q], upper_ref[...], preferred_element_type=F32) for q in subs]
        count = [jnp.dot(onehot[q], ones_cols, preferred_element_type=F32) for q in subs]
        for q in subs:
            w_rows[q].append(fold(jnp.where(hit[q], s3[q], 0.0)))
            picks[q].append(pick[q])
            rank_rows[q].append(fold(jnp.where(hit[q], to3(before[q] + lanes_of(base[q])), 0.0)))
        cur = [jnp.where(hit[q], neg, cur[q]) for q in subs]
        base = [base[q] + count[q] for q in subs]
    n8 = [jnp.floor((base[q] + (SUBLANES - 1.0)) * (1.0 / SUBLANES)) * SUBLANES for q in subs]
    run_start = [to3(lanes_of(_hdot(ltri_ref[...], n8[q]))) for q in subs]
    slots, weights = [], []
    for q in subs:
        slots.append(jnp.concatenate(
            [rank_rows[q][j] + fold(jnp.where(eid == picks[q][j][None], run_start[q], 0.0)) for j in range(TOP_K)],
            axis=0))
        w_all = jnp.concatenate(w_rows[q], axis=0)
        weights.append(w_all / jnp.sum(w_all, axis=0, keepdims=True) * ROUTED_SCALE)
    for q in subs:
        n8_ref[q] = n8[q]
    w_ref[...] = jnp.concatenate(weights, axis=1)
    slot_ref[...] = jnp.concatenate(slots, axis=1).astype(I32)


def _route(h2, router_w, router_bias):
    n_tok = h2.shape[0]
    tile = MOE_TILE
    n_tiles = n_tok // tile
    per_step = ROUTE_TILES if n_tiles % ROUTE_TILES == 0 else 1
    regroup = lambda a: a.reshape(N_GROUPS, GROUP_SIZE, -1).transpose(1, 0, 2).reshape(N_EXPERTS, -1)
    rw = regroup(router_w.T)
    bias = jnp.broadcast_to(regroup(router_bias.reshape(N_EXPERTS, 1)), (N_EXPERTS, tile))
    t_id = jnp.arange(tile, dtype=I32)
    upper = (t_id[:, None] < t_id[None, :]).astype(BF16)
    e_id = jnp.arange(N_EXPERTS, dtype=I32)
    ltri = (e_id[:, None] > e_id[None, :]).astype(F32)
    tok_blk = pl.BlockSpec((TOP_K, per_step * tile), lambda i: (0, i))
    full = lambda shape: pl.BlockSpec(shape, lambda i: (0,) * len(shape))
    slot_t, w_t, n8 = pl.pallas_call(
        _route_kernel,
        out_shape=[SDS((TOP_K, n_tok), I32), SDS((TOP_K, n_tok), F32),
                   SDS((n_tiles, N_EXPERTS, LANES), F32)],
        grid=(n_tiles // per_step,),
        in_specs=[pl.BlockSpec((per_step * tile, D_MODEL), lambda i: (i, 0)),
                  full((N_EXPERTS, D_MODEL)), full((N_EXPERTS, tile)), full((tile, tile)),
                  full((N_EXPERTS, N_EXPERTS))],
        out_specs=[tok_blk, tok_blk, pl.BlockSpec((per_step, N_EXPERTS, LANES), lambda i: (i, 0, 0))],
        compiler_params=_cparams(("arbitrary",)),
        name="moe_route",
    )(h2, rw, bias, upper, ltri)
    return slot_t, w_t, n8[:, :, 0].astype(I32)


def _for_each_run_piece(tile_idx, n8_ref, fn):
    for r in range(N_EXPERTS):
        n = n8_ref[tile_idx * N_EXPERTS + r]
        size = MOE_TILE
        while size >= SUBLANES:
            @pl.when((n & size) != 0)
            def _(size=size):
                fn(r, n & ~(2 * size - 1), size)
            size //= 2


def _wait_rows(total, make_wait):
    size = SUBLANES
    while size <= MOE_SLOTS:
        @pl.when((total & size) != 0)
        def _(size=size):
            make_wait(size).wait()
        size *= 2


def _pack_pairs(hi_bits, lo_bits):
    return (hi_bits & jnp.uint32(0xFFFF0000)) | (lo_bits >> 16)


def _unpack_pairs(u):
    hi = lax.bitcast_convert_type(u & jnp.uint32(0xFFFF0000), F32)
    lo = lax.bitcast_convert_type(u << 16, F32)
    return jnp.concatenate([hi, lo], axis=1).astype(BF16)


def _dispatch_kernel(n8_ref, off_ref, dst_ref, tot_ref, pad_end_ref, slot_ref, h_ref, xs_ref,
                     g_ref, zero_ref, sem):
    i = pl.program_id(0)
    tile = h_ref.shape[0]
    half = D_MODEL // 2

    @pl.when(i == 0)
    def _():
        zero_ref[...] = jnp.zeros(zero_ref.shape, U32)

        def last_block(e, back):
            start = pl.multiple_of(jnp.maximum(pad_end_ref[e] - back * MOE_ROWS, 0), MOE_ROWS)
            return pltpu.make_async_copy(zero_ref, xs_ref.at[pl.ds(start, MOE_ROWS), :], sem.at[0])

        def has_blocks(e, back):
            prev = jnp.where(e > 0, pad_end_ref[jnp.maximum(e - 1, 0)], 0)
            return pad_end_ref[e] - prev >= back * MOE_ROWS

        def clear(e, carry):
            for back in (1, 2):
                pl.when(has_blocks(e, back))(lambda back=back: last_block(e, back).start())
            return carry
        lax.fori_loop(0, N_EXPERTS, clear, 0)

        def done(e, carry):
            for back in (1, 2):
                pl.when(has_blocks(e, back))(lambda back=back: last_block(e, back).wait())
            return carry
        lax.fori_loop(0, N_EXPERTS, done, 0)

        def tail_block(b):
            start = pl.multiple_of(b * MOE_ROWS, MOE_ROWS)
            return pltpu.make_async_copy(zero_ref, xs_ref.at[pl.ds(start, MOE_ROWS), :], sem.at[0])

        def clear_tail(b, carry):
            tail_block(b).start()
            return carry

        def done_tail(b, carry):
            tail_block(b).wait()
            return carry
        used = pad_end_ref[N_EXPERTS - 1] // MOE_ROWS
        lax.fori_loop(used, xs_ref.shape[0] // MOE_ROWS, clear_tail, 0)
        lax.fori_loop(used, xs_ref.shape[0] // MOE_ROWS, done_tail, 0)

    slot_id = lax.broadcasted_iota(I16, (MOE_SLOTS, tile), 0)
    slots = slot_ref[...].astype(I16)
    sel = jnp.zeros((MOE_SLOTS, tile), BF16)
    for j in range(TOP_K):
        sel = jnp.where(slot_id == slots[j:j + 1, :], jnp.ones((), BF16), sel)
    g = jnp.dot(sel, h_ref[...].astype(BF16), preferred_element_type=F32)
    bits = lax.bitcast_convert_type(g, U32)
    buf = i % 2
    g_ref[buf] = _pack_pairs(bits[:, :half], bits[:, half:])

    def drain(step):
        total = tot_ref[step]
        whole = pltpu.make_async_copy(g_ref.at[step % 2, pl.ds(0, DISPATCH_WAIT_ROWS), :],
                                      xs_ref.at[pl.ds(0, DISPATCH_WAIT_ROWS), :], sem.at[step % 2])

        def wait_whole(k, carry):
            whole.wait()
            return carry
        lax.fori_loop(0, lax.shift_right_logical(total, DISPATCH_WAIT_ROWS.bit_length() - 1), wait_whole, 0)
        _wait_rows(total & (DISPATCH_WAIT_ROWS - 1), lambda rows: pltpu.make_async_copy(
            g_ref.at[step % 2, pl.ds(0, rows), :], xs_ref.at[pl.ds(0, rows), :], sem.at[step % 2]))

    pl.when(i > 0)(lambda: drain(i - 1))

    def copy(r, k):
        src = pl.multiple_of(off_ref[i * N_EXPERTS + r] + k * DISPATCH_PIECE, SUBLANES)
        dst = pl.multiple_of(dst_ref[i * N_EXPERTS + r] + k * DISPATCH_PIECE, SUBLANES)
        pltpu.make_async_copy(g_ref.at[buf, pl.ds(src, DISPATCH_PIECE), :],
                              xs_ref.at[pl.ds(dst, DISPATCH_PIECE), :], sem.at[buf]).start()

    def per_expert(r, carry):
        n = n8_ref[i * N_EXPERTS + r]
        pl.when(n > 0)(lambda: copy(r, 0))

        @pl.when(n > DISPATCH_PIECE)
        def _():
            def more(k, c):
                copy(r, k)
                return c
            lax.fori_loop(1, (n + (DISPATCH_PIECE - 1)) // DISPATCH_PIECE, more, 0)
        return carry
    lax.fori_loop(0, N_EXPERTS, per_expert, 0, unroll=RUN_LOOP_UNROLL)
    pl.when(i == pl.num_programs(0) - 1)(lambda: drain(i))


def _dispatch(h2, slot_t, n8_flat, off_flat, dst_flat, tot, pad_end, n_rows):
    n_tok = h2.shape[0]
    tile = MOE_TILE
    grid_spec = pltpu.PrefetchScalarGridSpec(
        num_scalar_prefetch=5,
        grid=(n_tok // tile,),
        in_specs=[pl.BlockSpec((TOP_K, tile), lambda i, *_: (0, i)),
                  pl.BlockSpec((tile, D_MODEL), lambda i, *_: (i, 0))],
        out_specs=pl.BlockSpec(memory_space=pl.ANY),
        scratch_shapes=[pltpu.VMEM((2, MOE_SLOTS, D_MODEL // 2), U32),
                        pltpu.VMEM((MOE_ROWS, D_MODEL // 2), U32), pltpu.SemaphoreType.DMA((2,))],
    )
    return pl.pallas_call(
        _dispatch_kernel,
        out_shape=SDS((n_rows, D_MODEL // 2), U32),
        grid_spec=grid_spec,
        compiler_params=_cparams(("arbitrary",)),
        name="moe_dispatch",
    )(n8_flat, off_flat, dst_flat, tot, pad_end, slot_t, h2)


def _expert_kernel(first_ref, nblk_ref, exp_of_ref, xs_ref, w1_ref, w3_ref, w2_ref, y_ref,
                   xbuf, ybuf, wb1, wb3, wb2, in_sem, out_sem):
    del exp_of_ref
    r = pl.program_id(0)
    half = D_MODEL // 2
    used = first_ref[N_EXPERTS - 1] + nblk_ref[N_EXPERTS - 1]
    part = MOE_ROWS // EXPERT_DMA_PARTS

    class _Copies:
        def __init__(self, make):
            self.parts = [make(p) for p in range(EXPERT_DMA_PARTS)]

        def start(self):
            for c in self.parts:
                c.start()

        def wait(self):
            for c in self.parts:
                c.wait()

    def hbm_rows(g, p):
        return pl.ds(pl.multiple_of(g * MOE_ROWS + p * part, part), part)

    in_copy = lambda g, slot: _Copies(lambda p: pltpu.make_async_copy(
        xs_ref.at[hbm_rows(g, p), :], xbuf.at[slot, pl.ds(p * part, part), :], in_sem.at[slot]))
    out_copy = lambda g, slot: _Copies(lambda p: pltpu.make_async_copy(
        ybuf.at[slot, pl.ds(p * part, part), :], y_ref.at[hbm_rows(g, p), :], out_sem.at[slot]))

    @pl.when(r == 0)
    def _():
        for g in range(EXPERT_AHEAD):
            pl.when(g < used)(lambda g=g: in_copy(g, g).start())

    wb1[...] = w1_ref[0, 0].astype(BF16)
    wb3[...] = w3_ref[0, 0].astype(BF16)
    wb2[...] = w2_ref[0, 0].astype(BF16)

    def process(blocks):
        dot = lambda a, b: jnp.dot(a, b, preferred_element_type=F32)
        slots = [g % EXPERT_BUFS for g in blocks]
        for g, slot in zip(blocks, slots):
            in_copy(g, slot).wait()
            ahead = g + EXPERT_AHEAD
            pl.when(ahead < used)(lambda ahead=ahead: in_copy(ahead, ahead % EXPERT_BUFS).start())
            pl.when(g >= EXPERT_BUFS)(lambda g=g, slot=slot: out_copy(g - EXPERT_BUFS, slot).wait())
        x = [_unpack_pairs(xbuf[slot]) for slot in slots]
        gate = [dot(xq, wb1[...]) for xq in x]
        up = [dot(xq, wb3[...]) for xq in x]
        hid = [(jax.nn.silu(a) * b).astype(BF16) for a, b in zip(gate, up)]
        for g, slot, hq in zip(blocks, slots, hid):
            bits = lax.bitcast_convert_type(dot(hq, wb2[...]).astype(BF16).astype(F32), U32)
            ybuf[slot] = _pack_pairs(bits[:, :half], bits[:, half:])
            out_copy(g, slot).start()

    def pair(k, carry):
        g = first_ref[r] + 2 * k
        process([g, g + 1])
        return carry
    n_mine = nblk_ref[r]
    lax.fori_loop(0, lax.shift_right_logical(n_mine, 1), pair, 0)
    pl.when((n_mine & 1) == 1)(lambda: process([first_ref[r] + n_mine - 1]))

    @pl.when(r == N_EXPERTS - 1)
    def _():
        for back in range(EXPERT_BUFS, 0, -1):
            pl.when(used >= back)(
                lambda back=back: out_copy(used - back, (used - back) % EXPERT_BUFS).wait())

        ybuf[0] = jnp.zeros(ybuf.shape[1:], U32)
        n_blocks = y_ref.shape[0] // MOE_ROWS

        def clear(g, carry):
            out_copy(g, 0).start()
            return carry

        def done(g, carry):
            out_copy(g, 0).wait()
            return carry
        lax.fori_loop(used, n_blocks, clear, 0)
        lax.fori_loop(used, n_blocks, done, 0)


def _experts(xs, first_block, n_block, exp_of_row, w1, w3, w2, layer):
    n_rows = xs.shape[0]
    half = D_MODEL // 2
    w_in_blk = pl.BlockSpec((1, 1, D_MODEL, EXPERT_FF), lambda r, first, nblk, eo: (layer, eo[r], 0, 0))
    grid_spec = pltpu.PrefetchScalarGridSpec(
        num_scalar_prefetch=3,
        grid=(N_EXPERTS,),
        in_specs=[pl.BlockSpec(memory_space=pl.ANY), w_in_blk, w_in_blk,
                  pl.BlockSpec((1, 1, EXPERT_FF, D_MODEL), lambda r, first, nblk, eo: (layer, eo[r], 0, 0))],
        out_specs=pl.BlockSpec(memory_space=pl.ANY),
        scratch_shapes=[pltpu.VMEM((EXPERT_BUFS, MOE_ROWS, half), U32),
                        pltpu.VMEM((EXPERT_BUFS, MOE_ROWS, half), U32),
                        pltpu.VMEM((D_MODEL, EXPERT_FF), BF16), pltpu.VMEM((D_MODEL, EXPERT_FF), BF16),
                        pltpu.VMEM((EXPERT_FF, D_MODEL), BF16),
                        pltpu.SemaphoreType.DMA((EXPERT_BUFS,)), pltpu.SemaphoreType.DMA((EXPERT_BUFS,))],
    )
    return pl.pallas_call(
        _expert_kernel,
        out_shape=SDS((n_rows, half), U32),
        grid_spec=grid_spec,
        compiler_params=_cparams(("arbitrary",)),
        name="moe_experts",
    )(first_block, n_block, exp_of_row, xs, w1, w3, w2)


def _combine_kernel(n8_ref, off_ref, src_ref, tot_ref, y_ref, slot_ref, w_ref, h_ref, x_ref, g2_ref,
                    s1_ref, s3_ref, s2_ref, fn_ref, o_ref, yt_ref, sem, *, final):
    tile = h_ref.shape[1]
    i = pl.program_id(0) * pl.num_programs(1) + pl.program_id(1)

    @pl.when(i == 0)
    def _():
        yt_ref[...] = jnp.zeros(yt_ref.shape, U32)

    def piece(r, offset, rows):
        src = pl.multiple_of(src_ref[i * N_EXPERTS + r] + offset, SUBLANES)
        dst = pl.multiple_of(off_ref[i * N_EXPERTS + r] + offset, SUBLANES)
        pltpu.make_async_copy(y_ref.at[pl.ds(src, rows), :], yt_ref.at[pl.ds(dst, rows), :], sem).start()

    _for_each_run_piece(i, n8_ref, piece)
    h = h_ref[0]
    shared = _bdot(jax.nn.silu(_bdot(h, s1_ref[...])) * _bdot(h, s3_ref[...]), s2_ref[...])
    slot_id = lax.broadcasted_iota(I16, (tile, MOE_SLOTS), 1)
    slots = slot_ref[0].astype(I16)
    w = w_ref[0].astype(BF16)
    pw = jnp.zeros((tile, MOE_SLOTS), BF16)
    for j in range(TOP_K):
        pw = jnp.where(slot_id == slots[:, j:j + 1], w[:, j:j + 1], pw)
    _wait_rows(tot_ref[i], lambda rows: pltpu.make_async_copy(
        y_ref.at[pl.ds(0, rows), :], yt_ref.at[pl.ds(0, rows), :], sem))
    routed = jnp.dot(pw, _unpack_pairs(yt_ref[...]), preferred_element_type=F32)
    xn = x_ref[0] + g2_ref[0] * (routed + shared)
    if final:
        xn = xn * lax.rsqrt(jnp.mean(xn * xn, axis=-1, keepdims=True) + NORM_EPS) * fn_ref[...]
    o_ref[0] = xn


def _combine(y, slot_nat, w_nat, n8_flat, off_flat, dst_flat, tot, h2, x, g2, s1_bf, s3_bf, s2_bf,
             final_norm, final):
    bsz, seq, _ = x.shape
    tile = MOE_TILE
    per_seq = seq // tile
    blk = pl.BlockSpec((1, tile, D_MODEL), lambda b, i, *_: (b, i, 0))
    tok = pl.BlockSpec((1, tile, TOP_K), lambda b, i, *_: (b, i, 0))
    full = lambda shape: pl.BlockSpec(shape, lambda b, i, *_: (0,) * len(shape))
    grid_spec = pltpu.PrefetchScalarGridSpec(
        num_scalar_prefetch=4,
        grid=(bsz, per_seq),
        in_specs=[pl.BlockSpec(memory_space=pl.ANY), tok, tok, blk, blk,
                  pl.BlockSpec((1, 1, D_MODEL), lambda b, i, *_: (b, 0, 0)),
                  full((D_MODEL, SHARED_FF)), full((D_MODEL, SHARED_FF)), full((SHARED_FF, D_MODEL)),
                  full((1, D_MODEL))],
        out_specs=blk,
        scratch_shapes=[pltpu.VMEM((MOE_SLOTS, D_MODEL // 2), U32), pltpu.SemaphoreType.DMA],
    )
    return pl.pallas_call(
        functools.partial(_combine_kernel, final=final),
        out_shape=SDS((bsz, seq, D_MODEL), F32),
        grid_spec=grid_spec,
        compiler_params=_cparams(("arbitrary", "arbitrary")),
        name="moe_combine",
    )(n8_flat, off_flat, dst_flat, tot, y, slot_nat, w_nat, h2, x, g2, s1_bf, s3_bf, s2_bf, final_norm)


def _moe(x, h2, g2, router_w, router_bias, w1, w3, w2, layer, s1_bf, s3_bf, s2_bf, final_norm, final):
    bsz, seq, _ = x.shape
    assert seq % MOE_TILE == 0, "token tiles must not straddle sequences"
    n_tok = bsz * seq
    slot_t, w_t, n8 = _route(h2.reshape(n_tok, D_MODEL), router_w, router_bias)
    n_tiles = n8.shape[0]
    counts = jnp.sum(n8, axis=0)
    padded = jnp.where(counts > 0, (counts + DISPATCH_SPARE + MOE_ROWS - 1) // MOE_ROWS * MOE_ROWS, 0)
    pad_end = jnp.cumsum(padded).astype(I32)
    pad_start = pad_end - padded
    run_row = (pad_start[None, :] + jnp.cumsum(n8, axis=0) - n8).astype(I32)
    run_slot = (jnp.cumsum(n8, axis=1) - n8).astype(I32)
    tot = jnp.sum(n8, axis=1).astype(I32)
    copied = jnp.sum((n8 + DISPATCH_PIECE - 1) // DISPATCH_PIECE * DISPATCH_PIECE, axis=1).astype(I32)
    max_rows = (n_tok * TOP_K + n_tiles * N_EXPERTS * (SUBLANES - 1)
                + N_EXPERTS * (DISPATCH_SPARE + MOE_ROWS - 1))
    n_blocks = (max_rows + MOE_ROWS - 1) // MOE_ROWS
    row_id = jnp.arange(N_EXPERTS, dtype=I32)
    exp_of_row = (row_id % N_GROUPS) * GROUP_SIZE + row_id // N_GROUPS
    flat = lambda a: a.reshape(-1).astype(I32)
    xs = _dispatch(h2.reshape(n_tok, D_MODEL), slot_t, flat(n8), flat(run_slot), flat(run_row), copied,
                   pad_end, n_blocks * MOE_ROWS)
    y = _experts(xs, (pad_start // MOE_ROWS).astype(I32), (padded // MOE_ROWS).astype(I32), exp_of_row,
                 w1, w3, w2, layer)
    nat = lambda a: a.T.reshape(bsz, seq, TOP_K)
    return _combine(y, nat(slot_t), nat(w_t), flat(n8), flat(run_slot), flat(run_row), tot, h2, x, g2,
                    s1_bf, s3_bf, s2_bf, final_norm, final)


def _pad_cols(a, width):
    return jnp.pad(a, ((0, 0), (0, width - a.shape[1])))


def _layout_w_in(w_in):
    b0 = COLS_A
    c0 = COLS_A + COLS_B
    g0 = c0 + COLS_C
    seg_b = _pad_cols(w_in[:, b0:c0], SEG_G - SEG_B)
    return jnp.concatenate([w_in[:, c0:g0], seg_b, w_in[:, g0:], w_in[:, :b0]], axis=1).astype(BF16)


def _layout_mu(mu):
    return _pad_cols(mu.reshape(1, -1), SEG_G - SEG_B)


def _pad_rows(a, height):
    return jnp.pad(a, ((0, height - a.shape[0]), (0, 0)))


def kernel(x, c, ada_w, ada_b, norm1, norm2, w_in, conv_a_w, conv_a_b, ln_a_g, ln_a_b, proj_a, mu_b, w0, w_up, a0, a_up, g_up, k_k, k_a, r_k, gn_b_g, gn_b_b, proj_b, conv_c_w, conv_c_b, lru_wa, lru_ba, lru_wx, lru_bx, lru_lambda, proj_c, w_out, router_w, router_bias, exp_w1, exp_w3, exp_w2, sh_w1, sh_w3, sh_w2, final_norm):
    depth = ada_w.shape[0]
    bsz = x.shape[0]
    mod = _ada_mod(c, ada_w, ada_b)
    for l in range(depth):
        sh1, sc1, g1, sh2, sc2, g2 = [mod[l, :, i * D_MODEL:(i + 1) * D_MODEL].reshape(bsz, 1, D_MODEL)
                                      for i in range(N_MOD)]
        p = _in_proj(x, sc1, sh1, norm1[l].reshape(1, -1), _layout_w_in(w_in[l]))
        merged = _conv_a(p, conv_a_w[l], conv_a_b[l], ln_a_g[l], ln_a_b[l], proj_a[l].astype(BF16))
        merged = _rwkv(p, merged, _layout_mu(mu_b[l]), w0[l], _pad_rows(w_up[l], LANES).astype(BF16),
                       a0[l], jnp.pad(a_up[l], ((LORA_W, 0), (0, 0))).astype(BF16), g_up[l].astype(BF16),
                       k_k[l], k_a[l], r_k[l], gn_b_g[l], gn_b_b[l], proj_b[l].astype(BF16))
        merged = _lru(p, merged, conv_c_w[l], conv_c_b[l], lru_wa[l].astype(BF16), lru_ba[l],
                      lru_wx[l].astype(BF16), lru_bx[l], lru_lambda[l], proj_c[l].astype(BF16))
        x, h2 = _out_proj(merged, x, g1, w_out[l].astype(BF16), norm2[l].reshape(1, -1), sc2, sh2)
        x = _moe(x, h2, g2, router_w[l], router_bias[l], exp_w1, exp_w3, exp_w2, l,
                 sh_w1[l].astype(BF16), sh_w3[l].astype(BF16), sh_w2[l].astype(BF16),
                 final_norm.reshape(1, -1), final=(l == depth - 1))
    return x
```

```python
import functools

import jax
import jax.numpy as jnp
from jax import lax
from jax.experimental import pallas as pl
from jax.experimental.pallas import tpu as pltpu

F32 = jnp.float32
BF16 = jnp.bfloat16
I32 = jnp.int32
I16 = jnp.int16
SDS = jax.ShapeDtypeStruct
HIGHEST = lax.Precision.HIGHEST

D_MODEL = 1024
N_MOD = 6
NORM_EPS = 1e-6
CONV_A_CH = 512
CONV_A_WIDTH = 31
CONV_A_LN_EPS = 1e-5
RWKV_HEADS = 8
RWKV_HEAD = 64
RWKV_DIM = RWKV_HEADS * RWKV_HEAD
LORA_W = 64
LORA_A = 64
LORA_G = 128
RWKV_GN_EPS = 64e-5
RWKV_CHUNK = 64
LRU_DIM = 1024
LRU_HEADS = 8
LRU_BLOCK = LRU_DIM // LRU_HEADS
LRU_CONV = 4
LRU_C = 8.0
N_EXPERTS = 64
TOP_K = 8
N_GROUPS = 8
GROUP_SIZE = N_EXPERTS // N_GROUPS
TOPK_GROUPS = 4
EXPERT_FF = 256
SHARED_FF = 256
ROUTED_SCALE = 2.5
SEG_C = 0
SEG_B = 2048
SEG_G = 4096
SEG_A = 7168
IN_COLS_PAD = 8192
RW_R, RW_K, RW_V, RW_XWA, RW_XG = 0, 512, 1024, 1536, 1664
COLS_A = 2 * CONV_A_CH
COLS_B = 3 * RWKV_DIM + LORA_W + LORA_A + LORA_G
COLS_C = 2 * LRU_DIM
VMEM_LIMIT = 56 * 1024 * 1024
MERGED_DTYPE = BF16
SUBLANES = 8
LANES = 128
MOE_ROWS = 512
EXPERT_DMA_PARTS = 4
EXPERT_AHEAD = 4
EXPERT_BUFS = EXPERT_AHEAD + 2


def _cparams(sem):
    return pltpu.CompilerParams(dimension_semantics=sem, vmem_limit_bytes=VMEM_LIMIT)


def _bdot(a, b):
    return jnp.dot(a.astype(BF16), b.astype(BF16), preferred_element_type=F32)


def _hdot(a, b):
    return jnp.dot(a, b, preferred_element_type=F32, precision=HIGHEST)


def _split(a):
    hi = a.astype(BF16)
    return hi, (a - hi.astype(F32)).astype(BF16)


def _head_sums(a, ones_bf):
    hi, lo = _split(a)
    return (jnp.dot(hi, ones_bf, preferred_element_type=F32)
            + jnp.dot(lo, ones_bf, preferred_element_type=F32))


def _hdot_nt(a, b):
    return lax.dot_general(a, b, (((1,), (1,)), ((), ())), preferred_element_type=F32,
                           precision=HIGHEST)


def _ada_kernel(c_ref, w_ref, b_ref, o_ref):
    cond = jax.nn.silu(c_ref[...])
    o_ref[0] = _bdot(cond, w_ref[0]) + b_ref[0]


def _ada_mod(c, ada_w, ada_b):
    depth, _, n = ada_w.shape
    bsz = c.shape[0]
    tn = 1536
    return pl.pallas_call(
        _ada_kernel,
        out_shape=SDS((depth, bsz, n), F32),
        grid=(depth, n // tn),
        in_specs=[pl.BlockSpec((bsz, D_MODEL), lambda l, j: (0, 0)),
                  pl.BlockSpec((1, D_MODEL, tn), lambda l, j: (l, 0, j)),
                  pl.BlockSpec((1, 1, tn), lambda l, j: (l, 0, j))],
        out_specs=pl.BlockSpec((1, bsz, tn), lambda l, j: (l, 0, j)),
        compiler_params=_cparams(("arbitrary", "arbitrary")),
        name="ada_mod",
    )(c, ada_w, ada_b.reshape(depth, 1, n))


def _modulated_rmsnorm(x, g, sc, sh):
    y = x * lax.rsqrt(jnp.mean(x * x, axis=-1, keepdims=True) + NORM_EPS)
    return (y * g) * (1.0 + sc) + sh


IN_PROJ_SUB = 512


def _in_kernel(x_ref, sc_ref, sh_ref, g_ref, w_ref, o_ref, h_ref):
    @pl.when(pl.program_id(2) == 0)
    def _():
        h_ref[...] = _modulated_rmsnorm(x_ref[0], g_ref[...], sc_ref[0], sh_ref[0]).astype(BF16)

    j = pl.program_id(2)
    ts, tn = o_ref.shape[1], o_ref.shape[2]
    is_gelu = j == SEG_C // tn
    is_gate = (j >= SEG_G // tn) & (j < SEG_A // tn)
    is_glu = j == SEG_A // tn

    def glu(acc):
        u = acc[:, :tn // 2] * jax.nn.sigmoid(acc[:, tn // 2:])
        return jnp.concatenate([u, jnp.zeros_like(u)], axis=1)

    def emit(act, sub):
        for r0 in range(0, ts, sub):
            rows = pl.ds(r0, sub)
            acc = jnp.dot(h_ref[rows, :], w_ref[...], preferred_element_type=F32)
            o_ref[0, rows, :] = act(acc).astype(o_ref.dtype)

    sub = min(ts, IN_PROJ_SUB)
    pl.when(is_gelu)(lambda: emit(functools.partial(jax.nn.gelu, approximate=True), sub))
    pl.when(is_gate)(lambda: emit(jax.nn.sigmoid, sub))
    pl.when(is_glu)(lambda: emit(glu, sub))
    pl.when(jnp.logical_not(is_gelu | is_gate | is_glu))(lambda: emit(lambda acc: acc, ts))


def _in_proj(x, sc, sh, g, w_pad):
    bsz, seq, _ = x.shape
    ts = min(seq, 2048)
    tn = 1024
    return pl.pallas_call(
        _in_kernel,
        out_shape=SDS((bsz, seq, IN_COLS_PAD), BF16),
        grid=(bsz, seq // ts, IN_COLS_PAD // tn),
        in_specs=[pl.BlockSpec((1, ts, D_MODEL), lambda b, i, j: (b, i, 0)),
                  pl.BlockSpec((1, 1, D_MODEL), lambda b, i, j: (b, 0, 0)),
                  pl.BlockSpec((1, 1, D_MODEL), lambda b, i, j: (b, 0, 0)),
                  pl.BlockSpec((1, D_MODEL), lambda b, i, j: (0, 0)),
                  pl.BlockSpec((D_MODEL, tn), lambda b, i, j: (0, j))],
        out_specs=pl.BlockSpec((1, ts, tn), lambda b, i, j: (b, i, j)),
        scratch_shapes=[pltpu.VMEM((ts, D_MODEL), BF16)],
        compiler_params=_cparams(("arbitrary", "arbitrary", "arbitrary")),
        name="in_proj",
    )(x, sc, sh, g, w_pad)


CONV_A_HALO = 32
CONV_A_SUB = 64


def _conv_a_kernel(pa_ref, pg_ref, cw_ref, cb_ref, lg_ref, lb_ref, pj_ref, o_ref, ext_ref, sh_ref, y_ref):
    ts = pa_ref.shape[1]

    @pl.when(pl.program_id(1) == 0)
    def _():
        ext_ref[pl.ds(0, CONV_A_HALO), :] = jnp.zeros((CONV_A_HALO, CONV_A_CH), F32)

    ext_ref[pl.ds(CONV_A_HALO, ts), :] = pa_ref[0].astype(F32)
    for p in range(1, SUBLANES):
        sh_ref[p - 1] = ext_ref[pl.ds(p, sh_ref.shape[1]), :]
    first = CONV_A_HALO - (CONV_A_WIDTH - 1)
    for r0 in range(0, ts, CONV_A_SUB):
        acc = jnp.zeros((CONV_A_SUB, CONV_A_CH), F32) + cb_ref[...]
        for j in range(CONV_A_WIDTH):
            phase = (first + j) % SUBLANES
            rows = pl.ds(r0 + first + j - phase, CONV_A_SUB)
            tap = ext_ref[rows, :] if phase == 0 else sh_ref[phase - 1, rows, :]
            acc = acc + tap * cw_ref[pl.ds(j, 1), :]
        y_ref[pl.ds(r0, CONV_A_SUB), :] = acc
    ext_ref[pl.ds(0, CONV_A_HALO), :] = ext_ref[pl.ds(ts, CONV_A_HALO), :]
    y = y_ref[...]
    mu = jnp.mean(y, axis=-1, keepdims=True)
    d = y - mu
    var = jnp.mean(d * d, axis=-1, keepdims=True)
    yn = d * lax.rsqrt(var + CONV_A_LN_EPS) * lg_ref[...] + lb_ref[...]
    o = _bdot(jax.nn.silu(yn), pj_ref[...])
    o_ref[0] = (pg_ref[0].astype(F32) * o).astype(o_ref.dtype)


def _conv_a(p, conv_w, conv_b, ln_g, ln_b, proj_bf):
    bsz, seq, _ = p.shape
    ts = min(seq, 512)
    row = lambda a: a.reshape(1, -1)
    full = lambda shape: pl.BlockSpec(shape, lambda b, i: (0,) * len(shape))
    return pl.pallas_call(
        _conv_a_kernel,
        out_shape=SDS((bsz, seq, D_MODEL), MERGED_DTYPE),
        grid=(bsz, seq // ts),
        in_specs=[pl.BlockSpec((1, ts, CONV_A_CH), lambda b, i: (b, i, SEG_A // CONV_A_CH)),
                  pl.BlockSpec((1, ts, D_MODEL), lambda b, i: (b, i, SEG_G // D_MODEL)),
                  full((CONV_A_WIDTH, CONV_A_CH)), full((1, CONV_A_CH)), full((1, CONV_A_CH)),
                  full((1, CONV_A_CH)), full((CONV_A_CH, D_MODEL))],
        out_specs=pl.BlockSpec((1, ts, D_MODEL), lambda b, i: (b, i, 0)),
        scratch_shapes=[pltpu.VMEM((ts + CONV_A_HALO, CONV_A_CH), F32),
                        pltpu.VMEM((SUBLANES - 1, ts + CONV_A_HALO - SUBLANES, CONV_A_CH), F32),
                        pltpu.VMEM((ts, CONV_A_CH), F32)],
        compiler_params=_cparams(("arbitrary", "arbitrary")),
        name="conv_a",
    )(p, p, conv_w, row(conv_b), row(ln_g), row(ln_b), proj_bf)


def _lru_kernel(pc_ref, pg_ref, m_ref, cw_ref, cb_ref, wa_ref, ba_ref, wx_ref, bx_ref, lam_ref,
                pj_ref, o_ref, ext_ref, h_ref, a_ref, b_ref):
    ts = pc_ref.shape[1]
    groups = ts // SUBLANES

    @pl.when(pl.program_id(1) == 0)
    def _():
        ext_ref[pl.ds(0, SUBLANES), :] = jnp.zeros((SUBLANES, LRU_DIM), F32)
        h_ref[...] = jnp.zeros((SUBLANES, LRU_DIM), F32)

    pc = pc_ref[0].astype(F32)
    y_gate = pc[:, :LRU_DIM]
    ext_ref[pl.ds(SUBLANES, ts), :] = pc[:, LRU_DIM:]
    first = SUBLANES - (LRU_CONV - 1)
    xc = jnp.zeros((ts, LRU_DIM), F32) + cb_ref[...]
    for j in range(LRU_CONV):
        xc = xc + ext_ref[pl.ds(first + j, ts), :] * cw_ref[pl.ds(j, 1), :]
    ext_ref[pl.ds(0, SUBLANES), :] = ext_ref[pl.ds(ts, SUBLANES), :]

    def block_diag(w_ref):
        return jnp.concatenate(
            [_bdot(xc[:, h * LRU_BLOCK:(h + 1) * LRU_BLOCK], w_ref[h]) for h in range(LRU_HEADS)],
            axis=1)

    gate_a = jax.nn.sigmoid(block_diag(wa_ref) + ba_ref[...])
    gate_x = jax.nn.sigmoid(block_diag(wx_ref) + bx_ref[...])
    log_a = -LRU_C * gate_a * jax.nn.softplus(-lam_ref[...])
    a = jnp.exp(log_a)
    b = xc * gate_x * jnp.sqrt(1.0 - jnp.exp(2.0 * log_a))

    a3 = a.reshape(groups, SUBLANES, LRU_DIM)
    b3 = b.reshape(groups, SUBLANES, LRU_DIM)
    row = lax.broadcasted_iota(I32, (groups, SUBLANES, LRU_DIM), 1)
    for s in (1, 2, 4):
        keep = row >= s
        b3 = jnp.where(keep, a3 * pltpu.roll(b3, s, axis=1) + b3, b3)
        a3 = jnp.where(keep, a3 * pltpu.roll(a3, s, axis=1), a3)
    a_ref[...] = a3.reshape(ts, LRU_DIM)
    b_ref[...] = b3.reshape(ts, LRU_DIM)
    h = h_ref[...]
    for g in range(groups):
        rows = pl.ds(g * SUBLANES, SUBLANES)
        hg = a_ref[rows, :] * h + b_ref[rows, :]
        b_ref[rows, :] = hg
        h = jnp.broadcast_to(hg[SUBLANES - 1:SUBLANES, :], (SUBLANES, LRU_DIM))
    h_ref[...] = h
    o = _bdot(b_ref[...] * y_gate, pj_ref[...])
    o_ref[0] = (m_ref[0].astype(F32) + pg_ref[0].astype(F32) * o).astype(o_ref.dtype)


def _lru(p, merged, conv_w, conv_b, wa_bf, ba, wx_bf, bx, lam, proj_bf):
    bsz, seq, _ = p.shape
    ts = min(seq, 512)
    row = lambda a: a.reshape(1, -1)
    full = lambda shape: pl.BlockSpec(shape, lambda b, i: (0,) * len(shape))
    return pl.pallas_call(
        _lru_kernel,
        out_shape=SDS((bsz, seq, D_MODEL), MERGED_DTYPE),
        grid=(bsz, seq // ts),
        in_specs=[pl.BlockSpec((1, ts, 2 * LRU_DIM), lambda b, i: (b, i, SEG_C // (2 * LRU_DIM))),
                  pl.BlockSpec((1, ts, D_MODEL), lambda b, i: (b, i, SEG_G // D_MODEL + 2)),
                  pl.BlockSpec((1, ts, D_MODEL), lambda b, i: (b, i, 0)),
                  full((LRU_CONV, LRU_DIM)), full((1, LRU_DIM)),
                  full((LRU_HEADS, LRU_BLOCK, LRU_BLOCK)), full((1, LRU_DIM)),
                  full((LRU_HEADS, LRU_BLOCK, LRU_BLOCK)), full((1, LRU_DIM)),
                  full((1, LRU_DIM)), full((LRU_DIM, D_MODEL))],
        out_specs=pl.BlockSpec((1, ts, D_MODEL), lambda b, i: (b, i, 0)),
        scratch_shapes=[pltpu.VMEM((ts + SUBLANES, LRU_DIM), F32),
                        pltpu.VMEM((SUBLANES, LRU_DIM), F32),
                        pltpu.VMEM((ts, LRU_DIM), F32),
                        pltpu.VMEM((ts, LRU_DIM), F32)],
        compiler_params=_cparams(("arbitrary", "arbitrary")),
        name="rg_lru",
    )(p, p, merged, conv_w, row(conv_b), wa_bf, row(ba), wx_bf, row(bx), row(lam), proj_bf)


def _rwkv_prep_kernel(pb_ref, mu_ref, w0_ref, wup_ref, a0_ref, aup_ref, gup_ref, kk_ref, ka_ref,
                      rk_ref, bd_ref, ltri_ref,
                      rt_ref, kkt_ref, kh_ref, bh_ref, v_ref, pinc_ref, bonus_ref, g_ref, ext_ref):
    ts = pb_ref.shape[1]

    @pl.when(pl.program_id(1) == 0)
    def _():
        ext_ref[pl.ds(0, SUBLANES), :] = jnp.zeros((SUBLANES, ext_ref.shape[1]), F32)

    p = pb_ref[0].astype(F32)
    ext_ref[pl.ds(SUBLANES, ts), :] = p
    prev = ext_ref[pl.ds(SUBLANES - 1, ts), :]
    ext_ref[pl.ds(0, SUBLANES), :] = ext_ref[pl.ds(ts, SUBLANES), :]
    pm = p + (prev - p) * mu_ref[...]
    r = pm[:, RW_R:RW_R + RWKV_DIM]
    k = pm[:, RW_K:RW_K + RWKV_DIM]
    v = pm[:, RW_V:RW_V + RWKV_DIM]
    xwa = pm[:, RW_XWA:RW_XWA + LANES]
    xg = pm[:, RW_XG:RW_XG + LORA_G]
    w = -jax.nn.softplus(-(w0_ref[...] + _bdot(jnp.tanh(xwa), wup_ref[...]))) - 0.5
    lw = -jnp.exp(w)
    a = jax.nn.sigmoid(a0_ref[...] + _bdot(xwa, aup_ref[...]))
    g_ref[0] = _bdot(jax.nn.sigmoid(xg), gup_ref[...]).astype(g_ref.dtype)
    kkr = k * kk_ref[...]
    ss = _head_sums(kkr * kkr, bd_ref[...])
    kk = kkr / jnp.maximum(jnp.sqrt(ss), 1e-12)
    k2 = k * (1.0 + (a - 1.0) * ka_ref[...])
    lw_hi = lw.astype(BF16)
    lw_mid, lw_lo = _split(lw - lw_hi.astype(F32))
    tri = ltri_ref[...]
    lcum = (jnp.dot(tri, lw_hi, preferred_element_type=F32) + jnp.dot(tri, lw_mid, preferred_element_type=F32)
            + jnp.dot(tri, lw_lo, preferred_element_type=F32))
    pinc = jnp.exp(lcum)
    pinv = jnp.exp(-lcum)
    rt_ref[0] = r * pinc
    kkt_ref[0] = kk * jnp.exp(lcum - lw)
    kh_ref[0] = k2 * pinv
    bh_ref[0] = kk * a * pinv
    v_ref[0] = v
    pinc_ref[0] = pinc
    bonus_ref[0] = (_head_sums(r * k2 * rk_ref[...], bd_ref[...]) * v).astype(bonus_ref.dtype)


def _dot3(a, b):
    d = lambda x, y: jnp.dot(x, y, preferred_element_type=F32)
    m = a[0].shape[0]
    both = d(jnp.concatenate([a[0], a[1]], axis=0), b[0])
    return both[:m] + both[m:] + d(a[0], b[1])


def _rwkv_scan_kernel(rt_ref, kkt_ref, kh_ref, bh_ref, v_ref, pinc_ref, y_ref, s_ref):
    c = RWKV_CHUNK
    n = RWKV_HEAD
    nb = rt_ref.shape[0]
    heads = range(nb * RWKV_HEADS)

    @pl.when(pl.program_id(1) == 0)
    def _():
        s_ref[...] = jnp.zeros(s_ref.shape, F32)

    row = lax.broadcasted_iota(I32, (c, c), 0)
    col = lax.broadcasted_iota(I32, (c, c), 1)
    eye = (row == col).astype(F32)
    same16 = (row // 16) == (col // 16)
    same32 = (row // 32) == (col // 32)
    row2 = lax.broadcasted_iota(I32, (c, 2 * c), 0)
    col2 = lax.broadcasted_iota(I32, (c, 2 * c), 1) % c
    nt = lambda a, b: lax.dot_general(a, b, (((1,), (1,)), ((), ())), preferred_element_type=F32)
    tn = lambda a, b: lax.dot_general(a, b, (((0,), (0,)), ((), ())), preferred_element_type=F32)
    dot = lambda a, b: jnp.dot(a, b, preferred_element_type=F32)
    sl = [pl.ds((h % RWKV_HEADS) * n, n) for h in heads]
    sq = [h // RWKV_HEADS for h in heads]
    v = [v_ref[sq[h], :, sl[h]] for h in heads]
    pc = [pinc_ref[sq[h], pl.ds(c - 1, 1), sl[h]] for h in heads]
    s = [s_ref[sq[h], :, sl[h]] for h in heads]
    lhs = [jnp.concatenate([kkt_ref[sq[h], :, sl[h]], rt_ref[sq[h], :, sl[h]]], axis=0) for h in heads]
    rhs = [jnp.concatenate([bh_ref[sq[h], :, sl[h]], kh_ref[sq[h], :, sl[h]]], axis=0) for h in heads]
    big = [nt(lhs[h], rhs[h]) for h in heads]
    from_state = [nt(lhs[h], s[h]) for h in heads]
    top = [jnp.where(row2 > col2, big[h][:c], 0.0) for h in heads]
    bot = [jnp.where(row2 >= col2, big[h][c:], 0.0) for h in heads]
    a_b = [top[h][:, :c] for h in heads]
    akv = [dot(top[h], jnp.concatenate([jnp.zeros((c, n), F32), v[h]], axis=0)) for h in heads]
    d16 = [jnp.where(same16, a_b[h], 0.0) for h in heads]
    sd = [_split(d16[h]) for h in heads]
    s2 = [_split(_dot3(sd[h], sd[h])) for h in heads]
    s4 = [_split(_dot3(s2[h], s2[h])) for h in heads]
    s8 = [_split(_dot3(s4[h], s4[h])) for h in heads]
    t = [eye - d16[h] for h in heads]
    for sp in (s2, s4, s8):
        t = [t[h] + _dot3(_split(t[h]), sp[h]) for h in heads]
    for off in ([jnp.where(same32 & jnp.logical_not(same16), a_b[h], 0.0) for h in heads],
                [jnp.where(same32, 0.0, a_b[h]) for h in heads]):
        tb = [t[h].astype(BF16) for h in heads]
        lt = [dot(off[h].astype(BF16), tb[h]).astype(BF16) for h in heads]
        t = [t[h] - dot(tb[h], lt[h]) for h in heads]
    u = [dot(t[h], from_state[h][:c] + akv[h]) for h in heads]
    vu = [jnp.concatenate([-u[h], v[h]], axis=0) for h in heads]
    y = [from_state[h][c:] + dot(bot[h], vu[h]) for h in heads]
    s_new = [s[h] * pc[h] + tn(vu[h], rhs[h] * pc[h]) for h in heads]
    for q in range(nb):
        mine = slice(q * RWKV_HEADS, (q + 1) * RWKV_HEADS)
        y_ref[q] = jnp.concatenate(y[mine], axis=1)
        s_ref[q] = jnp.concatenate(s_new[mine], axis=1)


def _rwkv_post_kernel(y_ref, bonus_ref, g_ref, pg_ref, m_ref, gg_ref, gb_ref, bdm_ref, pj_ref, o_ref):
    y = y_ref[0]
    mu = _head_sums(y, bdm_ref[...])
    d = y - mu
    var = _head_sums(d * d, bdm_ref[...])
    yn = d * lax.rsqrt(var + RWKV_GN_EPS) * gg_ref[...] + gb_ref[...] + bonus_ref[0].astype(F32)
    o = _bdot(yn * g_ref[0].astype(F32), pj_ref[...])
    o_ref[0] = (m_ref[0].astype(F32) + pg_ref[0].astype(F32) * o).astype(o_ref.dtype)


def _rwkv(p, merged, mu_pad, w0, wup_pad, a0, aup_pad, g_up, k_k, k_a, r_k, gn_g, gn_b, proj_bf):
    bsz, seq, _ = p.shape
    row = lambda a: a.reshape(1, -1)
    full = lambda shape: pl.BlockSpec(shape, lambda b, i: (0,) * len(shape))
    head_id = jnp.arange(RWKV_DIM, dtype=I32) // RWKV_HEAD
    bd = (head_id[:, None] == head_id[None, :]).astype(BF16)

    ts = min(seq, 256)
    t_id = jnp.arange(ts, dtype=I32)
    ltri = ((t_id[:, None] // RWKV_CHUNK == t_id[None, :] // RWKV_CHUNK)
            & (t_id[:, None] >= t_id[None, :])).astype(BF16)
    seq_blk = lambda width: pl.BlockSpec((1, ts, width), lambda b, i: (b, i, 0))
    wide = SDS((bsz, seq, RWKV_DIM), F32)
    rt, kkt, kh, bh, v, pinc, bonus, g = pl.pallas_call(
        _rwkv_prep_kernel,
        out_shape=[wide] * 6 + [SDS((bsz, seq, RWKV_DIM), BF16)] * 2,
        grid=(bsz, seq // ts),
        in_specs=[pl.BlockSpec((1, ts, 2048), lambda b, i: (b, i, SEG_B // 2048)),
                  full((1, 2048)), full((1, RWKV_DIM)), full((LANES, RWKV_DIM)),
                  full((1, RWKV_DIM)), full((LANES, RWKV_DIM)), full((LORA_G, RWKV_DIM)),
                  full((1, RWKV_DIM)), full((1, RWKV_DIM)), full((1, RWKV_DIM)),
                  full((RWKV_DIM, RWKV_DIM)), full((ts, ts))],
        out_specs=[seq_blk(RWKV_DIM)] * 8,
        scratch_shapes=[pltpu.VMEM((ts + SUBLANES, 2048), F32)],
        compiler_params=_cparams(("arbitrary", "arbitrary")),
        name="rwkv_prep",
    )(p, mu_pad, row(w0), wup_pad, row(a0), aup_pad, g_up, row(k_k), row(k_a), row(r_k), bd, ltri)

    c = RWKV_CHUNK
    nb = 2 if bsz % 2 == 0 else 1
    chunk_blk = pl.BlockSpec((nb, c, RWKV_DIM), lambda b, i: (b, i, 0))
    y = pl.pallas_call(
        _rwkv_scan_kernel,
        out_shape=wide,
        grid=(bsz // nb, seq // c),
        in_specs=[chunk_blk] * 6,
        out_specs=chunk_blk,
        scratch_shapes=[pltpu.VMEM((nb, RWKV_HEAD, RWKV_DIM), F32)],
        compiler_params=_cparams(("arbitrary", "arbitrary")),
        name="rwkv_scan",
    )(rt, kkt, kh, bh, v, pinc)

    tp = min(seq, 512)
    blk = lambda width: pl.BlockSpec((1, tp, width), lambda b, i: (b, i, 0))
    return pl.pallas_call(
        _rwkv_post_kernel,
        out_shape=SDS((bsz, seq, D_MODEL), MERGED_DTYPE),
        grid=(bsz, seq // tp),
        in_specs=[blk(RWKV_DIM), blk(RWKV_DIM), blk(RWKV_DIM),
                  pl.BlockSpec((1, tp, D_MODEL), lambda b, i: (b, i, SEG_G // D_MODEL + 1)),
                  blk(D_MODEL), full((1, RWKV_DIM)), full((1, RWKV_DIM)),
                  full((RWKV_DIM, RWKV_DIM)), full((RWKV_DIM, D_MODEL))],
        out_specs=blk(D_MODEL),
        compiler_params=_cparams(("arbitrary", "arbitrary")),
        name="rwkv_post",
    )(y, bonus, g, p, merged, row(gn_g), row(gn_b), bd * (1.0 / RWKV_HEAD), proj_bf)


def _out_kernel(m_ref, x_ref, g1_ref, w_ref, n2_ref, sc_ref, sh_ref, xo_ref, h_ref):
    xn = x_ref[0] + g1_ref[0] * _bdot(m_ref[0], w_ref[...])
    xo_ref[0] = xn
    h_ref[0] = _modulated_rmsnorm(xn, n2_ref[...], sc_ref[0], sh_ref[0])


def _out_proj(merged, x, g1, w_bf, norm2, sc2, sh2):
    bsz, seq, _ = x.shape
    ts = min(seq, 512)
    blk = pl.BlockSpec((1, ts, D_MODEL), lambda b, i: (b, i, 0))
    per_b = pl.BlockSpec((1, 1, D_MODEL), lambda b, i: (b, 0, 0))
    return pl.pallas_call(
        _out_kernel,
        out_shape=[SDS((bsz, seq, D_MODEL), F32)] * 2,
        grid=(bsz, seq // ts),
        in_specs=[blk, blk, per_b, pl.BlockSpec((D_MODEL, D_MODEL), lambda b, i: (0, 0)),
                  pl.BlockSpec((1, D_MODEL), lambda b, i: (0, 0)), per_b, per_b],
        out_specs=[blk, blk],
        compiler_params=_cparams(("arbitrary", "arbitrary")),
        name="out_proj",
    )(merged, x, g1, w_bf, norm2, sc2, sh2)


MOE_TILE = 256
MOE_SLOTS = 2560
RUN_LOOP_UNROLL = 4
DISPATCH_PIECE = 48
DISPATCH_SPARE = DISPATCH_PIECE - SUBLANES
DISPATCH_WAIT_ROWS = 2048
assert MOE_SLOTS >= TOP_K * MOE_TILE + N_EXPERTS * (SUBLANES - 1) + DISPATCH_SPARE
U32 = jnp.uint32


ROUTE_TILES = 2


def _route_kernel(h_ref, rw_ref, bias_ref, upper_ref, ltri_ref, slot_ref, w_ref, n8_ref):
    tile = MOE_TILE
    subs = range(h_ref.shape[0] // tile)
    neg = -jnp.inf
    shape3 = (GROUP_SIZE, N_GROUPS, tile)
    to3 = lambda a: a.reshape(shape3)
    lanes_of = lambda a: jnp.concatenate([a] * (tile // LANES), axis=1)
    fold = lambda a: jnp.sum(jnp.sum(a, axis=0), axis=0, keepdims=True)
    slab = lax.broadcasted_iota(I32, shape3, 0).astype(F32)
    grp = lax.broadcasted_iota(I32, shape3, 1).astype(F32)
    eid = grp * GROUP_SIZE + slab
    gi = lax.broadcasted_iota(I32, (N_GROUPS, tile), 0).astype(F32)
    ones_cols = jnp.ones((tile, LANES), BF16)
    s3 = [to3(jax.nn.sigmoid(_hdot_nt(rw_ref[...], h_ref[pl.ds(q * tile, tile), :]))) for q in subs]
    b3 = [s3[q] + to3(bias_ref[...]) for q in subs]
    m1 = [jnp.max(b3[q], axis=0, keepdims=True) for q in subs]
    first = [jnp.min(jnp.where(b3[q] == m1[q], slab, GROUP_SIZE), axis=0, keepdims=True) for q in subs]
    m2 = [jnp.max(jnp.where(slab == first[q], neg, b3[q]), axis=0, keepdims=True) for q in subs]
    gs = [(m1[q] + m2[q])[0] for q in subs]
    chosen = [jnp.zeros((N_GROUPS, tile), F32) for q in subs]
    for _ in range(TOPK_GROUPS):
        m = [jnp.max(gs[q], axis=0, keepdims=True) for q in subs]
        hit = [gi == jnp.min(jnp.where(gs[q] == m[q], gi, N_GROUPS), axis=0, keepdims=True) for q in subs]
        chosen = [jnp.where(hit[q], 1.0, chosen[q]) for q in subs]
        gs = [jnp.where(hit[q], neg, gs[q]) for q in subs]
    cur = [jnp.where((chosen[q] > 0.0)[None], b3[q], neg) for q in subs]
    base = [jnp.zeros((N_EXPERTS, LANES), F32) for q in subs]
    picks, w_rows, rank_rows = [[] for q in subs], [[] for q in subs], [[] for q in subs]
    for _ in range(TOP_K):
        m = [jnp.max(jnp.max(cur[q], axis=0), axis=0, keepdims=True)[None] for q in subs]
        pick = [jnp.min(jnp.min(jnp.where(cur[q] == m[q], eid, N_EXPERTS), axis=0), axis=0, keepdims=True)
                for q in subs]
        hit = [eid == pick[q][None] for q in subs]
        onehot = [hit[q].astype(BF16).reshape(N_EXPERTS, tile) for q in subs]
        before = [jnp.dot(onehot[q], upper_ref[...], preferred_element_type=F32) for q in subs]
        count = [jnp.dot(onehot[q], ones_cols, preferred_element_type=F32) for q in subs]
        for q in subs:
            w_rows[q].append(fold(jnp.where(hit[q], s3[q], 0.0)))
            picks[q].append(pick[q])
            rank_rows[q].append(fold(jnp.where(hit[q], to3(before[q] + lanes_of(base[q])), 0.0)))
        cur = [jnp.where(hit[q], neg, cur[q]) for q in subs]
        base = [base[q] + count[q] for q in subs]
    n8 = [jnp.floor((base[q] + (SUBLANES - 1.0)) * (1.0 / SUBLANES)) * SUBLANES for q in subs]
    run_start = [to3(lanes_of(_hdot(ltri_ref[...], n8[q]))) for q in subs]
    slots, weights = [], []
    for q in subs:
        slots.append(jnp.concatenate(
            [rank_rows[q][j] + fold(jnp.where(eid == picks[q][j][None], run_start[q], 0.0)) for j in range(TOP_K)],
            axis=0))
        w_all = jnp.concatenate(w_rows[q], axis=0)
        weights.append(w_all / jnp.sum(w_all, axis=0, keepdims=True) * ROUTED_SCALE)
    for q in subs:
        n8_ref[q] = n8[q]
    w_ref[...] = jnp.concatenate(weights, axis=1)
    slot_ref[...] = jnp.concatenate(slots, axis=1).astype(I32)


def _route(h2, router_w, router_bias):
    n_tok = h2.shape[0]
    tile = MOE_TILE
    n_tiles = n_tok // tile
    per_step = ROUTE_TILES if n_tiles % ROUTE_TILES == 0 else 1
    regroup = lambda a: a.reshape(N_GROUPS, GROUP_SIZE, -1).transpose(1, 0, 2).reshape(N_EXPERTS, -1)
    rw = regroup(router_w.T)
    bias = jnp.broadcast_to(regroup(router_bias.reshape(N_EXPERTS, 1)), (N_EXPERTS, tile))
    t_id = jnp.arange(tile, dtype=I32)
    upper = (t_id[:, None] < t_id[None, :]).astype(BF16)
    e_id = jnp.arange(N_EXPERTS, dtype=I32)
    ltri = (e_id[:, None] > e_id[None, :]).astype(F32)
    tok_blk = pl.BlockSpec((TOP_K, per_step * tile), lambda i: (0, i))
    full = lambda shape: pl.BlockSpec(shape, lambda i: (0,) * len(shape))
    slot_t, w_t, n8 = pl.pallas_call(
        _route_kernel,
        out_shape=[SDS((TOP_K, n_tok), I32), SDS((TOP_K, n_tok), F32),
                   SDS((n_tiles, N_EXPERTS, LANES), F32)],
        grid=(n_tiles // per_step,),
        in_specs=[pl.BlockSpec((per_step * tile, D_MODEL), lambda i: (i, 0)),
                  full((N_EXPERTS, D_MODEL)), full((N_EXPERTS, tile)), full((tile, tile)),
                  full((N_EXPERTS, N_EXPERTS))],
        out_specs=[tok_blk, tok_blk, pl.BlockSpec((per_step, N_EXPERTS, LANES), lambda i: (i, 0, 0))],
        compiler_params=_cparams(("arbitrary",)),
        name="moe_route",
    )(h2, rw, bias, upper, ltri)
    return slot_t, w_t, n8[:, :, 0].astype(I32)


def _for_each_run_piece(tile_idx, n8_ref, fn):
    for r in range(N_EXPERTS):
        n = n8_ref[tile_idx * N_EXPERTS + r]
        size = MOE_TILE
        while size >= SUBLANES:
            @pl.when((n & size) != 0)
            def _(size=size):
                fn(r, n & ~(2 * size - 1), size)
            size //= 2


def _wait_rows(total, make_wait):
    size = SUBLANES
    while size <= MOE_SLOTS:
        @pl.when((total & size) != 0)
        def _(size=size):
            make_wait(size).wait()
        size *= 2


def _pack_pairs(hi_bits, lo_bits):
    return (hi_bits & jnp.uint32(0xFFFF0000)) | (lo_bits >> 16)


def _unpack_pairs(u):
    hi = lax.bitcast_convert_type(u & jnp.uint32(0xFFFF0000), F32)
    lo = lax.bitcast_convert_type(u << 16, F32)
    return jnp.concatenate([hi, lo], axis=1).astype(BF16)


def _dispatch_kernel(n8_ref, off_ref, dst_ref, tot_ref, pad_end_ref, slot_ref, h_ref, xs_ref,
                     g_ref, zero_ref, sem):
    i = pl.program_id(0)
    tile = h_ref.shape[0]
    half = D_MODEL // 2

    @pl.when(i == 0)
    def _():
        zero_ref[...] = jnp.zeros(zero_ref.shape, U32)

        def last_block(e, back):
            start = pl.multiple_of(jnp.maximum(pad_end_ref[e] - back * MOE_ROWS, 0), MOE_ROWS)
            return pltpu.make_async_copy(zero_ref, xs_ref.at[pl.ds(start, MOE_ROWS), :], sem.at[0])

        def has_blocks(e, back):
            prev = jnp.where(e > 0, pad_end_ref[jnp.maximum(e - 1, 0)], 0)
            return pad_end_ref[e] - prev >= back * MOE_ROWS

        def clear(e, carry):
            for back in (1, 2):
                pl.when(has_blocks(e, back))(lambda back=back: last_block(e, back).start())
            return carry
        lax.fori_loop(0, N_EXPERTS, clear, 0)

        def done(e, carry):
            for back in (1, 2):
                pl.when(has_blocks(e, back))(lambda back=back: last_block(e, back).wait())
            return carry
        lax.fori_loop(0, N_EXPERTS, done, 0)

        def tail_block(b):
            start = pl.multiple_of(b * MOE_ROWS, MOE_ROWS)
            return pltpu.make_async_copy(zero_ref, xs_ref.at[pl.ds(start, MOE_ROWS), :], sem.at[0])

        def clear_tail(b, carry):
            tail_block(b).start()
            return carry

        def done_tail(b, carry):
            tail_block(b).wait()
            return carry
        used = pad_end_ref[N_EXPERTS - 1] // MOE_ROWS
        lax.fori_loop(used, xs_ref.shape[0] // MOE_ROWS, clear_tail, 0)
        lax.fori_loop(used, xs_ref.shape[0] // MOE_ROWS, done_tail, 0)

    slot_id = lax.broadcasted_iota(I16, (MOE_SLOTS, tile), 0)
    slots = slot_ref[...].astype(I16)
    sel = jnp.zeros((MOE_SLOTS, tile), BF16)
    for j in range(TOP_K):
        sel = jnp.where(slot_id == slots[j:j + 1, :], jnp.ones((), BF16), sel)
    g = jnp.dot(sel, h_ref[...].astype(BF16), preferred_element_type=F32)
    bits = lax.bitcast_convert_type(g, U32)
    buf = i % 2
    g_ref[buf] = _pack_pairs(bits[:, :half], bits[:, half:])

    def drain(step):
        total = tot_ref[step]
        whole = pltpu.make_async_copy(g_ref.at[step % 2, pl.ds(0, DISPATCH_WAIT_ROWS), :],
                                      xs_ref.at[pl.ds(0, DISPATCH_WAIT_ROWS), :], sem.at[step % 2])

        def wait_whole(k, carry):
            whole.wait()
            return carry
        lax.fori_loop(0, lax.shift_right_logical(total, DISPATCH_WAIT_ROWS.bit_length() - 1), wait_whole, 0)
        _wait_rows(total & (DISPATCH_WAIT_ROWS - 1), lambda rows: pltpu.make_async_copy(
            g_ref.at[step % 2, pl.ds(0, rows), :], xs_ref.at[pl.ds(0, rows), :], sem.at[step % 2]))

    pl.when(i > 0)(lambda: drain(i - 1))

    def copy(r, k):
        src = pl.multiple_of(off_ref[i * N_EXPERTS + r] + k * DISPATCH_PIECE, SUBLANES)
        dst = pl.multiple_of(dst_ref[i * N_EXPERTS + r] + k * DISPATCH_PIECE, SUBLANES)
        pltpu.make_async_copy(g_ref.at[buf, pl.ds(src, DISPATCH_PIECE), :],
                              xs_ref.at[pl.ds(dst, DISPATCH_PIECE), :], sem.at[buf]).start()

    def per_expert(r, carry):
        n = n8_ref[i * N_EXPERTS + r]
        pl.when(n > 0)(lambda: copy(r, 0))

        @pl.when(n > DISPATCH_PIECE)
        def _():
            def more(k, c):
                copy(r, k)
                return c
            lax.fori_loop(1, (n + (DISPATCH_PIECE - 1)) // DISPATCH_PIECE, more, 0)
        return carry
    lax.fori_loop(0, N_EXPERTS, per_expert, 0, unroll=RUN_LOOP_UNROLL)
    pl.when(i == pl.num_programs(0) - 1)(lambda: drain(i))


def _dispatch(h2, slot_t, n8_flat, off_flat, dst_flat, tot, pad_end, n_rows):
    n_tok = h2.shape[0]
    tile = MOE_TILE
    grid_spec = pltpu.PrefetchScalarGridSpec(
        num_scalar_prefetch=5,
        grid=(n_tok // tile,),
        in_specs=[pl.BlockSpec((TOP_K, tile), lambda i, *_: (0, i)),
                  pl.BlockSpec((tile, D_MODEL), lambda i, *_: (i, 0))],
        out_specs=pl.BlockSpec(memory_space=pl.ANY),
        scratch_shapes=[pltpu.VMEM((2, MOE_SLOTS, D_MODEL // 2), U32),
                        pltpu.VMEM((MOE_ROWS, D_MODEL // 2), U32), pltpu.SemaphoreType.DMA((2,))],
    )
    return pl.pallas_call(
        _dispatch_kernel,
        out_shape=SDS((n_rows, D_MODEL // 2), U32),
        grid_spec=grid_spec,
        compiler_params=_cparams(("arbitrary",)),
        name="moe_dispatch",
    )(n8_flat, off_flat, dst_flat, tot, pad_end, slot_t, h2)


def _expert_kernel(first_ref, nblk_ref, exp_of_ref, xs_ref, w1_ref, w3_ref, w2_ref, y_ref,
                   xbuf, ybuf, wb1, wb3, wb2, in_sem, out_sem):
    del exp_of_ref
    r = pl.program_id(0)
    half = D_MODEL // 2
    used = first_ref[N_EXPERTS - 1] + nblk_ref[N_EXPERTS - 1]
    part = MOE_ROWS // EXPERT_DMA_PARTS

    class _Copies:
        def __init__(self, make):
            self.parts = [make(p) for p in range(EXPERT_DMA_PARTS)]

        def start(self):
            for c in self.parts:
                c.start()

        def wait(self):
            for c in self.parts:
                c.wait()

    def hbm_rows(g, p):
        return pl.ds(pl.multiple_of(g * MOE_ROWS + p * part, part), part)

    in_copy = lambda g, slot: _Copies(lambda p: pltpu.make_async_copy(
        xs_ref.at[hbm_rows(g, p), :], xbuf.at[slot, pl.ds(p * part, part), :], in_sem.at[slot]))
    out_copy = lambda g, slot: _Copies(lambda p: pltpu.make_async_copy(
        ybuf.at[slot, pl.ds(p * part, part), :], y_ref.at[hbm_rows(g, p), :], out_sem.at[slot]))

    @pl.when(r == 0)
    def _():
        for g in range(EXPERT_AHEAD):
            pl.when(g < used)(lambda g=g: in_copy(g, g).start())

    wb1[...] = w1_ref[0, 0].astype(BF16)
    wb3[...] = w3_ref[0, 0].astype(BF16)
    wb2[...] = w2_ref[0, 0].astype(BF16)

    def process(blocks):
        dot = lambda a, b: jnp.dot(a, b, preferred_element_type=F32)
        slots = [g % EXPERT_BUFS for g in blocks]
        for g, slot in zip(blocks, slots):
            in_copy(g, slot).wait()
            ahead = g + EXPERT_AHEAD
            pl.when(ahead < used)(lambda ahead=ahead: in_copy(ahead, ahead % EXPERT_BUFS).start())
            pl.when(g >= EXPERT_BUFS)(lambda g=g, slot=slot: out_copy(g - EXPERT_BUFS, slot).wait())
        x = [_unpack_pairs(xbuf[slot]) for slot in slots]
        gate = [dot(xq, wb1[...]) for xq in x]
        up = [dot(xq, wb3[...]) for xq in x]
        hid = [(jax.nn.silu(a) * b).astype(BF16) for a, b in zip(gate, up)]
        for g, slot, hq in zip(blocks, slots, hid):
            bits = lax.bitcast_convert_type(dot(hq, wb2[...]).astype(BF16).astype(F32), U32)
            ybuf[slot] = _pack_pairs(bits[:, :half], bits[:, half:])
            out_copy(g, slot).start()

    def pair(k, carry):
        g = first_ref[r] + 2 * k
        process([g, g + 1])
        return carry
    n_mine = nblk_ref[r]
    lax.fori_loop(0, lax.shift_right_logical(n_mine, 1), pair, 0)
    pl.when((n_mine & 1) == 1)(lambda: process([first_ref[r] + n_mine - 1]))

    @pl.when(r == N_EXPERTS - 1)
    def _():
        for back in range(EXPERT_BUFS, 0, -1):
            pl.when(used >= back)(
                lambda back=back: out_copy(used - back, (used - back) % EXPERT_BUFS).wait())

        ybuf[0] = jnp.zeros(ybuf.shape[1:], U32)
        n_blocks = y_ref.shape[0] // MOE_ROWS

        def clear(g, carry):
            out_copy(g, 0).start()
            return carry

        def done(g, carry):
            out_copy(g, 0).wait()
            return carry
        lax.fori_loop(used, n_blocks, clear, 0)
        lax.fori_loop(used, n_blocks, done, 0)


def _experts(xs, first_block, n_block, exp_of_row, w1, w3, w2, layer):
    n_rows = xs.shape[0]
    half = D_MODEL // 2
    w_in_blk = pl.BlockSpec((1, 1, D_MODEL, EXPERT_FF), lambda r, first, nblk, eo: (layer, eo[r], 0, 0))
    grid_spec = pltpu.PrefetchScalarGridSpec(
        num_scalar_prefetch=3,
        grid=(N_EXPERTS,),
        in_specs=[pl.BlockSpec(memory_space=pl.ANY), w_in_blk, w_in_blk,
                  pl.BlockSpec((1, 1, EXPERT_FF, D_MODEL), lambda r, first, nblk, eo: (layer, eo[r], 0, 0))],
        out_specs=pl.BlockSpec(memory_space=pl.ANY),
        scratch_shapes=[pltpu.VMEM((EXPERT_BUFS, MOE_ROWS, half), U32),
                        pltpu.VMEM((EXPERT_BUFS, MOE_ROWS, half), U32),
                        pltpu.VMEM((D_MODEL, EXPERT_FF), BF16), pltpu.VMEM((D_MODEL, EXPERT_FF), BF16),
                        pltpu.VMEM((EXPERT_FF, D_MODEL), BF16),
                        pltpu.SemaphoreType.DMA((EXPERT_BUFS,)), pltpu.SemaphoreType.DMA((EXPERT_BUFS,))],
    )
    return pl.pallas_call(
        _expert_kernel,
        out_shape=SDS((n_rows, half), U32),
        grid_spec=grid_spec,
        compiler_params=_cparams(("arbitrary",)),
        name="moe_experts",
    )(first_block, n_block, exp_of_row, xs, w1, w3, w2)


def _combine_kernel(n8_ref, off_ref, src_ref, tot_ref, y_ref, slot_ref, w_ref, h_ref, x_ref, g2_ref,
                    s1_ref, s3_ref, s2_ref, fn_ref, o_ref, yt_ref, sem, *, final):
    tile = h_ref.shape[1]
    i = pl.program_id(0) * pl.num_programs(1) + pl.program_id(1)

    @pl.when(i == 0)
    def _():
        yt_ref[...] = jnp.zeros(yt_ref.shape, U32)

    def piece(r, offset, rows):
        src = pl.multiple_of(src_ref[i * N_EXPERTS + r] + offset, SUBLANES)
        dst = pl.multiple_of(off_ref[i * N_EXPERTS + r] + offset, SUBLANES)
        pltpu.make_async_copy(y_ref.at[pl.ds(src, rows), :], yt_ref.at[pl.ds(dst, rows), :], sem).start()

    _for_each_run_piece(i, n8_ref, piece)
    h = h_ref[0]
    shared = _bdot(jax.nn.silu(_bdot(h, s1_ref[...])) * _bdot(h, s3_ref[...]), s2_ref[...])
    slot_id = lax.broadcasted_iota(I16, (tile, MOE_SLOTS), 1)
    slots = slot_ref[0].astype(I16)
    w = w_ref[0].astype(BF16)
    pw = jnp.zeros((tile, MOE_SLOTS), BF16)
    for j in range(TOP_K):
        pw = jnp.where(slot_id == slots[:, j:j + 1], w[:, j:j + 1], pw)
    _wait_rows(tot_ref[i], lambda rows: pltpu.make_async_copy(
        y_ref.at[pl.ds(0, rows), :], yt_ref.at[pl.ds(0, rows), :], sem))
    routed = jnp.dot(pw, _unpack_pairs(yt_ref[...]), preferred_element_type=F32)
    xn = x_ref[0] + g2_ref[0] * (routed + shared)
    if final:
        xn = xn * lax.rsqrt(jnp.mean(xn * xn, axis=-1, keepdims=True) + NORM_EPS) * fn_ref[...]
    o_ref[0] = xn


def _combine(y, slot_nat, w_nat, n8_flat, off_flat, dst_flat, tot, h2, x, g2, s1_bf, s3_bf, s2_bf,
             final_norm, final):
    bsz, seq, _ = x.shape
    tile = MOE_TILE
    per_seq = seq // tile
    blk = pl.BlockSpec((1, tile, D_MODEL), lambda b, i, *_: (b, i, 0))
    tok = pl.BlockSpec((1, tile, TOP_K), lambda b, i, *_: (b, i, 0))
    full = lambda shape: pl.BlockSpec(shape, lambda b, i, *_: (0,) * len(shape))
    grid_spec = pltpu.PrefetchScalarGridSpec(
        num_scalar_prefetch=4,
        grid=(bsz, per_seq),
        in_specs=[pl.BlockSpec(memory_space=pl.ANY), tok, tok, blk, blk,
                  pl.BlockSpec((1, 1, D_MODEL), lambda b, i, *_: (b, 0, 0)),
                  full((D_MODEL, SHARED_FF)), full((D_MODEL, SHARED_FF)), full((SHARED_FF, D_MODEL)),
                  full((1, D_MODEL))],
        out_specs=blk,
        scratch_shapes=[pltpu.VMEM((MOE_SLOTS, D_MODEL // 2), U32), pltpu.SemaphoreType.DMA],
    )
    return pl.pallas_call(
        functools.partial(_combine_kernel, final=final),
        out_shape=SDS((bsz, seq, D_MODEL), F32),
        grid_spec=grid_spec,
        compiler_params=_cparams(("arbitrary", "arbitrary")),
        name="moe_combine",
    )(n8_flat, off_flat, dst_flat, tot, y, slot_nat, w_nat, h2, x, g2, s1_bf, s3_bf, s2_bf, final_norm)


def _moe(x, h2, g2, router_w, router_bias, w1, w3, w2, layer, s1_bf, s3_bf, s2_bf, final_norm, final):
    bsz, seq, _ = x.shape
    assert seq % MOE_TILE == 0, "token tiles must not straddle sequences"
    n_tok = bsz * seq
    slot_t, w_t, n8 = _route(h2.reshape(n_tok, D_MODEL), router_w, router_bias)
    n_tiles = n8.shape[0]
    counts = jnp.sum(n8, axis=0)
    padded = jnp.where(counts > 0, (counts + DISPATCH_SPARE + MOE_ROWS - 1) // MOE_ROWS * MOE_ROWS, 0)
    pad_end = jnp.cumsum(padded).astype(I32)
    pad_start = pad_end - padded
    run_row = (pad_start[None, :] + jnp.cumsum(n8, axis=0) - n8).astype(I32)
    run_slot = (jnp.cumsum(n8, axis=1) - n8).astype(I32)
    tot = jnp.sum(n8, axis=1).astype(I32)
    copied = jnp.sum((n8 + DISPATCH_PIECE - 1) // DISPATCH_PIECE * DISPATCH_PIECE, axis=1).astype(I32)
    max_rows = (n_tok * TOP_K + n_tiles * N_EXPERTS * (SUBLANES - 1)
                + N_EXPERTS * (DISPATCH_SPARE + MOE_ROWS - 1))
    n_blocks = (max_rows + MOE_ROWS - 1) // MOE_ROWS
    row_id = jnp.arange(N_EXPERTS, dtype=I32)
    exp_of_row = (row_id % N_GROUPS) * GROUP_SIZE + row_id // N_GROUPS
    flat = lambda a: a.reshape(-1).astype(I32)
    xs = _dispatch(h2.reshape(n_tok, D_MODEL), slot_t, flat(n8), flat(run_slot), flat(run_row), copied,
                   pad_end, n_blocks * MOE_ROWS)
    y = _experts(xs, (pad_start // MOE_ROWS).astype(I32), (padded // MOE_ROWS).astype(I32), exp_of_row,
                 w1, w3, w2, layer)
    nat = lambda a: a.T.reshape(bsz, seq, TOP_K)
    return _combine(y, nat(slot_t), nat(w_t), flat(n8), flat(run_slot), flat(run_row), tot, h2, x, g2,
                    s1_bf, s3_bf, s2_bf, final_norm, final)


def _pad_cols(a, width):
    return jnp.pad(a, ((0, 0), (0, width - a.shape[1])))


def _layout_w_in(w_in):
    b0 = COLS_A
    c0 = COLS_A + COLS_B
    g0 = c0 + COLS_C
    seg_b = _pad_cols(w_in[:, b0:c0], SEG_G - SEG_B)
    return jnp.concatenate([w_in[:, c0:g0], seg_b, w_in[:, g0:], w_in[:, :b0]], axis=1).astype(BF16)


def _layout_mu(mu):
    return _pad_cols(mu.reshape(1, -1), SEG_G - SEG_B)


def _pad_rows(a, height):
    return jnp.pad(a, ((0, height - a.shape[0]), (0, 0)))


def kernel(x, c, ada_w, ada_b, norm1, norm2, w_in, conv_a_w, conv_a_b, ln_a_g, ln_a_b, proj_a, mu_b, w0, w_up, a0, a_up, g_up, k_k, k_a, r_k, gn_b_g, gn_b_b, proj_b, conv_c_w, conv_c_b, lru_wa, lru_ba, lru_wx, lru_bx, lru_lambda, proj_c, w_out, router_w, router_bias, exp_w1, exp_w3, exp_w2, sh_w1, sh_w3, sh_w2, final_norm):
    depth = ada_w.shape[0]
    bsz = x.shape[0]
    mod = _ada_mod(c, ada_w, ada_b)
    for l in range(depth):
        sh1, sc1, g1, sh2, sc2, g2 = [mod[l, :, i * D_MODEL:(i + 1) * D_MODEL].reshape(bsz, 1, D_MODEL)
                                      for i in range(N_MOD)]
        p = _in_proj(x, sc1, sh1, norm1[l].reshape(1, -1), _layout_w_in(w_in[l]))
        merged = _conv_a(p, conv_a_w[l], conv_a_b[l], ln_a_g[l], ln_a_b[l], proj_a[l].astype(BF16))
        merged = _rwkv(p, merged, _layout_mu(mu_b[l]), w0[l], _pad_rows(w_up[l], LANES).astype(BF16),
                       a0[l], jnp.pad(a_up[l], ((LORA_W, 0), (0, 0))).astype(BF16), g_up[l].astype(BF16),
                       k_k[l], k_a[l], r_k[l], gn_b_g[l], gn_b_b[l], proj_b[l].astype(BF16))
        merged = _lru(p, merged, conv_c_w[l], conv_c_b[l], lru_wa[l].astype(BF16), lru_ba[l],
                      lru_wx[l].astype(BF16), lru_bx[l], lru_lambda[l], proj_c[l].astype(BF16))
        x, h2 = _out_proj(merged, x, g1, w_out[l].astype(BF16), norm2[l].reshape(1, -1), sc2, sh2)
        x = _moe(x, h2, g2, router_w[l], router_bias[l], exp_w1, exp_w3, exp_w2, l,
                 sh_w1[l].astype(BF16), sh_w3[l].astype(BF16), sh_w2[l].astype(BF16),
                 final_norm.reshape(1, -1), final=(l == depth - 1))
    return x
```

```python
import functools

import jax
import jax.numpy as jnp
from jax import lax
from jax.experimental import pallas as pl
from jax.experimental.pallas import tpu as pltpu

F32 = jnp.float32
BF16 = jnp.bfloat16
I32 = jnp.int32
I16 = jnp.int16
SDS = jax.ShapeDtypeStruct
HIGHEST = lax.Precision.HIGHEST

D_MODEL = 1024
N_MOD = 6
NORM_EPS = 1e-6
CONV_A_CH = 512
CONV_A_WIDTH = 31
CONV_A_LN_EPS = 1e-5
RWKV_HEADS = 8
RWKV_HEAD = 64
RWKV_DIM = RWKV_HEADS * RWKV_HEAD
LORA_W = 64
LORA_A = 64
LORA_G = 128
RWKV_GN_EPS = 64e-5
RWKV_CHUNK = 64
LRU_DIM = 1024
LRU_HEADS = 8
LRU_BLOCK = LRU_DIM // LRU_HEADS
LRU_CONV = 4
LRU_C = 8.0
N_EXPERTS = 64
TOP_K = 8
N_GROUPS = 8
GROUP_SIZE = N_EXPERTS // N_GROUPS
TOPK_GROUPS = 4
EXPERT_FF = 256
SHARED_FF = 256
ROUTED_SCALE = 2.5
SEG_C = 0
SEG_B = 2048
SEG_G = 4096
SEG_A = 7168
IN_COLS_PAD = 8192
RW_R, RW_K, RW_V, RW_XWA, RW_XG = 0, 512, 1024, 1536, 1664
COLS_A = 2 * CONV_A_CH
COLS_B = 3 * RWKV_DIM + LORA_W + LORA_A + LORA_G
COLS_C = 2 * LRU_DIM
VMEM_LIMIT = 56 * 1024 * 1024
MERGED_DTYPE = BF16
SUBLANES = 8
LANES = 128
MOE_ROWS = 512
EXPERT_DMA_PARTS = 4
EXPERT_AHEAD = 4
EXPERT_BUFS = EXPERT_AHEAD + 2


def _cparams(sem):
    return pltpu.CompilerParams(dimension_semantics=sem, vmem_limit_bytes=VMEM_LIMIT)


def _bdot(a, b):
    return jnp.dot(a.astype(BF16), b.astype(BF16), preferred_element_type=F32)


def _hdot(a, b):
    return jnp.dot(a, b, preferred_element_type=F32, precision=HIGHEST)


def _split(a):
    hi = a.astype(BF16)
    return hi, (a - hi.astype(F32)).astype(BF16)


def _head_sums(a, ones_bf):
    hi, lo = _split(a)
    return (jnp.dot(hi, ones_bf, preferred_element_type=F32)
            + jnp.dot(lo, ones_bf, preferred_element_type=F32))


def _hdot_nt(a, b):
    return lax.dot_general(a, b, (((1,), (1,)), ((), ())), preferred_element_type=F32,
                           precision=HIGHEST)


def _ada_kernel(c_ref, w_ref, b_ref, o_ref):
    cond = jax.nn.silu(c_ref[...])
    o_ref[0] = _bdot(cond, w_ref[0]) + b_ref[0]


def _ada_mod(c, ada_w, ada_b):
    depth, _, n = ada_w.shape
    bsz = c.shape[0]
    tn = 1536
    return pl.pallas_call(
        _ada_kernel,
        out_shape=SDS((depth, bsz, n), F32),
        grid=(depth, n // tn),
        in_specs=[pl.BlockSpec((bsz, D_MODEL), lambda l, j: (0, 0)),
                  pl.BlockSpec((1, D_MODEL, tn), lambda l, j: (l, 0, j)),
                  pl.BlockSpec((1, 1, tn), lambda l, j: (l, 0, j))],
        out_specs=pl.BlockSpec((1, bsz, tn), lambda l, j: (l, 0, j)),
        compiler_params=_cparams(("arbitrary", "arbitrary")),
        name="ada_mod",
    )(c, ada_w, ada_b.reshape(depth, 1, n))


def _modulated_rmsnorm(x, g, sc, sh):
    y = x * lax.rsqrt(jnp.mean(x * x, axis=-1, keepdims=True) + NORM_EPS)
    return (y * g) * (1.0 + sc) + sh


IN_PROJ_SUB = 512


def _in_kernel(x_ref, sc_ref, sh_ref, g_ref, w_ref, o_ref, h_ref):
    @pl.when(pl.program_id(2) == 0)
    def _():
        h_ref[...] = _modulated_rmsnorm(x_ref[0], g_ref[...], sc_ref[0], sh_ref[0]).astype(BF16)

    j = pl.program_id(2)
    ts, tn = o_ref.shape[1], o_ref.shape[2]
    is_gelu = j == SEG_C // tn
    is_gate = (j >= SEG_G // tn) & (j < SEG_A // tn)
    is_glu = j == SEG_A // tn

    def glu(acc):
        u = acc[:, :tn // 2] * jax.nn.sigmoid(acc[:, tn // 2:])
        return jnp.concatenate([u, jnp.zeros_like(u)], axis=1)

    def emit(act, sub):
        for r0 in range(0, ts, sub):
            rows = pl.ds(r0, sub)
            acc = jnp.dot(h_ref[rows, :], w_ref[...], preferred_element_type=F32)
            o_ref[0, rows, :] = act(acc).astype(o_ref.dtype)

    sub = min(ts, IN_PROJ_SUB)
    pl.when(is_gelu)(lambda: emit(functools.partial(jax.nn.gelu, approximate=True), sub))
    pl.when(is_gate)(lambda: emit(jax.nn.sigmoid, sub))
    pl.when(is_glu)(lambda: emit(glu, sub))
    pl.when(jnp.logical_not(is_gelu | is_gate | is_glu))(lambda: emit(lambda acc: acc, ts))


def _in_proj(x, sc, sh, g, w_pad):
    bsz, seq, _ = x.shape
    ts = min(seq, 2048)
    tn = 1024
    return pl.pallas_call(
        _in_kernel,
        out_shape=SDS((bsz, seq, IN_COLS_PAD), BF16),
        grid=(bsz, seq // ts, IN_COLS_PAD // tn),
        in_specs=[pl.BlockSpec((1, ts, D_MODEL), lambda b, i, j: (b, i, 0)),
                  pl.BlockSpec((1, 1, D_MODEL), lambda b, i, j: (b, 0, 0)),
                  pl.BlockSpec((1, 1, D_MODEL), lambda b, i, j: (b, 0, 0)),
                  pl.BlockSpec((1, D_MODEL), lambda b, i, j: (0, 0)),
                  pl.BlockSpec((D_MODEL, tn), lambda b, i, j: (0, j))],
        out_specs=pl.BlockSpec((1, ts, tn), lambda b, i, j: (b, i, j)),
        scratch_shapes=[pltpu.VMEM((ts, D_MODEL), BF16)],
        compiler_params=_cparams(("arbitrary", "arbitrary", "arbitrary")),
        name="in_proj",
    )(x, sc, sh, g, w_pad)


CONV_A_HALO = 32
CONV_A_SUB = 64


def _conv_a_kernel(pa_ref, pg_ref, cw_ref, cb_ref, lg_ref, lb_ref, pj_ref, o_ref, ext_ref, sh_ref, y_ref):
    ts = pa_ref.shape[1]

    @pl.when(pl.program_id(1) == 0)
    def _():
        ext_ref[pl.ds(0, CONV_A_HALO), :] = jnp.zeros((CONV_A_HALO, CONV_A_CH), F32)

    ext_ref[pl.ds(CONV_A_HALO, ts), :] = pa_ref[0].astype(F32)
    for p in range(1, SUBLANES):
        sh_ref[p - 1] = ext_ref[pl.ds(p, sh_ref.shape[1]), :]
    first = CONV_A_HALO - (CONV_A_WIDTH - 1)
    for r0 in range(0, ts, CONV_A_SUB):
        acc = jnp.zeros((CONV_A_SUB, CONV_A_CH), F32) + cb_ref[...]
        for j in range(CONV_A_WIDTH):
            phase = (first + j) % SUBLANES
            rows = pl.ds(r0 + first + j - phase, CONV_A_SUB)
            tap = ext_ref[rows, :] if phase == 0 else sh_ref[phase - 1, rows, :]
            acc = acc + tap * cw_ref[pl.ds(j, 1), :]
        y_ref[pl.ds(r0, CONV_A_SUB), :] = acc
    ext_ref[pl.ds(0, CONV_A_HALO), :] = ext_ref[pl.ds(ts, CONV_A_HALO), :]
    y = y_ref[...]
    mu = jnp.mean(y, axis=-1, keepdims=True)
    d = y - mu
    var = jnp.mean(d * d, axis=-1, keepdims=True)
    yn = d * lax.rsqrt(var + CONV_A_LN_EPS) * lg_ref[...] + lb_ref[...]
    o = _bdot(jax.nn.silu(yn), pj_ref[...])
    o_ref[0] = (pg_ref[0].astype(F32) * o).astype(o_ref.dtype)


def _conv_a(p, conv_w, conv_b, ln_g, ln_b, proj_bf):
    bsz, seq, _ = p.shape
    ts = min(seq, 1024)
    row = lambda a: a.reshape(1, -1)
    full = lambda shape: pl.BlockSpec(shape, lambda b, i: (0,) * len(shape))
    return pl.pallas_call(
        _conv_a_kernel,
        out_shape=SDS((bsz, seq, D_MODEL), MERGED_DTYPE),
        grid=(bsz, seq // ts),
        in_specs=[pl.BlockSpec((1, ts, CONV_A_CH), lambda b, i: (b, i, SEG_A // CONV_A_CH)),
                  pl.BlockSpec((1, ts, D_MODEL), lambda b, i: (b, i, SEG_G // D_MODEL)),
                  full((CONV_A_WIDTH, CONV_A_CH)), full((1, CONV_A_CH)), full((1, CONV_A_CH)),
                  full((1, CONV_A_CH)), full((CONV_A_CH, D_MODEL))],
        out_specs=pl.BlockSpec((1, ts, D_MODEL), lambda b, i: (b, i, 0)),
        scratch_shapes=[pltpu.VMEM((ts + CONV_A_HALO, CONV_A_CH), F32),
                        pltpu.VMEM((SUBLANES - 1, ts + CONV_A_HALO - SUBLANES, CONV_A_CH), F32),
                        pltpu.VMEM((ts, CONV_A_CH), F32)],
        compiler_params=_cparams(("arbitrary", "arbitrary")),
        name="conv_a",
    )(p, p, conv_w, row(conv_b), row(ln_g), row(ln_b), proj_bf)


def _lru_kernel(pc_ref, pg_ref, m_ref, cw_ref, cb_ref, wa_ref, ba_ref, wx_ref, bx_ref, lam_ref,
                pj_ref, o_ref, ext_ref, h_ref, a_ref, b_ref):
    ts = pc_ref.shape[1]
    groups = ts // SUBLANES

    @pl.when(pl.program_id(1) == 0)
    def _():
        ext_ref[pl.ds(0, SUBLANES), :] = jnp.zeros((SUBLANES, LRU_DIM), F32)
        h_ref[...] = jnp.zeros((SUBLANES, LRU_DIM), F32)

    pc = pc_ref[0].astype(F32)
    y_gate = pc[:, :LRU_DIM]
    ext_ref[pl.ds(SUBLANES, ts), :] = pc[:, LRU_DIM:]
    first = SUBLANES - (LRU_CONV - 1)
    xc = jnp.zeros((ts, LRU_DIM), F32) + cb_ref[...]
    for j in range(LRU_CONV):
        xc = xc + ext_ref[pl.ds(first + j, ts), :] * cw_ref[pl.ds(j, 1), :]
    ext_ref[pl.ds(0, SUBLANES), :] = ext_ref[pl.ds(ts, SUBLANES), :]

    def block_diag(w_ref):
        return jnp.concatenate(
            [_bdot(xc[:, h * LRU_BLOCK:(h + 1) * LRU_BLOCK], w_ref[h]) for h in range(LRU_HEADS)],
            axis=1)

    gate_a = jax.nn.sigmoid(block_diag(wa_ref) + ba_ref[...])
    gate_x = jax.nn.sigmoid(block_diag(wx_ref) + bx_ref[...])
    log_a = -LRU_C * gate_a * jax.nn.softplus(-lam_ref[...])
    a = jnp.exp(log_a)
    b = xc * gate_x * jnp.sqrt(1.0 - jnp.exp(2.0 * log_a))

    a3 = a.reshape(groups, SUBLANES, LRU_DIM)
    b3 = b.reshape(groups, SUBLANES, LRU_DIM)
    row = lax.broadcasted_iota(I32, (groups, SUBLANES, LRU_DIM), 1)
    for s in (1, 2, 4):
        keep = row >= s
        b3 = jnp.where(keep, a3 * pltpu.roll(b3, s, axis=1) + b3, b3)
        a3 = jnp.where(keep, a3 * pltpu.roll(a3, s, axis=1), a3)
    a_ref[...] = a3.reshape(ts, LRU_DIM)
    b_ref[...] = b3.reshape(ts, LRU_DIM)
    h = h_ref[...]
    for g in range(groups):
        rows = pl.ds(g * SUBLANES, SUBLANES)
        hg = a_ref[rows, :] * h + b_ref[rows, :]
        b_ref[rows, :] = hg
        h = jnp.broadcast_to(hg[SUBLANES - 1:SUBLANES, :], (SUBLANES, LRU_DIM))
    h_ref[...] = h
    o = _bdot(b_ref[...] * y_gate, pj_ref[...])
    o_ref[0] = (m_ref[0].astype(F32) + pg_ref[0].astype(F32) * o).astype(o_ref.dtype)


def _lru(p, merged, conv_w, conv_b, wa_bf, ba, wx_bf, bx, lam, proj_bf):
    bsz, seq, _ = p.shape
    ts = min(seq, 512)
    row = lambda a: a.reshape(1, -1)
    full = lambda shape: pl.BlockSpec(shape, lambda b, i: (0,) * len(shape))
    return pl.pallas_call(
        _lru_kernel,
        out_shape=SDS((bsz, seq, D_MODEL), MERGED_DTYPE),
        grid=(bsz, seq // ts),
        in_specs=[pl.BlockSpec((1, ts, 2 * LRU_DIM), lambda b, i: (b, i, SEG_C // (2 * LRU_DIM))),
                  pl.BlockSpec((1, ts, D_MODEL), lambda b, i: (b, i, SEG_G // D_MODEL + 2)),
                  pl.BlockSpec((1, ts, D_MODEL), lambda b, i: (b, i, 0)),
                  full((LRU_CONV, LRU_DIM)), full((1, LRU_DIM)),
                  full((LRU_HEADS, LRU_BLOCK, LRU_BLOCK)), full((1, LRU_DIM)),
                  full((LRU_HEADS, LRU_BLOCK, LRU_BLOCK)), full((1, LRU_DIM)),
                  full((1, LRU_DIM)), full((LRU_DIM, D_MODEL))],
        out_specs=pl.BlockSpec((1, ts, D_MODEL), lambda b, i: (b, i, 0)),
        scratch_shapes=[pltpu.VMEM((ts + SUBLANES, LRU_DIM), F32),
                        pltpu.VMEM((SUBLANES, LRU_DIM), F32),
                        pltpu.VMEM((ts, LRU_DIM), F32),
                        pltpu.VMEM((ts, LRU_DIM), F32)],
        compiler_params=_cparams(("arbitrary", "arbitrary")),
        name="rg_lru",
    )(p, p, merged, conv_w, row(conv_b), wa_bf, row(ba), wx_bf, row(bx), row(lam), proj_bf)


def _rwkv_prep_kernel(pb_ref, mu_ref, w0_ref, wup_ref, a0_ref, aup_ref, gup_ref, kk_ref, ka_ref,
                      rk_ref, bd_ref, ltri_ref,
                      rt_ref, kkt_ref, kh_ref, bh_ref, v_ref, pinc_ref, bonus_ref, g_ref, ext_ref):
    ts = pb_ref.shape[1]

    @pl.when(pl.program_id(1) == 0)
    def _():
        ext_ref[pl.ds(0, SUBLANES), :] = jnp.zeros((SUBLANES, ext_ref.shape[1]), F32)

    p = pb_ref[0].astype(F32)
    ext_ref[pl.ds(SUBLANES, ts), :] = p
    prev = ext_ref[pl.ds(SUBLANES - 1, ts), :]
    ext_ref[pl.ds(0, SUBLANES), :] = ext_ref[pl.ds(ts, SUBLANES), :]
    pm = p + (prev - p) * mu_ref[...]
    r = pm[:, RW_R:RW_R + RWKV_DIM]
    k = pm[:, RW_K:RW_K + RWKV_DIM]
    v = pm[:, RW_V:RW_V + RWKV_DIM]
    xwa = pm[:, RW_XWA:RW_XWA + LANES]
    xg = pm[:, RW_XG:RW_XG + LORA_G]
    w = -jax.nn.softplus(-(w0_ref[...] + _bdot(jnp.tanh(xwa), wup_ref[...]))) - 0.5
    lw = -jnp.exp(w)
    a = jax.nn.sigmoid(a0_ref[...] + _bdot(xwa, aup_ref[...]))
    g_ref[0] = _bdot(jax.nn.sigmoid(xg), gup_ref[...]).astype(g_ref.dtype)
    kkr = k * kk_ref[...]
    ss = _head_sums(kkr * kkr, bd_ref[...])
    kk = kkr / jnp.maximum(jnp.sqrt(ss), 1e-12)
    k2 = k * (1.0 + (a - 1.0) * ka_ref[...])
    lw_hi = lw.astype(BF16)
    lw_mid, lw_lo = _split(lw - lw_hi.astype(F32))
    tri = ltri_ref[...]
    lcum = (jnp.dot(tri, lw_hi, preferred_element_type=F32) + jnp.dot(tri, lw_mid, preferred_element_type=F32)
            + jnp.dot(tri, lw_lo, preferred_element_type=F32))
    pinc = jnp.exp(lcum)
    pinv = jnp.exp(-lcum)
    rt_ref[0] = r * pinc
    kkt_ref[0] = kk * jnp.exp(lcum - lw)
    kh_ref[0] = k2 * pinv
    bh_ref[0] = kk * a * pinv
    v_ref[0] = v
    pinc_ref[0] = pinc
    bonus_ref[0] = (_head_sums(r * k2 * rk_ref[...], bd_ref[...]) * v).astype(bonus_ref.dtype)


def _dot3(a, b):
    d = lambda x, y: jnp.dot(x, y, preferred_element_type=F32)
    m = a[0].shape[0]
    both = d(jnp.concatenate([a[0], a[1]], axis=0), b[0])
    return both[:m] + both[m:] + d(a[0], b[1])


def _rwkv_scan_kernel(rt_ref, kkt_ref, kh_ref, bh_ref, v_ref, pinc_ref, y_ref, s_ref):
    c = RWKV_CHUNK
    n = RWKV_HEAD
    nb = rt_ref.shape[0]
    heads = range(nb * RWKV_HEADS)

    @pl.when(pl.program_id(1) == 0)
    def _():
        s_ref[...] = jnp.zeros(s_ref.shape, F32)

    row = lax.broadcasted_iota(I32, (c, c), 0)
    col = lax.broadcasted_iota(I32, (c, c), 1)
    eye = (row == col).astype(F32)
    same16 = (row // 16) == (col // 16)
    same32 = (row // 32) == (col // 32)
    row2 = lax.broadcasted_iota(I32, (c, 2 * c), 0)
    col2 = lax.broadcasted_iota(I32, (c, 2 * c), 1) % c
    nt = lambda a, b: lax.dot_general(a, b, (((1,), (1,)), ((), ())), preferred_element_type=F32)
    tn = lambda a, b: lax.dot_general(a, b, (((0,), (0,)), ((), ())), preferred_element_type=F32)
    dot = lambda a, b: jnp.dot(a, b, preferred_element_type=F32)
    sl = [pl.ds((h % RWKV_HEADS) * n, n) for h in heads]
    sq = [h // RWKV_HEADS for h in heads]
    v = [v_ref[sq[h], :, sl[h]] for h in heads]
    pc = [pinc_ref[sq[h], pl.ds(c - 1, 1), sl[h]] for h in heads]
    s = [s_ref[sq[h], :, sl[h]] for h in heads]
    lhs = [jnp.concatenate([kkt_ref[sq[h], :, sl[h]], rt_ref[sq[h], :, sl[h]]], axis=0) for h in heads]
    rhs = [jnp.concatenate([bh_ref[sq[h], :, sl[h]], kh_ref[sq[h], :, sl[h]]], axis=0) for h in heads]
    big = [nt(lhs[h], rhs[h]) for h in heads]
    from_state = [nt(lhs[h], s[h]) for h in heads]
    top = [jnp.where(row2 > col2, big[h][:c], 0.0) for h in heads]
    bot = [jnp.where(row2 >= col2, big[h][c:], 0.0) for h in heads]
    a_b = [top[h][:, :c] for h in heads]
    akv = [dot(top[h], jnp.concatenate([jnp.zeros((c, n), F32), v[h]], axis=0)) for h in heads]
    d16 = [jnp.where(same16, a_b[h], 0.0) for h in heads]
    sd = [_split(d16[h]) for h in heads]
    s2 = [_split(_dot3(sd[h], sd[h])) for h in heads]
    s4 = [_split(_dot3(s2[h], s2[h])) for h in heads]
    s8 = [_split(_dot3(s4[h], s4[h])) for h in heads]
    t = [eye - d16[h] for h in heads]
    for sp in (s2, s4, s8):
        t = [t[h] + _dot3(_split(t[h]), sp[h]) for h in heads]
    for off in ([jnp.where(same32 & jnp.logical_not(same16), a_b[h], 0.0) for h in heads],
                [jnp.where(same32, 0.0, a_b[h]) for h in heads]):
        tb = [t[h].astype(BF16) for h in heads]
        lt = [dot(off[h].astype(BF16), tb[h]).astype(BF16) for h in heads]
        t = [t[h] - dot(tb[h], lt[h]) for h in heads]
    u = [dot(t[h], from_state[h][:c] + akv[h]) for h in heads]
    vu = [jnp.concatenate([-u[h], v[h]], axis=0) for h in heads]
    y = [from_state[h][c:] + dot(bot[h], vu[h]) for h in heads]
    s_new = [s[h] * pc[h] + tn(vu[h], rhs[h] * pc[h]) for h in heads]
    for q in range(nb):
        mine = slice(q * RWKV_HEADS, (q + 1) * RWKV_HEADS)
        y_ref[q] = jnp.concatenate(y[mine], axis=1)
        s_ref[q] = jnp.concatenate(s_new[mine], axis=1)


def _rwkv_post_kernel(y_ref, bonus_ref, g_ref, pg_ref, m_ref, gg_ref, gb_ref, bdm_ref, pj_ref, o_ref):
    y = y_ref[0]
    mu = _head_sums(y, bdm_ref[...])
    d = y - mu
    var = _head_sums(d * d, bdm_ref[...])
    yn = d * lax.rsqrt(var + RWKV_GN_EPS) * gg_ref[...] + gb_ref[...] + bonus_ref[0].astype(F32)
    o = _bdot(yn * g_ref[0].astype(F32), pj_ref[...])
    o_ref[0] = (m_ref[0].astype(F32) + pg_ref[0].astype(F32) * o).astype(o_ref.dtype)


def _rwkv(p, merged, mu_pad, w0, wup_pad, a0, aup_pad, g_up, k_k, k_a, r_k, gn_g, gn_b, proj_bf):
    bsz, seq, _ = p.shape
    row = lambda a: a.reshape(1, -1)
    full = lambda shape: pl.BlockSpec(shape, lambda b, i: (0,) * len(shape))
    head_id = jnp.arange(RWKV_DIM, dtype=I32) // RWKV_HEAD
    bd = (head_id[:, None] == head_id[None, :]).astype(BF16)

    ts = min(seq, 512)
    t_id = jnp.arange(ts, dtype=I32)
    ltri = ((t_id[:, None] // RWKV_CHUNK == t_id[None, :] // RWKV_CHUNK)
            & (t_id[:, None] >= t_id[None, :])).astype(BF16)
    seq_blk = lambda width: pl.BlockSpec((1, ts, width), lambda b, i: (b, i, 0))
    wide = SDS((bsz, seq, RWKV_DIM), F32)
    rt, kkt, kh, bh, v, pinc, bonus, g = pl.pallas_call(
        _rwkv_prep_kernel,
        out_shape=[wide] * 6 + [SDS((bsz, seq, RWKV_DIM), BF16)] * 2,
        grid=(bsz, seq // ts),
        in_specs=[pl.BlockSpec((1, ts, 2048), lambda b, i: (b, i, SEG_B // 2048)),
                  full((1, 2048)), full((1, RWKV_DIM)), full((LANES, RWKV_DIM)),
                  full((1, RWKV_DIM)), full((LANES, RWKV_DIM)), full((LORA_G, RWKV_DIM)),
                  full((1, RWKV_DIM)), full((1, RWKV_DIM)), full((1, RWKV_DIM)),
                  full((RWKV_DIM, RWKV_DIM)), full((ts, ts))],
        out_specs=[seq_blk(RWKV_DIM)] * 8,
        scratch_shapes=[pltpu.VMEM((ts + SUBLANES, 2048), F32)],
        compiler_params=_cparams(("arbitrary", "arbitrary")),
        name="rwkv_prep",
    )(p, mu_pad, row(w0), wup_pad, row(a0), aup_pad, g_up, row(k_k), row(k_a), row(r_k), bd, ltri)

    c = RWKV_CHUNK
    nb = next(n for n in (4, 2, 1) if bsz % n == 0)
    chunk_blk = pl.BlockSpec((nb, c, RWKV_DIM), lambda b, i: (b, i, 0))
    y = pl.pallas_call(
        _rwkv_scan_kernel,
        out_shape=wide,
        grid=(bsz // nb, seq // c),
        in_specs=[chunk_blk] * 6,
        out_specs=chunk_blk,
        scratch_shapes=[pltpu.VMEM((nb, RWKV_HEAD, RWKV_DIM), F32)],
        compiler_params=_cparams(("arbitrary", "arbitrary")),
        name="rwkv_scan",
    )(rt, kkt, kh, bh, v, pinc)

    tp = min(seq, 1024)
    blk = lambda width: pl.BlockSpec((1, tp, width), lambda b, i: (b, i, 0))
    return pl.pallas_call(
        _rwkv_post_kernel,
        out_shape=SDS((bsz, seq, D_MODEL), MERGED_DTYPE),
        grid=(bsz, seq // tp),
        in_specs=[blk(RWKV_DIM), blk(RWKV_DIM), blk(RWKV_DIM),
                  pl.BlockSpec((1, tp, D_MODEL), lambda b, i: (b, i, SEG_G // D_MODEL + 1)),
                  blk(D_MODEL), full((1, RWKV_DIM)), full((1, RWKV_DIM)),
                  full((RWKV_DIM, RWKV_DIM)), full((RWKV_DIM, D_MODEL))],
        out_specs=blk(D_MODEL),
        compiler_params=_cparams(("arbitrary", "arbitrary")),
        name="rwkv_post",
    )(y, bonus, g, p, merged, row(gn_g), row(gn_b), bd * (1.0 / RWKV_HEAD), proj_bf)


def _out_kernel(m_ref, x_ref, g1_ref, w_ref, n2_ref, sc_ref, sh_ref, xo_ref, h_ref):
    xn = x_ref[0] + g1_ref[0] * _bdot(m_ref[0], w_ref[...])
    xo_ref[0] = xn
    h_ref[0] = _modulated_rmsnorm(xn, n2_ref[...], sc_ref[0], sh_ref[0])


def _out_proj(merged, x, g1, w_bf, norm2, sc2, sh2):
    bsz, seq, _ = x.shape
    ts = min(seq, 1024)
    blk = pl.BlockSpec((1, ts, D_MODEL), lambda b, i: (b, i, 0))
    per_b = pl.BlockSpec((1, 1, D_MODEL), lambda b, i: (b, 0, 0))
    return pl.pallas_call(
        _out_kernel,
        out_shape=[SDS((bsz, seq, D_MODEL), F32)] * 2,
        grid=(bsz, seq // ts),
        in_specs=[blk, blk, per_b, pl.BlockSpec((D_MODEL, D_MODEL), lambda b, i: (0, 0)),
                  pl.BlockSpec((1, D_MODEL), lambda b, i: (0, 0)), per_b, per_b],
        out_specs=[blk, blk],
        compiler_params=_cparams(("arbitrary", "arbitrary")),
        name="out_proj",
    )(merged, x, g1, w_bf, norm2, sc2, sh2)


MOE_TILE = 256
MOE_SLOTS = 2560
RUN_LOOP_UNROLL = 4
DISPATCH_PIECE = 48
DISPATCH_SPARE = DISPATCH_PIECE - SUBLANES
DISPATCH_WAIT_ROWS = 2048
assert MOE_SLOTS >= TOP_K * MOE_TILE + N_EXPERTS * (SUBLANES - 1) + DISPATCH_SPARE
U32 = jnp.uint32


ROUTE_TILES = 2


def _route_kernel(h_ref, rw_ref, bias_ref, upper_ref, ltri_ref, slot_ref, w_ref, n8_ref):
    tile = MOE_TILE
    subs = range(h_ref.shape[0] // tile)
    neg = -jnp.inf
    shape3 = (GROUP_SIZE, N_GROUPS, tile)
    to3 = lambda a: a.reshape(shape3)
    lanes_of = lambda a: jnp.concatenate([a] * (tile // LANES), axis=1)
    fold = lambda a: jnp.sum(jnp.sum(a, axis=0), axis=0, keepdims=True)
    slab = lax.broadcasted_iota(I32, shape3, 0).astype(F32)
    grp = lax.broadcasted_iota(I32, shape3, 1).astype(F32)
    eid = grp * GROUP_SIZE + slab
    gi = lax.broadcasted_iota(I32, (N_GROUPS, tile), 0).astype(F32)
    ones_cols = jnp.ones((tile, LANES), BF16)
    s3 = [to3(jax.nn.sigmoid(_hdot_nt(rw_ref[...], h_ref[pl.ds(q * tile, tile), :]))) for q in subs]
    b3 = [s3[q] + to3(bias_ref[...]) for q in subs]
    m1 = [jnp.max(b3[q], axis=0, keepdims=True) for q in subs]
    first = [jnp.min(jnp.where(b3[q] == m1[q], slab, GROUP_SIZE), axis=0, keepdims=True) for q in subs]
    m2 = [jnp.max(jnp.where(slab == first[q], neg, b3[q]), axis=0, keepdims=True) for q in subs]
    gs = [(m1[q] + m2[q])[0] for q in subs]
    chosen = [jnp.zeros((N_GROUPS, tile), F32) for q in subs]
    for _ in range(TOPK_GROUPS):
        m = [jnp.max(gs[q], axis=0, keepdims=True) for q in subs]
        hit = [gi == jnp.min(jnp.where(gs[q] == m[q], gi, N_GROUPS), axis=0, keepdims=True) for q in subs]
        chosen = [jnp.where(hit[q], 1.0, chosen[q]) for q in subs]
        gs = [jnp.where(hit[q], neg, gs[q]) for q in subs]
    cur = [jnp.where((chosen[q] > 0.0)[None], b3[q], neg) for q in subs]
    base = [jnp.zeros((N_EXPERTS, LANES), F32) for q in subs]
    picks, w_rows, rank_rows = [[] for q in subs], [[] for q in subs], [[] for q in subs]
    for _ in range(TOP_K):
        m = [jnp.max(jnp.max(cur[q], axis=0), axis=0, keepdims=True)[None] for q in subs]
        pick = [jnp.min(jnp.min(jnp.where(cur[q] == m[q], eid, N_EXPERTS), axis=0), axis=0, keepdims=True)
                for q in subs]
        hit = [eid == pick[q][None] for q in subs]
        onehot = [hit[q].astype(BF16).reshape(N_EXPERTS, tile) for q in subs]
        before = [jnp.dot(onehot[q], upper_ref[...], preferred_element_type=F32) for q in subs]
        count = [jnp.dot(onehot[q], ones_cols, preferred_element_type=F32) for q in subs]
        for q in subs:
            w_rows[q].append(fold(jnp.where(hit[q], s3[q], 0.0)))
            picks[q].append(pick[q])
            rank_rows[q].append(fold(jnp.where(hit[q], to3(before[q] + lanes_of(base[q])), 0.0)))
        cur = [jnp.where(hit[q], neg, cur[q]) for q in subs]
        base = [base[q] + count[q] for q in subs]
    n8 = [jnp.floor((base[q] + (SUBLANES - 1.0)) * (1.0 / SUBLANES)) * SUBLANES for q in subs]
    run_start = [to3(lanes_of(_hdot(ltri_ref[...], n8[q]))) for q in subs]
    slots, weights = [], []
    for q in subs:
        slots.append(jnp.concatenate(
            [rank_rows[q][j] + fold(jnp.where(eid == picks[q][j][None], run_start[q], 0.0)) for j in range(TOP_K)],
            axis=0))
        w_all = jnp.concatenate(w_rows[q], axis=0)
        weights.append(w_all / jnp.sum(w_all, axis=0, keepdims=True) * ROUTED_SCALE)
    for q in subs:
        n8_ref[q] = n8[q]
    w_ref[...] = jnp.concatenate(weights, axis=1)
    slot_ref[...] = jnp.concatenate(slots, axis=1).astype(I32)


def _route(h2, router_w, router_bias):
    n_tok = h2.shape[0]
    tile = MOE_TILE
    n_tiles = n_tok // tile
    per_step = ROUTE_TILES if n_tiles % ROUTE_TILES == 0 else 1
    regroup = lambda a: a.reshape(N_GROUPS, GROUP_SIZE, -1).transpose(1, 0, 2).reshape(N_EXPERTS, -1)
    rw = regroup(router_w.T)
    bias = jnp.broadcast_to(regroup(router_bias.reshape(N_EXPERTS, 1)), (N_EXPERTS, tile))
    t_id = jnp.arange(tile, dtype=I32)
    upper = (t_id[:, None] < t_id[None, :]).astype(BF16)
    e_id = jnp.arange(N_EXPERTS, dtype=I32)
    ltri = (e_id[:, None] > e_id[None, :]).astype(F32)
    tok_blk = pl.BlockSpec((TOP_K, per_step * tile), lambda i: (0, i))
    full = lambda shape: pl.BlockSpec(shape, lambda i: (0,) * len(shape))
    slot_t, w_t, n8 = pl.pallas_call(
        _route_kernel,
        out_shape=[SDS((TOP_K, n_tok), I32), SDS((TOP_K, n_tok), F32),
                   SDS((n_tiles, N_EXPERTS, LANES), F32)],
        grid=(n_tiles // per_step,),
        in_specs=[pl.BlockSpec((per_step * tile, D_MODEL), lambda i: (i, 0)),
                  full((N_EXPERTS, D_MODEL)), full((N_EXPERTS, tile)), full((tile, tile)),
                  full((N_EXPERTS, N_EXPERTS))],
        out_specs=[tok_blk, tok_blk, pl.BlockSpec((per_step, N_EXPERTS, LANES), lambda i: (i, 0, 0))],
        compiler_params=_cparams(("arbitrary",)),
        name="moe_route",
    )(h2, rw, bias, upper, ltri)
    return slot_t, w_t, n8[:, :, 0].astype(I32)


def _for_each_run_piece(tile_idx, n8_ref, fn):
    for r in range(N_EXPERTS):
        n = n8_ref[tile_idx * N_EXPERTS + r]
        size = MOE_TILE
        while size >= SUBLANES:
            @pl.when((n & size) != 0)
            def _(size=size):
                fn(r, n & ~(2 * size - 1), size)
            size //= 2


def _wait_rows(total, make_wait):
    size = SUBLANES
    while size <= MOE_SLOTS:
        @pl.when((total & size) != 0)
        def _(size=size):
            make_wait(size).wait()
        size *= 2


def _pack_pairs(hi_bits, lo_bits):
    return (hi_bits & jnp.uint32(0xFFFF0000)) | (lo_bits >> 16)


def _unpack_pairs(u):
    hi = lax.bitcast_convert_type(u & jnp.uint32(0xFFFF0000), F32)
    lo = lax.bitcast_convert_type(u << 16, F32)
    return jnp.concatenate([hi, lo], axis=1).astype(BF16)


def _dispatch_kernel(n8_ref, off_ref, dst_ref, tot_ref, pad_end_ref, slot_ref, h_ref, xs_ref,
                     g_ref, zero_ref, sem):
    i = pl.program_id(0)
    tile = h_ref.shape[0]
    half = D_MODEL // 2

    @pl.when(i == 0)
    def _():
        zero_ref[...] = jnp.zeros(zero_ref.shape, U32)

        def last_block(e, back):
            start = pl.multiple_of(jnp.maximum(pad_end_ref[e] - back * MOE_ROWS, 0), MOE_ROWS)
            return pltpu.make_async_copy(zero_ref, xs_ref.at[pl.ds(start, MOE_ROWS), :], sem.at[0])

        def has_blocks(e, back):
            prev = jnp.where(e > 0, pad_end_ref[jnp.maximum(e - 1, 0)], 0)
            return pad_end_ref[e] - prev >= back * MOE_ROWS

        def clear(e, carry):
            for back in (1, 2):
                pl.when(has_blocks(e, back))(lambda back=back: last_block(e, back).start())
            return carry
        lax.fori_loop(0, N_EXPERTS, clear, 0)

        def done(e, carry):
            for back in (1, 2):
                pl.when(has_blocks(e, back))(lambda back=back: last_block(e, back).wait())
            return carry
        lax.fori_loop(0, N_EXPERTS, done, 0)

        def tail_block(b):
            start = pl.multiple_of(b * MOE_ROWS, MOE_ROWS)
            return pltpu.make_async_copy(zero_ref, xs_ref.at[pl.ds(start, MOE_ROWS), :], sem.at[0])

        def clear_tail(b, carry):
            tail_block(b).start()
            return carry

        def done_tail(b, carry):
            tail_block(b).wait()
            return carry
        used = pad_end_ref[N_EXPERTS - 1] // MOE_ROWS
        lax.fori_loop(used, xs_ref.shape[0] // MOE_ROWS, clear_tail, 0)
        lax.fori_loop(used, xs_ref.shape[0] // MOE_ROWS, done_tail, 0)

    slot_id = lax.broadcasted_iota(I16, (MOE_SLOTS, tile), 0)
    slots = slot_ref[...].astype(I16)
    sel = jnp.zeros((MOE_SLOTS, tile), BF16)
    for j in range(TOP_K):
        sel = jnp.where(slot_id == slots[j:j + 1, :], jnp.ones((), BF16), sel)
    g = jnp.dot(sel, h_ref[...].astype(BF16), preferred_element_type=F32)
    bits = lax.bitcast_convert_type(g, U32)
    buf = i % 2
    g_ref[buf] = _pack_pairs(bits[:, :half], bits[:, half:])

    def drain(step):
        total = tot_ref[step]
        whole = pltpu.make_async_copy(g_ref.at[step % 2, pl.ds(0, DISPATCH_WAIT_ROWS), :],
                                      xs_ref.at[pl.ds(0, DISPATCH_WAIT_ROWS), :], sem.at[step % 2])

        def wait_whole(k, carry):
            whole.wait()
            return carry
        lax.fori_loop(0, lax.shift_right_logical(total, DISPATCH_WAIT_ROWS.bit_length() - 1), wait_whole, 0)
        _wait_rows(total & (DISPATCH_WAIT_ROWS - 1), lambda rows: pltpu.make_async_copy(
            g_ref.at[step % 2, pl.ds(0, rows), :], xs_ref.at[pl.ds(0, rows), :], sem.at[step % 2]))

    pl.when(i > 0)(lambda: drain(i - 1))

    def copy(r, k):
        src = pl.multiple_of(off_ref[i * N_EXPERTS + r] + k * DISPATCH_PIECE, SUBLANES)
        dst = pl.multiple_of(dst_ref[i * N_EXPERTS + r] + k * DISPATCH_PIECE, SUBLANES)
        pltpu.make_async_copy(g_ref.at[buf, pl.ds(src, DISPATCH_PIECE), :],
                              xs_ref.at[pl.ds(dst, DISPATCH_PIECE), :], sem.at[buf]).start()

    def per_expert(r, carry):
        n = n8_ref[i * N_EXPERTS + r]
        pl.when(n > 0)(lambda: copy(r, 0))

        @pl.when(n > DISPATCH_PIECE)
        def _():
            def more(k, c):
                copy(r, k)
                return c
            lax.fori_loop(1, (n + (DISPATCH_PIECE - 1)) // DISPATCH_PIECE, more, 0)
        return carry
    lax.fori_loop(0, N_EXPERTS, per_expert, 0, unroll=RUN_LOOP_UNROLL)
    pl.when(i == pl.num_programs(0) - 1)(lambda: drain(i))


def _dispatch(h2, slot_t, n8_flat, off_flat, dst_flat, tot, pad_end, n_rows):
    n_tok = h2.shape[0]
    tile = MOE_TILE
    grid_spec = pltpu.PrefetchScalarGridSpec(
        num_scalar_prefetch=5,
        grid=(n_tok // tile,),
        in_specs=[pl.BlockSpec((TOP_K, tile), lambda i, *_: (0, i)),
                  pl.BlockSpec((tile, D_MODEL), lambda i, *_: (i, 0))],
        out_specs=pl.BlockSpec(memory_space=pl.ANY),
        scratch_shapes=[pltpu.VMEM((2, MOE_SLOTS, D_MODEL // 2), U32),
                        pltpu.VMEM((MOE_ROWS, D_MODEL // 2), U32), pltpu.SemaphoreType.DMA((2,))],
    )
    return pl.pallas_call(
        _dispatch_kernel,
        out_shape=SDS((n_rows, D_MODEL // 2), U32),
        grid_spec=grid_spec,
        compiler_params=_cparams(("arbitrary",)),
        name="moe_dispatch",
    )(n8_flat, off_flat, dst_flat, tot, pad_end, slot_t, h2)


def _expert_kernel(first_ref, nblk_ref, exp_of_ref, xs_ref, w1_ref, w3_ref, w2_ref, y_ref,
                   xbuf, ybuf, wb1, wb3, wb2, in_sem, out_sem):
    del exp_of_ref
    r = pl.program_id(0)
    half = D_MODEL // 2
    used = first_ref[N_EXPERTS - 1] + nblk_ref[N_EXPERTS - 1]
    part = MOE_ROWS // EXPERT_DMA_PARTS

    class _Copies:
        def __init__(self, make):
            self.parts = [make(p) for p in range(EXPERT_DMA_PARTS)]

        def start(self):
            for c in self.parts:
                c.start()

        def wait(self):
            for c in self.parts:
                c.wait()

    def hbm_rows(g, p):
        return pl.ds(pl.multiple_of(g * MOE_ROWS + p * part, part), part)

    in_copy = lambda g, slot: _Copies(lambda p: pltpu.make_async_copy(
        xs_ref.at[hbm_rows(g, p), :], xbuf.at[slot, pl.ds(p * part, part), :], in_sem.at[slot]))
    out_copy = lambda g, slot: _Copies(lambda p: pltpu.make_async_copy(
        ybuf.at[slot, pl.ds(p * part, part), :], y_ref.at[hbm_rows(g, p), :], out_sem.at[slot]))

    @pl.when(r == 0)
    def _():
        for g in range(EXPERT_AHEAD):
            pl.when(g < used)(lambda g=g: in_copy(g, g).start())

    wb1[...] = w1_ref[0, 0].astype(BF16)
    wb3[...] = w3_ref[0, 0].astype(BF16)
    wb2[...] = w2_ref[0, 0].astype(BF16)

    def process(blocks):
        dot = lambda a, b: jnp.dot(a, b, preferred_element_type=F32)
        slots = [g % EXPERT_BUFS for g in blocks]
        for g, slot in zip(blocks, slots):
            in_copy(g, slot).wait()
            ahead = g + EXPERT_AHEAD
            pl.when(ahead < used)(lambda ahead=ahead: in_copy(ahead, ahead % EXPERT_BUFS).start())
            pl.when(g >= EXPERT_BUFS)(lambda g=g, slot=slot: out_copy(g - EXPERT_BUFS, slot).wait())
        x = [_unpack_pairs(xbuf[slot]) for slot in slots]
        gate = [dot(xq, wb1[...]) for xq in x]
        up = [dot(xq, wb3[...]) for xq in x]
        hid = [(jax.nn.silu(a) * b).astype(BF16) for a, b in zip(gate, up)]
        for g, slot, hq in zip(blocks, slots, hid):
            bits = lax.bitcast_convert_type(dot(hq, wb2[...]).astype(BF16).astype(F32), U32)
            ybuf[slot] = _pack_pairs(bits[:, :half], bits[:, half:])
            out_copy(g, slot).start()

    def pair(k, carry):
        g = first_ref[r] + 2 * k
        process([g, g + 1])
        return carry
    n_mine = nblk_ref[r]
    lax.fori_loop(0, lax.shift_right_logical(n_mine, 1), pair, 0)
    pl.when((n_mine & 1) == 1)(lambda: process([first_ref[r] + n_mine - 1]))

    @pl.when(r == N_EXPERTS - 1)
    def _():
        for back in range(EXPERT_BUFS, 0, -1):
            pl.when(used >= back)(
                lambda back=back: out_copy(used - back, (used - back) % EXPERT_BUFS).wait())

        ybuf[0] = jnp.zeros(ybuf.shape[1:], U32)
        n_blocks = y_ref.shape[0] // MOE_ROWS

        def clear(g, carry):
            out_copy(g, 0).start()
            return carry

        def done(g, carry):
            out_copy(g, 0).wait()
            return carry
        lax.fori_loop(used, n_blocks, clear, 0)
        lax.fori_loop(used, n_blocks, done, 0)


def _experts(xs, first_block, n_block, exp_of_row, w1, w3, w2, layer):
    n_rows = xs.shape[0]
    half = D_MODEL // 2
    w_in_blk = pl.BlockSpec((1, 1, D_MODEL, EXPERT_FF), lambda r, first, nblk, eo: (layer, eo[r], 0, 0))
    grid_spec = pltpu.PrefetchScalarGridSpec(
        num_scalar_prefetch=3,
        grid=(N_EXPERTS,),
        in_specs=[pl.BlockSpec(memory_space=pl.ANY), w_in_blk, w_in_blk,
                  pl.BlockSpec((1, 1, EXPERT_FF, D_MODEL), lambda r, first, nblk, eo: (layer, eo[r], 0, 0))],
        out_specs=pl.BlockSpec(memory_space=pl.ANY),
        scratch_shapes=[pltpu.VMEM((EXPERT_BUFS, MOE_ROWS, half), U32),
                        pltpu.VMEM((EXPERT_BUFS, MOE_ROWS, half), U32),
                        pltpu.VMEM((D_MODEL, EXPERT_FF), BF16), pltpu.VMEM((D_MODEL, EXPERT_FF), BF16),
                        pltpu.VMEM((EXPERT_FF, D_MODEL), BF16),
                        pltpu.SemaphoreType.DMA((EXPERT_BUFS,)), pltpu.SemaphoreType.DMA((EXPERT_BUFS,))],
    )
    return pl.pallas_call(
        _expert_kernel,
        out_shape=SDS((n_rows, half), U32),
        grid_spec=grid_spec,
        compiler_params=_cparams(("arbitrary",)),
        name="moe_experts",
    )(first_block, n_block, exp_of_row, xs, w1, w3, w2)


def _combine_kernel(n8_ref, off_ref, src_ref, tot_ref, y_ref, slot_ref, w_ref, h_ref, x_ref, g2_ref,
                    s1_ref, s3_ref, s2_ref, fn_ref, o_ref, yt_ref, sem, *, final):
    tile = h_ref.shape[1]
    i = pl.program_id(0) * pl.num_programs(1) + pl.program_id(1)

    @pl.when(i == 0)
    def _():
        yt_ref[...] = jnp.zeros(yt_ref.shape, U32)

    def piece(r, offset, rows):
        src = pl.multiple_of(src_ref[i * N_EXPERTS + r] + offset, SUBLANES)
        dst = pl.multiple_of(off_ref[i * N_EXPERTS + r] + offset, SUBLANES)
        pltpu.make_async_copy(y_ref.at[pl.ds(src, rows), :], yt_ref.at[pl.ds(dst, rows), :], sem).start()

    _for_each_run_piece(i, n8_ref, piece)
    h = h_ref[0]
    shared = _bdot(jax.nn.silu(_bdot(h, s1_ref[...])) * _bdot(h, s3_ref[...]), s2_ref[...])
    slot_id = lax.broadcasted_iota(I16, (tile, MOE_SLOTS), 1)
    slots = slot_ref[0].astype(I16)
    w = w_ref[0].astype(BF16)
    pw = jnp.zeros((tile, MOE_SLOTS), BF16)
    for j in range(TOP_K):
        pw = jnp.where(slot_id == slots[:, j:j + 1], w[:, j:j + 1], pw)
    _wait_rows(tot_ref[i], lambda rows: pltpu.make_async_copy(
        y_ref.at[pl.ds(0, rows), :], yt_ref.at[pl.ds(0, rows), :], sem))
    routed = jnp.dot(pw, _unpack_pairs(yt_ref[...]), preferred_element_type=F32)
    xn = x_ref[0] + g2_ref[0] * (routed + shared)
    if final:
        xn = xn * lax.rsqrt(jnp.mean(xn * xn, axis=-1, keepdims=True) + NORM_EPS) * fn_ref[...]
    o_ref[0] = xn


def _combine(y, slot_nat, w_nat, n8_flat, off_flat, dst_flat, tot, h2, x, g2, s1_bf, s3_bf, s2_bf,
             final_norm, final):
    bsz, seq, _ = x.shape
    tile = MOE_TILE
    per_seq = seq // tile
    blk = pl.BlockSpec((1, tile, D_MODEL), lambda b, i, *_: (b, i, 0))
    tok = pl.BlockSpec((1, tile, TOP_K), lambda b, i, *_: (b, i, 0))
    full = lambda shape: pl.BlockSpec(shape, lambda b, i, *_: (0,) * len(shape))
    grid_spec = pltpu.PrefetchScalarGridSpec(
        num_scalar_prefetch=4,
        grid=(bsz, per_seq),
        in_specs=[pl.BlockSpec(memory_space=pl.ANY), tok, tok, blk, blk,
                  pl.BlockSpec((1, 1, D_MODEL), lambda b, i, *_: (b, 0, 0)),
                  full((D_MODEL, SHARED_FF)), full((D_MODEL, SHARED_FF)), full((SHARED_FF, D_MODEL)),
                  full((1, D_MODEL))],
        out_specs=blk,
        scratch_shapes=[pltpu.VMEM((MOE_SLOTS, D_MODEL // 2), U32), pltpu.SemaphoreType.DMA],
    )
    return pl.pallas_call(
        functools.partial(_combine_kernel, final=final),
        out_shape=SDS((bsz, seq, D_MODEL), F32),
        grid_spec=grid_spec,
        compiler_params=_cparams(("arbitrary", "arbitrary")),
        name="moe_combine",
    )(n8_flat, off_flat, dst_flat, tot, y, slot_nat, w_nat, h2, x, g2, s1_bf, s3_bf, s2_bf, final_norm)


def _moe(x, h2, g2, router_w, router_bias, w1, w3, w2, layer, s1_bf, s3_bf, s2_bf, final_norm, final):
    bsz, seq, _ = x.shape
    assert seq % MOE_TILE == 0, "token tiles must not straddle sequences"
    n_tok = bsz * seq
    slot_t, w_t, n8 = _route(h2.reshape(n_tok, D_MODEL), router_w, router_bias)
    n_tiles = n8.shape[0]
    counts = jnp.sum(n8, axis=0)
    padded = jnp.where(counts > 0, (counts + DISPATCH_SPARE + MOE_ROWS - 1) // MOE_ROWS * MOE_ROWS, 0)
    pad_end = jnp.cumsum(padded).astype(I32)
    pad_start = pad_end - padded
    run_row = (pad_start[None, :] + jnp.cumsum(n8, axis=0) - n8).astype(I32)
    run_slot = (jnp.cumsum(n8, axis=1) - n8).astype(I32)
    tot = jnp.sum(n8, axis=1).astype(I32)
    copied = jnp.sum((n8 + DISPATCH_PIECE - 1) // DISPATCH_PIECE * DISPATCH_PIECE, axis=1).astype(I32)
    max_rows = (n_tok * TOP_K + n_tiles * N_EXPERTS * (SUBLANES - 1)
                + N_EXPERTS * (DISPATCH_SPARE + MOE_ROWS - 1))
    n_blocks = (max_rows + MOE_ROWS - 1) // MOE_ROWS
    row_id = jnp.arange(N_EXPERTS, dtype=I32)
    exp_of_row = (row_id % N_GROUPS) * GROUP_SIZE + row_id // N_GROUPS
    flat = lambda a: a.reshape(-1).astype(I32)
    xs = _dispatch(h2.reshape(n_tok, D_MODEL), slot_t, flat(n8), flat(run_slot), flat(run_row), copied,
                   pad_end, n_blocks * MOE_ROWS)
    y = _experts(xs, (pad_start // MOE_ROWS).astype(I32), (padded // MOE_ROWS).astype(I32), exp_of_row,
                 w1, w3, w2, layer)
    nat = lambda a: a.T.reshape(bsz, seq, TOP_K)
    return _combine(y, nat(slot_t), nat(w_t), flat(n8), flat(run_slot), flat(run_row), tot, h2, x, g2,
                    s1_bf, s3_bf, s2_bf, final_norm, final)


def _pad_cols(a, width):
    return jnp.pad(a, ((0, 0), (0, width - a.shape[1])))


def _layout_w_in(w_in):
    b0 = COLS_A
    c0 = COLS_A + COLS_B
    g0 = c0 + COLS_C
    seg_b = _pad_cols(w_in[:, b0:c0], SEG_G - SEG_B)
    return jnp.concatenate([w_in[:, c0:g0], seg_b, w_in[:, g0:], w_in[:, :b0]], axis=1).astype(BF16)


def _layout_mu(mu):
    return _pad_cols(mu.reshape(1, -1), SEG_G - SEG_B)


def _pad_rows(a, height):
    return jnp.pad(a, ((0, height - a.shape[0]), (0, 0)))


def kernel(x, c, ada_w, ada_b, norm1, norm2, w_in, conv_a_w, conv_a_b, ln_a_g, ln_a_b, proj_a, mu_b, w0, w_up, a0, a_up, g_up, k_k, k_a, r_k, gn_b_g, gn_b_b, proj_b, conv_c_w, conv_c_b, lru_wa, lru_ba, lru_wx, lru_bx, lru_lambda, proj_c, w_out, router_w, router_bias, exp_w1, exp_w3, exp_w2, sh_w1, sh_w3, sh_w2, final_norm):
    depth = ada_w.shape[0]
    bsz = x.shape[0]
    mod = _ada_mod(c, ada_w, ada_b)
    for l in range(depth):
        sh1, sc1, g1, sh2, sc2, g2 = [mod[l, :, i * D_MODEL:(i + 1) * D_MODEL].reshape(bsz, 1, D_MODEL)
                                      for i in range(N_MOD)]
        p = _in_proj(x, sc1, sh1, norm1[l].reshape(1, -1), _layout_w_in(w_in[l]))
        merged = _conv_a(p, conv_a_w[l], conv_a_b[l], ln_a_g[l], ln_a_b[l], proj_a[l].astype(BF16))
        merged = _rwkv(p, merged, _layout_mu(mu_b[l]), w0[l], _pad_rows(w_up[l], LANES).astype(BF16),
                       a0[l], jnp.pad(a_up[l], ((LORA_W, 0), (0, 0))).astype(BF16), g_up[l].astype(BF16),
                       k_k[l], k_a[l], r_k[l], gn_b_g[l], gn_b_b[l], proj_b[l].astype(BF16))
        merged = _lru(p, merged, conv_c_w[l], conv_c_b[l], lru_wa[l].astype(BF16), lru_ba[l],
                      lru_wx[l].astype(BF16), lru_bx[l], lru_lambda[l], proj_c[l].astype(BF16))
        x, h2 = _out_proj(merged, x, g1, w_out[l].astype(BF16), norm2[l].reshape(1, -1), sc2, sh2)
        x = _moe(x, h2, g2, router_w[l], router_bias[l], exp_w1, exp_w3, exp_w2, l,
                 sh_w1[l].astype(BF16), sh_w3[l].astype(BF16), sh_w2[l].astype(BF16),
                 final_norm.reshape(1, -1), final=(l == depth - 1))
    return x
```

```python
import functools

import jax
import jax.numpy as jnp
from jax import lax
from jax.experimental import pallas as pl
from jax.experimental.pallas import tpu as pltpu

F32 = jnp.float32
BF16 = jnp.bfloat16
I32 = jnp.int32
I16 = jnp.int16
SDS = jax.ShapeDtypeStruct
HIGHEST = lax.Precision.HIGHEST

D_MODEL = 1024
N_MOD = 6
NORM_EPS = 1e-6
CONV_A_CH = 512
CONV_A_WIDTH = 31
CONV_A_LN_EPS = 1e-5
RWKV_HEADS = 8
RWKV_HEAD = 64
RWKV_DIM = RWKV_HEADS * RWKV_HEAD
LORA_W = 64
LORA_A = 64
LORA_G = 128
RWKV_GN_EPS = 64e-5
RWKV_CHUNK = 64
LRU_DIM = 1024
LRU_HEADS = 8
LRU_BLOCK = LRU_DIM // LRU_HEADS
LRU_CONV = 4
LRU_C = 8.0
N_EXPERTS = 64
TOP_K = 8
N_GROUPS = 8
GROUP_SIZE = N_EXPERTS // N_GROUPS
TOPK_GROUPS = 4
EXPERT_FF = 256
SHARED_FF = 256
ROUTED_SCALE = 2.5
SEG_C = 0
SEG_B = 2048
SEG_G = 4096
SEG_A = 7168
IN_COLS_PAD = 8192
RW_R, RW_K, RW_V, RW_XWA, RW_XG = 0, 512, 1024, 1536, 1664
COLS_A = 2 * CONV_A_CH
COLS_B = 3 * RWKV_DIM + LORA_W + LORA_A + LORA_G
COLS_C = 2 * LRU_DIM
VMEM_LIMIT = 56 * 1024 * 1024
MERGED_DTYPE = BF16
SUBLANES = 8
LANES = 128
MOE_ROWS = 512
EXPERT_DMA_PARTS = 4
EXPERT_AHEAD = 4
EXPERT_BUFS = EXPERT_AHEAD + 2


def _cparams(sem):
    return pltpu.CompilerParams(dimension_semantics=sem, vmem_limit_bytes=VMEM_LIMIT)


def _bdot(a, b):
    return jnp.dot(a.astype(BF16), b.astype(BF16), preferred_element_type=F32)


def _hdot(a, b):
    return jnp.dot(a, b, preferred_element_type=F32, precision=HIGHEST)


def _split(a):
    hi = a.astype(BF16)
    return hi, (a - hi.astype(F32)).astype(BF16)


def _head_sums(a, ones_bf):
    hi, lo = _split(a)
    return (jnp.dot(hi, ones_bf, preferred_element_type=F32)
            + jnp.dot(lo, ones_bf, preferred_element_type=F32))


def _hdot_nt(a, b):
    return lax.dot_general(a, b, (((1,), (1,)), ((), ())), preferred_element_type=F32,
                           precision=HIGHEST)


def _ada_kernel(c_ref, w_ref, b_ref, o_ref):
    cond = jax.nn.silu(c_ref[...])
    o_ref[0] = _bdot(cond, w_ref[0]) + b_ref[0]


def _ada_mod(c, ada_w, ada_b):
    depth, _, n = ada_w.shape
    bsz = c.shape[0]
    tn = 1536
    return pl.pallas_call(
        _ada_kernel,
        out_shape=SDS((depth, bsz, n), F32),
        grid=(depth, n // tn),
        in_specs=[pl.BlockSpec((bsz, D_MODEL), lambda l, j: (0, 0)),
                  pl.BlockSpec((1, D_MODEL, tn), lambda l, j: (l, 0, j)),
                  pl.BlockSpec((1, 1, tn), lambda l, j: (l, 0, j))],
        out_specs=pl.BlockSpec((1, bsz, tn), lambda l, j: (l, 0, j)),
        compiler_params=_cparams(("arbitrary", "arbitrary")),
        name="ada_mod",
    )(c, ada_w, ada_b.reshape(depth, 1, n))


def _modulated_rmsnorm(x, g, sc, sh):
    y = x * lax.rsqrt(jnp.mean(x * x, axis=-1, keepdims=True) + NORM_EPS)
    return (y * g) * (1.0 + sc) + sh


IN_PROJ_SUB = 512


def _in_kernel(x_ref, sc_ref, sh_ref, g_ref, w_ref, o_ref, h_ref):
    @pl.when(pl.program_id(2) == 0)
    def _():
        h_ref[...] = _modulated_rmsnorm(x_ref[0], g_ref[...], sc_ref[0], sh_ref[0]).astype(BF16)

    j = pl.program_id(2)
    ts, tn = o_ref.shape[1], o_ref.shape[2]
    is_gelu = j == SEG_C // tn
    is_gate = (j >= SEG_G // tn) & (j < SEG_A // tn)
    is_glu = j == SEG_A // tn

    def glu(acc):
        u = acc[:, :tn // 2] * jax.nn.sigmoid(acc[:, tn // 2:])
        return jnp.concatenate([u, jnp.zeros_like(u)], axis=1)

    def emit(act, sub):
        for r0 in range(0, ts, sub):
            rows = pl.ds(r0, sub)
            acc = jnp.dot(h_ref[rows, :], w_ref[...], preferred_element_type=F32)
            o_ref[0, rows, :] = act(acc).astype(o_ref.dtype)

    sub = min(ts, IN_PROJ_SUB)
    pl.when(is_gelu)(lambda: emit(functools.partial(jax.nn.gelu, approximate=True), sub))
    pl.when(is_gate)(lambda: emit(jax.nn.sigmoid, sub))
    pl.when(is_glu)(lambda: emit(glu, sub))
    pl.when(jnp.logical_not(is_gelu | is_gate | is_glu))(lambda: emit(lambda acc: acc, ts))


def _in_proj(x, sc, sh, g, w_pad):
    bsz, seq, _ = x.shape
    ts = min(seq, 2048)
    tn = 1024
    return pl.pallas_call(
        _in_kernel,
        out_shape=SDS((bsz, seq, IN_COLS_PAD), BF16),
        grid=(bsz, seq // ts, IN_COLS_PAD // tn),
        in_specs=[pl.BlockSpec((1, ts, D_MODEL), lambda b, i, j: (b, i, 0)),
                  pl.BlockSpec((1, 1, D_MODEL), lambda b, i, j: (b, 0, 0)),
                  pl.BlockSpec((1, 1, D_MODEL), lambda b, i, j: (b, 0, 0)),
                  pl.BlockSpec((1, D_MODEL), lambda b, i, j: (0, 0)),
                  pl.BlockSpec((D_MODEL, tn), lambda b, i, j: (0, j))],
        out_specs=pl.BlockSpec((1, ts, tn), lambda b, i, j: (b, i, j)),
        scratch_shapes=[pltpu.VMEM((ts, D_MODEL), BF16)],
        compiler_params=_cparams(("arbitrary", "arbitrary", "arbitrary")),
        name="in_proj",
    )(x, sc, sh, g, w_pad)


CONV_A_HALO = 32
CONV_A_SUB = 64


def _conv_a_kernel(pa_ref, pg_ref, cw_ref, cb_ref, lg_ref, lb_ref, pj_ref, o_ref, ext_ref, sh_ref, y_ref):
    ts = pa_ref.shape[1]

    @pl.when(pl.program_id(1) == 0)
    def _():
        ext_ref[pl.ds(0, CONV_A_HALO), :] = jnp.zeros((CONV_A_HALO, CONV_A_CH), F32)

    ext_ref[pl.ds(CONV_A_HALO, ts), :] = pa_ref[0].astype(F32)
    for p in range(1, SUBLANES):
        sh_ref[p - 1] = ext_ref[pl.ds(p, sh_ref.shape[1]), :]
    first = CONV_A_HALO - (CONV_A_WIDTH - 1)
    for r0 in range(0, ts, CONV_A_SUB):
        acc = jnp.zeros((CONV_A_SUB, CONV_A_CH), F32) + cb_ref[...]
        for j in range(CONV_A_WIDTH):
            phase = (first + j) % SUBLANES
            rows = pl.ds(r0 + first + j - phase, CONV_A_SUB)
            tap = ext_ref[rows, :] if phase == 0 else sh_ref[phase - 1, rows, :]
            acc = acc + tap * cw_ref[pl.ds(j, 1), :]
        y_ref[pl.ds(r0, CONV_A_SUB), :] = acc
    ext_ref[pl.ds(0, CONV_A_HALO), :] = ext_ref[pl.ds(ts, CONV_A_HALO), :]
    y = y_ref[...]
    mu = jnp.mean(y, axis=-1, keepdims=True)
    d = y - mu
    var = jnp.mean(d * d, axis=-1, keepdims=True)
    yn = d * lax.rsqrt(var + CONV_A_LN_EPS) * lg_ref[...] + lb_ref[...]
    o = _bdot(jax.nn.silu(yn), pj_ref[...])
    o_ref[0] = (pg_ref[0].astype(F32) * o).astype(o_ref.dtype)


def _conv_a(p, conv_w, conv_b, ln_g, ln_b, proj_bf):
    bsz, seq, _ = p.shape
    ts = min(seq, 1024)
    row = lambda a: a.reshape(1, -1)
    full = lambda shape: pl.BlockSpec(shape, lambda b, i: (0,) * len(shape))
    return pl.pallas_call(
        _conv_a_kernel,
        out_shape=SDS((bsz, seq, D_MODEL), MERGED_DTYPE),
        grid=(bsz, seq // ts),
        in_specs=[pl.BlockSpec((1, ts, CONV_A_CH), lambda b, i: (b, i, SEG_A // CONV_A_CH)),
                  pl.BlockSpec((1, ts, D_MODEL), lambda b, i: (b, i, SEG_G // D_MODEL)),
                  full((CONV_A_WIDTH, CONV_A_CH)), full((1, CONV_A_CH)), full((1, CONV_A_CH)),
                  full((1, CONV_A_CH)), full((CONV_A_CH, D_MODEL))],
        out_specs=pl.BlockSpec((1, ts, D_MODEL), lambda b, i: (b, i, 0)),
        scratch_shapes=[pltpu.VMEM((ts + CONV_A_HALO, CONV_A_CH), F32),
                        pltpu.VMEM((SUBLANES - 1, ts + CONV_A_HALO - SUBLANES, CONV_A_CH), F32),
                        pltpu.VMEM((ts, CONV_A_CH), F32)],
        compiler_params=_cparams(("arbitrary", "arbitrary")),
        name="conv_a",
    )(p, p, conv_w, row(conv_b), row(ln_g), row(ln_b), proj_bf)


def _lru_kernel(pc_ref, pg_ref, m_ref, cw_ref, cb_ref, wa_ref, ba_ref, wx_ref, bx_ref, lam_ref,
                pj_ref, o_ref, ext_ref, h_ref, a_ref, b_ref):
    ts = pc_ref.shape[1]
    groups = ts // SUBLANES

    @pl.when(pl.program_id(1) == 0)
    def _():
        ext_ref[pl.ds(0, SUBLANES), :] = jnp.zeros((SUBLANES, LRU_DIM), F32)
        h_ref[...] = jnp.zeros((SUBLANES, LRU_DIM), F32)

    pc = pc_ref[0].astype(F32)
    y_gate = pc[:, :LRU_DIM]
    ext_ref[pl.ds(SUBLANES, ts), :] = pc[:, LRU_DIM:]
    first = SUBLANES - (LRU_CONV - 1)
    xc = jnp.zeros((ts, LRU_DIM), F32) + cb_ref[...]
    for j in range(LRU_CONV):
        xc = xc + ext_ref[pl.ds(first + j, ts), :] * cw_ref[pl.ds(j, 1), :]
    ext_ref[pl.ds(0, SUBLANES), :] = ext_ref[pl.ds(ts, SUBLANES), :]

    def block_diag(w_ref):
        return jnp.concatenate(
            [_bdot(xc[:, h * LRU_BLOCK:(h + 1) * LRU_BLOCK], w_ref[h]) for h in range(LRU_HEADS)],
            axis=1)

    gate_a = jax.nn.sigmoid(block_diag(wa_ref) + ba_ref[...])
    gate_x = jax.nn.sigmoid(block_diag(wx_ref) + bx_ref[...])
    log_a = -LRU_C * gate_a * jax.nn.softplus(-lam_ref[...])
    a = jnp.exp(log_a)
    b = xc * gate_x * jnp.sqrt(1.0 - jnp.exp(2.0 * log_a))

    a3 = a.reshape(groups, SUBLANES, LRU_DIM)
    b3 = b.reshape(groups, SUBLANES, LRU_DIM)
    row = lax.broadcasted_iota(I32, (groups, SUBLANES, LRU_DIM), 1)
    for s in (1, 2, 4):
        keep = row >= s
        b3 = jnp.where(keep, a3 * pltpu.roll(b3, s, axis=1) + b3, b3)
        a3 = jnp.where(keep, a3 * pltpu.roll(a3, s, axis=1), a3)
    a_ref[...] = a3.reshape(ts, LRU_DIM)
    b_ref[...] = b3.reshape(ts, LRU_DIM)
    h = h_ref[...]
    for g in range(groups):
        rows = pl.ds(g * SUBLANES, SUBLANES)
        hg = a_ref[rows, :] * h + b_ref[rows, :]
        b_ref[rows, :] = hg
        h = jnp.broadcast_to(hg[SUBLANES - 1:SUBLANES, :], (SUBLANES, LRU_DIM))
    h_ref[...] = h
    o = _bdot(b_ref[...] * y_gate, pj_ref[...])
    o_ref[0] = (m_ref[0].astype(F32) + pg_ref[0].astype(F32) * o).astype(o_ref.dtype)


def _lru(p, merged, conv_w, conv_b, wa_bf, ba, wx_bf, bx, lam, proj_bf):
    bsz, seq, _ = p.shape
    ts = min(seq, 512)
    row = lambda a: a.reshape(1, -1)
    full = lambda shape: pl.BlockSpec(shape, lambda b, i: (0,) * len(shape))
    return pl.pallas_call(
        _lru_kernel,
        out_shape=SDS((bsz, seq, D_MODEL), MERGED_DTYPE),
        grid=(bsz, seq // ts),
        in_specs=[pl.BlockSpec((1, ts, 2 * LRU_DIM), lambda b, i: (b, i, SEG_C // (2 * LRU_DIM))),
                  pl.BlockSpec((1, ts, D_MODEL), lambda b, i: (b, i, SEG_G // D_MODEL + 2)),
                  pl.BlockSpec((1, ts, D_MODEL), lambda b, i: (b, i, 0)),
                  full((LRU_CONV, LRU_DIM)), full((1, LRU_DIM)),
                  full((LRU_HEADS, LRU_BLOCK, LRU_BLOCK)), full((1, LRU_DIM)),
                  full((LRU_HEADS, LRU_BLOCK, LRU_BLOCK)), full((1, LRU_DIM)),
                  full((1, LRU_DIM)), full((LRU_DIM, D_MODEL))],
        out_specs=pl.BlockSpec((1, ts, D_MODEL), lambda b, i: (b, i, 0)),
        scratch_shapes=[pltpu.VMEM((ts + SUBLANES, LRU_DIM), F32),
                        pltpu.VMEM((SUBLANES, LRU_DIM), F32),
                        pltpu.VMEM((ts, LRU_DIM), F32),
                        pltpu.VMEM((ts, LRU_DIM), F32)],
        compiler_params=_cparams(("arbitrary", "arbitrary")),
        name="rg_lru",
    )(p, p, merged, conv_w, row(conv_b), wa_bf, row(ba), wx_bf, row(bx), row(lam), proj_bf)


def _rwkv_prep_kernel(pb_ref, mu_ref, w0_ref, wup_ref, a0_ref, aup_ref, gup_ref, kk_ref, ka_ref,
                      rk_ref, bd_ref, ltri_ref,
                      rt_ref, kkt_ref, kh_ref, bh_ref, v_ref, pinc_ref, bonus_ref, g_ref, ext_ref):
    ts = pb_ref.shape[1]

    @pl.when(pl.program_id(1) == 0)
    def _():
        ext_ref[pl.ds(0, SUBLANES), :] = jnp.zeros((SUBLANES, ext_ref.shape[1]), F32)

    p = pb_ref[0].astype(F32)
    ext_ref[pl.ds(SUBLANES, ts), :] = p
    prev = ext_ref[pl.ds(SUBLANES - 1, ts), :]
    ext_ref[pl.ds(0, SUBLANES), :] = ext_ref[pl.ds(ts, SUBLANES), :]
    pm = p + (prev - p) * mu_ref[...]
    r = pm[:, RW_R:RW_R + RWKV_DIM]
    k = pm[:, RW_K:RW_K + RWKV_DIM]
    v = pm[:, RW_V:RW_V + RWKV_DIM]
    xwa = pm[:, RW_XWA:RW_XWA + LANES]
    xg = pm[:, RW_XG:RW_XG + LORA_G]
    w = -jax.nn.softplus(-(w0_ref[...] + _bdot(jnp.tanh(xwa), wup_ref[...]))) - 0.5
    lw = -jnp.exp(w)
    a = jax.nn.sigmoid(a0_ref[...] + _bdot(xwa, aup_ref[...]))
    g_ref[0] = _bdot(jax.nn.sigmoid(xg), gup_ref[...]).astype(g_ref.dtype)
    kkr = k * kk_ref[...]
    ss = _head_sums(kkr * kkr, bd_ref[...])
    kk = kkr / jnp.maximum(jnp.sqrt(ss), 1e-12)
    k2 = k * (1.0 + (a - 1.0) * ka_ref[...])
    lw_hi = lw.astype(BF16)
    lw_mid, lw_lo = _split(lw - lw_hi.astype(F32))
    tri = ltri_ref[...]
    lcum = (jnp.dot(tri, lw_hi, preferred_element_type=F32) + jnp.dot(tri, lw_mid, preferred_element_type=F32)
            + jnp.dot(tri, lw_lo, preferred_element_type=F32))
    pinc = jnp.exp(lcum)
    pinv = jnp.exp(-lcum)
    rt_ref[0] = r * pinc
    kkt_ref[0] = kk * jnp.exp(lcum - lw)
    kh_ref[0] = k2 * pinv
    bh_ref[0] = kk * a * pinv
    v_ref[0] = v
    pinc_ref[0] = pinc
    bonus_ref[0] = (_head_sums(r * k2 * rk_ref[...], bd_ref[...]) * v).astype(bonus_ref.dtype)


def _dot3(a, b):
    d = lambda x, y: jnp.dot(x, y, preferred_element_type=F32)
    m = a[0].shape[0]
    both = d(jnp.concatenate([a[0], a[1]], axis=0), b[0])
    return both[:m] + both[m:] + d(a[0], b[1])


def _rwkv_scan_kernel(rt_ref, kkt_ref, kh_ref, bh_ref, v_ref, pinc_ref, y_ref, s_ref):
    c = RWKV_CHUNK
    n = RWKV_HEAD
    nb = rt_ref.shape[0]
    heads = range(nb * RWKV_HEADS)

    @pl.when(pl.program_id(1) == 0)
    def _():
        s_ref[...] = jnp.zeros(s_ref.shape, F32)

    row = lax.broadcasted_iota(I32, (c, c), 0)
    col = lax.broadcasted_iota(I32, (c, c), 1)
    eye = (row == col).astype(F32)
    same16 = (row // 16) == (col // 16)
    same32 = (row // 32) == (col // 32)
    row2 = lax.broadcasted_iota(I32, (c, 2 * c), 0)
    col2 = lax.broadcasted_iota(I32, (c, 2 * c), 1) % c
    nt = lambda a, b: lax.dot_general(a, b, (((1,), (1,)), ((), ())), preferred_element_type=F32)
    tn = lambda a, b: lax.dot_general(a, b, (((0,), (0,)), ((), ())), preferred_element_type=F32)
    dot = lambda a, b: jnp.dot(a, b, preferred_element_type=F32)
    sl = [pl.ds((h % RWKV_HEADS) * n, n) for h in heads]
    sq = [h // RWKV_HEADS for h in heads]
    v = [v_ref[sq[h], :, sl[h]] for h in heads]
    pc = [pinc_ref[sq[h], pl.ds(c - 1, 1), sl[h]] for h in heads]
    s = [s_ref[sq[h], :, sl[h]] for h in heads]
    lhs = [jnp.concatenate([kkt_ref[sq[h], :, sl[h]], rt_ref[sq[h], :, sl[h]]], axis=0) for h in heads]
    rhs = [jnp.concatenate([bh_ref[sq[h], :, sl[h]], kh_ref[sq[h], :, sl[h]]], axis=0) for h in heads]
    big = [nt(lhs[h], rhs[h]) for h in heads]
    from_state = [nt(lhs[h], s[h]) for h in heads]
    top = [jnp.where(row2 > col2, big[h][:c], 0.0) for h in heads]
    bot = [jnp.where(row2 >= col2, big[h][c:], 0.0) for h in heads]
    a_b = [top[h][:, :c] for h in heads]
    akv = [dot(top[h], jnp.concatenate([jnp.zeros((c, n), F32), v[h]], axis=0)) for h in heads]
    d16 = [jnp.where(same16, a_b[h], 0.0) for h in heads]
    sd = [_split(d16[h]) for h in heads]
    s2 = [_split(_dot3(sd[h], sd[h])) for h in heads]
    s4 = [_split(_dot3(s2[h], s2[h])) for h in heads]
    s8 = [_split(_dot3(s4[h], s4[h])) for h in heads]
    t = [eye - d16[h] for h in heads]
    for sp in (s2, s4, s8):
        t = [t[h] + _dot3(_split(t[h]), sp[h]) for h in heads]
    for off in ([jnp.where(same32 & jnp.logical_not(same16), a_b[h], 0.0) for h in heads],
                [jnp.where(same32, 0.0, a_b[h]) for h in heads]):
        tb = [t[h].astype(BF16) for h in heads]
        lt = [dot(off[h].astype(BF16), tb[h]).astype(BF16) for h in heads]
        t = [t[h] - dot(tb[h], lt[h]) for h in heads]
    u = [dot(t[h], from_state[h][:c] + akv[h]) for h in heads]
    vu = [jnp.concatenate([-u[h], v[h]], axis=0) for h in heads]
    y = [from_state[h][c:] + dot(bot[h], vu[h]) for h in heads]
    s_new = [s[h] * pc[h] + tn(vu[h], rhs[h] * pc[h]) for h in heads]
    for q in range(nb):
        mine = slice(q * RWKV_HEADS, (q + 1) * RWKV_HEADS)
        y_ref[q] = jnp.concatenate(y[mine], axis=1)
        s_ref[q] = jnp.concatenate(s_new[mine], axis=1)


def _rwkv_post_kernel(y_ref, bonus_ref, g_ref, pg_ref, m_ref, gg_ref, gb_ref, bdm_ref, pj_ref, o_ref):
    y = y_ref[0]
    mu = _head_sums(y, bdm_ref[...])
    d = y - mu
    var = _head_sums(d * d, bdm_ref[...])
    yn = d * lax.rsqrt(var + RWKV_GN_EPS) * gg_ref[...] + gb_ref[...] + bonus_ref[0].astype(F32)
    o = _bdot(yn * g_ref[0].astype(F32), pj_ref[...])
    o_ref[0] = (m_ref[0].astype(F32) + pg_ref[0].astype(F32) * o).astype(o_ref.dtype)


def _rwkv(p, merged, mu_pad, w0, wup_pad, a0, aup_pad, g_up, k_k, k_a, r_k, gn_g, gn_b, proj_bf):
    bsz, seq, _ = p.shape
    row = lambda a: a.reshape(1, -1)
    full = lambda shape: pl.BlockSpec(shape, lambda b, i: (0,) * len(shape))
    head_id = jnp.arange(RWKV_DIM, dtype=I32) // RWKV_HEAD
    bd = (head_id[:, None] == head_id[None, :]).astype(BF16)

    ts = min(seq, 512)
    t_id = jnp.arange(ts, dtype=I32)
    ltri = ((t_id[:, None] // RWKV_CHUNK == t_id[None, :] // RWKV_CHUNK)
            & (t_id[:, None] >= t_id[None, :])).astype(BF16)
    seq_blk = lambda width: pl.BlockSpec((1, ts, width), lambda b, i: (b, i, 0))
    wide = SDS((bsz, seq, RWKV_DIM), F32)
    rt, kkt, kh, bh, v, pinc, bonus, g = pl.pallas_call(
        _rwkv_prep_kernel,
        out_shape=[wide] * 6 + [SDS((bsz, seq, RWKV_DIM), BF16)] * 2,
        grid=(bsz, seq // ts),
        in_specs=[pl.BlockSpec((1, ts, 2048), lambda b, i: (b, i, SEG_B // 2048)),
                  full((1, 2048)), full((1, RWKV_DIM)), full((LANES, RWKV_DIM)),
                  full((1, RWKV_DIM)), full((LANES, RWKV_DIM)), full((LORA_G, RWKV_DIM)),
                  full((1, RWKV_DIM)), full((1, RWKV_DIM)), full((1, RWKV_DIM)),
                  full((RWKV_DIM, RWKV_DIM)), full((ts, ts))],
        out_specs=[seq_blk(RWKV_DIM)] * 8,
        scratch_shapes=[pltpu.VMEM((ts + SUBLANES, 2048), F32)],
        compiler_params=_cparams(("arbitrary", "arbitrary")),
        name="rwkv_prep",
    )(p, mu_pad, row(w0), wup_pad, row(a0), aup_pad, g_up, row(k_k), row(k_a), row(r_k), bd, ltri)

    c = RWKV_CHUNK
    nb = next(n for n in (8, 4, 2, 1) if bsz % n == 0)
    chunk_blk = pl.BlockSpec((nb, c, RWKV_DIM), lambda b, i: (b, i, 0))
    y = pl.pallas_call(
        _rwkv_scan_kernel,
        out_shape=wide,
        grid=(bsz // nb, seq // c),
        in_specs=[chunk_blk] * 6,
        out_specs=chunk_blk,
        scratch_shapes=[pltpu.VMEM((nb, RWKV_HEAD, RWKV_DIM), F32)],
        compiler_params=_cparams(("arbitrary", "arbitrary")),
        name="rwkv_scan",
    )(rt, kkt, kh, bh, v, pinc)

    tp = min(seq, 1024)
    blk = lambda width: pl.BlockSpec((1, tp, width), lambda b, i: (b, i, 0))
    return pl.pallas_call(
        _rwkv_post_kernel,
        out_shape=SDS((bsz, seq, D_MODEL), MERGED_DTYPE),
        grid=(bsz, seq // tp),
        in_specs=[blk(RWKV_DIM), blk(RWKV_DIM), blk(RWKV_DIM),
                  pl.BlockSpec((1, tp, D_MODEL), lambda b, i: (b, i, SEG_G // D_MODEL + 1)),
                  blk(D_MODEL), full((1, RWKV_DIM)), full((1, RWKV_DIM)),
                  full((RWKV_DIM, RWKV_DIM)), full((RWKV_DIM, D_MODEL))],
        out_specs=blk(D_MODEL),
        compiler_params=_cparams(("arbitrary", "arbitrary")),
        name="rwkv_post",
    )(y, bonus, g, p, merged, row(gn_g), row(gn_b), bd * (1.0 / RWKV_HEAD), proj_bf)


def _out_kernel(m_ref, x_ref, g1_ref, w_ref, n2_ref, sc_ref, sh_ref, xo_ref, h_ref):
    xn = x_ref[0] + g1_ref[0] * _bdot(m_ref[0], w_ref[...])
    xo_ref[0] = xn
    h_ref[0] = _modulated_rmsnorm(xn, n2_ref[...], sc_ref[0], sh_ref[0])


def _out_proj(merged, x, g1, w_bf, norm2, sc2, sh2):
    bsz, seq, _ = x.shape
    ts = min(seq, 1024)
    blk = pl.BlockSpec((1, ts, D_MODEL), lambda b, i: (b, i, 0))
    per_b = pl.BlockSpec((1, 1, D_MODEL), lambda b, i: (b, 0, 0))
    return pl.pallas_call(
        _out_kernel,
        out_shape=[SDS((bsz, seq, D_MODEL), F32)] * 2,
        grid=(bsz, seq // ts),
        in_specs=[blk, blk, per_b, pl.BlockSpec((D_MODEL, D_MODEL), lambda b, i: (0, 0)),
                  pl.BlockSpec((1, D_MODEL), lambda b, i: (0, 0)), per_b, per_b],
        out_specs=[blk, blk],
        compiler_params=_cparams(("arbitrary", "arbitrary")),
        name="out_proj",
    )(merged, x, g1, w_bf, norm2, sc2, sh2)


MOE_TILE = 256
MOE_SLOTS = 2560
RUN_LOOP_UNROLL = 4
DISPATCH_PIECE = 48
DISPATCH_SPARE = DISPATCH_PIECE - SUBLANES
DISPATCH_WAIT_ROWS = 2048
assert MOE_SLOTS >= TOP_K * MOE_TILE + N_EXPERTS * (SUBLANES - 1) + DISPATCH_SPARE
U32 = jnp.uint32


ROUTE_TILES = 4


def _route_kernel(h_ref, rw_ref, bias_ref, upper_ref, ltri_ref, slot_ref, w_ref, n8_ref):
    tile = MOE_TILE
    subs = range(h_ref.shape[0] // tile)
    neg = -jnp.inf
    shape3 = (GROUP_SIZE, N_GROUPS, tile)
    to3 = lambda a: a.reshape(shape3)
    lanes_of = lambda a: jnp.concatenate([a] * (tile // LANES), axis=1)
    fold = lambda a: jnp.sum(jnp.sum(a, axis=0), axis=0, keepdims=True)
    slab = lax.broadcasted_iota(I32, shape3, 0).astype(F32)
    grp = lax.broadcasted_iota(I32, shape3, 1).astype(F32)
    eid = grp * GROUP_SIZE + slab
    gi = lax.broadcasted_iota(I32, (N_GROUPS, tile), 0).astype(F32)
    ones_cols = jnp.ones((tile, LANES), BF16)
    s3 = [to3(jax.nn.sigmoid(_hdot_nt(rw_ref[...], h_ref[pl.ds(q * tile, tile), :]))) for q in subs]
    b3 = [s3[q] + to3(bias_ref[...]) for q in subs]
    m1 = [jnp.max(b3[q], axis=0, keepdims=True) for q in subs]
    first = [jnp.min(jnp.where(b3[q] == m1[q], slab, GROUP_SIZE), axis=0, keepdims=True) for q in subs]
    m2 = [jnp.max(jnp.where(slab == first[q], neg, b3[q]), axis=0, keepdims=True) for q in subs]
    gs = [(m1[q] + m2[q])[0] for q in subs]
    chosen = [jnp.zeros((N_GROUPS, tile), F32) for q in subs]
    for _ in range(TOPK_GROUPS):
        m = [jnp.max(gs[q], axis=0, keepdims=True) for q in subs]
        hit = [gi == jnp.min(jnp.where(gs[q] == m[q], gi, N_GROUPS), axis=0, keepdims=True) for q in subs]
        chosen = [jnp.where(hit[q], 1.0, chosen[q]) for q in subs]
        gs = [jnp.where(hit[q], neg, gs[q]) for q in subs]
    cur = [jnp.where((chosen[q] > 0.0)[None], b3[q], neg) for q in subs]
    base = [jnp.zeros((N_EXPERTS, LANES), F32) for q in subs]
    picks, w_rows, rank_rows = [[] for q in subs], [[] for q in subs], [[] for q in subs]
    for _ in range(TOP_K):
        m = [jnp.max(jnp.max(cur[q], axis=0), axis=0, keepdims=True)[None] for q in subs]
        pick = [jnp.min(jnp.min(jnp.where(cur[q] == m[q], eid, N_EXPERTS), axis=0), axis=0, keepdims=True)
                for q in subs]
        hit = [eid == pick[q][None] for q in subs]
        onehot = [hit[q].astype(BF16).reshape(N_EXPERTS, tile) for q in subs]
        before = [jnp.dot(onehot[q], upper_ref[...], preferred_element_type=F32) for q in subs]
        count = [jnp.dot(onehot[q], ones_cols, preferred_element_type=F32) for q in subs]
        for q in subs:
            w_rows[q].append(fold(jnp.where(hit[q], s3[q], 0.0)))
            picks[q].append(pick[q])
            rank_rows[q].append(fold(jnp.where(hit[q], to3(before[q] + lanes_of(base[q])), 0.0)))
        cur = [jnp.where(hit[q], neg, cur[q]) for q in subs]
        base = [base[q] + count[q] for q in subs]
    n8 = [jnp.floor((base[q] + (SUBLANES - 1.0)) * (1.0 / SUBLANES)) * SUBLANES for q in subs]
    run_start = [to3(lanes_of(_hdot(ltri_ref[...], n8[q]))) for q in subs]
    slots, weights = [], []
    for q in subs:
        slots.append(jnp.concatenate(
            [rank_rows[q][j] + fold(jnp.where(eid == picks[q][j][None], run_start[q], 0.0)) for j in range(TOP_K)],
            axis=0))
        w_all = jnp.concatenate(w_rows[q], axis=0)
        weights.append(w_all / jnp.sum(w_all, axis=0, keepdims=True) * ROUTED_SCALE)
    for q in subs:
        n8_ref[q] = n8[q]
    w_ref[...] = jnp.concatenate(weights, axis=1)
    slot_ref[...] = jnp.concatenate(slots, axis=1).astype(I32)


def _route(h2, router_w, router_bias):
    n_tok = h2.shape[0]
    tile = MOE_TILE
    n_tiles = n_tok // tile
    per_step = ROUTE_TILES if n_tiles % ROUTE_TILES == 0 else 1
    regroup = lambda a: a.reshape(N_GROUPS, GROUP_SIZE, -1).transpose(1, 0, 2).reshape(N_EXPERTS, -1)
    rw = regroup(router_w.T)
    bias = jnp.broadcast_to(regroup(router_bias.reshape(N_EXPERTS, 1)), (N_EXPERTS, tile))
    t_id = jnp.arange(tile, dtype=I32)
    upper = (t_id[:, None] < t_id[None, :]).astype(BF16)
    e_id = jnp.arange(N_EXPERTS, dtype=I32)
    ltri = (e_id[:, None] > e_id[None, :]).astype(F32)
    tok_blk = pl.BlockSpec((TOP_K, per_step * tile), lambda i: (0, i))
    full = lambda shape: pl.BlockSpec(shape, lambda i: (0,) * len(shape))
    slot_t, w_t, n8 = pl.pallas_call(
        _route_kernel,
        out_shape=[SDS((TOP_K, n_tok), I32), SDS((TOP_K, n_tok), F32),
                   SDS((n_tiles, N_EXPERTS, LANES), F32)],
        grid=(n_tiles // per_step,),
        in_specs=[pl.BlockSpec((per_step * tile, D_MODEL), lambda i: (i, 0)),
                  full((N_EXPERTS, D_MODEL)), full((N_EXPERTS, tile)), full((tile, tile)),
                  full((N_EXPERTS, N_EXPERTS))],
        out_specs=[tok_blk, tok_blk, pl.BlockSpec((per_step, N_EXPERTS, LANES), lambda i: (i, 0, 0))],
        compiler_params=_cparams(("arbitrary",)),
        name="moe_route",
    )(h2, rw, bias, upper, ltri)
    return slot_t, w_t, n8[:, :, 0].astype(I32)


def _for_each_run_piece(tile_idx, n8_ref, fn):
    for r in range(N_EXPERTS):
        n = n8_ref[tile_idx * N_EXPERTS + r]
        size = MOE_TILE
        while size >= SUBLANES:
            @pl.when((n & size) != 0)
            def _(size=size):
                fn(r, n & ~(2 * size - 1), size)
            size //= 2


def _wait_rows(total, make_wait):
    size = SUBLANES
    while size <= MOE_SLOTS:
        @pl.when((total & size) != 0)
        def _(size=size):
            make_wait(size).wait()
        size *= 2


def _pack_pairs(hi_bits, lo_bits):
    return (hi_bits & jnp.uint32(0xFFFF0000)) | (lo_bits >> 16)


def _unpack_pairs(u):
    hi = lax.bitcast_convert_type(u & jnp.uint32(0xFFFF0000), F32)
    lo = lax.bitcast_convert_type(u << 16, F32)
    return jnp.concatenate([hi, lo], axis=1).astype(BF16)


def _dispatch_kernel(n8_ref, off_ref, dst_ref, tot_ref, pad_end_ref, slot_ref, h_ref, xs_ref,
                     g_ref, zero_ref, sem):
    i = pl.program_id(0)
    tile = h_ref.shape[0]
    half = D_MODEL // 2

    @pl.when(i == 0)
    def _():
        zero_ref[...] = jnp.zeros(zero_ref.shape, U32)

        def last_block(e, back):
            start = pl.multiple_of(jnp.maximum(pad_end_ref[e] - back * MOE_ROWS, 0), MOE_ROWS)
            return pltpu.make_async_copy(zero_ref, xs_ref.at[pl.ds(start, MOE_ROWS), :], sem.at[0])

        def has_blocks(e, back):
            prev = jnp.where(e > 0, pad_end_ref[jnp.maximum(e - 1, 0)], 0)
            return pad_end_ref[e] - prev >= back * MOE_ROWS

        def clear(e, carry):
            for back in (1, 2):
                pl.when(has_blocks(e, back))(lambda back=back: last_block(e, back).start())
            return carry
        lax.fori_loop(0, N_EXPERTS, clear, 0)

        def done(e, carry):
            for back in (1, 2):
                pl.when(has_blocks(e, back))(lambda back=back: last_block(e, back).wait())
            return carry
        lax.fori_loop(0, N_EXPERTS, done, 0)

        def tail_block(b):
            start = pl.multiple_of(b * MOE_ROWS, MOE_ROWS)
            return pltpu.make_async_copy(zero_ref, xs_ref.at[pl.ds(start, MOE_ROWS), :], sem.at[0])

        def clear_tail(b, carry):
            tail_block(b).start()
            return carry

        def done_tail(b, carry):
            tail_block(b).wait()
            return carry
        used = pad_end_ref[N_EXPERTS - 1] // MOE_ROWS
        lax.fori_loop(used, xs_ref.shape[0] // MOE_ROWS, clear_tail, 0)
        lax.fori_loop(used, xs_ref.shape[0] // MOE_ROWS, done_tail, 0)

    slot_id = lax.broadcasted_iota(I16, (MOE_SLOTS, tile), 0)
    slots = slot_ref[...].astype(I16)
    sel = jnp.zeros((MOE_SLOTS, tile), BF16)
    for j in range(TOP_K):
        sel = jnp.where(slot_id == slots[j:j + 1, :], jnp.ones((), BF16), sel)
    g = jnp.dot(sel, h_ref[...].astype(BF16), preferred_element_type=F32)
    bits = lax.bitcast_convert_type(g, U32)
    buf = i % 2
    g_ref[buf] = _pack_pairs(bits[:, :half], bits[:, half:])

    def drain(step):
        total = tot_ref[step]
        whole = pltpu.make_async_copy(g_ref.at[step % 2, pl.ds(0, DISPATCH_WAIT_ROWS), :],
                                      xs_ref.at[pl.ds(0, DISPATCH_WAIT_ROWS), :], sem.at[step % 2])

        def wait_whole(k, carry):
            whole.wait()
            return carry
        lax.fori_loop(0, lax.shift_right_logical(total, DISPATCH_WAIT_ROWS.bit_length() - 1), wait_whole, 0)
        _wait_rows(total & (DISPATCH_WAIT_ROWS - 1), lambda rows: pltpu.make_async_copy(
            g_ref.at[step % 2, pl.ds(0, rows), :], xs_ref.at[pl.ds(0, rows), :], sem.at[step % 2]))

    pl.when(i > 0)(lambda: drain(i - 1))

    def copy(r, k):
        src = pl.multiple_of(off_ref[i * N_EXPERTS + r] + k * DISPATCH_PIECE, SUBLANES)
        dst = pl.multiple_of(dst_ref[i * N_EXPERTS + r] + k * DISPATCH_PIECE, SUBLANES)
        pltpu.make_async_copy(g_ref.at[buf, pl.ds(src, DISPATCH_PIECE), :],
                              xs_ref.at[pl.ds(dst, DISPATCH_PIECE), :], sem.at[buf]).start()

    def per_expert(r, carry):
        n = n8_ref[i * N_EXPERTS + r]
        pl.when(n > 0)(lambda: copy(r, 0))

        @pl.when(n > DISPATCH_PIECE)
        def _():
            def more(k, c):
                copy(r, k)
                return c
            lax.fori_loop(1, (n + (DISPATCH_PIECE - 1)) // DISPATCH_PIECE, more, 0)
        return carry
    lax.fori_loop(0, N_EXPERTS, per_expert, 0, unroll=RUN_LOOP_UNROLL)
    pl.when(i == pl.num_programs(0) - 1)(lambda: drain(i))


def _dispatch(h2, slot_t, n8_flat, off_flat, dst_flat, tot, pad_end, n_rows):
    n_tok = h2.shape[0]
    tile = MOE_TILE
    grid_spec = pltpu.PrefetchScalarGridSpec(
        num_scalar_prefetch=5,
        grid=(n_tok // tile,),
        in_specs=[pl.BlockSpec((TOP_K, tile), lambda i, *_: (0, i)),
                  pl.BlockSpec((tile, D_MODEL), lambda i, *_: (i, 0))],
        out_specs=pl.BlockSpec(memory_space=pl.ANY),
        scratch_shapes=[pltpu.VMEM((2, MOE_SLOTS, D_MODEL // 2), U32),
                        pltpu.VMEM((MOE_ROWS, D_MODEL // 2), U32), pltpu.SemaphoreType.DMA((2,))],
    )
    return pl.pallas_call(
        _dispatch_kernel,
        out_shape=SDS((n_rows, D_MODEL // 2), U32),
        grid_spec=grid_spec,
        compiler_params=_cparams(("arbitrary",)),
        name="moe_dispatch",
    )(n8_flat, off_flat, dst_flat, tot, pad_end, slot_t, h2)


def _expert_kernel(first_ref, nblk_ref, exp_of_ref, xs_ref, w1_ref, w3_ref, w2_ref, y_ref,
                   xbuf, ybuf, wb1, wb3, wb2, in_sem, out_sem):
    del exp_of_ref
    r = pl.program_id(0)
    half = D_MODEL // 2
    used = first_ref[N_EXPERTS - 1] + nblk_ref[N_EXPERTS - 1]
    part = MOE_ROWS // EXPERT_DMA_PARTS

    class _Copies:
        def __init__(self, make):
            self.parts = [make(p) for p in range(EXPERT_DMA_PARTS)]

        def start(self):
            for c in self.parts:
                c.start()

        def wait(self):
            for c in self.parts:
                c.wait()

    def hbm_rows(g, p):
        return pl.ds(pl.multiple_of(g * MOE_ROWS + p * part, part), part)

    in_copy = lambda g, slot: _Copies(lambda p: pltpu.make_async_copy(
        xs_ref.at[hbm_rows(g, p), :], xbuf.at[slot, pl.ds(p * part, part), :], in_sem.at[slot]))
    out_copy = lambda g, slot: _Copies(lambda p: pltpu.make_async_copy(
        ybuf.at[slot, pl.ds(p * part, part), :], y_ref.at[hbm_rows(g, p), :], out_sem.at[slot]))

    @pl.when(r == 0)
    def _():
        for g in range(EXPERT_AHEAD):
            pl.when(g < used)(lambda g=g: in_copy(g, g).start())

    wb1[...] = w1_ref[0, 0].astype(BF16)
    wb3[...] = w3_ref[0, 0].astype(BF16)
    wb2[...] = w2_ref[0, 0].astype(BF16)

    def process(blocks):
        dot = lambda a, b: jnp.dot(a, b, preferred_element_type=F32)
        slots = [g % EXPERT_BUFS for g in blocks]
        for g, slot in zip(blocks, slots):
            in_copy(g, slot).wait()
            ahead = g + EXPERT_AHEAD
            pl.when(ahead < used)(lambda ahead=ahead: in_copy(ahead, ahead % EXPERT_BUFS).start())
            pl.when(g >= EXPERT_BUFS)(lambda g=g, slot=slot: out_copy(g - EXPERT_BUFS, slot).wait())
        x = [_unpack_pairs(xbuf[slot]) for slot in slots]
        gate = [dot(xq, wb1[...]) for xq in x]
        up = [dot(xq, wb3[...]) for xq in x]
        hid = [(jax.nn.silu(a) * b).astype(BF16) for a, b in zip(gate, up)]
        for g, slot, hq in zip(blocks, slots, hid):
            bits = lax.bitcast_convert_type(dot(hq, wb2[...]).astype(BF16).astype(F32), U32)
            ybuf[slot] = _pack_pairs(bits[:, :half], bits[:, half:])
            out_copy(g, slot).start()

    def pair(k, carry):
        g = first_ref[r] + 2 * k
        process([g, g + 1])
        return carry
    n_mine = nblk_ref[r]
    lax.fori_loop(0, lax.shift_right_logical(n_mine, 1), pair, 0)
    pl.when((n_mine & 1) == 1)(lambda: process([first_ref[r] + n_mine - 1]))

    @pl.when(r == N_EXPERTS - 1)
    def _():
        for back in range(EXPERT_BUFS, 0, -1):
            pl.when(used >= back)(
                lambda back=back: out_copy(used - back, (used - back) % EXPERT_BUFS).wait())

        ybuf[0] = jnp.zeros(ybuf.shape[1:], U32)
        n_blocks = y_ref.shape[0] // MOE_ROWS

        def clear(g, carry):
            out_copy(g, 0).start()
            return carry

        def done(g, carry):
            out_copy(g, 0).wait()
            return carry
        lax.fori_loop(used, n_blocks, clear, 0)
        lax.fori_loop(used, n_blocks, done, 0)


def _experts(xs, first_block, n_block, exp_of_row, w1, w3, w2, layer):
    n_rows = xs.shape[0]
    half = D_MODEL // 2
    w_in_blk = pl.BlockSpec((1, 1, D_MODEL, EXPERT_FF), lambda r, first, nblk, eo: (layer, eo[r], 0, 0))
    grid_spec = pltpu.PrefetchScalarGridSpec(
        num_scalar_prefetch=3,
        grid=(N_EXPERTS,),
        in_specs=[pl.BlockSpec(memory_space=pl.ANY), w_in_blk, w_in_blk,
                  pl.BlockSpec((1, 1, EXPERT_FF, D_MODEL), lambda r, first, nblk, eo: (layer, eo[r], 0, 0))],
        out_specs=pl.BlockSpec(memory_space=pl.ANY),
        scratch_shapes=[pltpu.VMEM((EXPERT_BUFS, MOE_ROWS, half), U32),
                        pltpu.VMEM((EXPERT_BUFS, MOE_ROWS, half), U32),
                        pltpu.VMEM((D_MODEL, EXPERT_FF), BF16), pltpu.VMEM((D_MODEL, EXPERT_FF), BF16),
                        pltpu.VMEM((EXPERT_FF, D_MODEL), BF16),
                        pltpu.SemaphoreType.DMA((EXPERT_BUFS,)), pltpu.SemaphoreType.DMA((EXPERT_BUFS,))],
    )
    return pl.pallas_call(
        _expert_kernel,
        out_shape=SDS((n_rows, half), U32),
        grid_spec=grid_spec,
        compiler_params=_cparams(("arbitrary",)),
        name="moe_experts",
    )(first_block, n_block, exp_of_row, xs, w1, w3, w2)


def _combine_kernel(n8_ref, off_ref, src_ref, tot_ref, y_ref, slot_ref, w_ref, h_ref, x_ref, g2_ref,
                    s1_ref, s3_ref, s2_ref, fn_ref, o_ref, yt_ref, sem, *, final):
    tile = h_ref.shape[1]
    i = pl.program_id(0) * pl.num_programs(1) + pl.program_id(1)

    @pl.when(i == 0)
    def _():
        yt_ref[...] = jnp.zeros(yt_ref.shape, U32)

    def piece(r, offset, rows):
        src = pl.multiple_of(src_ref[i * N_EXPERTS + r] + offset, SUBLANES)
        dst = pl.multiple_of(off_ref[i * N_EXPERTS + r] + offset, SUBLANES)
        pltpu.make_async_copy(y_ref.at[pl.ds(src, rows), :], yt_ref.at[pl.ds(dst, rows), :], sem).start()

    _for_each_run_piece(i, n8_ref, piece)
    h = h_ref[0]
    shared = _bdot(jax.nn.silu(_bdot(h, s1_ref[...])) * _bdot(h, s3_ref[...]), s2_ref[...])
    slot_id = lax.broadcasted_iota(I16, (tile, MOE_SLOTS), 1)
    slots = slot_ref[0].astype(I16)
    w = w_ref[0].astype(BF16)
    pw = jnp.zeros((tile, MOE_SLOTS), BF16)
    for j in range(TOP_K):
        pw = jnp.where(slot_id == slots[:, j:j + 1], w[:, j:j + 1], pw)
    _wait_rows(tot_ref[i], lambda rows: pltpu.make_async_copy(
        y_ref.at[pl.ds(0, rows), :], yt_ref.at[pl.ds(0, rows), :], sem))
    routed = jnp.dot(pw, _unpack_pairs(yt_ref[...]), preferred_element_type=F32)
    xn = x_ref[0] + g2_ref[0] * (routed + shared)
    if final:
        xn = xn * lax.rsqrt(jnp.mean(xn * xn, axis=-1, keepdims=True) + NORM_EPS) * fn_ref[...]
    o_ref[0] = xn


def _combine(y, slot_nat, w_nat, n8_flat, off_flat, dst_flat, tot, h2, x, g2, s1_bf, s3_bf, s2_bf,
             final_norm, final):
    bsz, seq, _ = x.shape
    tile = MOE_TILE
    per_seq = seq // tile
    blk = pl.BlockSpec((1, tile, D_MODEL), lambda b, i, *_: (b, i, 0))
    tok = pl.BlockSpec((1, tile, TOP_K), lambda b, i, *_: (b, i, 0))
    full = lambda shape: pl.BlockSpec(shape, lambda b, i, *_: (0,) * len(shape))
    grid_spec = pltpu.PrefetchScalarGridSpec(
        num_scalar_prefetch=4,
        grid=(bsz, per_seq),
        in_specs=[pl.BlockSpec(memory_space=pl.ANY), tok, tok, blk, blk,
                  pl.BlockSpec((1, 1, D_MODEL), lambda b, i, *_: (b, 0, 0)),
                  full((D_MODEL, SHARED_FF)), full((D_MODEL, SHARED_FF)), full((SHARED_FF, D_MODEL)),
                  full((1, D_MODEL))],
        out_specs=blk,
        scratch_shapes=[pltpu.VMEM((MOE_SLOTS, D_MODEL // 2), U32), pltpu.SemaphoreType.DMA],
    )
    return pl.pallas_call(
        functools.partial(_combine_kernel, final=final),
        out_shape=SDS((bsz, seq, D_MODEL), F32),
        grid_spec=grid_spec,
        compiler_params=_cparams(("arbitrary", "arbitrary")),
        name="moe_combine",
    )(n8_flat, off_flat, dst_flat, tot, y, slot_nat, w_nat, h2, x, g2, s1_bf, s3_bf, s2_bf, final_norm)


def _moe(x, h2, g2, router_w, router_bias, w1, w3, w2, layer, s1_bf, s3_bf, s2_bf, final_norm, final):
    bsz, seq, _ = x.shape
    assert seq % MOE_TILE == 0, "token tiles must not straddle sequences"
    n_tok = bsz * seq
    slot_t, w_t, n8 = _route(h2.reshape(n_tok, D_MODEL), router_w, router_bias)
    n_tiles = n8.shape[0]
    counts = jnp.sum(n8, axis=0)
    padded = jnp.where(counts > 0, (counts + DISPATCH_SPARE + MOE_ROWS - 1) // MOE_ROWS * MOE_ROWS, 0)
    pad_end = jnp.cumsum(padded).astype(I32)
    pad_start = pad_end - padded
    run_row = (pad_start[None, :] + jnp.cumsum(n8, axis=0) - n8).astype(I32)
    run_slot = (jnp.cumsum(n8, axis=1) - n8).astype(I32)
    tot = jnp.sum(n8, axis=1).astype(I32)
    copied = jnp.sum((n8 + DISPATCH_PIECE - 1) // DISPATCH_PIECE * DISPATCH_PIECE, axis=1).astype(I32)
    max_rows = (n_tok * TOP_K + n_tiles * N_EXPERTS * (SUBLANES - 1)
                + N_EXPERTS * (DISPATCH_SPARE + MOE_ROWS - 1))
    n_blocks = (max_rows + MOE_ROWS - 1) // MOE_ROWS
    row_id = jnp.arange(N_EXPERTS, dtype=I32)
    exp_of_row = (row_id % N_GROUPS) * GROUP_SIZE + row_id // N_GROUPS
    flat = lambda a: a.reshape(-1).astype(I32)
    xs = _dispatch(h2.reshape(n_tok, D_MODEL), slot_t, flat(n8), flat(run_slot), flat(run_row), copied,
                   pad_end, n_blocks * MOE_ROWS)
    y = _experts(xs, (pad_start // MOE_ROWS).astype(I32), (padded // MOE_ROWS).astype(I32), exp_of_row,
                 w1, w3, w2, layer)
    nat = lambda a: a.T.reshape(bsz, seq, TOP_K)
    return _combine(y, nat(slot_t), nat(w_t), flat(n8), flat(run_slot), flat(run_row), tot, h2, x, g2,
                    s1_bf, s3_bf, s2_bf, final_norm, final)


def _pad_cols(a, width):
    return jnp.pad(a, ((0, 0), (0, width - a.shape[1])))


def _layout_w_in(w_in):
    b0 = COLS_A
    c0 = COLS_A + COLS_B
    g0 = c0 + COLS_C
    seg_b = _pad_cols(w_in[:, b0:c0], SEG_G - SEG_B)
    return jnp.concatenate([w_in[:, c0:g0], seg_b, w_in[:, g0:], w_in[:, :b0]], axis=1).astype(BF16)


def _layout_mu(mu):
    return _pad_cols(mu.reshape(1, -1), SEG_G - SEG_B)


def _pad_rows(a, height):
    return jnp.pad(a, ((0, height - a.shape[0]), (0, 0)))


def kernel(x, c, ada_w, ada_b, norm1, norm2, w_in, conv_a_w, conv_a_b, ln_a_g, ln_a_b, proj_a, mu_b, w0, w_up, a0, a_up, g_up, k_k, k_a, r_k, gn_b_g, gn_b_b, proj_b, conv_c_w, conv_c_b, lru_wa, lru_ba, lru_wx, lru_bx, lru_lambda, proj_c, w_out, router_w, router_bias, exp_w1, exp_w3, exp_w2, sh_w1, sh_w3, sh_w2, final_norm):
    depth = ada_w.shape[0]
    bsz = x.shape[0]
    mod = _ada_mod(c, ada_w, ada_b)
    for l in range(depth):
        sh1, sc1, g1, sh2, sc2, g2 = [mod[l, :, i * D_MODEL:(i + 1) * D_MODEL].reshape(bsz, 1, D_MODEL)
                                      for i in range(N_MOD)]
        p = _in_proj(x, sc1, sh1, norm1[l].reshape(1, -1), _layout_w_in(w_in[l]))
        merged = _conv_a(p, conv_a_w[l], conv_a_b[l], ln_a_g[l], ln_a_b[l], proj_a[l].astype(BF16))
        merged = _rwkv(p, merged, _layout_mu(mu_b[l]), w0[l], _pad_rows(w_up[l], LANES).astype(BF16),
                       a0[l], jnp.pad(a_up[l], ((LORA_W, 0), (0, 0))).astype(BF16), g_up[l].astype(BF16),
                       k_k[l], k_a[l], r_k[l], gn_b_g[l], gn_b_b[l], proj_b[l].astype(BF16))
        merged = _lru(p, merged, conv_c_w[l], conv_c_b[l], lru_wa[l].astype(BF16), lru_ba[l],
                      lru_wx[l].astype(BF16), lru_bx[l], lru_lambda[l], proj_c[l].astype(BF16))
        x, h2 = _out_proj(merged, x, g1, w_out[l].astype(BF16), norm2[l].reshape(1, -1), sc2, sh2)
        x = _moe(x, h2, g2, router_w[l], router_bias[l], exp_w1, exp_w3, exp_w2, l,
                 sh_w1[l].astype(BF16), sh_w3[l].astype(BF16), sh_w2[l].astype(BF16),
                 final_norm.reshape(1, -1), final=(l == depth - 1))
    return x
```

```python
import functools

import jax
import jax.numpy as jnp
from jax import lax
from jax.experimental import pallas as pl
from jax.experimental.pallas import tpu as pltpu

F32 = jnp.float32
BF16 = jnp.bfloat16
I32 = jnp.int32
I16 = jnp.int16
SDS = jax.ShapeDtypeStruct
HIGHEST = lax.Precision.HIGHEST

D_MODEL = 1024
N_MOD = 6
NORM_EPS = 1e-6
CONV_A_CH = 512
CONV_A_WIDTH = 31
CONV_A_LN_EPS = 1e-5
RWKV_HEADS = 8
RWKV_HEAD = 64
RWKV_DIM = RWKV_HEADS * RWKV_HEAD
LORA_W = 64
LORA_A = 64
LORA_G = 128
RWKV_GN_EPS = 64e-5
RWKV_CHUNK = 64
LRU_DIM = 1024
LRU_HEADS = 8
LRU_BLOCK = LRU_DIM // LRU_HEADS
LRU_CONV = 4
LRU_C = 8.0
N_EXPERTS = 64
TOP_K = 8
N_GROUPS = 8
GROUP_SIZE = N_EXPERTS // N_GROUPS
TOPK_GROUPS = 4
EXPERT_FF = 256
SHARED_FF = 256
ROUTED_SCALE = 2.5
SEG_C = 0
SEG_B = 2048
SEG_G = 4096
SEG_A = 7168
IN_COLS_PAD = 8192
RW_R, RW_K, RW_V, RW_XWA, RW_XG = 0, 512, 1024, 1536, 1664
COLS_A = 2 * CONV_A_CH
COLS_B = 3 * RWKV_DIM + LORA_W + LORA_A + LORA_G
COLS_C = 2 * LRU_DIM
VMEM_LIMIT = 56 * 1024 * 1024
MERGED_DTYPE = BF16
SUBLANES = 8
LANES = 128
MOE_ROWS = 512
EXPERT_DMA_PARTS = 4
EXPERT_AHEAD = 4
EXPERT_BUFS = EXPERT_AHEAD + 2


def _cparams(sem):
    return pltpu.CompilerParams(dimension_semantics=sem, vmem_limit_bytes=VMEM_LIMIT)


def _bdot(a, b):
    return jnp.dot(a.astype(BF16), b.astype(BF16), preferred_element_type=F32)


def _hdot(a, b):
    return jnp.dot(a, b, preferred_element_type=F32, precision=HIGHEST)


def _split(a):
    hi = a.astype(BF16)
    return hi, (a - hi.astype(F32)).astype(BF16)


def _head_sums(a, ones_bf):
    hi, lo = _split(a)
    return (jnp.dot(hi, ones_bf, preferred_element_type=F32)
            + jnp.dot(lo, ones_bf, preferred_element_type=F32))


def _hdot_nt(a, b):
    return lax.dot_general(a, b, (((1,), (1,)), ((), ())), preferred_element_type=F32,
                           precision=HIGHEST)


def _ada_kernel(c_ref, w_ref, b_ref, o_ref):
    cond = jax.nn.silu(c_ref[...])
    o_ref[0] = _bdot(cond, w_ref[0]) + b_ref[0]


def _ada_mod(c, ada_w, ada_b):
    depth, _, n = ada_w.shape
    bsz = c.shape[0]
    tn = 1536
    return pl.pallas_call(
        _ada_kernel,
        out_shape=SDS((depth, bsz, n), F32),
        grid=(depth, n // tn),
        in_specs=[pl.BlockSpec((bsz, D_MODEL), lambda l, j: (0, 0)),
                  pl.BlockSpec((1, D_MODEL, tn), lambda l, j: (l, 0, j)),
                  pl.BlockSpec((1, 1, tn), lambda l, j: (l, 0, j))],
        out_specs=pl.BlockSpec((1, bsz, tn), lambda l, j: (l, 0, j)),
        compiler_params=_cparams(("arbitrary", "arbitrary")),
        name="ada_mod",
    )(c, ada_w, ada_b.reshape(depth, 1, n))


def _modulated_rmsnorm(x, g, sc, sh):
    y = x * lax.rsqrt(jnp.mean(x * x, axis=-1, keepdims=True) + NORM_EPS)
    return (y * g) * (1.0 + sc) + sh


IN_PROJ_SUB = 512


def _in_kernel(x_ref, sc_ref, sh_ref, g_ref, w_ref, o_ref, h_ref):
    @pl.when(pl.program_id(2) == 0)
    def _():
        h_ref[...] = _modulated_rmsnorm(x_ref[0], g_ref[...], sc_ref[0], sh_ref[0]).astype(BF16)

    j = pl.program_id(2)
    ts, tn = o_ref.shape[1], o_ref.shape[2]
    is_gelu = j == SEG_C // tn
    is_gate = (j >= SEG_G // tn) & (j < SEG_A // tn)
    is_glu = j == SEG_A // tn

    def glu(acc):
        u = acc[:, :tn // 2] * jax.nn.sigmoid(acc[:, tn // 2:])
        return jnp.concatenate([u, jnp.zeros_like(u)], axis=1)

    def emit(act, sub):
        for r0 in range(0, ts, sub):
            rows = pl.ds(r0, sub)
            acc = jnp.dot(h_ref[rows, :], w_ref[...], preferred_element_type=F32)
            o_ref[0, rows, :] = act(acc).astype(o_ref.dtype)

    sub = min(ts, IN_PROJ_SUB)
    pl.when(is_gelu)(lambda: emit(functools.partial(jax.nn.gelu, approximate=True), sub))
    pl.when(is_gate)(lambda: emit(jax.nn.sigmoid, sub))
    pl.when(is_glu)(lambda: emit(glu, sub))
    pl.when(jnp.logical_not(is_gelu | is_gate | is_glu))(lambda: emit(lambda acc: acc, ts))


def _in_proj(x, sc, sh, g, w_pad):
    bsz, seq, _ = x.shape
    ts = min(seq, 2048)
    tn = 1024
    return pl.pallas_call(
        _in_kernel,
        out_shape=SDS((bsz, seq, IN_COLS_PAD), BF16),
        grid=(bsz, seq // ts, IN_COLS_PAD // tn),
        in_specs=[pl.BlockSpec((1, ts, D_MODEL), lambda b, i, j: (b, i, 0)),
                  pl.BlockSpec((1, 1, D_MODEL), lambda b, i, j: (b, 0, 0)),
                  pl.BlockSpec((1, 1, D_MODEL), lambda b, i, j: (b, 0, 0)),
                  pl.BlockSpec((1, D_MODEL), lambda b, i, j: (0, 0)),
                  pl.BlockSpec((D_MODEL, tn), lambda b, i, j: (0, j))],
        out_specs=pl.BlockSpec((1, ts, tn), lambda b, i, j: (b, i, j)),
        scratch_shapes=[pltpu.VMEM((ts, D_MODEL), BF16)],
        compiler_params=_cparams(("arbitrary", "arbitrary", "arbitrary")),
        name="in_proj",
    )(x, sc, sh, g, w_pad)


CONV_A_HALO = 32
CONV_A_SUB = 64


def _conv_a_kernel(pa_ref, pg_ref, cw_ref, cb_ref, lg_ref, lb_ref, pj_ref, o_ref, ext_ref, sh_ref, y_ref):
    ts = pa_ref.shape[1]

    @pl.when(pl.program_id(1) == 0)
    def _():
        ext_ref[pl.ds(0, CONV_A_HALO), :] = jnp.zeros((CONV_A_HALO, CONV_A_CH), F32)

    ext_ref[pl.ds(CONV_A_HALO, ts), :] = pa_ref[0].astype(F32)
    for p in range(1, SUBLANES):
        sh_ref[p - 1] = ext_ref[pl.ds(p, sh_ref.shape[1]), :]
    first = CONV_A_HALO - (CONV_A_WIDTH - 1)
    for r0 in range(0, ts, CONV_A_SUB):
        acc = jnp.zeros((CONV_A_SUB, CONV_A_CH), F32) + cb_ref[...]
        for j in range(CONV_A_WIDTH):
            phase = (first + j) % SUBLANES
            rows = pl.ds(r0 + first + j - phase, CONV_A_SUB)
            tap = ext_ref[rows, :] if phase == 0 else sh_ref[phase - 1, rows, :]
            acc = acc + tap * cw_ref[pl.ds(j, 1), :]
        y_ref[pl.ds(r0, CONV_A_SUB), :] = acc
    ext_ref[pl.ds(0, CONV_A_HALO), :] = ext_ref[pl.ds(ts, CONV_A_HALO), :]
    y = y_ref[...]
    mu = jnp.mean(y, axis=-1, keepdims=True)
    d = y - mu
    var = jnp.mean(d * d, axis=-1, keepdims=True)
    yn = d * lax.rsqrt(var + CONV_A_LN_EPS) * lg_ref[...] + lb_ref[...]
    o = _bdot(jax.nn.silu(yn), pj_ref[...])
    o_ref[0] = (pg_ref[0].astype(F32) * o).astype(o_ref.dtype)


def _conv_a(p, conv_w, conv_b, ln_g, ln_b, proj_bf):
    bsz, seq, _ = p.shape
    ts = min(seq, 1024)
    row = lambda a: a.reshape(1, -1)
    full = lambda shape: pl.BlockSpec(shape, lambda b, i: (0,) * len(shape))
    return pl.pallas_call(
        _conv_a_kernel,
        out_shape=SDS((bsz, seq, D_MODEL), MERGED_DTYPE),
        grid=(bsz, seq // ts),
        in_specs=[pl.BlockSpec((1, ts, CONV_A_CH), lambda b, i: (b, i, SEG_A // CONV_A_CH)),
                  pl.BlockSpec((1, ts, D_MODEL), lambda b, i: (b, i, SEG_G // D_MODEL)),
                  full((CONV_A_WIDTH, CONV_A_CH)), full((1, CONV_A_CH)), full((1, CONV_A_CH)),
                  full((1, CONV_A_CH)), full((CONV_A_CH, D_MODEL))],
        out_specs=pl.BlockSpec((1, ts, D_MODEL), lambda b, i: (b, i, 0)),
        scratch_shapes=[pltpu.VMEM((ts + CONV_A_HALO, CONV_A_CH), F32),
                        pltpu.VMEM((SUBLANES - 1, ts + CONV_A_HALO - SUBLANES, CONV_A_CH), F32),
                        pltpu.VMEM((ts, CONV_A_CH), F32)],
        compiler_params=_cparams(("arbitrary", "arbitrary")),
        name="conv_a",
    )(p, p, conv_w, row(conv_b), row(ln_g), row(ln_b), proj_bf)


def _lru_kernel(pc_ref, pg_ref, m_ref, cw_ref, cb_ref, wa_ref, ba_ref, wx_ref, bx_ref, lam_ref,
                pj_ref, o_ref, ext_ref, h_ref, a_ref, b_ref):
    ts = pc_ref.shape[1]
    groups = ts // SUBLANES

    @pl.when(pl.program_id(1) == 0)
    def _():
        ext_ref[pl.ds(0, SUBLANES), :] = jnp.zeros((SUBLANES, LRU_DIM), F32)
        h_ref[...] = jnp.zeros((SUBLANES, LRU_DIM), F32)

    pc = pc_ref[0].astype(F32)
    y_gate = pc[:, :LRU_DIM]
    ext_ref[pl.ds(SUBLANES, ts), :] = pc[:, LRU_DIM:]
    first = SUBLANES - (LRU_CONV - 1)
    xc = jnp.zeros((ts, LRU_DIM), F32) + cb_ref[...]
    for j in range(LRU_CONV):
        xc = xc + ext_ref[pl.ds(first + j, ts), :] * cw_ref[pl.ds(j, 1), :]
    ext_ref[pl.ds(0, SUBLANES), :] = ext_ref[pl.ds(ts, SUBLANES), :]

    def block_diag(w_ref):
        return jnp.concatenate(
            [_bdot(xc[:, h * LRU_BLOCK:(h + 1) * LRU_BLOCK], w_ref[h]) for h in range(LRU_HEADS)],
            axis=1)

    gate_a = jax.nn.sigmoid(block_diag(wa_ref) + ba_ref[...])
    gate_x = jax.nn.sigmoid(block_diag(wx_ref) + bx_ref[...])
    log_a = -LRU_C * gate_a * jax.nn.softplus(-lam_ref[...])
    a = jnp.exp(log_a)
    b = xc * gate_x * jnp.sqrt(1.0 - a * a)

    a3 = a.reshape(groups, SUBLANES, LRU_DIM)
    b3 = b.reshape(groups, SUBLANES, LRU_DIM)
    row = lax.broadcasted_iota(I32, (groups, SUBLANES, LRU_DIM), 1)
    for s in (1, 2, 4):
        keep = row >= s
        b3 = jnp.where(keep, a3 * pltpu.roll(b3, s, axis=1) + b3, b3)
        a3 = jnp.where(keep, a3 * pltpu.roll(a3, s, axis=1), a3)
    a_ref[...] = a3.reshape(ts, LRU_DIM)
    b_ref[...] = b3.reshape(ts, LRU_DIM)
    h = h_ref[...]
    for g in range(groups):
        rows = pl.ds(g * SUBLANES, SUBLANES)
        hg = a_ref[rows, :] * h + b_ref[rows, :]
        b_ref[rows, :] = hg
        h = jnp.broadcast_to(hg[SUBLANES - 1:SUBLANES, :], (SUBLANES, LRU_DIM))
    h_ref[...] = h
    o = _bdot(b_ref[...] * y_gate, pj_ref[...])
    o_ref[0] = (m_ref[0].astype(F32) + pg_ref[0].astype(F32) * o).astype(o_ref.dtype)


def _lru(p, merged, conv_w, conv_b, wa_bf, ba, wx_bf, bx, lam, proj_bf):
    bsz, seq, _ = p.shape
    ts = min(seq, 512)
    row = lambda a: a.reshape(1, -1)
    full = lambda shape: pl.BlockSpec(shape, lambda b, i: (0,) * len(shape))
    return pl.pallas_call(
        _lru_kernel,
        out_shape=SDS((bsz, seq, D_MODEL), MERGED_DTYPE),
        grid=(bsz, seq // ts),
        in_specs=[pl.BlockSpec((1, ts, 2 * LRU_DIM), lambda b, i: (b, i, SEG_C // (2 * LRU_DIM))),
                  pl.BlockSpec((1, ts, D_MODEL), lambda b, i: (b, i, SEG_G // D_MODEL + 2)),
                  pl.BlockSpec((1, ts, D_MODEL), lambda b, i: (b, i, 0)),
                  full((LRU_CONV, LRU_DIM)), full((1, LRU_DIM)),
                  full((LRU_HEADS, LRU_BLOCK, LRU_BLOCK)), full((1, LRU_DIM)),
                  full((LRU_HEADS, LRU_BLOCK, LRU_BLOCK)), full((1, LRU_DIM)),
                  full((1, LRU_DIM)), full((LRU_DIM, D_MODEL))],
        out_specs=pl.BlockSpec((1, ts, D_MODEL), lambda b, i: (b, i, 0)),
        scratch_shapes=[pltpu.VMEM((ts + SUBLANES, LRU_DIM), F32),
                        pltpu.VMEM((SUBLANES, LRU_DIM), F32),
                        pltpu.VMEM((ts, LRU_DIM), F32),
                        pltpu.VMEM((ts, LRU_DIM), F32)],
        compiler_params=_cparams(("arbitrary", "arbitrary")),
        name="rg_lru",
    )(p, p, merged, conv_w, row(conv_b), wa_bf, row(ba), wx_bf, row(bx), row(lam), proj_bf)


def _rwkv_prep_kernel(pb_ref, mu_ref, w0_ref, wup_ref, a0_ref, aup_ref, gup_ref, kk_ref, ka_ref,
                      rk_ref, bd_ref, ltri_ref,
                      rt_ref, kkt_ref, kh_ref, bh_ref, v_ref, pinc_ref, bonus_ref, g_ref, ext_ref):
    ts = pb_ref.shape[1]

    @pl.when(pl.program_id(1) == 0)
    def _():
        ext_ref[pl.ds(0, SUBLANES), :] = jnp.zeros((SUBLANES, ext_ref.shape[1]), F32)

    p = pb_ref[0].astype(F32)
    ext_ref[pl.ds(SUBLANES, ts), :] = p
    prev = ext_ref[pl.ds(SUBLANES - 1, ts), :]
    ext_ref[pl.ds(0, SUBLANES), :] = ext_ref[pl.ds(ts, SUBLANES), :]
    pm = p + (prev - p) * mu_ref[...]
    r = pm[:, RW_R:RW_R + RWKV_DIM]
    k = pm[:, RW_K:RW_K + RWKV_DIM]
    v = pm[:, RW_V:RW_V + RWKV_DIM]
    xwa = pm[:, RW_XWA:RW_XWA + LANES]
    xg = pm[:, RW_XG:RW_XG + LORA_G]
    w = -jax.nn.softplus(-(w0_ref[...] + _bdot(jnp.tanh(xwa), wup_ref[...]))) - 0.5
    lw = -jnp.exp(w)
    a = jax.nn.sigmoid(a0_ref[...] + _bdot(xwa, aup_ref[...]))
    g_ref[0] = _bdot(jax.nn.sigmoid(xg), gup_ref[...]).astype(g_ref.dtype)
    kkr = k * kk_ref[...]
    ss = _head_sums(kkr * kkr, bd_ref[...])
    kk = kkr / jnp.maximum(jnp.sqrt(ss), 1e-12)
    k2 = k * (1.0 + (a - 1.0) * ka_ref[...])
    lw_hi = lw.astype(BF16)
    lw_mid, lw_lo = _split(lw - lw_hi.astype(F32))
    tri = ltri_ref[...]
    lcum = (jnp.dot(tri, lw_hi, preferred_element_type=F32) + jnp.dot(tri, lw_mid, preferred_element_type=F32)
            + jnp.dot(tri, lw_lo, preferred_element_type=F32))
    pinc = jnp.exp(lcum)
    pinv = jnp.exp(-lcum)
    rt_ref[0] = r * pinc
    kkt_ref[0] = kk * jnp.exp(lcum - lw)
    kh_ref[0] = k2 * pinv
    bh_ref[0] = kk * a * pinv
    v_ref[0] = v
    pinc_ref[0] = pinc
    bonus_ref[0] = (_head_sums(r * k2 * rk_ref[...], bd_ref[...]) * v).astype(bonus_ref.dtype)


def _dot3(a, b):
    d = lambda x, y: jnp.dot(x, y, preferred_element_type=F32)
    m = a[0].shape[0]
    both = d(jnp.concatenate([a[0], a[1]], axis=0), b[0])
    return both[:m] + both[m:] + d(a[0], b[1])


def _rwkv_scan_kernel(rt_ref, kkt_ref, kh_ref, bh_ref, v_ref, pinc_ref, y_ref, s_ref):
    c = RWKV_CHUNK
    n = RWKV_HEAD
    nb = rt_ref.shape[0]
    heads = range(nb * RWKV_HEADS)

    @pl.when(pl.program_id(1) == 0)
    def _():
        s_ref[...] = jnp.zeros(s_ref.shape, F32)

    row = lax.broadcasted_iota(I32, (c, c), 0)
    col = lax.broadcasted_iota(I32, (c, c), 1)
    eye = (row == col).astype(F32)
    same16 = (row // 16) == (col // 16)
    same32 = (row // 32) == (col // 32)
    row2 = lax.broadcasted_iota(I32, (c, 2 * c), 0)
    col2 = lax.broadcasted_iota(I32, (c, 2 * c), 1) % c
    nt = lambda a, b: lax.dot_general(a, b, (((1,), (1,)), ((), ())), preferred_element_type=F32)
    tn = lambda a, b: lax.dot_general(a, b, (((0,), (0,)), ((), ())), preferred_element_type=F32)
    dot = lambda a, b: jnp.dot(a, b, preferred_element_type=F32)
    sl = [pl.ds((h % RWKV_HEADS) * n, n) for h in heads]
    sq = [h // RWKV_HEADS for h in heads]
    v = [v_ref[sq[h], :, sl[h]] for h in heads]
    pc = [pinc_ref[sq[h], pl.ds(c - 1, 1), sl[h]] for h in heads]
    s = [s_ref[sq[h], :, sl[h]] for h in heads]
    lhs = [jnp.concatenate([kkt_ref[sq[h], :, sl[h]], rt_ref[sq[h], :, sl[h]]], axis=0) for h in heads]
    rhs = [jnp.concatenate([bh_ref[sq[h], :, sl[h]], kh_ref[sq[h], :, sl[h]]], axis=0) for h in heads]
    big = [nt(lhs[h], rhs[h]) for h in heads]
    from_state = [nt(lhs[h], s[h]) for h in heads]
    top = [jnp.where(row2 > col2, big[h][:c], 0.0) for h in heads]
    bot = [jnp.where(row2 >= col2, big[h][c:], 0.0) for h in heads]
    a_b = [top[h][:, :c] for h in heads]
    akv = [dot(top[h], jnp.concatenate([jnp.zeros((c, n), F32), v[h]], axis=0)) for h in heads]
    d16 = [jnp.where(same16, a_b[h], 0.0) for h in heads]
    sd = [_split(d16[h]) for h in heads]
    s2 = [_split(_dot3(sd[h], sd[h])) for h in heads]
    s4 = [_split(_dot3(s2[h], s2[h])) for h in heads]
    s8 = [_split(_dot3(s4[h], s4[h])) for h in heads]
    t = [eye - d16[h] for h in heads]
    for sp in (s2, s4, s8):
        t = [t[h] + _dot3(_split(t[h]), sp[h]) for h in heads]
    for off in ([jnp.where(same32 & jnp.logical_not(same16), a_b[h], 0.0) for h in heads],
                [jnp.where(same32, 0.0, a_b[h]) for h in heads]):
        tb = [t[h].astype(BF16) for h in heads]
        lt = [dot(off[h].astype(BF16), tb[h]).astype(BF16) for h in heads]
        t = [t[h] - dot(tb[h], lt[h]) for h in heads]
    u = [dot(t[h], from_state[h][:c] + akv[h]) for h in heads]
    vu = [jnp.concatenate([-u[h], v[h]], axis=0) for h in heads]
    y = [from_state[h][c:] + dot(bot[h], vu[h]) for h in heads]
    s_new = [s[h] * pc[h] + tn(vu[h], rhs[h] * pc[h]) for h in heads]
    for q in range(nb):
        mine = slice(q * RWKV_HEADS, (q + 1) * RWKV_HEADS)
        y_ref[q] = jnp.concatenate(y[mine], axis=1)
        s_ref[q] = jnp.concatenate(s_new[mine], axis=1)


def _rwkv_post_kernel(y_ref, bonus_ref, g_ref, pg_ref, m_ref, gg_ref, gb_ref, bdm_ref, pj_ref, o_ref):
    y = y_ref[0]
    mu = _head_sums(y, bdm_ref[...])
    d = y - mu
    var = _head_sums(d * d, bdm_ref[...])
    yn = d * lax.rsqrt(var + RWKV_GN_EPS) * gg_ref[...] + gb_ref[...] + bonus_ref[0].astype(F32)
    o = _bdot(yn * g_ref[0].astype(F32), pj_ref[...])
    o_ref[0] = (m_ref[0].astype(F32) + pg_ref[0].astype(F32) * o).astype(o_ref.dtype)


def _rwkv(p, merged, mu_pad, w0, wup_pad, a0, aup_pad, g_up, k_k, k_a, r_k, gn_g, gn_b, proj_bf):
    bsz, seq, _ = p.shape
    row = lambda a: a.reshape(1, -1)
    full = lambda shape: pl.BlockSpec(shape, lambda b, i: (0,) * len(shape))
    head_id = jnp.arange(RWKV_DIM, dtype=I32) // RWKV_HEAD
    bd = (head_id[:, None] == head_id[None, :]).astype(BF16)

    ts = min(seq, 512)
    t_id = jnp.arange(ts, dtype=I32)
    ltri = ((t_id[:, None] // RWKV_CHUNK == t_id[None, :] // RWKV_CHUNK)
            & (t_id[:, None] >= t_id[None, :])).astype(BF16)
    seq_blk = lambda width: pl.BlockSpec((1, ts, width), lambda b, i: (b, i, 0))
    wide = SDS((bsz, seq, RWKV_DIM), F32)
    rt, kkt, kh, bh, v, pinc, bonus, g = pl.pallas_call(
        _rwkv_prep_kernel,
        out_shape=[wide] * 6 + [SDS((bsz, seq, RWKV_DIM), BF16)] * 2,
        grid=(bsz, seq // ts),
        in_specs=[pl.BlockSpec((1, ts, 2048), lambda b, i: (b, i, SEG_B // 2048)),
                  full((1, 2048)), full((1, RWKV_DIM)), full((LANES, RWKV_DIM)),
                  full((1, RWKV_DIM)), full((LANES, RWKV_DIM)), full((LORA_G, RWKV_DIM)),
                  full((1, RWKV_DIM)), full((1, RWKV_DIM)), full((1, RWKV_DIM)),
                  full((RWKV_DIM, RWKV_DIM)), full((ts, ts))],
        out_specs=[seq_blk(RWKV_DIM)] * 8,
        scratch_shapes=[pltpu.VMEM((ts + SUBLANES, 2048), F32)],
        compiler_params=_cparams(("arbitrary", "arbitrary")),
        name="rwkv_prep",
    )(p, mu_pad, row(w0), wup_pad, row(a0), aup_pad, g_up, row(k_k), row(k_a), row(r_k), bd, ltri)

    c = RWKV_CHUNK
    nb = next(n for n in (8, 4, 2, 1) if bsz % n == 0)
    chunk_blk = pl.BlockSpec((nb, c, RWKV_DIM), lambda b, i: (b, i, 0))
    y = pl.pallas_call(
        _rwkv_scan_kernel,
        out_shape=wide,
        grid=(bsz // nb, seq // c),
        in_specs=[chunk_blk] * 6,
        out_specs=chunk_blk,
        scratch_shapes=[pltpu.VMEM((nb, RWKV_HEAD, RWKV_DIM), F32)],
        compiler_params=_cparams(("arbitrary", "arbitrary")),
        name="rwkv_scan",
    )(rt, kkt, kh, bh, v, pinc)

    tp = min(seq, 1024)
    blk = lambda width: pl.BlockSpec((1, tp, width), lambda b, i: (b, i, 0))
    return pl.pallas_call(
        _rwkv_post_kernel,
        out_shape=SDS((bsz, seq, D_MODEL), MERGED_DTYPE),
        grid=(bsz, seq // tp),
        in_specs=[blk(RWKV_DIM), blk(RWKV_DIM), blk(RWKV_DIM),
                  pl.BlockSpec((1, tp, D_MODEL), lambda b, i: (b, i, SEG_G // D_MODEL + 1)),
                  blk(D_MODEL), full((1, RWKV_DIM)), full((1, RWKV_DIM)),
                  full((RWKV_DIM, RWKV_DIM)), full((RWKV_DIM, D_MODEL))],
        out_specs=blk(D_MODEL),
        compiler_params=_cparams(("arbitrary", "arbitrary")),
        name="rwkv_post",
    )(y, bonus, g, p, merged, row(gn_g), row(gn_b), bd * (1.0 / RWKV_HEAD), proj_bf)


def _out_kernel(m_ref, x_ref, g1_ref, w_ref, n2_ref, sc_ref, sh_ref, xo_ref, h_ref):
    xn = x_ref[0] + g1_ref[0] * _bdot(m_ref[0], w_ref[...])
    xo_ref[0] = xn
    h_ref[0] = _modulated_rmsnorm(xn, n2_ref[...], sc_ref[0], sh_ref[0])


def _out_proj(merged, x, g1, w_bf, norm2, sc2, sh2):
    bsz, seq, _ = x.shape
    ts = min(seq, 1024)
    blk = pl.BlockSpec((1, ts, D_MODEL), lambda b, i: (b, i, 0))
    per_b = pl.BlockSpec((1, 1, D_MODEL), lambda b, i: (b, 0, 0))
    return pl.pallas_call(
        _out_kernel,
        out_shape=[SDS((bsz, seq, D_MODEL), F32)] * 2,
        grid=(bsz, seq // ts),
        in_specs=[blk, blk, per_b, pl.BlockSpec((D_MODEL, D_MODEL), lambda b, i: (0, 0)),
                  pl.BlockSpec((1, D_MODEL), lambda b, i: (0, 0)), per_b, per_b],
        out_specs=[blk, blk],
        compiler_params=_cparams(("arbitrary", "arbitrary")),
        name="out_proj",
    )(merged, x, g1, w_bf, norm2, sc2, sh2)


MOE_TILE = 256
MOE_SLOTS = 2560
RUN_LOOP_UNROLL = 4
DISPATCH_PIECE = 48
DISPATCH_SPARE = DISPATCH_PIECE - SUBLANES
DISPATCH_WAIT_ROWS = 2048
assert MOE_SLOTS >= TOP_K * MOE_TILE + N_EXPERTS * (SUBLANES - 1) + DISPATCH_SPARE
U32 = jnp.uint32


ROUTE_TILES = 8


def _route_kernel(h_ref, rw_ref, bias_ref, upper_ref, ltri_ref, slot_ref, w_ref, n8_ref):
    tile = MOE_TILE
    subs = range(h_ref.shape[0] // tile)
    neg = -jnp.inf
    shape3 = (GROUP_SIZE, N_GROUPS, tile)
    to3 = lambda a: a.reshape(shape3)
    lanes_of = lambda a: jnp.concatenate([a] * (tile // LANES), axis=1)
    fold = lambda a: jnp.sum(jnp.sum(a, axis=0), axis=0, keepdims=True)
    slab = lax.broadcasted_iota(I32, shape3, 0).astype(F32)
    grp = lax.broadcasted_iota(I32, shape3, 1).astype(F32)
    eid = grp * GROUP_SIZE + slab
    gi = lax.broadcasted_iota(I32, (N_GROUPS, tile), 0).astype(F32)
    ones_cols = jnp.ones((tile, LANES), BF16)
    s3 = [to3(jax.nn.sigmoid(_hdot_nt(rw_ref[...], h_ref[pl.ds(q * tile, tile), :]))) for q in subs]
    b3 = [s3[q] + to3(bias_ref[...]) for q in subs]
    m1 = [jnp.max(b3[q], axis=0, keepdims=True) for q in subs]
    first = [jnp.min(jnp.where(b3[q] == m1[q], slab, GROUP_SIZE), axis=0, keepdims=True) for q in subs]
    m2 = [jnp.max(jnp.where(slab == first[q], neg, b3[q]), axis=0, keepdims=True) for q in subs]
    gs = [(m1[q] + m2[q])[0] for q in subs]
    chosen = [jnp.zeros((N_GROUPS, tile), F32) for q in subs]
    for _ in range(TOPK_GROUPS):
        m = [jnp.max(gs[q], axis=0, keepdims=True) for q in subs]
        hit = [gi == jnp.min(jnp.where(gs[q] == m[q], gi, N_GROUPS), axis=0, keepdims=True) for q in subs]
        chosen = [jnp.where(hit[q], 1.0, chosen[q]) for q in subs]
        gs = [jnp.where(hit[q], neg, gs[q]) for q in subs]
    cur = [jnp.where((chosen[q] > 0.0)[None], b3[q], neg) for q in subs]
    base = [jnp.zeros((N_EXPERTS, LANES), F32) for q in subs]
    picks, w_rows, rank_rows = [[] for q in subs], [[] for q in subs], [[] for q in subs]
    for _ in range(TOP_K):
        m = [jnp.max(jnp.max(cur[q], axis=0), axis=0, keepdims=True)[None] for q in subs]
        pick = [jnp.min(jnp.min(jnp.where(cur[q] == m[q], eid, N_EXPERTS), axis=0), axis=0, keepdims=True)
                for q in subs]
        hit = [eid == pick[q][None] for q in subs]
        onehot = [hit[q].astype(BF16).reshape(N_EXPERTS, tile) for q in subs]
        before = [jnp.dot(onehot[q], upper_ref[...], preferred_element_type=F32) for q in subs]
        count = [jnp.dot(onehot[q], ones_cols, preferred_element_type=F32) for q in subs]
        for q in subs:
            w_rows[q].append(fold(jnp.where(hit[q], s3[q], 0.0)))
            picks[q].append(pick[q])
            rank_rows[q].append(fold(jnp.where(hit[q], to3(before[q] + lanes_of(base[q])), 0.0)))
        cur = [jnp.where(hit[q], neg, cur[q]) for q in subs]
        base = [base[q] + count[q] for q in subs]
    n8 = [jnp.floor((base[q] + (SUBLANES - 1.0)) * (1.0 / SUBLANES)) * SUBLANES for q in subs]
    run_start = [to3(lanes_of(_hdot(ltri_ref[...], n8[q]))) for q in subs]
    slots, weights = [], []
    for q in subs:
        slots.append(jnp.concatenate(
            [rank_rows[q][j] + fold(jnp.where(eid == picks[q][j][None], run_start[q], 0.0)) for j in range(TOP_K)],
            axis=0))
        w_all = jnp.concatenate(w_rows[q], axis=0)
        weights.append(w_all / jnp.sum(w_all, axis=0, keepdims=True) * ROUTED_SCALE)
    for q in subs:
        n8_ref[q] = n8[q]
    w_ref[...] = jnp.concatenate(weights, axis=1)
    slot_ref[...] = jnp.concatenate(slots, axis=1).astype(I32)


def _route(h2, router_w, router_bias):
    n_tok = h2.shape[0]
    tile = MOE_TILE
    n_tiles = n_tok // tile
    per_step = ROUTE_TILES if n_tiles % ROUTE_TILES == 0 else 1
    regroup = lambda a: a.reshape(N_GROUPS, GROUP_SIZE, -1).transpose(1, 0, 2).reshape(N_EXPERTS, -1)
    rw = regroup(router_w.T)
    bias = jnp.broadcast_to(regroup(router_bias.reshape(N_EXPERTS, 1)), (N_EXPERTS, tile))
    t_id = jnp.arange(tile, dtype=I32)
    upper = (t_id[:, None] < t_id[None, :]).astype(BF16)
    e_id = jnp.arange(N_EXPERTS, dtype=I32)
    ltri = (e_id[:, None] > e_id[None, :]).astype(F32)
    tok_blk = pl.BlockSpec((TOP_K, per_step * tile), lambda i: (0, i))
    full = lambda shape: pl.BlockSpec(shape, lambda i: (0,) * len(shape))
    slot_t, w_t, n8 = pl.pallas_call(
        _route_kernel,
        out_shape=[SDS((TOP_K, n_tok), I32), SDS((TOP_K, n_tok), F32),
                   SDS((n_tiles, N_EXPERTS, LANES), F32)],
        grid=(n_tiles // per_step,),
        in_specs=[pl.BlockSpec((per_step * tile, D_MODEL), lambda i: (i, 0)),
                  full((N_EXPERTS, D_MODEL)), full((N_EXPERTS, tile)), full((tile, tile)),
                  full((N_EXPERTS, N_EXPERTS))],
        out_specs=[tok_blk, tok_blk, pl.BlockSpec((per_step, N_EXPERTS, LANES), lambda i: (i, 0, 0))],
        compiler_params=_cparams(("arbitrary",)),
        name="moe_route",
    )(h2, rw, bias, upper, ltri)
    return slot_t, w_t, n8[:, :, 0].astype(I32)


def _for_each_run_piece(tile_idx, n8_ref, fn):
    for r in range(N_EXPERTS):
        n = n8_ref[tile_idx * N_EXPERTS + r]
        size = MOE_TILE
        while size >= SUBLANES:
            @pl.when((n & size) != 0)
            def _(size=size):
                fn(r, n & ~(2 * size - 1), size)
            size //= 2


def _wait_rows(total, make_wait):
    size = SUBLANES
    while size <= MOE_SLOTS:
        @pl.when((total & size) != 0)
        def _(size=size):
            make_wait(size).wait()
        size *= 2


def _pack_pairs(hi_bits, lo_bits):
    return (hi_bits & jnp.uint32(0xFFFF0000)) | (lo_bits >> 16)


def _unpack_pairs(u):
    hi = lax.bitcast_convert_type(u & jnp.uint32(0xFFFF0000), F32)
    lo = lax.bitcast_convert_type(u << 16, F32)
    return jnp.concatenate([hi, lo], axis=1).astype(BF16)


def _dispatch_kernel(n8_ref, off_ref, dst_ref, tot_ref, pad_end_ref, slot_ref, h_ref, xs_ref,
                     g_ref, zero_ref, sem):
    i = pl.program_id(0)
    tile = h_ref.shape[0]
    half = D_MODEL // 2

    @pl.when(i == 0)
    def _():
        zero_ref[...] = jnp.zeros(zero_ref.shape, U32)

        def last_block(e, back):
            start = pl.multiple_of(jnp.maximum(pad_end_ref[e] - back * MOE_ROWS, 0), MOE_ROWS)
            return pltpu.make_async_copy(zero_ref, xs_ref.at[pl.ds(start, MOE_ROWS), :], sem.at[0])

        def has_blocks(e, back):
            prev = jnp.where(e > 0, pad_end_ref[jnp.maximum(e - 1, 0)], 0)
            return pad_end_ref[e] - prev >= back * MOE_ROWS

        def clear(e, carry):
            for back in (1, 2):
                pl.when(has_blocks(e, back))(lambda back=back: last_block(e, back).start())
            return carry
        lax.fori_loop(0, N_EXPERTS, clear, 0)

        def done(e, carry):
            for back in (1, 2):
                pl.when(has_blocks(e, back))(lambda back=back: last_block(e, back).wait())
            return carry
        lax.fori_loop(0, N_EXPERTS, done, 0)

        def tail_block(b):
            start = pl.multiple_of(b * MOE_ROWS, MOE_ROWS)
            return pltpu.make_async_copy(zero_ref, xs_ref.at[pl.ds(start, MOE_ROWS), :], sem.at[0])

        def clear_tail(b, carry):
            tail_block(b).start()
            return carry

        def done_tail(b, carry):
            tail_block(b).wait()
            return carry
        used = pad_end_ref[N_EXPERTS - 1] // MOE_ROWS
        lax.fori_loop(used, xs_ref.shape[0] // MOE_ROWS, clear_tail, 0)
        lax.fori_loop(used, xs_ref.shape[0] // MOE_ROWS, done_tail, 0)

    slot_id = lax.broadcasted_iota(I16, (MOE_SLOTS, tile), 0)
    slots = slot_ref[...].astype(I16)
    sel = jnp.zeros((MOE_SLOTS, tile), BF16)
    for j in range(TOP_K):
        sel = jnp.where(slot_id == slots[j:j + 1, :], jnp.ones((), BF16), sel)
    g = jnp.dot(sel, h_ref[...].astype(BF16), preferred_element_type=F32)
    bits = lax.bitcast_convert_type(g, U32)
    buf = i % 2
    g_ref[buf] = _pack_pairs(bits[:, :half], bits[:, half:])

    def drain(step):
        total = tot_ref[step]
        whole = pltpu.make_async_copy(g_ref.at[step % 2, pl.ds(0, DISPATCH_WAIT_ROWS), :],
                                      xs_ref.at[pl.ds(0, DISPATCH_WAIT_ROWS), :], sem.at[step % 2])

        def wait_whole(k, carry):
            whole.wait()
            return carry
        lax.fori_loop(0, lax.shift_right_logical(total, DISPATCH_WAIT_ROWS.bit_length() - 1), wait_whole, 0)
        _wait_rows(total & (DISPATCH_WAIT_ROWS - 1), lambda rows: pltpu.make_async_copy(
            g_ref.at[step % 2, pl.ds(0, rows), :], xs_ref.at[pl.ds(0, rows), :], sem.at[step % 2]))

    pl.when(i > 0)(lambda: drain(i - 1))

    def copy(r, k):
        src = pl.multiple_of(off_ref[i * N_EXPERTS + r] + k * DISPATCH_PIECE, SUBLANES)
        dst = pl.multiple_of(dst_ref[i * N_EXPERTS + r] + k * DISPATCH_PIECE, SUBLANES)
        pltpu.make_async_copy(g_ref.at[buf, pl.ds(src, DISPATCH_PIECE), :],
                              xs_ref.at[pl.ds(dst, DISPATCH_PIECE), :], sem.at[buf]).start()

    def per_expert(r, carry):
        n = n8_ref[i * N_EXPERTS + r]
        pl.when(n > 0)(lambda: copy(r, 0))

        @pl.when(n > DISPATCH_PIECE)
        def _():
            def more(k, c):
                copy(r, k)
                return c
            lax.fori_loop(1, (n + (DISPATCH_PIECE - 1)) // DISPATCH_PIECE, more, 0)
        return carry
    lax.fori_loop(0, N_EXPERTS, per_expert, 0, unroll=RUN_LOOP_UNROLL)
    pl.when(i == pl.num_programs(0) - 1)(lambda: drain(i))


def _dispatch(h2, slot_t, n8_flat, off_flat, dst_flat, tot, pad_end, n_rows):
    n_tok = h2.shape[0]
    tile = MOE_TILE
    grid_spec = pltpu.PrefetchScalarGridSpec(
        num_scalar_prefetch=5,
        grid=(n_tok // tile,),
        in_specs=[pl.BlockSpec((TOP_K, tile), lambda i, *_: (0, i)),
                  pl.BlockSpec((tile, D_MODEL), lambda i, *_: (i, 0))],
        out_specs=pl.BlockSpec(memory_space=pl.ANY),
        scratch_shapes=[pltpu.VMEM((2, MOE_SLOTS, D_MODEL // 2), U32),
                        pltpu.VMEM((MOE_ROWS, D_MODEL // 2), U32), pltpu.SemaphoreType.DMA((2,))],
    )
    return pl.pallas_call(
        _dispatch_kernel,
        out_shape=SDS((n_rows, D_MODEL // 2), U32),
        grid_spec=grid_spec,
        compiler_params=_cparams(("arbitrary",)),
        name="moe_dispatch",
    )(n8_flat, off_flat, dst_flat, tot, pad_end, slot_t, h2)


def _expert_kernel(first_ref, nblk_ref, exp_of_ref, xs_ref, w1_ref, w3_ref, w2_ref, y_ref,
                   xbuf, ybuf, wb1, wb3, wb2, in_sem, out_sem):
    del exp_of_ref
    r = pl.program_id(0)
    half = D_MODEL // 2
    used = first_ref[N_EXPERTS - 1] + nblk_ref[N_EXPERTS - 1]
    part = MOE_ROWS // EXPERT_DMA_PARTS

    class _Copies:
        def __init__(self, make):
            self.parts = [make(p) for p in range(EXPERT_DMA_PARTS)]

        def start(self):
            for c in self.parts:
                c.start()

        def wait(self):
            for c in self.parts:
                c.wait()

    def hbm_rows(g, p):
        return pl.ds(pl.multiple_of(g * MOE_ROWS + p * part, part), part)

    in_copy = lambda g, slot: _Copies(lambda p: pltpu.make_async_copy(
        xs_ref.at[hbm_rows(g, p), :], xbuf.at[slot, pl.ds(p * part, part), :], in_sem.at[slot]))
    out_copy = lambda g, slot: _Copies(lambda p: pltpu.make_async_copy(
        ybuf.at[slot, pl.ds(p * part, part), :], y_ref.at[hbm_rows(g, p), :], out_sem.at[slot]))

    @pl.when(r == 0)
    def _():
        for g in range(EXPERT_AHEAD):
            pl.when(g < used)(lambda g=g: in_copy(g, g).start())

    wb1[...] = w1_ref[0, 0].astype(BF16)
    wb3[...] = w3_ref[0, 0].astype(BF16)
    wb2[...] = w2_ref[0, 0].astype(BF16)

    def process(blocks):
        dot = lambda a, b: jnp.dot(a, b, preferred_element_type=F32)
        slots = [g % EXPERT_BUFS for g in blocks]
        for g, slot in zip(blocks, slots):
            in_copy(g, slot).wait()
            ahead = g + EXPERT_AHEAD
            pl.when(ahead < used)(lambda ahead=ahead: in_copy(ahead, ahead % EXPERT_BUFS).start())
            pl.when(g >= EXPERT_BUFS)(lambda g=g, slot=slot: out_copy(g - EXPERT_BUFS, slot).wait())
        x = [_unpack_pairs(xbuf[slot]) for slot in slots]
        gate = [dot(xq, wb1[...]) for xq in x]
        up = [dot(xq, wb3[...]) for xq in x]
        hid = [(jax.nn.silu(a) * b).astype(BF16) for a, b in zip(gate, up)]
        for g, slot, hq in zip(blocks, slots, hid):
            bits = lax.bitcast_convert_type(dot(hq, wb2[...]).astype(BF16).astype(F32), U32)
            ybuf[slot] = _pack_pairs(bits[:, :half], bits[:, half:])
            out_copy(g, slot).start()

    def pair(k, carry):
        g = first_ref[r] + 2 * k
        process([g, g + 1])
        return carry
    n_mine = nblk_ref[r]
    lax.fori_loop(0, lax.shift_right_logical(n_mine, 1), pair, 0)
    pl.when((n_mine & 1) == 1)(lambda: process([first_ref[r] + n_mine - 1]))

    @pl.when(r == N_EXPERTS - 1)
    def _():
        for back in range(EXPERT_BUFS, 0, -1):
            pl.when(used >= back)(
                lambda back=back: out_copy(used - back, (used - back) % EXPERT_BUFS).wait())

        ybuf[0] = jnp.zeros(ybuf.shape[1:], U32)
        n_blocks = y_ref.shape[0] // MOE_ROWS

        def clear(g, carry):
            out_copy(g, 0).start()
            return carry

        def done(g, carry):
            out_copy(g, 0).wait()
            return carry
        lax.fori_loop(used, n_blocks, clear, 0)
        lax.fori_loop(used, n_blocks, done, 0)


def _experts(xs, first_block, n_block, exp_of_row, w1, w3, w2, layer):
    n_rows = xs.shape[0]
    half = D_MODEL // 2
    w_in_blk = pl.BlockSpec((1, 1, D_MODEL, EXPERT_FF), lambda r, first, nblk, eo: (layer, eo[r], 0, 0))
    grid_spec = pltpu.PrefetchScalarGridSpec(
        num_scalar_prefetch=3,
        grid=(N_EXPERTS,),
        in_specs=[pl.BlockSpec(memory_space=pl.ANY), w_in_blk, w_in_blk,
                  pl.BlockSpec((1, 1, EXPERT_FF, D_MODEL), lambda r, first, nblk, eo: (layer, eo[r], 0, 0))],
        out_specs=pl.BlockSpec(memory_space=pl.ANY),
        scratch_shapes=[pltpu.VMEM((EXPERT_BUFS, MOE_ROWS, half), U32),
                        pltpu.VMEM((EXPERT_BUFS, MOE_ROWS, half), U32),
                        pltpu.VMEM((D_MODEL, EXPERT_FF), BF16), pltpu.VMEM((D_MODEL, EXPERT_FF), BF16),
                        pltpu.VMEM((EXPERT_FF, D_MODEL), BF16),
                        pltpu.SemaphoreType.DMA((EXPERT_BUFS,)), pltpu.SemaphoreType.DMA((EXPERT_BUFS,))],
    )
    return pl.pallas_call(
        _expert_kernel,
        out_shape=SDS((n_rows, half), U32),
        grid_spec=grid_spec,
        compiler_params=_cparams(("arbitrary",)),
        name="moe_experts",
    )(first_block, n_block, exp_of_row, xs, w1, w3, w2)


def _combine_kernel(n8_ref, off_ref, src_ref, tot_ref, y_ref, slot_ref, w_ref, h_ref, x_ref, g2_ref,
                    s1_ref, s3_ref, s2_ref, fn_ref, o_ref, yt_ref, sem, *, final):
    tile = h_ref.shape[1]
    i = pl.program_id(0) * pl.num_programs(1) + pl.program_id(1)

    @pl.when(i == 0)
    def _():
        yt_ref[...] = jnp.zeros(yt_ref.shape, U32)

    def piece(r, offset, rows):
        src = pl.multiple_of(src_ref[i * N_EXPERTS + r] + offset, SUBLANES)
        dst = pl.multiple_of(off_ref[i * N_EXPERTS + r] + offset, SUBLANES)
        pltpu.make_async_copy(y_ref.at[pl.ds(src, rows), :], yt_ref.at[pl.ds(dst, rows), :], sem).start()

    _for_each_run_piece(i, n8_ref, piece)
    h = h_ref[0]
    shared = _bdot(jax.nn.silu(_bdot(h, s1_ref[...])) * _bdot(h, s3_ref[...]), s2_ref[...])
    slot_id = lax.broadcasted_iota(I16, (tile, MOE_SLOTS), 1)
    slots = slot_ref[0].astype(I16)
    w = w_ref[0].astype(BF16)
    pw = jnp.zeros((tile, MOE_SLOTS), BF16)
    for j in range(TOP_K):
        pw = jnp.where(slot_id == slots[:, j:j + 1], w[:, j:j + 1], pw)
    _wait_rows(tot_ref[i], lambda rows: pltpu.make_async_copy(
        y_ref.at[pl.ds(0, rows), :], yt_ref.at[pl.ds(0, rows), :], sem))
    routed = jnp.dot(pw, _unpack_pairs(yt_ref[...]), preferred_element_type=F32)
    xn = x_ref[0] + g2_ref[0] * (routed + shared)
    if final:
        xn = xn * lax.rsqrt(jnp.mean(xn * xn, axis=-1, keepdims=True) + NORM_EPS) * fn_ref[...]
    o_ref[0] = xn


def _combine(y, slot_nat, w_nat, n8_flat, off_flat, dst_flat, tot, h2, x, g2, s1_bf, s3_bf, s2_bf,
             final_norm, final):
    bsz, seq, _ = x.shape
    tile = MOE_TILE
    per_seq = seq // tile
    blk = pl.BlockSpec((1, tile, D_MODEL), lambda b, i, *_: (b, i, 0))
    tok = pl.BlockSpec((1, tile, TOP_K), lambda b, i, *_: (b, i, 0))
    full = lambda shape: pl.BlockSpec(shape, lambda b, i, *_: (0,) * len(shape))
    grid_spec = pltpu.PrefetchScalarGridSpec(
        num_scalar_prefetch=4,
        grid=(bsz, per_seq),
        in_specs=[pl.BlockSpec(memory_space=pl.ANY), tok, tok, blk, blk,
                  pl.BlockSpec((1, 1, D_MODEL), lambda b, i, *_: (b, 0, 0)),
                  full((D_MODEL, SHARED_FF)), full((D_MODEL, SHARED_FF)), full((SHARED_FF, D_MODEL)),
                  full((1, D_MODEL))],
        out_specs=blk,
        scratch_shapes=[pltpu.VMEM((MOE_SLOTS, D_MODEL // 2), U32), pltpu.SemaphoreType.DMA],
    )
    return pl.pallas_call(
        functools.partial(_combine_kernel, final=final),
        out_shape=SDS((bsz, seq, D_MODEL), F32),
        grid_spec=grid_spec,
        compiler_params=_cparams(("arbitrary", "arbitrary")),
        name="moe_combine",
    )(n8_flat, off_flat, dst_flat, tot, y, slot_nat, w_nat, h2, x, g2, s1_bf, s3_bf, s2_bf, final_norm)


def _moe(x, h2, g2, router_w, router_bias, w1, w3, w2, layer, s1_bf, s3_bf, s2_bf, final_norm, final):
    bsz, seq, _ = x.shape
    assert seq % MOE_TILE == 0, "token tiles must not straddle sequences"
    n_tok = bsz * seq
    slot_t, w_t, n8 = _route(h2.reshape(n_tok, D_MODEL), router_w, router_bias)
    n_tiles = n8.shape[0]
    counts = jnp.sum(n8, axis=0)
    padded = jnp.where(counts > 0, (counts + DISPATCH_SPARE + MOE_ROWS - 1) // MOE_ROWS * MOE_ROWS, 0)
    pad_end = jnp.cumsum(padded).astype(I32)
    pad_start = pad_end - padded
    run_row = (pad_start[None, :] + jnp.cumsum(n8, axis=0) - n8).astype(I32)
    run_slot = (jnp.cumsum(n8, axis=1) - n8).astype(I32)
    tot = jnp.sum(n8, axis=1).astype(I32)
    copied = jnp.sum((n8 + DISPATCH_PIECE - 1) // DISPATCH_PIECE * DISPATCH_PIECE, axis=1).astype(I32)
    max_rows = (n_tok * TOP_K + n_tiles * N_EXPERTS * (SUBLANES - 1)
                + N_EXPERTS * (DISPATCH_SPARE + MOE_ROWS - 1))
    n_blocks = (max_rows + MOE_ROWS - 1) // MOE_ROWS
    row_id = jnp.arange(N_EXPERTS, dtype=I32)
    exp_of_row = (row_id % N_GROUPS) * GROUP_SIZE + row_id // N_GROUPS
    flat = lambda a: a.reshape(-1).astype(I32)
    xs = _dispatch(h2.reshape(n_tok, D_MODEL), slot_t, flat(n8), flat(run_slot), flat(run_row), copied,
                   pad_end, n_blocks * MOE_ROWS)
    y = _experts(xs, (pad_start // MOE_ROWS).astype(I32), (padded // MOE_ROWS).astype(I32), exp_of_row,
                 w1, w3, w2, layer)
    nat = lambda a: a.T.reshape(bsz, seq, TOP_K)
    return _combine(y, nat(slot_t), nat(w_t), flat(n8), flat(run_slot), flat(run_row), tot, h2, x, g2,
                    s1_bf, s3_bf, s2_bf, final_norm, final)


def _pad_cols(a, width):
    return jnp.pad(a, ((0, 0), (0, width - a.shape[1])))


def _layout_w_in(w_in):
    b0 = COLS_A
    c0 = COLS_A + COLS_B
    g0 = c0 + COLS_C
    seg_b = _pad_cols(w_in[:, b0:c0], SEG_G - SEG_B)
    return jnp.concatenate([w_in[:, c0:g0], seg_b, w_in[:, g0:], w_in[:, :b0]], axis=1).astype(BF16)


def _layout_mu(mu):
    return _pad_cols(mu.reshape(1, -1), SEG_G - SEG_B)


def _pad_rows(a, height):
    return jnp.pad(a, ((0, height - a.shape[0]), (0, 0)))


def kernel(x, c, ada_w, ada_b, norm1, norm2, w_in, conv_a_w, conv_a_b, ln_a_g, ln_a_b, proj_a, mu_b, w0, w_up, a0, a_up, g_up, k_k, k_a, r_k, gn_b_g, gn_b_b, proj_b, conv_c_w, conv_c_b, lru_wa, lru_ba, lru_wx, lru_bx, lru_lambda, proj_c, w_out, router_w, router_bias, exp_w1, exp_w3, exp_w2, sh_w1, sh_w3, sh_w2, final_norm):
    depth = ada_w.shape[0]
    bsz = x.shape[0]
    mod = _ada_mod(c, ada_w, ada_b)
    for l in range(depth):
        sh1, sc1, g1, sh2, sc2, g2 = [mod[l, :, i * D_MODEL:(i + 1) * D_MODEL].reshape(bsz, 1, D_MODEL)
                                      for i in range(N_MOD)]
        p = _in_proj(x, sc1, sh1, norm1[l].reshape(1, -1), _layout_w_in(w_in[l]))
        merged = _conv_a(p, conv_a_w[l], conv_a_b[l], ln_a_g[l], ln_a_b[l], proj_a[l].astype(BF16))
        merged = _rwkv(p, merged, _layout_mu(mu_b[l]), w0[l], _pad_rows(w_up[l], LANES).astype(BF16),
                       a0[l], jnp.pad(a_up[l], ((LORA_W, 0), (0, 0))).astype(BF16), g_up[l].astype(BF16),
                       k_k[l], k_a[l], r_k[l], gn_b_g[l], gn_b_b[l], proj_b[l].astype(BF16))
        merged = _lru(p, merged, conv_c_w[l], conv_c_b[l], lru_wa[l].astype(BF16), lru_ba[l],
                      lru_wx[l].astype(BF16), lru_bx[l], lru_lambda[l], proj_c[l].astype(BF16))
        x, h2 = _out_proj(merged, x, g1, w_out[l].astype(BF16), norm2[l].reshape(1, -1), sc2, sh2)
        x = _moe(x, h2, g2, router_w[l], router_bias[l], exp_w1, exp_w3, exp_w2, l,
                 sh_w1[l].astype(BF16), sh_w3[l].astype(BF16), sh_w2[l].astype(BF16),
                 final_norm.reshape(1, -1), final=(l == depth - 1))
    return x
```

```python
import functools

import jax
import jax.numpy as jnp
from jax import lax
from jax.experimental import pallas as pl
from jax.experimental.pallas import tpu as pltpu

F32 = jnp.float32
BF16 = jnp.bfloat16
I32 = jnp.int32
I16 = jnp.int16
SDS = jax.ShapeDtypeStruct
HIGHEST = lax.Precision.HIGHEST

D_MODEL = 1024
N_MOD = 6
NORM_EPS = 1e-6
CONV_A_CH = 512
CONV_A_WIDTH = 31
CONV_A_LN_EPS = 1e-5
RWKV_HEADS = 8
RWKV_HEAD = 64
RWKV_DIM = RWKV_HEADS * RWKV_HEAD
LORA_W = 64
LORA_A = 64
LORA_G = 128
RWKV_GN_EPS = 64e-5
RWKV_CHUNK = 64
LRU_DIM = 1024
LRU_HEADS = 8
LRU_BLOCK = LRU_DIM // LRU_HEADS
LRU_CONV = 4
LRU_C = 8.0
N_EXPERTS = 64
TOP_K = 8
N_GROUPS = 8
GROUP_SIZE = N_EXPERTS // N_GROUPS
TOPK_GROUPS = 4
EXPERT_FF = 256
SHARED_FF = 256
ROUTED_SCALE = 2.5
SEG_C = 0
SEG_B = 2048
SEG_G = 4096
SEG_A = 7168
IN_COLS_PAD = 8192
RW_R, RW_K, RW_V, RW_XWA, RW_XG = 0, 512, 1024, 1536, 1664
COLS_A = 2 * CONV_A_CH
COLS_B = 3 * RWKV_DIM + LORA_W + LORA_A + LORA_G
COLS_C = 2 * LRU_DIM
VMEM_LIMIT = 56 * 1024 * 1024
MERGED_DTYPE = BF16
SUBLANES = 8
LANES = 128
MOE_ROWS = 512
EXPERT_DMA_PARTS = 4
EXPERT_AHEAD = 4
EXPERT_BUFS = EXPERT_AHEAD + 2


def _cparams(sem):
    return pltpu.CompilerParams(dimension_semantics=sem, vmem_limit_bytes=VMEM_LIMIT)


def _bdot(a, b):
    return jnp.dot(a.astype(BF16), b.astype(BF16), preferred_element_type=F32)


def _hdot(a, b):
    return jnp.dot(a, b, preferred_element_type=F32, precision=HIGHEST)


def _split(a):
    hi = a.astype(BF16)
    return hi, (a - hi.astype(F32)).astype(BF16)


def _head_sums(a, ones_bf):
    hi, lo = _split(a)
    return (jnp.dot(hi, ones_bf, preferred_element_type=F32)
            + jnp.dot(lo, ones_bf, preferred_element_type=F32))


def _hdot_nt(a, b):
    return lax.dot_general(a, b, (((1,), (1,)), ((), ())), preferred_element_type=F32,
                           precision=HIGHEST)


def _ada_kernel(c_ref, w_ref, b_ref, o_ref):
    cond = jax.nn.silu(c_ref[...])
    o_ref[0] = _bdot(cond, w_ref[0]) + b_ref[0]


def _ada_mod(c, ada_w, ada_b):
    depth, _, n = ada_w.shape
    bsz = c.shape[0]
    tn = 1536
    return pl.pallas_call(
        _ada_kernel,
        out_shape=SDS((depth, bsz, n), F32),
        grid=(depth, n // tn),
        in_specs=[pl.BlockSpec((bsz, D_MODEL), lambda l, j: (0, 0)),
                  pl.BlockSpec((1, D_MODEL, tn), lambda l, j: (l, 0, j)),
                  pl.BlockSpec((1, 1, tn), lambda l, j: (l, 0, j))],
        out_specs=pl.BlockSpec((1, bsz, tn), lambda l, j: (l, 0, j)),
        compiler_params=_cparams(("arbitrary", "arbitrary")),
        name="ada_mod",
    )(c, ada_w, ada_b.reshape(depth, 1, n))


def _modulated_rmsnorm(x, g, sc, sh):
    y = x * lax.rsqrt(jnp.mean(x * x, axis=-1, keepdims=True) + NORM_EPS)
    return (y * g) * (1.0 + sc) + sh


IN_PROJ_SUB = 512


def _in_kernel(x_ref, sc_ref, sh_ref, g_ref, w_ref, o_ref, h_ref):
    @pl.when(pl.program_id(2) == 0)
    def _():
        h_ref[...] = _modulated_rmsnorm(x_ref[0], g_ref[...], sc_ref[0], sh_ref[0]).astype(BF16)

    j = pl.program_id(2)
    ts, tn = o_ref.shape[1], o_ref.shape[2]
    is_gelu = j == SEG_C // tn
    is_gate = (j >= SEG_G // tn) & (j < SEG_A // tn)
    is_glu = j == SEG_A // tn

    def glu(acc):
        u = acc[:, :tn // 2] * jax.nn.sigmoid(acc[:, tn // 2:])
        return jnp.concatenate([u, jnp.zeros_like(u)], axis=1)

    def emit(act, sub):
        for r0 in range(0, ts, sub):
            rows = pl.ds(r0, sub)
            acc = jnp.dot(h_ref[rows, :], w_ref[...], preferred_element_type=F32)
            o_ref[0, rows, :] = act(acc).astype(o_ref.dtype)

    sub = min(ts, IN_PROJ_SUB)
    pl.when(is_gelu)(lambda: emit(functools.partial(jax.nn.gelu, approximate=True), sub))
    pl.when(is_gate)(lambda: emit(jax.nn.sigmoid, sub))
    pl.when(is_glu)(lambda: emit(glu, sub))
    pl.when(jnp.logical_not(is_gelu | is_gate | is_glu))(lambda: emit(lambda acc: acc, ts))


def _in_proj(x, sc, sh, g, w_pad):
    bsz, seq, _ = x.shape
    ts = min(seq, 2048)
    tn = 1024
    return pl.pallas_call(
        _in_kernel,
        out_shape=SDS((bsz, seq, IN_COLS_PAD), BF16),
        grid=(bsz, seq // ts, IN_COLS_PAD // tn),
        in_specs=[pl.BlockSpec((1, ts, D_MODEL), lambda b, i, j: (b, i, 0)),
                  pl.BlockSpec((1, 1, D_MODEL), lambda b, i, j: (b, 0, 0)),
                  pl.BlockSpec((1, 1, D_MODEL), lambda b, i, j: (b, 0, 0)),
                  pl.BlockSpec((1, D_MODEL), lambda b, i, j: (0, 0)),
                  pl.BlockSpec((D_MODEL, tn), lambda b, i, j: (0, j))],
        out_specs=pl.BlockSpec((1, ts, tn), lambda b, i, j: (b, i, j)),
        scratch_shapes=[pltpu.VMEM((ts, D_MODEL), BF16)],
        compiler_params=_cparams(("arbitrary", "arbitrary", "arbitrary")),
        name="in_proj",
    )(x, sc, sh, g, w_pad)


CONV_A_HALO = 32
CONV_A_SUB = 64


def _conv_a_kernel(pa_ref, pg_ref, cw_ref, cb_ref, lg_ref, lb_ref, pj_ref, o_ref, ext_ref, sh_ref, y_ref):
    ts = pa_ref.shape[1]

    @pl.when(pl.program_id(1) == 0)
    def _():
        ext_ref[pl.ds(0, CONV_A_HALO), :] = jnp.zeros((CONV_A_HALO, CONV_A_CH), F32)

    ext_ref[pl.ds(CONV_A_HALO, ts), :] = pa_ref[0].astype(F32)
    for p in range(1, SUBLANES):
        sh_ref[p - 1] = ext_ref[pl.ds(p, sh_ref.shape[1]), :]
    first = CONV_A_HALO - (CONV_A_WIDTH - 1)
    for r0 in range(0, ts, CONV_A_SUB):
        acc = jnp.zeros((CONV_A_SUB, CONV_A_CH), F32) + cb_ref[...]
        for j in range(CONV_A_WIDTH):
            phase = (first + j) % SUBLANES
            rows = pl.ds(r0 + first + j - phase, CONV_A_SUB)
            tap = ext_ref[rows, :] if phase == 0 else sh_ref[phase - 1, rows, :]
            acc = acc + tap * cw_ref[pl.ds(j, 1), :]
        y_ref[pl.ds(r0, CONV_A_SUB), :] = acc
    ext_ref[pl.ds(0, CONV_A_HALO), :] = ext_ref[pl.ds(ts, CONV_A_HALO), :]
    y = y_ref[...]
    mu = jnp.mean(y, axis=-1, keepdims=True)
    d = y - mu
    var = jnp.mean(d * d, axis=-1, keepdims=True)
    yn = d * lax.rsqrt(var + CONV_A_LN_EPS) * lg_ref[...] + lb_ref[...]
    o = _bdot(jax.nn.silu(yn), pj_ref[...])
    o_ref[0] = (pg_ref[0].astype(F32) * o).astype(o_ref.dtype)


def _conv_a(p, conv_w, conv_b, ln_g, ln_b, proj_bf):
    bsz, seq, _ = p.shape
    ts = min(seq, 1024)
    row = lambda a: a.reshape(1, -1)
    full = lambda shape: pl.BlockSpec(shape, lambda b, i: (0,) * len(shape))
    return pl.pallas_call(
        _conv_a_kernel,
        out_shape=SDS((bsz, seq, D_MODEL), MERGED_DTYPE),
        grid=(bsz, seq // ts),
        in_specs=[pl.BlockSpec((1, ts, CONV_A_CH), lambda b, i: (b, i, SEG_A // CONV_A_CH)),
                  pl.BlockSpec((1, ts, D_MODEL), lambda b, i: (b, i, SEG_G // D_MODEL)),
                  full((CONV_A_WIDTH, CONV_A_CH)), full((1, CONV_A_CH)), full((1, CONV_A_CH)),
                  full((1, CONV_A_CH)), full((CONV_A_CH, D_MODEL))],
        out_specs=pl.BlockSpec((1, ts, D_MODEL), lambda b, i: (b, i, 0)),
        scratch_shapes=[pltpu.VMEM((ts + CONV_A_HALO, CONV_A_CH), F32),
                        pltpu.VMEM((SUBLANES - 1, ts + CONV_A_HALO - SUBLANES, CONV_A_CH), F32),
                        pltpu.VMEM((ts, CONV_A_CH), F32)],
        compiler_params=_cparams(("arbitrary", "arbitrary")),
        name="conv_a",
    )(p, p, conv_w, row(conv_b), row(ln_g), row(ln_b), proj_bf)


def _lru_kernel(pc_ref, pg_ref, m_ref, cw_ref, cb_ref, wa_ref, ba_ref, wx_ref, bx_ref, lam_ref,
                pj_ref, o_ref, ext_ref, h_ref, a_ref, b_ref):
    ts = pc_ref.shape[1]
    groups = ts // SUBLANES

    @pl.when(pl.program_id(1) == 0)
    def _():
        ext_ref[pl.ds(0, SUBLANES), :] = jnp.zeros((SUBLANES, LRU_DIM), F32)
        h_ref[...] = jnp.zeros((SUBLANES, LRU_DIM), F32)

    pc = pc_ref[0].astype(F32)
    y_gate = pc[:, :LRU_DIM]
    ext_ref[pl.ds(SUBLANES, ts), :] = pc[:, LRU_DIM:]
    first = SUBLANES - (LRU_CONV - 1)
    xc = jnp.zeros((ts, LRU_DIM), F32) + cb_ref[...]
    for j in range(LRU_CONV):
        xc = xc + ext_ref[pl.ds(first + j, ts), :] * cw_ref[pl.ds(j, 1), :]
    ext_ref[pl.ds(0, SUBLANES), :] = ext_ref[pl.ds(ts, SUBLANES), :]

    def block_diag(w_ref):
        return jnp.concatenate(
            [_bdot(xc[:, h * LRU_BLOCK:(h + 1) * LRU_BLOCK], w_ref[h]) for h in range(LRU_HEADS)],
            axis=1)

    gate_a = jax.nn.sigmoid(block_diag(wa_ref) + ba_ref[...])
    gate_x = jax.nn.sigmoid(block_diag(wx_ref) + bx_ref[...])
    log_a = -LRU_C * gate_a * jax.nn.softplus(-lam_ref[...])
    a = jnp.exp(log_a)
    b = xc * gate_x * jnp.sqrt(1.0 - a * a)

    a3 = a.reshape(groups, SUBLANES, LRU_DIM)
    b3 = b.reshape(groups, SUBLANES, LRU_DIM)
    row = lax.broadcasted_iota(I32, (groups, SUBLANES, LRU_DIM), 1)
    for s in (1, 2, 4):
        keep = row >= s
        b3 = jnp.where(keep, a3 * pltpu.roll(b3, s, axis=1) + b3, b3)
        a3 = jnp.where(keep, a3 * pltpu.roll(a3, s, axis=1), a3)
    a_ref[...] = a3.reshape(ts, LRU_DIM)
    b_ref[...] = b3.reshape(ts, LRU_DIM)
    h = h_ref[...]
    for g in range(groups):
        rows = pl.ds(g * SUBLANES, SUBLANES)
        hg = a_ref[rows, :] * h + b_ref[rows, :]
        b_ref[rows, :] = hg
        h = jnp.broadcast_to(hg[SUBLANES - 1:SUBLANES, :], (SUBLANES, LRU_DIM))
    h_ref[...] = h
    o = _bdot(b_ref[...] * y_gate, pj_ref[...])
    o_ref[0] = (m_ref[0].astype(F32) + pg_ref[0].astype(F32) * o).astype(o_ref.dtype)


def _lru(p, merged, conv_w, conv_b, wa_bf, ba, wx_bf, bx, lam, proj_bf):
    bsz, seq, _ = p.shape
    ts = min(seq, 512)
    row = lambda a: a.reshape(1, -1)
    full = lambda shape: pl.BlockSpec(shape, lambda b, i: (0,) * len(shape))
    return pl.pallas_call(
        _lru_kernel,
        out_shape=SDS((bsz, seq, D_MODEL), MERGED_DTYPE),
        grid=(bsz, seq // ts),
        in_specs=[pl.BlockSpec((1, ts, 2 * LRU_DIM), lambda b, i: (b, i, SEG_C // (2 * LRU_DIM))),
                  pl.BlockSpec((1, ts, D_MODEL), lambda b, i: (b, i, SEG_G // D_MODEL + 2)),
                  pl.BlockSpec((1, ts, D_MODEL), lambda b, i: (b, i, 0)),
                  full((LRU_CONV, LRU_DIM)), full((1, LRU_DIM)),
                  full((LRU_HEADS, LRU_BLOCK, LRU_BLOCK)), full((1, LRU_DIM)),
                  full((LRU_HEADS, LRU_BLOCK, LRU_BLOCK)), full((1, LRU_DIM)),
                  full((1, LRU_DIM)), full((LRU_DIM, D_MODEL))],
        out_specs=pl.BlockSpec((1, ts, D_MODEL), lambda b, i: (b, i, 0)),
        scratch_shapes=[pltpu.VMEM((ts + SUBLANES, LRU_DIM), F32),
                        pltpu.VMEM((SUBLANES, LRU_DIM), F32),
                        pltpu.VMEM((ts, LRU_DIM), F32),
                        pltpu.VMEM((ts, LRU_DIM), F32)],
        compiler_params=_cparams(("arbitrary", "arbitrary")),
        name="rg_lru",
    )(p, p, merged, conv_w, row(conv_b), wa_bf, row(ba), wx_bf, row(bx), row(lam), proj_bf)


def _rwkv_prep_kernel(pb_ref, mu_ref, w0_ref, wup_ref, a0_ref, aup_ref, gup_ref, kk_ref, ka_ref,
                      rk_ref, bd_ref, ltri_ref,
                      rt_ref, kkt_ref, kh_ref, bh_ref, v_ref, pinc_ref, bonus_ref, g_ref, ext_ref):
    ts = pb_ref.shape[1]

    @pl.when(pl.program_id(1) == 0)
    def _():
        ext_ref[pl.ds(0, SUBLANES), :] = jnp.zeros((SUBLANES, ext_ref.shape[1]), F32)

    p = pb_ref[0].astype(F32)
    ext_ref[pl.ds(SUBLANES, ts), :] = p
    prev = ext_ref[pl.ds(SUBLANES - 1, ts), :]
    ext_ref[pl.ds(0, SUBLANES), :] = ext_ref[pl.ds(ts, SUBLANES), :]
    pm = p + (prev - p) * mu_ref[...]
    r = pm[:, RW_R:RW_R + RWKV_DIM]
    k = pm[:, RW_K:RW_K + RWKV_DIM]
    v = pm[:, RW_V:RW_V + RWKV_DIM]
    xwa = pm[:, RW_XWA:RW_XWA + LANES]
    xg = pm[:, RW_XG:RW_XG + LORA_G]
    w = -jax.nn.softplus(-(w0_ref[...] + _bdot(jnp.tanh(xwa), wup_ref[...]))) - 0.5
    lw = -jnp.exp(w)
    a = jax.nn.sigmoid(a0_ref[...] + _bdot(xwa, aup_ref[...]))
    g_ref[0] = _bdot(jax.nn.sigmoid(xg), gup_ref[...]).astype(g_ref.dtype)
    kkr = k * kk_ref[...]
    ss = _head_sums(kkr * kkr, bd_ref[...])
    kk = kkr / jnp.maximum(jnp.sqrt(ss), 1e-12)
    k2 = k * (1.0 + (a - 1.0) * ka_ref[...])
    lw_hi = lw.astype(BF16)
    lw_mid, lw_lo = _split(lw - lw_hi.astype(F32))
    tri = ltri_ref[...]
    lcum = (jnp.dot(tri, lw_hi, preferred_element_type=F32) + jnp.dot(tri, lw_mid, preferred_element_type=F32)
            + jnp.dot(tri, lw_lo, preferred_element_type=F32))
    pinc = jnp.exp(lcum)
    pinv = jnp.exp(-lcum)
    rt_ref[0] = r * pinc
    kkt_ref[0] = kk * jnp.exp(lcum - lw)
    kh_ref[0] = k2 * pinv
    bh_ref[0] = kk * a * pinv
    v_ref[0] = v
    pinc_ref[0] = pinc
    bonus_ref[0] = (_head_sums(r * k2 * rk_ref[...], bd_ref[...]) * v).astype(bonus_ref.dtype)


def _dot3(a, b):
    d = lambda x, y: jnp.dot(x, y, preferred_element_type=F32)
    m = a[0].shape[0]
    both = d(jnp.concatenate([a[0], a[1]], axis=0), b[0])
    return both[:m] + both[m:] + d(a[0], b[1])


def _rwkv_scan_kernel(rt_ref, kkt_ref, kh_ref, bh_ref, v_ref, pinc_ref, y_ref, s_ref):
    c = RWKV_CHUNK
    n = RWKV_HEAD
    nb = rt_ref.shape[0]
    heads = range(nb * RWKV_HEADS)

    @pl.when(pl.program_id(1) == 0)
    def _():
        s_ref[...] = jnp.zeros(s_ref.shape, F32)

    row = lax.broadcasted_iota(I32, (c, c), 0)
    col = lax.broadcasted_iota(I32, (c, c), 1)
    eye = (row == col).astype(F32)
    same16 = (row // 16) == (col // 16)
    same32 = (row // 32) == (col // 32)
    row2 = lax.broadcasted_iota(I32, (c, 2 * c), 0)
    col2 = lax.broadcasted_iota(I32, (c, 2 * c), 1) % c
    nt = lambda a, b: lax.dot_general(a, b, (((1,), (1,)), ((), ())), preferred_element_type=F32)
    tn = lambda a, b: lax.dot_general(a, b, (((0,), (0,)), ((), ())), preferred_element_type=F32)
    dot = lambda a, b: jnp.dot(a, b, preferred_element_type=F32)
    sl = [pl.ds((h % RWKV_HEADS) * n, n) for h in heads]
    sq = [h // RWKV_HEADS for h in heads]
    v = [v_ref[sq[h], :, sl[h]] for h in heads]
    pc = [pinc_ref[sq[h], pl.ds(c - 1, 1), sl[h]] for h in heads]
    s = [s_ref[sq[h], :, sl[h]] for h in heads]
    lhs = [jnp.concatenate([kkt_ref[sq[h], :, sl[h]], rt_ref[sq[h], :, sl[h]]], axis=0) for h in heads]
    rhs = [jnp.concatenate([bh_ref[sq[h], :, sl[h]], kh_ref[sq[h], :, sl[h]]], axis=0) for h in heads]
    big = [nt(lhs[h], rhs[h]) for h in heads]
    from_state = [nt(lhs[h], s[h]) for h in heads]
    top = [jnp.where(row2 > col2, big[h][:c], 0.0) for h in heads]
    bot = [jnp.where(row2 >= col2, big[h][c:], 0.0) for h in heads]
    a_b = [top[h][:, :c] for h in heads]
    akv = [dot(top[h], jnp.concatenate([jnp.zeros((c, n), F32), v[h]], axis=0)) for h in heads]
    d16 = [jnp.where(same16, a_b[h], 0.0) for h in heads]
    sd = [_split(d16[h]) for h in heads]
    s2 = [_split(_dot3(sd[h], sd[h])) for h in heads]
    s4 = [_split(_dot3(s2[h], s2[h])) for h in heads]
    s8 = [_split(_dot3(s4[h], s4[h])) for h in heads]
    t = [eye - d16[h] for h in heads]
    for sp in (s2, s4, s8):
        t = [t[h] + _dot3(_split(t[h]), sp[h]) for h in heads]
    for off in ([jnp.where(same32 & jnp.logical_not(same16), a_b[h], 0.0) for h in heads],
                [jnp.where(same32, 0.0, a_b[h]) for h in heads]):
        tb = [t[h].astype(BF16) for h in heads]
        lt = [dot(off[h].astype(BF16), tb[h]).astype(BF16) for h in heads]
        t = [t[h] - dot(tb[h], lt[h]) for h in heads]
    u = [dot(t[h], from_state[h][:c] + akv[h]) for h in heads]
    vu = [jnp.concatenate([-u[h], v[h]], axis=0) for h in heads]
    y = [from_state[h][c:] + dot(bot[h], vu[h]) for h in heads]
    s_new = [s[h] * pc[h] + tn(vu[h], rhs[h] * pc[h]) for h in heads]
    for q in range(nb):
        mine = slice(q * RWKV_HEADS, (q + 1) * RWKV_HEADS)
        y_ref[q] = jnp.concatenate(y[mine], axis=1)
        s_ref[q] = jnp.concatenate(s_new[mine], axis=1)


def _rwkv_post_kernel(y_ref, bonus_ref, g_ref, pg_ref, m_ref, gg_ref, gb_ref, bdm_ref, pj_ref, o_ref):
    y = y_ref[0]
    mu = _head_sums(y, bdm_ref[...])
    d = y - mu
    var = _head_sums(d * d, bdm_ref[...])
    yn = d * lax.rsqrt(var + RWKV_GN_EPS) * gg_ref[...] + gb_ref[...] + bonus_ref[0].astype(F32)
    o = _bdot(yn * g_ref[0].astype(F32), pj_ref[...])
    o_ref[0] = (m_ref[0].astype(F32) + pg_ref[0].astype(F32) * o).astype(o_ref.dtype)


def _rwkv(p, merged, mu_pad, w0, wup_pad, a0, aup_pad, g_up, k_k, k_a, r_k, gn_g, gn_b, proj_bf):
    bsz, seq, _ = p.shape
    row = lambda a: a.reshape(1, -1)
    full = lambda shape: pl.BlockSpec(shape, lambda b, i: (0,) * len(shape))
    head_id = jnp.arange(RWKV_DIM, dtype=I32) // RWKV_HEAD
    bd = (head_id[:, None] == head_id[None, :]).astype(BF16)

    ts = min(seq, 512)
    t_id = jnp.arange(ts, dtype=I32)
    ltri = ((t_id[:, None] // RWKV_CHUNK == t_id[None, :] // RWKV_CHUNK)
            & (t_id[:, None] >= t_id[None, :])).astype(BF16)
    seq_blk = lambda width: pl.BlockSpec((1, ts, width), lambda b, i: (b, i, 0))
    wide = SDS((bsz, seq, RWKV_DIM), F32)
    rt, kkt, kh, bh, v, pinc, bonus, g = pl.pallas_call(
        _rwkv_prep_kernel,
        out_shape=[wide] * 6 + [SDS((bsz, seq, RWKV_DIM), BF16)] * 2,
        grid=(bsz, seq // ts),
        in_specs=[pl.BlockSpec((1, ts, 2048), lambda b, i: (b, i, SEG_B // 2048)),
                  full((1, 2048)), full((1, RWKV_DIM)), full((LANES, RWKV_DIM)),
                  full((1, RWKV_DIM)), full((LANES, RWKV_DIM)), full((LORA_G, RWKV_DIM)),
                  full((1, RWKV_DIM)), full((1, RWKV_DIM)), full((1, RWKV_DIM)),
                  full((RWKV_DIM, RWKV_DIM)), full((ts, ts))],
        out_specs=[seq_blk(RWKV_DIM)] * 8,
        scratch_shapes=[pltpu.VMEM((ts + SUBLANES, 2048), F32)],
        compiler_params=_cparams(("arbitrary", "arbitrary")),
        name="rwkv_prep",
    )(p, mu_pad, row(w0), wup_pad, row(a0), aup_pad, g_up, row(k_k), row(k_a), row(r_k), bd, ltri)

    c = RWKV_CHUNK
    nb = next(n for n in (8, 4, 2, 1) if bsz % n == 0)
    chunk_blk = pl.BlockSpec((nb, c, RWKV_DIM), lambda b, i: (b, i, 0))
    y = pl.pallas_call(
        _rwkv_scan_kernel,
        out_shape=wide,
        grid=(bsz // nb, seq // c),
        in_specs=[chunk_blk] * 6,
        out_specs=chunk_blk,
        scratch_shapes=[pltpu.VMEM((nb, RWKV_HEAD, RWKV_DIM), F32)],
        compiler_params=_cparams(("arbitrary", "arbitrary")),
        name="rwkv_scan",
    )(rt, kkt, kh, bh, v, pinc)

    tp = min(seq, 1024)
    blk = lambda width: pl.BlockSpec((1, tp, width), lambda b, i: (b, i, 0))
    return pl.pallas_call(
        _rwkv_post_kernel,
        out_shape=SDS((bsz, seq, D_MODEL), MERGED_DTYPE),
        grid=(bsz, seq // tp),
        in_specs=[blk(RWKV_DIM), blk(RWKV_DIM), blk(RWKV_DIM),
                  pl.BlockSpec((1, tp, D_MODEL), lambda b, i: (b, i, SEG_G // D_MODEL + 1)),
                  blk(D_MODEL), full((1, RWKV_DIM)), full((1, RWKV_DIM)),
                  full((RWKV_DIM, RWKV_DIM)), full((RWKV_DIM, D_MODEL))],
        out_specs=blk(D_MODEL),
        compiler_params=_cparams(("arbitrary", "arbitrary")),
        name="rwkv_post",
    )(y, bonus, g, p, merged, row(gn_g), row(gn_b), bd * (1.0 / RWKV_HEAD), proj_bf)


def _out_kernel(m_ref, x_ref, g1_ref, w_ref, n2_ref, sc_ref, sh_ref, xo_ref, h_ref):
    xn = x_ref[0] + g1_ref[0] * _bdot(m_ref[0], w_ref[...])
    xo_ref[0] = xn
    h_ref[0] = _modulated_rmsnorm(xn, n2_ref[...], sc_ref[0], sh_ref[0])


def _out_proj(merged, x, g1, w_bf, norm2, sc2, sh2):
    bsz, seq, _ = x.shape
    ts = min(seq, 1024)
    blk = pl.BlockSpec((1, ts, D_MODEL), lambda b, i: (b, i, 0))
    per_b = pl.BlockSpec((1, 1, D_MODEL), lambda b, i: (b, 0, 0))
    return pl.pallas_call(
        _out_kernel,
        out_shape=[SDS((bsz, seq, D_MODEL), F32)] * 2,
        grid=(bsz, seq // ts),
        in_specs=[blk, blk, per_b, pl.BlockSpec((D_MODEL, D_MODEL), lambda b, i: (0, 0)),
                  pl.BlockSpec((1, D_MODEL), lambda b, i: (0, 0)), per_b, per_b],
        out_specs=[blk, blk],
        compiler_params=_cparams(("arbitrary", "arbitrary")),
        name="out_proj",
    )(merged, x, g1, w_bf, norm2, sc2, sh2)


MOE_TILE = 256
MOE_SLOTS = 2560
RUN_LOOP_UNROLL = 4
DISPATCH_PIECE = 48
DISPATCH_SPARE = DISPATCH_PIECE - SUBLANES
DISPATCH_WAIT_ROWS = 2048
assert MOE_SLOTS >= TOP_K * MOE_TILE + N_EXPERTS * (SUBLANES - 1) + DISPATCH_SPARE
U32 = jnp.uint32


ROUTE_TILES = 8


def _route_kernel(h_ref, rw_ref, bias_ref, upper_ref, ltri_ref, slot_ref, w_ref, n8_ref):
    tile = MOE_TILE
    subs = range(h_ref.shape[0] // tile)
    neg = -jnp.inf
    shape3 = (GROUP_SIZE, N_GROUPS, tile)
    to3 = lambda a: a.reshape(shape3)
    lanes_of = lambda a: jnp.concatenate([a] * (tile // LANES), axis=1)
    fold = lambda a: jnp.sum(jnp.sum(a, axis=0), axis=0, keepdims=True)
    slab = lax.broadcasted_iota(I32, shape3, 0).astype(F32)
    grp = lax.broadcasted_iota(I32, shape3, 1).astype(F32)
    eid = grp * GROUP_SIZE + slab
    gi = lax.broadcasted_iota(I32, (N_GROUPS, tile), 0).astype(F32)
    ones_cols = jnp.ones((tile, LANES), BF16)
    s3 = [to3(jax.nn.sigmoid(_hdot_nt(rw_ref[...], h_ref[pl.ds(q * tile, tile), :]))) for q in subs]
    b3 = [s3[q] + to3(bias_ref[...]) for q in subs]
    m1 = [jnp.max(b3[q], axis=0, keepdims=True) for q in subs]
    first = [jnp.min(jnp.where(b3[q] == m1[q], slab, GROUP_SIZE), axis=0, keepdims=True) for q in subs]
    m2 = [jnp.max(jnp.where(slab == first[q], neg, b3[q]), axis=0, keepdims=True) for q in subs]
    gs = [(m1[q] + m2[q])[0] for q in subs]
    chosen = [jnp.zeros((N_GROUPS, tile), F32) for q in subs]
    for _ in range(TOPK_GROUPS):
        m = [jnp.max(gs[q], axis=0, keepdims=True) for q in subs]
        hit = [gi == jnp.min(jnp.where(gs[q] == m[q], gi, N_GROUPS), axis=0, keepdims=True) for q in subs]
        chosen = [jnp.where(hit[q], 1.0, chosen[q]) for q in subs]
        gs = [jnp.where(hit[q], neg, gs[q]) for q in subs]
    cur = [jnp.where((chosen[q] > 0.0)[None], b3[q], neg) for q in subs]
    base = [jnp.zeros((N_EXPERTS, LANES), F32) for q in subs]
    picks, w_rows, rank_rows = [[] for q in subs], [[] for q in subs], [[] for q in subs]
    for _ in range(TOP_K):
        m = [jnp.max(jnp.max(cur[q], axis=0), axis=0, keepdims=True)[None] for q in subs]
        pick = [jnp.min(jnp.min(jnp.where(cur[q] == m[q], eid, N_EXPERTS), axis=0), axis=0, keepdims=True)
                for q in subs]
        hit = [eid == pick[q][None] for q in subs]
        onehot = [hit[q].astype(BF16).reshape(N_EXPERTS, tile) for q in subs]
        before = [jnp.dot(onehot[q], upper_ref[...], preferred_element_type=F32) for q in subs]
        count = [jnp.dot(onehot[q], ones_cols, preferred_element_type=F32) for q in subs]
        for q in subs:
            w_rows[q].append(fold(jnp.where(hit[q], s3[q], 0.0)))
            picks[q].append(pick[q])
            rank_rows[q].append(fold(jnp.where(hit[q], to3(before[q] + lanes_of(base[q])), 0.0)))
        cur = [jnp.where(hit[q], neg, cur[q]) for q in subs]
        base = [base[q] + count[q] for q in subs]
    n8 = [jnp.floor((base[q] + (SUBLANES - 1.0)) * (1.0 / SUBLANES)) * SUBLANES for q in subs]
    run_start = [to3(lanes_of(_hdot(ltri_ref[...], n8[q]))) for q in subs]
    slots, weights = [], []
    for q in subs:
        slots.append(jnp.concatenate(
            [rank_rows[q][j] + fold(jnp.where(eid == picks[q][j][None], run_start[q], 0.0)) for j in range(TOP_K)],
            axis=0))
        w_all = jnp.concatenate(w_rows[q], axis=0)
        weights.append(w_all / jnp.sum(w_all, axis=0, keepdims=True) * ROUTED_SCALE)
    for q in subs:
        n8_ref[q] = n8[q]
    w_ref[...] = jnp.concatenate(weights, axis=1)
    slot_ref[...] = jnp.concatenate(slots, axis=1).astype(I32)


def _route(h2, router_w, router_bias):
    n_tok = h2.shape[0]
    tile = MOE_TILE
    n_tiles = n_tok // tile
    per_step = ROUTE_TILES if n_tiles % ROUTE_TILES == 0 else 1
    regroup = lambda a: a.reshape(N_GROUPS, GROUP_SIZE, -1).transpose(1, 0, 2).reshape(N_EXPERTS, -1)
    rw = regroup(router_w.T)
    bias = jnp.broadcast_to(regroup(router_bias.reshape(N_EXPERTS, 1)), (N_EXPERTS, tile))
    t_id = jnp.arange(tile, dtype=I32)
    upper = (t_id[:, None] < t_id[None, :]).astype(BF16)
    e_id = jnp.arange(N_EXPERTS, dtype=I32)
    ltri = (e_id[:, None] > e_id[None, :]).astype(F32)
    tok_blk = pl.BlockSpec((TOP_K, per_step * tile), lambda i: (0, i))
    full = lambda shape: pl.BlockSpec(shape, lambda i: (0,) * len(shape))
    slot_t, w_t, n8 = pl.pallas_call(
        _route_kernel,
        out_shape=[SDS((TOP_K, n_tok), I32), SDS((TOP_K, n_tok), F32),
                   SDS((n_tiles, N_EXPERTS, LANES), F32)],
        grid=(n_tiles // per_step,),
        in_specs=[pl.BlockSpec((per_step * tile, D_MODEL), lambda i: (i, 0)),
                  full((N_EXPERTS, D_MODEL)), full((N_EXPERTS, tile)), full((tile, tile)),
                  full((N_EXPERTS, N_EXPERTS))],
        out_specs=[tok_blk, tok_blk, pl.BlockSpec((per_step, N_EXPERTS, LANES), lambda i: (i, 0, 0))],
        compiler_params=_cparams(("arbitrary",)),
        name="moe_route",
    )(h2, rw, bias, upper, ltri)
    return slot_t, w_t, n8[:, :, 0].astype(I32)


def _for_each_run_piece(tile_idx, n8_ref, fn):
    for r in range(N_EXPERTS):
        n = n8_ref[tile_idx * N_EXPERTS + r]
        size = MOE_TILE
        while size >= SUBLANES:
            @pl.when((n & size) != 0)
            def _(size=size):
                fn(r, n & ~(2 * size - 1), size)
            size //= 2


def _wait_rows(total, make_wait):
    size = SUBLANES
    while size <= MOE_SLOTS:
        @pl.when((total & size) != 0)
        def _(size=size):
            make_wait(size).wait()
        size *= 2


def _pack_pairs(hi_bits, lo_bits):
    return (hi_bits & jnp.uint32(0xFFFF0000)) | (lo_bits >> 16)


def _unpack_pairs(u):
    hi = lax.bitcast_convert_type(u & jnp.uint32(0xFFFF0000), F32)
    lo = lax.bitcast_convert_type(u << 16, F32)
    return jnp.concatenate([hi, lo], axis=1).astype(BF16)


def _dispatch_kernel(n8_ref, off_ref, dst_ref, tot_ref, pad_end_ref, slot_ref, h_ref, xs_ref,
                     g_ref, zero_ref, sem):
    i = pl.program_id(0)
    tile = h_ref.shape[0]
    half = D_MODEL // 2

    @pl.when(i == 0)
    def _():
        zero_ref[...] = jnp.zeros(zero_ref.shape, U32)

        def last_block(e, back):
            start = pl.multiple_of(jnp.maximum(pad_end_ref[e] - back * MOE_ROWS, 0), MOE_ROWS)
            return pltpu.make_async_copy(zero_ref, xs_ref.at[pl.ds(start, MOE_ROWS), :], sem.at[0])

        def has_blocks(e, back):
            prev = jnp.where(e > 0, pad_end_ref[jnp.maximum(e - 1, 0)], 0)
            return pad_end_ref[e] - prev >= back * MOE_ROWS

        def clear(e, carry):
            for back in (1, 2):
                pl.when(has_blocks(e, back))(lambda back=back: last_block(e, back).start())
            return carry
        lax.fori_loop(0, N_EXPERTS, clear, 0)

        def done(e, carry):
            for back in (1, 2):
                pl.when(has_blocks(e, back))(lambda back=back: last_block(e, back).wait())
            return carry
        lax.fori_loop(0, N_EXPERTS, done, 0)

        def tail_block(b):
            start = pl.multiple_of(b * MOE_ROWS, MOE_ROWS)
            return pltpu.make_async_copy(zero_ref, xs_ref.at[pl.ds(start, MOE_ROWS), :], sem.at[0])

        def clear_tail(b, carry):
            tail_block(b).start()
            return carry

        def done_tail(b, carry):
            tail_block(b).wait()
            return carry
        used = pad_end_ref[N_EXPERTS - 1] // MOE_ROWS
        lax.fori_loop(used, xs_ref.shape[0] // MOE_ROWS, clear_tail, 0)
        lax.fori_loop(used, xs_ref.shape[0] // MOE_ROWS, done_tail, 0)

    slot_id = lax.broadcasted_iota(I16, (MOE_SLOTS, tile), 0)
    slots = slot_ref[...].astype(I16)
    sel = jnp.zeros((MOE_SLOTS, tile), BF16)
    for j in range(TOP_K):
        sel = jnp.where(slot_id == slots[j:j + 1, :], jnp.ones((), BF16), sel)
    g = jnp.dot(sel, h_ref[...].astype(BF16), preferred_element_type=F32)
    bits = lax.bitcast_convert_type(g, U32)
    buf = i % 2
    g_ref[buf] = _pack_pairs(bits[:, :half], bits[:, half:])

    def drain(step):
        total = tot_ref[step]
        whole = pltpu.make_async_copy(g_ref.at[step % 2, pl.ds(0, DISPATCH_WAIT_ROWS), :],
                                      xs_ref.at[pl.ds(0, DISPATCH_WAIT_ROWS), :], sem.at[step % 2])

        def wait_whole(k, carry):
            whole.wait()
            return carry
        lax.fori_loop(0, lax.shift_right_logical(total, DISPATCH_WAIT_ROWS.bit_length() - 1), wait_whole, 0)
        _wait_rows(total & (DISPATCH_WAIT_ROWS - 1), lambda rows: pltpu.make_async_copy(
            g_ref.at[step % 2, pl.ds(0, rows), :], xs_ref.at[pl.ds(0, rows), :], sem.at[step % 2]))

    pl.when(i > 0)(lambda: drain(i - 1))

    def copy(r, k):
        src = pl.multiple_of(off_ref[i * N_EXPERTS + r] + k * DISPATCH_PIECE, SUBLANES)
        dst = pl.multiple_of(dst_ref[i * N_EXPERTS + r] + k * DISPATCH_PIECE, SUBLANES)
        pltpu.make_async_copy(g_ref.at[buf, pl.ds(src, DISPATCH_PIECE), :],
                              xs_ref.at[pl.ds(dst, DISPATCH_PIECE), :], sem.at[buf]).start()

    def per_expert(r, carry):
        n = n8_ref[i * N_EXPERTS + r]
        pl.when(n > 0)(lambda: copy(r, 0))

        @pl.when(n > DISPATCH_PIECE)
        def _():
            def more(k, c):
                copy(r, k)
                return c
            lax.fori_loop(1, (n + (DISPATCH_PIECE - 1)) // DISPATCH_PIECE, more, 0)
        return carry
    lax.fori_loop(0, N_EXPERTS, per_expert, 0, unroll=RUN_LOOP_UNROLL)
    pl.when(i == pl.num_programs(0) - 1)(lambda: drain(i))


def _dispatch(h2, slot_t, n8_flat, off_flat, dst_flat, tot, pad_end, n_rows):
    n_tok = h2.shape[0]
    tile = MOE_TILE
    grid_spec = pltpu.PrefetchScalarGridSpec(
        num_scalar_prefetch=5,
        grid=(n_tok // tile,),
        in_specs=[pl.BlockSpec((TOP_K, tile), lambda i, *_: (0, i)),
                  pl.BlockSpec((tile, D_MODEL), lambda i, *_: (i, 0))],
        out_specs=pl.BlockSpec(memory_space=pl.ANY),
        scratch_shapes=[pltpu.VMEM((2, MOE_SLOTS, D_MODEL // 2), U32),
                        pltpu.VMEM((MOE_ROWS, D_MODEL // 2), U32), pltpu.SemaphoreType.DMA((2,))],
    )
    return pl.pallas_call(
        _dispatch_kernel,
        out_shape=SDS((n_rows, D_MODEL // 2), U32),
        grid_spec=grid_spec,
        compiler_params=_cparams(("arbitrary",)),
        name="moe_dispatch",
    )(n8_flat, off_flat, dst_flat, tot, pad_end, slot_t, h2)


def _expert_kernel(first_ref, nblk_ref, exp_of_ref, xs_ref, w1_ref, w3_ref, w2_ref, y_ref,
                   xbuf, ybuf, wb1, wb3, wb2, in_sem, out_sem):
    del exp_of_ref
    r = pl.program_id(0)
    half = D_MODEL // 2
    used = first_ref[N_EXPERTS - 1] + nblk_ref[N_EXPERTS - 1]
    part = MOE_ROWS // EXPERT_DMA_PARTS

    class _Copies:
        def __init__(self, make):
            self.parts = [make(p) for p in range(EXPERT_DMA_PARTS)]

        def start(self):
            for c in self.parts:
                c.start()

        def wait(self):
            for c in self.parts:
                c.wait()

    def hbm_rows(g, p):
        return pl.ds(pl.multiple_of(g * MOE_ROWS + p * part, part), part)

    in_copy = lambda g, slot: _Copies(lambda p: pltpu.make_async_copy(
        xs_ref.at[hbm_rows(g, p), :], xbuf.at[slot, pl.ds(p * part, part), :], in_sem.at[slot]))
    out_copy = lambda g, slot: _Copies(lambda p: pltpu.make_async_copy(
        ybuf.at[slot, pl.ds(p * part, part), :], y_ref.at[hbm_rows(g, p), :], out_sem.at[slot]))

    @pl.when(r == 0)
    def _():
        for g in range(EXPERT_AHEAD):
            pl.when(g < used)(lambda g=g: in_copy(g, g).start())

    wb1[...] = w1_ref[0, 0].astype(BF16)
    wb3[...] = w3_ref[0, 0].astype(BF16)
    wb2[...] = w2_ref[0, 0].astype(BF16)

    def process(blocks):
        dot = lambda a, b: jnp.dot(a, b, preferred_element_type=F32)
        slots = [g % EXPERT_BUFS for g in blocks]
        for g, slot in zip(blocks, slots):
            in_copy(g, slot).wait()
            ahead = g + EXPERT_AHEAD
            pl.when(ahead < used)(lambda ahead=ahead: in_copy(ahead, ahead % EXPERT_BUFS).start())
            pl.when(g >= EXPERT_BUFS)(lambda g=g, slot=slot: out_copy(g - EXPERT_BUFS, slot).wait())
        x = [_unpack_pairs(xbuf[slot]) for slot in slots]
        gate = [dot(xq, wb1[...]) for xq in x]
        up = [dot(xq, wb3[...]) for xq in x]
        hid = [(jax.nn.silu(a) * b).astype(BF16) for a, b in zip(gate, up)]
        for g, slot, hq in zip(blocks, slots, hid):
            bits = lax.bitcast_convert_type(dot(hq, wb2[...]).astype(BF16).astype(F32), U32)
            ybuf[slot] = _pack_pairs(bits[:, :half], bits[:, half:])
            out_copy(g, slot).start()

    def pair(k, carry):
        g = first_ref[r] + 2 * k
        process([g, g + 1])
        return carry
    n_mine = nblk_ref[r]
    lax.fori_loop(0, lax.shift_right_logical(n_mine, 1), pair, 0)
    pl.when((n_mine & 1) == 1)(lambda: process([first_ref[r] + n_mine - 1]))

    @pl.when(r == N_EXPERTS - 1)
    def _():
        for back in range(EXPERT_BUFS, 0, -1):
            pl.when(used >= back)(
                lambda back=back: out_copy(used - back, (used - back) % EXPERT_BUFS).wait())

        ybuf[0] = jnp.zeros(ybuf.shape[1:], U32)
        n_blocks = y_ref.shape[0] // MOE_ROWS

        def clear(g, carry):
            out_copy(g, 0).start()
            return carry

        def done(g, carry):
            out_copy(g, 0).wait()
            return carry
        lax.fori_loop(used, n_blocks, clear, 0)
        lax.fori_loop(used, n_blocks, done, 0)


def _experts(xs, first_block, n_block, exp_of_row, w1, w3, w2, layer):
    n_rows = xs.shape[0]
    half = D_MODEL // 2
    w_in_blk = pl.BlockSpec((1, 1, D_MODEL, EXPERT_FF), lambda r, first, nblk, eo: (layer, eo[r], 0, 0))
    grid_spec = pltpu.PrefetchScalarGridSpec(
        num_scalar_prefetch=3,
        grid=(N_EXPERTS,),
        in_specs=[pl.BlockSpec(memory_space=pl.ANY), w_in_blk, w_in_blk,
                  pl.BlockSpec((1, 1, EXPERT_FF, D_MODEL), lambda r, first, nblk, eo: (layer, eo[r], 0, 0))],
        out_specs=pl.BlockSpec(memory_space=pl.ANY),
        scratch_shapes=[pltpu.VMEM((EXPERT_BUFS, MOE_ROWS, half), U32),
                        pltpu.VMEM((EXPERT_BUFS, MOE_ROWS, half), U32),
                        pltpu.VMEM((D_MODEL, EXPERT_FF), BF16), pltpu.VMEM((D_MODEL, EXPERT_FF), BF16),
                        pltpu.VMEM((EXPERT_FF, D_MODEL), BF16),
                        pltpu.SemaphoreType.DMA((EXPERT_BUFS,)), pltpu.SemaphoreType.DMA((EXPERT_BUFS,))],
    )
    return pl.pallas_call(
        _expert_kernel,
        out_shape=SDS((n_rows, half), U32),
        grid_spec=grid_spec,
        compiler_params=_cparams(("arbitrary",)),
        name="moe_experts",
    )(first_block, n_block, exp_of_row, xs, w1, w3, w2)


def _combine_kernel(n8_ref, off_ref, src_ref, tot_ref, y_ref, slot_ref, w_ref, h_ref, x_ref, g2_ref,
                    s1_ref, s3_ref, s2_ref, fn_ref, o_ref, yt_ref, sem, *, final):
    tile = h_ref.shape[1]
    i = pl.program_id(0) * pl.num_programs(1) + pl.program_id(1)

    @pl.when(i == 0)
    def _():
        yt_ref[...] = jnp.zeros(yt_ref.shape, U32)

    def piece(r, offset, rows):
        src = pl.multiple_of(src_ref[i * N_EXPERTS + r] + offset, SUBLANES)
        dst = pl.multiple_of(off_ref[i * N_EXPERTS + r] + offset, SUBLANES)
        pltpu.make_async_copy(y_ref.at[pl.ds(src, rows), :], yt_ref.at[pl.ds(dst, rows), :], sem).start(
            priority=r % 2)

    _for_each_run_piece(i, n8_ref, piece)
    h = h_ref[0]
    shared = _bdot(jax.nn.silu(_bdot(h, s1_ref[...])) * _bdot(h, s3_ref[...]), s2_ref[...])
    slot_id = lax.broadcasted_iota(I16, (tile, MOE_SLOTS), 1)
    slots = slot_ref[0].astype(I16)
    w = w_ref[0].astype(BF16)
    pw = jnp.zeros((tile, MOE_SLOTS), BF16)
    for j in range(TOP_K):
        pw = jnp.where(slot_id == slots[:, j:j + 1], w[:, j:j + 1], pw)
    _wait_rows(tot_ref[i], lambda rows: pltpu.make_async_copy(
        y_ref.at[pl.ds(0, rows), :], yt_ref.at[pl.ds(0, rows), :], sem))
    routed = jnp.dot(pw, _unpack_pairs(yt_ref[...]), preferred_element_type=F32)
    xn = x_ref[0] + g2_ref[0] * (routed + shared)
    if final:
        xn = xn * lax.rsqrt(jnp.mean(xn * xn, axis=-1, keepdims=True) + NORM_EPS) * fn_ref[...]
    o_ref[0] = xn


def _combine(y, slot_nat, w_nat, n8_flat, off_flat, dst_flat, tot, h2, x, g2, s1_bf, s3_bf, s2_bf,
             final_norm, final):
    bsz, seq, _ = x.shape
    tile = MOE_TILE
    per_seq = seq // tile
    blk = pl.BlockSpec((1, tile, D_MODEL), lambda b, i, *_: (b, i, 0))
    tok = pl.BlockSpec((1, tile, TOP_K), lambda b, i, *_: (b, i, 0))
    full = lambda shape: pl.BlockSpec(shape, lambda b, i, *_: (0,) * len(shape))
    grid_spec = pltpu.PrefetchScalarGridSpec(
        num_scalar_prefetch=4,
        grid=(bsz, per_seq),
        in_specs=[pl.BlockSpec(memory_space=pl.ANY), tok, tok, blk, blk,
                  pl.BlockSpec((1, 1, D_MODEL), lambda b, i, *_: (b, 0, 0)),
                  full((D_MODEL, SHARED_FF)), full((D_MODEL, SHARED_FF)), full((SHARED_FF, D_MODEL)),
                  full((1, D_MODEL))],
        out_specs=blk,
        scratch_shapes=[pltpu.VMEM((MOE_SLOTS, D_MODEL // 2), U32), pltpu.SemaphoreType.DMA],
    )
    return pl.pallas_call(
        functools.partial(_combine_kernel, final=final),
        out_shape=SDS((bsz, seq, D_MODEL), F32),
        grid_spec=grid_spec,
        compiler_params=_cparams(("arbitrary", "arbitrary")),
        name="moe_combine",
    )(n8_flat, off_flat, dst_flat, tot, y, slot_nat, w_nat, h2, x, g2, s1_bf, s3_bf, s2_bf, final_norm)


def _moe(x, h2, g2, router_w, router_bias, w1, w3, w2, layer, s1_bf, s3_bf, s2_bf, final_norm, final):
    bsz, seq, _ = x.shape
    assert seq % MOE_TILE == 0, "token tiles must not straddle sequences"
    n_tok = bsz * seq
    slot_t, w_t, n8 = _route(h2.reshape(n_tok, D_MODEL), router_w, router_bias)
    n_tiles = n8.shape[0]
    counts = jnp.sum(n8, axis=0)
    padded = jnp.where(counts > 0, (counts + DISPATCH_SPARE + MOE_ROWS - 1) // MOE_ROWS * MOE_ROWS, 0)
    pad_end = jnp.cumsum(padded).astype(I32)
    pad_start = pad_end - padded
    run_row = (pad_start[None, :] + jnp.cumsum(n8, axis=0) - n8).astype(I32)
    run_slot = (jnp.cumsum(n8, axis=1) - n8).astype(I32)
    tot = jnp.sum(n8, axis=1).astype(I32)
    copied = jnp.sum((n8 + DISPATCH_PIECE - 1) // DISPATCH_PIECE * DISPATCH_PIECE, axis=1).astype(I32)
    max_rows = (n_tok * TOP_K + n_tiles * N_EXPERTS * (SUBLANES - 1)
                + N_EXPERTS * (DISPATCH_SPARE + MOE_ROWS - 1))
    n_blocks = (max_rows + MOE_ROWS - 1) // MOE_ROWS
    row_id = jnp.arange(N_EXPERTS, dtype=I32)
    exp_of_row = (row_id % N_GROUPS) * GROUP_SIZE + row_id // N_GROUPS
    flat = lambda a: a.reshape(-1).astype(I32)
    xs = _dispatch(h2.reshape(n_tok, D_MODEL), slot_t, flat(n8), flat(run_slot), flat(run_row), copied,
                   pad_end, n_blocks * MOE_ROWS)
    y = _experts(xs, (pad_start // MOE_ROWS).astype(I32), (padded // MOE_ROWS).astype(I32), exp_of_row,
                 w1, w3, w2, layer)
    nat = lambda a: a.T.reshape(bsz, seq, TOP_K)
    return _combine(y, nat(slot_t), nat(w_t), flat(n8), flat(run_slot), flat(run_row), tot, h2, x, g2,
                    s1_bf, s3_bf, s2_bf, final_norm, final)


def _pad_cols(a, width):
    return jnp.pad(a, ((0, 0), (0, width - a.shape[1])))


def _layout_w_in(w_in):
    b0 = COLS_A
    c0 = COLS_A + COLS_B
    g0 = c0 + COLS_C
    seg_b = _pad_cols(w_in[:, b0:c0], SEG_G - SEG_B)
    return jnp.concatenate([w_in[:, c0:g0], seg_b, w_in[:, g0:], w_in[:, :b0]], axis=1).astype(BF16)


def _layout_mu(mu):
    return _pad_cols(mu.reshape(1, -1), SEG_G - SEG_B)


def _pad_rows(a, height):
    return jnp.pad(a, ((0, height - a.shape[0]), (0, 0)))


def kernel(x, c, ada_w, ada_b, norm1, norm2, w_in, conv_a_w, conv_a_b, ln_a_g, ln_a_b, proj_a, mu_b, w0, w_up, a0, a_up, g_up, k_k, k_a, r_k, gn_b_g, gn_b_b, proj_b, conv_c_w, conv_c_b, lru_wa, lru_ba, lru_wx, lru_bx, lru_lambda, proj_c, w_out, router_w, router_bias, exp_w1, exp_w3, exp_w2, sh_w1, sh_w3, sh_w2, final_norm):
    depth = ada_w.shape[0]
    bsz = x.shape[0]
    mod = _ada_mod(c, ada_w, ada_b)
    for l in range(depth):
        sh1, sc1, g1, sh2, sc2, g2 = [mod[l, :, i * D_MODEL:(i + 1) * D_MODEL].reshape(bsz, 1, D_MODEL)
                                      for i in range(N_MOD)]
        p = _in_proj(x, sc1, sh1, norm1[l].reshape(1, -1), _layout_w_in(w_in[l]))
        merged = _conv_a(p, conv_a_w[l], conv_a_b[l], ln_a_g[l], ln_a_b[l], proj_a[l].astype(BF16))
        merged = _rwkv(p, merged, _layout_mu(mu_b[l]), w0[l], _pad_rows(w_up[l], LANES).astype(BF16),
                       a0[l], jnp.pad(a_up[l], ((LORA_W, 0), (0, 0))).astype(BF16), g_up[l].astype(BF16),
                       k_k[l], k_a[l], r_k[l], gn_b_g[l], gn_b_b[l], proj_b[l].astype(BF16))
        merged = _lru(p, merged, conv_c_w[l], conv_c_b[l], lru_wa[l].astype(BF16), lru_ba[l],
                      lru_wx[l].astype(BF16), lru_bx[l], lru_lambda[l], proj_c[l].astype(BF16))
        x, h2 = _out_proj(merged, x, g1, w_out[l].astype(BF16), norm2[l].reshape(1, -1), sc2, sh2)
        x = _moe(x, h2, g2, router_w[l], router_bias[l], exp_w1, exp_w3, exp_w2, l,
                 sh_w1[l].astype(BF16), sh_w3[l].astype(BF16), sh_w2[l].astype(BF16),
                 final_norm.reshape(1, -1), final=(l == depth - 1))
    return x
```
